```python
import jax
import jax.numpy as jnp
from jax import lax
import numpy as np

D_MODEL = 1024
BATCH = 8
SEQ = 16384
DEPTH = 1

CTX_LEN = 256
GRID_W = 64
D_CONV = D_MODEL // 2
CONV_WIDTH = 31
CONV_PAD = CONV_WIDTH // 2
N_GLA_HEADS = 4
D_GLA_V = D_MODEL // 2
GLA_HEAD_V = D_GLA_V // N_GLA_HEADS
GLA_HEAD_K = GLA_HEAD_V // 2
D_GLA_K = N_GLA_HEADS * GLA_HEAD_K
GLA_RANK = 16
GLA_TAU = 16.0
CHUNK = 64
D_MIX = D_CONV + D_GLA_V
D_IN = 2 * D_CONV + 2 * D_GLA_K + 2 * D_GLA_V + 2 * GLA_RANK
SPLITS = (2 * D_CONV,
          2 * D_CONV + D_GLA_K,
          2 * D_CONV + 2 * D_GLA_K,
          2 * D_CONV + 2 * D_GLA_K + D_GLA_V,
          2 * D_CONV + 2 * D_GLA_K + 2 * D_GLA_V,
          2 * D_CONV + 2 * D_GLA_K + 2 * D_GLA_V + GLA_RANK)
D_FF = -(-8 * D_MODEL // 768) * 256
EPS = 1e-6

kernel_name = 'hybrid_conformer_gla_dit_block'


def rms_norm(x, g):
    xf = x.astype(jnp.float32)
    y = xf * lax.rsqrt(jnp.mean(xf * xf, axis=-1, keepdims=True) + EPS)
    return (y * g.astype(jnp.float32)).astype(x.dtype)


def layer_norm(x, g, b):
    xf = x.astype(jnp.float32)
    mu = jnp.mean(xf, axis=-1, keepdims=True)
    var = jnp.mean(jnp.square(xf - mu), axis=-1, keepdims=True)
    y = (xf - mu) * lax.rsqrt(var + EPS)
    return (y * g.astype(jnp.float32) + b.astype(jnp.float32)).astype(x.dtype)


def modulate(x, shift, scale):
    return x * (1 + scale) + shift


def conformer_conv(u, conv_w, conv_b, ln_g, ln_b, n_seg, seg_len):
    a, gate = jnp.split(u, 2, axis=-1)
    v = a * jax.nn.sigmoid(gate)
    bsz, L, C = v.shape
    vs = v.reshape(bsz * n_seg, seg_len, C)
    y = lax.conv_general_dilated(vs, conv_w[:, None, :].astype(vs.dtype), (1,),
                                 [(CONV_PAD, CONV_PAD)],
                                 dimension_numbers=('NWC', 'WIO', 'NWC'),
                                 feature_group_count=C)
    y = y.reshape(bsz, L, C) + conv_b
    return jax.nn.silu(layer_norm(y, ln_g, ln_b))


def gla_chunked(q, k, v, log_a, s0, with_output):
    f32 = jnp.float32
    bsz, nh, L, dk = k.shape
    dv = v.shape[-1]
    n = L // CHUNK
    kc = k.astype(f32).reshape(bsz, nh, n, CHUNK, dk)
    vc = v.astype(f32).reshape(bsz, nh, n, CHUNK, dv)
    b = jnp.cumsum(log_a.astype(f32).reshape(bsz, nh, n, CHUNK, dk), axis=3)
    b_last = b[:, :, :, -1:, :]
    chunk_kv = jnp.einsum('bhncd,bhncv->bhndv', kc * jnp.exp(b_last - b), vc)
    decay = jnp.exp(b_last[:, :, :, 0, :])

    def step(s, inp):
        dec, kv = inp
        return dec[..., None] * s + kv, s

    s_final, s_enter = lax.scan(step, s0.astype(f32),
                                (jnp.moveaxis(decay, 2, 0), jnp.moveaxis(chunk_kv, 2, 0)))
    if not with_output:
        return None, s_final
    s_enter = jnp.moveaxis(s_enter, 0, 2)
    qc = q.astype(f32).reshape(bsz, nh, n, CHUNK, dk) * (dk ** -0.5)
    q_t = qc * jnp.exp(b)
    k_t = kc * jnp.exp(-b)
    attn = jnp.einsum('bhncd,bhnsd->bhncs', q_t, k_t)
    mask = jnp.tril(jnp.ones((CHUNK, CHUNK), dtype=bool))
    attn = jnp.where(mask, attn, 0.0)
    o = (jnp.einsum('bhncs,bhnsv->bhncv', attn, vc)
         + jnp.einsum('bhncd,bhndv->bhncv', q_t, s_enter))
    return o.reshape(bsz, nh, L, dv).astype(v.dtype), s_final


def gla_bidir(q, k, v, la_f, la_b, s0_f, s0_b, with_output):
    flip = lambda t: jnp.flip(t, axis=2)
    o_f, s_f = gla_chunked(q, k, v, la_f, s0_f, with_output)
    o_b, s_b = gla_chunked(flip(q), flip(k), flip(v), flip(la_b), s0_b, with_output)
    o = o_f + flip(o_b) if with_output else None
    return o, s_f, s_b


def project(h, w_in, w_a2_f, b_a_f, w_a2_b, b_a_b):
    p = h @ w_in
    u_conv, q, k, v, g, r_f, r_b = jnp.split(p, SPLITS, axis=-1)
    bsz, L, _ = h.shape
    heads = lambda t, d: t.reshape(bsz, L, N_GLA_HEADS, d).transpose(0, 2, 1, 3)
    la_f = jax.nn.log_sigmoid((r_f @ w_a2_f + b_a_f).astype(jnp.float32)) / GLA_TAU
    la_b = jax.nn.log_sigmoid((r_b @ w_a2_b + b_a_b).astype(jnp.float32)) / GLA_TAU
    return (u_conv, heads(q, GLA_HEAD_K), heads(k, GLA_HEAD_K), heads(v, GLA_HEAD_V), g,
            heads(la_f, GLA_HEAD_K), heads(la_b, GLA_HEAD_K))


def merge(u_conv, o, g, n_seg, seg_len, conv_w, conv_b, conv_ln_g, conv_ln_b, gla_norm_g, w_out):
    conv_o = conformer_conv(u_conv, conv_w, conv_b, conv_ln_g, conv_ln_b, n_seg, seg_len)
    bsz, nh, L, dv = o.shape
    o = rms_norm(o, gla_norm_g).transpose(0, 2, 1, 3).reshape(bsz, L, nh * dv).astype(g.dtype)
    o = o * jax.nn.silu(g)
    return jnp.concatenate([conv_o.astype(g.dtype), o], axis=-1) @ w_out


def swiglu(h, w_gate, w_up, w_down):
    return (jax.nn.silu(h @ w_gate) * (h @ w_up)) @ w_down


def _fwd_setup_inputs(seed: int = 0) -> dict:
    key = jax.random.key(seed)
    ks = jax.random.split(key, 23)
    nrm = lambda k, s, sc: jax.random.normal(k, s, jnp.float32) * sc
    L = DEPTH
    return {
        'x': nrm(ks[0], (BATCH, SEQ, D_MODEL), 1.0),
        'c': nrm(ks[1], (BATCH, D_MODEL), 1.0),
        'ctx': nrm(ks[2], (BATCH, CTX_LEN, D_MODEL), 1.0),
        'c_ctx': nrm(ks[3], (D_MODEL,), 1.0),
        'w_mod': nrm(ks[4], (L, D_MODEL, 6 * D_MODEL), 0.5 * D_MODEL ** -0.5),
        'b_mod': nrm(ks[5], (L, 6 * D_MODEL), 0.02),
        'norm1_g': 1.0 + nrm(ks[6], (L, D_MODEL), 0.02),
        'norm2_g': 1.0 + nrm(ks[7], (L, D_MODEL), 0.02),
        'w_in': nrm(ks[8], (L, D_MODEL, D_IN), D_MODEL ** -0.5),
        'conv_w': nrm(ks[9], (L, CONV_WIDTH, D_CONV), CONV_WIDTH ** -0.5),
        'conv_b': nrm(ks[10], (L, D_CONV), 0.02),
        'conv_ln_g': 1.0 + nrm(ks[11], (L, D_CONV), 0.02),
        'conv_ln_b': nrm(ks[12], (L, D_CONV), 0.02),
        'w_a2_f': nrm(ks[13], (L, GLA_RANK, D_GLA_K), GLA_RANK ** -0.5),
        'b_a_f': nrm(ks[14], (L, D_GLA_K), 0.1),
        'w_a2_b': nrm(ks[15], (L, GLA_RANK, D_GLA_K), GLA_RANK ** -0.5),
        'b_a_b': nrm(ks[16], (L, D_GLA_K), 0.1),
        'gla_norm_g': 1.0 + nrm(ks[17], (L, GLA_HEAD_V), 0.02),
        'w_out': nrm(ks[18], (L, D_MIX, D_MODEL), D_MIX ** -0.5),
        'w_gate': nrm(ks[19], (L, D_MODEL, D_FF), D_MODEL ** -0.5),
        'w_up': nrm(ks[20], (L, D_MODEL, D_FF), D_MODEL ** -0.5),
        'w_down': nrm(ks[21], (L, D_FF, D_MODEL), D_FF ** -0.5),
        'final_g': 1.0 + nrm(ks[22], (D_MODEL,), 0.02),
    }


def _fwd_reference(x, c, ctx, c_ctx, w_mod, b_mod, norm1_g, norm2_g, w_in, conv_w, conv_b,
              conv_ln_g, conv_ln_b, w_a2_f, b_a_f, w_a2_b, b_a_b, gla_norm_g, w_out,
              w_gate, w_up, w_down, final_g):
    bsz, n_lat, _ = x.shape
    rows = n_lat // GRID_W
    n_ctx = ctx.shape[1]
    for l in range(DEPTH):
        last = l == DEPTH - 1
        mod = jax.nn.silu(c) @ w_mod[l] + b_mod[l]
        sh1, sc1, g1, sh2, sc2, g2 = jnp.split(mod[:, None, :], 6, axis=-1)
        mod_c = jax.nn.silu(c_ctx) @ w_mod[l] + b_mod[l]
        csh1, csc1, cg1, csh2, csc2, cg2 = jnp.split(mod_c, 6, axis=-1)

        hc = modulate(rms_norm(ctx, norm1_g[l]), csh1, csc1)
        uc, qc, kc, vc, gc, laf_c, lab_c = project(hc, w_in[l], w_a2_f[l], b_a_f[l], w_a2_b[l], b_a_b[l])
        s_zero = jnp.zeros((bsz, N_GLA_HEADS, GLA_HEAD_K, GLA_HEAD_V), jnp.float32)
        oc, sf_c, sb_c = gla_bidir(qc, kc, vc, laf_c, lab_c, s_zero, s_zero, not last)

        h = modulate(rms_norm(x, norm1_g[l]), sh1, sc1)
        u, q, k, v, g, laf, lab = project(h, w_in[l], w_a2_f[l], b_a_f[l], w_a2_b[l], b_a_b[l])
        o, _, _ = gla_bidir(q, k, v, laf, lab, sf_c, sb_c, True)
        x = x + g1 * merge(u, o, g, rows, GRID_W, conv_w[l], conv_b[l], conv_ln_g[l],
                           conv_ln_b[l], gla_norm_g[l], w_out[l])
        h2 = modulate(rms_norm(x, norm2_g[l]), sh2, sc2)
        x = x + g2 * swiglu(h2, w_gate[l], w_up[l], w_down[l])

        if not last:
            ctx = ctx + cg1 * merge(uc, oc, gc, 1, n_ctx, conv_w[l], conv_b[l], conv_ln_g[l],
                                    conv_ln_b[l], gla_norm_g[l], w_out[l])
            hc2 = modulate(rms_norm(ctx, norm2_g[l]), csh2, csc2)
            ctx = ctx + cg2 * swiglu(hc2, w_gate[l], w_up[l], w_down[l])
    return rms_norm(x, final_g)


import jax as _jax
import jax.numpy as _jnp

TWIN_FORMAT = 'train_step'
FWD_PARAMS = ['x', 'c', 'ctx', 'c_ctx', 'w_mod', 'b_mod', 'norm1_g', 'norm2_g', 'w_in', 'conv_w', 'conv_b', 'conv_ln_g', 'conv_ln_b', 'w_a2_f', 'b_a_f', 'w_a2_b', 'b_a_b', 'gla_norm_g', 'w_out', 'w_gate', 'w_up', 'w_down', 'final_g']
TWIN_WEIGHTS = ['c_ctx', 'w_mod', 'b_mod', 'norm1_g', 'norm2_g', 'w_in', 'conv_w', 'conv_b', 'conv_ln_g', 'conv_ln_b', 'w_a2_f', 'b_a_f', 'w_a2_b', 'b_a_b', 'gla_norm_g', 'w_out', 'w_gate', 'w_up', 'w_down', 'final_g']
TWIN_DIFF_INPUT = 'x'
TWIN_INPUTS = ['x', 'c', 'ctx', 'c_ctx', 'w_mod', 'b_mod', 'norm1_g', 'norm2_g', 'w_in', 'conv_w', 'conv_b', 'conv_ln_g', 'conv_ln_b', 'w_a2_f', 'b_a_f', 'w_a2_b', 'b_a_b', 'gla_norm_g', 'w_out', 'w_gate', 'w_up', 'w_down', 'final_g', 'loss_target', 'm_c_ctx', 'm_w_mod', 'm_b_mod', 'm_norm1_g', 'm_norm2_g', 'm_w_in', 'm_conv_w', 'm_conv_b', 'm_conv_ln_g', 'm_conv_ln_b', 'm_w_a2_f', 'm_b_a_f', 'm_w_a2_b', 'm_b_a_b', 'm_gla_norm_g', 'm_w_out', 'm_w_gate', 'm_w_up', 'm_w_down', 'm_final_g', 'v_c_ctx', 'v_w_mod', 'v_b_mod', 'v_norm1_g', 'v_norm2_g', 'v_w_in', 'v_conv_w', 'v_conv_b', 'v_conv_ln_g', 'v_conv_ln_b', 'v_w_a2_f', 'v_b_a_f', 'v_w_a2_b', 'v_b_a_b', 'v_gla_norm_g', 'v_w_out', 'v_w_gate', 'v_w_up', 'v_w_down', 'v_final_g']
TWIN_OUTPUTS = ['loss', 'grad_x', 'grad_c_ctx', 'grad_w_mod', 'grad_b_mod', 'grad_norm1_g', 'grad_norm2_g', 'grad_w_in', 'grad_conv_w', 'grad_conv_b', 'grad_conv_ln_g', 'grad_conv_ln_b', 'grad_w_a2_f', 'grad_b_a_f', 'grad_w_a2_b', 'grad_b_a_b', 'grad_gla_norm_g', 'grad_w_out', 'grad_w_gate', 'grad_w_up', 'grad_w_down', 'grad_final_g', 'delta_c_ctx', 'delta_w_mod', 'delta_b_mod', 'delta_norm1_g', 'delta_norm2_g', 'delta_w_in', 'delta_conv_w', 'delta_conv_b', 'delta_conv_ln_g', 'delta_conv_ln_b', 'delta_w_a2_f', 'delta_b_a_f', 'delta_w_a2_b', 'delta_b_a_b', 'delta_gla_norm_g', 'delta_w_out', 'delta_w_gate', 'delta_w_up', 'delta_w_down', 'delta_final_g', 'new_m_c_ctx', 'new_m_w_mod', 'new_m_b_mod', 'new_m_norm1_g', 'new_m_norm2_g', 'new_m_w_in', 'new_m_conv_w', 'new_m_conv_b', 'new_m_conv_ln_g', 'new_m_conv_ln_b', 'new_m_w_a2_f', 'new_m_b_a_f', 'new_m_w_a2_b', 'new_m_b_a_b', 'new_m_gla_norm_g', 'new_m_w_out', 'new_m_w_gate', 'new_m_w_up', 'new_m_w_down', 'new_m_final_g', 'new_v_c_ctx', 'new_v_w_mod', 'new_v_b_mod', 'new_v_norm1_g', 'new_v_norm2_g', 'new_v_w_in', 'new_v_conv_w', 'new_v_conv_b', 'new_v_conv_ln_g', 'new_v_conv_ln_b', 'new_v_w_a2_f', 'new_v_b_a_f', 'new_v_w_a2_b', 'new_v_b_a_b', 'new_v_gla_norm_g', 'new_v_w_out', 'new_v_w_gate', 'new_v_w_up', 'new_v_w_down', 'new_v_final_g']
TWIN_LEAF_KINDS = {'loss': 'loss', 'grad_x': 'grad_x', 'grad_c_ctx': 'grad_w', 'grad_w_mod': 'grad_w', 'grad_b_mod': 'grad_w', 'grad_norm1_g': 'grad_w', 'grad_norm2_g': 'grad_w', 'grad_w_in': 'grad_w', 'grad_conv_w': 'grad_w', 'grad_conv_b': 'grad_w', 'grad_conv_ln_g': 'grad_w', 'grad_conv_ln_b': 'grad_w', 'grad_w_a2_f': 'grad_w', 'grad_b_a_f': 'grad_w', 'grad_w_a2_b': 'grad_w', 'grad_b_a_b': 'grad_w', 'grad_gla_norm_g': 'grad_w', 'grad_w_out': 'grad_w', 'grad_w_gate': 'grad_w', 'grad_w_up': 'grad_w', 'grad_w_down': 'grad_w', 'grad_final_g': 'grad_w', 'delta_c_ctx': 'delta_w', 'delta_w_mod': 'delta_w', 'delta_b_mod': 'delta_w', 'delta_norm1_g': 'delta_w', 'delta_norm2_g': 'delta_w', 'delta_w_in': 'delta_w', 'delta_conv_w': 'delta_w', 'delta_conv_b': 'delta_w', 'delta_conv_ln_g': 'delta_w', 'delta_conv_ln_b': 'delta_w', 'delta_w_a2_f': 'delta_w', 'delta_b_a_f': 'delta_w', 'delta_w_a2_b': 'delta_w', 'delta_b_a_b': 'delta_w', 'delta_gla_norm_g': 'delta_w', 'delta_w_out': 'delta_w', 'delta_w_gate': 'delta_w', 'delta_w_up': 'delta_w', 'delta_w_down': 'delta_w', 'delta_final_g': 'delta_w', 'new_m_c_ctx': 'new_m', 'new_m_w_mod': 'new_m', 'new_m_b_mod': 'new_m', 'new_m_norm1_g': 'new_m', 'new_m_norm2_g': 'new_m', 'new_m_w_in': 'new_m', 'new_m_conv_w': 'new_m', 'new_m_conv_b': 'new_m', 'new_m_conv_ln_g': 'new_m', 'new_m_conv_ln_b': 'new_m', 'new_m_w_a2_f': 'new_m', 'new_m_b_a_f': 'new_m', 'new_m_w_a2_b': 'new_m', 'new_m_b_a_b': 'new_m', 'new_m_gla_norm_g': 'new_m', 'new_m_w_out': 'new_m', 'new_m_w_gate': 'new_m', 'new_m_w_up': 'new_m', 'new_m_w_down': 'new_m', 'new_m_final_g': 'new_m', 'new_v_c_ctx': 'new_v', 'new_v_w_mod': 'new_v', 'new_v_b_mod': 'new_v', 'new_v_norm1_g': 'new_v', 'new_v_norm2_g': 'new_v', 'new_v_w_in': 'new_v', 'new_v_conv_w': 'new_v', 'new_v_conv_b': 'new_v', 'new_v_conv_ln_g': 'new_v', 'new_v_conv_ln_b': 'new_v', 'new_v_w_a2_f': 'new_v', 'new_v_b_a_f': 'new_v', 'new_v_w_a2_b': 'new_v', 'new_v_b_a_b': 'new_v', 'new_v_gla_norm_g': 'new_v', 'new_v_w_out': 'new_v', 'new_v_w_gate': 'new_v', 'new_v_w_up': 'new_v', 'new_v_w_down': 'new_v', 'new_v_final_g': 'new_v'}


def _forward(args):
    return _fwd_reference(*[args[k] for k in FWD_PARAMS])


def _output_shape():
    def fwd():
        inp = _fwd_setup_inputs(0)
        return _fwd_reference(*[inp[k] for k in FWD_PARAMS])
    out = _jax.eval_shape(fwd)
    return out.shape, out.dtype

N_MICROBATCH = 1
ADAM_LR = 0.001
ADAM_B1 = 0.9
ADAM_B2 = 0.999
ADAM_EPS = 1e-08
ADAM_WD = 0.01
ADAM_STEP = 10
PER_EXAMPLE_BATCH_AXIS = {'x': 0, 'c': 0, 'ctx': 0, 'loss_target': 0}
SHARED_INPUTS = []
_WEIGHT_DTYPES = {'c_ctx': _jnp.float32, 'w_mod': _jnp.float32, 'b_mod': _jnp.float32, 'norm1_g': _jnp.float32, 'norm2_g': _jnp.float32, 'w_in': _jnp.float32, 'conv_w': _jnp.float32, 'conv_b': _jnp.float32, 'conv_ln_g': _jnp.float32, 'conv_ln_b': _jnp.float32, 'w_a2_f': _jnp.float32, 'b_a_f': _jnp.float32, 'w_a2_b': _jnp.float32, 'b_a_b': _jnp.float32, 'gla_norm_g': _jnp.float32, 'w_out': _jnp.float32, 'w_gate': _jnp.float32, 'w_up': _jnp.float32, 'w_down': _jnp.float32, 'final_g': _jnp.float32}
MOMENT_SCALE = {'c_ctx': 7.515258e-03, 'w_mod': 1.038814e-01, 'b_mod': 1.695294e-01, 'norm1_g': 1.186084e-01, 'norm2_g': 1.060601e-01, 'w_in': 7.315865e-02, 'conv_w': 6.670318e-02, 'conv_b': 1.251778e-01, 'conv_ln_g': 7.994039e-02, 'conv_ln_b': 6.773355e-02, 'w_a2_f': 1.237338e-02, 'b_a_f': 3.314934e-02, 'w_a2_b': 1.221766e-02, 'b_a_b': 3.265161e-02, 'gla_norm_g': 1.467098e-01, 'w_out': 6.757024e-02, 'w_gate': 4.609354e-02, 'w_up': 4.455452e-02, 'w_down': 7.392387e-02, 'final_g': 1.280415e+02}


def _to_microbatches(a, axis):
    t = _jnp.moveaxis(a, axis, 0)
    t = t.reshape((N_MICROBATCH, t.shape[0] // N_MICROBATCH) + t.shape[1:])
    return _jnp.moveaxis(t, 1, axis + 1)


def setup_inputs(seed: int = 0) -> dict:
    inp = _fwd_setup_inputs(seed)
    key = _jax.random.fold_in(_jax.random.key(seed), 7919)
    shape, _ = _output_shape()
    out = dict(inp)
    out["loss_target"] = _jax.random.normal(_jax.random.fold_in(key, 0), shape, _jnp.float32)
    for i, name in enumerate(TWIN_WEIGHTS):
        w = inp[name].astype(_jnp.float32)
        if MOMENT_SCALE is None:
            s = _jnp.sqrt(_jnp.mean(_jnp.square(w)) + 1e-30)
        else:
            s = MOMENT_SCALE[name]
        km, kv = _jax.random.split(_jax.random.fold_in(key, i + 1))
        out[name] = w
        out["m_" + name] = s * _jax.random.normal(km, w.shape, _jnp.float32)
        out["v_" + name] = (s * s) * _jax.random.uniform(kv, w.shape, _jnp.float32, 0.5, 1.5)
    if N_MICROBATCH > 1:
        for name, axis in PER_EXAMPLE_BATCH_AXIS.items():
            out[name] = _to_microbatches(out[name], axis)
    return {'x': out['x'], 'c': out['c'], 'ctx': out['ctx'], 'c_ctx': out['c_ctx'], 'w_mod': out['w_mod'], 'b_mod': out['b_mod'], 'norm1_g': out['norm1_g'], 'norm2_g': out['norm2_g'], 'w_in': out['w_in'], 'conv_w': out['conv_w'], 'conv_b': out['conv_b'], 'conv_ln_g': out['conv_ln_g'], 'conv_ln_b': out['conv_ln_b'], 'w_a2_f': out['w_a2_f'], 'b_a_f': out['b_a_f'], 'w_a2_b': out['w_a2_b'], 'b_a_b': out['b_a_b'], 'gla_norm_g': out['gla_norm_g'], 'w_out': out['w_out'], 'w_gate': out['w_gate'], 'w_up': out['w_up'], 'w_down': out['w_down'], 'final_g': out['final_g'], 'loss_target': out['loss_target'], 'm_c_ctx': out['m_c_ctx'], 'm_w_mod': out['m_w_mod'], 'm_b_mod': out['m_b_mod'], 'm_norm1_g': out['m_norm1_g'], 'm_norm2_g': out['m_norm2_g'], 'm_w_in': out['m_w_in'], 'm_conv_w': out['m_conv_w'], 'm_conv_b': out['m_conv_b'], 'm_conv_ln_g': out['m_conv_ln_g'], 'm_conv_ln_b': out['m_conv_ln_b'], 'm_w_a2_f': out['m_w_a2_f'], 'm_b_a_f': out['m_b_a_f'], 'm_w_a2_b': out['m_w_a2_b'], 'm_b_a_b': out['m_b_a_b'], 'm_gla_norm_g': out['m_gla_norm_g'], 'm_w_out': out['m_w_out'], 'm_w_gate': out['m_w_gate'], 'm_w_up': out['m_w_up'], 'm_w_down': out['m_w_down'], 'm_final_g': out['m_final_g'], 'v_c_ctx': out['v_c_ctx'], 'v_w_mod': out['v_w_mod'], 'v_b_mod': out['v_b_mod'], 'v_norm1_g': out['v_norm1_g'], 'v_norm2_g': out['v_norm2_g'], 'v_w_in': out['v_w_in'], 'v_conv_w': out['v_conv_w'], 'v_conv_b': out['v_conv_b'], 'v_conv_ln_g': out['v_conv_ln_g'], 'v_conv_ln_b': out['v_conv_ln_b'], 'v_w_a2_f': out['v_w_a2_f'], 'v_b_a_f': out['v_b_a_f'], 'v_w_a2_b': out['v_w_a2_b'], 'v_b_a_b': out['v_b_a_b'], 'v_gla_norm_g': out['v_gla_norm_g'], 'v_w_out': out['v_w_out'], 'v_w_gate': out['v_w_gate'], 'v_w_up': out['v_w_up'], 'v_w_down': out['v_w_down'], 'v_final_g': out['v_final_g']}


def _loss(weights, diff, rest, loss_target):
    with _jax.named_scope("forward"):
        args = {**rest, TWIN_DIFF_INPUT: diff, **{k: w.astype(_WEIGHT_DTYPES[k]) for k, w in weights.items()}}
        y = _forward(args)
    with _jax.named_scope("loss_head"):
        err = _jnp.square(y.astype(_jnp.float32) - loss_target)
        return 0.5 * _jnp.sum(_jnp.mean(err, axis=-1)) if err.ndim else 0.5 * err


def _adamw(w, g, m, v):
    m = ADAM_B1 * m + (1.0 - ADAM_B1) * g
    v = ADAM_B2 * v + (1.0 - ADAM_B2) * _jnp.square(g)
    m_hat = m / (1.0 - ADAM_B1 ** ADAM_STEP)
    v_hat = v / (1.0 - ADAM_B2 ** ADAM_STEP)
    delta = -ADAM_LR * (m_hat / (_jnp.sqrt(v_hat) + ADAM_EPS) + ADAM_WD * w)
    return delta, m, v


def reference(x, c, ctx, c_ctx, w_mod, b_mod, norm1_g, norm2_g, w_in, conv_w, conv_b, conv_ln_g, conv_ln_b, w_a2_f, b_a_f, w_a2_b, b_a_b, gla_norm_g, w_out, w_gate, w_up, w_down, final_g, loss_target, m_c_ctx, m_w_mod, m_b_mod, m_norm1_g, m_norm2_g, m_w_in, m_conv_w, m_conv_b, m_conv_ln_g, m_conv_ln_b, m_w_a2_f, m_b_a_f, m_w_a2_b, m_b_a_b, m_gla_norm_g, m_w_out, m_w_gate, m_w_up, m_w_down, m_final_g, v_c_ctx, v_w_mod, v_b_mod, v_norm1_g, v_norm2_g, v_w_in, v_conv_w, v_conv_b, v_conv_ln_g, v_conv_ln_b, v_w_a2_f, v_b_a_f, v_w_a2_b, v_b_a_b, v_gla_norm_g, v_w_out, v_w_gate, v_w_up, v_w_down, v_final_g):
    given = dict(x=x, c=c, ctx=ctx, c_ctx=c_ctx, w_mod=w_mod, b_mod=b_mod, norm1_g=norm1_g, norm2_g=norm2_g, w_in=w_in, conv_w=conv_w, conv_b=conv_b, conv_ln_g=conv_ln_g, conv_ln_b=conv_ln_b, w_a2_f=w_a2_f, b_a_f=b_a_f, w_a2_b=w_a2_b, b_a_b=b_a_b, gla_norm_g=gla_norm_g, w_out=w_out, w_gate=w_gate, w_up=w_up, w_down=w_down, final_g=final_g, loss_target=loss_target, m_c_ctx=m_c_ctx, m_w_mod=m_w_mod, m_b_mod=m_b_mod, m_norm1_g=m_norm1_g, m_norm2_g=m_norm2_g, m_w_in=m_w_in, m_conv_w=m_conv_w, m_conv_b=m_conv_b, m_conv_ln_g=m_conv_ln_g, m_conv_ln_b=m_conv_ln_b, m_w_a2_f=m_w_a2_f, m_b_a_f=m_b_a_f, m_w_a2_b=m_w_a2_b, m_b_a_b=m_b_a_b, m_gla_norm_g=m_gla_norm_g, m_w_out=m_w_out, m_w_gate=m_w_gate, m_w_up=m_w_up, m_w_down=m_w_down, m_final_g=m_final_g, v_c_ctx=v_c_ctx, v_w_mod=v_w_mod, v_b_mod=v_b_mod, v_norm1_g=v_norm1_g, v_norm2_g=v_norm2_g, v_w_in=v_w_in, v_conv_w=v_conv_w, v_conv_b=v_conv_b, v_conv_ln_g=v_conv_ln_g, v_conv_ln_b=v_conv_ln_b, v_w_a2_f=v_w_a2_f, v_b_a_f=v_b_a_f, v_w_a2_b=v_w_a2_b, v_b_a_b=v_b_a_b, v_gla_norm_g=v_gla_norm_g, v_w_out=v_w_out, v_w_gate=v_w_gate, v_w_up=v_w_up, v_w_down=v_w_down, v_final_g=v_final_g)
    weights = {n: given[n] for n in TWIN_WEIGHTS}
    shared = {n: given[n] for n in SHARED_INPUTS}
    per_example = {n: given[n] for n in ['x', 'c', 'ctx']}
    grad_fn = _jax.value_and_grad(_loss, argnums=(0, 1))

    def one_microbatch(ex, loss_target):
        ex = dict(ex)
        diff = ex.pop(TWIN_DIFF_INPUT)
        return grad_fn(weights, diff, {**shared, **ex}, loss_target)

    if N_MICROBATCH == 1:
        loss, (grad_w, grad_x) = one_microbatch(per_example, given["loss_target"])
    else:
        def body(carry, xs):
            loss_sum, grad_sum = carry
            l_k, (gw_k, gx_k) = one_microbatch(xs[0], xs[1])
            with _jax.named_scope("update"):
                return (loss_sum + l_k, _jax.tree.map(_jnp.add, grad_sum, gw_k)), gx_k

        init = (_jnp.zeros((), _jnp.float32), _jax.tree.map(_jnp.zeros_like, weights))
        (loss, grad_w), grad_x = _jax.lax.scan(body, init, (per_example, given["loss_target"]))
    with _jax.named_scope("update"):
        delta_w, new_m, new_v = {}, {}, {}
        for n in TWIN_WEIGHTS:
            delta_w[n], new_m[n], new_v[n] = _adamw(weights[n], grad_w[n], given["m_" + n], given["v_" + n])
    return (loss, grad_x, *[grad_w[n] for n in TWIN_WEIGHTS], *[delta_w[n] for n in TWIN_WEIGHTS],
            *[new_m[n] for n in TWIN_WEIGHTS], *[new_v[n] for n in TWIN_WEIGHTS])
```

```python
import functools

import jax
import jax.numpy as jnp
from jax import lax
from jax.experimental import pallas as pl
from jax.experimental.pallas import tpu as pltpu

F32 = jnp.float32
BF = jnp.bfloat16

D = 1024
DC = 512
NH = 4
HK = 64
HV = 128
DK = NH * HK
DV = NH * HV
RANK = 16
CH = 64
GW = 64
CW = 31
CPAD = CW // 2
SEGP = GW + 32
DFF = 2816
DIN = 2592
DINP = 2688
EPS = 1e-6
TAU = 16.0
QSCALE = HK ** -0.5
NCHIP = 4
NDEV = 8

ADAM_LR = 0.001
ADAM_B1 = 0.9
ADAM_B2 = 0.999
ADAM_EPS = 1e-08
ADAM_WD = 0.01
ADAM_STEP = 10

VMEM_LIMIT = 56 * 1024 * 1024
MESH = pl.DeviceIdType.MESH


def _dot(a, b):
    return jnp.dot(a, b, preferred_element_type=F32)


def _dot_nt(a, b):
    return lax.dot_general(a, b, (((1,), (1,)), ((), ())), preferred_element_type=F32)


def _dot_tn(a, b):
    return lax.dot_general(a, b, (((0,), (0,)), ((), ())), preferred_element_type=F32)


def _split3(x):
    hi = x.astype(BF)
    r1 = x - hi.astype(F32)
    mid = r1.astype(BF)
    lo = (r1 - mid.astype(F32)).astype(BF)
    return hi, mid, lo


def _mask_dot(t, x):
    hi, mid, lo = _split3(x)
    return _dot(t, hi) + _dot(t, mid) + _dot(t, lo)


def _sigmoid(x):
    return 1.0 / (1.0 + jnp.exp(-x))


def _log_sigmoid(x):
    return jnp.minimum(x, 0.0) - jnp.log(1.0 + jnp.exp(-jnp.abs(x)))


def _colsum8(z):
    t, c = z.shape
    return jnp.sum(z.reshape(t // 8, 8, c), axis=0)


def _tri(n, kind):
    r = lax.broadcasted_iota(jnp.int32, (n, n), 0)
    c = lax.broadcasted_iota(jnp.int32, (n, n), 1)
    m = {"le": c <= r, "lt": c < r, "ge": c >= r, "gt": c > r}[kind]
    return m


def _full(shape):
    nd = len(shape)
    return pl.BlockSpec(shape, lambda *_: (0,) * nd)


def _cparams(sem, vmem=VMEM_LIMIT):
    return pltpu.CompilerParams(dimension_semantics=sem, vmem_limit_bytes=vmem)


def _conv_taps(pad_ref, s, w_ref, c0, cw, flip):
    acc = jnp.zeros((GW, cw), F32)
    for j in range(CW):
        wj = w_ref[pl.ds((CW - 1 - j) if flip else j, 1), pl.ds(c0, cw)]
        acc = acc + wj * pad_ref[s, pl.ds(j + 1, GW), pl.ds(c0, cw)]
    return acc


def _ln_stats(yb):
    mu = jnp.mean(yb, axis=-1, keepdims=True)
    yc = yb - mu
    var = jnp.mean(yc * yc, axis=-1, keepdims=True)
    rs = lax.rsqrt(var + EPS)
    return yc * rs, rs


def fwd_in(x, vec1, win, convw, cvec, wa2, ba, tm):
    n = x.shape[0]
    nseg = tm // GW
    half = DC // 2

    def body(x_ref, vec_ref, win_ref, cw_ref, cv_ref, wa2_ref, ba_ref,
             ag_ref, yb_ref, co_ref, qk_ref, v_ref, g_ref, la_ref, r_ref, pad_ref):
        xx = x_ref[...]
        rstd = lax.rsqrt(jnp.mean(xx * xx, axis=-1, keepdims=True) + EPS)
        h = (xx * rstd * vec_ref[0:1, :]) * (1.0 + vec_ref[2:3, :]) + vec_ref[1:2, :]
        p = _dot(h.astype(BF), win_ref[...])
        ag_ref[...] = p[:, :2 * DC].astype(BF)
        qk_ref[...] = p[:, 2 * DC:2 * DC + 2 * DK].astype(BF)
        v_ref[...] = p[:, 2 * DC + 2 * DK:2 * DC + 2 * DK + DV].astype(BF)
        g_ref[...] = p[:, 2 * DC + 2 * DK + DV:2 * DC + 2 * DK + 2 * DV].astype(BF)
        r = p[:, DINP - 128:].astype(BF)
        r_ref[...] = r
        la_ref[...] = _log_sigmoid(_dot(r, wa2_ref[...]) + ba_ref[...]) * (1.0 / TAU)

        vc = p[:, :DC] * _sigmoid(p[:, DC:2 * DC])
        zeros = jnp.zeros((nseg, 16, DC), F32)
        pad_ref[:, 0:16, :] = zeros
        pad_ref[:, 16 + GW:SEGP, :] = zeros
        pad_ref[:, 16:16 + GW, :] = vc.reshape(nseg, GW, DC)

        def seg(s, carry):
            for c0 in (0, half):
                y = _conv_taps(pad_ref, s, cw_ref, c0, half, False)
                yb_ref[pl.ds(pl.multiple_of(s * GW, GW), GW), pl.ds(c0, half)] = y + cv_ref[0:1, c0:c0 + half]
            return carry

        lax.fori_loop(0, nseg, seg, 0)
        yn, _ = _ln_stats(yb_ref[...])
        ln = yn * cv_ref[1:2, :] + cv_ref[2:3, :]
        co_ref[...] = (ln * _sigmoid(ln)).astype(BF)

    tok = lambda w: pl.BlockSpec((tm, w), lambda i: (i, 0))
    return pl.pallas_call(
        body, grid=(n // tm,), name="fwd_in",
        in_specs=[tok(D), _full(vec1.shape), _full(win.shape), _full(convw.shape), _full(cvec.shape),
                  _full(wa2.shape), _full(ba.shape)],
        out_specs=[tok(2 * DC), tok(DC), tok(DC), tok(2 * DK), tok(DV), tok(DV), tok(2 * DK), tok(128)],
        out_shape=[jax.ShapeDtypeStruct((n, 2 * DC), BF), jax.ShapeDtypeStruct((n, DC), F32),
                   jax.ShapeDtypeStruct((n, DC), BF), jax.ShapeDtypeStruct((n, 2 * DK), BF),
                   jax.ShapeDtypeStruct((n, DV), BF), jax.ShapeDtypeStruct((n, DV), BF),
                   jax.ShapeDtypeStruct((n, 2 * DK), F32), jax.ShapeDtypeStruct((n, 128), BF)],
        scratch_shapes=[pltpu.VMEM((nseg, SEGP, DC), F32)],
        compiler_params=_cparams(("arbitrary",)),
    )(x, vec1, win, convw, cvec, wa2, ba)


def _gla_dir(d):
    return (_tri(CH, "le"), CH - 1) if d == 0 else (_tri(CH, "ge"), 0)


def _gla_chunk_terms(qk, la, d):
    seen, last = _gla_dir(d)
    b = _mask_dot(seen.astype(BF), la)
    bl = b[last:last + 1, :]
    eb = jnp.exp(b)
    enb = jnp.exp(-b)
    ekd = jnp.exp(bl - b)
    ebl = jnp.exp(bl)
    q = qk[:, :DK].astype(F32) * QSCALE
    k = qk[:, DK:].astype(F32)
    return eb, enb, ekd, ebl, q * eb, k * enb, k * ekd


def gla_fwd(qk, v, la, s0, tm):
    n = qk.shape[0]
    nt = n // tm
    nc = tm // CH

    def body(qkf_ref, vf_ref, laf_ref, qkb_ref, vb_ref, lab_ref, s0_ref,
             of_ref, ob_ref, sef_ref, seb_ref, st_ref):
        @pl.when(pl.program_id(0) == 0)
        def _():
            st_ref[...] = s0_ref[...]

        def chunk(ci, carry):
            for d, (qk_ref, v_ref, la_ref, o_ref, se_ref) in enumerate(
                    ((qkf_ref, vf_ref, laf_ref, of_ref, sef_ref), (qkb_ref, vb_ref, lab_ref, ob_ref, seb_ref))):
                c = ci if d == 0 else nc - 1 - ci
                rows = pl.ds(pl.multiple_of(c * CH, CH), CH)
                amask, _ = _gla_dir(d)
                eb, enb, ekd, ebl, qt, kt, kd = _gla_chunk_terms(qk_ref[rows, :], la_ref[rows, :], d)
                qt = qt.astype(BF)
                kt = kt.astype(BF)
                kd = kd.astype(BF)
                vv = v_ref[rows, :]
                for h in range(NH):
                    ks = slice(h * HK, (h + 1) * HK)
                    vs = slice(h * HV, (h + 1) * HV)
                    st = st_ref[d, h]
                    se_ref[c, h] = st
                    a = jnp.where(amask, _dot_nt(qt[:, ks], kt[:, ks]), 0.0)
                    o_ref[rows, vs] = _dot(a.astype(BF), vv[:, vs]) + _dot_nt(qt[:, ks], st.astype(BF))
                    st_ref[d, h] = ebl[:, ks] * st + _dot_tn(vv[:, vs], kd[:, ks])
            return carry

        lax.fori_loop(0, nc, chunk, 0)

    fw = lambda w, col=0: pl.BlockSpec((tm, w), lambda i: (i, col))
    bw = lambda w, col=0: pl.BlockSpec((tm, w), lambda i: (nt - 1 - i, col))
    se_f = pl.BlockSpec((nc, NH, HV, HK), lambda i: (i, 0, 0, 0))
    se_b = pl.BlockSpec((nc, NH, HV, HK), lambda i: (nt - 1 - i, 0, 0, 0))
    se_shape = jax.ShapeDtypeStruct((n // CH, NH, HV, HK), F32)
    return pl.pallas_call(
        body, grid=(nt,), name="gla_fwd",
        in_specs=[fw(2 * DK), fw(DV), fw(DK, 0), bw(2 * DK), bw(DV), bw(DK, 1), _full(s0.shape)],
        out_specs=[fw(DV), bw(DV), se_f, se_b],
        out_shape=[jax.ShapeDtypeStruct((n, DV), F32), jax.ShapeDtypeStruct((n, DV), F32), se_shape, se_shape],
        scratch_shapes=[pltpu.VMEM((2, NH, HV, HK), F32)],
        compiler_params=_cparams(("arbitrary",)),
    )(qk, v, la, qk, v, la, s0)


def gla_bwd(qk, v, la, do, se_f, se_b, tm):
    n = qk.shape[0]
    nt = n // tm
    nc = tm // CH

    def body(qkf_ref, vf_ref, laf_ref, dof_ref, sef_ref, qkb_ref, vb_ref, lab_ref, dob_ref, seb_ref,
             dqkf_ref, dvf_ref, dlaf_ref, dqkb_ref, dvb_ref, dlab_ref, ds0_ref, ds_ref):
        @pl.when(pl.program_id(0) == 0)
        def _():
            ds_ref[...] = jnp.zeros_like(ds_ref)

        def chunk(ci, carry):
            for d, (qk_ref, v_ref, la_ref, do_ref, se_ref, dqk_ref, dv_ref, dla_ref) in enumerate(
                    ((qkf_ref, vf_ref, laf_ref, dof_ref, sef_ref, dqkf_ref, dvf_ref, dlaf_ref),
                     (qkb_ref, vb_ref, lab_ref, dob_ref, seb_ref, dqkb_ref, dvb_ref, dlab_ref))):
                c = nc - 1 - ci if d == 0 else ci
                rows = pl.ds(pl.multiple_of(c * CH, CH), CH)
                amask, last = _gla_dir(d)
                cum_t = _gla_dir(1 - d)[0]
                eb, enb, ekd, ebl, qt, kt, kd = _gla_chunk_terms(qk_ref[rows, :], la_ref[rows, :], d)
                qtb = qt.astype(BF)
                ktb = kt.astype(BF)
                kdb = kd.astype(BF)
                vv = v_ref[rows, :]
                dd = do_ref[rows, :]
                is_last = lax.broadcasted_iota(jnp.int32, (CH, HK), 0) == last
                dq_parts, dk_parts, db_parts = [], [], []
                for h in range(NH):
                    ks = slice(h * HK, (h + 1) * HK)
                    vs = slice(h * HV, (h + 1) * HV)
                    st = se_ref[c, h]
                    dsn = ds_ref[d, h]
                    dsnb = dsn.astype(BF)
                    a = jnp.where(amask, _dot_nt(qtb[:, ks], ktb[:, ks]), 0.0).astype(BF)
                    da = jnp.where(amask, _dot_nt(dd[:, vs], vv[:, vs]), 0.0).astype(BF)
                    dv_ref[rows, vs] = (_dot_tn(a, dd[:, vs]) + _dot_nt(kdb[:, ks], dsnb)).astype(BF)
                    dkd = _dot(vv[:, vs], dsnb)
                    dqt = _dot(da, ktb[:, ks]) + _dot(dd[:, vs], st.astype(BF))
                    dkt = _dot_tn(da, qtb[:, ks])
                    ds_ref[d, h] = _dot_tn(dd[:, vs], qtb[:, ks]) + ebl[:, ks] * dsn
                    debl = jnp.sum(st * dsn, axis=0, keepdims=True)
                    dq_parts.append(dqt * eb[:, ks] * QSCALE)
                    dk_parts.append(dkt * enb[:, ks] + dkd * ekd[:, ks])
                    dkdkd = dkd * kd[:, ks]
                    dbl = jnp.sum(dkdkd, axis=0, keepdims=True) + debl * ebl[:, ks]
                    db_parts.append(dqt * qt[:, ks] - dkt * kt[:, ks] - dkdkd + jnp.where(is_last, dbl, 0.0))
                dqk_ref[rows, :] = jnp.concatenate(dq_parts + dk_parts, axis=1).astype(BF)
                dla_ref[rows, :] = _mask_dot(cum_t.astype(BF), jnp.concatenate(db_parts, axis=1))
            return carry

        lax.fori_loop(0, nc, chunk, 0)

        @pl.when(pl.program_id(0) == nt - 1)
        def _():
            ds0_ref[...] = ds_ref[...]

    up = lambda w, col=0: pl.BlockSpec((tm, w), lambda i: (i, col))
    dn = lambda w, col=0: pl.BlockSpec((tm, w), lambda i: (nt - 1 - i, col))
    se_up = pl.BlockSpec((nc, NH, HV, HK), lambda i: (i, 0, 0, 0))
    se_dn = pl.BlockSpec((nc, NH, HV, HK), lambda i: (nt - 1 - i, 0, 0, 0))
    return pl.pallas_call(
        body, grid=(nt,), name="gla_bwd",
        in_specs=[dn(2 * DK), dn(DV), dn(DK, 0), dn(DV), se_dn, up(2 * DK), up(DV), up(DK, 1), up(DV), se_up],
        out_specs=[dn(2 * DK), dn(DV), dn(DK), up(2 * DK), up(DV), up(DK), _full((2, NH, HV, HK))],
        out_shape=[jax.ShapeDtypeStruct((n, 2 * DK), BF), jax.ShapeDtypeStruct((n, DV), BF),
                   jax.ShapeDtypeStruct((n, DK), F32), jax.ShapeDtypeStruct((n, 2 * DK), BF),
                   jax.ShapeDtypeStruct((n, DV), BF), jax.ShapeDtypeStruct((n, DK), F32),
                   jax.ShapeDtypeStruct((2, NH, HV, HK), F32)],
        scratch_shapes=[pltpu.VMEM((2, NH, HV, HK), F32)],
        compiler_params=_cparams(("arbitrary",)),
    )(qk, v, la, do, se_f, qk, v, la, do, se_b)


def _head_norm(o):
    ons, rss = [], []
    for h in range(NH):
        oh = o[:, h * HV:(h + 1) * HV]
        rs = lax.rsqrt(jnp.mean(oh * oh, axis=-1, keepdims=True) + EPS)
        ons.append(oh * rs)
        rss.append(rs)
    return ons, rss


def merge_fwd(x, o_f, o_b, g, co, vecm, gn, wout, tm):
    n = x.shape[0]

    def body(x_ref, of_ref, ob_ref, g_ref, co_ref, vec_ref, gn_ref, w_ref, x1_ref, y1_ref, cat_ref):
        o = of_ref[...] + ob_ref[...]
        ons, _ = _head_norm(o)
        gg = g_ref[...].astype(F32)
        sil = gg * _sigmoid(gg)
        cat_ref[:, :DC] = co_ref[...]
        for h in range(NH):
            vs = slice(h * HV, (h + 1) * HV)
            cat_ref[:, DC + h * HV:DC + (h + 1) * HV] = (ons[h] * gn_ref[:, vs] * sil[:, vs]).astype(BF)
        y1 = _dot(cat_ref[...], w_ref[...])
        y1_ref[...] = y1.astype(BF)
        x1_ref[...] = x_ref[...] + vec_ref[0:1, :] * y1

    tok = lambda w: pl.BlockSpec((tm, w), lambda i: (i, 0))
    return pl.pallas_call(
        body, grid=(n // tm,), name="merge_fwd",
        in_specs=[tok(D), tok(DV), tok(DV), tok(DV), tok(DC), _full(vecm.shape), _full(gn.shape), _full(wout.shape)],
        out_specs=[tok(D), tok(D), tok(D)],
        out_shape=[jax.ShapeDtypeStruct((n, D), F32), jax.ShapeDtypeStruct((n, D), BF), jax.ShapeDtypeStruct((n, D), BF)],
        compiler_params=_cparams(("arbitrary",)),
    )(x, o_f, o_b, g, co, vecm, gn, wout)


def merge_bwd(dx1, y1, o_f, o_b, g, vecm, gn, wout, tm):
    n = dx1.shape[0]

    def body(dx1_ref, y1_ref, of_ref, ob_ref, g_ref, vec_ref, gn_ref, w_ref,
             dy1_ref, dco_ref, do_ref, dg_ref, s1_ref, s2_ref):
        @pl.when(pl.program_id(0) == 0)
        def _():
            s1_ref[...] = jnp.zeros_like(s1_ref)
            s2_ref[...] = jnp.zeros_like(s2_ref)

        dx1 = dx1_ref[...]
        s1_ref[...] += _colsum8(dx1 * y1_ref[...].astype(F32))
        dy1 = (dx1 * vec_ref[0:1, :]).astype(BF)
        dy1_ref[...] = dy1
        dcat = _dot_nt(dy1, w_ref[...])
        dco_ref[...] = dcat[:, :DC].astype(BF)
        o = of_ref[...] + ob_ref[...]
        ons, rss = _head_norm(o)
        gg = g_ref[...].astype(F32)
        sg = _sigmoid(gg)
        sil = gg * sg
        dsil = sg * (1.0 + gg * (1.0 - sg))
        for h in range(NH):
            vs = slice(h * HV, (h + 1) * HV)
            do2 = dcat[:, DC + h * HV:DC + (h + 1) * HV]
            gnh = gn_ref[:, vs]
            t = do2 * sil[:, vs]
            s2_ref[:, vs] += _colsum8(t * ons[h])
            don = t * gnh
            do_ref[:, vs] = (rss[h] * (don - ons[h] * jnp.mean(don * ons[h], axis=-1, keepdims=True))).astype(BF)
            dg_ref[:, vs] = (do2 * ons[h] * gnh * dsil[:, vs]).astype(BF)

    tok = lambda w: pl.BlockSpec((tm, w), lambda i: (i, 0))
    return pl.pallas_call(
        body, grid=(n // tm,), name="merge_bwd",
        in_specs=[tok(D), tok(D), tok(DV), tok(DV), tok(DV), _full(vecm.shape), _full(gn.shape), _full(wout.shape)],
        out_specs=[tok(D), tok(DC), tok(DV), tok(DV), _full((8, D)), _full((8, DV))],
        out_shape=[jax.ShapeDtypeStruct((n, D), BF), jax.ShapeDtypeStruct((n, DC), BF), jax.ShapeDtypeStruct((n, DV), BF),
                   jax.ShapeDtypeStruct((n, DV), BF), jax.ShapeDtypeStruct((8, D), F32), jax.ShapeDtypeStruct((8, DV), F32)],
        compiler_params=_cparams(("arbitrary",)),
    )(dx1, y1, o_f, o_b, g, vecm, gn, wout)


def ffn_fwd_bwd(x1, tgt, vecf, wg, wu, wd, tm):
    n = x1.shape[0]

    def body(x1_ref, t_ref, vec_ref, wg_ref, wu_ref, wd_ref,
             dx1_ref, h2_ref, act_ref, dgt_ref, dup_ref, dy2_ref, s_ref):
        @pl.when(pl.program_id(0) == 0)
        def _():
            s_ref[...] = jnp.zeros_like(s_ref)

        n2g, sh2, sc2, g2, fg = (vec_ref[i:i + 1, :] for i in range(5))
        x1 = x1_ref[...]
        r2 = lax.rsqrt(jnp.mean(x1 * x1, axis=-1, keepdims=True) + EPS)
        xn2 = x1 * r2
        h2 = (xn2 * n2g * (1.0 + sc2) + sh2).astype(BF)
        h2_ref[...] = h2
        gt = _dot(h2, wg_ref[...])
        up = _dot(h2, wu_ref[...])
        sg = _sigmoid(gt)
        sil = gt * sg
        act = (sil * up).astype(BF)
        act_ref[...] = act
        y2 = _dot(act, wd_ref[...])
        x2 = x1 + g2 * y2
        r3 = lax.rsqrt(jnp.mean(x2 * x2, axis=-1, keepdims=True) + EPS)
        xn3 = x2 * r3
        e = xn3 * fg - t_ref[...]
        s_ref[40:48, :] += _colsum8(e * e) * (0.5 / D)
        dyo = e * (1.0 / D)
        s_ref[0:8, :] += _colsum8(dyo * xn3)
        dxn3 = dyo * fg
        dx2 = r3 * (dxn3 - xn3 * jnp.mean(dxn3 * xn3, axis=-1, keepdims=True))
        s_ref[8:16, :] += _colsum8(dx2 * y2)
        dy2 = (dx2 * g2).astype(BF)
        dy2_ref[...] = dy2
        dact = _dot_nt(dy2, wd_ref[...])
        dup = (dact * sil).astype(BF)
        dgt = (dact * up * (sg * (1.0 + gt * (1.0 - sg)))).astype(BF)
        dup_ref[...] = dup
        dgt_ref[...] = dgt
        dh2 = _dot_nt(dgt, wg_ref[...]) + _dot_nt(dup, wu_ref[...])
        s_ref[16:24, :] += _colsum8(dh2)
        t = dh2 * xn2
        s_ref[24:32, :] += _colsum8(t * n2g)
        s_ref[32:40, :] += _colsum8(t * (1.0 + sc2))
        dxn2 = dh2 * ((1.0 + sc2) * n2g)
        dx1_ref[...] = dx2 + r2 * (dxn2 - xn2 * jnp.mean(dxn2 * xn2, axis=-1, keepdims=True))

    tok = lambda w: pl.BlockSpec((tm, w), lambda i: (i, 0))
    wspec = lambda a: pl.BlockSpec(a.shape, lambda i: (0, 0), pipeline_mode=pl.Buffered(1))
    return pl.pallas_call(
        body, grid=(n // tm,), name="ffn_fwd_bwd",
        in_specs=[tok(D), tok(D), _full(vecf.shape), wspec(wg), wspec(wu), wspec(wd)],
        out_specs=[tok(D), tok(D), tok(DFF), tok(DFF), tok(DFF), tok(D), _full((48, D))],
        out_shape=[jax.ShapeDtypeStruct((n, D), F32), jax.ShapeDtypeStruct((n, D), BF), jax.ShapeDtypeStruct((n, DFF), BF),
                   jax.ShapeDtypeStruct((n, DFF), BF), jax.ShapeDtypeStruct((n, DFF), BF), jax.ShapeDtypeStruct((n, D), BF),
                   jax.ShapeDtypeStruct((48, D), F32)],
        compiler_params=_cparams(("arbitrary",)),
    )(x1, tgt, vecf, wg, wu, wd)


def wgrad(a, b, init, t1, t2, tn, name):
    n, k1 = a.shape
    k2 = b.shape[1]

    def body(a_ref, b_ref, i_ref, o_ref):
        @pl.when(pl.program_id(2) == 0)
        def _():
            o_ref[...] = i_ref[...]

        o_ref[...] += _dot_tn(a_ref[...], b_ref[...])

    return pl.pallas_call(
        body, grid=(k1 // t1, k2 // t2, n // tn), name=name,
        in_specs=[pl.BlockSpec((tn, t1), lambda i, j, k: (k, i)), pl.BlockSpec((tn, t2), lambda i, j, k: (k, j)),
                  pl.BlockSpec((t1, t2), lambda i, j, k: (i, j))],
        out_specs=pl.BlockSpec((t1, t2), lambda i, j, k: (i, j)),
        out_shape=jax.ShapeDtypeStruct((k1, k2), F32),
        input_output_aliases={2: 0},
        compiler_params=_cparams(("parallel", "parallel", "arbitrary")),
    )(a, b, init)


def bwd_in(x, dx1, ag, yb, dco, dqk_f, dqk_b, dv_f, dv_b, dg, dla_f, dla_b, la, r, vec1, win, convw, cvec, wa2, tm):
    n = x.shape[0]
    nseg = tm // GW
    half = DC // 2

    def body(x_ref, dx1_ref, ag_ref, yb_ref, dco_ref, dqkf_ref, dqkb_ref, dvf_ref, dvb_ref, dg_ref, dlaf_ref, dlab_ref,
             la_ref, r_ref, vec_ref, win_ref, cw_ref, cv_ref, wa2_ref,
             gx_ref, h_ref, dp_ref, dwa2_ref, dcw_ref, s_ref, pad1_ref, pad2_ref, dvc_ref, dcw8_ref):
        first = pl.program_id(0) == 0

        @pl.when(first)
        def _():
            s_ref[...] = jnp.zeros_like(s_ref)
            dwa2_ref[...] = jnp.zeros_like(dwa2_ref)
            dcw8_ref[...] = jnp.zeros_like(dcw8_ref)

        yn, rs = _ln_stats(yb_ref[...])
        lng = cv_ref[1:2, :]
        ln = yn * lng + cv_ref[2:3, :]
        sgl = _sigmoid(ln)
        dln = dco_ref[...].astype(F32) * (sgl * (1.0 + ln * (1.0 - sgl)))
        dyn = dln * lng
        dyb = rs * (dyn - jnp.mean(dyn, axis=-1, keepdims=True) - yn * jnp.mean(dyn * yn, axis=-1, keepdims=True))
        s_ref[24:32, 0:DC] += _colsum8(dyb)
        s_ref[24:32, DC:D] += _colsum8(dln * yn)
        s_ref[32:40, 0:DC] += _colsum8(dln)

        agv = ag_ref[...].astype(F32)
        a = agv[:, :DC]
        sgg = _sigmoid(agv[:, DC:])
        zeros = jnp.zeros((nseg, 16, DC), F32)
        for pr, val in ((pad1_ref, a * sgg), (pad2_ref, dyb)):
            pr[:, 0:16, :] = zeros
            pr[:, 16 + GW:SEGP, :] = zeros
            pr[:, 16:16 + GW, :] = val.reshape(nseg, GW, DC)

        def seg(s, carry):
            rows = pl.ds(pl.multiple_of(s * GW, GW), GW)
            for c0 in (0, half):
                cs = pl.ds(c0, half)
                dvc_ref[rows, cs] = _conv_taps(pad2_ref, s, cw_ref, c0, half, True)
                dys = pad2_ref[s, pl.ds(16, GW), cs]
                for j in range(CW):
                    dcw8_ref[j, :, cs] += _colsum8(dys * pad1_ref[s, pl.ds(j + 1, GW), cs])
            return carry

        lax.fori_loop(0, nseg, seg, 0)
        dvc = dvc_ref[...]
        dp_ref[:, 0:DC] = (dvc * sgg).astype(BF)
        dp_ref[:, DC:2 * DC] = (dvc * a * sgg * (1.0 - sgg)).astype(BF)

        dp_ref[:, 2 * DC:2 * DC + 2 * DK] = (dqkf_ref[...].astype(F32) + dqkb_ref[...].astype(F32)).astype(BF)
        dp_ref[:, 2 * DC + 2 * DK:2 * DC + 2 * DK + DV] = (dvf_ref[...].astype(F32) + dvb_ref[...].astype(F32)).astype(BF)
        dp_ref[:, 2 * DC + 2 * DK + DV:2 * DC + 2 * DK + 2 * DV] = dg_ref[...]

        la = la_ref[...]
        dla = jnp.concatenate([dlaf_ref[...], dlab_ref[...]], axis=1)
        dpre = dla * (1.0 - jnp.exp(TAU * la)) * (1.0 / TAU)
        s_ref[32:40, DC:D] += _colsum8(dpre)
        dpreb = dpre.astype(BF)
        dwa2_ref[...] += _dot_tn(r_ref[...], dpreb)
        dp_ref[:, DINP - 128:] = _dot_nt(dpreb, wa2_ref[...]).astype(BF)

        dh = _dot_nt(dp_ref[...], win_ref[...])
        xx = x_ref[...]
        n1g, sh1, sc1 = vec_ref[0:1, :], vec_ref[1:2, :], vec_ref[2:3, :]
        rstd = lax.rsqrt(jnp.mean(xx * xx, axis=-1, keepdims=True) + EPS)
        xn = xx * rstd
        h_ref[...] = (xn * n1g * (1.0 + sc1) + sh1).astype(BF)
        s_ref[0:8, :] += _colsum8(dh)
        t = dh * xn
        s_ref[8:16, :] += _colsum8(t * n1g)
        s_ref[16:24, :] += _colsum8(t * (1.0 + sc1))
        dxn = dh * ((1.0 + sc1) * n1g)
        gx_ref[...] = dx1_ref[...] + rstd * (dxn - xn * jnp.mean(dxn * xn, axis=-1, keepdims=True))

        @pl.when(pl.program_id(0) == pl.num_programs(0) - 1)
        def _():
            dcw_ref[...] = jnp.sum(dcw8_ref[...], axis=1)

    tok = lambda w: pl.BlockSpec((tm, w), lambda i: (i, 0))
    return pl.pallas_call(
        body, grid=(n // tm,), name="bwd_in",
        in_specs=[tok(D), tok(D), tok(2 * DC), tok(DC), tok(DC), tok(2 * DK), tok(2 * DK), tok(DV), tok(DV), tok(DV),
                  tok(DK), tok(DK), tok(2 * DK), tok(128), _full(vec1.shape),
                  pl.BlockSpec(win.shape, lambda i: (0, 0), pipeline_mode=pl.Buffered(1)),
                  _full(convw.shape), _full(cvec.shape), _full(wa2.shape)],
        out_specs=[tok(D), tok(D), tok(DINP), _full((128, 2 * DK)), _full((32, DC)), _full((40, D))],
        out_shape=[jax.ShapeDtypeStruct((n, D), F32), jax.ShapeDtypeStruct((n, D), BF), jax.ShapeDtypeStruct((n, DINP), BF),
                   jax.ShapeDtypeStruct((128, 2 * DK), F32), jax.ShapeDtypeStruct((32, DC), F32),
                   jax.ShapeDtypeStruct((40, D), F32)],
        scratch_shapes=[pltpu.VMEM((nseg, SEGP, DC), F32), pltpu.VMEM((nseg, SEGP, DC), F32), pltpu.VMEM((tm, DC), F32),
                        pltpu.VMEM((32, 8, DC), F32)],
        compiler_params=_cparams(("arbitrary",)),
    )(x, dx1, ag, yb, dco, dqk_f, dqk_b, dv_f, dv_b, dg, dla_f, dla_b, la, r, vec1, win, convw, cvec, wa2)


def _ctx_common(ctx_ref, vec_ref, win_ref, wa2_ref, ba_ref):
    cx = ctx_ref[...]
    t = cx.shape[0]
    rstd = lax.rsqrt(jnp.mean(cx * cx, axis=-1, keepdims=True) + EPS)
    xn = cx * rstd
    hc = (xn * vec_ref[0:1, :] * (1.0 + vec_ref[2:3, :]) + vec_ref[1:2, :]).astype(BF)
    k0 = 2 * DC + DK
    kv = _dot(hc, win_ref[:, k0:k0 + DK + DV]).astype(BF).astype(F32)
    r = _dot(hc, win_ref[:, DINP - 128:]).astype(BF)
    la = _log_sigmoid(_dot(r, wa2_ref[...]) + ba_ref[...]) * (1.0 / TAU)
    incl = _tri(t, "le").astype(BF)
    strict = _tri(t, "lt").astype(BF)
    bf = _mask_dot(incl, la[:, :DK])
    wf = jnp.exp(bf[t - 1:t, :] - bf)
    wb = jnp.exp(_mask_dot(strict, la[:, DK:]))
    return xn, hc, kv[:, :DK], kv[:, DK:], r, la, wf, wb


def ctx_fwd(ctx, vecc, win, wa2, ba):
    def body(ctx_ref, vec_ref, win_ref, wa2_ref, ba_ref, s_ref):
        _, _, k, v, _, _, wf, wb = _ctx_common(ctx_ref, vec_ref, win_ref, wa2_ref, ba_ref)
        vb = v.astype(BF)
        for d, w in enumerate((wf, wb)):
            kd = (k * w).astype(BF)
            for h in range(NH):
                s_ref[d, h] = _dot_tn(vb[:, h * HV:(h + 1) * HV], kd[:, h * HK:(h + 1) * HK])

    return pl.pallas_call(
        body, name="ctx_fwd", out_shape=jax.ShapeDtypeStruct((2, NH, HV, HK), F32),
        compiler_params=pltpu.CompilerParams(vmem_limit_bytes=VMEM_LIMIT),
    )(ctx, vecc, win, wa2, ba)


def ctx_bwd(ctx, vecc, win, wa2, ba, ds0):
    t = ctx.shape[0]

    def body(ctx_ref, vec_ref, win_ref, wa2_ref, ba_ref, ds_ref, dwin_ref, dwa2_ref, s_ref, dpc_ref):
        xn, hc, k, v, r, la, wf, wb = _ctx_common(ctx_ref, vec_ref, win_ref, wa2_ref, ba_ref)
        vb = v.astype(BF)
        strict = _tri(t, "lt").astype(BF)
        strict_t = _tri(t, "gt").astype(BF)
        dpc_ref[...] = jnp.zeros_like(dpc_ref)
        k0 = 2 * DC + DK
        dk = jnp.zeros((t, DK), F32)
        des = []
        for d, w in enumerate((wf, wb)):
            kd = (k * w).astype(BF)
            dkds = []
            for h in range(NH):
                dsb = ds_ref[d, h].astype(BF)
                dkds.append(_dot(vb[:, h * HV:(h + 1) * HV], dsb))
                dvh = _dot_nt(kd[:, h * HK:(h + 1) * HK], dsb)
                vs = slice(k0 + DK + h * HV, k0 + DK + (h + 1) * HV)
                if d == 0:
                    dpc_ref[:, vs] = dvh.astype(BF)
                else:
                    dpc_ref[:, vs] = (dpc_ref[:, vs].astype(F32) + dvh).astype(BF)
            dkd = jnp.concatenate(dkds, axis=1)
            dk = dk + dkd * w
            des.append(dkd * k * w)
        dpc_ref[:, k0:k0 + DK] = dk.astype(BF)
        dla = jnp.concatenate([_mask_dot(strict, des[0]), _mask_dot(strict_t, des[1])], axis=1)
        dpre = dla * (1.0 - jnp.exp(TAU * la)) * (1.0 / TAU)
        dpreb = dpre.astype(BF)
        dwa2_ref[...] = _dot_tn(r, dpreb)
        dpc_ref[:, DINP - 128:] = _dot_nt(dpreb, wa2_ref[...]).astype(BF)
        dpc = dpc_ref[...]
        dwin_ref[...] = _dot_tn(hc, dpc)
        dhc = _dot_nt(dpc, win_ref[...])
        n1g, sc1 = vec_ref[0:1, :], vec_ref[2:3, :]
        tt = dhc * xn
        s_ref[...] = jnp.zeros_like(s_ref)
        s_ref[0:1, :] = jnp.sum(tt * (1.0 + sc1), axis=0, keepdims=True)
        s_ref[1:2, :] = jnp.sum(dhc, axis=0, keepdims=True)
        s_ref[2:3, :] = jnp.sum(tt * n1g, axis=0, keepdims=True)
        s_ref[3:4, DC:D] = jnp.sum(dpre, axis=0, keepdims=True)

    return pl.pallas_call(
        body, name="ctx_bwd",
        out_shape=[jax.ShapeDtypeStruct((D, DINP), F32), jax.ShapeDtypeStruct((128, 2 * DK), F32),
                   jax.ShapeDtypeStruct((8, D), F32)],
        scratch_shapes=[pltpu.VMEM((t, DINP), BF)],
        compiler_params=pltpu.CompilerParams(vmem_limit_bytes=VMEM_LIMIT),
    )(ctx, vecc, win, wa2, ba, ds0)


def _silu(x):
    return x * _sigmoid(x)


def mod_fwd(cext, wm, bm):
    def body(c_ref, w_ref, b_ref, o_ref):
        o_ref[...] = _dot(_silu(c_ref[...]).astype(BF), w_ref[...].astype(BF)) + b_ref[...]

    return pl.pallas_call(body, name="mod_fwd", out_shape=jax.ShapeDtypeStruct((cext.shape[0], wm.shape[1]), F32),
                          compiler_params=pltpu.CompilerParams(vmem_limit_bytes=VMEM_LIMIT))(cext, wm, bm)


def mod_bwd(cext, dm, wm):
    def body(c_ref, d_ref, w_ref, gw_ref, ds_ref):
        dmb = d_ref[...].astype(BF)
        gw_ref[...] = _dot_tn(_silu(c_ref[...]).astype(BF), dmb)
        ds_ref[...] = _dot_nt(dmb, w_ref[...].astype(BF))

    return pl.pallas_call(body, name="mod_bwd",
                          out_shape=[jax.ShapeDtypeStruct(wm.shape, F32), jax.ShapeDtypeStruct(cext.shape, F32)],
                          compiler_params=pltpu.CompilerParams(vmem_limit_bytes=VMEM_LIMIT))(cext, dm, wm)


def pack_small(sf, s1, s2, sd, sc, dcw, dwa2, dwa2_c):
    def body(sf_ref, s1_ref, s2_ref, sd_ref, sc_ref, dcw_ref, dwa2_ref, dwa2c_ref, o_ref, ocw_ref, owa_ref):
        rsum = lambda ref, i: jnp.sum(ref[8 * i:8 * i + 8, :], axis=0, keepdims=True)
        o_ref[...] = jnp.zeros_like(o_ref)
        o_ref[0:1, :] = rsum(sd_ref, 0)
        o_ref[1:2, :] = rsum(sd_ref, 1)
        o_ref[2:3, :] = rsum(s1_ref, 0)
        o_ref[3:4, :] = rsum(sf_ref, 2)
        o_ref[4:5, :] = rsum(sf_ref, 3)
        o_ref[5:6, :] = rsum(sf_ref, 1)
        o_ref[6:7, :] = sc_ref[1:2, :]
        o_ref[7:8, :] = sc_ref[2:3, :]
        o_ref[8:9, :] = rsum(sd_ref, 2) + sc_ref[0:1, :]
        o_ref[9:10, :] = rsum(sf_ref, 4)
        o_ref[10:11, :] = rsum(sf_ref, 0)
        o_ref[11:12, :] = rsum(sd_ref, 3)
        o_ref[12:13, :] = rsum(sd_ref, 4) + sc_ref[3:4, :]
        g = jnp.sum(s2_ref[...], axis=0, keepdims=True)
        o_ref[13:14, 0:HV] = g[:, 0:HV] + g[:, HV:2 * HV] + g[:, 2 * HV:3 * HV] + g[:, 3 * HV:4 * HV]
        o_ref[14:15, :] = rsum(sf_ref, 5)
        ocw_ref[...] = dcw_ref[...]
        owa_ref[...] = dwa2_ref[0:32, :] + dwa2c_ref[0:32, :]

    return pl.pallas_call(body, name="pack_small",
                          out_shape=[jax.ShapeDtypeStruct((16, D), F32), jax.ShapeDtypeStruct((32, DC), F32),
                                     jax.ShapeDtypeStruct((32, 2 * DK), F32)])(sf, s1, s2, sd, sc, dcw, dwa2, dwa2_c)


def sum_leading(a, rows, name):
    k, r, c = a.shape

    def body(a_ref, o_ref):
        acc = a_ref[0]
        for i in range(1, k):
            acc = acc + a_ref[i]
        o_ref[...] = acc

    return pl.pallas_call(
        body, grid=(r // rows,), name=name,
        in_specs=[pl.BlockSpec((k, rows, c), lambda i: (0, i, 0))],
        out_specs=pl.BlockSpec((rows, c), lambda i: (i, 0)),
        out_shape=jax.ShapeDtypeStruct((r, c), F32),
        compiler_params=_cparams(("parallel",)),
    )(a)


def add_pairs(a, b, rows, name):
    k, r, c = a.shape

    def body(a_ref, b_ref, o_ref):
        o_ref[...] = a_ref[...] + b_ref[...]

    spec = pl.BlockSpec((1, rows, c), lambda i, j: (i, j, 0))
    return pl.pallas_call(
        body, grid=(k, r // rows), name=name, in_specs=[spec, spec], out_specs=spec,
        out_shape=jax.ShapeDtypeStruct(a.shape, F32), compiler_params=_cparams(("parallel", "parallel")),
    )(a, b)


def small_totals(g8):
    r = g8.shape[1]

    def body(g_ref, t_ref, bm_ref, loss_ref):
        acc = g_ref[0]
        for i in range(1, NDEV):
            acc = acc + g_ref[i]
        t_ref[...] = acc
        bm_ref[...] = jnp.zeros_like(bm_ref)
        bm_ref[0:6, :] = acc[0:6, :]
        bm_ref[0:2, :] += acc[6:8, :]
        loss_ref[...] = jnp.broadcast_to(jnp.sum(acc[14:15, :], axis=1, keepdims=True), loss_ref.shape)

    return pl.pallas_call(body, name="small_totals",
                          out_shape=[jax.ShapeDtypeStruct((r, D), F32), jax.ShapeDtypeStruct((8, D), F32),
                                     jax.ShapeDtypeStruct((8, 128), F32)])(g8)


def cctx_grad(p8, c_ctx_row):
    def body(p_ref, c_ref, o_ref):
        acc = p_ref[0, 0:1, :]
        for j in range(1, NCHIP):
            acc = acc + p_ref[2 * j, 0:1, :]
        cc = c_ref[0:1, :]
        sg = _sigmoid(cc)
        o_ref[...] = jnp.zeros_like(o_ref)
        o_ref[0:1, :] = acc * (sg * (1.0 + cc * (1.0 - sg)))

    return pl.pallas_call(body, name="cctx_grad", out_shape=jax.ShapeDtypeStruct((8, D), F32))(p8, c_ctx_row)


def adamw(w, g, m, v, rows, name):
    r, c = w.shape

    def body(w_ref, g_ref, m_ref, v_ref, d_ref, nm_ref, nv_ref):
        gg = g_ref[...]
        nm = ADAM_B1 * m_ref[...] + (1.0 - ADAM_B1) * gg
        nv = ADAM_B2 * v_ref[...] + (1.0 - ADAM_B2) * (gg * gg)
        m_hat = nm / (1.0 - ADAM_B1 ** ADAM_STEP)
        v_hat = nv / (1.0 - ADAM_B2 ** ADAM_STEP)
        d_ref[...] = -ADAM_LR * (m_hat / (jnp.sqrt(v_hat) + ADAM_EPS) + ADAM_WD * w_ref[...])
        nm_ref[...] = nm
        nv_ref[...] = nv

    spec = pl.BlockSpec((rows, c), lambda i: (i, 0))
    sds = jax.ShapeDtypeStruct((r, c), F32)
    return pl.pallas_call(
        body, grid=(r // rows,), name=name, in_specs=[spec] * 4, out_specs=[spec] * 3, out_shape=[sds] * 3,
        compiler_params=_cparams(("parallel",)),
    )(w, g, m, v)


def _me():
    return lax.axis_index("x"), lax.axis_index("y"), lax.axis_index("c")


def _flip(v, bit):
    return 1 - v if bit else v


ANY = pl.BlockSpec(memory_space=pl.ANY)


def all_gather8(x, name):
    r, c = x.shape

    def body(x_ref, o_ref, ssem, rsem, lsem):
        mx, my, mc = _me()
        me = 4 * mx + 2 * my + mc
        local = pltpu.make_async_copy(x_ref, o_ref.at[me], lsem)
        local.start()

        def copy(k):
            px, py, pc = _flip(mx, k & 4), _flip(my, k & 2), _flip(mc, k & 1)
            return px, py, pc

        sends = []
        for k in range(1, NDEV):
            cp = pltpu.make_async_remote_copy(src_ref=x_ref, dst_ref=o_ref.at[me], send_sem=ssem.at[k - 1],
                                              recv_sem=rsem.at[k - 1], device_id=copy(k), device_id_type=MESH)
            cp.start()
            sends.append(cp)
        for k in range(1, NDEV):
            px, py, pc = copy(k)
            pltpu.make_async_remote_copy(src_ref=x_ref, dst_ref=o_ref.at[4 * px + 2 * py + pc], send_sem=ssem.at[k - 1],
                                         recv_sem=rsem.at[k - 1], device_id=(px, py, pc), device_id_type=MESH).wait_recv()
        for cp in sends:
            cp.wait_send()
        local.wait()

    vm = pl.BlockSpec(memory_space=pltpu.VMEM)
    return pl.pallas_call(
        body, name=name, in_specs=[vm], out_specs=vm, out_shape=jax.ShapeDtypeStruct((NDEV, r, c), x.dtype),
        scratch_shapes=[pltpu.SemaphoreType.DMA((NDEV - 1,)), pltpu.SemaphoreType.DMA((NDEV - 1,)), pltpu.SemaphoreType.DMA],
    )(x)


def _chip_peers(mx, my):
    out = []
    for p in range(1, NCHIP):
        px, py = _flip(mx, p & 2), _flip(my, p & 1)
        out.append((px, py, 2 * px + py))
    return out


def gather_weights(shards):
    n = len(shards)

    def body(*refs):
        ins, outs = refs[:n], refs[n:2 * n]
        ssem, rsem, lsem = refs[2 * n:]
        mx, my, mc = _me()
        jme = 2 * mx + my
        peers = _chip_peers(mx, my)
        started = []
        for k in range(n):
            lc = pltpu.make_async_copy(ins[k], outs[k].at[jme], lsem.at[k])
            lc.start()
            started.append(lc)
        sends = []
        for k in range(n):
            for p, (px, py, _) in enumerate(peers):
                cp = pltpu.make_async_remote_copy(src_ref=ins[k], dst_ref=outs[k].at[jme], send_sem=ssem.at[3 * k + p],
                                                  recv_sem=rsem.at[3 * k + p], device_id=(px, py, mc), device_id_type=MESH)
                cp.start()
                sends.append(cp)
        for k in range(n):
            for p, (px, py, jp) in enumerate(peers):
                pltpu.make_async_remote_copy(src_ref=ins[k], dst_ref=outs[k].at[jp], send_sem=ssem.at[3 * k + p],
                                             recv_sem=rsem.at[3 * k + p], device_id=(px, py, mc), device_id_type=MESH).wait_recv()
        for cp in sends:
            cp.wait_send()
        for lc in started:
            lc.wait()

    return pl.pallas_call(
        body, name="gather_weights", in_specs=[ANY] * n, out_specs=[ANY] * n,
        out_shape=[jax.ShapeDtypeStruct((NCHIP,) + s.shape, s.dtype) for s in shards],
        scratch_shapes=[pltpu.SemaphoreType.DMA((3 * n,)), pltpu.SemaphoreType.DMA((3 * n,)), pltpu.SemaphoreType.DMA((n,))],
    )(*shards)


def rs_sibling(grads):
    n = len(grads)

    def body(*refs):
        ins, mine, theirs = refs[:n], refs[n:2 * n], refs[2 * n:3 * n]
        ssem, rsem, lsem = refs[3 * n:]
        mx, my, mc = _me()
        work = []
        for k in range(n):
            r2 = grads[k].shape[1] // 2
            keep = pl.ds(pl.multiple_of(mc * r2, 8), r2)
            give = pl.ds(pl.multiple_of((1 - mc) * r2, 8), r2)
            lc = pltpu.make_async_copy(ins[k].at[:, keep, :], mine[k], lsem.at[k])
            cp = pltpu.make_async_remote_copy(src_ref=ins[k].at[:, give, :], dst_ref=theirs[k], send_sem=ssem.at[k],
                                              recv_sem=rsem.at[k], device_id=(mx, my, 1 - mc), device_id_type=MESH)
            lc.start()
            cp.start()
            work.append((lc, cp))
        for lc, cp in work:
            cp.wait_recv()
            cp.wait_send()
            lc.wait()

    half = [jax.ShapeDtypeStruct((NCHIP, g.shape[1] // 2, g.shape[2]), F32) for g in grads]
    return pl.pallas_call(
        body, name="rs_sibling", in_specs=[ANY] * n, out_specs=[ANY] * (2 * n), out_shape=half + half,
        scratch_shapes=[pltpu.SemaphoreType.DMA((n,)), pltpu.SemaphoreType.DMA((n,)), pltpu.SemaphoreType.DMA((n,))],
    )(*grads)


def rs_chips(parts):
    n = len(parts)

    def body(*refs):
        ins, outs = refs[:n], refs[n:2 * n]
        ssem, rsem, lsem = refs[2 * n:]
        mx, my, mc = _me()
        jme = 2 * mx + my
        peers = _chip_peers(mx, my)
        locals_ = []
        for k in range(n):
            lc = pltpu.make_async_copy(ins[k].at[jme], outs[k].at[jme], lsem.at[k])
            lc.start()
            locals_.append(lc)
        sends = []
        for k in range(n):
            for p, (px, py, jp) in enumerate(peers):
                cp = pltpu.make_async_remote_copy(src_ref=ins[k].at[jp], dst_ref=outs[k].at[jme], send_sem=ssem.at[3 * k + p],
                                                  recv_sem=rsem.at[3 * k + p], device_id=(px, py, mc), device_id_type=MESH)
                cp.start()
                sends.append(cp)
        for k in range(n):
            for p, (px, py, jp) in enumerate(peers):
                pltpu.make_async_remote_copy(src_ref=ins[k].at[jp], dst_ref=outs[k].at[jp], send_sem=ssem.at[3 * k + p],
                                             recv_sem=rsem.at[3 * k + p], device_id=(px, py, mc), device_id_type=MESH).wait_recv()
        for cp in sends:
            cp.wait_send()
        for lc in locals_:
            lc.wait()

    return pl.pallas_call(
        body, name="rs_chips", in_specs=[ANY] * n, out_specs=[ANY] * n,
        out_shape=[jax.ShapeDtypeStruct(p.shape, F32) for p in parts],
        scratch_shapes=[pltpu.SemaphoreType.DMA((3 * n,)), pltpu.SemaphoreType.DMA((3 * n,)), pltpu.SemaphoreType.DMA((n,))],
    )(*parts)


def share_halves(halves):
    n = len(halves)

    def body(*refs):
        ins, outs = refs[:n], refs[n:2 * n]
        ssem, rsem, lsem = refs[2 * n:]
        mx, my, mc = _me()
        work = []
        for k in range(n):
            r2 = halves[k].shape[0]
            mine = pl.ds(pl.multiple_of(mc * r2, 8), r2)
            theirs = pl.ds(pl.multiple_of((1 - mc) * r2, 8), r2)
            lc = pltpu.make_async_copy(ins[k], outs[k].at[mine, :], lsem.at[k])
            cp = pltpu.make_async_remote_copy(src_ref=ins[k], dst_ref=outs[k].at[mine, :], send_sem=ssem.at[k],
                                              recv_sem=rsem.at[k], device_id=(mx, my, 1 - mc), device_id_type=MESH)
            wt = pltpu.make_async_remote_copy(src_ref=ins[k], dst_ref=outs[k].at[theirs, :], send_sem=ssem.at[k],
                                              recv_sem=rsem.at[k], device_id=(mx, my, 1 - mc), device_id_type=MESH)
            lc.start()
            cp.start()
            work.append((lc, cp, wt))
        for lc, cp, wt in work:
            wt.wait_recv()
            cp.wait_send()
            lc.wait()

    return pl.pallas_call(
        body, name="share_halves", in_specs=[ANY] * n, out_specs=[ANY] * n,
        out_shape=[jax.ShapeDtypeStruct((2 * h.shape[0], h.shape[1]), F32) for h in halves],
        scratch_shapes=[pltpu.SemaphoreType.DMA((n,)), pltpu.SemaphoreType.DMA((n,)), pltpu.SemaphoreType.DMA((n,))],
    )(*halves)


TM_IN = 256
TM_GLA = 512
TM_MERGE = 512
TM_FFN = 256
TN_WGRAD = 1024

WEIGHTS = ['c_ctx', 'w_mod', 'b_mod', 'norm1_g', 'norm2_g', 'w_in', 'conv_w', 'conv_b', 'conv_ln_g', 'conv_ln_b', 'w_a2_f',
           'b_a_f', 'w_a2_b', 'b_a_b', 'gla_norm_g', 'w_out', 'w_gate', 'w_up', 'w_down', 'final_g']
BIG = ['w_in', 'w_out', 'w_gate', 'w_up', 'w_down']


def _rows(*vs):
    out = jnp.zeros((8, vs[0].shape[-1]), F32)
    for i, v in enumerate(vs):
        out = out.at[i].set(v.reshape(-1))
    return out


def _small_slab(p):
    cat = lambda *ks: jnp.concatenate([p[k].reshape(-1) for k in ks])
    vecs = _rows(p['c_ctx'], p['norm1_g'], p['norm2_g'], p['final_g'], cat('conv_b', 'conv_ln_g'),
                 cat('conv_ln_b', 'b_a_f', 'b_a_b'), jnp.pad(p['gla_norm_g'].reshape(-1), (0, D - HV)))
    bmod = jnp.pad(p['b_mod'].reshape(6, D), ((0, 2), (0, 0)))
    shards = jnp.pad(jnp.concatenate([jnp.pad(p['conv_w'].reshape(-1), (0, DC // NCHIP)), cat('w_a2_f', 'w_a2_b')]),
                     (0, 2 * D)).reshape(8, D)
    return jnp.concatenate([vecs, bmod, shards], axis=0)


def _unslab(s):
    return {
        'c_ctx': s[0], 'norm1_g': s[1:2], 'norm2_g': s[2:3], 'final_g': s[3],
        'conv_b': s[4:5, :DC], 'conv_ln_g': s[4:5, DC:], 'conv_ln_b': s[5:6, :DC],
        'b_a_f': s[5:6, DC:DC + DK], 'b_a_b': s[5:6, DC + DK:], 'gla_norm_g': s[6:7, :HV],
        'b_mod': s[8:14].reshape(1, 6 * D),
        'conv_w': s[16:20].reshape(32, DC // NCHIP)[:CW].reshape(1, CW, DC // NCHIP),
        'w_a2_f': s[20].reshape(1, RANK, DK // NCHIP), 'w_a2_b': s[21].reshape(1, RANK, DK // NCHIP),
    }


def kernel(x, c, ctx, c_ctx, w_mod, b_mod, norm1_g, norm2_g, w_in, conv_w, conv_b, conv_ln_g, conv_ln_b, w_a2_f, b_a_f, w_a2_b, b_a_b, gla_norm_g, w_out, w_gate, w_up, w_down, final_g, loss_target, m_c_ctx, m_w_mod, m_b_mod, m_norm1_g, m_norm2_g, m_w_in, m_conv_w, m_conv_b, m_conv_ln_g, m_conv_ln_b, m_w_a2_f, m_b_a_f, m_w_a2_b, m_b_a_b, m_gla_norm_g, m_w_out, m_w_gate, m_w_up, m_w_down, m_final_g, v_c_ctx, v_w_mod, v_b_mod, v_norm1_g, v_norm2_g, v_w_in, v_conv_w, v_conv_b, v_conv_ln_g, v_conv_ln_b, v_w_a2_f, v_b_a_f, v_w_a2_b, v_b_a_b, v_gla_norm_g, v_w_out, v_w_gate, v_w_up, v_w_down, v_final_g):
    w = dict(c_ctx=c_ctx, w_mod=w_mod, b_mod=b_mod, norm1_g=norm1_g, norm2_g=norm2_g, w_in=w_in, conv_w=conv_w, conv_b=conv_b,
             conv_ln_g=conv_ln_g, conv_ln_b=conv_ln_b, w_a2_f=w_a2_f, b_a_f=b_a_f, w_a2_b=w_a2_b, b_a_b=b_a_b,
             gla_norm_g=gla_norm_g, w_out=w_out, w_gate=w_gate, w_up=w_up, w_down=w_down, final_g=final_g)
    m = dict(c_ctx=m_c_ctx, w_mod=m_w_mod, b_mod=m_b_mod, norm1_g=m_norm1_g, norm2_g=m_norm2_g, w_in=m_w_in, conv_w=m_conv_w,
             conv_b=m_conv_b, conv_ln_g=m_conv_ln_g, conv_ln_b=m_conv_ln_b, w_a2_f=m_w_a2_f, b_a_f=m_b_a_f, w_a2_b=m_w_a2_b,
             b_a_b=m_b_a_b, gla_norm_g=m_gla_norm_g, w_out=m_w_out, w_gate=m_w_gate, w_up=m_w_up, w_down=m_w_down,
             final_g=m_final_g)
    v = dict(c_ctx=v_c_ctx, w_mod=v_w_mod, b_mod=v_b_mod, norm1_g=v_norm1_g, norm2_g=v_norm2_g, w_in=v_w_in, conv_w=v_conv_w,
             conv_b=v_conv_b, conv_ln_g=v_conv_ln_g, conv_ln_b=v_conv_ln_b, w_a2_f=v_w_a2_f, b_a_f=v_b_a_f, w_a2_b=v_w_a2_b,
             b_a_b=v_b_a_b, gla_norm_g=v_gla_norm_g, w_out=v_w_out, w_gate=v_w_gate, w_up=v_w_up, w_down=v_w_down,
             final_g=v_final_g)
    mx, my, mc = _me()
    jme = 2 * mx + my
    me = 4 * mx + 2 * my + mc
    wmc = D * 6 // NCHIP
    xx, tgt, cx = x[0], loss_target[0], ctx[0]
    n = xx.shape[0]

    c8 = all_gather8(_rows(c[0]), "gather_c")[:, 0, :]
    cext = jnp.concatenate([c8, _rows(c_ctx)], axis=0)
    mloc = mod_fwd(cext, w_mod[0], lax.dynamic_slice_in_dim(b_mod, jme * wmc, wmc, axis=1))
    mall = all_gather8(mloc, "gather_mod")
    mall = jnp.concatenate([mall[2 * j] for j in range(NCHIP)], axis=1)
    sh1, sc1, g1, sh2, sc2, g2 = jnp.split(lax.dynamic_slice_in_dim(mall, me, 1, axis=0)[0], 6)
    csh1, csc1 = mall[8, :D], mall[8, D:2 * D]

    gw = gather_weights([w[k][0].astype(BF) for k in BIG])
    cols = lambda a: jnp.transpose(a, (1, 0, 2)).reshape(a.shape[1], -1)
    win = jnp.pad(cols(gw[0]), ((0, 0), (0, DINP - DIN)))
    wout = gw[1].reshape(D, D)
    wg, wu = cols(gw[2]), cols(gw[3])
    wd = gw[4].reshape(DFF, D)

    sw = jnp.concatenate([jnp.pad(conv_w[0], ((0, 1), (0, 0))).reshape(1, -1), w_a2_f[0].reshape(1, -1),
                          w_a2_b[0].reshape(1, -1)], axis=1)
    sw8 = all_gather8(jnp.pad(sw.reshape(6, D), ((0, 2), (0, 0))), "gather_small_w")
    swc = jnp.stack([sw8[2 * j] for j in range(NCHIP)]).reshape(NCHIP, 8 * D)
    convw = jnp.transpose(swc[:, :32 * 128].reshape(NCHIP, 32, 128), (1, 0, 2)).reshape(32, DC)
    a2 = lambda o: jnp.transpose(swc[:, o:o + RANK * 64].reshape(NCHIP, RANK, 64), (1, 0, 2)).reshape(RANK, DK)
    wa2 = jnp.zeros((128, 2 * DK), F32).at[0:RANK, 0:DK].set(a2(32 * 128)).at[RANK:2 * RANK, DK:].set(a2(32 * 128 + RANK * 64))
    wa2 = wa2.astype(BF)
    ba = jnp.concatenate([b_a_f, b_a_b], axis=1)
    cvec = _rows(conv_b, conv_ln_g, conv_ln_b)
    vec1 = _rows(norm1_g, sh1, sc1)
    vecc = _rows(norm1_g, csh1, csc1)
    vecm = _rows(g1)
    vecf = _rows(norm2_g, sh2, sc2, g2, final_g)
    gn = jnp.tile(gla_norm_g, (1, NH))

    s0 = ctx_fwd(cx, vecc, win, wa2, ba)
    ag, yb, co, qk, vv, gg, la, r = fwd_in(xx, vec1, win, convw, cvec, wa2, ba, TM_IN)
    o_f, o_b, se_f, se_b = gla_fwd(qk, vv, la, s0, TM_GLA)
    x1, y1, cat = merge_fwd(xx, o_f, o_b, gg, co, vecm, gn, wout, TM_MERGE)

    dx1, h2, act, dgt, dup, dy2, sf = ffn_fwd_bwd(x1, tgt, vecf, wg, wu, wd, TM_FFN)
    zero = lambda *s: jnp.zeros(s, F32)
    d_wg = wgrad(h2, dgt, zero(D, DFF), D, DFF // 2, TN_WGRAD, "wgrad_gate")
    d_wu = wgrad(h2, dup, zero(D, DFF), D, DFF // 2, TN_WGRAD, "wgrad_up")
    d_wd = wgrad(act, dy2, zero(DFF, D), DFF // 2, D, TN_WGRAD, "wgrad_down")
    dy1, dco, do, dg, s1, s2 = merge_bwd(dx1, y1, o_f, o_b, gg, vecm, gn, wout, TM_MERGE)
    d_wout = wgrad(cat, dy1, zero(D, D), D, D, TN_WGRAD, "wgrad_out")
    dqk_f, dv_f, dla_f, dqk_b, dv_b, dla_b, ds0 = gla_bwd(qk, vv, la, do, se_f, se_b, TM_GLA)
    dwin_c, dwa2_c, sc = ctx_bwd(cx, vecc, win, wa2, ba, ds0)
    grad_x, h, dp, dwa2, dcw, sd = bwd_in(xx, dx1, ag, yb, dco, dqk_f, dqk_b, dv_f, dv_b, dg, dla_f, dla_b, la, r,
                                          vec1, win, convw, cvec, wa2, TM_IN)
    d_win = wgrad(h, dp, dwin_c, D, DINP // 3, TN_WGRAD, "wgrad_in")

    rows16, dcw_t, dwa2_t = pack_small(sf, s1, s2, sd, sc, dcw, dwa2, dwa2_c)
    sp = jnp.concatenate([rows16, dcw_t.reshape(16, D), dwa2_t.reshape(16, D)], axis=0)
    g8 = all_gather8(sp, "gather_small_grads")
    tot, bm_g, loss8 = small_totals(g8)
    loss = loss8[0, 0]
    dmod8 = g8[:, 0:6, :].reshape(NDEV, 6 * D)
    dmodc = jnp.concatenate([tot[6], tot[7], jnp.zeros((4 * D,), F32)])
    dm = jnp.concatenate([dmod8, _rows(dmodc)], axis=0)
    dm = lax.dynamic_slice_in_dim(dm, jme * wmc, wmc, axis=1)
    g_wmod, dsil = mod_bwd(cext, dm, w_mod[0])
    p8 = all_gather8(dsil[8:16], "gather_dsilu")
    g_cctx = cctx_grad(p8, _rows(c_ctx))[0]

    shard = lambda a, k: jnp.transpose(a.reshape(a.shape[0], NCHIP, k), (1, 0, 2))
    full = [shard(d_win[:, :DIN], DIN // NCHIP), d_wout.reshape(NCHIP, D // NCHIP, D), shard(d_wg, DFF // NCHIP),
            shard(d_wu, DFF // NCHIP), d_wd.reshape(NCHIP, DFF // NCHIP, D)]
    halves = rs_sibling(full)
    nb = len(BIG)
    parts = [add_pairs(halves[k], halves[nb + k], halves[k].shape[1] // 4, "rs_add_" + BIG[k]) for k in range(nb)]
    recv = rs_chips(parts)
    sums = [sum_leading(recv[k], recv[k].shape[1] // 4, "rs_sum_" + BIG[k]) for k in range(nb)]
    gbig = share_halves(sums)

    grads = {k: gbig[i][None] for i, k in enumerate(BIG)}
    grads['w_mod'] = g_wmod[None]
    small_g = {
        'c_ctx': g_cctx, 'b_mod': bm_g[0:6].reshape(1, 6 * D), 'norm1_g': tot[8:9], 'norm2_g': tot[9:10], 'final_g': tot[10],
        'conv_b': tot[11:12, :DC], 'conv_ln_g': tot[11:12, DC:], 'conv_ln_b': tot[12:13, :DC],
        'b_a_f': tot[12:13, DC:DC + DK], 'b_a_b': tot[12:13, DC + DK:], 'gla_norm_g': tot[13:14, :HV],
        'conv_w': lax.dynamic_slice_in_dim(tot[16:32].reshape(32, DC)[:CW], jme * (DC // NCHIP), DC // NCHIP, axis=1)[None],
        'w_a2_f': lax.dynamic_slice_in_dim(tot[32:48].reshape(32, 2 * DK)[0:RANK, 0:DK], jme * (DK // NCHIP), DK // NCHIP, axis=1)[None],
        'w_a2_b': lax.dynamic_slice_in_dim(tot[32:48].reshape(32, 2 * DK)[RANK:2 * RANK, DK:], jme * (DK // NCHIP), DK // NCHIP, axis=1)[None],
    }
    grads.update(small_g)
    delta, new_m, new_v = {}, {}, {}
    for k in BIG + ['w_mod']:
        rws = 128 if w[k].shape[1] % 128 == 0 else 88
        d_, m_, v_ = adamw(w[k][0], grads[k][0], m[k][0], v[k][0], rws, "adamw_" + k)
        delta[k], new_m[k], new_v[k] = d_[None], m_[None], v_[None]
    sd_, sm_, sv_ = adamw(_small_slab(w), _small_slab(small_g), _small_slab(m), _small_slab(v), 24,
                          "adamw_small")
    for dst, slab in ((delta, sd_), (new_m, sm_), (new_v, sv_)):
        dst.update(_unslab(slab))
    out = [loss, grad_x[None]]
    for group in (grads, delta, new_m, new_v):
        out += [group[k].reshape(w[k].shape) for k in WEIGHTS]
    return tuple(out)
```

```python
import functools

import jax
import jax.numpy as jnp
from jax import lax
from jax.experimental import pallas as pl
from jax.experimental.pallas import tpu as pltpu

F32 = jnp.float32
BF = jnp.bfloat16

D = 1024
DC = 512
NH = 4
HK = 64
HV = 128
DK = NH * HK
DV = NH * HV
RANK = 16
CH = 64
GW = 64
CW = 31
CPAD = CW // 2
SEGP = GW + 32
DFF = 2816
DIN = 2592
DINP = 2688
EPS = 1e-6
TAU = 16.0
QSCALE = HK ** -0.5
NCHIP = 4
NDEV = 8

ADAM_LR = 0.001
ADAM_B1 = 0.9
ADAM_B2 = 0.999
ADAM_EPS = 1e-08
ADAM_WD = 0.01
ADAM_STEP = 10

VMEM_LIMIT = 56 * 1024 * 1024
MESH = pl.DeviceIdType.MESH


def _dot(a, b):
    return jnp.dot(a, b, preferred_element_type=F32)


def _dot_nt(a, b):
    return lax.dot_general(a, b, (((1,), (1,)), ((), ())), preferred_element_type=F32)


def _dot_tn(a, b):
    return lax.dot_general(a, b, (((0,), (0,)), ((), ())), preferred_element_type=F32)


def _split3(x):
    hi = x.astype(BF)
    r1 = x - hi.astype(F32)
    mid = r1.astype(BF)
    lo = (r1 - mid.astype(F32)).astype(BF)
    return hi, mid, lo


def _mask_dot(t, x):
    hi, mid, lo = _split3(x)
    return _dot(t, hi) + _dot(t, mid) + _dot(t, lo)


def _sigmoid(x):
    return 1.0 / (1.0 + jnp.exp(-x))


def _log_sigmoid(x):
    return jnp.minimum(x, 0.0) - jnp.log(1.0 + jnp.exp(-jnp.abs(x)))


def _colsum8(z):
    t, c = z.shape
    return jnp.sum(z.reshape(t // 8, 8, c), axis=0)


def _tri(n, kind):
    r = lax.broadcasted_iota(jnp.int32, (n, n), 0)
    c = lax.broadcasted_iota(jnp.int32, (n, n), 1)
    m = {"le": c <= r, "lt": c < r, "ge": c >= r, "gt": c > r}[kind]
    return m


def _full(shape):
    nd = len(shape)
    return pl.BlockSpec(shape, lambda *_: (0,) * nd)


def _cparams(sem, vmem=VMEM_LIMIT):
    return pltpu.CompilerParams(dimension_semantics=sem, vmem_limit_bytes=vmem)


def _conv_taps(pad_ref, s, w_ref, c0, cw, flip):
    acc = jnp.zeros((GW, cw), F32)
    for j in range(CW):
        wj = w_ref[pl.ds((CW - 1 - j) if flip else j, 1), pl.ds(c0, cw)]
        acc = acc + wj * pad_ref[s, pl.ds(j + 1, GW), pl.ds(c0, cw)]
    return acc


def _ln_stats(yb):
    mu = jnp.mean(yb, axis=-1, keepdims=True)
    yc = yb - mu
    var = jnp.mean(yc * yc, axis=-1, keepdims=True)
    rs = lax.rsqrt(var + EPS)
    return yc * rs, rs


def fwd_in(x, vec1, win, convw, cvec, wa2, ba, tm):
    n = x.shape[0]
    nseg = tm // GW
    half = DC // 2

    def body(x_ref, vec_ref, win_ref, cw_ref, cv_ref, wa2_ref, ba_ref,
             ag_ref, yb_ref, co_ref, qk_ref, v_ref, g_ref, la_ref, r_ref, pad_ref):
        xx = x_ref[...]
        rstd = lax.rsqrt(jnp.mean(xx * xx, axis=-1, keepdims=True) + EPS)
        h = (xx * rstd * vec_ref[0:1, :]) * (1.0 + vec_ref[2:3, :]) + vec_ref[1:2, :]
        p = _dot(h.astype(BF), win_ref[...])
        ag_ref[...] = p[:, :2 * DC].astype(BF)
        qk_ref[...] = p[:, 2 * DC:2 * DC + 2 * DK].astype(BF)
        v_ref[...] = p[:, 2 * DC + 2 * DK:2 * DC + 2 * DK + DV].astype(BF)
        g_ref[...] = p[:, 2 * DC + 2 * DK + DV:2 * DC + 2 * DK + 2 * DV].astype(BF)
        r = p[:, DINP - 128:].astype(BF)
        r_ref[...] = r
        la_ref[...] = _log_sigmoid(_dot(r, wa2_ref[...]) + ba_ref[...]) * (1.0 / TAU)

        vc = p[:, :DC] * _sigmoid(p[:, DC:2 * DC])
        zeros = jnp.zeros((nseg, 16, DC), F32)
        pad_ref[:, 0:16, :] = zeros
        pad_ref[:, 16 + GW:SEGP, :] = zeros
        pad_ref[:, 16:16 + GW, :] = vc.reshape(nseg, GW, DC)

        def seg(s, carry):
            for c0 in (0, half):
                y = _conv_taps(pad_ref, s, cw_ref, c0, half, False)
                yb_ref[pl.ds(pl.multiple_of(s * GW, GW), GW), pl.ds(c0, half)] = y + cv_ref[0:1, c0:c0 + half]
            return carry

        lax.fori_loop(0, nseg, seg, 0)
        yn, _ = _ln_stats(yb_ref[...])
        ln = yn * cv_ref[1:2, :] + cv_ref[2:3, :]
        co_ref[...] = (ln * _sigmoid(ln)).astype(BF)

    tok = lambda w: pl.BlockSpec((tm, w), lambda i: (i, 0))
    return pl.pallas_call(
        body, grid=(n // tm,), name="fwd_in",
        in_specs=[tok(D), _full(vec1.shape), _full(win.shape), _full(convw.shape), _full(cvec.shape),
                  _full(wa2.shape), _full(ba.shape)],
        out_specs=[tok(2 * DC), tok(DC), tok(DC), tok(2 * DK), tok(DV), tok(DV), tok(2 * DK), tok(128)],
        out_shape=[jax.ShapeDtypeStruct((n, 2 * DC), BF), jax.ShapeDtypeStruct((n, DC), F32),
                   jax.ShapeDtypeStruct((n, DC), BF), jax.ShapeDtypeStruct((n, 2 * DK), BF),
                   jax.ShapeDtypeStruct((n, DV), BF), jax.ShapeDtypeStruct((n, DV), BF),
                   jax.ShapeDtypeStruct((n, 2 * DK), F32), jax.ShapeDtypeStruct((n, 128), BF)],
        scratch_shapes=[pltpu.VMEM((nseg, SEGP, DC), F32)],
        compiler_params=_cparams(("arbitrary",)),
    )(x, vec1, win, convw, cvec, wa2, ba)


def _gla_dir(d):
    return (_tri(CH, "le"), CH - 1) if d == 0 else (_tri(CH, "ge"), 0)


def _gla_chunk_terms(qk, la, d):
    seen, last = _gla_dir(d)
    b = _mask_dot(seen.astype(BF), la)
    bl = b[last:last + 1, :]
    eb = jnp.exp(b)
    enb = jnp.exp(-b)
    ekd = jnp.exp(bl - b)
    ebl = jnp.exp(bl)
    q = qk[:, :DK].astype(F32) * QSCALE
    k = qk[:, DK:].astype(F32)
    return eb, enb, ekd, ebl, q * eb, k * enb, k * ekd


def gla_fwd(qk, v, la, s0, tm):
    n = qk.shape[0]
    nt = n // tm
    nc = tm // CH

    def body(qkf_ref, vf_ref, laf_ref, qkb_ref, vb_ref, lab_ref, s0_ref,
             of_ref, ob_ref, sef_ref, seb_ref, st_ref):
        @pl.when(pl.program_id(0) == 0)
        def _():
            st_ref[...] = s0_ref[...]

        def chunk(ci, carry):
            for d, (qk_ref, v_ref, la_ref, o_ref, se_ref) in enumerate(
                    ((qkf_ref, vf_ref, laf_ref, of_ref, sef_ref), (qkb_ref, vb_ref, lab_ref, ob_ref, seb_ref))):
                c = ci if d == 0 else nc - 1 - ci
                rows = pl.ds(pl.multiple_of(c * CH, CH), CH)
                amask, _ = _gla_dir(d)
                eb, enb, ekd, ebl, qt, kt, kd = _gla_chunk_terms(qk_ref[rows, :], la_ref[rows, :], d)
                qt = qt.astype(BF)
                kt = kt.astype(BF)
                kd = kd.astype(BF)
                vv = v_ref[rows, :]
                for h in range(NH):
                    ks = slice(h * HK, (h + 1) * HK)
                    vs = slice(h * HV, (h + 1) * HV)
                    st = st_ref[d, h]
                    se_ref[c, h] = st
                    a = jnp.where(amask, _dot_nt(qt[:, ks], kt[:, ks]), 0.0)
                    o_ref[rows, vs] = _dot(a.astype(BF), vv[:, vs]) + _dot_nt(qt[:, ks], st.astype(BF))
                    st_ref[d, h] = ebl[:, ks] * st + _dot_tn(vv[:, vs], kd[:, ks])
            return carry

        lax.fori_loop(0, nc, chunk, 0)

    fw = lambda w, col=0: pl.BlockSpec((tm, w), lambda i: (i, col))
    bw = lambda w, col=0: pl.BlockSpec((tm, w), lambda i: (nt - 1 - i, col))
    se_f = pl.BlockSpec((nc, NH, HV, HK), lambda i: (i, 0, 0, 0))
    se_b = pl.BlockSpec((nc, NH, HV, HK), lambda i: (nt - 1 - i, 0, 0, 0))
    se_shape = jax.ShapeDtypeStruct((n // CH, NH, HV, HK), F32)
    return pl.pallas_call(
        body, grid=(nt,), name="gla_fwd",
        in_specs=[fw(2 * DK), fw(DV), fw(DK, 0), bw(2 * DK), bw(DV), bw(DK, 1), _full(s0.shape)],
        out_specs=[fw(DV), bw(DV), se_f, se_b],
        out_shape=[jax.ShapeDtypeStruct((n, DV), F32), jax.ShapeDtypeStruct((n, DV), F32), se_shape, se_shape],
        scratch_shapes=[pltpu.VMEM((2, NH, HV, HK), F32)],
        compiler_params=_cparams(("arbitrary",)),
    )(qk, v, la, qk, v, la, s0)


def gla_bwd(qk, v, la, do, se_f, se_b, tm):
    n = qk.shape[0]
    nt = n // tm
    nc = tm // CH

    def body(qkf_ref, vf_ref, laf_ref, dof_ref, sef_ref, qkb_ref, vb_ref, lab_ref, dob_ref, seb_ref,
             dqkf_ref, dvf_ref, dlaf_ref, dqkb_ref, dvb_ref, dlab_ref, ds0_ref, ds_ref):
        @pl.when(pl.program_id(0) == 0)
        def _():
            ds_ref[...] = jnp.zeros_like(ds_ref)

        def chunk(ci, carry):
            for d, (qk_ref, v_ref, la_ref, do_ref, se_ref, dqk_ref, dv_ref, dla_ref) in enumerate(
                    ((qkf_ref, vf_ref, laf_ref, dof_ref, sef_ref, dqkf_ref, dvf_ref, dlaf_ref),
                     (qkb_ref, vb_ref, lab_ref, dob_ref, seb_ref, dqkb_ref, dvb_ref, dlab_ref))):
                c = nc - 1 - ci if d == 0 else ci
                rows = pl.ds(pl.multiple_of(c * CH, CH), CH)
                amask, last = _gla_dir(d)
                cum_t = _gla_dir(1 - d)[0]
                eb, enb, ekd, ebl, qt, kt, kd = _gla_chunk_terms(qk_ref[rows, :], la_ref[rows, :], d)
                qtb = qt.astype(BF)
                ktb = kt.astype(BF)
                kdb = kd.astype(BF)
                vv = v_ref[rows, :]
                dd = do_ref[rows, :]
                is_last = lax.broadcasted_iota(jnp.int32, (CH, HK), 0) == last
                dq_parts, dk_parts, db_parts = [], [], []
                for h in range(NH):
                    ks = slice(h * HK, (h + 1) * HK)
                    vs = slice(h * HV, (h + 1) * HV)
                    st = se_ref[c, h]
                    dsn = ds_ref[d, h]
                    dsnb = dsn.astype(BF)
                    a = jnp.where(amask, _dot_nt(qtb[:, ks], ktb[:, ks]), 0.0).astype(BF)
                    da = jnp.where(amask, _dot_nt(dd[:, vs], vv[:, vs]), 0.0).astype(BF)
                    dv_ref[rows, vs] = (_dot_tn(a, dd[:, vs]) + _dot_nt(kdb[:, ks], dsnb)).astype(BF)
                    dkd = _dot(vv[:, vs], dsnb)
                    dqt = _dot(da, ktb[:, ks]) + _dot(dd[:, vs], st.astype(BF))
                    dkt = _dot_tn(da, qtb[:, ks])
                    ds_ref[d, h] = _dot_tn(dd[:, vs], qtb[:, ks]) + ebl[:, ks] * dsn
                    debl = jnp.sum(st * dsn, axis=0, keepdims=True)
                    dq_parts.append(dqt * eb[:, ks] * QSCALE)
                    dk_parts.append(dkt * enb[:, ks] + dkd * ekd[:, ks])
                    dkdkd = dkd * kd[:, ks]
                    dbl = jnp.sum(dkdkd, axis=0, keepdims=True) + debl * ebl[:, ks]
                    db_parts.append(dqt * qt[:, ks] - dkt * kt[:, ks] - dkdkd + jnp.where(is_last, dbl, 0.0))
                dqk_ref[rows, :] = jnp.concatenate(dq_parts + dk_parts, axis=1).astype(BF)
                dla_ref[rows, :] = _mask_dot(cum_t.astype(BF), jnp.concatenate(db_parts, axis=1))
            return carry

        lax.fori_loop(0, nc, chunk, 0)

        @pl.when(pl.program_id(0) == nt - 1)
        def _():
            ds0_ref[...] = ds_ref[...]

    up = lambda w, col=0: pl.BlockSpec((tm, w), lambda i: (i, col))
    dn = lambda w, col=0: pl.BlockSpec((tm, w), lambda i: (nt - 1 - i, col))
    se_up = pl.BlockSpec((nc, NH, HV, HK), lambda i: (i, 0, 0, 0))
    se_dn = pl.BlockSpec((nc, NH, HV, HK), lambda i: (nt - 1 - i, 0, 0, 0))
    return pl.pallas_call(
        body, grid=(nt,), name="gla_bwd",
        in_specs=[dn(2 * DK), dn(DV), dn(DK, 0), dn(DV), se_dn, up(2 * DK), up(DV), up(DK, 1), up(DV), se_up],
        out_specs=[dn(2 * DK), dn(DV), dn(DK), up(2 * DK), up(DV), up(DK), _full((2, NH, HV, HK))],
        out_shape=[jax.ShapeDtypeStruct((n, 2 * DK), BF), jax.ShapeDtypeStruct((n, DV), BF),
                   jax.ShapeDtypeStruct((n, DK), F32), jax.ShapeDtypeStruct((n, 2 * DK), BF),
                   jax.ShapeDtypeStruct((n, DV), BF), jax.ShapeDtypeStruct((n, DK), F32),
                   jax.ShapeDtypeStruct((2, NH, HV, HK), F32)],
        scratch_shapes=[pltpu.VMEM((2, NH, HV, HK), F32)],
        compiler_params=_cparams(("arbitrary",)),
    )(qk, v, la, do, se_f, qk, v, la, do, se_b)


def _head_norm(o):
    ons, rss = [], []
    for h in range(NH):
        oh = o[:, h * HV:(h + 1) * HV]
        rs = lax.rsqrt(jnp.mean(oh * oh, axis=-1, keepdims=True) + EPS)
        ons.append(oh * rs)
        rss.append(rs)
    return ons, rss


def merge_fwd(x, o_f, o_b, g, co, vecm, gn, wout, tm):
    n = x.shape[0]

    def body(x_ref, of_ref, ob_ref, g_ref, co_ref, vec_ref, gn_ref, w_ref, x1_ref, y1_ref, cat_ref):
        o = of_ref[...] + ob_ref[...]
        ons, _ = _head_norm(o)
        gg = g_ref[...].astype(F32)
        sil = gg * _sigmoid(gg)
        cat_ref[:, :DC] = co_ref[...]
        for h in range(NH):
            vs = slice(h * HV, (h + 1) * HV)
            cat_ref[:, DC + h * HV:DC + (h + 1) * HV] = (ons[h] * gn_ref[:, vs] * sil[:, vs]).astype(BF)
        y1 = _dot(cat_ref[...], w_ref[...])
        y1_ref[...] = y1.astype(BF)
        x1_ref[...] = x_ref[...] + vec_ref[0:1, :] * y1

    tok = lambda w: pl.BlockSpec((tm, w), lambda i: (i, 0))
    return pl.pallas_call(
        body, grid=(n // tm,), name="merge_fwd",
        in_specs=[tok(D), tok(DV), tok(DV), tok(DV), tok(DC), _full(vecm.shape), _full(gn.shape), _full(wout.shape)],
        out_specs=[tok(D), tok(D), tok(D)],
        out_shape=[jax.ShapeDtypeStruct((n, D), F32), jax.ShapeDtypeStruct((n, D), BF), jax.ShapeDtypeStruct((n, D), BF)],
        compiler_params=_cparams(("arbitrary",)),
    )(x, o_f, o_b, g, co, vecm, gn, wout)


def merge_bwd(dx1, y1, o_f, o_b, g, vecm, gn, wout, tm):
    n = dx1.shape[0]

    def body(dx1_ref, y1_ref, of_ref, ob_ref, g_ref, vec_ref, gn_ref, w_ref,
             dy1_ref, dco_ref, do_ref, dg_ref, s1_ref, s2_ref):
        @pl.when(pl.program_id(0) == 0)
        def _():
            s1_ref[...] = jnp.zeros_like(s1_ref)
            s2_ref[...] = jnp.zeros_like(s2_ref)

        dx1 = dx1_ref[...]
        s1_ref[...] += _colsum8(dx1 * y1_ref[...].astype(F32))
        dy1 = (dx1 * vec_ref[0:1, :]).astype(BF)
        dy1_ref[...] = dy1
        dcat = _dot_nt(dy1, w_ref[...])
        dco_ref[...] = dcat[:, :DC].astype(BF)
        o = of_ref[...] + ob_ref[...]
        ons, rss = _head_norm(o)
        gg = g_ref[...].astype(F32)
        sg = _sigmoid(gg)
        sil = gg * sg
        dsil = sg * (1.0 + gg * (1.0 - sg))
        for h in range(NH):
            vs = slice(h * HV, (h + 1) * HV)
            do2 = dcat[:, DC + h * HV:DC + (h + 1) * HV]
            gnh = gn_ref[:, vs]
            t = do2 * sil[:, vs]
            s2_ref[:, vs] += _colsum8(t * ons[h])
            don = t * gnh
            do_ref[:, vs] = (rss[h] * (don - ons[h] * jnp.mean(don * ons[h], axis=-1, keepdims=True))).astype(BF)
            dg_ref[:, vs] = (do2 * ons[h] * gnh * dsil[:, vs]).astype(BF)

    tok = lambda w: pl.BlockSpec((tm, w), lambda i: (i, 0))
    return pl.pallas_call(
        body, grid=(n // tm,), name="merge_bwd",
        in_specs=[tok(D), tok(D), tok(DV), tok(DV), tok(DV), _full(vecm.shape), _full(gn.shape), _full(wout.shape)],
        out_specs=[tok(D), tok(DC), tok(DV), tok(DV), _full((8, D)), _full((8, DV))],
        out_shape=[jax.ShapeDtypeStruct((n, D), BF), jax.ShapeDtypeStruct((n, DC), BF), jax.ShapeDtypeStruct((n, DV), BF),
                   jax.ShapeDtypeStruct((n, DV), BF), jax.ShapeDtypeStruct((8, D), F32), jax.ShapeDtypeStruct((8, DV), F32)],
        compiler_params=_cparams(("arbitrary",)),
    )(dx1, y1, o_f, o_b, g, vecm, gn, wout)


def ffn_fwd_bwd(x1, tgt, vecf, wg, wu, wd, tm):
    n = x1.shape[0]

    def body(x1_ref, t_ref, vec_ref, wg_ref, wu_ref, wd_ref,
             dx1_ref, h2_ref, act_ref, dgt_ref, dup_ref, dy2_ref, s_ref):
        @pl.when(pl.program_id(0) == 0)
        def _():
            s_ref[...] = jnp.zeros_like(s_ref)

        n2g, sh2, sc2, g2, fg = (vec_ref[i:i + 1, :] for i in range(5))
        x1 = x1_ref[...]
        r2 = lax.rsqrt(jnp.mean(x1 * x1, axis=-1, keepdims=True) + EPS)
        xn2 = x1 * r2
        h2 = (xn2 * n2g * (1.0 + sc2) + sh2).astype(BF)
        h2_ref[...] = h2
        gt = _dot(h2, wg_ref[...])
        up = _dot(h2, wu_ref[...])
        sg = _sigmoid(gt)
        sil = gt * sg
        act = (sil * up).astype(BF)
        act_ref[...] = act
        y2 = _dot(act, wd_ref[...])
        x2 = x1 + g2 * y2
        r3 = lax.rsqrt(jnp.mean(x2 * x2, axis=-1, keepdims=True) + EPS)
        xn3 = x2 * r3
        e = xn3 * fg - t_ref[...]
        s_ref[40:48, :] += _colsum8(e * e) * (0.5 / D)
        dyo = e * (1.0 / D)
        s_ref[0:8, :] += _colsum8(dyo * xn3)
        dxn3 = dyo * fg
        dx2 = r3 * (dxn3 - xn3 * jnp.mean(dxn3 * xn3, axis=-1, keepdims=True))
        s_ref[8:16, :] += _colsum8(dx2 * y2)
        dy2 = (dx2 * g2).astype(BF)
        dy2_ref[...] = dy2
        dact = _dot_nt(dy2, wd_ref[...])
        dup = (dact * sil).astype(BF)
        dgt = (dact * up * (sg * (1.0 + gt * (1.0 - sg)))).astype(BF)
        dup_ref[...] = dup
        dgt_ref[...] = dgt
        dh2 = _dot_nt(dgt, wg_ref[...]) + _dot_nt(dup, wu_ref[...])
        s_ref[16:24, :] += _colsum8(dh2)
        t = dh2 * xn2
        s_ref[24:32, :] += _colsum8(t * n2g)
        s_ref[32:40, :] += _colsum8(t * (1.0 + sc2))
        dxn2 = dh2 * ((1.0 + sc2) * n2g)
        dx1_ref[...] = dx2 + r2 * (dxn2 - xn2 * jnp.mean(dxn2 * xn2, axis=-1, keepdims=True))

    tok = lambda w: pl.BlockSpec((tm, w), lambda i: (i, 0))
    wspec = lambda a: pl.BlockSpec(a.shape, lambda i: (0, 0), pipeline_mode=pl.Buffered(1))
    return pl.pallas_call(
        body, grid=(n // tm,), name="ffn_fwd_bwd",
        in_specs=[tok(D), tok(D), _full(vecf.shape), wspec(wg), wspec(wu), wspec(wd)],
        out_specs=[tok(D), tok(D), tok(DFF), tok(DFF), tok(DFF), tok(D), _full((48, D))],
        out_shape=[jax.ShapeDtypeStruct((n, D), F32), jax.ShapeDtypeStruct((n, D), BF), jax.ShapeDtypeStruct((n, DFF), BF),
                   jax.ShapeDtypeStruct((n, DFF), BF), jax.ShapeDtypeStruct((n, DFF), BF), jax.ShapeDtypeStruct((n, D), BF),
                   jax.ShapeDtypeStruct((48, D), F32)],
        compiler_params=_cparams(("arbitrary",)),
    )(x1, tgt, vecf, wg, wu, wd)


def wgrad(a, b, init, t1, t2, tn, name):
    n, k1 = a.shape
    k2 = b.shape[1]

    def body(a_ref, b_ref, i_ref, o_ref):
        @pl.when(pl.program_id(2) == 0)
        def _():
            o_ref[...] = i_ref[...]

        o_ref[...] += _dot_tn(a_ref[...], b_ref[...])

    return pl.pallas_call(
        body, grid=(k1 // t1, k2 // t2, n // tn), name=name,
        in_specs=[pl.BlockSpec((tn, t1), lambda i, j, k: (k, i)), pl.BlockSpec((tn, t2), lambda i, j, k: (k, j)),
                  pl.BlockSpec((t1, t2), lambda i, j, k: (i, j))],
        out_specs=pl.BlockSpec((t1, t2), lambda i, j, k: (i, j)),
        out_shape=jax.ShapeDtypeStruct((k1, k2), F32),
        input_output_aliases={2: 0},
        compiler_params=_cparams(("parallel", "parallel", "arbitrary")),
    )(a, b, init)


def bwd_in(x, dx1, ag, yb, dco, dqk_f, dqk_b, dv_f, dv_b, dg, dla_f, dla_b, la, r, vec1, win, convw, cvec, wa2, tm):
    n = x.shape[0]
    nseg = tm // GW
    half = DC // 2

    def body(x_ref, dx1_ref, ag_ref, yb_ref, dco_ref, dqkf_ref, dqkb_ref, dvf_ref, dvb_ref, dg_ref, dlaf_ref, dlab_ref,
             la_ref, r_ref, vec_ref, win_ref, cw_ref, cv_ref, wa2_ref,
             gx_ref, h_ref, dp_ref, dwa2_ref, dcw_ref, s_ref, pad1_ref, pad2_ref, dvc_ref, dcw8_ref):
        first = pl.program_id(0) == 0

        @pl.when(first)
        def _():
            s_ref[...] = jnp.zeros_like(s_ref)
            dwa2_ref[...] = jnp.zeros_like(dwa2_ref)
            dcw8_ref[...] = jnp.zeros_like(dcw8_ref)

        yn, rs = _ln_stats(yb_ref[...])
        lng = cv_ref[1:2, :]
        ln = yn * lng + cv_ref[2:3, :]
        sgl = _sigmoid(ln)
        dln = dco_ref[...].astype(F32) * (sgl * (1.0 + ln * (1.0 - sgl)))
        dyn = dln * lng
        dyb = rs * (dyn - jnp.mean(dyn, axis=-1, keepdims=True) - yn * jnp.mean(dyn * yn, axis=-1, keepdims=True))
        s_ref[24:32, 0:DC] += _colsum8(dyb)
        s_ref[24:32, DC:D] += _colsum8(dln * yn)
        s_ref[32:40, 0:DC] += _colsum8(dln)

        agv = ag_ref[...].astype(F32)
        a = agv[:, :DC]
        sgg = _sigmoid(agv[:, DC:])
        zeros = jnp.zeros((nseg, 16, DC), F32)
        for pr, val in ((pad1_ref, a * sgg), (pad2_ref, dyb)):
            pr[:, 0:16, :] = zeros
            pr[:, 16 + GW:SEGP, :] = zeros
            pr[:, 16:16 + GW, :] = val.reshape(nseg, GW, DC)

        def seg(s, carry):
            rows = pl.ds(pl.multiple_of(s * GW, GW), GW)
            for c0 in (0, half):
                cs = pl.ds(c0, half)
                dvc_ref[rows, cs] = _conv_taps(pad2_ref, s, cw_ref, c0, half, True)
                dys = pad2_ref[s, pl.ds(16, GW), cs]
                for j in range(CW):
                    dcw8_ref[j, :, cs] += _colsum8(dys * pad1_ref[s, pl.ds(j + 1, GW), cs])
            return carry

        lax.fori_loop(0, nseg, seg, 0)
        dvc = dvc_ref[...]
        dp_ref[:, 0:DC] = (dvc * sgg).astype(BF)
        dp_ref[:, DC:2 * DC] = (dvc * a * sgg * (1.0 - sgg)).astype(BF)

        dp_ref[:, 2 * DC:2 * DC + 2 * DK] = (dqkf_ref[...].astype(F32) + dqkb_ref[...].astype(F32)).astype(BF)
        dp_ref[:, 2 * DC + 2 * DK:2 * DC + 2 * DK + DV] = (dvf_ref[...].astype(F32) + dvb_ref[...].astype(F32)).astype(BF)
        dp_ref[:, 2 * DC + 2 * DK + DV:2 * DC + 2 * DK + 2 * DV] = dg_ref[...]

        la = la_ref[...]
        dla = jnp.concatenate([dlaf_ref[...], dlab_ref[...]], axis=1)
        dpre = dla * (1.0 - jnp.exp(TAU * la)) * (1.0 / TAU)
        s_ref[32:40, DC:D] += _colsum8(dpre)
        dpreb = dpre.astype(BF)
        dwa2_ref[...] += _dot_tn(r_ref[...], dpreb)
        dp_ref[:, DINP - 128:] = _dot_nt(dpreb, wa2_ref[...]).astype(BF)

        dh = _dot_nt(dp_ref[...], win_ref[...])
        xx = x_ref[...]
        n1g, sh1, sc1 = vec_ref[0:1, :], vec_ref[1:2, :], vec_ref[2:3, :]
        rstd = lax.rsqrt(jnp.mean(xx * xx, axis=-1, keepdims=True) + EPS)
        xn = xx * rstd
        h_ref[...] = (xn * n1g * (1.0 + sc1) + sh1).astype(BF)
        s_ref[0:8, :] += _colsum8(dh)
        t = dh * xn
        s_ref[8:16, :] += _colsum8(t * n1g)
        s_ref[16:24, :] += _colsum8(t * (1.0 + sc1))
        dxn = dh * ((1.0 + sc1) * n1g)
        gx_ref[...] = dx1_ref[...] + rstd * (dxn - xn * jnp.mean(dxn * xn, axis=-1, keepdims=True))

        @pl.when(pl.program_id(0) == pl.num_programs(0) - 1)
        def _():
            dcw_ref[...] = jnp.sum(dcw8_ref[...], axis=1)

    tok = lambda w: pl.BlockSpec((tm, w), lambda i: (i, 0))
    return pl.pallas_call(
        body, grid=(n // tm,), name="bwd_in",
        in_specs=[tok(D), tok(D), tok(2 * DC), tok(DC), tok(DC), tok(2 * DK), tok(2 * DK), tok(DV), tok(DV), tok(DV),
                  tok(DK), tok(DK), tok(2 * DK), tok(128), _full(vec1.shape),
                  pl.BlockSpec(win.shape, lambda i: (0, 0), pipeline_mode=pl.Buffered(1)),
                  _full(convw.shape), _full(cvec.shape), _full(wa2.shape)],
        out_specs=[tok(D), tok(D), tok(DINP), _full((128, 2 * DK)), _full((32, DC)), _full((40, D))],
        out_shape=[jax.ShapeDtypeStruct((n, D), F32), jax.ShapeDtypeStruct((n, D), BF), jax.ShapeDtypeStruct((n, DINP), BF),
                   jax.ShapeDtypeStruct((128, 2 * DK), F32), jax.ShapeDtypeStruct((32, DC), F32),
                   jax.ShapeDtypeStruct((40, D), F32)],
        scratch_shapes=[pltpu.VMEM((nseg, SEGP, DC), F32), pltpu.VMEM((nseg, SEGP, DC), F32), pltpu.VMEM((tm, DC), F32),
                        pltpu.VMEM((32, 8, DC), F32)],
        compiler_params=_cparams(("arbitrary",)),
    )(x, dx1, ag, yb, dco, dqk_f, dqk_b, dv_f, dv_b, dg, dla_f, dla_b, la, r, vec1, win, convw, cvec, wa2)


def _ctx_common(ctx_ref, vec_ref, win_ref, wa2_ref, ba_ref):
    cx = ctx_ref[...]
    t = cx.shape[0]
    rstd = lax.rsqrt(jnp.mean(cx * cx, axis=-1, keepdims=True) + EPS)
    xn = cx * rstd
    hc = (xn * vec_ref[0:1, :] * (1.0 + vec_ref[2:3, :]) + vec_ref[1:2, :]).astype(BF)
    k0 = 2 * DC + DK
    kv = _dot(hc, win_ref[:, k0:k0 + DK + DV]).astype(BF).astype(F32)
    r = _dot(hc, win_ref[:, DINP - 128:]).astype(BF)
    la = _log_sigmoid(_dot(r, wa2_ref[...]) + ba_ref[...]) * (1.0 / TAU)
    incl = _tri(t, "le").astype(BF)
    strict = _tri(t, "lt").astype(BF)
    bf = _mask_dot(incl, la[:, :DK])
    wf = jnp.exp(bf[t - 1:t, :] - bf)
    wb = jnp.exp(_mask_dot(strict, la[:, DK:]))
    return xn, hc, kv[:, :DK], kv[:, DK:], r, la, wf, wb


def ctx_fwd(ctx, vecc, win, wa2, ba):
    def body(ctx_ref, vec_ref, win_ref, wa2_ref, ba_ref, s_ref):
        _, _, k, v, _, _, wf, wb = _ctx_common(ctx_ref, vec_ref, win_ref, wa2_ref, ba_ref)
        vb = v.astype(BF)
        for d, w in enumerate((wf, wb)):
            kd = (k * w).astype(BF)
            for h in range(NH):
                s_ref[d, h] = _dot_tn(vb[:, h * HV:(h + 1) * HV], kd[:, h * HK:(h + 1) * HK])

    return pl.pallas_call(
        body, name="ctx_fwd", out_shape=jax.ShapeDtypeStruct((2, NH, HV, HK), F32),
        compiler_params=pltpu.CompilerParams(vmem_limit_bytes=VMEM_LIMIT),
    )(ctx, vecc, win, wa2, ba)


def ctx_bwd(ctx, vecc, win, wa2, ba, ds0):
    t = ctx.shape[0]

    def body(ctx_ref, vec_ref, win_ref, wa2_ref, ba_ref, ds_ref, dwin_ref, dwa2_ref, s_ref, dpc_ref):
        xn, hc, k, v, r, la, wf, wb = _ctx_common(ctx_ref, vec_ref, win_ref, wa2_ref, ba_ref)
        vb = v.astype(BF)
        strict = _tri(t, "lt").astype(BF)
        strict_t = _tri(t, "gt").astype(BF)
        dpc_ref[...] = jnp.zeros_like(dpc_ref)
        k0 = 2 * DC + DK
        dk = jnp.zeros((t, DK), F32)
        des = []
        for d, w in enumerate((wf, wb)):
            kd = (k * w).astype(BF)
            dkds = []
            for h in range(NH):
                dsb = ds_ref[d, h].astype(BF)
                dkds.append(_dot(vb[:, h * HV:(h + 1) * HV], dsb))
                dvh = _dot_nt(kd[:, h * HK:(h + 1) * HK], dsb)
                vs = slice(k0 + DK + h * HV, k0 + DK + (h + 1) * HV)
                if d == 0:
                    dpc_ref[:, vs] = dvh.astype(BF)
                else:
                    dpc_ref[:, vs] = (dpc_ref[:, vs].astype(F32) + dvh).astype(BF)
            dkd = jnp.concatenate(dkds, axis=1)
            dk = dk + dkd * w
            des.append(dkd * k * w)
        dpc_ref[:, k0:k0 + DK] = dk.astype(BF)
        dla = jnp.concatenate([_mask_dot(strict, des[0]), _mask_dot(strict_t, des[1])], axis=1)
        dpre = dla * (1.0 - jnp.exp(TAU * la)) * (1.0 / TAU)
        dpreb = dpre.astype(BF)
        dwa2_ref[...] = _dot_tn(r, dpreb)
        dpc_ref[:, DINP - 128:] = _dot_nt(dpreb, wa2_ref[...]).astype(BF)
        dpc = dpc_ref[...]
        dwin_ref[...] = _dot_tn(hc, dpc)
        dhc = _dot_nt(dpc, win_ref[...])
        n1g, sc1 = vec_ref[0:1, :], vec_ref[2:3, :]
        tt = dhc * xn
        s_ref[...] = jnp.zeros_like(s_ref)
        s_ref[0:1, :] = jnp.sum(tt * (1.0 + sc1), axis=0, keepdims=True)
        s_ref[1:2, :] = jnp.sum(dhc, axis=0, keepdims=True)
        s_ref[2:3, :] = jnp.sum(tt * n1g, axis=0, keepdims=True)
        s_ref[3:4, DC:D] = jnp.sum(dpre, axis=0, keepdims=True)

    return pl.pallas_call(
        body, name="ctx_bwd",
        out_shape=[jax.ShapeDtypeStruct((D, DINP), F32), jax.ShapeDtypeStruct((128, 2 * DK), F32),
                   jax.ShapeDtypeStruct((8, D), F32)],
        scratch_shapes=[pltpu.VMEM((t, DINP), BF)],
        compiler_params=pltpu.CompilerParams(vmem_limit_bytes=VMEM_LIMIT),
    )(ctx, vecc, win, wa2, ba, ds0)


def _silu(x):
    return x * _sigmoid(x)


def mod_fwd(cext, wm, bm):
    def body(c_ref, w_ref, b_ref, o_ref):
        o_ref[...] = _dot(_silu(c_ref[...]).astype(BF), w_ref[...].astype(BF)) + b_ref[...]

    return pl.pallas_call(body, name="mod_fwd", out_shape=jax.ShapeDtypeStruct((cext.shape[0], wm.shape[1]), F32),
                          compiler_params=pltpu.CompilerParams(vmem_limit_bytes=VMEM_LIMIT))(cext, wm, bm)


def mod_bwd(cext, dm, wm):
    def body(c_ref, d_ref, w_ref, gw_ref, ds_ref):
        dmb = d_ref[...].astype(BF)
        gw_ref[...] = _dot_tn(_silu(c_ref[...]).astype(BF), dmb)
        ds_ref[...] = _dot_nt(dmb, w_ref[...].astype(BF))

    return pl.pallas_call(body, name="mod_bwd",
                          out_shape=[jax.ShapeDtypeStruct(wm.shape, F32), jax.ShapeDtypeStruct(cext.shape, F32)],
                          compiler_params=pltpu.CompilerParams(vmem_limit_bytes=VMEM_LIMIT))(cext, dm, wm)


def pack_small(sf, s1, s2, sd, sc, dcw, dwa2, dwa2_c):
    def body(sf_ref, s1_ref, s2_ref, sd_ref, sc_ref, dcw_ref, dwa2_ref, dwa2c_ref, o_ref, ocw_ref, owa_ref):
        rsum = lambda ref, i: jnp.sum(ref[8 * i:8 * i + 8, :], axis=0, keepdims=True)
        o_ref[...] = jnp.zeros_like(o_ref)
        o_ref[0:1, :] = rsum(sd_ref, 0)
        o_ref[1:2, :] = rsum(sd_ref, 1)
        o_ref[2:3, :] = rsum(s1_ref, 0)
        o_ref[3:4, :] = rsum(sf_ref, 2)
        o_ref[4:5, :] = rsum(sf_ref, 3)
        o_ref[5:6, :] = rsum(sf_ref, 1)
        o_ref[6:7, :] = sc_ref[1:2, :]
        o_ref[7:8, :] = sc_ref[2:3, :]
        o_ref[8:9, :] = rsum(sd_ref, 2) + sc_ref[0:1, :]
        o_ref[9:10, :] = rsum(sf_ref, 4)
        o_ref[10:11, :] = rsum(sf_ref, 0)
        o_ref[11:12, :] = rsum(sd_ref, 3)
        o_ref[12:13, :] = rsum(sd_ref, 4) + sc_ref[3:4, :]
        g = jnp.sum(s2_ref[...], axis=0, keepdims=True)
        o_ref[13:14, 0:HV] = g[:, 0:HV] + g[:, HV:2 * HV] + g[:, 2 * HV:3 * HV] + g[:, 3 * HV:4 * HV]
        o_ref[14:15, :] = rsum(sf_ref, 5)
        ocw_ref[...] = dcw_ref[...]
        owa_ref[...] = dwa2_ref[0:32, :] + dwa2c_ref[0:32, :]

    return pl.pallas_call(body, name="pack_small",
                          out_shape=[jax.ShapeDtypeStruct((16, D), F32), jax.ShapeDtypeStruct((32, DC), F32),
                                     jax.ShapeDtypeStruct((32, 2 * DK), F32)])(sf, s1, s2, sd, sc, dcw, dwa2, dwa2_c)


def sum_leading(a, rows, name):
    k, r, c = a.shape

    def body(a_ref, o_ref):
        acc = a_ref[0]
        for i in range(1, k):
            acc = acc + a_ref[i]
        o_ref[...] = acc

    return pl.pallas_call(
        body, grid=(r // rows,), name=name,
        in_specs=[pl.BlockSpec((k, rows, c), lambda i: (0, i, 0))],
        out_specs=pl.BlockSpec((rows, c), lambda i: (i, 0)),
        out_shape=jax.ShapeDtypeStruct((r, c), F32),
        compiler_params=_cparams(("parallel",)),
    )(a)


def add_pairs(a, b, rows, name):
    k, r, c = a.shape

    def body(a_ref, b_ref, o_ref):
        o_ref[...] = a_ref[...] + b_ref[...]

    spec = pl.BlockSpec((1, rows, c), lambda i, j: (i, j, 0))
    return pl.pallas_call(
        body, grid=(k, r // rows), name=name, in_specs=[spec, spec], out_specs=spec,
        out_shape=jax.ShapeDtypeStruct(a.shape, F32), compiler_params=_cparams(("parallel", "parallel")),
    )(a, b)


def small_totals(g8):
    r = g8.shape[1]

    def body(g_ref, t_ref, bm_ref, loss_ref):
        acc = g_ref[0]
        for i in range(1, NDEV):
            acc = acc + g_ref[i]
        t_ref[...] = acc
        bm_ref[...] = jnp.zeros_like(bm_ref)
        bm_ref[0:6, :] = acc[0:6, :]
        bm_ref[0:2, :] += acc[6:8, :]
        loss_ref[...] = jnp.broadcast_to(jnp.sum(acc[14:15, :], axis=1, keepdims=True), loss_ref.shape)

    return pl.pallas_call(body, name="small_totals",
                          out_shape=[jax.ShapeDtypeStruct((r, D), F32), jax.ShapeDtypeStruct((8, D), F32),
                                     jax.ShapeDtypeStruct((8, 128), F32)])(g8)


def cctx_grad(p8, c_ctx_row):
    def body(p_ref, c_ref, o_ref):
        acc = p_ref[0, 0:1, :]
        for j in range(1, NCHIP):
            acc = acc + p_ref[2 * j, 0:1, :]
        cc = c_ref[0:1, :]
        sg = _sigmoid(cc)
        o_ref[...] = jnp.zeros_like(o_ref)
        o_ref[0:1, :] = acc * (sg * (1.0 + cc * (1.0 - sg)))

    return pl.pallas_call(body, name="cctx_grad", out_shape=jax.ShapeDtypeStruct((8, D), F32))(p8, c_ctx_row)


def adamw(w, g, m, v, rows, name):
    r, c = w.shape

    def body(w_ref, g_ref, m_ref, v_ref, d_ref, nm_ref, nv_ref):
        gg = g_ref[...]
        nm = ADAM_B1 * m_ref[...] + (1.0 - ADAM_B1) * gg
        nv = ADAM_B2 * v_ref[...] + (1.0 - ADAM_B2) * (gg * gg)
        m_hat = nm / (1.0 - ADAM_B1 ** ADAM_STEP)
        v_hat = nv / (1.0 - ADAM_B2 ** ADAM_STEP)
        d_ref[...] = -ADAM_LR * (m_hat / (jnp.sqrt(v_hat) + ADAM_EPS) + ADAM_WD * w_ref[...])
        nm_ref[...] = nm
        nv_ref[...] = nv

    spec = pl.BlockSpec((rows, c), lambda i: (i, 0))
    sds = jax.ShapeDtypeStruct((r, c), F32)
    return pl.pallas_call(
        body, grid=(r // rows,), name=name, in_specs=[spec] * 4, out_specs=[spec] * 3, out_shape=[sds] * 3,
        compiler_params=_cparams(("parallel",)),
    )(w, g, m, v)


def _me():
    return lax.axis_index("x"), lax.axis_index("y"), lax.axis_index("c")


def _flip(v, bit):
    return 1 - v if bit else v


ANY = pl.BlockSpec(memory_space=pl.ANY)


def all_gather8(x, name):
    r, c = x.shape

    def body(x_ref, o_ref, ssem, rsem, lsem):
        mx, my, mc = _me()
        me = 4 * mx + 2 * my + mc
        local = pltpu.make_async_copy(x_ref, o_ref.at[me], lsem)
        local.start()

        def copy(k):
            px, py, pc = _flip(mx, k & 4), _flip(my, k & 2), _flip(mc, k & 1)
            return px, py, pc

        sends = []
        for k in range(1, NDEV):
            cp = pltpu.make_async_remote_copy(src_ref=x_ref, dst_ref=o_ref.at[me], send_sem=ssem.at[k - 1],
                                              recv_sem=rsem.at[k - 1], device_id=copy(k), device_id_type=MESH)
            cp.start()
            sends.append(cp)
        for k in range(1, NDEV):
            px, py, pc = copy(k)
            pltpu.make_async_remote_copy(src_ref=x_ref, dst_ref=o_ref.at[4 * px + 2 * py + pc], send_sem=ssem.at[k - 1],
                                         recv_sem=rsem.at[k - 1], device_id=(px, py, pc), device_id_type=MESH).wait_recv()
        for cp in sends:
            cp.wait_send()
        local.wait()

    vm = pl.BlockSpec(memory_space=pltpu.VMEM)
    return pl.pallas_call(
        body, name=name, in_specs=[vm], out_specs=vm, out_shape=jax.ShapeDtypeStruct((NDEV, r, c), x.dtype),
        scratch_shapes=[pltpu.SemaphoreType.DMA((NDEV - 1,)), pltpu.SemaphoreType.DMA((NDEV - 1,)), pltpu.SemaphoreType.DMA],
    )(x)


def _chip_peers(mx, my):
    out = []
    for p in range(1, NCHIP):
        px, py = _flip(mx, p & 2), _flip(my, p & 1)
        out.append((px, py, 2 * px + py))
    return out


def gather_weights(shards):
    n = len(shards)

    def body(*refs):
        ins, outs = refs[:n], refs[n:2 * n]
        ssem, rsem, lsem = refs[2 * n:]
        mx, my, mc = _me()
        jme = 2 * mx + my
        peers = _chip_peers(mx, my)
        started = []
        for k in range(n):
            lc = pltpu.make_async_copy(ins[k], outs[k].at[jme], lsem.at[k])
            lc.start()
            started.append(lc)
        sends = []
        for k in range(n):
            for p, (px, py, _) in enumerate(peers):
                cp = pltpu.make_async_remote_copy(src_ref=ins[k], dst_ref=outs[k].at[jme], send_sem=ssem.at[3 * k + p],
                                                  recv_sem=rsem.at[3 * k + p], device_id=(px, py, mc), device_id_type=MESH)
                cp.start()
                sends.append(cp)
        for k in range(n):
            for p, (px, py, jp) in enumerate(peers):
                pltpu.make_async_remote_copy(src_ref=ins[k], dst_ref=outs[k].at[jp], send_sem=ssem.at[3 * k + p],
                                             recv_sem=rsem.at[3 * k + p], device_id=(px, py, mc), device_id_type=MESH).wait_recv()
        for cp in sends:
            cp.wait_send()
        for lc in started:
            lc.wait()

    return pl.pallas_call(
        body, name="gather_weights", in_specs=[ANY] * n, out_specs=[ANY] * n,
        out_shape=[jax.ShapeDtypeStruct((NCHIP,) + s.shape, s.dtype) for s in shards],
        scratch_shapes=[pltpu.SemaphoreType.DMA((3 * n,)), pltpu.SemaphoreType.DMA((3 * n,)), pltpu.SemaphoreType.DMA((n,))],
    )(*shards)


def rs_sibling(grads):
    n = len(grads)

    def body(*refs):
        ins, mine, theirs = refs[:n], refs[n:2 * n], refs[2 * n:3 * n]
        ssem, rsem, lsem = refs[3 * n:]
        mx, my, mc = _me()
        work = []
        for k in range(n):
            r2 = grads[k].shape[1] // 2
            keep = pl.ds(pl.multiple_of(mc * r2, 8), r2)
            give = pl.ds(pl.multiple_of((1 - mc) * r2, 8), r2)
            lc = pltpu.make_async_copy(ins[k].at[:, keep, :], mine[k], lsem.at[k])
            cp = pltpu.make_async_remote_copy(src_ref=ins[k].at[:, give, :], dst_ref=theirs[k], send_sem=ssem.at[k],
                                              recv_sem=rsem.at[k], device_id=(mx, my, 1 - mc), device_id_type=MESH)
            lc.start()
            cp.start()
            work.append((lc, cp))
        for lc, cp in work:
            cp.wait_recv()
            cp.wait_send()
            lc.wait()

    half = [jax.ShapeDtypeStruct((NCHIP, g.shape[1] // 2, g.shape[2]), F32) for g in grads]
    return pl.pallas_call(
        body, name="rs_sibling", in_specs=[ANY] * n, out_specs=[ANY] * (2 * n), out_shape=half + half,
        scratch_shapes=[pltpu.SemaphoreType.DMA((n,)), pltpu.SemaphoreType.DMA((n,)), pltpu.SemaphoreType.DMA((n,))],
    )(*grads)


def rs_chips(parts):
    n = len(parts)

    def body(*refs):
        ins, outs = refs[:n], refs[n:2 * n]
        ssem, rsem, lsem = refs[2 * n:]
        mx, my, mc = _me()
        jme = 2 * mx + my
        peers = _chip_peers(mx, my)
        locals_ = []
        for k in range(n):
            lc = pltpu.make_async_copy(ins[k].at[jme], outs[k].at[jme], lsem.at[k])
            lc.start()
            locals_.append(lc)
        sends = []
        for k in range(n):
            for p, (px, py, jp) in enumerate(peers):
                cp = pltpu.make_async_remote_copy(src_ref=ins[k].at[jp], dst_ref=outs[k].at[jme], send_sem=ssem.at[3 * k + p],
                                                  recv_sem=rsem.at[3 * k + p], device_id=(px, py, mc), device_id_type=MESH)
                cp.start()
                sends.append(cp)
        for k in range(n):
            for p, (px, py, jp) in enumerate(peers):
                pltpu.make_async_remote_copy(src_ref=ins[k].at[jp], dst_ref=outs[k].at[jp], send_sem=ssem.at[3 * k + p],
                                             recv_sem=rsem.at[3 * k + p], device_id=(px, py, mc), device_id_type=MESH).wait_recv()
        for cp in sends:
            cp.wait_send()
        for lc in locals_:
            lc.wait()

    return pl.pallas_call(
        body, name="rs_chips", in_specs=[ANY] * n, out_specs=[ANY] * n,
        out_shape=[jax.ShapeDtypeStruct(p.shape, F32) for p in parts],
        scratch_shapes=[pltpu.SemaphoreType.DMA((3 * n,)), pltpu.SemaphoreType.DMA((3 * n,)), pltpu.SemaphoreType.DMA((n,))],
    )(*parts)


def share_halves(halves):
    n = len(halves)

    def body(*refs):
        ins, outs = refs[:n], refs[n:2 * n]
        ssem, rsem, lsem = refs[2 * n:]
        mx, my, mc = _me()
        work = []
        for k in range(n):
            r2 = halves[k].shape[0]
            mine = pl.ds(pl.multiple_of(mc * r2, 8), r2)
            theirs = pl.ds(pl.multiple_of((1 - mc) * r2, 8), r2)
            lc = pltpu.make_async_copy(ins[k], outs[k].at[mine, :], lsem.at[k])
            cp = pltpu.make_async_remote_copy(src_ref=ins[k], dst_ref=outs[k].at[mine, :], send_sem=ssem.at[k],
                                              recv_sem=rsem.at[k], device_id=(mx, my, 1 - mc), device_id_type=MESH)
            wt = pltpu.make_async_remote_copy(src_ref=ins[k], dst_ref=outs[k].at[theirs, :], send_sem=ssem.at[k],
                                              recv_sem=rsem.at[k], device_id=(mx, my, 1 - mc), device_id_type=MESH)
            lc.start()
            cp.start()
            work.append((lc, cp, wt))
        for lc, cp, wt in work:
            wt.wait_recv()
            cp.wait_send()
            lc.wait()

    return pl.pallas_call(
        body, name="share_halves", in_specs=[ANY] * n, out_specs=[ANY] * n,
        out_shape=[jax.ShapeDtypeStruct((2 * h.shape[0], h.shape[1]), F32) for h in halves],
        scratch_shapes=[pltpu.SemaphoreType.DMA((n,)), pltpu.SemaphoreType.DMA((n,)), pltpu.SemaphoreType.DMA((n,))],
    )(*halves)


def sibling_add(g, ngrp, hr, tr, name):
    c_ = g.shape[1]
    nt = hr // tr

    def body(cidx, keep_ref, give_ref, o_ref, land, ssem, rsem):
        mx, my, mc = _me()
        t = pl.program_id(0) * nt + pl.program_id(1)
        s = t % 2
        cp = pltpu.make_async_remote_copy(src_ref=give_ref, dst_ref=land.at[s], send_sem=ssem.at[s], recv_sem=rsem.at[s],
                                          device_id=(mx, my, 1 - mc), device_id_type=MESH)
        cp.start()
        cp.wait_recv()
        o_ref[...] = keep_ref[...] + land[s]
        cp.wait_send()

    grid_spec = pltpu.PrefetchScalarGridSpec(
        num_scalar_prefetch=1, grid=(ngrp, nt),
        in_specs=[pl.BlockSpec((tr, c_), lambda i, j, cr: ((2 * i + cr[0]) * nt + j, 0)),
                  pl.BlockSpec((tr, c_), lambda i, j, cr: ((2 * i + 1 - cr[0]) * nt + j, 0))],
        out_specs=pl.BlockSpec((tr, c_), lambda i, j, cr: (i * nt + j, 0)),
        scratch_shapes=[pltpu.VMEM((2, tr, c_), F32), pltpu.SemaphoreType.DMA((2,)), pltpu.SemaphoreType.DMA((2,))])
    cidx = lax.axis_index("c").astype(jnp.int32).reshape(1)
    return pl.pallas_call(body, grid_spec=grid_spec, name=name, out_shape=jax.ShapeDtypeStruct((ngrp * hr, c_), F32),
                          compiler_params=_cparams(("arbitrary", "arbitrary")))(cidx, g, g)


def finish_weight(b, w, m, v, tr, name):
    _, r2, c_ = b.shape

    def body(b_ref, w_ref, m_ref, v_ref, g_ref, d_ref, nm_ref, nv_ref, mine, land, ssem, rsem):
        mx, my, mc = _me()
        t = pl.program_id(0)
        s = t % 2
        mine[s] = (b_ref[0] + b_ref[1]) + (b_ref[2] + b_ref[3])
        cp = pltpu.make_async_remote_copy(src_ref=mine.at[s], dst_ref=land.at[s], send_sem=ssem.at[s], recv_sem=rsem.at[s],
                                          device_id=(mx, my, 1 - mc), device_id_type=MESH)
        cp.start()
        cp.wait_recv()
        g_ref[mc] = mine[s]
        g_ref[1 - mc] = land[s]
        cp.wait_send()
        gg = g_ref[...]
        nm = ADAM_B1 * m_ref[...] + (1.0 - ADAM_B1) * gg
        nv = ADAM_B2 * v_ref[...] + (1.0 - ADAM_B2) * (gg * gg)
        m_hat = nm / (1.0 - ADAM_B1 ** ADAM_STEP)
        v_hat = nv / (1.0 - ADAM_B2 ** ADAM_STEP)
        d_ref[...] = -ADAM_LR * (m_hat / (jnp.sqrt(v_hat) + ADAM_EPS) + ADAM_WD * w_ref[...])
        nm_ref[...] = nm
        nv_ref[...] = nv

    spec = pl.BlockSpec((2, tr, c_), lambda i: (0, i, 0))
    sds = jax.ShapeDtypeStruct((2, r2, c_), F32)
    return pl.pallas_call(
        body, grid=(r2 // tr,), name=name,
        in_specs=[pl.BlockSpec((NCHIP, tr, c_), lambda i: (0, i, 0)), spec, spec, spec],
        out_specs=[spec] * 4, out_shape=[sds] * 4,
        scratch_shapes=[pltpu.VMEM((2, tr, c_), F32), pltpu.VMEM((2, tr, c_), F32), pltpu.SemaphoreType.DMA((2,)),
                        pltpu.SemaphoreType.DMA((2,))],
        compiler_params=_cparams(("arbitrary",)))(b, w, m, v)


TM_IN = 256
TM_GLA = 512
TM_MERGE = 512
TM_FFN = 256
TN_WGRAD = 1024

WEIGHTS = ['c_ctx', 'w_mod', 'b_mod', 'norm1_g', 'norm2_g', 'w_in', 'conv_w', 'conv_b', 'conv_ln_g', 'conv_ln_b', 'w_a2_f',
           'b_a_f', 'w_a2_b', 'b_a_b', 'gla_norm_g', 'w_out', 'w_gate', 'w_up', 'w_down', 'final_g']
BIG = ['w_in', 'w_out', 'w_gate', 'w_up', 'w_down']


def _rows(*vs):
    out = jnp.zeros((8, vs[0].shape[-1]), F32)
    for i, v in enumerate(vs):
        out = out.at[i].set(v.reshape(-1))
    return out


def _small_slab(p):
    cat = lambda *ks: jnp.concatenate([p[k].reshape(-1) for k in ks])
    vecs = _rows(p['c_ctx'], p['norm1_g'], p['norm2_g'], p['final_g'], cat('conv_b', 'conv_ln_g'),
                 cat('conv_ln_b', 'b_a_f', 'b_a_b'), jnp.pad(p['gla_norm_g'].reshape(-1), (0, D - HV)))
    bmod = jnp.pad(p['b_mod'].reshape(6, D), ((0, 2), (0, 0)))
    shards = jnp.pad(jnp.concatenate([jnp.pad(p['conv_w'].reshape(-1), (0, DC // NCHIP)), cat('w_a2_f', 'w_a2_b')]),
                     (0, 2 * D)).reshape(8, D)
    return jnp.concatenate([vecs, bmod, shards], axis=0)


def _unslab(s):
    return {
        'c_ctx': s[0], 'norm1_g': s[1:2], 'norm2_g': s[2:3], 'final_g': s[3],
        'conv_b': s[4:5, :DC], 'conv_ln_g': s[4:5, DC:], 'conv_ln_b': s[5:6, :DC],
        'b_a_f': s[5:6, DC:DC + DK], 'b_a_b': s[5:6, DC + DK:], 'gla_norm_g': s[6:7, :HV],
        'b_mod': s[8:14].reshape(1, 6 * D),
        'conv_w': s[16:20].reshape(32, DC // NCHIP)[:CW].reshape(1, CW, DC // NCHIP),
        'w_a2_f': s[20].reshape(1, RANK, DK // NCHIP), 'w_a2_b': s[21].reshape(1, RANK, DK // NCHIP),
    }


def kernel(x, c, ctx, c_ctx, w_mod, b_mod, norm1_g, norm2_g, w_in, conv_w, conv_b, conv_ln_g, conv_ln_b, w_a2_f, b_a_f, w_a2_b, b_a_b, gla_norm_g, w_out, w_gate, w_up, w_down, final_g, loss_target, m_c_ctx, m_w_mod, m_b_mod, m_norm1_g, m_norm2_g, m_w_in, m_conv_w, m_conv_b, m_conv_ln_g, m_conv_ln_b, m_w_a2_f, m_b_a_f, m_w_a2_b, m_b_a_b, m_gla_norm_g, m_w_out, m_w_gate, m_w_up, m_w_down, m_final_g, v_c_ctx, v_w_mod, v_b_mod, v_norm1_g, v_norm2_g, v_w_in, v_conv_w, v_conv_b, v_conv_ln_g, v_conv_ln_b, v_w_a2_f, v_b_a_f, v_w_a2_b, v_b_a_b, v_gla_norm_g, v_w_out, v_w_gate, v_w_up, v_w_down, v_final_g):
    w = dict(c_ctx=c_ctx, w_mod=w_mod, b_mod=b_mod, norm1_g=norm1_g, norm2_g=norm2_g, w_in=w_in, conv_w=conv_w, conv_b=conv_b,
             conv_ln_g=conv_ln_g, conv_ln_b=conv_ln_b, w_a2_f=w_a2_f, b_a_f=b_a_f, w_a2_b=w_a2_b, b_a_b=b_a_b,
             gla_norm_g=gla_norm_g, w_out=w_out, w_gate=w_gate, w_up=w_up, w_down=w_down, final_g=final_g)
    m = dict(c_ctx=m_c_ctx, w_mod=m_w_mod, b_mod=m_b_mod, norm1_g=m_norm1_g, norm2_g=m_norm2_g, w_in=m_w_in, conv_w=m_conv_w,
             conv_b=m_conv_b, conv_ln_g=m_conv_ln_g, conv_ln_b=m_conv_ln_b, w_a2_f=m_w_a2_f, b_a_f=m_b_a_f, w_a2_b=m_w_a2_b,
             b_a_b=m_b_a_b, gla_norm_g=m_gla_norm_g, w_out=m_w_out, w_gate=m_w_gate, w_up=m_w_up, w_down=m_w_down,
             final_g=m_final_g)
    v = dict(c_ctx=v_c_ctx, w_mod=v_w_mod, b_mod=v_b_mod, norm1_g=v_norm1_g, norm2_g=v_norm2_g, w_in=v_w_in, conv_w=v_conv_w,
             conv_b=v_conv_b, conv_ln_g=v_conv_ln_g, conv_ln_b=v_conv_ln_b, w_a2_f=v_w_a2_f, b_a_f=v_b_a_f, w_a2_b=v_w_a2_b,
             b_a_b=v_b_a_b, gla_norm_g=v_gla_norm_g, w_out=v_w_out, w_gate=v_w_gate, w_up=v_w_up, w_down=v_w_down,
             final_g=v_final_g)
    mx, my, mc = _me()
    jme = 2 * mx + my
    me = 4 * mx + 2 * my + mc
    wmc = D * 6 // NCHIP
    xx, tgt, cx = x[0], loss_target[0], ctx[0]
    n = xx.shape[0]

    c8 = all_gather8(_rows(c[0]), "gather_c")[:, 0, :]
    cext = jnp.concatenate([c8, _rows(c_ctx)], axis=0)
    mloc = mod_fwd(cext, w_mod[0], lax.dynamic_slice_in_dim(b_mod, jme * wmc, wmc, axis=1))
    mall = all_gather8(mloc, "gather_mod")
    mall = jnp.concatenate([mall[2 * j] for j in range(NCHIP)], axis=1)
    sh1, sc1, g1, sh2, sc2, g2 = jnp.split(lax.dynamic_slice_in_dim(mall, me, 1, axis=0)[0], 6)
    csh1, csc1 = mall[8, :D], mall[8, D:2 * D]

    gw = gather_weights([w[k][0].astype(BF) for k in BIG])
    cols = lambda a: jnp.transpose(a, (1, 0, 2)).reshape(a.shape[1], -1)
    win = jnp.pad(cols(gw[0]), ((0, 0), (0, DINP - DIN)))
    wout = gw[1].reshape(D, D)
    wg, wu = cols(gw[2]), cols(gw[3])
    wd = gw[4].reshape(DFF, D)

    sw = jnp.concatenate([jnp.pad(conv_w[0], ((0, 1), (0, 0))).reshape(1, -1), w_a2_f[0].reshape(1, -1),
                          w_a2_b[0].reshape(1, -1)], axis=1)
    sw8 = all_gather8(jnp.pad(sw.reshape(6, D), ((0, 2), (0, 0))), "gather_small_w")
    swc = jnp.stack([sw8[2 * j] for j in range(NCHIP)]).reshape(NCHIP, 8 * D)
    convw = jnp.transpose(swc[:, :32 * 128].reshape(NCHIP, 32, 128), (1, 0, 2)).reshape(32, DC)
    a2 = lambda o: jnp.transpose(swc[:, o:o + RANK * 64].reshape(NCHIP, RANK, 64), (1, 0, 2)).reshape(RANK, DK)
    wa2 = jnp.zeros((128, 2 * DK), F32).at[0:RANK, 0:DK].set(a2(32 * 128)).at[RANK:2 * RANK, DK:].set(a2(32 * 128 + RANK * 64))
    wa2 = wa2.astype(BF)
    ba = jnp.concatenate([b_a_f, b_a_b], axis=1)
    cvec = _rows(conv_b, conv_ln_g, conv_ln_b)
    vec1 = _rows(norm1_g, sh1, sc1)
    vecc = _rows(norm1_g, csh1, csc1)
    vecm = _rows(g1)
    vecf = _rows(norm2_g, sh2, sc2, g2, final_g)
    gn = jnp.tile(gla_norm_g, (1, NH))

    s0 = ctx_fwd(cx, vecc, win, wa2, ba)
    ag, yb, co, qk, vv, gg, la, r = fwd_in(xx, vec1, win, convw, cvec, wa2, ba, TM_IN)
    o_f, o_b, se_f, se_b = gla_fwd(qk, vv, la, s0, TM_GLA)
    x1, y1, cat = merge_fwd(xx, o_f, o_b, gg, co, vecm, gn, wout, TM_MERGE)

    dx1, h2, act, dgt, dup, dy2, sf = ffn_fwd_bwd(x1, tgt, vecf, wg, wu, wd, TM_FFN)
    zero = lambda *s: jnp.zeros(s, F32)
    d_wg = wgrad(h2, dgt, zero(D, DFF), D, DFF // 2, TN_WGRAD, "wgrad_gate")
    d_wu = wgrad(h2, dup, zero(D, DFF), D, DFF // 2, TN_WGRAD, "wgrad_up")
    d_wd = wgrad(act, dy2, zero(DFF, D), DFF // 2, D, TN_WGRAD, "wgrad_down")
    dy1, dco, do, dg, s1, s2 = merge_bwd(dx1, y1, o_f, o_b, gg, vecm, gn, wout, TM_MERGE)
    d_wout = wgrad(cat, dy1, zero(D, D), D, D, TN_WGRAD, "wgrad_out")
    dqk_f, dv_f, dla_f, dqk_b, dv_b, dla_b, ds0 = gla_bwd(qk, vv, la, do, se_f, se_b, TM_GLA)
    dwin_c, dwa2_c, sc = ctx_bwd(cx, vecc, win, wa2, ba, ds0)
    grad_x, h, dp, dwa2, dcw, sd = bwd_in(xx, dx1, ag, yb, dco, dqk_f, dqk_b, dv_f, dv_b, dg, dla_f, dla_b, la, r,
                                          vec1, win, convw, cvec, wa2, TM_IN)
    d_win = wgrad(h, dp, dwin_c, D, DINP // 3, TN_WGRAD, "wgrad_in")

    rows16, dcw_t, dwa2_t = pack_small(sf, s1, s2, sd, sc, dcw, dwa2, dwa2_c)
    sp = jnp.concatenate([rows16, dcw_t.reshape(16, D), dwa2_t.reshape(16, D)], axis=0)
    g8 = all_gather8(sp, "gather_small_grads")
    tot, bm_g, loss8 = small_totals(g8)
    loss = loss8[0, 0]
    dmod8 = g8[:, 0:6, :].reshape(NDEV, 6 * D)
    dmodc = jnp.concatenate([tot[6], tot[7], jnp.zeros((4 * D,), F32)])
    dm = jnp.concatenate([dmod8, _rows(dmodc)], axis=0)
    dm = lax.dynamic_slice_in_dim(dm, jme * wmc, wmc, axis=1)
    g_wmod, dsil = mod_bwd(cext, dm, w_mod[0])
    p8 = all_gather8(dsil[8:16], "gather_dsilu")
    g_cctx = cctx_grad(p8, _rows(c_ctx))[0]

    shard = lambda a, k: jnp.transpose(a.reshape(a.shape[0], NCHIP, k), (1, 0, 2))
    hd = D // 2
    parts = [shard(sibling_add(d_win, 1, hd, 128, "xadd_w_in")[:, :DIN], DIN // NCHIP),
             sibling_add(d_wout, NCHIP, hd // NCHIP, hd // NCHIP, "xadd_w_out").reshape(NCHIP, hd // NCHIP, D),
             shard(sibling_add(d_wg, 1, hd, 128, "xadd_w_gate"), DFF // NCHIP),
             shard(sibling_add(d_wu, 1, hd, 128, "xadd_w_up"), DFF // NCHIP),
             sibling_add(d_wd, NCHIP, DFF // 8, DFF // 16, "xadd_w_down").reshape(NCHIP, DFF // 8, D)]
    recv = rs_chips(parts)

    grads, delta, new_m, new_v = {}, {}, {}, {}
    for i, k in enumerate(BIG):
        r2, cc = recv[i].shape[1:]
        halves = lambda a: a[0].reshape(2, r2, cc)
        outs = finish_weight(recv[i], halves(w[k]), halves(m[k]), halves(v[k]), 88 if r2 % 128 else 128, "finish_" + k)
        grads[k], delta[k], new_m[k], new_v[k] = (o.reshape(w[k].shape) for o in outs)
    grads['w_mod'] = g_wmod[None]
    d_, m_, v_ = adamw(w_mod[0], g_wmod, m_w_mod[0], v_w_mod[0], 128, "adamw_w_mod")
    delta['w_mod'], new_m['w_mod'], new_v['w_mod'] = d_[None], m_[None], v_[None]
    small_g = {
        'c_ctx': g_cctx, 'b_mod': bm_g[0:6].reshape(1, 6 * D), 'norm1_g': tot[8:9], 'norm2_g': tot[9:10], 'final_g': tot[10],
        'conv_b': tot[11:12, :DC], 'conv_ln_g': tot[11:12, DC:], 'conv_ln_b': tot[12:13, :DC],
        'b_a_f': tot[12:13, DC:DC + DK], 'b_a_b': tot[12:13, DC + DK:], 'gla_norm_g': tot[13:14, :HV],
        'conv_w': lax.dynamic_slice_in_dim(tot[16:32].reshape(32, DC)[:CW], jme * (DC // NCHIP), DC // NCHIP, axis=1)[None],
        'w_a2_f': lax.dynamic_slice_in_dim(tot[32:48].reshape(32, 2 * DK)[0:RANK, 0:DK], jme * (DK // NCHIP), DK // NCHIP, axis=1)[None],
        'w_a2_b': lax.dynamic_slice_in_dim(tot[32:48].reshape(32, 2 * DK)[RANK:2 * RANK, DK:], jme * (DK // NCHIP), DK // NCHIP, axis=1)[None],
    }
    grads.update(small_g)
    sd_, sm_, sv_ = adamw(_small_slab(w), _small_slab(small_g), _small_slab(m), _small_slab(v), 24,
                          "adamw_small")
    for dst, slab in ((delta, sd_), (new_m, sm_), (new_v, sv_)):
        dst.update(_unslab(slab))
    out = [loss, grad_x[None]]
    for group in (grads, delta, new_m, new_v):
        out += [group[k].reshape(w[k].shape) for k in WEIGHTS]
    return tuple(out)
```

```python
import functools

import jax
import jax.numpy as jnp
from jax import lax
from jax.experimental import pallas as pl
from jax.experimental.pallas import tpu as pltpu

F32 = jnp.float32
BF = jnp.bfloat16

D = 1024
DC = 512
NH = 4
HK = 64
HV = 128
DK = NH * HK
DV = NH * HV
RANK = 16
CH = 64
GW = 64
CW = 31
CPAD = CW // 2
SEGP = GW + 32
DFF = 2816
DIN = 2592
DINP = 2688
EPS = 1e-6
TAU = 16.0
QSCALE = HK ** -0.5
NCHIP = 4
NDEV = 8

ADAM_LR = 0.001
ADAM_B1 = 0.9
ADAM_B2 = 0.999
ADAM_EPS = 1e-08
ADAM_WD = 0.01
ADAM_STEP = 10

VMEM_LIMIT = 56 * 1024 * 1024
MESH = pl.DeviceIdType.MESH


def _dot(a, b):
    return jnp.dot(a, b, preferred_element_type=F32)


def _dot_nt(a, b):
    return lax.dot_general(a, b, (((1,), (1,)), ((), ())), preferred_element_type=F32)


def _dot_tn(a, b):
    return lax.dot_general(a, b, (((0,), (0,)), ((), ())), preferred_element_type=F32)


def _split3(x):
    hi = x.astype(BF)
    r1 = x - hi.astype(F32)
    mid = r1.astype(BF)
    lo = (r1 - mid.astype(F32)).astype(BF)
    return hi, mid, lo


def _mask_dot(t, x):
    hi, mid, lo = _split3(x)
    return _dot(t, hi) + _dot(t, mid) + _dot(t, lo)


def _sigmoid(x):
    return 1.0 / (1.0 + jnp.exp(-x))


def _log_sigmoid(x):
    return jnp.minimum(x, 0.0) - jnp.log(1.0 + jnp.exp(-jnp.abs(x)))


def _colsum8(z):
    t, c = z.shape
    return jnp.sum(z.reshape(t // 8, 8, c), axis=0)


def _tri(n, kind):
    r = lax.broadcasted_iota(jnp.int32, (n, n), 0)
    c = lax.broadcasted_iota(jnp.int32, (n, n), 1)
    m = {"le": c <= r, "lt": c < r, "ge": c >= r, "gt": c > r}[kind]
    return m


def _full(shape):
    nd = len(shape)
    return pl.BlockSpec(shape, lambda *_: (0,) * nd)


def _cparams(sem, vmem=VMEM_LIMIT):
    return pltpu.CompilerParams(dimension_semantics=sem, vmem_limit_bytes=vmem)


def _call(body, grid, name, in_specs, out_specs, out_shape, scratch, operands, exchange=None, carried=()):
    n_in, n_out, n_scr = len(in_specs), len(out_specs), len(scratch)
    if exchange is None:
        fn = body
    else:
        n = exchange.n

        def fn(*refs):
            ins, cin = refs[:n_in], refs[n_in:n_in + n]
            outs, cout = refs[n_in + n:n_in + n + n_out], refs[n_in + n + n_out:n_in + 2 * n + n_out]
            rest = refs[n_in + 2 * n + n_out:]
            scr, sems = rest[:n_scr], rest[n_scr:]

            @pl.when(pl.program_id(0) == 0)
            def _():
                exchange.start(cin, cout, sems)

            body(*ins, *outs, *scr)

            @pl.when(pl.program_id(0) == pl.num_programs(0) - 1)
            def _():
                exchange.wait(cin, cout, sems)

        any_spec = pl.BlockSpec(memory_space=pl.ANY)
        in_specs = list(in_specs) + [any_spec] * n
        out_specs = list(out_specs) + [any_spec] * n
        out_shape = list(out_shape) + exchange.out_shape
        scratch = list(scratch) + exchange.scratch
    return pl.pallas_call(fn, grid=grid, name=name, in_specs=in_specs, out_specs=out_specs, out_shape=out_shape,
                          scratch_shapes=scratch, compiler_params=_cparams(("arbitrary",)))(*operands, *carried)


def _conv_taps(pad_ref, s, w_ref, c0, cw, flip):
    acc = jnp.zeros((GW, cw), F32)
    for j in range(CW):
        wj = w_ref[pl.ds((CW - 1 - j) if flip else j, 1), pl.ds(c0, cw)]
        acc = acc + wj * pad_ref[s, pl.ds(j + 1, GW), pl.ds(c0, cw)]
    return acc


def _ln_stats(yb):
    mu = jnp.mean(yb, axis=-1, keepdims=True)
    yc = yb - mu
    var = jnp.mean(yc * yc, axis=-1, keepdims=True)
    rs = lax.rsqrt(var + EPS)
    return yc * rs, rs


def fwd_in(x, vec1, win, convw, cvec, wa2, ba, tm, exchange=None, carried=()):
    n = x.shape[0]
    nseg = tm // GW
    half = DC // 2

    def body(x_ref, vec_ref, win_ref, cw_ref, cv_ref, wa2_ref, ba_ref,
             ag_ref, yb_ref, co_ref, qk_ref, v_ref, g_ref, la_ref, r_ref, pad_ref):
        xx = x_ref[...]
        rstd = lax.rsqrt(jnp.mean(xx * xx, axis=-1, keepdims=True) + EPS)
        h = (xx * rstd * vec_ref[0:1, :]) * (1.0 + vec_ref[2:3, :]) + vec_ref[1:2, :]
        p = _dot(h.astype(BF), win_ref[...])
        ag_ref[...] = p[:, :2 * DC].astype(BF)
        qk_ref[...] = p[:, 2 * DC:2 * DC + 2 * DK].astype(BF)
        v_ref[...] = p[:, 2 * DC + 2 * DK:2 * DC + 2 * DK + DV].astype(BF)
        g_ref[...] = p[:, 2 * DC + 2 * DK + DV:2 * DC + 2 * DK + 2 * DV].astype(BF)
        r = p[:, DINP - 128:].astype(BF)
        r_ref[...] = r
        la_ref[...] = _log_sigmoid(_dot(r, wa2_ref[...]) + ba_ref[...]) * (1.0 / TAU)

        vc = p[:, :DC] * _sigmoid(p[:, DC:2 * DC])
        zeros = jnp.zeros((nseg, 16, DC), F32)
        pad_ref[:, 0:16, :] = zeros
        pad_ref[:, 16 + GW:SEGP, :] = zeros
        pad_ref[:, 16:16 + GW, :] = vc.reshape(nseg, GW, DC)

        def seg(s, carry):
            for c0 in (0, half):
                y = _conv_taps(pad_ref, s, cw_ref, c0, half, False)
                yb_ref[pl.ds(pl.multiple_of(s * GW, GW), GW), pl.ds(c0, half)] = y + cv_ref[0:1, c0:c0 + half]
            return carry

        lax.fori_loop(0, nseg, seg, 0)
        yn, _ = _ln_stats(yb_ref[...])
        ln = yn * cv_ref[1:2, :] + cv_ref[2:3, :]
        co_ref[...] = (ln * _sigmoid(ln)).astype(BF)

    tok = lambda w: pl.BlockSpec((tm, w), lambda i: (i, 0))
    return _call(
        body, (n // tm,), "fwd_in",
        [tok(D), _full(vec1.shape), _full(win.shape), _full(convw.shape), _full(cvec.shape), _full(wa2.shape), _full(ba.shape)],
        [tok(2 * DC), tok(DC), tok(DC), tok(2 * DK), tok(DV), tok(DV), tok(2 * DK), tok(128)],
        [jax.ShapeDtypeStruct((n, 2 * DC), BF), jax.ShapeDtypeStruct((n, DC), F32),
         jax.ShapeDtypeStruct((n, DC), BF), jax.ShapeDtypeStruct((n, 2 * DK), BF),
         jax.ShapeDtypeStruct((n, DV), BF), jax.ShapeDtypeStruct((n, DV), BF),
         jax.ShapeDtypeStruct((n, 2 * DK), F32), jax.ShapeDtypeStruct((n, 128), BF)],
        [pltpu.VMEM((nseg, SEGP, DC), F32)],
        (x, vec1, win, convw, cvec, wa2, ba), exchange, carried)


def _gla_dir(d):
    return (_tri(CH, "le"), CH - 1) if d == 0 else (_tri(CH, "ge"), 0)


def _gla_chunk_terms(qk, la, d):
    seen, last = _gla_dir(d)
    b = _mask_dot(seen.astype(BF), la)
    bl = b[last:last + 1, :]
    eb = jnp.exp(b)
    enb = jnp.exp(-b)
    ekd = jnp.exp(bl - b)
    ebl = jnp.exp(bl)
    q = qk[:, :DK].astype(F32) * QSCALE
    k = qk[:, DK:].astype(F32)
    return eb, enb, ekd, ebl, q * eb, k * enb, k * ekd


def gla_fwd(qk, v, la, s0, tm):
    n = qk.shape[0]
    nt = n // tm
    nc = tm // CH

    def body(qkf_ref, vf_ref, laf_ref, qkb_ref, vb_ref, lab_ref, s0_ref,
             of_ref, ob_ref, sef_ref, seb_ref, st_ref):
        @pl.when(pl.program_id(0) == 0)
        def _():
            st_ref[...] = s0_ref[...]

        def chunk(ci, carry):
            for d, (qk_ref, v_ref, la_ref, o_ref, se_ref) in enumerate(
                    ((qkf_ref, vf_ref, laf_ref, of_ref, sef_ref), (qkb_ref, vb_ref, lab_ref, ob_ref, seb_ref))):
                c = ci if d == 0 else nc - 1 - ci
                rows = pl.ds(pl.multiple_of(c * CH, CH), CH)
                amask, _ = _gla_dir(d)
                eb, enb, ekd, ebl, qt, kt, kd = _gla_chunk_terms(qk_ref[rows, :], la_ref[rows, :], d)
                qt = qt.astype(BF)
                kt = kt.astype(BF)
                kd = kd.astype(BF)
                vv = v_ref[rows, :]
                for h in range(NH):
                    ks = slice(h * HK, (h + 1) * HK)
                    vs = slice(h * HV, (h + 1) * HV)
                    st = st_ref[d, h]
                    se_ref[c, h] = st
                    a = jnp.where(amask, _dot_nt(qt[:, ks], kt[:, ks]), 0.0)
                    o_ref[rows, vs] = _dot(a.astype(BF), vv[:, vs]) + _dot_nt(qt[:, ks], st.astype(BF))
                    st_ref[d, h] = ebl[:, ks] * st + _dot_tn(vv[:, vs], kd[:, ks])
            return carry

        lax.fori_loop(0, nc, chunk, 0)

    fw = lambda w, col=0: pl.BlockSpec((tm, w), lambda i: (i, col))
    bw = lambda w, col=0: pl.BlockSpec((tm, w), lambda i: (nt - 1 - i, col))
    se_f = pl.BlockSpec((nc, NH, HV, HK), lambda i: (i, 0, 0, 0))
    se_b = pl.BlockSpec((nc, NH, HV, HK), lambda i: (nt - 1 - i, 0, 0, 0))
    se_shape = jax.ShapeDtypeStruct((n // CH, NH, HV, HK), F32)
    return pl.pallas_call(
        body, grid=(nt,), name="gla_fwd",
        in_specs=[fw(2 * DK), fw(DV), fw(DK, 0), bw(2 * DK), bw(DV), bw(DK, 1), _full(s0.shape)],
        out_specs=[fw(DV), bw(DV), se_f, se_b],
        out_shape=[jax.ShapeDtypeStruct((n, DV), F32), jax.ShapeDtypeStruct((n, DV), F32), se_shape, se_shape],
        scratch_shapes=[pltpu.VMEM((2, NH, HV, HK), F32)],
        compiler_params=_cparams(("arbitrary",)),
    )(qk, v, la, qk, v, la, s0)


def gla_bwd(qk, v, la, do, se_f, se_b, tm, exchange=None, carried=()):
    n = qk.shape[0]
    nt = n // tm
    nc = tm // CH

    def body(qkf_ref, vf_ref, laf_ref, dof_ref, sef_ref, qkb_ref, vb_ref, lab_ref, dob_ref, seb_ref,
             dqkf_ref, dvf_ref, dlaf_ref, dqkb_ref, dvb_ref, dlab_ref, ds0_ref, ds_ref):
        @pl.when(pl.program_id(0) == 0)
        def _():
            ds_ref[...] = jnp.zeros_like(ds_ref)

        def chunk(ci, carry):
            for d, (qk_ref, v_ref, la_ref, do_ref, se_ref, dqk_ref, dv_ref, dla_ref) in enumerate(
                    ((qkf_ref, vf_ref, laf_ref, dof_ref, sef_ref, dqkf_ref, dvf_ref, dlaf_ref),
                     (qkb_ref, vb_ref, lab_ref, dob_ref, seb_ref, dqkb_ref, dvb_ref, dlab_ref))):
                c = nc - 1 - ci if d == 0 else ci
                rows = pl.ds(pl.multiple_of(c * CH, CH), CH)
                amask, last = _gla_dir(d)
                cum_t = _gla_dir(1 - d)[0]
                eb, enb, ekd, ebl, qt, kt, kd = _gla_chunk_terms(qk_ref[rows, :], la_ref[rows, :], d)
                qtb = qt.astype(BF)
                ktb = kt.astype(BF)
                kdb = kd.astype(BF)
                vv = v_ref[rows, :]
                dd = do_ref[rows, :]
                is_last = lax.broadcasted_iota(jnp.int32, (CH, HK), 0) == last
                dq_parts, dk_parts, db_parts = [], [], []
                for h in range(NH):
                    ks = slice(h * HK, (h + 1) * HK)
                    vs = slice(h * HV, (h + 1) * HV)
                    st = se_ref[c, h]
                    dsn = ds_ref[d, h]
                    dsnb = dsn.astype(BF)
                    a = jnp.where(amask, _dot_nt(qtb[:, ks], ktb[:, ks]), 0.0).astype(BF)
                    da = jnp.where(amask, _dot_nt(dd[:, vs], vv[:, vs]), 0.0).astype(BF)
                    dv_ref[rows, vs] = (_dot_tn(a, dd[:, vs]) + _dot_nt(kdb[:, ks], dsnb)).astype(BF)
                    dkd = _dot(vv[:, vs], dsnb)
                    dqt = _dot(da, ktb[:, ks]) + _dot(dd[:, vs], st.astype(BF))
                    dkt = _dot_tn(da, qtb[:, ks])
                    ds_ref[d, h] = _dot_tn(dd[:, vs], qtb[:, ks]) + ebl[:, ks] * dsn
                    debl = jnp.sum(st * dsn, axis=0, keepdims=True)
                    dq_parts.append(dqt * eb[:, ks] * QSCALE)
                    dk_parts.append(dkt * enb[:, ks] + dkd * ekd[:, ks])
                    dkdkd = dkd * kd[:, ks]
                    dbl = jnp.sum(dkdkd, axis=0, keepdims=True) + debl * ebl[:, ks]
                    db_parts.append(dqt * qt[:, ks] - dkt * kt[:, ks] - dkdkd + jnp.where(is_last, dbl, 0.0))
                dqk_ref[rows, :] = jnp.concatenate(dq_parts + dk_parts, axis=1).astype(BF)
                dla_ref[rows, :] = _mask_dot(cum_t.astype(BF), jnp.concatenate(db_parts, axis=1))
            return carry

        lax.fori_loop(0, nc, chunk, 0)

        @pl.when(pl.program_id(0) == nt - 1)
        def _():
            ds0_ref[...] = ds_ref[...]

    up = lambda w, col=0: pl.BlockSpec((tm, w), lambda i: (i, col))
    dn = lambda w, col=0: pl.BlockSpec((tm, w), lambda i: (nt - 1 - i, col))
    se_up = pl.BlockSpec((nc, NH, HV, HK), lambda i: (i, 0, 0, 0))
    se_dn = pl.BlockSpec((nc, NH, HV, HK), lambda i: (nt - 1 - i, 0, 0, 0))
    return _call(
        body, (nt,), "gla_bwd",
        [dn(2 * DK), dn(DV), dn(DK, 0), dn(DV), se_dn, up(2 * DK), up(DV), up(DK, 1), up(DV), se_up],
        [dn(2 * DK), dn(DV), dn(DK), up(2 * DK), up(DV), up(DK), _full((2, NH, HV, HK))],
        [jax.ShapeDtypeStruct((n, 2 * DK), BF), jax.ShapeDtypeStruct((n, DV), BF),
         jax.ShapeDtypeStruct((n, DK), F32), jax.ShapeDtypeStruct((n, 2 * DK), BF),
         jax.ShapeDtypeStruct((n, DV), BF), jax.ShapeDtypeStruct((n, DK), F32),
         jax.ShapeDtypeStruct((2, NH, HV, HK), F32)],
        [pltpu.VMEM((2, NH, HV, HK), F32)],
        (qk, v, la, do, se_f, qk, v, la, do, se_b), exchange, carried)


def _head_norm(o):
    ons, rss = [], []
    for h in range(NH):
        oh = o[:, h * HV:(h + 1) * HV]
        rs = lax.rsqrt(jnp.mean(oh * oh, axis=-1, keepdims=True) + EPS)
        ons.append(oh * rs)
        rss.append(rs)
    return ons, rss


def merge_fwd(x, o_f, o_b, g, co, vecm, gn, wout, tm):
    n = x.shape[0]

    def body(x_ref, of_ref, ob_ref, g_ref, co_ref, vec_ref, gn_ref, w_ref, x1_ref, y1_ref, cat_ref):
        o = of_ref[...] + ob_ref[...]
        ons, _ = _head_norm(o)
        gg = g_ref[...].astype(F32)
        sil = gg * _sigmoid(gg)
        cat_ref[:, :DC] = co_ref[...]
        for h in range(NH):
            vs = slice(h * HV, (h + 1) * HV)
            cat_ref[:, DC + h * HV:DC + (h + 1) * HV] = (ons[h] * gn_ref[:, vs] * sil[:, vs]).astype(BF)
        y1 = _dot(cat_ref[...], w_ref[...])
        y1_ref[...] = y1.astype(BF)
        x1_ref[...] = x_ref[...] + vec_ref[0:1, :] * y1

    tok = lambda w: pl.BlockSpec((tm, w), lambda i: (i, 0))
    return pl.pallas_call(
        body, grid=(n // tm,), name="merge_fwd",
        in_specs=[tok(D), tok(DV), tok(DV), tok(DV), tok(DC), _full(vecm.shape), _full(gn.shape), _full(wout.shape)],
        out_specs=[tok(D), tok(D), tok(D)],
        out_shape=[jax.ShapeDtypeStruct((n, D), F32), jax.ShapeDtypeStruct((n, D), BF), jax.ShapeDtypeStruct((n, D), BF)],
        compiler_params=_cparams(("arbitrary",)),
    )(x, o_f, o_b, g, co, vecm, gn, wout)


def merge_bwd(dx1, y1, o_f, o_b, g, vecm, gn, wout, tm):
    n = dx1.shape[0]

    def body(dx1_ref, y1_ref, of_ref, ob_ref, g_ref, vec_ref, gn_ref, w_ref,
             dy1_ref, dco_ref, do_ref, dg_ref, s1_ref, s2_ref):
        @pl.when(pl.program_id(0) == 0)
        def _():
            s1_ref[...] = jnp.zeros_like(s1_ref)
            s2_ref[...] = jnp.zeros_like(s2_ref)

        dx1 = dx1_ref[...]
        s1_ref[...] += _colsum8(dx1 * y1_ref[...].astype(F32))
        dy1 = (dx1 * vec_ref[0:1, :]).astype(BF)
        dy1_ref[...] = dy1
        dcat = _dot_nt(dy1, w_ref[...])
        dco_ref[...] = dcat[:, :DC].astype(BF)
        o = of_ref[...] + ob_ref[...]
        ons, rss = _head_norm(o)
        gg = g_ref[...].astype(F32)
        sg = _sigmoid(gg)
        sil = gg * sg
        dsil = sg * (1.0 + gg * (1.0 - sg))
        for h in range(NH):
            vs = slice(h * HV, (h + 1) * HV)
            do2 = dcat[:, DC + h * HV:DC + (h + 1) * HV]
            gnh = gn_ref[:, vs]
            t = do2 * sil[:, vs]
            s2_ref[:, vs] += _colsum8(t * ons[h])
            don = t * gnh
            do_ref[:, vs] = (rss[h] * (don - ons[h] * jnp.mean(don * ons[h], axis=-1, keepdims=True))).astype(BF)
            dg_ref[:, vs] = (do2 * ons[h] * gnh * dsil[:, vs]).astype(BF)

    tok = lambda w: pl.BlockSpec((tm, w), lambda i: (i, 0))
    return pl.pallas_call(
        body, grid=(n // tm,), name="merge_bwd",
        in_specs=[tok(D), tok(D), tok(DV), tok(DV), tok(DV), _full(vecm.shape), _full(gn.shape), _full(wout.shape)],
        out_specs=[tok(D), tok(DC), tok(DV), tok(DV), _full((8, D)), _full((8, DV))],
        out_shape=[jax.ShapeDtypeStruct((n, D), BF), jax.ShapeDtypeStruct((n, DC), BF), jax.ShapeDtypeStruct((n, DV), BF),
                   jax.ShapeDtypeStruct((n, DV), BF), jax.ShapeDtypeStruct((8, D), F32), jax.ShapeDtypeStruct((8, DV), F32)],
        compiler_params=_cparams(("arbitrary",)),
    )(dx1, y1, o_f, o_b, g, vecm, gn, wout)


def ffn_fwd_bwd(x1, tgt, vecf, wg, wu, wd, tm):
    n = x1.shape[0]

    def body(x1_ref, t_ref, vec_ref, wg_ref, wu_ref, wd_ref,
             dx1_ref, h2_ref, act_ref, dgt_ref, dup_ref, dy2_ref, s_ref):
        @pl.when(pl.program_id(0) == 0)
        def _():
            s_ref[...] = jnp.zeros_like(s_ref)

        n2g, sh2, sc2, g2, fg = (vec_ref[i:i + 1, :] for i in range(5))
        x1 = x1_ref[...]
        r2 = lax.rsqrt(jnp.mean(x1 * x1, axis=-1, keepdims=True) + EPS)
        xn2 = x1 * r2
        h2 = (xn2 * n2g * (1.0 + sc2) + sh2).astype(BF)
        h2_ref[...] = h2
        gt = _dot(h2, wg_ref[...])
        up = _dot(h2, wu_ref[...])
        sg = _sigmoid(gt)
        sil = gt * sg
        act = (sil * up).astype(BF)
        act_ref[...] = act
        y2 = _dot(act, wd_ref[...])
        x2 = x1 + g2 * y2
        r3 = lax.rsqrt(jnp.mean(x2 * x2, axis=-1, keepdims=True) + EPS)
        xn3 = x2 * r3
        e = xn3 * fg - t_ref[...]
        s_ref[40:48, :] += _colsum8(e * e) * (0.5 / D)
        dyo = e * (1.0 / D)
        s_ref[0:8, :] += _colsum8(dyo * xn3)
        dxn3 = dyo * fg
        dx2 = r3 * (dxn3 - xn3 * jnp.mean(dxn3 * xn3, axis=-1, keepdims=True))
        s_ref[8:16, :] += _colsum8(dx2 * y2)
        dy2 = (dx2 * g2).astype(BF)
        dy2_ref[...] = dy2
        dact = _dot_nt(dy2, wd_ref[...])
        dup = (dact * sil).astype(BF)
        dgt = (dact * up * (sg * (1.0 + gt * (1.0 - sg)))).astype(BF)
        dup_ref[...] = dup
        dgt_ref[...] = dgt
        dh2 = _dot_nt(dgt, wg_ref[...]) + _dot_nt(dup, wu_ref[...])
        s_ref[16:24, :] += _colsum8(dh2)
        t = dh2 * xn2
        s_ref[24:32, :] += _colsum8(t * n2g)
        s_ref[32:40, :] += _colsum8(t * (1.0 + sc2))
        dxn2 = dh2 * ((1.0 + sc2) * n2g)
        dx1_ref[...] = dx2 + r2 * (dxn2 - xn2 * jnp.mean(dxn2 * xn2, axis=-1, keepdims=True))

    tok = lambda w: pl.BlockSpec((tm, w), lambda i: (i, 0))
    wspec = lambda a: pl.BlockSpec(a.shape, lambda i: (0, 0), pipeline_mode=pl.Buffered(1))
    return pl.pallas_call(
        body, grid=(n // tm,), name="ffn_fwd_bwd",
        in_specs=[tok(D), tok(D), _full(vecf.shape), wspec(wg), wspec(wu), wspec(wd)],
        out_specs=[tok(D), tok(D), tok(DFF), tok(DFF), tok(DFF), tok(D), _full((48, D))],
        out_shape=[jax.ShapeDtypeStruct((n, D), F32), jax.ShapeDtypeStruct((n, D), BF), jax.ShapeDtypeStruct((n, DFF), BF),
                   jax.ShapeDtypeStruct((n, DFF), BF), jax.ShapeDtypeStruct((n, DFF), BF), jax.ShapeDtypeStruct((n, D), BF),
                   jax.ShapeDtypeStruct((48, D), F32)],
        compiler_params=_cparams(("arbitrary",)),
    )(x1, tgt, vecf, wg, wu, wd)


def wgrad(a, b, init, t1, t2, tn, name):
    n, k1 = a.shape
    k2 = b.shape[1]

    def body(a_ref, b_ref, *rest):
        o_ref = rest[-1]

        @pl.when(pl.program_id(2) == 0)
        def _():
            o_ref[...] = rest[0][...] if init is not None else jnp.zeros_like(o_ref)

        o_ref[...] += _dot_tn(a_ref[...], b_ref[...])

    ospec = pl.BlockSpec((t1, t2), lambda i, j, k: (i, j))
    extra = ([ospec], {2: 0}, (init,)) if init is not None else ([], {}, ())
    return pl.pallas_call(
        body, grid=(k1 // t1, k2 // t2, n // tn), name=name,
        in_specs=[pl.BlockSpec((tn, t1), lambda i, j, k: (k, i)), pl.BlockSpec((tn, t2), lambda i, j, k: (k, j))] + extra[0],
        out_specs=ospec, out_shape=jax.ShapeDtypeStruct((k1, k2), F32), input_output_aliases=extra[1],
        compiler_params=_cparams(("parallel", "parallel", "arbitrary")),
    )(a, b, *extra[2])


def bwd_in(x, dx1, ag, yb, dco, dqk_f, dqk_b, dv_f, dv_b, dg, dla_f, dla_b, la, r, vec1, win, convw, cvec, wa2, tm):
    n = x.shape[0]
    nseg = tm // GW
    half = DC // 2

    def body(x_ref, dx1_ref, ag_ref, yb_ref, dco_ref, dqkf_ref, dqkb_ref, dvf_ref, dvb_ref, dg_ref, dlaf_ref, dlab_ref,
             la_ref, r_ref, vec_ref, win_ref, cw_ref, cv_ref, wa2_ref,
             gx_ref, h_ref, dp_ref, dwa2_ref, dcw_ref, s_ref, pad1_ref, pad2_ref, dvc_ref, dcw8_ref):
        first = pl.program_id(0) == 0

        @pl.when(first)
        def _():
            s_ref[...] = jnp.zeros_like(s_ref)
            dwa2_ref[...] = jnp.zeros_like(dwa2_ref)
            dcw8_ref[...] = jnp.zeros_like(dcw8_ref)

        yn, rs = _ln_stats(yb_ref[...])
        lng = cv_ref[1:2, :]
        ln = yn * lng + cv_ref[2:3, :]
        sgl = _sigmoid(ln)
        dln = dco_ref[...].astype(F32) * (sgl * (1.0 + ln * (1.0 - sgl)))
        dyn = dln * lng
        dyb = rs * (dyn - jnp.mean(dyn, axis=-1, keepdims=True) - yn * jnp.mean(dyn * yn, axis=-1, keepdims=True))
        s_ref[24:32, 0:DC] += _colsum8(dyb)
        s_ref[24:32, DC:D] += _colsum8(dln * yn)
        s_ref[32:40, 0:DC] += _colsum8(dln)

        agv = ag_ref[...].astype(F32)
        a = agv[:, :DC]
        sgg = _sigmoid(agv[:, DC:])
        zeros = jnp.zeros((nseg, 16, DC), F32)
        for pr, val in ((pad1_ref, a * sgg), (pad2_ref, dyb)):
            pr[:, 0:16, :] = zeros
            pr[:, 16 + GW:SEGP, :] = zeros
            pr[:, 16:16 + GW, :] = val.reshape(nseg, GW, DC)

        def seg(s, carry):
            rows = pl.ds(pl.multiple_of(s * GW, GW), GW)
            for c0 in (0, half):
                cs = pl.ds(c0, half)
                dvc_ref[rows, cs] = _conv_taps(pad2_ref, s, cw_ref, c0, half, True)
                dys = pad2_ref[s, pl.ds(16, GW), cs]
                for j in range(CW):
                    dcw8_ref[j, :, cs] += _colsum8(dys * pad1_ref[s, pl.ds(j + 1, GW), cs])
            return carry

        lax.fori_loop(0, nseg, seg, 0)
        dvc = dvc_ref[...]
        dp_ref[:, 0:DC] = (dvc * sgg).astype(BF)
        dp_ref[:, DC:2 * DC] = (dvc * a * sgg * (1.0 - sgg)).astype(BF)

        dp_ref[:, 2 * DC:2 * DC + 2 * DK] = (dqkf_ref[...].astype(F32) + dqkb_ref[...].astype(F32)).astype(BF)
        dp_ref[:, 2 * DC + 2 * DK:2 * DC + 2 * DK + DV] = (dvf_ref[...].astype(F32) + dvb_ref[...].astype(F32)).astype(BF)
        dp_ref[:, 2 * DC + 2 * DK + DV:2 * DC + 2 * DK + 2 * DV] = dg_ref[...]

        la = la_ref[...]
        dla = jnp.concatenate([dlaf_ref[...], dlab_ref[...]], axis=1)
        dpre = dla * (1.0 - jnp.exp(TAU * la)) * (1.0 / TAU)
        s_ref[32:40, DC:D] += _colsum8(dpre)
        dpreb = dpre.astype(BF)
        dwa2_ref[...] += _dot_tn(r_ref[...], dpreb)
        dp_ref[:, DINP - 128:] = _dot_nt(dpreb, wa2_ref[...]).astype(BF)

        dh = _dot_nt(dp_ref[...], win_ref[...])
        xx = x_ref[...]
        n1g, sh1, sc1 = vec_ref[0:1, :], vec_ref[1:2, :], vec_ref[2:3, :]
        rstd = lax.rsqrt(jnp.mean(xx * xx, axis=-1, keepdims=True) + EPS)
        xn = xx * rstd
        h_ref[...] = (xn * n1g * (1.0 + sc1) + sh1).astype(BF)
        s_ref[0:8, :] += _colsum8(dh)
        t = dh * xn
        s_ref[8:16, :] += _colsum8(t * n1g)
        s_ref[16:24, :] += _colsum8(t * (1.0 + sc1))
        dxn = dh * ((1.0 + sc1) * n1g)
        gx_ref[...] = dx1_ref[...] + rstd * (dxn - xn * jnp.mean(dxn * xn, axis=-1, keepdims=True))

        @pl.when(pl.program_id(0) == pl.num_programs(0) - 1)
        def _():
            dcw_ref[...] = jnp.sum(dcw8_ref[...], axis=1)

    tok = lambda w: pl.BlockSpec((tm, w), lambda i: (i, 0))
    return pl.pallas_call(
        body, grid=(n // tm,), name="bwd_in",
        in_specs=[tok(D), tok(D), tok(2 * DC), tok(DC), tok(DC), tok(2 * DK), tok(2 * DK), tok(DV), tok(DV), tok(DV),
                  tok(DK), tok(DK), tok(2 * DK), tok(128), _full(vec1.shape),
                  pl.BlockSpec(win.shape, lambda i: (0, 0), pipeline_mode=pl.Buffered(1)),
                  _full(convw.shape), _full(cvec.shape), _full(wa2.shape)],
        out_specs=[tok(D), tok(D), tok(DINP), _full((128, 2 * DK)), _full((32, DC)), _full((40, D))],
        out_shape=[jax.ShapeDtypeStruct((n, D), F32), jax.ShapeDtypeStruct((n, D), BF), jax.ShapeDtypeStruct((n, DINP), BF),
                   jax.ShapeDtypeStruct((128, 2 * DK), F32), jax.ShapeDtypeStruct((32, DC), F32),
                   jax.ShapeDtypeStruct((40, D), F32)],
        scratch_shapes=[pltpu.VMEM((nseg, SEGP, DC), F32), pltpu.VMEM((nseg, SEGP, DC), F32), pltpu.VMEM((tm, DC), F32),
                        pltpu.VMEM((32, 8, DC), F32)],
        compiler_params=_cparams(("arbitrary",)),
    )(x, dx1, ag, yb, dco, dqk_f, dqk_b, dv_f, dv_b, dg, dla_f, dla_b, la, r, vec1, win, convw, cvec, wa2)


def _ctx_common(ctx_ref, vec_ref, win_ref, wa2_ref, ba_ref):
    cx = ctx_ref[...]
    t = cx.shape[0]
    rstd = lax.rsqrt(jnp.mean(cx * cx, axis=-1, keepdims=True) + EPS)
    xn = cx * rstd
    hc = (xn * vec_ref[0:1, :] * (1.0 + vec_ref[2:3, :]) + vec_ref[1:2, :]).astype(BF)
    k0 = 2 * DC + DK
    kv = _dot(hc, win_ref[:, k0:k0 + DK + DV]).astype(BF).astype(F32)
    r = _dot(hc, win_ref[:, DINP - 128:]).astype(BF)
    la = _log_sigmoid(_dot(r, wa2_ref[...]) + ba_ref[...]) * (1.0 / TAU)
    incl = _tri(t, "le").astype(BF)
    strict = _tri(t, "lt").astype(BF)
    bf = _mask_dot(incl, la[:, :DK])
    wf = jnp.exp(bf[t - 1:t, :] - bf)
    wb = jnp.exp(_mask_dot(strict, la[:, DK:]))
    return xn, hc, kv[:, :DK], kv[:, DK:], r, la, wf, wb


def ctx_fwd(ctx, vecc, win, wa2, ba):
    def body(ctx_ref, vec_ref, win_ref, wa2_ref, ba_ref, s_ref):
        _, _, k, v, _, _, wf, wb = _ctx_common(ctx_ref, vec_ref, win_ref, wa2_ref, ba_ref)
        vb = v.astype(BF)
        for d, w in enumerate((wf, wb)):
            kd = (k * w).astype(BF)
            for h in range(NH):
                s_ref[d, h] = _dot_tn(vb[:, h * HV:(h + 1) * HV], kd[:, h * HK:(h + 1) * HK])

    return pl.pallas_call(
        body, name="ctx_fwd", out_shape=jax.ShapeDtypeStruct((2, NH, HV, HK), F32),
        compiler_params=pltpu.CompilerParams(vmem_limit_bytes=VMEM_LIMIT),
    )(ctx, vecc, win, wa2, ba)


def ctx_bwd(ctx, vecc, win, wa2, ba, ds0):
    t = ctx.shape[0]

    def body(ctx_ref, vec_ref, win_ref, wa2_ref, ba_ref, ds_ref, dwin_ref, dwa2_ref, s_ref, dpc_ref):
        xn, hc, k, v, r, la, wf, wb = _ctx_common(ctx_ref, vec_ref, win_ref, wa2_ref, ba_ref)
        vb = v.astype(BF)
        strict = _tri(t, "lt").astype(BF)
        strict_t = _tri(t, "gt").astype(BF)
        dpc_ref[...] = jnp.zeros_like(dpc_ref)
        k0 = 2 * DC + DK
        dk = jnp.zeros((t, DK), F32)
        des = []
        for d, w in enumerate((wf, wb)):
            kd = (k * w).astype(BF)
            dkds = []
            for h in range(NH):
                dsb = ds_ref[d, h].astype(BF)
                dkds.append(_dot(vb[:, h * HV:(h + 1) * HV], dsb))
                dvh = _dot_nt(kd[:, h * HK:(h + 1) * HK], dsb)
                vs = slice(k0 + DK + h * HV, k0 + DK + (h + 1) * HV)
                if d == 0:
                    dpc_ref[:, vs] = dvh.astype(BF)
                else:
                    dpc_ref[:, vs] = (dpc_ref[:, vs].astype(F32) + dvh).astype(BF)
            dkd = jnp.concatenate(dkds, axis=1)
            dk = dk + dkd * w
            des.append(dkd * k * w)
        dpc_ref[:, k0:k0 + DK] = dk.astype(BF)
        dla = jnp.concatenate([_mask_dot(strict, des[0]), _mask_dot(strict_t, des[1])], axis=1)
        dpre = dla * (1.0 - jnp.exp(TAU * la)) * (1.0 / TAU)
        dpreb = dpre.astype(BF)
        dwa2_ref[...] = _dot_tn(r, dpreb)
        dpc_ref[:, DINP - 128:] = _dot_nt(dpreb, wa2_ref[...]).astype(BF)
        dpc = dpc_ref[...]
        dwin_ref[...] = _dot_tn(hc, dpc)
        dhc = _dot_nt(dpc, win_ref[...])
        n1g, sc1 = vec_ref[0:1, :], vec_ref[2:3, :]
        tt = dhc * xn
        s_ref[...] = jnp.zeros_like(s_ref)
        s_ref[0:1, :] = jnp.sum(tt * (1.0 + sc1), axis=0, keepdims=True)
        s_ref[1:2, :] = jnp.sum(dhc, axis=0, keepdims=True)
        s_ref[2:3, :] = jnp.sum(tt * n1g, axis=0, keepdims=True)
        s_ref[3:4, DC:D] = jnp.sum(dpre, axis=0, keepdims=True)

    return pl.pallas_call(
        body, name="ctx_bwd",
        out_shape=[jax.ShapeDtypeStruct((D, DINP), F32), jax.ShapeDtypeStruct((128, 2 * DK), F32),
                   jax.ShapeDtypeStruct((8, D), F32)],
        scratch_shapes=[pltpu.VMEM((t, DINP), BF)],
        compiler_params=pltpu.CompilerParams(vmem_limit_bytes=VMEM_LIMIT),
    )(ctx, vecc, win, wa2, ba, ds0)


def _silu(x):
    return x * _sigmoid(x)


def mod_fwd(cext, wm, bm):
    def body(c_ref, w_ref, b_ref, o_ref):
        o_ref[...] = _dot(_silu(c_ref[...]).astype(BF), w_ref[...].astype(BF)) + b_ref[...]

    return pl.pallas_call(body, name="mod_fwd", out_shape=jax.ShapeDtypeStruct((cext.shape[0], wm.shape[1]), F32),
                          compiler_params=pltpu.CompilerParams(vmem_limit_bytes=VMEM_LIMIT))(cext, wm, bm)


def mod_bwd(cext, dm, wm):
    def body(c_ref, d_ref, w_ref, gw_ref, ds_ref):
        dmb = d_ref[...].astype(BF)
        gw_ref[...] = _dot_tn(_silu(c_ref[...]).astype(BF), dmb)
        ds_ref[...] = _dot_nt(dmb, w_ref[...].astype(BF))

    return pl.pallas_call(body, name="mod_bwd",
                          out_shape=[jax.ShapeDtypeStruct(wm.shape, F32), jax.ShapeDtypeStruct(cext.shape, F32)],
                          compiler_params=pltpu.CompilerParams(vmem_limit_bytes=VMEM_LIMIT))(cext, dm, wm)


def pack_small(sf, s1, s2, sd, sc, dcw, dwa2, dwa2_c):
    def body(sf_ref, s1_ref, s2_ref, sd_ref, sc_ref, dcw_ref, dwa2_ref, dwa2c_ref, o_ref, ocw_ref, owa_ref):
        rsum = lambda ref, i: jnp.sum(ref[8 * i:8 * i + 8, :], axis=0, keepdims=True)
        o_ref[...] = jnp.zeros_like(o_ref)
        o_ref[0:1, :] = rsum(sd_ref, 0)
        o_ref[1:2, :] = rsum(sd_ref, 1)
        o_ref[2:3, :] = rsum(s1_ref, 0)
        o_ref[3:4, :] = rsum(sf_ref, 2)
        o_ref[4:5, :] = rsum(sf_ref, 3)
        o_ref[5:6, :] = rsum(sf_ref, 1)
        o_ref[6:7, :] = sc_ref[1:2, :]
        o_ref[7:8, :] = sc_ref[2:3, :]
        o_ref[8:9, :] = rsum(sd_ref, 2) + sc_ref[0:1, :]
        o_ref[9:10, :] = rsum(sf_ref, 4)
        o_ref[10:11, :] = rsum(sf_ref, 0)
        o_ref[11:12, :] = rsum(sd_ref, 3)
        o_ref[12:13, :] = rsum(sd_ref, 4) + sc_ref[3:4, :]
        g = jnp.sum(s2_ref[...], axis=0, keepdims=True)
        o_ref[13:14, 0:HV] = g[:, 0:HV] + g[:, HV:2 * HV] + g[:, 2 * HV:3 * HV] + g[:, 3 * HV:4 * HV]
        o_ref[14:15, :] = rsum(sf_ref, 5)
        ocw_ref[...] = dcw_ref[...]
        owa_ref[...] = dwa2_ref[0:32, :] + dwa2c_ref[0:32, :]

    return pl.pallas_call(body, name="pack_small",
                          out_shape=[jax.ShapeDtypeStruct((16, D), F32), jax.ShapeDtypeStruct((32, DC), F32),
                                     jax.ShapeDtypeStruct((32, 2 * DK), F32)])(sf, s1, s2, sd, sc, dcw, dwa2, dwa2_c)


def small_totals(g8):
    r = g8.shape[1]

    def body(g_ref, t_ref, bm_ref, loss_ref):
        acc = g_ref[0]
        for i in range(1, NDEV):
            acc = acc + g_ref[i]
        t_ref[...] = acc
        bm_ref[...] = jnp.zeros_like(bm_ref)
        bm_ref[0:6, :] = acc[0:6, :]
        bm_ref[0:2, :] += acc[6:8, :]
        loss_ref[...] = jnp.broadcast_to(jnp.sum(acc[14:15, :], axis=1, keepdims=True), loss_ref.shape)

    return pl.pallas_call(body, name="small_totals",
                          out_shape=[jax.ShapeDtypeStruct((r, D), F32), jax.ShapeDtypeStruct((8, D), F32),
                                     jax.ShapeDtypeStruct((8, 128), F32)])(g8)


def cctx_grad(p8, c_ctx_row):
    def body(p_ref, c_ref, o_ref):
        acc = p_ref[0, 0:1, :]
        for j in range(1, NCHIP):
            acc = acc + p_ref[2 * j, 0:1, :]
        cc = c_ref[0:1, :]
        sg = _sigmoid(cc)
        o_ref[...] = jnp.zeros_like(o_ref)
        o_ref[0:1, :] = acc * (sg * (1.0 + cc * (1.0 - sg)))

    return pl.pallas_call(body, name="cctx_grad", out_shape=jax.ShapeDtypeStruct((8, D), F32))(p8, c_ctx_row)


def adamw(w, g, m, v, rows, name):
    r, c = w.shape

    def body(w_ref, g_ref, m_ref, v_ref, d_ref, nm_ref, nv_ref):
        gg = g_ref[...]
        nm = ADAM_B1 * m_ref[...] + (1.0 - ADAM_B1) * gg
        nv = ADAM_B2 * v_ref[...] + (1.0 - ADAM_B2) * (gg * gg)
        m_hat = nm / (1.0 - ADAM_B1 ** ADAM_STEP)
        v_hat = nv / (1.0 - ADAM_B2 ** ADAM_STEP)
        d_ref[...] = -ADAM_LR * (m_hat / (jnp.sqrt(v_hat) + ADAM_EPS) + ADAM_WD * w_ref[...])
        nm_ref[...] = nm
        nv_ref[...] = nv

    spec = pl.BlockSpec((rows, c), lambda i: (i, 0))
    sds = jax.ShapeDtypeStruct((r, c), F32)
    return pl.pallas_call(
        body, grid=(r // rows,), name=name, in_specs=[spec] * 4, out_specs=[spec] * 3, out_shape=[sds] * 3,
        compiler_params=_cparams(("parallel",)),
    )(w, g, m, v)


def _me():
    return lax.axis_index("x"), lax.axis_index("y"), lax.axis_index("c")


def _flip(v, bit):
    return 1 - v if bit else v


ANY = pl.BlockSpec(memory_space=pl.ANY)


def all_gather8(x, name):
    r, c = x.shape

    def body(x_ref, o_ref, ssem, rsem, lsem):
        mx, my, mc = _me()
        me = 4 * mx + 2 * my + mc
        local = pltpu.make_async_copy(x_ref, o_ref.at[me], lsem)
        local.start()

        def copy(k):
            px, py, pc = _flip(mx, k & 4), _flip(my, k & 2), _flip(mc, k & 1)
            return px, py, pc

        sends = []
        for k in range(1, NDEV):
            cp = pltpu.make_async_remote_copy(src_ref=x_ref, dst_ref=o_ref.at[me], send_sem=ssem.at[k - 1],
                                              recv_sem=rsem.at[k - 1], device_id=copy(k), device_id_type=MESH)
            cp.start()
            sends.append(cp)
        for k in range(1, NDEV):
            px, py, pc = copy(k)
            pltpu.make_async_remote_copy(src_ref=x_ref, dst_ref=o_ref.at[4 * px + 2 * py + pc], send_sem=ssem.at[k - 1],
                                         recv_sem=rsem.at[k - 1], device_id=(px, py, pc), device_id_type=MESH).wait_recv()
        for cp in sends:
            cp.wait_send()
        local.wait()

    vm = pl.BlockSpec(memory_space=pltpu.VMEM)
    return pl.pallas_call(
        body, name=name, in_specs=[vm], out_specs=vm, out_shape=jax.ShapeDtypeStruct((NDEV, r, c), x.dtype),
        scratch_shapes=[pltpu.SemaphoreType.DMA((NDEV - 1,)), pltpu.SemaphoreType.DMA((NDEV - 1,)), pltpu.SemaphoreType.DMA],
    )(x)


def _chip_peers(mx, my):
    out = []
    for p in range(1, NCHIP):
        px, py = _flip(mx, p & 2), _flip(my, p & 1)
        out.append((px, py, 2 * px + py))
    return out


class ChipExchange:
    def __init__(self, kind, arrays):
        self.kind = kind
        self.n = len(arrays)
        if kind == "gather":
            self.out_shape = [jax.ShapeDtypeStruct((NCHIP,) + a.shape, a.dtype) for a in arrays]
        else:
            self.out_shape = [jax.ShapeDtypeStruct(a.shape, a.dtype) for a in arrays]
        self.scratch = [pltpu.SemaphoreType.DMA((3 * self.n,)), pltpu.SemaphoreType.DMA((3 * self.n,)),
                        pltpu.SemaphoreType.DMA((self.n,))]

    def _copies(self, ins, outs, sems):
        ssem, rsem, lsem = sems
        mx, my, mc = _me()
        jme = 2 * mx + my
        gather = self.kind == "gather"
        local, sends, waits = [], [], []
        for k in range(self.n):
            local.append(pltpu.make_async_copy(ins[k] if gather else ins[k].at[jme], outs[k].at[jme], lsem.at[k]))
            for p, (px, py, jp) in enumerate(_chip_peers(mx, my)):
                src = ins[k] if gather else ins[k].at[jp]
                sem = dict(send_sem=ssem.at[3 * k + p], recv_sem=rsem.at[3 * k + p], device_id=(px, py, mc),
                           device_id_type=MESH)
                sends.append(pltpu.make_async_remote_copy(src_ref=src, dst_ref=outs[k].at[jme], **sem))
                waits.append(pltpu.make_async_remote_copy(src_ref=src, dst_ref=outs[k].at[jp], **sem))
        return local, sends, waits

    def start(self, ins, outs, sems):
        local, sends, _ = self._copies(ins, outs, sems)
        for cp in local + sends:
            cp.start()

    def wait(self, ins, outs, sems):
        local, _, waits = self._copies(ins, outs, sems)
        for cp in waits:
            cp.wait_recv()
        for cp in waits:
            cp.wait_send()
        for cp in local:
            cp.wait()


def chip_exchange(kind, arrays, name):
    ex = ChipExchange(kind, arrays)
    n = ex.n

    def body(*refs):
        ins, outs, sems = refs[:n], refs[n:2 * n], refs[2 * n:]
        ex.start(ins, outs, sems)
        ex.wait(ins, outs, sems)

    return pl.pallas_call(body, name=name, in_specs=[ANY] * n, out_specs=[ANY] * n, out_shape=ex.out_shape,
                          scratch_shapes=ex.scratch)(*arrays)


def sibling_add(g, ngrp, hr, tr, name):
    c_ = g.shape[1]
    nt = hr // tr

    def body(cidx, keep_ref, give_ref, o_ref, land, ssem, rsem):
        mx, my, mc = _me()
        t = pl.program_id(0) * nt + pl.program_id(1)
        s = t % 2
        cp = pltpu.make_async_remote_copy(src_ref=give_ref, dst_ref=land.at[s], send_sem=ssem.at[s], recv_sem=rsem.at[s],
                                          device_id=(mx, my, 1 - mc), device_id_type=MESH)
        cp.start()
        cp.wait_recv()
        o_ref[...] = keep_ref[...] + land[s]
        cp.wait_send()

    grid_spec = pltpu.PrefetchScalarGridSpec(
        num_scalar_prefetch=1, grid=(ngrp, nt),
        in_specs=[pl.BlockSpec((tr, c_), lambda i, j, cr: ((2 * i + cr[0]) * nt + j, 0)),
                  pl.BlockSpec((tr, c_), lambda i, j, cr: ((2 * i + 1 - cr[0]) * nt + j, 0))],
        out_specs=pl.BlockSpec((tr, c_), lambda i, j, cr: (i * nt + j, 0)),
        scratch_shapes=[pltpu.VMEM((2, tr, c_), F32), pltpu.SemaphoreType.DMA((2,)), pltpu.SemaphoreType.DMA((2,))])
    cidx = lax.axis_index("c").astype(jnp.int32).reshape(1)
    return pl.pallas_call(body, grid_spec=grid_spec, name=name, out_shape=jax.ShapeDtypeStruct((ngrp * hr, c_), F32),
                          compiler_params=_cparams(("arbitrary", "arbitrary")))(cidx, g, g)


def finish_weight(b, w, m, v, tr, name):
    _, r2, c_ = b.shape

    def body(b_ref, w_ref, m_ref, v_ref, g_ref, d_ref, nm_ref, nv_ref, mine, land, ssem, rsem):
        mx, my, mc = _me()
        t = pl.program_id(0)
        s = t % 2
        mine[s] = (b_ref[0] + b_ref[1]) + (b_ref[2] + b_ref[3])
        cp = pltpu.make_async_remote_copy(src_ref=mine.at[s], dst_ref=land.at[s], send_sem=ssem.at[s], recv_sem=rsem.at[s],
                                          device_id=(mx, my, 1 - mc), device_id_type=MESH)
        cp.start()
        cp.wait_recv()
        g_ref[mc] = mine[s]
        g_ref[1 - mc] = land[s]
        cp.wait_send()
        gg = g_ref[...]
        nm = ADAM_B1 * m_ref[...] + (1.0 - ADAM_B1) * gg
        nv = ADAM_B2 * v_ref[...] + (1.0 - ADAM_B2) * (gg * gg)
        m_hat = nm / (1.0 - ADAM_B1 ** ADAM_STEP)
        v_hat = nv / (1.0 - ADAM_B2 ** ADAM_STEP)
        d_ref[...] = -ADAM_LR * (m_hat / (jnp.sqrt(v_hat) + ADAM_EPS) + ADAM_WD * w_ref[...])
        nm_ref[...] = nm
        nv_ref[...] = nv

    spec = pl.BlockSpec((2, tr, c_), lambda i: (0, i, 0))
    sds = jax.ShapeDtypeStruct((2, r2, c_), F32)
    return pl.pallas_call(
        body, grid=(r2 // tr,), name=name,
        in_specs=[pl.BlockSpec((NCHIP, tr, c_), lambda i: (0, i, 0)), spec, spec, spec],
        out_specs=[spec] * 4, out_shape=[sds] * 4,
        scratch_shapes=[pltpu.VMEM((2, tr, c_), F32), pltpu.VMEM((2, tr, c_), F32), pltpu.SemaphoreType.DMA((2,)),
                        pltpu.SemaphoreType.DMA((2,))],
        compiler_params=_cparams(("arbitrary",)))(b, w, m, v)


TM_IN = 256
TM_GLA = 512
TM_MERGE = 512
TM_FFN = 256
TN_WGRAD = 1024

WEIGHTS = ['c_ctx', 'w_mod', 'b_mod', 'norm1_g', 'norm2_g', 'w_in', 'conv_w', 'conv_b', 'conv_ln_g', 'conv_ln_b', 'w_a2_f',
           'b_a_f', 'w_a2_b', 'b_a_b', 'gla_norm_g', 'w_out', 'w_gate', 'w_up', 'w_down', 'final_g']
BIG = ['w_in', 'w_out', 'w_gate', 'w_up', 'w_down']


def _rows(*vs):
    out = jnp.zeros((8, vs[0].shape[-1]), F32)
    for i, v in enumerate(vs):
        out = out.at[i].set(v.reshape(-1))
    return out


def _small_slab(p):
    cat = lambda *ks: jnp.concatenate([p[k].reshape(-1) for k in ks])
    vecs = _rows(p['c_ctx'], p['norm1_g'], p['norm2_g'], p['final_g'], cat('conv_b', 'conv_ln_g'),
                 cat('conv_ln_b', 'b_a_f', 'b_a_b'), jnp.pad(p['gla_norm_g'].reshape(-1), (0, D - HV)))
    bmod = jnp.pad(p['b_mod'].reshape(6, D), ((0, 2), (0, 0)))
    shards = jnp.pad(jnp.concatenate([jnp.pad(p['conv_w'].reshape(-1), (0, DC // NCHIP)), cat('w_a2_f', 'w_a2_b')]),
                     (0, 2 * D)).reshape(8, D)
    return jnp.concatenate([vecs, bmod, shards], axis=0)


def _unslab(s):
    return {
        'c_ctx': s[0], 'norm1_g': s[1:2], 'norm2_g': s[2:3], 'final_g': s[3],
        'conv_b': s[4:5, :DC], 'conv_ln_g': s[4:5, DC:], 'conv_ln_b': s[5:6, :DC],
        'b_a_f': s[5:6, DC:DC + DK], 'b_a_b': s[5:6, DC + DK:], 'gla_norm_g': s[6:7, :HV],
        'b_mod': s[8:14].reshape(1, 6 * D),
        'conv_w': s[16:20].reshape(32, DC // NCHIP)[:CW].reshape(1, CW, DC // NCHIP),
        'w_a2_f': s[20].reshape(1, RANK, DK // NCHIP), 'w_a2_b': s[21].reshape(1, RANK, DK // NCHIP),
    }


def kernel(x, c, ctx, c_ctx, w_mod, b_mod, norm1_g, norm2_g, w_in, conv_w, conv_b, conv_ln_g, conv_ln_b, w_a2_f, b_a_f, w_a2_b, b_a_b, gla_norm_g, w_out, w_gate, w_up, w_down, final_g, loss_target, m_c_ctx, m_w_mod, m_b_mod, m_norm1_g, m_norm2_g, m_w_in, m_conv_w, m_conv_b, m_conv_ln_g, m_conv_ln_b, m_w_a2_f, m_b_a_f, m_w_a2_b, m_b_a_b, m_gla_norm_g, m_w_out, m_w_gate, m_w_up, m_w_down, m_final_g, v_c_ctx, v_w_mod, v_b_mod, v_norm1_g, v_norm2_g, v_w_in, v_conv_w, v_conv_b, v_conv_ln_g, v_conv_ln_b, v_w_a2_f, v_b_a_f, v_w_a2_b, v_b_a_b, v_gla_norm_g, v_w_out, v_w_gate, v_w_up, v_w_down, v_final_g):
    w = dict(c_ctx=c_ctx, w_mod=w_mod, b_mod=b_mod, norm1_g=norm1_g, norm2_g=norm2_g, w_in=w_in, conv_w=conv_w, conv_b=conv_b,
             conv_ln_g=conv_ln_g, conv_ln_b=conv_ln_b, w_a2_f=w_a2_f, b_a_f=b_a_f, w_a2_b=w_a2_b, b_a_b=b_a_b,
             gla_norm_g=gla_norm_g, w_out=w_out, w_gate=w_gate, w_up=w_up, w_down=w_down, final_g=final_g)
    m = dict(c_ctx=m_c_ctx, w_mod=m_w_mod, b_mod=m_b_mod, norm1_g=m_norm1_g, norm2_g=m_norm2_g, w_in=m_w_in, conv_w=m_conv_w,
             conv_b=m_conv_b, conv_ln_g=m_conv_ln_g, conv_ln_b=m_conv_ln_b, w_a2_f=m_w_a2_f, b_a_f=m_b_a_f, w_a2_b=m_w_a2_b,
             b_a_b=m_b_a_b, gla_norm_g=m_gla_norm_g, w_out=m_w_out, w_gate=m_w_gate, w_up=m_w_up, w_down=m_w_down,
             final_g=m_final_g)
    v = dict(c_ctx=v_c_ctx, w_mod=v_w_mod, b_mod=v_b_mod, norm1_g=v_norm1_g, norm2_g=v_norm2_g, w_in=v_w_in, conv_w=v_conv_w,
             conv_b=v_conv_b, conv_ln_g=v_conv_ln_g, conv_ln_b=v_conv_ln_b, w_a2_f=v_w_a2_f, b_a_f=v_b_a_f, w_a2_b=v_w_a2_b,
             b_a_b=v_b_a_b, gla_norm_g=v_gla_norm_g, w_out=v_w_out, w_gate=v_w_gate, w_up=v_w_up, w_down=v_w_down,
             final_g=v_final_g)
    mx, my, mc = _me()
    jme = 2 * mx + my
    me = 4 * mx + 2 * my + mc
    wmc = D * 6 // NCHIP
    xx, tgt, cx = x[0], loss_target[0], ctx[0]
    n = xx.shape[0]

    sw = jnp.concatenate([jnp.pad(conv_w[0], ((0, 1), (0, 0))).reshape(-1), w_a2_f[0].reshape(-1), w_a2_b[0].reshape(-1)])
    cs8 = all_gather8(jnp.concatenate([_rows(c[0]), jnp.pad(sw.reshape(6, D), ((0, 2), (0, 0)))], axis=0), "gather_c_small_w")
    c8 = cs8[:, 0, :]
    swc = jnp.stack([cs8[2 * j, 8:16] for j in range(NCHIP)]).reshape(NCHIP, 8 * D)
    convw = jnp.transpose(swc[:, :32 * 128].reshape(NCHIP, 32, 128), (1, 0, 2)).reshape(32, DC)
    a2 = lambda o: jnp.transpose(swc[:, o:o + RANK * 64].reshape(NCHIP, RANK, 64), (1, 0, 2)).reshape(RANK, DK)
    wa2 = jnp.zeros((128, 2 * DK), F32).at[0:RANK, 0:DK].set(a2(32 * 128)).at[RANK:2 * RANK, DK:].set(a2(32 * 128 + RANK * 64))
    wa2 = wa2.astype(BF)

    cext = jnp.concatenate([c8, _rows(c_ctx)], axis=0)
    mloc = mod_fwd(cext, w_mod[0], lax.dynamic_slice_in_dim(b_mod, jme * wmc, wmc, axis=1))
    mall = all_gather8(mloc, "gather_mod")
    mall = jnp.concatenate([mall[2 * j] for j in range(NCHIP)], axis=1)
    sh1, sc1, g1, sh2, sc2, g2 = jnp.split(lax.dynamic_slice_in_dim(mall, me, 1, axis=0)[0], 6)
    csh1, csc1 = mall[8, :D], mall[8, D:2 * D]

    bshard = [w[k][0].astype(BF) for k in BIG]
    cols = lambda a: jnp.transpose(a, (1, 0, 2)).reshape(a.shape[1], -1)
    win = jnp.pad(cols(chip_exchange("gather", bshard[:1], "gather_w_in")[0]), ((0, 0), (0, DINP - DIN)))
    ba = jnp.concatenate([b_a_f, b_a_b], axis=1)
    cvec = _rows(conv_b, conv_ln_g, conv_ln_b)
    vec1 = _rows(norm1_g, sh1, sc1)
    vecc = _rows(norm1_g, csh1, csc1)
    vecm = _rows(g1)
    vecf = _rows(norm2_g, sh2, sc2, g2, final_g)
    gn = jnp.tile(gla_norm_g, (1, NH))

    s0 = ctx_fwd(cx, vecc, win, wa2, ba)
    res = fwd_in(xx, vec1, win, convw, cvec, wa2, ba, TM_IN, ChipExchange("gather", bshard[1:]), bshard[1:])
    ag, yb, co, qk, vv, gg, la, r = res[:8]
    wout = res[8].reshape(D, D)
    wg, wu = cols(res[9]), cols(res[10])
    wd = res[11].reshape(DFF, D)
    o_f, o_b, se_f, se_b = gla_fwd(qk, vv, la, s0, TM_GLA)
    x1, y1, cat = merge_fwd(xx, o_f, o_b, gg, co, vecm, gn, wout, TM_MERGE)

    dx1, h2, act, dgt, dup, dy2, sf = ffn_fwd_bwd(x1, tgt, vecf, wg, wu, wd, TM_FFN)
    d_wg = wgrad(h2, dgt, None, D, DFF // 2, TN_WGRAD, "wgrad_gate")
    d_wu = wgrad(h2, dup, None, D, DFF // 2, TN_WGRAD, "wgrad_up")
    d_wd = wgrad(act, dy2, None, DFF // 2, D, TN_WGRAD, "wgrad_down")
    dy1, dco, do, dg, s1, s2 = merge_bwd(dx1, y1, o_f, o_b, gg, vecm, gn, wout, TM_MERGE)
    d_wout = wgrad(cat, dy1, None, D, D, TN_WGRAD, "wgrad_out")

    shard = lambda a, k: jnp.transpose(a.reshape(a.shape[0], NCHIP, k), (1, 0, 2))
    hd = D // 2
    parts = [sibling_add(d_wout, NCHIP, hd // NCHIP, hd // NCHIP, "xadd_w_out").reshape(NCHIP, hd // NCHIP, D),
             shard(sibling_add(d_wg, 1, hd, 128, "xadd_w_gate"), DFF // NCHIP),
             shard(sibling_add(d_wu, 1, hd, 128, "xadd_w_up"), DFF // NCHIP),
             sibling_add(d_wd, NCHIP, DFF // 8, DFF // 16, "xadd_w_down").reshape(NCHIP, DFF // 8, D)]
    res = gla_bwd(qk, vv, la, do, se_f, se_b, TM_GLA, ChipExchange("scatter", parts), parts)
    dqk_f, dv_f, dla_f, dqk_b, dv_b, dla_b, ds0 = res[:7]
    recv = list(res[7:])
    dwin_c, dwa2_c, sc = ctx_bwd(cx, vecc, win, wa2, ba, ds0)
    grad_x, h, dp, dwa2, dcw, sd = bwd_in(xx, dx1, ag, yb, dco, dqk_f, dqk_b, dv_f, dv_b, dg, dla_f, dla_b, la, r,
                                          vec1, win, convw, cvec, wa2, TM_IN)
    d_win = wgrad(h, dp, dwin_c, D, DINP // 3, TN_WGRAD, "wgrad_in")
    part_in = shard(sibling_add(d_win, 1, hd, 128, "xadd_w_in")[:, :DIN], DIN // NCHIP)
    recv = list(chip_exchange("scatter", [part_in], "scatter_w_in")) + recv

    rows16, dcw_t, dwa2_t = pack_small(sf, s1, s2, sd, sc, dcw, dwa2, dwa2_c)
    sp = jnp.concatenate([rows16, dcw_t.reshape(16, D), dwa2_t.reshape(16, D)], axis=0)
    g8 = all_gather8(sp, "gather_small_grads")
    tot, bm_g, loss8 = small_totals(g8)
    loss = loss8[0, 0]
    dmod8 = g8[:, 0:6, :].reshape(NDEV, 6 * D)
    dmodc = jnp.concatenate([tot[6], tot[7], jnp.zeros((4 * D,), F32)])
    dm = jnp.concatenate([dmod8, _rows(dmodc)], axis=0)
    dm = lax.dynamic_slice_in_dim(dm, jme * wmc, wmc, axis=1)
    g_wmod, dsil = mod_bwd(cext, dm, w_mod[0])
    p8 = all_gather8(dsil[8:16], "gather_dsilu")
    g_cctx = cctx_grad(p8, _rows(c_ctx))[0]

    grads, delta, new_m, new_v = {}, {}, {}, {}
    for i, k in enumerate(BIG):
        r2, cc = recv[i].shape[1:]
        halves = lambda a: a[0].reshape(2, r2, cc)
        outs = finish_weight(recv[i], halves(w[k]), halves(m[k]), halves(v[k]), 88 if r2 % 128 else 128, "finish_" + k)
        grads[k], delta[k], new_m[k], new_v[k] = (o.reshape(w[k].shape) for o in outs)
    grads['w_mod'] = g_wmod[None]
    d_, m_, v_ = adamw(w_mod[0], g_wmod, m_w_mod[0], v_w_mod[0], 128, "adamw_w_mod")
    delta['w_mod'], new_m['w_mod'], new_v['w_mod'] = d_[None], m_[None], v_[None]
    small_g = {
        'c_ctx': g_cctx, 'b_mod': bm_g[0:6].reshape(1, 6 * D), 'norm1_g': tot[8:9], 'norm2_g': tot[9:10], 'final_g': tot[10],
        'conv_b': tot[11:12, :DC], 'conv_ln_g': tot[11:12, DC:], 'conv_ln_b': tot[12:13, :DC],
        'b_a_f': tot[12:13, DC:DC + DK], 'b_a_b': tot[12:13, DC + DK:], 'gla_norm_g': tot[13:14, :HV],
        'conv_w': lax.dynamic_slice_in_dim(tot[16:32].reshape(32, DC)[:CW], jme * (DC // NCHIP), DC // NCHIP, axis=1)[None],
        'w_a2_f': lax.dynamic_slice_in_dim(tot[32:48].reshape(32, 2 * DK)[0:RANK, 0:DK], jme * (DK // NCHIP), DK // NCHIP, axis=1)[None],
        'w_a2_b': lax.dynamic_slice_in_dim(tot[32:48].reshape(32, 2 * DK)[RANK:2 * RANK, DK:], jme * (DK // NCHIP), DK // NCHIP, axis=1)[None],
    }
    grads.update(small_g)
    sd_, sm_, sv_ = adamw(_small_slab(w), _small_slab(small_g), _small_slab(m), _small_slab(v), 24,
                          "adamw_small")
    for dst, slab in ((delta, sd_), (new_m, sm_), (new_v, sv_)):
        dst.update(_unslab(slab))
    out = [loss, grad_x[None]]
    for group in (grads, delta, new_m, new_v):
        out += [group[k].reshape(w[k].shape) for k in WEIGHTS]
    return tuple(out)
```

```python
import functools

import jax
import jax.numpy as jnp
from jax import lax
from jax.experimental import pallas as pl
from jax.experimental.pallas import tpu as pltpu

F32 = jnp.float32
BF = jnp.bfloat16

D = 1024
DC = 512
NH = 4
HK = 64
HV = 128
DK = NH * HK
DV = NH * HV
RANK = 16
CH = 64
GW = 64
CW = 31
CPAD = CW // 2
SEGP = GW + 32
DFF = 2816
DIN = 2592
DINP = 2688
EPS = 1e-6
TAU = 16.0
QSCALE = HK ** -0.5
NCHIP = 4
NDEV = 8

ADAM_LR = 0.001
ADAM_B1 = 0.9
ADAM_B2 = 0.999
ADAM_EPS = 1e-08
ADAM_WD = 0.01
ADAM_STEP = 10

VMEM_LIMIT = 56 * 1024 * 1024
MESH = pl.DeviceIdType.MESH


def _dot(a, b):
    return jnp.dot(a, b, preferred_element_type=F32)


def _dot_nt(a, b):
    return lax.dot_general(a, b, (((1,), (1,)), ((), ())), preferred_element_type=F32)


def _dot_tn(a, b):
    return lax.dot_general(a, b, (((0,), (0,)), ((), ())), preferred_element_type=F32)


def _split3(x):
    hi = x.astype(BF)
    r1 = x - hi.astype(F32)
    mid = r1.astype(BF)
    lo = (r1 - mid.astype(F32)).astype(BF)
    return hi, mid, lo


def _mask_dot(t, x):
    hi, mid, lo = _split3(x)
    return _dot(t, hi) + _dot(t, mid) + _dot(t, lo)


def _sigmoid(x):
    return 1.0 / (1.0 + jnp.exp(-x))


def _log_sigmoid(x):
    return jnp.minimum(x, 0.0) - jnp.log(1.0 + jnp.exp(-jnp.abs(x)))


def _colsum8(z):
    t, c = z.shape
    return jnp.sum(z.reshape(t // 8, 8, c), axis=0)


def _tri(n, kind):
    r = lax.broadcasted_iota(jnp.int32, (n, n), 0)
    c = lax.broadcasted_iota(jnp.int32, (n, n), 1)
    m = {"le": c <= r, "lt": c < r, "ge": c >= r, "gt": c > r}[kind]
    return m


def _full(shape):
    nd = len(shape)
    return pl.BlockSpec(shape, lambda *_: (0,) * nd)


def _cparams(sem, vmem=VMEM_LIMIT):
    return pltpu.CompilerParams(dimension_semantics=sem, vmem_limit_bytes=vmem)


def _call(body, grid, name, in_specs, out_specs, out_shape, scratch, operands, exchange=None, carried=()):
    n_in, n_out, n_scr = len(in_specs), len(out_specs), len(scratch)
    if exchange is None:
        fn = body
    else:
        n = exchange.n

        def fn(*refs):
            ins, cin = refs[:n_in], refs[n_in:n_in + n]
            outs, cout = refs[n_in + n:n_in + n + n_out], refs[n_in + n + n_out:n_in + 2 * n + n_out]
            rest = refs[n_in + 2 * n + n_out:]
            scr, sems = rest[:n_scr], rest[n_scr:]

            @pl.when(pl.program_id(0) == 0)
            def _():
                exchange.start(cin, cout, sems)

            body(*ins, *outs, *scr)

            @pl.when(pl.program_id(0) == pl.num_programs(0) - 1)
            def _():
                exchange.wait(cin, cout, sems)

        any_spec = pl.BlockSpec(memory_space=pl.ANY)
        in_specs = list(in_specs) + [any_spec] * n
        out_specs = list(out_specs) + [any_spec] * n
        out_shape = list(out_shape) + exchange.out_shape
        scratch = list(scratch) + exchange.scratch
    return pl.pallas_call(fn, grid=grid, name=name, in_specs=in_specs, out_specs=out_specs, out_shape=out_shape,
                          scratch_shapes=scratch, compiler_params=_cparams(("arbitrary",)))(*operands, *carried)


def _fill_padded(pad_ref, val, nseg):
    zeros = jnp.zeros((nseg, 16, val.shape[-1]), F32)
    pad_ref[:, 0:16, :] = zeros
    pad_ref[:, 16 + GW:SEGP, :] = zeros
    pad_ref[:, 16:16 + GW, :] = val.reshape(nseg, GW, val.shape[-1])


def _tap_slabs(pad_ref, s, cs):
    whole = pad_ref[s, :, cs]
    for r in range(8):
        slab = whole if r == 0 else pltpu.roll(whole, SEGP - r, axis=0)
        for a in range(4):
            j = r + 8 * a - 1
            if 0 <= j < CW:
                yield j, slab[8 * a:8 * a + GW]


def _conv_taps(pad_ref, s, w_ref, c0, cw, flip):
    acc = jnp.zeros((GW, cw), F32)
    for j, rows in _tap_slabs(pad_ref, s, pl.ds(c0, cw)):
        acc = acc + w_ref[pl.ds((CW - 1 - j) if flip else j, 1), pl.ds(c0, cw)] * rows
    return acc


def _ln_stats(yb):
    mu = jnp.mean(yb, axis=-1, keepdims=True)
    yc = yb - mu
    var = jnp.mean(yc * yc, axis=-1, keepdims=True)
    rs = lax.rsqrt(var + EPS)
    return yc * rs, rs


def fwd_in(x, vec1, win, convw, cvec, wa2, ba, tm, exchange=None, carried=()):
    n = x.shape[0]
    nseg = tm // GW
    cg = 128

    def body(x_ref, vec_ref, win_ref, cw_ref, cv_ref, wa2_ref, ba_ref,
             ag_ref, yb_ref, co_ref, qk_ref, v_ref, g_ref, la_ref, r_ref, pad_ref):
        xx = x_ref[...]
        rstd = lax.rsqrt(jnp.mean(xx * xx, axis=-1, keepdims=True) + EPS)
        h = (xx * rstd * vec_ref[0:1, :]) * (1.0 + vec_ref[2:3, :]) + vec_ref[1:2, :]
        p = _dot(h.astype(BF), win_ref[...])
        ag_ref[...] = p[:, :2 * DC].astype(BF)
        qk_ref[...] = p[:, 2 * DC:2 * DC + 2 * DK].astype(BF)
        v_ref[...] = p[:, 2 * DC + 2 * DK:2 * DC + 2 * DK + DV].astype(BF)
        g_ref[...] = p[:, 2 * DC + 2 * DK + DV:2 * DC + 2 * DK + 2 * DV].astype(BF)
        r = p[:, DINP - 128:].astype(BF)
        r_ref[...] = r
        la_ref[...] = _log_sigmoid(_dot(r, wa2_ref[...]) + ba_ref[...]) * (1.0 / TAU)

        _fill_padded(pad_ref, p[:, :DC] * _sigmoid(p[:, DC:2 * DC]), nseg)

        def seg(s, carry):
            for c0 in range(0, DC, cg):
                y = _conv_taps(pad_ref, s, cw_ref, c0, cg, False)
                yb_ref[pl.ds(pl.multiple_of(s * GW, GW), GW), pl.ds(c0, cg)] = y + cv_ref[0:1, c0:c0 + cg]
            return carry

        lax.fori_loop(0, nseg, seg, 0)
        yn, _ = _ln_stats(yb_ref[...])
        ln = yn * cv_ref[1:2, :] + cv_ref[2:3, :]
        co_ref[...] = (ln * _sigmoid(ln)).astype(BF)

    tok = lambda w: pl.BlockSpec((tm, w), lambda i: (i, 0))
    return _call(
        body, (n // tm,), "fwd_in",
        [tok(D), _full(vec1.shape), _full(win.shape), _full(convw.shape), _full(cvec.shape), _full(wa2.shape), _full(ba.shape)],
        [tok(2 * DC), tok(DC), tok(DC), tok(2 * DK), tok(DV), tok(DV), tok(2 * DK), tok(128)],
        [jax.ShapeDtypeStruct((n, 2 * DC), BF), jax.ShapeDtypeStruct((n, DC), F32),
         jax.ShapeDtypeStruct((n, DC), BF), jax.ShapeDtypeStruct((n, 2 * DK), BF),
         jax.ShapeDtypeStruct((n, DV), BF), jax.ShapeDtypeStruct((n, DV), BF),
         jax.ShapeDtypeStruct((n, 2 * DK), F32), jax.ShapeDtypeStruct((n, 128), BF)],
        [pltpu.VMEM((nseg, SEGP, DC), F32)],
        (x, vec1, win, convw, cvec, wa2, ba), exchange, carried)


def _gla_dir(d):
    return (_tri(CH, "le"), CH - 1) if d == 0 else (_tri(CH, "ge"), 0)


def _gla_chunk_terms(qk, la, d):
    seen, last = _gla_dir(d)
    b = _mask_dot(seen.astype(BF), la)
    bl = b[last:last + 1, :]
    eb = jnp.exp(b)
    enb = jnp.exp(-b)
    ekd = jnp.exp(bl - b)
    ebl = jnp.exp(bl)
    q = qk[:, :DK].astype(F32) * QSCALE
    k = qk[:, DK:].astype(F32)
    return eb, enb, ekd, ebl, q * eb, k * enb, k * ekd


NP = NH // 2
PW = 2 * HK


def _lo_lanes(shape):
    return lax.broadcasted_iota(jnp.int32, shape, len(shape) - 1) < HK


def _pair_sel(lo, hi):
    return jnp.where(_lo_lanes(lo.shape), lo, hi)


def _only(x, which):
    keep = _lo_lanes(x.shape) if which == 0 else jnp.logical_not(_lo_lanes(x.shape))
    return jnp.where(keep, x, jnp.zeros_like(x))


def gla_fwd(qk, v, la, s0, tm):
    n = qk.shape[0]
    nt = n // tm
    nc = tm // CH

    def body(qkf_ref, vf_ref, laf_ref, qkb_ref, vb_ref, lab_ref, s0_ref, of_ref, ob_ref, sef_ref, seb_ref, st_ref):
        @pl.when(pl.program_id(0) == 0)
        def _():
            st_ref[...] = s0_ref[...]

        def chunk(ci, carry):
            t = []
            for d, (qk_ref, v_ref, la_ref) in enumerate(((qkf_ref, vf_ref, laf_ref), (qkb_ref, vb_ref, lab_ref))):
                c = ci if d == 0 else nc - 1 - ci
                rows = pl.ds(pl.multiple_of(c * CH, CH), CH)
                eb, enb, ekd, ebl, qt, kt, kd = _gla_chunk_terms(qk_ref[rows, :], la_ref[rows, :], d)
                t.append(dict(c=c, rows=rows, ebl=ebl, qt=qt.astype(BF), kt=kt.astype(BF), kd=kd.astype(BF),
                              vv=v_ref[rows, :], st=[st_ref[d, p] for p in range(NP)], amask=_gla_dir(d)[0]))
            dh = [(d, h) for d in range(2) for h in range(NH)]
            ps = lambda h: slice((h // 2) * PW, (h // 2 + 1) * PW)
            vs = lambda h: slice(h * HV, (h + 1) * HV)
            qm = {(d, h): _only(t[d]['qt'][:, ps(h)], h % 2) for d, h in dh}
            a = {(d, h): jnp.where(t[d]['amask'], _dot_nt(qm[d, h], t[d]['kt'][:, ps(h)]), 0.0).astype(BF) for d, h in dh}
            o = {(d, h): _dot(a[d, h], t[d]['vv'][:, vs(h)]) + _dot_nt(qm[d, h], t[d]['st'][h // 2].astype(BF))
                 for d, h in dh}
            kv = {(d, h): _dot_tn(t[d]['vv'][:, vs(h)], t[d]['kd'][:, ps(h)]) for d, h in dh}
            for d, (o_ref, se_ref) in enumerate(((of_ref, sef_ref), (ob_ref, seb_ref))):
                for h in range(NH):
                    o_ref[t[d]['rows'], vs(h)] = o[d, h]
                for p in range(NP):
                    se_ref[t[d]['c'], p] = t[d]['st'][p]
                    st_ref[d, p] = (t[d]['ebl'][:, p * PW:(p + 1) * PW] * t[d]['st'][p]
                                    + _pair_sel(kv[d, 2 * p], kv[d, 2 * p + 1]))
            return carry

        lax.fori_loop(0, nc, chunk, 0, unroll=2)

    fw = lambda w, col=0: pl.BlockSpec((tm, w), lambda i: (i, col))
    bw = lambda w, col=0: pl.BlockSpec((tm, w), lambda i: (nt - 1 - i, col))
    se_f = pl.BlockSpec((nc, NP, HV, PW), lambda i: (i, 0, 0, 0))
    se_b = pl.BlockSpec((nc, NP, HV, PW), lambda i: (nt - 1 - i, 0, 0, 0))
    se_shape = jax.ShapeDtypeStruct((n // CH, NP, HV, PW), F32)
    return pl.pallas_call(
        body, grid=(nt,), name="gla_fwd",
        in_specs=[fw(2 * DK), fw(DV), fw(DK, 0), bw(2 * DK), bw(DV), bw(DK, 1), _full(s0.shape)],
        out_specs=[fw(DV), bw(DV), se_f, se_b],
        out_shape=[jax.ShapeDtypeStruct((n, DV), F32), jax.ShapeDtypeStruct((n, DV), F32), se_shape, se_shape],
        scratch_shapes=[pltpu.VMEM((2, NP, HV, PW), F32)],
        compiler_params=_cparams(("arbitrary",)),
    )(qk, v, la, qk, v, la, s0)


def gla_bwd(qk, v, la, do, se_f, se_b, tm, exchange=None, carried=()):
    n = qk.shape[0]
    nt = n // tm
    nc = tm // CH

    def body(qkf_ref, vf_ref, laf_ref, dof_ref, sef_ref, qkb_ref, vb_ref, lab_ref, dob_ref, seb_ref,
             dqkf_ref, dvf_ref, dlaf_ref, dqkb_ref, dvb_ref, dlab_ref, ds0_ref, ds_ref):
        @pl.when(pl.program_id(0) == 0)
        def _():
            ds_ref[...] = jnp.zeros_like(ds_ref)

        def chunk(ci, carry):
            t = []
            for d, (qk_ref, v_ref, la_ref, do_ref, se_ref) in enumerate(
                    ((qkf_ref, vf_ref, laf_ref, dof_ref, sef_ref), (qkb_ref, vb_ref, lab_ref, dob_ref, seb_ref))):
                c = nc - 1 - ci if d == 0 else ci
                rows = pl.ds(pl.multiple_of(c * CH, CH), CH)
                amask, last = _gla_dir(d)
                eb, enb, ekd, ebl, qt, kt, kd = _gla_chunk_terms(qk_ref[rows, :], la_ref[rows, :], d)
                t.append(dict(rows=rows, amask=amask, last=last, eb=eb, enb=enb, ekd=ekd, ebl=ebl, qt=qt, kt=kt, kd=kd,
                              qtb=qt.astype(BF), ktb=kt.astype(BF), kdb=kd.astype(BF), vv=v_ref[rows, :], dd=do_ref[rows, :],
                              st=[se_ref[c, p] for p in range(NP)], dsn=[ds_ref[d, p] for p in range(NP)]))
            dh = [(d, h) for d in range(2) for h in range(NH)]
            dp = [(d, p) for d in range(2) for p in range(NP)]
            ps = lambda h: slice((h // 2) * PW, (h // 2 + 1) * PW)
            vs = lambda h: slice(h * HV, (h + 1) * HV)
            stb = {(d, p): t[d]['st'][p].astype(BF) for d, p in dp}
            dsnb = {(d, p): t[d]['dsn'][p].astype(BF) for d, p in dp}
            qm = {(d, h): _only(t[d]['qtb'][:, ps(h)], h % 2) for d, h in dh}
            km = {(d, h): _only(t[d]['kdb'][:, ps(h)], h % 2) for d, h in dh}
            a = {(d, h): jnp.where(t[d]['amask'], _dot_nt(qm[d, h], t[d]['ktb'][:, ps(h)]), 0.0).astype(BF) for d, h in dh}
            da = {(d, h): jnp.where(t[d]['amask'], _dot_nt(t[d]['dd'][:, vs(h)], t[d]['vv'][:, vs(h)]), 0.0).astype(BF)
                  for d, h in dh}
            dv = {(d, h): _dot_tn(a[d, h], t[d]['dd'][:, vs(h)]) + _dot_nt(km[d, h], dsnb[d, h // 2]) for d, h in dh}
            dkd = {(d, h): _dot(t[d]['vv'][:, vs(h)], dsnb[d, h // 2]) for d, h in dh}
            dqt = {(d, h): _dot(da[d, h], t[d]['ktb'][:, ps(h)]) + _dot(t[d]['dd'][:, vs(h)], stb[d, h // 2]) for d, h in dh}
            dkt = {(d, h): _dot_tn(da[d, h], t[d]['qtb'][:, ps(h)]) for d, h in dh}
            dsq = {(d, h): _dot_tn(t[d]['dd'][:, vs(h)], t[d]['qtb'][:, ps(h)]) for d, h in dh}
            for d, (dqk_ref, dv_ref, dla_ref) in enumerate(((dqkf_ref, dvf_ref, dlaf_ref), (dqkb_ref, dvb_ref, dlab_ref))):
                td = t[d]
                rows = td['rows']
                for h in range(NH):
                    dv_ref[rows, vs(h)] = dv[d, h].astype(BF)
                pair = lambda x: jnp.concatenate([_pair_sel(x[d, 2 * p], x[d, 2 * p + 1]) for p in range(NP)], axis=1)
                dqt_, dkt_, dkd_ = pair(dqt), pair(dkt), pair(dkd)
                debl = jnp.concatenate([jnp.sum(td['st'][p] * td['dsn'][p], axis=0, keepdims=True) for p in range(NP)], axis=1)
                for p in range(NP):
                    ds_ref[d, p] = _pair_sel(dsq[d, 2 * p], dsq[d, 2 * p + 1]) + td['ebl'][:, p * PW:(p + 1) * PW] * td['dsn'][p]
                dkdkd = dkd_ * td['kd']
                dbl = jnp.sum(dkdkd, axis=0, keepdims=True) + debl * td['ebl']
                is_last = lax.broadcasted_iota(jnp.int32, (CH, DK), 0) == td['last']
                db = dqt_ * td['qt'] - dkt_ * td['kt'] - dkdkd + jnp.where(is_last, dbl, 0.0)
                dqk_ref[rows, :] = jnp.concatenate([dqt_ * td['eb'] * QSCALE, dkt_ * td['enb'] + dkd_ * td['ekd']], axis=1).astype(BF)
                dla_ref[rows, :] = _mask_dot(_gla_dir(1 - d)[0].astype(BF), db)
            return carry

        lax.fori_loop(0, nc, chunk, 0, unroll=2)

        @pl.when(pl.program_id(0) == nt - 1)
        def _():
            ds0_ref[...] = ds_ref[...]

    up = lambda w, col=0: pl.BlockSpec((tm, w), lambda i: (i, col))
    dn = lambda w, col=0: pl.BlockSpec((tm, w), lambda i: (nt - 1 - i, col))
    se_up = pl.BlockSpec((nc, NP, HV, PW), lambda i: (i, 0, 0, 0))
    se_dn = pl.BlockSpec((nc, NP, HV, PW), lambda i: (nt - 1 - i, 0, 0, 0))
    return _call(
        body, (nt,), "gla_bwd",
        [dn(2 * DK), dn(DV), dn(DK, 0), dn(DV), se_dn, up(2 * DK), up(DV), up(DK, 1), up(DV), se_up],
        [dn(2 * DK), dn(DV), dn(DK), up(2 * DK), up(DV), up(DK), _full((2, NP, HV, PW))],
        [jax.ShapeDtypeStruct((n, 2 * DK), BF), jax.ShapeDtypeStruct((n, DV), BF),
         jax.ShapeDtypeStruct((n, DK), F32), jax.ShapeDtypeStruct((n, 2 * DK), BF),
         jax.ShapeDtypeStruct((n, DV), BF), jax.ShapeDtypeStruct((n, DK), F32),
         jax.ShapeDtypeStruct((2, NP, HV, PW), F32)],
        [pltpu.VMEM((2, NP, HV, PW), F32)],
        (qk, v, la, do, se_f, qk, v, la, do, se_b), exchange, carried)


def _head_norm(o):
    ons, rss = [], []
    for h in range(NH):
        oh = o[:, h * HV:(h + 1) * HV]
        rs = lax.rsqrt(jnp.mean(oh * oh, axis=-1, keepdims=True) + EPS)
        ons.append(oh * rs)
        rss.append(rs)
    return ons, rss


def merge_fwd(x, o_f, o_b, g, co, vecm, gn, wout, tm):
    n = x.shape[0]

    def body(x_ref, of_ref, ob_ref, g_ref, co_ref, vec_ref, gn_ref, w_ref, x1_ref, y1_ref, cat_ref):
        o = of_ref[...] + ob_ref[...]
        ons, _ = _head_norm(o)
        gg = g_ref[...].astype(F32)
        sil = gg * _sigmoid(gg)
        cat_ref[:, :DC] = co_ref[...]
        for h in range(NH):
            vs = slice(h * HV, (h + 1) * HV)
            cat_ref[:, DC + h * HV:DC + (h + 1) * HV] = (ons[h] * gn_ref[:, vs] * sil[:, vs]).astype(BF)
        y1 = _dot(cat_ref[...], w_ref[...])
        y1_ref[...] = y1.astype(BF)
        x1_ref[...] = x_ref[...] + vec_ref[0:1, :] * y1

    tok = lambda w: pl.BlockSpec((tm, w), lambda i: (i, 0))
    return pl.pallas_call(
        body, grid=(n // tm,), name="merge_fwd",
        in_specs=[tok(D), tok(DV), tok(DV), tok(DV), tok(DC), _full(vecm.shape), _full(gn.shape), _full(wout.shape)],
        out_specs=[tok(D), tok(D), tok(D)],
        out_shape=[jax.ShapeDtypeStruct((n, D), F32), jax.ShapeDtypeStruct((n, D), BF), jax.ShapeDtypeStruct((n, D), BF)],
        compiler_params=_cparams(("arbitrary",)),
    )(x, o_f, o_b, g, co, vecm, gn, wout)


def merge_bwd(dx1, y1, o_f, o_b, g, vecm, gn, wout, tm):
    n = dx1.shape[0]

    def body(dx1_ref, y1_ref, of_ref, ob_ref, g_ref, vec_ref, gn_ref, w_ref,
             dy1_ref, dco_ref, do_ref, dg_ref, s1_ref, s2_ref):
        @pl.when(pl.program_id(0) == 0)
        def _():
            s1_ref[...] = jnp.zeros_like(s1_ref)
            s2_ref[...] = jnp.zeros_like(s2_ref)

        dx1 = dx1_ref[...]
        s1_ref[...] += _colsum8(dx1 * y1_ref[...].astype(F32))
        dy1 = (dx1 * vec_ref[0:1, :]).astype(BF)
        dy1_ref[...] = dy1
        dcat = _dot_nt(dy1, w_ref[...])
        dco_ref[...] = dcat[:, :DC].astype(BF)
        o = of_ref[...] + ob_ref[...]
        ons, rss = _head_norm(o)
        gg = g_ref[...].astype(F32)
        sg = _sigmoid(gg)
        sil = gg * sg
        dsil = sg * (1.0 + gg * (1.0 - sg))
        for h in range(NH):
            vs = slice(h * HV, (h + 1) * HV)
            do2 = dcat[:, DC + h * HV:DC + (h + 1) * HV]
            gnh = gn_ref[:, vs]
            t = do2 * sil[:, vs]
            s2_ref[:, vs] += _colsum8(t * ons[h])
            don = t * gnh
            do_ref[:, vs] = (rss[h] * (don - ons[h] * jnp.mean(don * ons[h], axis=-1, keepdims=True))).astype(BF)
            dg_ref[:, vs] = (do2 * ons[h] * gnh * dsil[:, vs]).astype(BF)

    tok = lambda w: pl.BlockSpec((tm, w), lambda i: (i, 0))
    return pl.pallas_call(
        body, grid=(n // tm,), name="merge_bwd",
        in_specs=[tok(D), tok(D), tok(DV), tok(DV), tok(DV), _full(vecm.shape), _full(gn.shape), _full(wout.shape)],
        out_specs=[tok(D), tok(DC), tok(DV), tok(DV), _full((8, D)), _full((8, DV))],
        out_shape=[jax.ShapeDtypeStruct((n, D), BF), jax.ShapeDtypeStruct((n, DC), BF), jax.ShapeDtypeStruct((n, DV), BF),
                   jax.ShapeDtypeStruct((n, DV), BF), jax.ShapeDtypeStruct((8, D), F32), jax.ShapeDtypeStruct((8, DV), F32)],
        compiler_params=_cparams(("arbitrary",)),
    )(dx1, y1, o_f, o_b, g, vecm, gn, wout)


def ffn_fwd_bwd(x1, tgt, vecf, wg, wu, wd, tm):
    n = x1.shape[0]

    def body(x1_ref, t_ref, vec_ref, wg_ref, wu_ref, wd_ref,
             dx1_ref, h2_ref, act_ref, dgt_ref, dup_ref, dy2_ref, s_ref):
        @pl.when(pl.program_id(0) == 0)
        def _():
            s_ref[...] = jnp.zeros_like(s_ref)

        n2g, sh2, sc2, g2, fg = (vec_ref[i:i + 1, :] for i in range(5))
        x1 = x1_ref[...]
        r2 = lax.rsqrt(jnp.mean(x1 * x1, axis=-1, keepdims=True) + EPS)
        xn2 = x1 * r2
        h2 = (xn2 * n2g * (1.0 + sc2) + sh2).astype(BF)
        h2_ref[...] = h2
        gt = _dot(h2, wg_ref[...])
        up = _dot(h2, wu_ref[...])
        sg = _sigmoid(gt)
        sil = gt * sg
        act = (sil * up).astype(BF)
        act_ref[...] = act
        y2 = _dot(act, wd_ref[...])
        x2 = x1 + g2 * y2
        r3 = lax.rsqrt(jnp.mean(x2 * x2, axis=-1, keepdims=True) + EPS)
        xn3 = x2 * r3
        e = xn3 * fg - t_ref[...]
        s_ref[40:48, :] += _colsum8(e * e) * (0.5 / D)
        dyo = e * (1.0 / D)
        s_ref[0:8, :] += _colsum8(dyo * xn3)
        dxn3 = dyo * fg
        dx2 = r3 * (dxn3 - xn3 * jnp.mean(dxn3 * xn3, axis=-1, keepdims=True))
        s_ref[8:16, :] += _colsum8(dx2 * y2)
        dy2 = (dx2 * g2).astype(BF)
        dy2_ref[...] = dy2
        dact = _dot_nt(dy2, wd_ref[...])
        dup = (dact * sil).astype(BF)
        dgt = (dact * up * (sg * (1.0 + gt * (1.0 - sg)))).astype(BF)
        dup_ref[...] = dup
        dgt_ref[...] = dgt
        dh2 = _dot_nt(dgt, wg_ref[...]) + _dot_nt(dup, wu_ref[...])
        s_ref[16:24, :] += _colsum8(dh2)
        t = dh2 * xn2
        s_ref[24:32, :] += _colsum8(t * n2g)
        s_ref[32:40, :] += _colsum8(t * (1.0 + sc2))
        dxn2 = dh2 * ((1.0 + sc2) * n2g)
        dx1_ref[...] = dx2 + r2 * (dxn2 - xn2 * jnp.mean(dxn2 * xn2, axis=-1, keepdims=True))

    tok = lambda w: pl.BlockSpec((tm, w), lambda i: (i, 0))
    wspec = lambda a: pl.BlockSpec(a.shape, lambda i: (0, 0), pipeline_mode=pl.Buffered(1))
    return pl.pallas_call(
        body, grid=(n // tm,), name="ffn_fwd_bwd",
        in_specs=[tok(D), tok(D), _full(vecf.shape), wspec(wg), wspec(wu), wspec(wd)],
        out_specs=[tok(D), tok(D), tok(DFF), tok(DFF), tok(DFF), tok(D), _full((48, D))],
        out_shape=[jax.ShapeDtypeStruct((n, D), F32), jax.ShapeDtypeStruct((n, D), BF), jax.ShapeDtypeStruct((n, DFF), BF),
                   jax.ShapeDtypeStruct((n, DFF), BF), jax.ShapeDtypeStruct((n, DFF), BF), jax.ShapeDtypeStruct((n, D), BF),
                   jax.ShapeDtypeStruct((48, D), F32)],
        compiler_params=_cparams(("arbitrary",)),
    )(x1, tgt, vecf, wg, wu, wd)


def wgrad(a, b, init, t1, t2, tn, name):
    n, k1 = a.shape
    k2 = b.shape[1]

    def body(a_ref, b_ref, *rest):
        o_ref = rest[-1]

        @pl.when(pl.program_id(2) == 0)
        def _():
            o_ref[...] = rest[0][...] if init is not None else jnp.zeros_like(o_ref)

        o_ref[...] += _dot_tn(a_ref[...], b_ref[...])

    ospec = pl.BlockSpec((t1, t2), lambda i, j, k: (i, j))
    extra = ([ospec], {2: 0}, (init,)) if init is not None else ([], {}, ())
    return pl.pallas_call(
        body, grid=(k1 // t1, k2 // t2, n // tn), name=name,
        in_specs=[pl.BlockSpec((tn, t1), lambda i, j, k: (k, i)), pl.BlockSpec((tn, t2), lambda i, j, k: (k, j))] + extra[0],
        out_specs=ospec, out_shape=jax.ShapeDtypeStruct((k1, k2), F32), input_output_aliases=extra[1],
        compiler_params=_cparams(("parallel", "parallel", "arbitrary")),
    )(a, b, *extra[2])


def bwd_in(x, dx1, ag, yb, dco, dqk_f, dqk_b, dv_f, dv_b, dg, dla_f, dla_b, la, r, vec1, win, convw, cvec, wa2, tm):
    n = x.shape[0]
    nseg = tm // GW
    cg = 128

    def body(x_ref, dx1_ref, ag_ref, yb_ref, dco_ref, dqkf_ref, dqkb_ref, dvf_ref, dvb_ref, dg_ref, dlaf_ref, dlab_ref,
             la_ref, r_ref, vec_ref, win_ref, cw_ref, cv_ref, wa2_ref,
             gx_ref, h_ref, dp_ref, dwa2_ref, dcw_ref, s_ref, vc_ref, pad2_ref, dvc_ref, dcw8_ref):
        first = pl.program_id(0) == 0

        @pl.when(first)
        def _():
            s_ref[...] = jnp.zeros_like(s_ref)
            dwa2_ref[...] = jnp.zeros_like(dwa2_ref)
            dcw8_ref[...] = jnp.zeros_like(dcw8_ref)

        yn, rs = _ln_stats(yb_ref[...])
        lng = cv_ref[1:2, :]
        ln = yn * lng + cv_ref[2:3, :]
        sgl = _sigmoid(ln)
        dln = dco_ref[...].astype(F32) * (sgl * (1.0 + ln * (1.0 - sgl)))
        dyn = dln * lng
        dyb = rs * (dyn - jnp.mean(dyn, axis=-1, keepdims=True) - yn * jnp.mean(dyn * yn, axis=-1, keepdims=True))
        s_ref[24:32, 0:DC] += _colsum8(dyb)
        s_ref[24:32, DC:D] += _colsum8(dln * yn)
        s_ref[32:40, 0:DC] += _colsum8(dln)

        agv = ag_ref[...].astype(F32)
        a = agv[:, :DC]
        sgg = _sigmoid(agv[:, DC:])
        vc_ref[...] = a * sgg
        _fill_padded(pad2_ref, dyb, nseg)

        def seg(s, carry):
            rows = pl.ds(pl.multiple_of(s * GW, GW), GW)
            for c0 in range(0, DC, cg):
                cs = pl.ds(c0, cg)
                vcs = vc_ref[rows, cs]
                acc = jnp.zeros((GW, cg), F32)
                for j, rows_j in _tap_slabs(pad2_ref, s, cs):
                    acc = acc + cw_ref[pl.ds(CW - 1 - j, 1), cs] * rows_j
                    dcw8_ref[CW - 1 - j, :, cs] += _colsum8(vcs * rows_j)
                dvc_ref[rows, cs] = acc
            return carry

        lax.fori_loop(0, nseg, seg, 0)
        dvc = dvc_ref[...]
        dp_ref[:, 0:DC] = (dvc * sgg).astype(BF)
        dp_ref[:, DC:2 * DC] = (dvc * a * sgg * (1.0 - sgg)).astype(BF)

        dp_ref[:, 2 * DC:2 * DC + 2 * DK] = (dqkf_ref[...].astype(F32) + dqkb_ref[...].astype(F32)).astype(BF)
        dp_ref[:, 2 * DC + 2 * DK:2 * DC + 2 * DK + DV] = (dvf_ref[...].astype(F32) + dvb_ref[...].astype(F32)).astype(BF)
        dp_ref[:, 2 * DC + 2 * DK + DV:2 * DC + 2 * DK + 2 * DV] = dg_ref[...]

        la = la_ref[...]
        dla = jnp.concatenate([dlaf_ref[...], dlab_ref[...]], axis=1)
        dpre = dla * (1.0 - jnp.exp(TAU * la)) * (1.0 / TAU)
        s_ref[32:40, DC:D] += _colsum8(dpre)
        dpreb = dpre.astype(BF)
        dwa2_ref[...] += _dot_tn(r_ref[...], dpreb)
        dp_ref[:, DINP - 128:] = _dot_nt(dpreb, wa2_ref[...]).astype(BF)

        dh = _dot_nt(dp_ref[...], win_ref[...])
        xx = x_ref[...]
        n1g, sh1, sc1 = vec_ref[0:1, :], vec_ref[1:2, :], vec_ref[2:3, :]
        rstd = lax.rsqrt(jnp.mean(xx * xx, axis=-1, keepdims=True) + EPS)
        xn = xx * rstd
        h_ref[...] = (xn * n1g * (1.0 + sc1) + sh1).astype(BF)
        s_ref[0:8, :] += _colsum8(dh)
        t = dh * xn
        s_ref[8:16, :] += _colsum8(t * n1g)
        s_ref[16:24, :] += _colsum8(t * (1.0 + sc1))
        dxn = dh * ((1.0 + sc1) * n1g)
        gx_ref[...] = dx1_ref[...] + rstd * (dxn - xn * jnp.mean(dxn * xn, axis=-1, keepdims=True))

        @pl.when(pl.program_id(0) == pl.num_programs(0) - 1)
        def _():
            dcw_ref[...] = jnp.sum(dcw8_ref[...], axis=1)

    tok = lambda w: pl.BlockSpec((tm, w), lambda i: (i, 0))
    return pl.pallas_call(
        body, grid=(n // tm,), name="bwd_in",
        in_specs=[tok(D), tok(D), tok(2 * DC), tok(DC), tok(DC), tok(2 * DK), tok(2 * DK), tok(DV), tok(DV), tok(DV),
                  tok(DK), tok(DK), tok(2 * DK), tok(128), _full(vec1.shape),
                  pl.BlockSpec(win.shape, lambda i: (0, 0), pipeline_mode=pl.Buffered(1)),
                  _full(convw.shape), _full(cvec.shape), _full(wa2.shape)],
        out_specs=[tok(D), tok(D), tok(DINP), _full((128, 2 * DK)), _full((32, DC)), _full((40, D))],
        out_shape=[jax.ShapeDtypeStruct((n, D), F32), jax.ShapeDtypeStruct((n, D), BF), jax.ShapeDtypeStruct((n, DINP), BF),
                   jax.ShapeDtypeStruct((128, 2 * DK), F32), jax.ShapeDtypeStruct((32, DC), F32),
                   jax.ShapeDtypeStruct((40, D), F32)],
        scratch_shapes=[pltpu.VMEM((tm, DC), F32), pltpu.VMEM((nseg, SEGP, DC), F32), pltpu.VMEM((tm, DC), F32),
                        pltpu.VMEM((32, 8, DC), F32)],
        compiler_params=_cparams(("arbitrary",)),
    )(x, dx1, ag, yb, dco, dqk_f, dqk_b, dv_f, dv_b, dg, dla_f, dla_b, la, r, vec1, win, convw, cvec, wa2)


def _ctx_common(ctx_ref, vec_ref, win_ref, wa2_ref, ba_ref):
    cx = ctx_ref[...]
    t = cx.shape[0]
    rstd = lax.rsqrt(jnp.mean(cx * cx, axis=-1, keepdims=True) + EPS)
    xn = cx * rstd
    hc = (xn * vec_ref[0:1, :] * (1.0 + vec_ref[2:3, :]) + vec_ref[1:2, :]).astype(BF)
    k0 = 2 * DC + DK
    kv = _dot(hc, win_ref[:, k0:k0 + DK + DV]).astype(BF).astype(F32)
    r = _dot(hc, win_ref[:, DINP - 128:]).astype(BF)
    la = _log_sigmoid(_dot(r, wa2_ref[...]) + ba_ref[...]) * (1.0 / TAU)
    incl = _tri(t, "le").astype(BF)
    strict = _tri(t, "lt").astype(BF)
    bf = _mask_dot(incl, la[:, :DK])
    wf = jnp.exp(bf[t - 1:t, :] - bf)
    wb = jnp.exp(_mask_dot(strict, la[:, DK:]))
    return xn, hc, kv[:, :DK], kv[:, DK:], r, la, wf, wb


def ctx_fwd(ctx, vecc, win, wa2, ba):
    def body(ctx_ref, vec_ref, win_ref, wa2_ref, ba_ref, s_ref):
        _, _, k, v, _, _, wf, wb = _ctx_common(ctx_ref, vec_ref, win_ref, wa2_ref, ba_ref)
        vb = v.astype(BF)
        for d, w in enumerate((wf, wb)):
            kd = (k * w).astype(BF)
            for h in range(NH):
                s_ref[d, h // 2, :, (h % 2) * HK:(h % 2 + 1) * HK] = _dot_tn(vb[:, h * HV:(h + 1) * HV], kd[:, h * HK:(h + 1) * HK])

    return pl.pallas_call(
        body, name="ctx_fwd", out_shape=jax.ShapeDtypeStruct((2, NP, HV, PW), F32),
        compiler_params=pltpu.CompilerParams(vmem_limit_bytes=VMEM_LIMIT),
    )(ctx, vecc, win, wa2, ba)


def ctx_bwd(ctx, vecc, win, wa2, ba, ds0):
    t = ctx.shape[0]

    def body(ctx_ref, vec_ref, win_ref, wa2_ref, ba_ref, ds_ref, dwin_ref, dwa2_ref, s_ref, dpc_ref):
        xn, hc, k, v, r, la, wf, wb = _ctx_common(ctx_ref, vec_ref, win_ref, wa2_ref, ba_ref)
        vb = v.astype(BF)
        strict = _tri(t, "lt").astype(BF)
        strict_t = _tri(t, "gt").astype(BF)
        dpc_ref[...] = jnp.zeros_like(dpc_ref)
        k0 = 2 * DC + DK
        dk = jnp.zeros((t, DK), F32)
        des = []
        for d, w in enumerate((wf, wb)):
            kd = (k * w).astype(BF)
            dkds = []
            for h in range(NH):
                dsb = ds_ref[d, h // 2, :, (h % 2) * HK:(h % 2 + 1) * HK].astype(BF)
                dkds.append(_dot(vb[:, h * HV:(h + 1) * HV], dsb))
                dvh = _dot_nt(kd[:, h * HK:(h + 1) * HK], dsb)
                vs = slice(k0 + DK + h * HV, k0 + DK + (h + 1) * HV)
                if d == 0:
                    dpc_ref[:, vs] = dvh.astype(BF)
                else:
                    dpc_ref[:, vs] = (dpc_ref[:, vs].astype(F32) + dvh).astype(BF)
            dkd = jnp.concatenate(dkds, axis=1)
            dk = dk + dkd * w
            des.append(dkd * k * w)
        dpc_ref[:, k0:k0 + DK] = dk.astype(BF)
        dla = jnp.concatenate([_mask_dot(strict, des[0]), _mask_dot(strict_t, des[1])], axis=1)
        dpre = dla * (1.0 - jnp.exp(TAU * la)) * (1.0 / TAU)
        dpreb = dpre.astype(BF)
        dwa2_ref[...] = _dot_tn(r, dpreb)
        dpc_ref[:, DINP - 128:] = _dot_nt(dpreb, wa2_ref[...]).astype(BF)
        dpc = dpc_ref[...]
        dwin_ref[...] = _dot_tn(hc, dpc)
        dhc = _dot_nt(dpc, win_ref[...])
        n1g, sc1 = vec_ref[0:1, :], vec_ref[2:3, :]
        tt = dhc * xn
        s_ref[...] = jnp.zeros_like(s_ref)
        s_ref[0:1, :] = jnp.sum(tt * (1.0 + sc1), axis=0, keepdims=True)
        s_ref[1:2, :] = jnp.sum(dhc, axis=0, keepdims=True)
        s_ref[2:3, :] = jnp.sum(tt * n1g, axis=0, keepdims=True)
        s_ref[3:4, DC:D] = jnp.sum(dpre, axis=0, keepdims=True)

    return pl.pallas_call(
        body, name="ctx_bwd",
        out_shape=[jax.ShapeDtypeStruct((D, DINP), F32), jax.ShapeDtypeStruct((128, 2 * DK), F32),
                   jax.ShapeDtypeStruct((8, D), F32)],
        scratch_shapes=[pltpu.VMEM((t, DINP), BF)],
        compiler_params=pltpu.CompilerParams(vmem_limit_bytes=VMEM_LIMIT),
    )(ctx, vecc, win, wa2, ba, ds0)


def _silu(x):
    return x * _sigmoid(x)


def mod_fwd(cext, wm, bm):
    def body(c_ref, w_ref, b_ref, o_ref):
        o_ref[...] = _dot(_silu(c_ref[...]).astype(BF), w_ref[...].astype(BF)) + b_ref[...]

    return pl.pallas_call(body, name="mod_fwd", out_shape=jax.ShapeDtypeStruct((cext.shape[0], wm.shape[1]), F32),
                          compiler_params=pltpu.CompilerParams(vmem_limit_bytes=VMEM_LIMIT))(cext, wm, bm)


def mod_bwd(cext, dm, wm):
    def body(c_ref, d_ref, w_ref, gw_ref, ds_ref):
        dmb = d_ref[...].astype(BF)
        gw_ref[...] = _dot_tn(_silu(c_ref[...]).astype(BF), dmb)
        ds_ref[...] = _dot_nt(dmb, w_ref[...].astype(BF))

    return pl.pallas_call(body, name="mod_bwd",
                          out_shape=[jax.ShapeDtypeStruct(wm.shape, F32), jax.ShapeDtypeStruct(cext.shape, F32)],
                          compiler_params=pltpu.CompilerParams(vmem_limit_bytes=VMEM_LIMIT))(cext, dm, wm)


def pack_small(sf, s1, s2, sd, sc, dcw, dwa2, dwa2_c):
    def body(sf_ref, s1_ref, s2_ref, sd_ref, sc_ref, dcw_ref, dwa2_ref, dwa2c_ref, o_ref, ocw_ref, owa_ref):
        rsum = lambda ref, i: jnp.sum(ref[8 * i:8 * i + 8, :], axis=0, keepdims=True)
        o_ref[...] = jnp.zeros_like(o_ref)
        o_ref[0:1, :] = rsum(sd_ref, 0)
        o_ref[1:2, :] = rsum(sd_ref, 1)
        o_ref[2:3, :] = rsum(s1_ref, 0)
        o_ref[3:4, :] = rsum(sf_ref, 2)
        o_ref[4:5, :] = rsum(sf_ref, 3)
        o_ref[5:6, :] = rsum(sf_ref, 1)
        o_ref[6:7, :] = sc_ref[1:2, :]
        o_ref[7:8, :] = sc_ref[2:3, :]
        o_ref[8:9, :] = rsum(sd_ref, 2) + sc_ref[0:1, :]
        o_ref[9:10, :] = rsum(sf_ref, 4)
        o_ref[10:11, :] = rsum(sf_ref, 0)
        o_ref[11:12, :] = rsum(sd_ref, 3)
        o_ref[12:13, :] = rsum(sd_ref, 4) + sc_ref[3:4, :]
        g = jnp.sum(s2_ref[...], axis=0, keepdims=True)
        o_ref[13:14, 0:HV] = g[:, 0:HV] + g[:, HV:2 * HV] + g[:, 2 * HV:3 * HV] + g[:, 3 * HV:4 * HV]
        o_ref[14:15, :] = rsum(sf_ref, 5)
        ocw_ref[...] = dcw_ref[...]
        owa_ref[...] = dwa2_ref[0:32, :] + dwa2c_ref[0:32, :]

    return pl.pallas_call(body, name="pack_small",
                          out_shape=[jax.ShapeDtypeStruct((16, D), F32), jax.ShapeDtypeStruct((32, DC), F32),
                                     jax.ShapeDtypeStruct((32, 2 * DK), F32)])(sf, s1, s2, sd, sc, dcw, dwa2, dwa2_c)


def small_totals(g8):
    r = g8.shape[1]

    def body(g_ref, t_ref, bm_ref, loss_ref):
        acc = g_ref[0]
        for i in range(1, NDEV):
            acc = acc + g_ref[i]
        t_ref[...] = acc
        bm_ref[...] = jnp.zeros_like(bm_ref)
        bm_ref[0:6, :] = acc[0:6, :]
        bm_ref[0:2, :] += acc[6:8, :]
        loss_ref[...] = jnp.broadcast_to(jnp.sum(acc[14:15, :], axis=1, keepdims=True), loss_ref.shape)

    return pl.pallas_call(body, name="small_totals",
                          out_shape=[jax.ShapeDtypeStruct((r, D), F32), jax.ShapeDtypeStruct((8, D), F32),
                                     jax.ShapeDtypeStruct((8, 128), F32)])(g8)


def cctx_grad(p8, c_ctx_row):
    def body(p_ref, c_ref, o_ref):
        acc = p_ref[0, 0:1, :]
        for j in range(1, NCHIP):
            acc = acc + p_ref[2 * j, 0:1, :]
        cc = c_ref[0:1, :]
        sg = _sigmoid(cc)
        o_ref[...] = jnp.zeros_like(o_ref)
        o_ref[0:1, :] = acc * (sg * (1.0 + cc * (1.0 - sg)))

    return pl.pallas_call(body, name="cctx_grad", out_shape=jax.ShapeDtypeStruct((8, D), F32))(p8, c_ctx_row)


def adamw(w, g, m, v, rows, name):
    r, c = w.shape

    def body(w_ref, g_ref, m_ref, v_ref, d_ref, nm_ref, nv_ref):
        gg = g_ref[...]
        nm = ADAM_B1 * m_ref[...] + (1.0 - ADAM_B1) * gg
        nv = ADAM_B2 * v_ref[...] + (1.0 - ADAM_B2) * (gg * gg)
        m_hat = nm / (1.0 - ADAM_B1 ** ADAM_STEP)
        v_hat = nv / (1.0 - ADAM_B2 ** ADAM_STEP)
        d_ref[...] = -ADAM_LR * (m_hat / (jnp.sqrt(v_hat) + ADAM_EPS) + ADAM_WD * w_ref[...])
        nm_ref[...] = nm
        nv_ref[...] = nv

    spec = pl.BlockSpec((rows, c), lambda i: (i, 0))
    sds = jax.ShapeDtypeStruct((r, c), F32)
    return pl.pallas_call(
        body, grid=(r // rows,), name=name, in_specs=[spec] * 4, out_specs=[spec] * 3, out_shape=[sds] * 3,
        compiler_params=_cparams(("parallel",)),
    )(w, g, m, v)


def _me():
    return lax.axis_index("x"), lax.axis_index("y"), lax.axis_index("c")


def _flip(v, bit):
    return 1 - v if bit else v


ANY = pl.BlockSpec(memory_space=pl.ANY)


def all_gather8(x, name):
    r, c = x.shape

    def body(x_ref, o_ref, ssem, rsem, lsem):
        mx, my, mc = _me()
        me = 4 * mx + 2 * my + mc
        local = pltpu.make_async_copy(x_ref, o_ref.at[me], lsem)
        local.start()

        def copy(k):
            px, py, pc = _flip(mx, k & 4), _flip(my, k & 2), _flip(mc, k & 1)
            return px, py, pc

        sends = []
        for k in range(1, NDEV):
            cp = pltpu.make_async_remote_copy(src_ref=x_ref, dst_ref=o_ref.at[me], send_sem=ssem.at[k - 1],
                                              recv_sem=rsem.at[k - 1], device_id=copy(k), device_id_type=MESH)
            cp.start()
            sends.append(cp)
        for k in range(1, NDEV):
            px, py, pc = copy(k)
            pltpu.make_async_remote_copy(src_ref=x_ref, dst_ref=o_ref.at[4 * px + 2 * py + pc], send_sem=ssem.at[k - 1],
                                         recv_sem=rsem.at[k - 1], device_id=(px, py, pc), device_id_type=MESH).wait_recv()
        for cp in sends:
            cp.wait_send()
        local.wait()

    vm = pl.BlockSpec(memory_space=pltpu.VMEM)
    return pl.pallas_call(
        body, name=name, in_specs=[vm], out_specs=vm, out_shape=jax.ShapeDtypeStruct((NDEV, r, c), x.dtype),
        scratch_shapes=[pltpu.SemaphoreType.DMA((NDEV - 1,)), pltpu.SemaphoreType.DMA((NDEV - 1,)), pltpu.SemaphoreType.DMA],
    )(x)


def _chip_peers(mx, my):
    out = []
    for p in range(1, NCHIP):
        px, py = _flip(mx, p & 2), _flip(my, p & 1)
        out.append((px, py, 2 * px + py))
    return out


class ChipExchange:
    def __init__(self, kind, arrays):
        self.kind = kind
        self.n = len(arrays)
        if kind == "gather":
            self.out_shape = [jax.ShapeDtypeStruct((NCHIP,) + a.shape, a.dtype) for a in arrays]
        else:
            self.out_shape = [jax.ShapeDtypeStruct(a.shape, a.dtype) for a in arrays]
        self.scratch = [pltpu.SemaphoreType.DMA((3 * self.n,)), pltpu.SemaphoreType.DMA((3 * self.n,)),
                        pltpu.SemaphoreType.DMA((self.n,))]

    def _copies(self, ins, outs, sems):
        ssem, rsem, lsem = sems
        mx, my, mc = _me()
        jme = 2 * mx + my
        gather = self.kind == "gather"
        local, sends, waits = [], [], []
        for k in range(self.n):
            local.append(pltpu.make_async_copy(ins[k] if gather else ins[k].at[jme], outs[k].at[jme], lsem.at[k]))
            for p, (px, py, jp) in enumerate(_chip_peers(mx, my)):
                src = ins[k] if gather else ins[k].at[jp]
                sem = dict(send_sem=ssem.at[3 * k + p], recv_sem=rsem.at[3 * k + p], device_id=(px, py, mc),
                           device_id_type=MESH)
                sends.append(pltpu.make_async_remote_copy(src_ref=src, dst_ref=outs[k].at[jme], **sem))
                waits.append(pltpu.make_async_remote_copy(src_ref=src, dst_ref=outs[k].at[jp], **sem))
        return local, sends, waits

    def start(self, ins, outs, sems):
        local, sends, _ = self._copies(ins, outs, sems)
        for cp in local + sends:
            cp.start()

    def wait(self, ins, outs, sems):
        local, _, waits = self._copies(ins, outs, sems)
        for cp in waits:
            cp.wait_recv()
        for cp in waits:
            cp.wait_send()
        for cp in local:
            cp.wait()


def chip_exchange(kind, arrays, name):
    ex = ChipExchange(kind, arrays)
    n = ex.n

    def body(*refs):
        ins, outs, sems = refs[:n], refs[n:2 * n], refs[2 * n:]
        ex.start(ins, outs, sems)
        ex.wait(ins, outs, sems)

    return pl.pallas_call(body, name=name, in_specs=[ANY] * n, out_specs=[ANY] * n, out_shape=ex.out_shape,
                          scratch_shapes=ex.scratch)(*arrays)


def sibling_add(g, ngrp, hr, tr, name):
    c_ = g.shape[1]
    nt = hr // tr

    def body(cidx, keep_ref, give_ref, o_ref, land, ssem, rsem):
        mx, my, mc = _me()
        t = pl.program_id(0) * nt + pl.program_id(1)
        s = t % 2
        cp = pltpu.make_async_remote_copy(src_ref=give_ref, dst_ref=land.at[s], send_sem=ssem.at[s], recv_sem=rsem.at[s],
                                          device_id=(mx, my, 1 - mc), device_id_type=MESH)
        cp.start()
        cp.wait_recv()
        o_ref[...] = keep_ref[...] + land[s]
        cp.wait_send()

    grid_spec = pltpu.PrefetchScalarGridSpec(
        num_scalar_prefetch=1, grid=(ngrp, nt),
        in_specs=[pl.BlockSpec((tr, c_), lambda i, j, cr: ((2 * i + cr[0]) * nt + j, 0)),
                  pl.BlockSpec((tr, c_), lambda i, j, cr: ((2 * i + 1 - cr[0]) * nt + j, 0))],
        out_specs=pl.BlockSpec((tr, c_), lambda i, j, cr: (i * nt + j, 0)),
        scratch_shapes=[pltpu.VMEM((2, tr, c_), F32), pltpu.SemaphoreType.DMA((2,)), pltpu.SemaphoreType.DMA((2,))])
    cidx = lax.axis_index("c").astype(jnp.int32).reshape(1)
    return pl.pallas_call(body, grid_spec=grid_spec, name=name, out_shape=jax.ShapeDtypeStruct((ngrp * hr, c_), F32),
                          compiler_params=_cparams(("arbitrary", "arbitrary")))(cidx, g, g)


def finish_weight(b, w, m, v, tr, name):
    _, r2, c_ = b.shape

    def body(b_ref, w_ref, m_ref, v_ref, g_ref, d_ref, nm_ref, nv_ref, mine, land, ssem, rsem):
        mx, my, mc = _me()
        t = pl.program_id(0)
        s = t % 2
        mine[s] = (b_ref[0] + b_ref[1]) + (b_ref[2] + b_ref[3])
        cp = pltpu.make_async_remote_copy(src_ref=mine.at[s], dst_ref=land.at[s], send_sem=ssem.at[s], recv_sem=rsem.at[s],
                                          device_id=(mx, my, 1 - mc), device_id_type=MESH)
        cp.start()
        cp.wait_recv()
        g_ref[mc] = mine[s]
        g_ref[1 - mc] = land[s]
        cp.wait_send()
        gg = g_ref[...]
        nm = ADAM_B1 * m_ref[...] + (1.0 - ADAM_B1) * gg
        nv = ADAM_B2 * v_ref[...] + (1.0 - ADAM_B2) * (gg * gg)
        m_hat = nm / (1.0 - ADAM_B1 ** ADAM_STEP)
        v_hat = nv / (1.0 - ADAM_B2 ** ADAM_STEP)
        d_ref[...] = -ADAM_LR * (m_hat / (jnp.sqrt(v_hat) + ADAM_EPS) + ADAM_WD * w_ref[...])
        nm_ref[...] = nm
        nv_ref[...] = nv

    spec = pl.BlockSpec((2, tr, c_), lambda i: (0, i, 0))
    sds = jax.ShapeDtypeStruct((2, r2, c_), F32)
    return pl.pallas_call(
        body, grid=(r2 // tr,), name=name,
        in_specs=[pl.BlockSpec((NCHIP, tr, c_), lambda i: (0, i, 0)), spec, spec, spec],
        out_specs=[spec] * 4, out_shape=[sds] * 4,
        scratch_shapes=[pltpu.VMEM((2, tr, c_), F32), pltpu.VMEM((2, tr, c_), F32), pltpu.SemaphoreType.DMA((2,)),
                        pltpu.SemaphoreType.DMA((2,))],
        compiler_params=_cparams(("arbitrary",)))(b, w, m, v)


TM_IN = 256
TM_GLA = 512
TM_MERGE = 512
TM_FFN = 256
TN_WGRAD = 1024

WEIGHTS = ['c_ctx', 'w_mod', 'b_mod', 'norm1_g', 'norm2_g', 'w_in', 'conv_w', 'conv_b', 'conv_ln_g', 'conv_ln_b', 'w_a2_f',
           'b_a_f', 'w_a2_b', 'b_a_b', 'gla_norm_g', 'w_out', 'w_gate', 'w_up', 'w_down', 'final_g']
BIG = ['w_in', 'w_out', 'w_gate', 'w_up', 'w_down']


def _rows(*vs):
    out = jnp.zeros((8, vs[0].shape[-1]), F32)
    for i, v in enumerate(vs):
        out = out.at[i].set(v.reshape(-1))
    return out


def _small_slab(p):
    cat = lambda *ks: jnp.concatenate([p[k].reshape(-1) for k in ks])
    vecs = _rows(p['c_ctx'], p['norm1_g'], p['norm2_g'], p['final_g'], cat('conv_b', 'conv_ln_g'),
                 cat('conv_ln_b', 'b_a_f', 'b_a_b'), jnp.pad(p['gla_norm_g'].reshape(-1), (0, D - HV)))
    bmod = jnp.pad(p['b_mod'].reshape(6, D), ((0, 2), (0, 0)))
    shards = jnp.pad(jnp.concatenate([jnp.pad(p['conv_w'].reshape(-1), (0, DC // NCHIP)), cat('w_a2_f', 'w_a2_b')]),
                     (0, 2 * D)).reshape(8, D)
    return jnp.concatenate([vecs, bmod, shards], axis=0)


def _unslab(s):
    return {
        'c_ctx': s[0], 'norm1_g': s[1:2], 'norm2_g': s[2:3], 'final_g': s[3],
        'conv_b': s[4:5, :DC], 'conv_ln_g': s[4:5, DC:], 'conv_ln_b': s[5:6, :DC],
        'b_a_f': s[5:6, DC:DC + DK], 'b_a_b': s[5:6, DC + DK:], 'gla_norm_g': s[6:7, :HV],
        'b_mod': s[8:14].reshape(1, 6 * D),
        'conv_w': s[16:20].reshape(32, DC // NCHIP)[:CW].reshape(1, CW, DC // NCHIP),
        'w_a2_f': s[20].reshape(1, RANK, DK // NCHIP), 'w_a2_b': s[21].reshape(1, RANK, DK // NCHIP),
    }


def kernel(x, c, ctx, c_ctx, w_mod, b_mod, norm1_g, norm2_g, w_in, conv_w, conv_b, conv_ln_g, conv_ln_b, w_a2_f, b_a_f, w_a2_b, b_a_b, gla_norm_g, w_out, w_gate, w_up, w_down, final_g, loss_target, m_c_ctx, m_w_mod, m_b_mod, m_norm1_g, m_norm2_g, m_w_in, m_conv_w, m_conv_b, m_conv_ln_g, m_conv_ln_b, m_w_a2_f, m_b_a_f, m_w_a2_b, m_b_a_b, m_gla_norm_g, m_w_out, m_w_gate, m_w_up, m_w_down, m_final_g, v_c_ctx, v_w_mod, v_b_mod, v_norm1_g, v_norm2_g, v_w_in, v_conv_w, v_conv_b, v_conv_ln_g, v_conv_ln_b, v_w_a2_f, v_b_a_f, v_w_a2_b, v_b_a_b, v_gla_norm_g, v_w_out, v_w_gate, v_w_up, v_w_down, v_final_g):
    w = dict(c_ctx=c_ctx, w_mod=w_mod, b_mod=b_mod, norm1_g=norm1_g, norm2_g=norm2_g, w_in=w_in, conv_w=conv_w, conv_b=conv_b,
             conv_ln_g=conv_ln_g, conv_ln_b=conv_ln_b, w_a2_f=w_a2_f, b_a_f=b_a_f, w_a2_b=w_a2_b, b_a_b=b_a_b,
             gla_norm_g=gla_norm_g, w_out=w_out, w_gate=w_gate, w_up=w_up, w_down=w_down, final_g=final_g)
    m = dict(c_ctx=m_c_ctx, w_mod=m_w_mod, b_mod=m_b_mod, norm1_g=m_norm1_g, norm2_g=m_norm2_g, w_in=m_w_in, conv_w=m_conv_w,
             conv_b=m_conv_b, conv_ln_g=m_conv_ln_g, conv_ln_b=m_conv_ln_b, w_a2_f=m_w_a2_f, b_a_f=m_b_a_f, w_a2_b=m_w_a2_b,
             b_a_b=m_b_a_b, gla_norm_g=m_gla_norm_g, w_out=m_w_out, w_gate=m_w_gate, w_up=m_w_up, w_down=m_w_down,
             final_g=m_final_g)
    v = dict(c_ctx=v_c_ctx, w_mod=v_w_mod, b_mod=v_b_mod, norm1_g=v_norm1_g, norm2_g=v_norm2_g, w_in=v_w_in, conv_w=v_conv_w,
             conv_b=v_conv_b, conv_ln_g=v_conv_ln_g, conv_ln_b=v_conv_ln_b, w_a2_f=v_w_a2_f, b_a_f=v_b_a_f, w_a2_b=v_w_a2_b,
             b_a_b=v_b_a_b, gla_norm_g=v_gla_norm_g, w_out=v_w_out, w_gate=v_w_gate, w_up=v_w_up, w_down=v_w_down,
             final_g=v_final_g)
    mx, my, mc = _me()
    jme = 2 * mx + my
    me = 4 * mx + 2 * my + mc
    wmc = D * 6 // NCHIP
    xx, tgt, cx = x[0], loss_target[0], ctx[0]
    n = xx.shape[0]

    sw = jnp.concatenate([jnp.pad(conv_w[0], ((0, 1), (0, 0))).reshape(-1), w_a2_f[0].reshape(-1), w_a2_b[0].reshape(-1)])
    cs8 = all_gather8(jnp.concatenate([_rows(c[0]), jnp.pad(sw.reshape(6, D), ((0, 2), (0, 0)))], axis=0), "gather_c_small_w")
    c8 = cs8[:, 0, :]
    swc = jnp.stack([cs8[2 * j, 8:16] for j in range(NCHIP)]).reshape(NCHIP, 8 * D)
    convw = jnp.transpose(swc[:, :32 * 128].reshape(NCHIP, 32, 128), (1, 0, 2)).reshape(32, DC)
    a2 = lambda o: jnp.transpose(swc[:, o:o + RANK * 64].reshape(NCHIP, RANK, 64), (1, 0, 2)).reshape(RANK, DK)
    wa2 = jnp.zeros((128, 2 * DK), F32).at[0:RANK, 0:DK].set(a2(32 * 128)).at[RANK:2 * RANK, DK:].set(a2(32 * 128 + RANK * 64))
    wa2 = wa2.astype(BF)

    cext = jnp.concatenate([c8, _rows(c_ctx)], axis=0)
    mloc = mod_fwd(cext, w_mod[0], lax.dynamic_slice_in_dim(b_mod, jme * wmc, wmc, axis=1))
    mall = all_gather8(mloc, "gather_mod")
    mall = jnp.concatenate([mall[2 * j] for j in range(NCHIP)], axis=1)
    sh1, sc1, g1, sh2, sc2, g2 = jnp.split(lax.dynamic_slice_in_dim(mall, me, 1, axis=0)[0], 6)
    csh1, csc1 = mall[8, :D], mall[8, D:2 * D]

    bshard = [w[k][0].astype(BF) for k in BIG]
    cols = lambda a: jnp.transpose(a, (1, 0, 2)).reshape(a.shape[1], -1)
    win = jnp.pad(cols(chip_exchange("gather", bshard[:1], "gather_w_in")[0]), ((0, 0), (0, DINP - DIN)))
    ba = jnp.concatenate([b_a_f, b_a_b], axis=1)
    cvec = _rows(conv_b, conv_ln_g, conv_ln_b)
    vec1 = _rows(norm1_g, sh1, sc1)
    vecc = _rows(norm1_g, csh1, csc1)
    vecm = _rows(g1)
    vecf = _rows(norm2_g, sh2, sc2, g2, final_g)
    gn = jnp.tile(gla_norm_g, (1, NH))

    s0 = ctx_fwd(cx, vecc, win, wa2, ba)
    res = fwd_in(xx, vec1, win, convw, cvec, wa2, ba, TM_IN, ChipExchange("gather", bshard[1:]), bshard[1:])
    ag, yb, co, qk, vv, gg, la, r = res[:8]
    wout = res[8].reshape(D, D)
    wg, wu = cols(res[9]), cols(res[10])
    wd = res[11].reshape(DFF, D)
    o_f, o_b, se_f, se_b = gla_fwd(qk, vv, la, s0, TM_GLA)
    x1, y1, cat = merge_fwd(xx, o_f, o_b, gg, co, vecm, gn, wout, TM_MERGE)

    dx1, h2, act, dgt, dup, dy2, sf = ffn_fwd_bwd(x1, tgt, vecf, wg, wu, wd, TM_FFN)
    d_wg = wgrad(h2, dgt, None, D, DFF // 2, TN_WGRAD, "wgrad_gate")
    d_wu = wgrad(h2, dup, None, D, DFF // 2, TN_WGRAD, "wgrad_up")
    d_wd = wgrad(act, dy2, None, DFF // 2, D, TN_WGRAD, "wgrad_down")
    dy1, dco, do, dg, s1, s2 = merge_bwd(dx1, y1, o_f, o_b, gg, vecm, gn, wout, TM_MERGE)
    d_wout = wgrad(cat, dy1, None, D, D, TN_WGRAD, "wgrad_out")

    shard = lambda a, k: jnp.transpose(a.reshape(a.shape[0], NCHIP, k), (1, 0, 2))
    hd = D // 2
    parts = [sibling_add(d_wout, NCHIP, hd // NCHIP, hd // NCHIP, "xadd_w_out").reshape(NCHIP, hd // NCHIP, D),
             shard(sibling_add(d_wg, 1, hd, 128, "xadd_w_gate"), DFF // NCHIP),
             shard(sibling_add(d_wu, 1, hd, 128, "xadd_w_up"), DFF // NCHIP),
             sibling_add(d_wd, NCHIP, DFF // 8, DFF // 16, "xadd_w_down").reshape(NCHIP, DFF // 8, D)]
    res = gla_bwd(qk, vv, la, do, se_f, se_b, TM_GLA, ChipExchange("scatter", parts), parts)
    dqk_f, dv_f, dla_f, dqk_b, dv_b, dla_b, ds0 = res[:7]
    recv = list(res[7:])
    dwin_c, dwa2_c, sc = ctx_bwd(cx, vecc, win, wa2, ba, ds0)
    grad_x, h, dp, dwa2, dcw, sd = bwd_in(xx, dx1, ag, yb, dco, dqk_f, dqk_b, dv_f, dv_b, dg, dla_f, dla_b, la, r,
                                          vec1, win, convw, cvec, wa2, TM_IN)
    d_win = wgrad(h, dp, dwin_c, D, DINP // 3, TN_WGRAD, "wgrad_in")
    part_in = shard(sibling_add(d_win, 1, hd, 128, "xadd_w_in")[:, :DIN], DIN // NCHIP)
    recv = list(chip_exchange("scatter", [part_in], "scatter_w_in")) + recv

    rows16, dcw_t, dwa2_t = pack_small(sf, s1, s2, sd, sc, dcw, dwa2, dwa2_c)
    sp = jnp.concatenate([rows16, dcw_t.reshape(16, D), dwa2_t.reshape(16, D)], axis=0)
    g8 = all_gather8(sp, "gather_small_grads")
    tot, bm_g, loss8 = small_totals(g8)
    loss = loss8[0, 0]
    dmod8 = g8[:, 0:6, :].reshape(NDEV, 6 * D)
    dmodc = jnp.concatenate([tot[6], tot[7], jnp.zeros((4 * D,), F32)])
    dm = jnp.concatenate([dmod8, _rows(dmodc)], axis=0)
    dm = lax.dynamic_slice_in_dim(dm, jme * wmc, wmc, axis=1)
    g_wmod, dsil = mod_bwd(cext, dm, w_mod[0])
    p8 = all_gather8(dsil[8:16], "gather_dsilu")
    g_cctx = cctx_grad(p8, _rows(c_ctx))[0]

    grads, delta, new_m, new_v = {}, {}, {}, {}
    for i, k in enumerate(BIG):
        r2, cc = recv[i].shape[1:]
        halves = lambda a: a[0].reshape(2, r2, cc)
        outs = finish_weight(recv[i], halves(w[k]), halves(m[k]), halves(v[k]), 88 if r2 % 128 else 128, "finish_" + k)
        grads[k], delta[k], new_m[k], new_v[k] = (o.reshape(w[k].shape) for o in outs)
    grads['w_mod'] = g_wmod[None]
    d_, m_, v_ = adamw(w_mod[0], g_wmod, m_w_mod[0], v_w_mod[0], 128, "adamw_w_mod")
    delta['w_mod'], new_m['w_mod'], new_v['w_mod'] = d_[None], m_[None], v_[None]
    small_g = {
        'c_ctx': g_cctx, 'b_mod': bm_g[0:6].reshape(1, 6 * D), 'norm1_g': tot[8:9], 'norm2_g': tot[9:10], 'final_g': tot[10],
        'conv_b': tot[11:12, :DC], 'conv_ln_g': tot[11:12, DC:], 'conv_ln_b': tot[12:13, :DC],
        'b_a_f': tot[12:13, DC:DC + DK], 'b_a_b': tot[12:13, DC + DK:], 'gla_norm_g': tot[13:14, :HV],
        'conv_w': lax.dynamic_slice_in_dim(tot[16:32].reshape(32, DC)[:CW], jme * (DC // NCHIP), DC // NCHIP, axis=1)[None],
        'w_a2_f': lax.dynamic_slice_in_dim(tot[32:48].reshape(32, 2 * DK)[0:RANK, 0:DK], jme * (DK // NCHIP), DK // NCHIP, axis=1)[None],
        'w_a2_b': lax.dynamic_slice_in_dim(tot[32:48].reshape(32, 2 * DK)[RANK:2 * RANK, DK:], jme * (DK // NCHIP), DK // NCHIP, axis=1)[None],
    }
    grads.update(small_g)
    sd_, sm_, sv_ = adamw(_small_slab(w), _small_slab(small_g), _small_slab(m), _small_slab(v), 24,
                          "adamw_small")
    for dst, slab in ((delta, sd_), (new_m, sm_), (new_v, sv_)):
        dst.update(_unslab(slab))
    out = [loss, grad_x[None]]
    for group in (grads, delta, new_m, new_v):
        out += [group[k].reshape(w[k].shape) for k in WEIGHTS]
    return tuple(out)
```

```python
import functools

import jax
import jax.numpy as jnp
from jax import lax
from jax.experimental import pallas as pl
from jax.experimental.pallas import tpu as pltpu

F32 = jnp.float32
BF = jnp.bfloat16

D = 1024
DC = 512
NH = 4
HK = 64
HV = 128
DK = NH * HK
DV = NH * HV
RANK = 16
CH = 64
GW = 64
CW = 31
CPAD = CW // 2
SEGP = GW + 32
DFF = 2816
DIN = 2592
DINP = 2688
EPS = 1e-6
TAU = 16.0
QSCALE = HK ** -0.5
NCHIP = 4
NDEV = 8

ADAM_LR = 0.001
ADAM_B1 = 0.9
ADAM_B2 = 0.999
ADAM_EPS = 1e-08
ADAM_WD = 0.01
ADAM_STEP = 10

VMEM_LIMIT = 56 * 1024 * 1024
MESH = pl.DeviceIdType.MESH


def _dot(a, b):
    return jnp.dot(a, b, preferred_element_type=F32)


def _dot_nt(a, b):
    return lax.dot_general(a, b, (((1,), (1,)), ((), ())), preferred_element_type=F32)


def _dot_tn(a, b):
    return lax.dot_general(a, b, (((0,), (0,)), ((), ())), preferred_element_type=F32)


def _split3(x):
    hi = x.astype(BF)
    r1 = x - hi.astype(F32)
    mid = r1.astype(BF)
    lo = (r1 - mid.astype(F32)).astype(BF)
    return hi, mid, lo


def _mask_dot(t, x):
    hi, mid, lo = _split3(x)
    return _dot(t, hi) + _dot(t, mid) + _dot(t, lo)


def _sigmoid(x):
    return 1.0 / (1.0 + jnp.exp(-x))


def _log_sigmoid(x):
    return jnp.minimum(x, 0.0) - jnp.log(1.0 + jnp.exp(-jnp.abs(x)))


def _colsum8(z):
    t, c = z.shape
    return jnp.sum(z.reshape(t // 8, 8, c), axis=0)


def _tri(n, kind):
    r = lax.broadcasted_iota(jnp.int32, (n, n), 0)
    c = lax.broadcasted_iota(jnp.int32, (n, n), 1)
    m = {"le": c <= r, "lt": c < r, "ge": c >= r, "gt": c > r}[kind]
    return m


def _full(shape):
    nd = len(shape)
    return pl.BlockSpec(shape, lambda *_: (0,) * nd)


def _cparams(sem, vmem=VMEM_LIMIT):
    return pltpu.CompilerParams(dimension_semantics=sem, vmem_limit_bytes=vmem)


def _call(body, grid, name, in_specs, out_specs, out_shape, scratch, operands, exchange=None, carried=()):
    n_in, n_out, n_scr = len(in_specs), len(out_specs), len(scratch)
    if exchange is None:
        fn = body
    else:
        n = exchange.n

        def fn(*refs):
            ins, cin = refs[:n_in], refs[n_in:n_in + n]
            outs, cout = refs[n_in + n:n_in + n + n_out], refs[n_in + n + n_out:n_in + 2 * n + n_out]
            rest = refs[n_in + 2 * n + n_out:]
            scr, sems = rest[:n_scr], rest[n_scr:]

            @pl.when(pl.program_id(0) == 0)
            def _():
                exchange.start(cin, cout, sems)

            body(*ins, *outs, *scr)

            @pl.when(pl.program_id(0) == pl.num_programs(0) - 1)
            def _():
                exchange.wait(cin, cout, sems)

        any_spec = pl.BlockSpec(memory_space=pl.ANY)
        in_specs = list(in_specs) + [any_spec] * n
        out_specs = list(out_specs) + [any_spec] * n
        out_shape = list(out_shape) + exchange.out_shape
        scratch = list(scratch) + exchange.scratch
    return pl.pallas_call(fn, grid=grid, name=name, in_specs=in_specs, out_specs=out_specs, out_shape=out_shape,
                          scratch_shapes=scratch, compiler_params=_cparams(("arbitrary",)))(*operands, *carried)


def _fill_padded(pad_ref, val, nseg):
    zeros = jnp.zeros((nseg, 16, val.shape[-1]), F32)
    pad_ref[:, 0:16, :] = zeros
    pad_ref[:, 16 + GW:SEGP, :] = zeros
    pad_ref[:, 16:16 + GW, :] = val.reshape(nseg, GW, val.shape[-1])


def _tap_slabs(pad_ref, s, cs):
    whole = pad_ref[s, :, cs]
    for r in range(8):
        slab = whole if r == 0 else pltpu.roll(whole, SEGP - r, axis=0)
        for a in range(4):
            j = r + 8 * a - 1
            if 0 <= j < CW:
                yield j, slab[8 * a:8 * a + GW]


def _conv_taps(pad_ref, s, w_ref, c0, cw, flip):
    acc = jnp.zeros((GW, cw), F32)
    for j, rows in _tap_slabs(pad_ref, s, pl.ds(c0, cw)):
        acc = acc + w_ref[pl.ds((CW - 1 - j) if flip else j, 1), pl.ds(c0, cw)] * rows
    return acc


def _ln_stats(yb):
    mu = jnp.mean(yb, axis=-1, keepdims=True)
    yc = yb - mu
    var = jnp.mean(yc * yc, axis=-1, keepdims=True)
    rs = lax.rsqrt(var + EPS)
    return yc * rs, rs


def fwd_in(x, vec1, win, convw, cvec, wa2, ba, tm, exchange=None, carried=()):
    n = x.shape[0]
    nseg = tm // GW
    cg = 128

    def body(x_ref, vec_ref, win_ref, cw_ref, cv_ref, wa2_ref, ba_ref,
             ag_ref, yb_ref, co_ref, qk_ref, v_ref, g_ref, la_ref, r_ref, pad_ref):
        xx = x_ref[...]
        rstd = lax.rsqrt(jnp.mean(xx * xx, axis=-1, keepdims=True) + EPS)
        h = (xx * rstd * vec_ref[0:1, :]) * (1.0 + vec_ref[2:3, :]) + vec_ref[1:2, :]
        p = _dot(h.astype(BF), win_ref[...])
        ag_ref[...] = p[:, :2 * DC].astype(BF)
        qk_ref[...] = p[:, 2 * DC:2 * DC + 2 * DK].astype(BF)
        v_ref[...] = p[:, 2 * DC + 2 * DK:2 * DC + 2 * DK + DV].astype(BF)
        g_ref[...] = p[:, 2 * DC + 2 * DK + DV:2 * DC + 2 * DK + 2 * DV].astype(BF)
        r = p[:, DINP - 128:].astype(BF)
        r_ref[...] = r
        la_ref[...] = _log_sigmoid(_dot(r, wa2_ref[...]) + ba_ref[...]) * (1.0 / TAU)

        _fill_padded(pad_ref, p[:, :DC] * _sigmoid(p[:, DC:2 * DC]), nseg)

        def seg(s, carry):
            for c0 in range(0, DC, cg):
                y = _conv_taps(pad_ref, s, cw_ref, c0, cg, False)
                yb_ref[pl.ds(pl.multiple_of(s * GW, GW), GW), pl.ds(c0, cg)] = y + cv_ref[0:1, c0:c0 + cg]
            return carry

        lax.fori_loop(0, nseg, seg, 0)
        yn, _ = _ln_stats(yb_ref[...])
        ln = yn * cv_ref[1:2, :] + cv_ref[2:3, :]
        co_ref[...] = (ln * _sigmoid(ln)).astype(BF)

    tok = lambda w: pl.BlockSpec((tm, w), lambda i: (i, 0))
    return _call(
        body, (n // tm,), "fwd_in",
        [tok(D), _full(vec1.shape), _full(win.shape), _full(convw.shape), _full(cvec.shape), _full(wa2.shape), _full(ba.shape)],
        [tok(2 * DC), tok(DC), tok(DC), tok(2 * DK), tok(DV), tok(DV), tok(2 * DK), tok(128)],
        [jax.ShapeDtypeStruct((n, 2 * DC), BF), jax.ShapeDtypeStruct((n, DC), F32),
         jax.ShapeDtypeStruct((n, DC), BF), jax.ShapeDtypeStruct((n, 2 * DK), BF),
         jax.ShapeDtypeStruct((n, DV), BF), jax.ShapeDtypeStruct((n, DV), BF),
         jax.ShapeDtypeStruct((n, 2 * DK), F32), jax.ShapeDtypeStruct((n, 128), BF)],
        [pltpu.VMEM((nseg, SEGP, DC), F32)],
        (x, vec1, win, convw, cvec, wa2, ba), exchange, carried)


def _gla_dir(d):
    return (_tri(CH, "le"), CH - 1) if d == 0 else (_tri(CH, "ge"), 0)


def _gla_chunk_terms(qk, la, d):
    seen, last = _gla_dir(d)
    b = _mask_dot(seen.astype(BF), la)
    bl = b[last:last + 1, :]
    eb = jnp.exp(b)
    enb = jnp.exp(-b)
    ekd = jnp.exp(bl - b)
    ebl = jnp.exp(bl)
    q = qk[:, :DK].astype(F32) * QSCALE
    k = qk[:, DK:].astype(F32)
    return eb, enb, ekd, ebl, q * eb, k * enb, k * ekd


NP = NH // 2
PW = 2 * HK


def _lo_lanes(shape):
    return lax.broadcasted_iota(jnp.int32, shape, len(shape) - 1) < HK


def _pair_sel(lo, hi):
    return jnp.where(_lo_lanes(lo.shape), lo, hi)


def _only(x, which):
    keep = _lo_lanes(x.shape) if which == 0 else jnp.logical_not(_lo_lanes(x.shape))
    return jnp.where(keep, x, jnp.zeros_like(x))


def gla_fwd(qk, v, la, s0, tm):
    n = qk.shape[0]
    nt = n // tm
    nc = tm // CH

    def body(qkf_ref, vf_ref, laf_ref, qkb_ref, vb_ref, lab_ref, s0_ref, of_ref, ob_ref, sef_ref, seb_ref, st_ref):
        @pl.when(pl.program_id(0) == 0)
        def _():
            st_ref[...] = s0_ref[...]

        def chunk(ci, carry):
            t = []
            for d, (qk_ref, v_ref, la_ref) in enumerate(((qkf_ref, vf_ref, laf_ref), (qkb_ref, vb_ref, lab_ref))):
                c = ci if d == 0 else nc - 1 - ci
                rows = pl.ds(pl.multiple_of(c * CH, CH), CH)
                eb, enb, ekd, ebl, qt, kt, kd = _gla_chunk_terms(qk_ref[rows, :], la_ref[rows, :], d)
                t.append(dict(c=c, rows=rows, ebl=ebl, qt=qt.astype(BF), kt=kt.astype(BF), kd=kd.astype(BF),
                              vv=v_ref[rows, :], st=[st_ref[d, p] for p in range(NP)], amask=_gla_dir(d)[0]))
            dh = [(d, h) for d in range(2) for h in range(NH)]
            ps = lambda h: slice((h // 2) * PW, (h // 2 + 1) * PW)
            vs = lambda h: slice(h * HV, (h + 1) * HV)
            qm = {(d, h): _only(t[d]['qt'][:, ps(h)], h % 2) for d, h in dh}
            a = {(d, h): jnp.where(t[d]['amask'], _dot_nt(qm[d, h], t[d]['kt'][:, ps(h)]), 0.0).astype(BF) for d, h in dh}
            o = {(d, h): _dot(a[d, h], t[d]['vv'][:, vs(h)]) + _dot_nt(qm[d, h], t[d]['st'][h // 2].astype(BF))
                 for d, h in dh}
            kv = {(d, h): _dot_tn(t[d]['vv'][:, vs(h)], t[d]['kd'][:, ps(h)]) for d, h in dh}
            for d, (o_ref, se_ref) in enumerate(((of_ref, sef_ref), (ob_ref, seb_ref))):
                for h in range(NH):
                    o_ref[t[d]['rows'], vs(h)] = o[d, h].astype(BF)
                for p in range(NP):
                    se_ref[t[d]['c'], p] = t[d]['st'][p]
                    st_ref[d, p] = (t[d]['ebl'][:, p * PW:(p + 1) * PW] * t[d]['st'][p]
                                    + _pair_sel(kv[d, 2 * p], kv[d, 2 * p + 1]))
            return carry

        lax.fori_loop(0, nc, chunk, 0, unroll=2)

    fw = lambda w, col=0: pl.BlockSpec((tm, w), lambda i: (i, col))
    bw = lambda w, col=0: pl.BlockSpec((tm, w), lambda i: (nt - 1 - i, col))
    se_f = pl.BlockSpec((nc, NP, HV, PW), lambda i: (i, 0, 0, 0))
    se_b = pl.BlockSpec((nc, NP, HV, PW), lambda i: (nt - 1 - i, 0, 0, 0))
    se_shape = jax.ShapeDtypeStruct((n // CH, NP, HV, PW), F32)
    return pl.pallas_call(
        body, grid=(nt,), name="gla_fwd",
        in_specs=[fw(2 * DK), fw(DV), fw(DK, 0), bw(2 * DK), bw(DV), bw(DK, 1), _full(s0.shape)],
        out_specs=[fw(DV), bw(DV), se_f, se_b],
        out_shape=[jax.ShapeDtypeStruct((n, DV), BF), jax.ShapeDtypeStruct((n, DV), BF), se_shape, se_shape],
        scratch_shapes=[pltpu.VMEM((2, NP, HV, PW), F32)],
        compiler_params=_cparams(("arbitrary",)),
    )(qk, v, la, qk, v, la, s0)


def gla_bwd(qk, v, la, do, se_f, se_b, tm, exchange=None, carried=()):
    n = qk.shape[0]
    nt = n // tm
    nc = tm // CH

    def body(qkf_ref, vf_ref, laf_ref, dof_ref, sef_ref, qkb_ref, vb_ref, lab_ref, dob_ref, seb_ref,
             dqkf_ref, dvf_ref, dlaf_ref, dqkb_ref, dvb_ref, dlab_ref, ds0_ref, ds_ref):
        @pl.when(pl.program_id(0) == 0)
        def _():
            ds_ref[...] = jnp.zeros_like(ds_ref)

        def chunk(ci, carry):
            t = []
            for d, (qk_ref, v_ref, la_ref, do_ref, se_ref) in enumerate(
                    ((qkf_ref, vf_ref, laf_ref, dof_ref, sef_ref), (qkb_ref, vb_ref, lab_ref, dob_ref, seb_ref))):
                c = nc - 1 - ci if d == 0 else ci
                rows = pl.ds(pl.multiple_of(c * CH, CH), CH)
                amask, last = _gla_dir(d)
                eb, enb, ekd, ebl, qt, kt, kd = _gla_chunk_terms(qk_ref[rows, :], la_ref[rows, :], d)
                t.append(dict(rows=rows, amask=amask, last=last, eb=eb, enb=enb, ekd=ekd, ebl=ebl, qt=qt, kt=kt, kd=kd,
                              qtb=qt.astype(BF), ktb=kt.astype(BF), kdb=kd.astype(BF), vv=v_ref[rows, :], dd=do_ref[rows, :],
                              st=[se_ref[c, p] for p in range(NP)], dsn=[ds_ref[d, p] for p in range(NP)]))
            dh = [(d, h) for d in range(2) for h in range(NH)]
            dp = [(d, p) for d in range(2) for p in range(NP)]
            ps = lambda h: slice((h // 2) * PW, (h // 2 + 1) * PW)
            vs = lambda h: slice(h * HV, (h + 1) * HV)
            stb = {(d, p): t[d]['st'][p].astype(BF) for d, p in dp}
            dsnb = {(d, p): t[d]['dsn'][p].astype(BF) for d, p in dp}
            qm = {(d, h): _only(t[d]['qtb'][:, ps(h)], h % 2) for d, h in dh}
            km = {(d, h): _only(t[d]['kdb'][:, ps(h)], h % 2) for d, h in dh}
            a = {(d, h): jnp.where(t[d]['amask'], _dot_nt(qm[d, h], t[d]['ktb'][:, ps(h)]), 0.0).astype(BF) for d, h in dh}
            da = {(d, h): jnp.where(t[d]['amask'], _dot_nt(t[d]['dd'][:, vs(h)], t[d]['vv'][:, vs(h)]), 0.0).astype(BF)
                  for d, h in dh}
            dv = {(d, h): _dot_tn(a[d, h], t[d]['dd'][:, vs(h)]) + _dot_nt(km[d, h], dsnb[d, h // 2]) for d, h in dh}
            dkd = {(d, h): _dot(t[d]['vv'][:, vs(h)], dsnb[d, h // 2]) for d, h in dh}
            dqt = {(d, h): _dot(da[d, h], t[d]['ktb'][:, ps(h)]) + _dot(t[d]['dd'][:, vs(h)], stb[d, h // 2]) for d, h in dh}
            dkt = {(d, h): _dot_tn(da[d, h], t[d]['qtb'][:, ps(h)]) for d, h in dh}
            dsq = {(d, h): _dot_tn(t[d]['dd'][:, vs(h)], t[d]['qtb'][:, ps(h)]) for d, h in dh}
            for d, (dqk_ref, dv_ref, dla_ref) in enumerate(((dqkf_ref, dvf_ref, dlaf_ref), (dqkb_ref, dvb_ref, dlab_ref))):
                td = t[d]
                rows = td['rows']
                for h in range(NH):
                    dv_ref[rows, vs(h)] = dv[d, h].astype(BF)
                pair = lambda x: jnp.concatenate([_pair_sel(x[d, 2 * p], x[d, 2 * p + 1]) for p in range(NP)], axis=1)
                dqt_, dkt_, dkd_ = pair(dqt), pair(dkt), pair(dkd)
                debl = jnp.concatenate([jnp.sum(td['st'][p] * td['dsn'][p], axis=0, keepdims=True) for p in range(NP)], axis=1)
                for p in range(NP):
                    ds_ref[d, p] = _pair_sel(dsq[d, 2 * p], dsq[d, 2 * p + 1]) + td['ebl'][:, p * PW:(p + 1) * PW] * td['dsn'][p]
                dkdkd = dkd_ * td['kd']
                dbl = jnp.sum(dkdkd, axis=0, keepdims=True) + debl * td['ebl']
                is_last = lax.broadcasted_iota(jnp.int32, (CH, DK), 0) == td['last']
                db = dqt_ * td['qt'] - dkt_ * td['kt'] - dkdkd + jnp.where(is_last, dbl, 0.0)
                dqk_ref[rows, :] = jnp.concatenate([dqt_ * td['eb'] * QSCALE, dkt_ * td['enb'] + dkd_ * td['ekd']], axis=1).astype(BF)
                dla_ref[rows, :] = _mask_dot(_gla_dir(1 - d)[0].astype(BF), db)
            return carry

        lax.fori_loop(0, nc, chunk, 0, unroll=2)

        @pl.when(pl.program_id(0) == nt - 1)
        def _():
            ds0_ref[...] = ds_ref[...]

    up = lambda w, col=0: pl.BlockSpec((tm, w), lambda i: (i, col))
    dn = lambda w, col=0: pl.BlockSpec((tm, w), lambda i: (nt - 1 - i, col))
    se_up = pl.BlockSpec((nc, NP, HV, PW), lambda i: (i, 0, 0, 0))
    se_dn = pl.BlockSpec((nc, NP, HV, PW), lambda i: (nt - 1 - i, 0, 0, 0))
    return _call(
        body, (nt,), "gla_bwd",
        [dn(2 * DK), dn(DV), dn(DK, 0), dn(DV), se_dn, up(2 * DK), up(DV), up(DK, 1), up(DV), se_up],
        [dn(2 * DK), dn(DV), dn(DK), up(2 * DK), up(DV), up(DK), _full((2, NP, HV, PW))],
        [jax.ShapeDtypeStruct((n, 2 * DK), BF), jax.ShapeDtypeStruct((n, DV), BF),
         jax.ShapeDtypeStruct((n, DK), F32), jax.ShapeDtypeStruct((n, 2 * DK), BF),
         jax.ShapeDtypeStruct((n, DV), BF), jax.ShapeDtypeStruct((n, DK), F32),
         jax.ShapeDtypeStruct((2, NP, HV, PW), F32)],
        [pltpu.VMEM((2, NP, HV, PW), F32)],
        (qk, v, la, do, se_f, qk, v, la, do, se_b), exchange, carried)


def _head_norm(o):
    ons, rss = [], []
    for h in range(NH):
        oh = o[:, h * HV:(h + 1) * HV]
        rs = lax.rsqrt(jnp.mean(oh * oh, axis=-1, keepdims=True) + EPS)
        ons.append(oh * rs)
        rss.append(rs)
    return ons, rss


def merge_fwd(x, o_f, o_b, g, co, vecm, gn, wout, tm):
    n = x.shape[0]

    def body(x_ref, of_ref, ob_ref, g_ref, co_ref, vec_ref, gn_ref, w_ref, x1_ref, y1_ref, cat_ref):
        o = of_ref[...].astype(F32) + ob_ref[...].astype(F32)
        ons, _ = _head_norm(o)
        gg = g_ref[...].astype(F32)
        sil = gg * _sigmoid(gg)
        cat_ref[:, :DC] = co_ref[...]
        for h in range(NH):
            vs = slice(h * HV, (h + 1) * HV)
            cat_ref[:, DC + h * HV:DC + (h + 1) * HV] = (ons[h] * gn_ref[:, vs] * sil[:, vs]).astype(BF)
        y1 = _dot(cat_ref[...], w_ref[...])
        y1_ref[...] = y1.astype(BF)
        x1_ref[...] = x_ref[...] + vec_ref[0:1, :] * y1

    tok = lambda w: pl.BlockSpec((tm, w), lambda i: (i, 0))
    return pl.pallas_call(
        body, grid=(n // tm,), name="merge_fwd",
        in_specs=[tok(D), tok(DV), tok(DV), tok(DV), tok(DC), _full(vecm.shape), _full(gn.shape), _full(wout.shape)],
        out_specs=[tok(D), tok(D), tok(D)],
        out_shape=[jax.ShapeDtypeStruct((n, D), F32), jax.ShapeDtypeStruct((n, D), BF), jax.ShapeDtypeStruct((n, D), BF)],
        compiler_params=_cparams(("arbitrary",)),
    )(x, o_f, o_b, g, co, vecm, gn, wout)


def merge_bwd(dx1, y1, o_f, o_b, g, vecm, gn, wout, tm):
    n = dx1.shape[0]

    def body(dx1_ref, y1_ref, of_ref, ob_ref, g_ref, vec_ref, gn_ref, w_ref,
             dy1_ref, dco_ref, do_ref, dg_ref, s1_ref, s2_ref):
        @pl.when(pl.program_id(0) == 0)
        def _():
            s1_ref[...] = jnp.zeros_like(s1_ref)
            s2_ref[...] = jnp.zeros_like(s2_ref)

        dx1 = dx1_ref[...]
        s1_ref[...] += _colsum8(dx1 * y1_ref[...].astype(F32))
        dy1 = (dx1 * vec_ref[0:1, :]).astype(BF)
        dy1_ref[...] = dy1
        dcat = _dot_nt(dy1, w_ref[...])
        dco_ref[...] = dcat[:, :DC].astype(BF)
        o = of_ref[...].astype(F32) + ob_ref[...].astype(F32)
        ons, rss = _head_norm(o)
        gg = g_ref[...].astype(F32)
        sg = _sigmoid(gg)
        sil = gg * sg
        dsil = sg * (1.0 + gg * (1.0 - sg))
        for h in range(NH):
            vs = slice(h * HV, (h + 1) * HV)
            do2 = dcat[:, DC + h * HV:DC + (h + 1) * HV]
            gnh = gn_ref[:, vs]
            t = do2 * sil[:, vs]
            s2_ref[:, vs] += _colsum8(t * ons[h])
            don = t * gnh
            do_ref[:, vs] = (rss[h] * (don - ons[h] * jnp.mean(don * ons[h], axis=-1, keepdims=True))).astype(BF)
            dg_ref[:, vs] = (do2 * ons[h] * gnh * dsil[:, vs]).astype(BF)

    tok = lambda w: pl.BlockSpec((tm, w), lambda i: (i, 0))
    return pl.pallas_call(
        body, grid=(n // tm,), name="merge_bwd",
        in_specs=[tok(D), tok(D), tok(DV), tok(DV), tok(DV), _full(vecm.shape), _full(gn.shape), _full(wout.shape)],
        out_specs=[tok(D), tok(DC), tok(DV), tok(DV), _full((8, D)), _full((8, DV))],
        out_shape=[jax.ShapeDtypeStruct((n, D), BF), jax.ShapeDtypeStruct((n, DC), BF), jax.ShapeDtypeStruct((n, DV), BF),
                   jax.ShapeDtypeStruct((n, DV), BF), jax.ShapeDtypeStruct((8, D), F32), jax.ShapeDtypeStruct((8, DV), F32)],
        compiler_params=_cparams(("arbitrary",)),
    )(dx1, y1, o_f, o_b, g, vecm, gn, wout)


def ffn_fwd_bwd(x1, tgt, vecf, wg, wu, wd, tm):
    n = x1.shape[0]

    def body(x1_ref, t_ref, vec_ref, wg_ref, wu_ref, wd_ref,
             dx1_ref, h2_ref, act_ref, dgt_ref, dup_ref, dy2_ref, s_ref):
        @pl.when(pl.program_id(0) == 0)
        def _():
            s_ref[...] = jnp.zeros_like(s_ref)

        n2g, sh2, sc2, g2, fg = (vec_ref[i:i + 1, :] for i in range(5))
        x1 = x1_ref[...]
        r2 = lax.rsqrt(jnp.mean(x1 * x1, axis=-1, keepdims=True) + EPS)
        xn2 = x1 * r2
        h2 = (xn2 * n2g * (1.0 + sc2) + sh2).astype(BF)
        h2_ref[...] = h2
        gt = _dot(h2, wg_ref[...])
        up = _dot(h2, wu_ref[...])
        sg = _sigmoid(gt)
        sil = gt * sg
        act = (sil * up).astype(BF)
        act_ref[...] = act
        y2 = _dot(act, wd_ref[...])
        x2 = x1 + g2 * y2
        r3 = lax.rsqrt(jnp.mean(x2 * x2, axis=-1, keepdims=True) + EPS)
        xn3 = x2 * r3
        e = xn3 * fg - t_ref[...]
        s_ref[40:48, :] += _colsum8(e * e) * (0.5 / D)
        dyo = e * (1.0 / D)
        s_ref[0:8, :] += _colsum8(dyo * xn3)
        dxn3 = dyo * fg
        dx2 = r3 * (dxn3 - xn3 * jnp.mean(dxn3 * xn3, axis=-1, keepdims=True))
        s_ref[8:16, :] += _colsum8(dx2 * y2)
        dy2 = (dx2 * g2).astype(BF)
        dy2_ref[...] = dy2
        dact = _dot_nt(dy2, wd_ref[...])
        dup = (dact * sil).astype(BF)
        dgt = (dact * up * (sg * (1.0 + gt * (1.0 - sg)))).astype(BF)
        dup_ref[...] = dup
        dgt_ref[...] = dgt
        dh2 = _dot_nt(dgt, wg_ref[...]) + _dot_nt(dup, wu_ref[...])
        s_ref[16:24, :] += _colsum8(dh2)
        t = dh2 * xn2
        s_ref[24:32, :] += _colsum8(t * n2g)
        s_ref[32:40, :] += _colsum8(t * (1.0 + sc2))
        dxn2 = dh2 * ((1.0 + sc2) * n2g)
        dx1_ref[...] = dx2 + r2 * (dxn2 - xn2 * jnp.mean(dxn2 * xn2, axis=-1, keepdims=True))

    tok = lambda w: pl.BlockSpec((tm, w), lambda i: (i, 0))
    wspec = lambda a: pl.BlockSpec(a.shape, lambda i: (0, 0), pipeline_mode=pl.Buffered(1))
    return pl.pallas_call(
        body, grid=(n // tm,), name="ffn_fwd_bwd",
        in_specs=[tok(D), tok(D), _full(vecf.shape), wspec(wg), wspec(wu), wspec(wd)],
        out_specs=[tok(D), tok(D), tok(DFF), tok(DFF), tok(DFF), tok(D), _full((48, D))],
        out_shape=[jax.ShapeDtypeStruct((n, D), F32), jax.ShapeDtypeStruct((n, D), BF), jax.ShapeDtypeStruct((n, DFF), BF),
                   jax.ShapeDtypeStruct((n, DFF), BF), jax.ShapeDtypeStruct((n, DFF), BF), jax.ShapeDtypeStruct((n, D), BF),
                   jax.ShapeDtypeStruct((48, D), F32)],
        compiler_params=_cparams(("arbitrary",)),
    )(x1, tgt, vecf, wg, wu, wd)


def wgrad(a, b, init, t1, t2, tn, name):
    n, k1 = a.shape
    k2 = b.shape[1]

    def body(a_ref, b_ref, *rest):
        o_ref = rest[-1]

        @pl.when(pl.program_id(2) == 0)
        def _():
            o_ref[...] = rest[0][...] if init is not None else jnp.zeros_like(o_ref)

        o_ref[...] += _dot_tn(a_ref[...], b_ref[...])

    ospec = pl.BlockSpec((t1, t2), lambda i, j, k: (i, j))
    extra = ([ospec], {2: 0}, (init,)) if init is not None else ([], {}, ())
    return pl.pallas_call(
        body, grid=(k1 // t1, k2 // t2, n // tn), name=name,
        in_specs=[pl.BlockSpec((tn, t1), lambda i, j, k: (k, i)), pl.BlockSpec((tn, t2), lambda i, j, k: (k, j))] + extra[0],
        out_specs=ospec, out_shape=jax.ShapeDtypeStruct((k1, k2), F32), input_output_aliases=extra[1],
        compiler_params=_cparams(("parallel", "parallel", "arbitrary")),
    )(a, b, *extra[2])


def bwd_in(x, dx1, ag, yb, dco, dqk_f, dqk_b, dv_f, dv_b, dg, dla_f, dla_b, la, r, vec1, win, convw, cvec, wa2, tm):
    n = x.shape[0]
    nseg = tm // GW
    cg = 128

    def body(x_ref, dx1_ref, ag_ref, yb_ref, dco_ref, dqkf_ref, dqkb_ref, dvf_ref, dvb_ref, dg_ref, dlaf_ref, dlab_ref,
             la_ref, r_ref, vec_ref, win_ref, cw_ref, cv_ref, wa2_ref,
             gx_ref, h_ref, dp_ref, dwa2_ref, dcw_ref, s_ref, vc_ref, pad2_ref, dvc_ref, dcw8_ref):
        first = pl.program_id(0) == 0

        @pl.when(first)
        def _():
            s_ref[...] = jnp.zeros_like(s_ref)
            dwa2_ref[...] = jnp.zeros_like(dwa2_ref)
            dcw8_ref[...] = jnp.zeros_like(dcw8_ref)

        yn, rs = _ln_stats(yb_ref[...])
        lng = cv_ref[1:2, :]
        ln = yn * lng + cv_ref[2:3, :]
        sgl = _sigmoid(ln)
        dln = dco_ref[...].astype(F32) * (sgl * (1.0 + ln * (1.0 - sgl)))
        dyn = dln * lng
        dyb = rs * (dyn - jnp.mean(dyn, axis=-1, keepdims=True) - yn * jnp.mean(dyn * yn, axis=-1, keepdims=True))
        s_ref[24:32, 0:DC] += _colsum8(dyb)
        s_ref[24:32, DC:D] += _colsum8(dln * yn)
        s_ref[32:40, 0:DC] += _colsum8(dln)

        agv = ag_ref[...].astype(F32)
        a = agv[:, :DC]
        sgg = _sigmoid(agv[:, DC:])
        vc_ref[...] = a * sgg
        _fill_padded(pad2_ref, dyb, nseg)

        def seg(s, carry):
            rows = pl.ds(pl.multiple_of(s * GW, GW), GW)
            for c0 in range(0, DC, cg):
                cs = pl.ds(c0, cg)
                vcs = vc_ref[rows, cs]
                acc = jnp.zeros((GW, cg), F32)
                for j, rows_j in _tap_slabs(pad2_ref, s, cs):
                    acc = acc + cw_ref[pl.ds(CW - 1 - j, 1), cs] * rows_j
                    dcw8_ref[CW - 1 - j, :, cs] += _colsum8(vcs * rows_j)
                dvc_ref[rows, cs] = acc
            return carry

        lax.fori_loop(0, nseg, seg, 0)
        dvc = dvc_ref[...]
        dp_ref[:, 0:DC] = (dvc * sgg).astype(BF)
        dp_ref[:, DC:2 * DC] = (dvc * a * sgg * (1.0 - sgg)).astype(BF)

        dp_ref[:, 2 * DC:2 * DC + 2 * DK] = (dqkf_ref[...].astype(F32) + dqkb_ref[...].astype(F32)).astype(BF)
        dp_ref[:, 2 * DC + 2 * DK:2 * DC + 2 * DK + DV] = (dvf_ref[...].astype(F32) + dvb_ref[...].astype(F32)).astype(BF)
        dp_ref[:, 2 * DC + 2 * DK + DV:2 * DC + 2 * DK + 2 * DV] = dg_ref[...]

        la = la_ref[...]
        dla = jnp.concatenate([dlaf_ref[...], dlab_ref[...]], axis=1)
        dpre = dla * (1.0 - jnp.exp(TAU * la)) * (1.0 / TAU)
        s_ref[32:40, DC:D] += _colsum8(dpre)
        dpreb = dpre.astype(BF)
        dwa2_ref[...] += _dot_tn(r_ref[...], dpreb)
        dp_ref[:, DINP - 128:] = _dot_nt(dpreb, wa2_ref[...]).astype(BF)

        dh = _dot_nt(dp_ref[...], win_ref[...])
        xx = x_ref[...]
        n1g, sh1, sc1 = vec_ref[0:1, :], vec_ref[1:2, :], vec_ref[2:3, :]
        rstd = lax.rsqrt(jnp.mean(xx * xx, axis=-1, keepdims=True) + EPS)
        xn = xx * rstd
        h_ref[...] = (xn * n1g * (1.0 + sc1) + sh1).astype(BF)
        s_ref[0:8, :] += _colsum8(dh)
        t = dh * xn
        s_ref[8:16, :] += _colsum8(t * n1g)
        s_ref[16:24, :] += _colsum8(t * (1.0 + sc1))
        dxn = dh * ((1.0 + sc1) * n1g)
        gx_ref[...] = dx1_ref[...] + rstd * (dxn - xn * jnp.mean(dxn * xn, axis=-1, keepdims=True))

        @pl.when(pl.program_id(0) == pl.num_programs(0) - 1)
        def _():
            dcw_ref[...] = jnp.sum(dcw8_ref[...], axis=1)

    tok = lambda w: pl.BlockSpec((tm, w), lambda i: (i, 0))
    return pl.pallas_call(
        body, grid=(n // tm,), name="bwd_in",
        in_specs=[tok(D), tok(D), tok(2 * DC), tok(DC), tok(DC), tok(2 * DK), tok(2 * DK), tok(DV), tok(DV), tok(DV),
                  tok(DK), tok(DK), tok(2 * DK), tok(128), _full(vec1.shape),
                  pl.BlockSpec(win.shape, lambda i: (0, 0), pipeline_mode=pl.Buffered(1)),
                  _full(convw.shape), _full(cvec.shape), _full(wa2.shape)],
        out_specs=[tok(D), tok(D), tok(DINP), _full((128, 2 * DK)), _full((32, DC)), _full((40, D))],
        out_shape=[jax.ShapeDtypeStruct((n, D), F32), jax.ShapeDtypeStruct((n, D), BF), jax.ShapeDtypeStruct((n, DINP), BF),
                   jax.ShapeDtypeStruct((128, 2 * DK), F32), jax.ShapeDtypeStruct((32, DC), F32),
                   jax.ShapeDtypeStruct((40, D), F32)],
        scratch_shapes=[pltpu.VMEM((tm, DC), F32), pltpu.VMEM((nseg, SEGP, DC), F32), pltpu.VMEM((tm, DC), F32),
                        pltpu.VMEM((32, 8, DC), F32)],
        compiler_params=_cparams(("arbitrary",)),
    )(x, dx1, ag, yb, dco, dqk_f, dqk_b, dv_f, dv_b, dg, dla_f, dla_b, la, r, vec1, win, convw, cvec, wa2)


def _ctx_common(ctx_ref, vec_ref, win_ref, wa2_ref, ba_ref):
    cx = ctx_ref[...]
    t = cx.shape[0]
    rstd = lax.rsqrt(jnp.mean(cx * cx, axis=-1, keepdims=True) + EPS)
    xn = cx * rstd
    hc = (xn * vec_ref[0:1, :] * (1.0 + vec_ref[2:3, :]) + vec_ref[1:2, :]).astype(BF)
    k0 = 2 * DC + DK
    kv = _dot(hc, win_ref[:, k0:k0 + DK + DV]).astype(BF).astype(F32)
    r = _dot(hc, win_ref[:, DINP - 128:]).astype(BF)
    la = _log_sigmoid(_dot(r, wa2_ref[...]) + ba_ref[...]) * (1.0 / TAU)
    incl = _tri(t, "le").astype(BF)
    strict = _tri(t, "lt").astype(BF)
    bf = _mask_dot(incl, la[:, :DK])
    wf = jnp.exp(bf[t - 1:t, :] - bf)
    wb = jnp.exp(_mask_dot(strict, la[:, DK:]))
    return xn, hc, kv[:, :DK], kv[:, DK:], r, la, wf, wb


def ctx_fwd(ctx, vecc, win, wa2, ba):
    def body(ctx_ref, vec_ref, win_ref, wa2_ref, ba_ref, s_ref):
        _, _, k, v, _, _, wf, wb = _ctx_common(ctx_ref, vec_ref, win_ref, wa2_ref, ba_ref)
        vb = v.astype(BF)
        for d, w in enumerate((wf, wb)):
            kd = (k * w).astype(BF)
            for h in range(NH):
                s_ref[d, h // 2, :, (h % 2) * HK:(h % 2 + 1) * HK] = _dot_tn(vb[:, h * HV:(h + 1) * HV], kd[:, h * HK:(h + 1) * HK])

    return pl.pallas_call(
        body, name="ctx_fwd", out_shape=jax.ShapeDtypeStruct((2, NP, HV, PW), F32),
        compiler_params=pltpu.CompilerParams(vmem_limit_bytes=VMEM_LIMIT),
    )(ctx, vecc, win, wa2, ba)


def ctx_bwd(ctx, vecc, win, wa2, ba, ds0):
    t = ctx.shape[0]

    def body(ctx_ref, vec_ref, win_ref, wa2_ref, ba_ref, ds_ref, dwin_ref, dwa2_ref, s_ref, dpc_ref):
        xn, hc, k, v, r, la, wf, wb = _ctx_common(ctx_ref, vec_ref, win_ref, wa2_ref, ba_ref)
        vb = v.astype(BF)
        strict = _tri(t, "lt").astype(BF)
        strict_t = _tri(t, "gt").astype(BF)
        dpc_ref[...] = jnp.zeros_like(dpc_ref)
        k0 = 2 * DC + DK
        dk = jnp.zeros((t, DK), F32)
        des = []
        for d, w in enumerate((wf, wb)):
            kd = (k * w).astype(BF)
            dkds = []
            for h in range(NH):
                dsb = ds_ref[d, h // 2, :, (h % 2) * HK:(h % 2 + 1) * HK].astype(BF)
                dkds.append(_dot(vb[:, h * HV:(h + 1) * HV], dsb))
                dvh = _dot_nt(kd[:, h * HK:(h + 1) * HK], dsb)
                vs = slice(k0 + DK + h * HV, k0 + DK + (h + 1) * HV)
                if d == 0:
                    dpc_ref[:, vs] = dvh.astype(BF)
                else:
                    dpc_ref[:, vs] = (dpc_ref[:, vs].astype(F32) + dvh).astype(BF)
            dkd = jnp.concatenate(dkds, axis=1)
            dk = dk + dkd * w
            des.append(dkd * k * w)
        dpc_ref[:, k0:k0 + DK] = dk.astype(BF)
        dla = jnp.concatenate([_mask_dot(strict, des[0]), _mask_dot(strict_t, des[1])], axis=1)
        dpre = dla * (1.0 - jnp.exp(TAU * la)) * (1.0 / TAU)
        dpreb = dpre.astype(BF)
        dwa2_ref[...] = _dot_tn(r, dpreb)
        dpc_ref[:, DINP - 128:] = _dot_nt(dpreb, wa2_ref[...]).astype(BF)
        dpc = dpc_ref[...]
        dwin_ref[...] = _dot_tn(hc, dpc)
        dhc = _dot_nt(dpc, win_ref[...])
        n1g, sc1 = vec_ref[0:1, :], vec_ref[2:3, :]
        tt = dhc * xn
        s_ref[...] = jnp.zeros_like(s_ref)
        s_ref[0:1, :] = jnp.sum(tt * (1.0 + sc1), axis=0, keepdims=True)
        s_ref[1:2, :] = jnp.sum(dhc, axis=0, keepdims=True)
        s_ref[2:3, :] = jnp.sum(tt * n1g, axis=0, keepdims=True)
        s_ref[3:4, DC:D] = jnp.sum(dpre, axis=0, keepdims=True)

    return pl.pallas_call(
        body, name="ctx_bwd",
        out_shape=[jax.ShapeDtypeStruct((D, DINP), F32), jax.ShapeDtypeStruct((128, 2 * DK), F32),
                   jax.ShapeDtypeStruct((8, D), F32)],
        scratch_shapes=[pltpu.VMEM((t, DINP), BF)],
        compiler_params=pltpu.CompilerParams(vmem_limit_bytes=VMEM_LIMIT),
    )(ctx, vecc, win, wa2, ba, ds0)


def _silu(x):
    return x * _sigmoid(x)


def mod_fwd(cext, wm, bm):
    def body(c_ref, w_ref, b_ref, o_ref):
        o_ref[...] = _dot(_silu(c_ref[...]).astype(BF), w_ref[...].astype(BF)) + b_ref[...]

    return pl.pallas_call(body, name="mod_fwd", out_shape=jax.ShapeDtypeStruct((cext.shape[0], wm.shape[1]), F32),
                          compiler_params=pltpu.CompilerParams(vmem_limit_bytes=VMEM_LIMIT))(cext, wm, bm)


def mod_bwd(cext, dm, wm):
    def body(c_ref, d_ref, w_ref, gw_ref, ds_ref):
        dmb = d_ref[...].astype(BF)
        gw_ref[...] = _dot_tn(_silu(c_ref[...]).astype(BF), dmb)
        ds_ref[...] = _dot_nt(dmb, w_ref[...].astype(BF))

    return pl.pallas_call(body, name="mod_bwd",
                          out_shape=[jax.ShapeDtypeStruct(wm.shape, F32), jax.ShapeDtypeStruct(cext.shape, F32)],
                          compiler_params=pltpu.CompilerParams(vmem_limit_bytes=VMEM_LIMIT))(cext, dm, wm)


def pack_small(sf, s1, s2, sd, sc, dcw, dwa2, dwa2_c):
    def body(sf_ref, s1_ref, s2_ref, sd_ref, sc_ref, dcw_ref, dwa2_ref, dwa2c_ref, o_ref, ocw_ref, owa_ref):
        rsum = lambda ref, i: jnp.sum(ref[8 * i:8 * i + 8, :], axis=0, keepdims=True)
        o_ref[...] = jnp.zeros_like(o_ref)
        o_ref[0:1, :] = rsum(sd_ref, 0)
        o_ref[1:2, :] = rsum(sd_ref, 1)
        o_ref[2:3, :] = rsum(s1_ref, 0)
        o_ref[3:4, :] = rsum(sf_ref, 2)
        o_ref[4:5, :] = rsum(sf_ref, 3)
        o_ref[5:6, :] = rsum(sf_ref, 1)
        o_ref[6:7, :] = sc_ref[1:2, :]
        o_ref[7:8, :] = sc_ref[2:3, :]
        o_ref[8:9, :] = rsum(sd_ref, 2) + sc_ref[0:1, :]
        o_ref[9:10, :] = rsum(sf_ref, 4)
        o_ref[10:11, :] = rsum(sf_ref, 0)
        o_ref[11:12, :] = rsum(sd_ref, 3)
        o_ref[12:13, :] = rsum(sd_ref, 4) + sc_ref[3:4, :]
        g = jnp.sum(s2_ref[...], axis=0, keepdims=True)
        o_ref[13:14, 0:HV] = g[:, 0:HV] + g[:, HV:2 * HV] + g[:, 2 * HV:3 * HV] + g[:, 3 * HV:4 * HV]
        o_ref[14:15, :] = rsum(sf_ref, 5)
        ocw_ref[...] = dcw_ref[...]
        owa_ref[...] = dwa2_ref[0:32, :] + dwa2c_ref[0:32, :]

    return pl.pallas_call(body, name="pack_small",
                          out_shape=[jax.ShapeDtypeStruct((16, D), F32), jax.ShapeDtypeStruct((32, DC), F32),
                                     jax.ShapeDtypeStruct((32, 2 * DK), F32)])(sf, s1, s2, sd, sc, dcw, dwa2, dwa2_c)


def small_totals(g8):
    r = g8.shape[1]

    def body(g_ref, t_ref, bm_ref, loss_ref):
        acc = g_ref[0]
        for i in range(1, NDEV):
            acc = acc + g_ref[i]
        t_ref[...] = acc
        bm_ref[...] = jnp.zeros_like(bm_ref)
        bm_ref[0:6, :] = acc[0:6, :]
        bm_ref[0:2, :] += acc[6:8, :]
        loss_ref[...] = jnp.broadcast_to(jnp.sum(acc[14:15, :], axis=1, keepdims=True), loss_ref.shape)

    return pl.pallas_call(body, name="small_totals",
                          out_shape=[jax.ShapeDtypeStruct((r, D), F32), jax.ShapeDtypeStruct((8, D), F32),
                                     jax.ShapeDtypeStruct((8, 128), F32)])(g8)


def cctx_grad(p8, c_ctx_row):
    def body(p_ref, c_ref, o_ref):
        acc = p_ref[0, 0:1, :]
        for j in range(1, NCHIP):
            acc = acc + p_ref[2 * j, 0:1, :]
        cc = c_ref[0:1, :]
        sg = _sigmoid(cc)
        o_ref[...] = jnp.zeros_like(o_ref)
        o_ref[0:1, :] = acc * (sg * (1.0 + cc * (1.0 - sg)))

    return pl.pallas_call(body, name="cctx_grad", out_shape=jax.ShapeDtypeStruct((8, D), F32))(p8, c_ctx_row)


def adamw(w, g, m, v, rows, name):
    r, c = w.shape

    def body(w_ref, g_ref, m_ref, v_ref, d_ref, nm_ref, nv_ref):
        gg = g_ref[...]
        nm = ADAM_B1 * m_ref[...] + (1.0 - ADAM_B1) * gg
        nv = ADAM_B2 * v_ref[...] + (1.0 - ADAM_B2) * (gg * gg)
        m_hat = nm / (1.0 - ADAM_B1 ** ADAM_STEP)
        v_hat = nv / (1.0 - ADAM_B2 ** ADAM_STEP)
        d_ref[...] = -ADAM_LR * (m_hat / (jnp.sqrt(v_hat) + ADAM_EPS) + ADAM_WD * w_ref[...])
        nm_ref[...] = nm
        nv_ref[...] = nv

    spec = pl.BlockSpec((rows, c), lambda i: (i, 0))
    sds = jax.ShapeDtypeStruct((r, c), F32)
    return pl.pallas_call(
        body, grid=(r // rows,), name=name, in_specs=[spec] * 4, out_specs=[spec] * 3, out_shape=[sds] * 3,
        compiler_params=_cparams(("parallel",)),
    )(w, g, m, v)


def _me():
    return lax.axis_index("x"), lax.axis_index("y"), lax.axis_index("c")


def _flip(v, bit):
    return 1 - v if bit else v


ANY = pl.BlockSpec(memory_space=pl.ANY)


def all_gather8(x, name):
    r, c = x.shape

    def body(x_ref, o_ref, ssem, rsem, lsem):
        mx, my, mc = _me()
        me = 4 * mx + 2 * my + mc
        local = pltpu.make_async_copy(x_ref, o_ref.at[me], lsem)
        local.start()

        def copy(k):
            px, py, pc = _flip(mx, k & 4), _flip(my, k & 2), _flip(mc, k & 1)
            return px, py, pc

        sends = []
        for k in range(1, NDEV):
            cp = pltpu.make_async_remote_copy(src_ref=x_ref, dst_ref=o_ref.at[me], send_sem=ssem.at[k - 1],
                                              recv_sem=rsem.at[k - 1], device_id=copy(k), device_id_type=MESH)
            cp.start()
            sends.append(cp)
        for k in range(1, NDEV):
            px, py, pc = copy(k)
            pltpu.make_async_remote_copy(src_ref=x_ref, dst_ref=o_ref.at[4 * px + 2 * py + pc], send_sem=ssem.at[k - 1],
                                         recv_sem=rsem.at[k - 1], device_id=(px, py, pc), device_id_type=MESH).wait_recv()
        for cp in sends:
            cp.wait_send()
        local.wait()

    vm = pl.BlockSpec(memory_space=pltpu.VMEM)
    return pl.pallas_call(
        body, name=name, in_specs=[vm], out_specs=vm, out_shape=jax.ShapeDtypeStruct((NDEV, r, c), x.dtype),
        scratch_shapes=[pltpu.SemaphoreType.DMA((NDEV - 1,)), pltpu.SemaphoreType.DMA((NDEV - 1,)), pltpu.SemaphoreType.DMA],
    )(x)


def _chip_peers(mx, my):
    out = []
    for p in range(1, NCHIP):
        px, py = _flip(mx, p & 2), _flip(my, p & 1)
        out.append((px, py, 2 * px + py))
    return out


class ChipExchange:
    def __init__(self, kind, arrays):
        self.kind = kind
        self.n = len(arrays)
        if kind == "gather":
            self.out_shape = [jax.ShapeDtypeStruct((NCHIP,) + a.shape, a.dtype) for a in arrays]
        else:
            self.out_shape = [jax.ShapeDtypeStruct(a.shape, a.dtype) for a in arrays]
        self.scratch = [pltpu.SemaphoreType.DMA((3 * self.n,)), pltpu.SemaphoreType.DMA((3 * self.n,)),
                        pltpu.SemaphoreType.DMA((self.n,))]

    def _copies(self, ins, outs, sems):
        ssem, rsem, lsem = sems
        mx, my, mc = _me()
        jme = 2 * mx + my
        gather = self.kind == "gather"
        local, sends, waits = [], [], []
        for k in range(self.n):
            local.append(pltpu.make_async_copy(ins[k] if gather else ins[k].at[jme], outs[k].at[jme], lsem.at[k]))
            for p, (px, py, jp) in enumerate(_chip_peers(mx, my)):
                src = ins[k] if gather else ins[k].at[jp]
                sem = dict(send_sem=ssem.at[3 * k + p], recv_sem=rsem.at[3 * k + p], device_id=(px, py, mc),
                           device_id_type=MESH)
                sends.append(pltpu.make_async_remote_copy(src_ref=src, dst_ref=outs[k].at[jme], **sem))
                waits.append(pltpu.make_async_remote_copy(src_ref=src, dst_ref=outs[k].at[jp], **sem))
        return local, sends, waits

    def start(self, ins, outs, sems):
        local, sends, _ = self._copies(ins, outs, sems)
        for cp in local + sends:
            cp.start()

    def wait(self, ins, outs, sems):
        local, _, waits = self._copies(ins, outs, sems)
        for cp in waits:
            cp.wait_recv()
        for cp in waits:
            cp.wait_send()
        for cp in local:
            cp.wait()


def chip_exchange(kind, arrays, name):
    ex = ChipExchange(kind, arrays)
    n = ex.n

    def body(*refs):
        ins, outs, sems = refs[:n], refs[n:2 * n], refs[2 * n:]
        ex.start(ins, outs, sems)
        ex.wait(ins, outs, sems)

    return pl.pallas_call(body, name=name, in_specs=[ANY] * n, out_specs=[ANY] * n, out_shape=ex.out_shape,
                          scratch_shapes=ex.scratch)(*arrays)


def sibling_add(g, ngrp, hr, tr, name):
    c_ = g.shape[1]
    nt = hr // tr

    def body(cidx, keep_ref, give_ref, o_ref, land, ssem, rsem):
        mx, my, mc = _me()
        t = pl.program_id(0) * nt + pl.program_id(1)
        s = t % 2
        cp = pltpu.make_async_remote_copy(src_ref=give_ref, dst_ref=land.at[s], send_sem=ssem.at[s], recv_sem=rsem.at[s],
                                          device_id=(mx, my, 1 - mc), device_id_type=MESH)
        cp.start()
        cp.wait_recv()
        o_ref[...] = keep_ref[...] + land[s]
        cp.wait_send()

    grid_spec = pltpu.PrefetchScalarGridSpec(
        num_scalar_prefetch=1, grid=(ngrp, nt),
        in_specs=[pl.BlockSpec((tr, c_), lambda i, j, cr: ((2 * i + cr[0]) * nt + j, 0)),
                  pl.BlockSpec((tr, c_), lambda i, j, cr: ((2 * i + 1 - cr[0]) * nt + j, 0))],
        out_specs=pl.BlockSpec((tr, c_), lambda i, j, cr: (i * nt + j, 0)),
        scratch_shapes=[pltpu.VMEM((2, tr, c_), F32), pltpu.SemaphoreType.DMA((2,)), pltpu.SemaphoreType.DMA((2,))])
    cidx = lax.axis_index("c").astype(jnp.int32).reshape(1)
    return pl.pallas_call(body, grid_spec=grid_spec, name=name, out_shape=jax.ShapeDtypeStruct((ngrp * hr, c_), F32),
                          compiler_params=_cparams(("arbitrary", "arbitrary")))(cidx, g, g)


def finish_weight(b, w, m, v, tr, name):
    _, r2, c_ = b.shape

    def body(b_ref, w_ref, m_ref, v_ref, g_ref, d_ref, nm_ref, nv_ref, mine, land, ssem, rsem):
        mx, my, mc = _me()
        t = pl.program_id(0)
        s = t % 2
        mine[s] = (b_ref[0].astype(F32) + b_ref[1].astype(F32)) + (b_ref[2].astype(F32) + b_ref[3].astype(F32))
        cp = pltpu.make_async_remote_copy(src_ref=mine.at[s], dst_ref=land.at[s], send_sem=ssem.at[s], recv_sem=rsem.at[s],
                                          device_id=(mx, my, 1 - mc), device_id_type=MESH)
        cp.start()
        cp.wait_recv()
        g_ref[mc] = mine[s]
        g_ref[1 - mc] = land[s]
        cp.wait_send()
        gg = g_ref[...]
        nm = ADAM_B1 * m_ref[...] + (1.0 - ADAM_B1) * gg
        nv = ADAM_B2 * v_ref[...] + (1.0 - ADAM_B2) * (gg * gg)
        m_hat = nm / (1.0 - ADAM_B1 ** ADAM_STEP)
        v_hat = nv / (1.0 - ADAM_B2 ** ADAM_STEP)
        d_ref[...] = -ADAM_LR * (m_hat / (jnp.sqrt(v_hat) + ADAM_EPS) + ADAM_WD * w_ref[...])
        nm_ref[...] = nm
        nv_ref[...] = nv

    spec = pl.BlockSpec((2, tr, c_), lambda i: (0, i, 0))
    sds = jax.ShapeDtypeStruct((2, r2, c_), F32)
    return pl.pallas_call(
        body, grid=(r2 // tr,), name=name,
        in_specs=[pl.BlockSpec((NCHIP, tr, c_), lambda i: (0, i, 0)), spec, spec, spec],
        out_specs=[spec] * 4, out_shape=[sds] * 4,
        scratch_shapes=[pltpu.VMEM((2, tr, c_), F32), pltpu.VMEM((2, tr, c_), F32), pltpu.SemaphoreType.DMA((2,)),
                        pltpu.SemaphoreType.DMA((2,))],
        compiler_params=_cparams(("arbitrary",)))(b, w, m, v)


TM_IN = 256
TM_GLA = 512
TM_MERGE = 512
TM_FFN = 256
TN_WGRAD = 2048

WEIGHTS = ['c_ctx', 'w_mod', 'b_mod', 'norm1_g', 'norm2_g', 'w_in', 'conv_w', 'conv_b', 'conv_ln_g', 'conv_ln_b', 'w_a2_f',
           'b_a_f', 'w_a2_b', 'b_a_b', 'gla_norm_g', 'w_out', 'w_gate', 'w_up', 'w_down', 'final_g']
BIG = ['w_in', 'w_out', 'w_gate', 'w_up', 'w_down']


def _rows(*vs):
    out = jnp.zeros((8, vs[0].shape[-1]), F32)
    for i, v in enumerate(vs):
        out = out.at[i].set(v.reshape(-1))
    return out


def _small_slab(p):
    cat = lambda *ks: jnp.concatenate([p[k].reshape(-1) for k in ks])
    vecs = _rows(p['c_ctx'], p['norm1_g'], p['norm2_g'], p['final_g'], cat('conv_b', 'conv_ln_g'),
                 cat('conv_ln_b', 'b_a_f', 'b_a_b'), jnp.pad(p['gla_norm_g'].reshape(-1), (0, D - HV)))
    bmod = jnp.pad(p['b_mod'].reshape(6, D), ((0, 2), (0, 0)))
    shards = jnp.pad(jnp.concatenate([jnp.pad(p['conv_w'].reshape(-1), (0, DC // NCHIP)), cat('w_a2_f', 'w_a2_b')]),
                     (0, 2 * D)).reshape(8, D)
    return jnp.concatenate([vecs, bmod, shards], axis=0)


def _unslab(s):
    return {
        'c_ctx': s[0], 'norm1_g': s[1:2], 'norm2_g': s[2:3], 'final_g': s[3],
        'conv_b': s[4:5, :DC], 'conv_ln_g': s[4:5, DC:], 'conv_ln_b': s[5:6, :DC],
        'b_a_f': s[5:6, DC:DC + DK], 'b_a_b': s[5:6, DC + DK:], 'gla_norm_g': s[6:7, :HV],
        'b_mod': s[8:14].reshape(1, 6 * D),
        'conv_w': s[16:20].reshape(32, DC // NCHIP)[:CW].reshape(1, CW, DC // NCHIP),
        'w_a2_f': s[20].reshape(1, RANK, DK // NCHIP), 'w_a2_b': s[21].reshape(1, RANK, DK // NCHIP),
    }


def kernel(x, c, ctx, c_ctx, w_mod, b_mod, norm1_g, norm2_g, w_in, conv_w, conv_b, conv_ln_g, conv_ln_b, w_a2_f, b_a_f, w_a2_b, b_a_b, gla_norm_g, w_out, w_gate, w_up, w_down, final_g, loss_target, m_c_ctx, m_w_mod, m_b_mod, m_norm1_g, m_norm2_g, m_w_in, m_conv_w, m_conv_b, m_conv_ln_g, m_conv_ln_b, m_w_a2_f, m_b_a_f, m_w_a2_b, m_b_a_b, m_gla_norm_g, m_w_out, m_w_gate, m_w_up, m_w_down, m_final_g, v_c_ctx, v_w_mod, v_b_mod, v_norm1_g, v_norm2_g, v_w_in, v_conv_w, v_conv_b, v_conv_ln_g, v_conv_ln_b, v_w_a2_f, v_b_a_f, v_w_a2_b, v_b_a_b, v_gla_norm_g, v_w_out, v_w_gate, v_w_up, v_w_down, v_final_g):
    w = dict(c_ctx=c_ctx, w_mod=w_mod, b_mod=b_mod, norm1_g=norm1_g, norm2_g=norm2_g, w_in=w_in, conv_w=conv_w, conv_b=conv_b,
             conv_ln_g=conv_ln_g, conv_ln_b=conv_ln_b, w_a2_f=w_a2_f, b_a_f=b_a_f, w_a2_b=w_a2_b, b_a_b=b_a_b,
             gla_norm_g=gla_norm_g, w_out=w_out, w_gate=w_gate, w_up=w_up, w_down=w_down, final_g=final_g)
    m = dict(c_ctx=m_c_ctx, w_mod=m_w_mod, b_mod=m_b_mod, norm1_g=m_norm1_g, norm2_g=m_norm2_g, w_in=m_w_in, conv_w=m_conv_w,
             conv_b=m_conv_b, conv_ln_g=m_conv_ln_g, conv_ln_b=m_conv_ln_b, w_a2_f=m_w_a2_f, b_a_f=m_b_a_f, w_a2_b=m_w_a2_b,
             b_a_b=m_b_a_b, gla_norm_g=m_gla_norm_g, w_out=m_w_out, w_gate=m_w_gate, w_up=m_w_up, w_down=m_w_down,
             final_g=m_final_g)
    v = dict(c_ctx=v_c_ctx, w_mod=v_w_mod, b_mod=v_b_mod, norm1_g=v_norm1_g, norm2_g=v_norm2_g, w_in=v_w_in, conv_w=v_conv_w,
             conv_b=v_conv_b, conv_ln_g=v_conv_ln_g, conv_ln_b=v_conv_ln_b, w_a2_f=v_w_a2_f, b_a_f=v_b_a_f, w_a2_b=v_w_a2_b,
             b_a_b=v_b_a_b, gla_norm_g=v_gla_norm_g, w_out=v_w_out, w_gate=v_w_gate, w_up=v_w_up, w_down=v_w_down,
             final_g=v_final_g)
    mx, my, mc = _me()
    jme = 2 * mx + my
    me = 4 * mx + 2 * my + mc
    wmc = D * 6 // NCHIP
    xx, tgt, cx = x[0], loss_target[0], ctx[0]
    n = xx.shape[0]

    sw = jnp.concatenate([jnp.pad(conv_w[0], ((0, 1), (0, 0))).reshape(-1), w_a2_f[0].reshape(-1), w_a2_b[0].reshape(-1)])
    cs8 = all_gather8(jnp.concatenate([_rows(c[0]), jnp.pad(sw.reshape(6, D), ((0, 2), (0, 0)))], axis=0), "gather_c_small_w")
    c8 = cs8[:, 0, :]
    swc = jnp.stack([cs8[2 * j, 8:16] for j in range(NCHIP)]).reshape(NCHIP, 8 * D)
    convw = jnp.transpose(swc[:, :32 * 128].reshape(NCHIP, 32, 128), (1, 0, 2)).reshape(32, DC)
    a2 = lambda o: jnp.transpose(swc[:, o:o + RANK * 64].reshape(NCHIP, RANK, 64), (1, 0, 2)).reshape(RANK, DK)
    wa2 = jnp.zeros((128, 2 * DK), F32).at[0:RANK, 0:DK].set(a2(32 * 128)).at[RANK:2 * RANK, DK:].set(a2(32 * 128 + RANK * 64))
    wa2 = wa2.astype(BF)

    cext = jnp.concatenate([c8, _rows(c_ctx)], axis=0)
    mloc = mod_fwd(cext, w_mod[0], lax.dynamic_slice_in_dim(b_mod, jme * wmc, wmc, axis=1))
    mall = all_gather8(mloc, "gather_mod")
    mall = jnp.concatenate([mall[2 * j] for j in range(NCHIP)], axis=1)
    sh1, sc1, g1, sh2, sc2, g2 = jnp.split(lax.dynamic_slice_in_dim(mall, me, 1, axis=0)[0], 6)
    csh1, csc1 = mall[8, :D], mall[8, D:2 * D]

    bshard = [w[k][0].astype(BF) for k in BIG]
    cols = lambda a: jnp.transpose(a, (1, 0, 2)).reshape(a.shape[1], -1)
    win = jnp.pad(cols(chip_exchange("gather", bshard[:1], "gather_w_in")[0]), ((0, 0), (0, DINP - DIN)))
    ba = jnp.concatenate([b_a_f, b_a_b], axis=1)
    cvec = _rows(conv_b, conv_ln_g, conv_ln_b)
    vec1 = _rows(norm1_g, sh1, sc1)
    vecc = _rows(norm1_g, csh1, csc1)
    vecm = _rows(g1)
    vecf = _rows(norm2_g, sh2, sc2, g2, final_g)
    gn = jnp.tile(gla_norm_g, (1, NH))

    s0 = ctx_fwd(cx, vecc, win, wa2, ba)
    res = fwd_in(xx, vec1, win, convw, cvec, wa2, ba, TM_IN, ChipExchange("gather", bshard[1:]), bshard[1:])
    ag, yb, co, qk, vv, gg, la, r = res[:8]
    wout = res[8].reshape(D, D)
    wg, wu = cols(res[9]), cols(res[10])
    wd = res[11].reshape(DFF, D)
    o_f, o_b, se_f, se_b = gla_fwd(qk, vv, la, s0, TM_GLA)
    x1, y1, cat = merge_fwd(xx, o_f, o_b, gg, co, vecm, gn, wout, TM_MERGE)

    dx1, h2, act, dgt, dup, dy2, sf = ffn_fwd_bwd(x1, tgt, vecf, wg, wu, wd, TM_FFN)
    d_wg = wgrad(h2, dgt, None, D, DFF // 2, TN_WGRAD, "wgrad_gate")
    d_wu = wgrad(h2, dup, None, D, DFF // 2, TN_WGRAD, "wgrad_up")
    d_wd = wgrad(act, dy2, None, DFF // 2, D, TN_WGRAD, "wgrad_down")
    dy1, dco, do, dg, s1, s2 = merge_bwd(dx1, y1, o_f, o_b, gg, vecm, gn, wout, TM_MERGE)
    d_wout = wgrad(cat, dy1, None, D, D, TN_WGRAD, "wgrad_out")

    shard = lambda a, k: jnp.transpose(a.reshape(a.shape[0], NCHIP, k), (1, 0, 2))
    hd = D // 2
    parts = [sibling_add(d_wout, NCHIP, hd // NCHIP, hd // NCHIP, "xadd_w_out").reshape(NCHIP, hd // NCHIP, D),
             shard(sibling_add(d_wg, 1, hd, 128, "xadd_w_gate"), DFF // NCHIP),
             shard(sibling_add(d_wu, 1, hd, 128, "xadd_w_up"), DFF // NCHIP),
             sibling_add(d_wd, NCHIP, DFF // 8, DFF // 16, "xadd_w_down").reshape(NCHIP, DFF // 8, D)]
    parts = [p.astype(BF) for p in parts]
    res = gla_bwd(qk, vv, la, do, se_f, se_b, TM_GLA, ChipExchange("scatter", parts), parts)
    dqk_f, dv_f, dla_f, dqk_b, dv_b, dla_b, ds0 = res[:7]
    recv = list(res[7:])
    dwin_c, dwa2_c, sc = ctx_bwd(cx, vecc, win, wa2, ba, ds0)
    grad_x, h, dp, dwa2, dcw, sd = bwd_in(xx, dx1, ag, yb, dco, dqk_f, dqk_b, dv_f, dv_b, dg, dla_f, dla_b, la, r,
                                          vec1, win, convw, cvec, wa2, TM_IN)
    d_win = wgrad(h, dp, dwin_c, D, DINP // 3, TN_WGRAD, "wgrad_in")
    part_in = shard(sibling_add(d_win, 1, hd, 128, "xadd_w_in")[:, :DIN], DIN // NCHIP).astype(BF)
    recv = list(chip_exchange("scatter", [part_in], "scatter_w_in")) + recv

    rows16, dcw_t, dwa2_t = pack_small(sf, s1, s2, sd, sc, dcw, dwa2, dwa2_c)
    sp = jnp.concatenate([rows16, dcw_t.reshape(16, D), dwa2_t.reshape(16, D)], axis=0)
    g8 = all_gather8(sp, "gather_small_grads")
    tot, bm_g, loss8 = small_totals(g8)
    loss = loss8[0, 0]
    dmod8 = g8[:, 0:6, :].reshape(NDEV, 6 * D)
    dmodc = jnp.concatenate([tot[6], tot[7], jnp.zeros((4 * D,), F32)])
    dm = jnp.concatenate([dmod8, _rows(dmodc)], axis=0)
    dm = lax.dynamic_slice_in_dim(dm, jme * wmc, wmc, axis=1)
    g_wmod, dsil = mod_bwd(cext, dm, w_mod[0])
    p8 = all_gather8(dsil[8:16], "gather_dsilu")
    g_cctx = cctx_grad(p8, _rows(c_ctx))[0]

    grads, delta, new_m, new_v = {}, {}, {}, {}
    for i, k in enumerate(BIG):
        r2, cc = recv[i].shape[1:]
        halves = lambda a: a[0].reshape(2, r2, cc)
        outs = finish_weight(recv[i], halves(w[k]), halves(m[k]), halves(v[k]), 176 if r2 % 128 else 128, "finish_" + k)
        grads[k], delta[k], new_m[k], new_v[k] = (o.reshape(w[k].shape) for o in outs)
    grads['w_mod'] = g_wmod[None]
    d_, m_, v_ = adamw(w_mod[0], g_wmod, m_w_mod[0], v_w_mod[0], 128, "adamw_w_mod")
    delta['w_mod'], new_m['w_mod'], new_v['w_mod'] = d_[None], m_[None], v_[None]
    small_g = {
        'c_ctx': g_cctx, 'b_mod': bm_g[0:6].reshape(1, 6 * D), 'norm1_g': tot[8:9], 'norm2_g': tot[9:10], 'final_g': tot[10],
        'conv_b': tot[11:12, :DC], 'conv_ln_g': tot[11:12, DC:], 'conv_ln_b': tot[12:13, :DC],
        'b_a_f': tot[12:13, DC:DC + DK], 'b_a_b': tot[12:13, DC + DK:], 'gla_norm_g': tot[13:14, :HV],
        'conv_w': lax.dynamic_slice_in_dim(tot[16:32].reshape(32, DC)[:CW], jme * (DC // NCHIP), DC // NCHIP, axis=1)[None],
        'w_a2_f': lax.dynamic_slice_in_dim(tot[32:48].reshape(32, 2 * DK)[0:RANK, 0:DK], jme * (DK // NCHIP), DK // NCHIP, axis=1)[None],
        'w_a2_b': lax.dynamic_slice_in_dim(tot[32:48].reshape(32, 2 * DK)[RANK:2 * RANK, DK:], jme * (DK // NCHIP), DK // NCHIP, axis=1)[None],
    }
    grads.update(small_g)
    sd_, sm_, sv_ = adamw(_small_slab(w), _small_slab(small_g), _small_slab(m), _small_slab(v), 24,
                          "adamw_small")
    for dst, slab in ((delta, sd_), (new_m, sm_), (new_v, sv_)):
        dst.update(_unslab(slab))
    out = [loss, grad_x[None]]
    for group in (grads, delta, new_m, new_v):
        out += [group[k].reshape(w[k].shape) for k in WEIGHTS]
    return tuple(out)
```

```python
import functools

import jax
import jax.numpy as jnp
from jax import lax
from jax.experimental import pallas as pl
from jax.experimental.pallas import tpu as pltpu

F32 = jnp.float32
BF = jnp.bfloat16

D = 1024
DC = 512
NH = 4
HK = 64
HV = 128
DK = NH * HK
DV = NH * HV
RANK = 16
CH = 64
GW = 64
CW = 31
CPAD = CW // 2
SEGP = GW + 32
DFF = 2816
DIN = 2592
DINP = 2688
EPS = 1e-6
TAU = 16.0
QSCALE = HK ** -0.5
NCHIP = 4
NDEV = 8

ADAM_LR = 0.001
ADAM_B1 = 0.9
ADAM_B2 = 0.999
ADAM_EPS = 1e-08
ADAM_WD = 0.01
ADAM_STEP = 10

VMEM_LIMIT = 56 * 1024 * 1024
MESH = pl.DeviceIdType.MESH


def _dot(a, b):
    return jnp.dot(a, b, preferred_element_type=F32)


def _dot_nt(a, b):
    return lax.dot_general(a, b, (((1,), (1,)), ((), ())), preferred_element_type=F32)


def _dot_tn(a, b):
    return lax.dot_general(a, b, (((0,), (0,)), ((), ())), preferred_element_type=F32)


def _split3(x):
    hi = x.astype(BF)
    r1 = x - hi.astype(F32)
    mid = r1.astype(BF)
    lo = (r1 - mid.astype(F32)).astype(BF)
    return hi, mid, lo


def _mask_dot(t, x):
    hi, mid, lo = _split3(x)
    return _dot(t, hi) + _dot(t, mid) + _dot(t, lo)


def _sigmoid(x):
    return 1.0 / (1.0 + jnp.exp(-x))


def _log_sigmoid(x):
    return jnp.minimum(x, 0.0) - jnp.log(1.0 + jnp.exp(-jnp.abs(x)))


def _colsum8(z):
    t, c = z.shape
    return jnp.sum(z.reshape(t // 8, 8, c), axis=0)


def _tri(n, kind):
    r = lax.broadcasted_iota(jnp.int32, (n, n), 0)
    c = lax.broadcasted_iota(jnp.int32, (n, n), 1)
    m = {"le": c <= r, "lt": c < r, "ge": c >= r, "gt": c > r}[kind]
    return m


def _full(shape):
    nd = len(shape)
    return pl.BlockSpec(shape, lambda *_: (0,) * nd)


def _cparams(sem, vmem=VMEM_LIMIT):
    return pltpu.CompilerParams(dimension_semantics=sem, vmem_limit_bytes=vmem)


def _call(body, grid, name, in_specs, out_specs, out_shape, scratch, operands, exchange=None, carried=()):
    n_in, n_out, n_scr = len(in_specs), len(out_specs), len(scratch)
    if exchange is None:
        fn = body
    else:
        n = exchange.n

        def fn(*refs):
            ins, cin = refs[:n_in], refs[n_in:n_in + n]
            outs, cout = refs[n_in + n:n_in + n + n_out], refs[n_in + n + n_out:n_in + 2 * n + n_out]
            rest = refs[n_in + 2 * n + n_out:]
            scr, sems = rest[:n_scr], rest[n_scr:]

            @pl.when(pl.program_id(0) == 0)
            def _():
                exchange.start(cin, cout, sems)

            body(*ins, *outs, *scr)

            @pl.when(pl.program_id(0) == pl.num_programs(0) - 1)
            def _():
                exchange.wait(cin, cout, sems)

        any_spec = pl.BlockSpec(memory_space=pl.ANY)
        in_specs = list(in_specs) + [any_spec] * n
        out_specs = list(out_specs) + [any_spec] * n
        out_shape = list(out_shape) + exchange.out_shape
        scratch = list(scratch) + exchange.scratch
    return pl.pallas_call(fn, grid=grid, name=name, in_specs=in_specs, out_specs=out_specs, out_shape=out_shape,
                          scratch_shapes=scratch, compiler_params=_cparams(("arbitrary",)))(*operands, *carried)


def _fill_padded(pad_ref, val, nseg):
    zeros = jnp.zeros((nseg, 16, val.shape[-1]), F32)
    pad_ref[:, 0:16, :] = zeros
    pad_ref[:, 16 + GW:SEGP, :] = zeros
    pad_ref[:, 16:16 + GW, :] = val.reshape(nseg, GW, val.shape[-1])


def _tap_slabs(pad_ref, s, cs):
    whole = pad_ref[s, :, cs]
    for r in range(8):
        slab = whole if r == 0 else pltpu.roll(whole, SEGP - r, axis=0)
        for a in range(4):
            j = r + 8 * a - 1
            if 0 <= j < CW:
                yield j, slab[8 * a:8 * a + GW]


def _conv_taps(pad_ref, s, w_ref, c0, cw, flip):
    acc = jnp.zeros((GW, cw), F32)
    for j, rows in _tap_slabs(pad_ref, s, pl.ds(c0, cw)):
        acc = acc + w_ref[pl.ds((CW - 1 - j) if flip else j, 1), pl.ds(c0, cw)] * rows
    return acc


def _ln_stats(yb):
    mu = jnp.mean(yb, axis=-1, keepdims=True)
    yc = yb - mu
    var = jnp.mean(yc * yc, axis=-1, keepdims=True)
    rs = lax.rsqrt(var + EPS)
    return yc * rs, rs


def fwd_in(x, vec1, win, convw, cvec, wa2, ba, tm, exchange=None, carried=()):
    n = x.shape[0]
    nseg = tm // GW
    cg = 128

    def body(x_ref, vec_ref, win_ref, cw_ref, cv_ref, wa2_ref, ba_ref,
             ag_ref, yb_ref, co_ref, qk_ref, v_ref, g_ref, la_ref, r_ref, pad_ref):
        xx = x_ref[...]
        rstd = lax.rsqrt(jnp.mean(xx * xx, axis=-1, keepdims=True) + EPS)
        h = (xx * rstd * vec_ref[0:1, :]) * (1.0 + vec_ref[2:3, :]) + vec_ref[1:2, :]
        p = _dot(h.astype(BF), win_ref[...])
        ag_ref[...] = p[:, :2 * DC].astype(BF)
        qk_ref[...] = p[:, 2 * DC:2 * DC + 2 * DK].astype(BF)
        v_ref[...] = p[:, 2 * DC + 2 * DK:2 * DC + 2 * DK + DV].astype(BF)
        g_ref[...] = p[:, 2 * DC + 2 * DK + DV:2 * DC + 2 * DK + 2 * DV].astype(BF)
        r = p[:, DINP - 128:].astype(BF)
        r_ref[...] = r
        la_ref[...] = _log_sigmoid(_dot(r, wa2_ref[...]) + ba_ref[...]) * (1.0 / TAU)

        _fill_padded(pad_ref, p[:, :DC] * _sigmoid(p[:, DC:2 * DC]), nseg)

        def seg(s, carry):
            for c0 in range(0, DC, cg):
                y = _conv_taps(pad_ref, s, cw_ref, c0, cg, False)
                yb_ref[pl.ds(pl.multiple_of(s * GW, GW), GW), pl.ds(c0, cg)] = y + cv_ref[0:1, c0:c0 + cg]
            return carry

        lax.fori_loop(0, nseg, seg, 0)
        yn, _ = _ln_stats(yb_ref[...])
        ln = yn * cv_ref[1:2, :] + cv_ref[2:3, :]
        co_ref[...] = (ln * _sigmoid(ln)).astype(BF)

    tok = lambda w: pl.BlockSpec((tm, w), lambda i: (i, 0))
    return _call(
        body, (n // tm,), "fwd_in",
        [tok(D), _full(vec1.shape), _full(win.shape), _full(convw.shape), _full(cvec.shape), _full(wa2.shape), _full(ba.shape)],
        [tok(2 * DC), tok(DC), tok(DC), tok(2 * DK), tok(DV), tok(DV), tok(2 * DK), tok(128)],
        [jax.ShapeDtypeStruct((n, 2 * DC), BF), jax.ShapeDtypeStruct((n, DC), F32),
         jax.ShapeDtypeStruct((n, DC), BF), jax.ShapeDtypeStruct((n, 2 * DK), BF),
         jax.ShapeDtypeStruct((n, DV), BF), jax.ShapeDtypeStruct((n, DV), BF),
         jax.ShapeDtypeStruct((n, 2 * DK), F32), jax.ShapeDtypeStruct((n, 128), BF)],
        [pltpu.VMEM((nseg, SEGP, DC), F32)],
        (x, vec1, win, convw, cvec, wa2, ba), exchange, carried)


def _gla_dir(d):
    return (_tri(CH, "le"), CH - 1) if d == 0 else (_tri(CH, "ge"), 0)


def _gla_chunk_terms(qk, la, d):
    seen, last = _gla_dir(d)
    b = _mask_dot(seen.astype(BF), la)
    bl = b[last:last + 1, :]
    eb = jnp.exp(b)
    enb = jnp.exp(-b)
    ekd = jnp.exp(bl - b)
    ebl = jnp.exp(bl)
    q = qk[:, :DK].astype(F32) * QSCALE
    k = qk[:, DK:].astype(F32)
    return eb, enb, ekd, ebl, q * eb, k * enb, k * ekd


NP = NH // 2
PW = 2 * HK


def _lo_lanes(shape):
    return lax.broadcasted_iota(jnp.int32, shape, len(shape) - 1) < HK


def _pair_sel(lo, hi):
    return jnp.where(_lo_lanes(lo.shape), lo, hi)


def _only(x, which):
    keep = _lo_lanes(x.shape) if which == 0 else jnp.logical_not(_lo_lanes(x.shape))
    return jnp.where(keep, x, jnp.zeros_like(x))


def gla_fwd(qk, v, la, s0, tm):
    n = qk.shape[0]
    nt = n // tm
    nc = tm // CH

    def body(qkf_ref, vf_ref, laf_ref, qkb_ref, vb_ref, lab_ref, s0_ref, of_ref, ob_ref, sef_ref, seb_ref, st_ref):
        @pl.when(pl.program_id(0) == 0)
        def _():
            st_ref[...] = s0_ref[...]

        def chunk(ci, carry):
            t = []
            for d, (qk_ref, v_ref, la_ref) in enumerate(((qkf_ref, vf_ref, laf_ref), (qkb_ref, vb_ref, lab_ref))):
                c = ci if d == 0 else nc - 1 - ci
                rows = pl.ds(pl.multiple_of(c * CH, CH), CH)
                eb, enb, ekd, ebl, qt, kt, kd = _gla_chunk_terms(qk_ref[rows, :], la_ref[rows, :], d)
                t.append(dict(c=c, rows=rows, ebl=ebl, qt=qt.astype(BF), kt=kt.astype(BF), kd=kd.astype(BF),
                              vv=v_ref[rows, :], st=[st_ref[d, p] for p in range(NP)], amask=_gla_dir(d)[0]))
            dh = [(d, h) for d in range(2) for h in range(NH)]
            ps = lambda h: slice((h // 2) * PW, (h // 2 + 1) * PW)
            vs = lambda h: slice(h * HV, (h + 1) * HV)
            qm = {(d, h): _only(t[d]['qt'][:, ps(h)], h % 2) for d, h in dh}
            a = {(d, h): jnp.where(t[d]['amask'], _dot_nt(qm[d, h], t[d]['kt'][:, ps(h)]), 0.0).astype(BF) for d, h in dh}
            o = {(d, h): _dot(a[d, h], t[d]['vv'][:, vs(h)]) + _dot_nt(qm[d, h], t[d]['st'][h // 2].astype(BF))
                 for d, h in dh}
            kv = {(d, h): _dot_tn(t[d]['vv'][:, vs(h)], t[d]['kd'][:, ps(h)]) for d, h in dh}
            for d, (o_ref, se_ref) in enumerate(((of_ref, sef_ref), (ob_ref, seb_ref))):
                for h in range(NH):
                    o_ref[t[d]['rows'], vs(h)] = o[d, h].astype(BF)
                for p in range(NP):
                    se_ref[t[d]['c'], p] = t[d]['st'][p]
                    st_ref[d, p] = (t[d]['ebl'][:, p * PW:(p + 1) * PW] * t[d]['st'][p]
                                    + _pair_sel(kv[d, 2 * p], kv[d, 2 * p + 1]))
            return carry

        lax.fori_loop(0, nc, chunk, 0, unroll=2)

    fw = lambda w, col=0: pl.BlockSpec((tm, w), lambda i: (i, col))
    bw = lambda w, col=0: pl.BlockSpec((tm, w), lambda i: (nt - 1 - i, col))
    se_f = pl.BlockSpec((nc, NP, HV, PW), lambda i: (i, 0, 0, 0))
    se_b = pl.BlockSpec((nc, NP, HV, PW), lambda i: (nt - 1 - i, 0, 0, 0))
    se_shape = jax.ShapeDtypeStruct((n // CH, NP, HV, PW), F32)
    return pl.pallas_call(
        body, grid=(nt,), name="gla_fwd",
        in_specs=[fw(2 * DK), fw(DV), fw(DK, 0), bw(2 * DK), bw(DV), bw(DK, 1), _full(s0.shape)],
        out_specs=[fw(DV), bw(DV), se_f, se_b],
        out_shape=[jax.ShapeDtypeStruct((n, DV), BF), jax.ShapeDtypeStruct((n, DV), BF), se_shape, se_shape],
        scratch_shapes=[pltpu.VMEM((2, NP, HV, PW), F32)],
        compiler_params=_cparams(("arbitrary",)),
    )(qk, v, la, qk, v, la, s0)


def gla_bwd(qk, v, la, do, se_f, se_b, tm, exchange=None, carried=()):
    n = qk.shape[0]
    nt = n // tm
    nc = tm // CH

    def body(qkf_ref, vf_ref, laf_ref, dof_ref, sef_ref, qkb_ref, vb_ref, lab_ref, dob_ref, seb_ref,
             dqkf_ref, dvf_ref, dlaf_ref, dqkb_ref, dvb_ref, dlab_ref, ds0_ref, ds_ref):
        @pl.when(pl.program_id(0) == 0)
        def _():
            ds_ref[...] = jnp.zeros_like(ds_ref)

        def chunk(ci, carry):
            t = []
            for d, (qk_ref, v_ref, la_ref, do_ref, se_ref) in enumerate(
                    ((qkf_ref, vf_ref, laf_ref, dof_ref, sef_ref), (qkb_ref, vb_ref, lab_ref, dob_ref, seb_ref))):
                c = nc - 1 - ci if d == 0 else ci
                rows = pl.ds(pl.multiple_of(c * CH, CH), CH)
                amask, last = _gla_dir(d)
                eb, enb, ekd, ebl, qt, kt, kd = _gla_chunk_terms(qk_ref[rows, :], la_ref[rows, :], d)
                t.append(dict(rows=rows, amask=amask, last=last, eb=eb, enb=enb, ekd=ekd, ebl=ebl, qt=qt, kt=kt, kd=kd,
                              qtb=qt.astype(BF), ktb=kt.astype(BF), kdb=kd.astype(BF), vv=v_ref[rows, :], dd=do_ref[rows, :],
                              st=[se_ref[c, p] for p in range(NP)], dsn=[ds_ref[d, p] for p in range(NP)]))
            dh = [(d, h) for d in range(2) for h in range(NH)]
            dp = [(d, p) for d in range(2) for p in range(NP)]
            ps = lambda h: slice((h // 2) * PW, (h // 2 + 1) * PW)
            vs = lambda h: slice(h * HV, (h + 1) * HV)
            stb = {(d, p): t[d]['st'][p].astype(BF) for d, p in dp}
            dsnb = {(d, p): t[d]['dsn'][p].astype(BF) for d, p in dp}
            qm = {(d, h): _only(t[d]['qtb'][:, ps(h)], h % 2) for d, h in dh}
            km = {(d, h): _only(t[d]['kdb'][:, ps(h)], h % 2) for d, h in dh}
            a = {(d, h): jnp.where(t[d]['amask'], _dot_nt(qm[d, h], t[d]['ktb'][:, ps(h)]), 0.0).astype(BF) for d, h in dh}
            da = {(d, h): jnp.where(t[d]['amask'], _dot_nt(t[d]['dd'][:, vs(h)], t[d]['vv'][:, vs(h)]), 0.0).astype(BF)
                  for d, h in dh}
            dv = {(d, h): _dot_tn(a[d, h], t[d]['dd'][:, vs(h)]) + _dot_nt(km[d, h], dsnb[d, h // 2]) for d, h in dh}
            dkd = {(d, h): _dot(t[d]['vv'][:, vs(h)], dsnb[d, h // 2]) for d, h in dh}
            dqt = {(d, h): _dot(da[d, h], t[d]['ktb'][:, ps(h)]) + _dot(t[d]['dd'][:, vs(h)], stb[d, h // 2]) for d, h in dh}
            dkt = {(d, h): _dot_tn(da[d, h], t[d]['qtb'][:, ps(h)]) for d, h in dh}
            dsq = {(d, h): _dot_tn(t[d]['dd'][:, vs(h)], t[d]['qtb'][:, ps(h)]) for d, h in dh}
            for d, (dqk_ref, dv_ref, dla_ref) in enumerate(((dqkf_ref, dvf_ref, dlaf_ref), (dqkb_ref, dvb_ref, dlab_ref))):
                td = t[d]
                rows = td['rows']
                for h in range(NH):
                    dv_ref[rows, vs(h)] = dv[d, h].astype(BF)
                pair = lambda x: jnp.concatenate([_pair_sel(x[d, 2 * p], x[d, 2 * p + 1]) for p in range(NP)], axis=1)
                dqt_, dkt_, dkd_ = pair(dqt), pair(dkt), pair(dkd)
                debl = jnp.concatenate([jnp.sum(td['st'][p] * td['dsn'][p], axis=0, keepdims=True) for p in range(NP)], axis=1)
                for p in range(NP):
                    ds_ref[d, p] = _pair_sel(dsq[d, 2 * p], dsq[d, 2 * p + 1]) + td['ebl'][:, p * PW:(p + 1) * PW] * td['dsn'][p]
                dkdkd = dkd_ * td['kd']
                dbl = jnp.sum(dkdkd, axis=0, keepdims=True) + debl * td['ebl']
                is_last = lax.broadcasted_iota(jnp.int32, (CH, DK), 0) == td['last']
                db = dqt_ * td['qt'] - dkt_ * td['kt'] - dkdkd + jnp.where(is_last, dbl, 0.0)
                dqk_ref[rows, :] = jnp.concatenate([dqt_ * td['eb'] * QSCALE, dkt_ * td['enb'] + dkd_ * td['ekd']], axis=1).astype(BF)
                dla_ref[rows, :] = _mask_dot(_gla_dir(1 - d)[0].astype(BF), db)
            return carry

        lax.fori_loop(0, nc, chunk, 0, unroll=2)

        @pl.when(pl.program_id(0) == nt - 1)
        def _():
            ds0_ref[...] = ds_ref[...]

    up = lambda w, col=0: pl.BlockSpec((tm, w), lambda i: (i, col))
    dn = lambda w, col=0: pl.BlockSpec((tm, w), lambda i: (nt - 1 - i, col))
    se_up = pl.BlockSpec((nc, NP, HV, PW), lambda i: (i, 0, 0, 0))
    se_dn = pl.BlockSpec((nc, NP, HV, PW), lambda i: (nt - 1 - i, 0, 0, 0))
    return _call(
        body, (nt,), "gla_bwd",
        [dn(2 * DK), dn(DV), dn(DK, 0), dn(DV), se_dn, up(2 * DK), up(DV), up(DK, 1), up(DV), se_up],
        [dn(2 * DK), dn(DV), dn(DK), up(2 * DK), up(DV), up(DK), _full((2, NP, HV, PW))],
        [jax.ShapeDtypeStruct((n, 2 * DK), BF), jax.ShapeDtypeStruct((n, DV), BF),
         jax.ShapeDtypeStruct((n, DK), F32), jax.ShapeDtypeStruct((n, 2 * DK), BF),
         jax.ShapeDtypeStruct((n, DV), BF), jax.ShapeDtypeStruct((n, DK), F32),
         jax.ShapeDtypeStruct((2, NP, HV, PW), F32)],
        [pltpu.VMEM((2, NP, HV, PW), F32)],
        (qk, v, la, do, se_f, qk, v, la, do, se_b), exchange, carried)


def _head_norm(o):
    ons, rss = [], []
    for h in range(NH):
        oh = o[:, h * HV:(h + 1) * HV]
        rs = lax.rsqrt(jnp.mean(oh * oh, axis=-1, keepdims=True) + EPS)
        ons.append(oh * rs)
        rss.append(rs)
    return ons, rss


def merge_fwd(x, o_f, o_b, g, co, vecm, gn, wout, tm):
    n = x.shape[0]

    def body(x_ref, of_ref, ob_ref, g_ref, co_ref, vec_ref, gn_ref, w_ref, x1_ref, y1_ref, cat_ref):
        o = of_ref[...].astype(F32) + ob_ref[...].astype(F32)
        ons, _ = _head_norm(o)
        gg = g_ref[...].astype(F32)
        sil = gg * _sigmoid(gg)
        cat_ref[:, :DC] = co_ref[...]
        for h in range(NH):
            vs = slice(h * HV, (h + 1) * HV)
            cat_ref[:, DC + h * HV:DC + (h + 1) * HV] = (ons[h] * gn_ref[:, vs] * sil[:, vs]).astype(BF)
        y1 = _dot(cat_ref[...], w_ref[...])
        y1_ref[...] = y1.astype(BF)
        x1_ref[...] = x_ref[...] + vec_ref[0:1, :] * y1

    tok = lambda w: pl.BlockSpec((tm, w), lambda i: (i, 0))
    return pl.pallas_call(
        body, grid=(n // tm,), name="merge_fwd",
        in_specs=[tok(D), tok(DV), tok(DV), tok(DV), tok(DC), _full(vecm.shape), _full(gn.shape), _full(wout.shape)],
        out_specs=[tok(D), tok(D), tok(D)],
        out_shape=[jax.ShapeDtypeStruct((n, D), F32), jax.ShapeDtypeStruct((n, D), BF), jax.ShapeDtypeStruct((n, D), BF)],
        compiler_params=_cparams(("arbitrary",)),
    )(x, o_f, o_b, g, co, vecm, gn, wout)


def merge_bwd(dx1, y1, o_f, o_b, g, vecm, gn, wout, tm):
    n = dx1.shape[0]

    def body(dx1_ref, y1_ref, of_ref, ob_ref, g_ref, vec_ref, gn_ref, w_ref,
             dy1_ref, dco_ref, do_ref, dg_ref, s1_ref, s2_ref):
        @pl.when(pl.program_id(0) == 0)
        def _():
            s1_ref[...] = jnp.zeros_like(s1_ref)
            s2_ref[...] = jnp.zeros_like(s2_ref)

        dx1 = dx1_ref[...]
        s1_ref[...] += _colsum8(dx1 * y1_ref[...].astype(F32))
        dy1 = (dx1 * vec_ref[0:1, :]).astype(BF)
        dy1_ref[...] = dy1
        dcat = _dot_nt(dy1, w_ref[...])
        dco_ref[...] = dcat[:, :DC].astype(BF)
        o = of_ref[...].astype(F32) + ob_ref[...].astype(F32)
        ons, rss = _head_norm(o)
        gg = g_ref[...].astype(F32)
        sg = _sigmoid(gg)
        sil = gg * sg
        dsil = sg * (1.0 + gg * (1.0 - sg))
        for h in range(NH):
            vs = slice(h * HV, (h + 1) * HV)
            do2 = dcat[:, DC + h * HV:DC + (h + 1) * HV]
            gnh = gn_ref[:, vs]
            t = do2 * sil[:, vs]
            s2_ref[:, vs] += _colsum8(t * ons[h])
            don = t * gnh
            do_ref[:, vs] = (rss[h] * (don - ons[h] * jnp.mean(don * ons[h], axis=-1, keepdims=True))).astype(BF)
            dg_ref[:, vs] = (do2 * ons[h] * gnh * dsil[:, vs]).astype(BF)

    tok = lambda w: pl.BlockSpec((tm, w), lambda i: (i, 0))
    return pl.pallas_call(
        body, grid=(n // tm,), name="merge_bwd",
        in_specs=[tok(D), tok(D), tok(DV), tok(DV), tok(DV), _full(vecm.shape), _full(gn.shape), _full(wout.shape)],
        out_specs=[tok(D), tok(DC), tok(DV), tok(DV), _full((8, D)), _full((8, DV))],
        out_shape=[jax.ShapeDtypeStruct((n, D), BF), jax.ShapeDtypeStruct((n, DC), BF), jax.ShapeDtypeStruct((n, DV), BF),
                   jax.ShapeDtypeStruct((n, DV), BF), jax.ShapeDtypeStruct((8, D), F32), jax.ShapeDtypeStruct((8, DV), F32)],
        compiler_params=_cparams(("arbitrary",)),
    )(dx1, y1, o_f, o_b, g, vecm, gn, wout)


def ffn_fwd_bwd(x1, tgt, vecf, wg, wu, wd, tm):
    n = x1.shape[0]

    def body(x1_ref, t_ref, vec_ref, wg_ref, wu_ref, wd_ref,
             dx1_ref, h2_ref, act_ref, dgt_ref, dup_ref, dy2_ref, s_ref):
        @pl.when(pl.program_id(0) == 0)
        def _():
            s_ref[...] = jnp.zeros_like(s_ref)

        n2g, sh2, sc2, g2, fg = (vec_ref[i:i + 1, :] for i in range(5))
        x1 = x1_ref[...]
        r2 = lax.rsqrt(jnp.mean(x1 * x1, axis=-1, keepdims=True) + EPS)
        xn2 = x1 * r2
        h2 = (xn2 * n2g * (1.0 + sc2) + sh2).astype(BF)
        h2_ref[...] = h2
        gt = _dot(h2, wg_ref[...])
        up = _dot(h2, wu_ref[...])
        sg = _sigmoid(gt)
        sil = gt * sg
        act = (sil * up).astype(BF)
        act_ref[...] = act
        y2 = _dot(act, wd_ref[...])
        x2 = x1 + g2 * y2
        r3 = lax.rsqrt(jnp.mean(x2 * x2, axis=-1, keepdims=True) + EPS)
        xn3 = x2 * r3
        e = xn3 * fg - t_ref[...]
        s_ref[40:48, :] += _colsum8(e * e) * (0.5 / D)
        dyo = e * (1.0 / D)
        s_ref[0:8, :] += _colsum8(dyo * xn3)
        dxn3 = dyo * fg
        dx2 = r3 * (dxn3 - xn3 * jnp.mean(dxn3 * xn3, axis=-1, keepdims=True))
        s_ref[8:16, :] += _colsum8(dx2 * y2)
        dy2 = (dx2 * g2).astype(BF)
        dy2_ref[...] = dy2
        dact = _dot_nt(dy2, wd_ref[...])
        dup = (dact * sil).astype(BF)
        dgt = (dact * up * (sg * (1.0 + gt * (1.0 - sg)))).astype(BF)
        dup_ref[...] = dup
        dgt_ref[...] = dgt
        dh2 = _dot_nt(dgt, wg_ref[...]) + _dot_nt(dup, wu_ref[...])
        s_ref[16:24, :] += _colsum8(dh2)
        t = dh2 * xn2
        s_ref[24:32, :] += _colsum8(t * n2g)
        s_ref[32:40, :] += _colsum8(t * (1.0 + sc2))
        dxn2 = dh2 * ((1.0 + sc2) * n2g)
        dx1_ref[...] = dx2 + r2 * (dxn2 - xn2 * jnp.mean(dxn2 * xn2, axis=-1, keepdims=True))

    tok = lambda w: pl.BlockSpec((tm, w), lambda i: (i, 0))
    wspec = lambda a: pl.BlockSpec(a.shape, lambda i: (0, 0), pipeline_mode=pl.Buffered(1))
    return pl.pallas_call(
        body, grid=(n // tm,), name="ffn_fwd_bwd",
        in_specs=[tok(D), tok(D), _full(vecf.shape), wspec(wg), wspec(wu), wspec(wd)],
        out_specs=[tok(D), tok(D), tok(DFF), tok(DFF), tok(DFF), tok(D), _full((48, D))],
        out_shape=[jax.ShapeDtypeStruct((n, D), F32), jax.ShapeDtypeStruct((n, D), BF), jax.ShapeDtypeStruct((n, DFF), BF),
                   jax.ShapeDtypeStruct((n, DFF), BF), jax.ShapeDtypeStruct((n, DFF), BF), jax.ShapeDtypeStruct((n, D), BF),
                   jax.ShapeDtypeStruct((48, D), F32)],
        compiler_params=_cparams(("arbitrary",)),
    )(x1, tgt, vecf, wg, wu, wd)


def wgrad(a, b, init, t1, t2, tn, name):
    n, k1 = a.shape
    k2 = b.shape[1]

    def body(a_ref, b_ref, *rest):
        o_ref = rest[-1]

        @pl.when(pl.program_id(2) == 0)
        def _():
            o_ref[...] = rest[0][...] if init is not None else jnp.zeros_like(o_ref)

        o_ref[...] += _dot_tn(a_ref[...], b_ref[...])

    ospec = pl.BlockSpec((t1, t2), lambda i, j, k: (i, j))
    extra = ([ospec], {2: 0}, (init,)) if init is not None else ([], {}, ())
    return pl.pallas_call(
        body, grid=(k1 // t1, k2 // t2, n // tn), name=name,
        in_specs=[pl.BlockSpec((tn, t1), lambda i, j, k: (k, i)), pl.BlockSpec((tn, t2), lambda i, j, k: (k, j))] + extra[0],
        out_specs=ospec, out_shape=jax.ShapeDtypeStruct((k1, k2), F32), input_output_aliases=extra[1],
        compiler_params=_cparams(("parallel", "parallel", "arbitrary")),
    )(a, b, *extra[2])


def bwd_in(x, dx1, ag, yb, dco, dqk_f, dqk_b, dv_f, dv_b, dg, dla_f, dla_b, la, r, vec1, win, convw, cvec, wa2, tm):
    n = x.shape[0]
    nseg = tm // GW
    cg = 128

    def body(x_ref, dx1_ref, ag_ref, yb_ref, dco_ref, dqkf_ref, dqkb_ref, dvf_ref, dvb_ref, dg_ref, dlaf_ref, dlab_ref,
             la_ref, r_ref, vec_ref, win_ref, cw_ref, cv_ref, wa2_ref,
             gx_ref, h_ref, dp_ref, dwa2_ref, dcw_ref, s_ref, vc_ref, pad2_ref, dvc_ref, dcw8_ref):
        first = pl.program_id(0) == 0

        @pl.when(first)
        def _():
            s_ref[...] = jnp.zeros_like(s_ref)
            dwa2_ref[...] = jnp.zeros_like(dwa2_ref)
            dcw8_ref[...] = jnp.zeros_like(dcw8_ref)

        yn, rs = _ln_stats(yb_ref[...])
        lng = cv_ref[1:2, :]
        ln = yn * lng + cv_ref[2:3, :]
        sgl = _sigmoid(ln)
        dln = dco_ref[...].astype(F32) * (sgl * (1.0 + ln * (1.0 - sgl)))
        dyn = dln * lng
        dyb = rs * (dyn - jnp.mean(dyn, axis=-1, keepdims=True) - yn * jnp.mean(dyn * yn, axis=-1, keepdims=True))
        s_ref[24:32, 0:DC] += _colsum8(dyb)
        s_ref[24:32, DC:D] += _colsum8(dln * yn)
        s_ref[32:40, 0:DC] += _colsum8(dln)

        agv = ag_ref[...].astype(F32)
        a = agv[:, :DC]
        sgg = _sigmoid(agv[:, DC:])
        vc_ref[...] = a * sgg
        _fill_padded(pad2_ref, dyb, nseg)

        def seg(s, carry):
            rows = pl.ds(pl.multiple_of(s * GW, GW), GW)
            for c0 in range(0, DC, cg):
                cs = pl.ds(c0, cg)
                vcs = vc_ref[rows, cs]
                acc = jnp.zeros((GW, cg), F32)
                for j, rows_j in _tap_slabs(pad2_ref, s, cs):
                    acc = acc + cw_ref[pl.ds(CW - 1 - j, 1), cs] * rows_j
                    dcw8_ref[CW - 1 - j, :, cs] += _colsum8(vcs * rows_j)
                dvc_ref[rows, cs] = acc
            return carry

        lax.fori_loop(0, nseg, seg, 0)
        dvc = dvc_ref[...]
        dp_ref[:, 0:DC] = (dvc * sgg).astype(BF)
        dp_ref[:, DC:2 * DC] = (dvc * a * sgg * (1.0 - sgg)).astype(BF)

        dp_ref[:, 2 * DC:2 * DC + 2 * DK] = (dqkf_ref[...].astype(F32) + dqkb_ref[...].astype(F32)).astype(BF)
        dp_ref[:, 2 * DC + 2 * DK:2 * DC + 2 * DK + DV] = (dvf_ref[...].astype(F32) + dvb_ref[...].astype(F32)).astype(BF)
        dp_ref[:, 2 * DC + 2 * DK + DV:2 * DC + 2 * DK + 2 * DV] = dg_ref[...]

        la = la_ref[...]
        dla = jnp.concatenate([dlaf_ref[...], dlab_ref[...]], axis=1)
        dpre = dla * (1.0 - jnp.exp(TAU * la)) * (1.0 / TAU)
        s_ref[32:40, DC:D] += _colsum8(dpre)
        dpreb = dpre.astype(BF)
        dwa2_ref[...] += _dot_tn(r_ref[...], dpreb)
        dp_ref[:, DINP - 128:] = _dot_nt(dpreb, wa2_ref[...]).astype(BF)

        dh = _dot_nt(dp_ref[...], win_ref[...])
        xx = x_ref[...]
        n1g, sh1, sc1 = vec_ref[0:1, :], vec_ref[1:2, :], vec_ref[2:3, :]
        rstd = lax.rsqrt(jnp.mean(xx * xx, axis=-1, keepdims=True) + EPS)
        xn = xx * rstd
        h_ref[...] = (xn * n1g * (1.0 + sc1) + sh1).astype(BF)
        s_ref[0:8, :] += _colsum8(dh)
        t = dh * xn
        s_ref[8:16, :] += _colsum8(t * n1g)
        s_ref[16:24, :] += _colsum8(t * (1.0 + sc1))
        dxn = dh * ((1.0 + sc1) * n1g)
        gx_ref[...] = dx1_ref[...] + rstd * (dxn - xn * jnp.mean(dxn * xn, axis=-1, keepdims=True))

        @pl.when(pl.program_id(0) == pl.num_programs(0) - 1)
        def _():
            dcw_ref[...] = jnp.sum(dcw8_ref[...], axis=1)

    tok = lambda w: pl.BlockSpec((tm, w), lambda i: (i, 0))
    return pl.pallas_call(
        body, grid=(n // tm,), name="bwd_in",
        in_specs=[tok(D), tok(D), tok(2 * DC), tok(DC), tok(DC), tok(2 * DK), tok(2 * DK), tok(DV), tok(DV), tok(DV),
                  tok(DK), tok(DK), tok(2 * DK), tok(128), _full(vec1.shape),
                  pl.BlockSpec(win.shape, lambda i: (0, 0), pipeline_mode=pl.Buffered(1)),
                  _full(convw.shape), _full(cvec.shape), _full(wa2.shape)],
        out_specs=[tok(D), tok(D), tok(DINP), _full((128, 2 * DK)), _full((32, DC)), _full((40, D))],
        out_shape=[jax.ShapeDtypeStruct((n, D), F32), jax.ShapeDtypeStruct((n, D), BF), jax.ShapeDtypeStruct((n, DINP), BF),
                   jax.ShapeDtypeStruct((128, 2 * DK), F32), jax.ShapeDtypeStruct((32, DC), F32),
                   jax.ShapeDtypeStruct((40, D), F32)],
        scratch_shapes=[pltpu.VMEM((tm, DC), F32), pltpu.VMEM((nseg, SEGP, DC), F32), pltpu.VMEM((tm, DC), F32),
                        pltpu.VMEM((32, 8, DC), F32)],
        compiler_params=_cparams(("arbitrary",)),
    )(x, dx1, ag, yb, dco, dqk_f, dqk_b, dv_f, dv_b, dg, dla_f, dla_b, la, r, vec1, win, convw, cvec, wa2)


def _ctx_common(ctx_ref, vec_ref, win_ref, wa2_ref, ba_ref):
    cx = ctx_ref[...]
    t = cx.shape[0]
    rstd = lax.rsqrt(jnp.mean(cx * cx, axis=-1, keepdims=True) + EPS)
    xn = cx * rstd
    hc = (xn * vec_ref[0:1, :] * (1.0 + vec_ref[2:3, :]) + vec_ref[1:2, :]).astype(BF)
    k0 = 2 * DC + DK
    kv = _dot(hc, win_ref[:, k0:k0 + DK + DV]).astype(BF).astype(F32)
    r = _dot(hc, win_ref[:, DINP - 128:]).astype(BF)
    la = _log_sigmoid(_dot(r, wa2_ref[...]) + ba_ref[...]) * (1.0 / TAU)
    incl = _tri(t, "le").astype(BF)
    strict = _tri(t, "lt").astype(BF)
    bf = _mask_dot(incl, la[:, :DK])
    wf = jnp.exp(bf[t - 1:t, :] - bf)
    wb = jnp.exp(_mask_dot(strict, la[:, DK:]))
    return xn, hc, kv[:, :DK], kv[:, DK:], r, la, wf, wb


def ctx_fwd(ctx, vecc, win, wa2, ba):
    def body(ctx_ref, vec_ref, win_ref, wa2_ref, ba_ref, s_ref):
        _, _, k, v, _, _, wf, wb = _ctx_common(ctx_ref, vec_ref, win_ref, wa2_ref, ba_ref)
        vb = v.astype(BF)
        for d, w in enumerate((wf, wb)):
            kd = (k * w).astype(BF)
            for h in range(NH):
                s_ref[d, h // 2, :, (h % 2) * HK:(h % 2 + 1) * HK] = _dot_tn(vb[:, h * HV:(h + 1) * HV], kd[:, h * HK:(h + 1) * HK])

    return pl.pallas_call(
        body, name="ctx_fwd", out_shape=jax.ShapeDtypeStruct((2, NP, HV, PW), F32),
        compiler_params=pltpu.CompilerParams(vmem_limit_bytes=VMEM_LIMIT),
    )(ctx, vecc, win, wa2, ba)


def ctx_bwd(ctx, vecc, win, wa2, ba, ds0):
    t = ctx.shape[0]

    def body(ctx_ref, vec_ref, win_ref, wa2_ref, ba_ref, ds_ref, dwin_ref, dwa2_ref, s_ref, dpc_ref):
        xn, hc, k, v, r, la, wf, wb = _ctx_common(ctx_ref, vec_ref, win_ref, wa2_ref, ba_ref)
        vb = v.astype(BF)
        strict = _tri(t, "lt").astype(BF)
        strict_t = _tri(t, "gt").astype(BF)
        dpc_ref[...] = jnp.zeros_like(dpc_ref)
        k0 = 2 * DC + DK
        dk = jnp.zeros((t, DK), F32)
        des = []
        for d, w in enumerate((wf, wb)):
            kd = (k * w).astype(BF)
            dkds = []
            for h in range(NH):
                dsb = ds_ref[d, h // 2, :, (h % 2) * HK:(h % 2 + 1) * HK].astype(BF)
                dkds.append(_dot(vb[:, h * HV:(h + 1) * HV], dsb))
                dvh = _dot_nt(kd[:, h * HK:(h + 1) * HK], dsb)
                vs = slice(k0 + DK + h * HV, k0 + DK + (h + 1) * HV)
                if d == 0:
                    dpc_ref[:, vs] = dvh.astype(BF)
                else:
                    dpc_ref[:, vs] = (dpc_ref[:, vs].astype(F32) + dvh).astype(BF)
            dkd = jnp.concatenate(dkds, axis=1)
            dk = dk + dkd * w
            des.append(dkd * k * w)
        dpc_ref[:, k0:k0 + DK] = dk.astype(BF)
        dla = jnp.concatenate([_mask_dot(strict, des[0]), _mask_dot(strict_t, des[1])], axis=1)
        dpre = dla * (1.0 - jnp.exp(TAU * la)) * (1.0 / TAU)
        dpreb = dpre.astype(BF)
        dwa2_ref[...] = _dot_tn(r, dpreb)
        dpc_ref[:, DINP - 128:] = _dot_nt(dpreb, wa2_ref[...]).astype(BF)
        dpc = dpc_ref[...]
        dwin_ref[...] = _dot_tn(hc, dpc)
        dhc = _dot_nt(dpc, win_ref[...])
        n1g, sc1 = vec_ref[0:1, :], vec_ref[2:3, :]
        tt = dhc * xn
        s_ref[...] = jnp.zeros_like(s_ref)
        s_ref[0:1, :] = jnp.sum(tt * (1.0 + sc1), axis=0, keepdims=True)
        s_ref[1:2, :] = jnp.sum(dhc, axis=0, keepdims=True)
        s_ref[2:3, :] = jnp.sum(tt * n1g, axis=0, keepdims=True)
        s_ref[3:4, DC:D] = jnp.sum(dpre, axis=0, keepdims=True)

    return pl.pallas_call(
        body, name="ctx_bwd",
        out_shape=[jax.ShapeDtypeStruct((D, DINP), F32), jax.ShapeDtypeStruct((128, 2 * DK), F32),
                   jax.ShapeDtypeStruct((8, D), F32)],
        scratch_shapes=[pltpu.VMEM((t, DINP), BF)],
        compiler_params=pltpu.CompilerParams(vmem_limit_bytes=VMEM_LIMIT),
    )(ctx, vecc, win, wa2, ba, ds0)


def _silu(x):
    return x * _sigmoid(x)


def mod_fwd(cext, wm, bm):
    def body(c_ref, w_ref, b_ref, o_ref):
        o_ref[...] = _dot(_silu(c_ref[...]).astype(BF), w_ref[...].astype(BF)) + b_ref[...]

    return pl.pallas_call(body, name="mod_fwd", out_shape=jax.ShapeDtypeStruct((cext.shape[0], wm.shape[1]), F32),
                          compiler_params=pltpu.CompilerParams(vmem_limit_bytes=VMEM_LIMIT))(cext, wm, bm)


def mod_bwd(cext, dm, wm):
    def body(c_ref, d_ref, w_ref, gw_ref, ds_ref):
        dmb = d_ref[...].astype(BF)
        gw_ref[...] = _dot_tn(_silu(c_ref[...]).astype(BF), dmb)
        ds_ref[...] = _dot_nt(dmb, w_ref[...].astype(BF))

    return pl.pallas_call(body, name="mod_bwd",
                          out_shape=[jax.ShapeDtypeStruct(wm.shape, F32), jax.ShapeDtypeStruct(cext.shape, F32)],
                          compiler_params=pltpu.CompilerParams(vmem_limit_bytes=VMEM_LIMIT))(cext, dm, wm)


def pack_small(sf, s1, s2, sd, sc, dcw, dwa2, dwa2_c):
    def body(sf_ref, s1_ref, s2_ref, sd_ref, sc_ref, dcw_ref, dwa2_ref, dwa2c_ref, o_ref, ocw_ref, owa_ref):
        rsum = lambda ref, i: jnp.sum(ref[8 * i:8 * i + 8, :], axis=0, keepdims=True)
        o_ref[...] = jnp.zeros_like(o_ref)
        o_ref[0:1, :] = rsum(sd_ref, 0)
        o_ref[1:2, :] = rsum(sd_ref, 1)
        o_ref[2:3, :] = rsum(s1_ref, 0)
        o_ref[3:4, :] = rsum(sf_ref, 2)
        o_ref[4:5, :] = rsum(sf_ref, 3)
        o_ref[5:6, :] = rsum(sf_ref, 1)
        o_ref[6:7, :] = sc_ref[1:2, :]
        o_ref[7:8, :] = sc_ref[2:3, :]
        o_ref[8:9, :] = rsum(sd_ref, 2) + sc_ref[0:1, :]
        o_ref[9:10, :] = rsum(sf_ref, 4)
        o_ref[10:11, :] = rsum(sf_ref, 0)
        o_ref[11:12, :] = rsum(sd_ref, 3)
        o_ref[12:13, :] = rsum(sd_ref, 4) + sc_ref[3:4, :]
        g = jnp.sum(s2_ref[...], axis=0, keepdims=True)
        o_ref[13:14, 0:HV] = g[:, 0:HV] + g[:, HV:2 * HV] + g[:, 2 * HV:3 * HV] + g[:, 3 * HV:4 * HV]
        o_ref[14:15, :] = rsum(sf_ref, 5)
        ocw_ref[...] = dcw_ref[...]
        owa_ref[...] = dwa2_ref[0:32, :] + dwa2c_ref[0:32, :]

    return pl.pallas_call(body, name="pack_small",
                          out_shape=[jax.ShapeDtypeStruct((16, D), F32), jax.ShapeDtypeStruct((32, DC), F32),
                                     jax.ShapeDtypeStruct((32, 2 * DK), F32)])(sf, s1, s2, sd, sc, dcw, dwa2, dwa2_c)


def small_totals(g8):
    r = g8.shape[1]

    def body(g_ref, t_ref, bm_ref, loss_ref):
        acc = g_ref[0]
        for i in range(1, NDEV):
            acc = acc + g_ref[i]
        t_ref[...] = acc
        bm_ref[...] = jnp.zeros_like(bm_ref)
        bm_ref[0:6, :] = acc[0:6, :]
        bm_ref[0:2, :] += acc[6:8, :]
        loss_ref[...] = jnp.broadcast_to(jnp.sum(acc[14:15, :], axis=1, keepdims=True), loss_ref.shape)

    return pl.pallas_call(body, name="small_totals",
                          out_shape=[jax.ShapeDtypeStruct((r, D), F32), jax.ShapeDtypeStruct((8, D), F32),
                                     jax.ShapeDtypeStruct((8, 128), F32)])(g8)


def cctx_grad(p8, c_ctx_row):
    def body(p_ref, c_ref, o_ref):
        acc = p_ref[0, 0:1, :]
        for j in range(1, NCHIP):
            acc = acc + p_ref[2 * j, 0:1, :]
        cc = c_ref[0:1, :]
        sg = _sigmoid(cc)
        o_ref[...] = jnp.zeros_like(o_ref)
        o_ref[0:1, :] = acc * (sg * (1.0 + cc * (1.0 - sg)))

    return pl.pallas_call(body, name="cctx_grad", out_shape=jax.ShapeDtypeStruct((8, D), F32))(p8, c_ctx_row)


def adamw(w, g, m, v, rows, name, emit_grad=False):
    r, c = w.shape

    def body(w_ref, g_ref, m_ref, v_ref, d_ref, nm_ref, nv_ref, *go_ref):
        gg = g_ref[...]
        nm = ADAM_B1 * m_ref[...] + (1.0 - ADAM_B1) * gg
        nv = ADAM_B2 * v_ref[...] + (1.0 - ADAM_B2) * (gg * gg)
        m_hat = nm / (1.0 - ADAM_B1 ** ADAM_STEP)
        v_hat = nv / (1.0 - ADAM_B2 ** ADAM_STEP)
        d_ref[...] = -ADAM_LR * (m_hat / (jnp.sqrt(v_hat) + ADAM_EPS) + ADAM_WD * w_ref[...])
        nm_ref[...] = nm
        nv_ref[...] = nv
        if emit_grad:
            go_ref[0][...] = gg

    spec = pl.BlockSpec((rows, c), lambda i: (i, 0))
    sds = jax.ShapeDtypeStruct((r, c), F32)
    nout = 4 if emit_grad else 3
    return pl.pallas_call(
        body, grid=(r // rows,), name=name, in_specs=[spec] * 4, out_specs=[spec] * nout, out_shape=[sds] * nout,
        compiler_params=_cparams(("parallel",)),
    )(w, g, m, v)


def _me():
    return lax.axis_index("x"), lax.axis_index("y"), lax.axis_index("c")


def _flip(v, bit):
    return 1 - v if bit else v


ANY = pl.BlockSpec(memory_space=pl.ANY)


def _gather8(x_ref, o_ref, ssem, rsem, lsem):
    mx, my, mc = _me()
    me = 4 * mx + 2 * my + mc
    local = pltpu.make_async_copy(x_ref, o_ref.at[me], lsem)
    local.start()
    peer = lambda k: (_flip(mx, k & 4), _flip(my, k & 2), _flip(mc, k & 1))
    sends = []
    for k in range(1, NDEV):
        cp = pltpu.make_async_remote_copy(src_ref=x_ref, dst_ref=o_ref.at[me], send_sem=ssem.at[k - 1],
                                          recv_sem=rsem.at[k - 1], device_id=peer(k), device_id_type=MESH)
        cp.start()
        sends.append(cp)
    for k in range(1, NDEV):
        px, py, pc = peer(k)
        pltpu.make_async_remote_copy(src_ref=x_ref, dst_ref=o_ref.at[4 * px + 2 * py + pc], send_sem=ssem.at[k - 1],
                                     recv_sem=rsem.at[k - 1], device_id=(px, py, pc), device_id_type=MESH).wait_recv()
    for cp in sends:
        cp.wait_send()
    local.wait()


def _gather8_sems():
    return [pltpu.SemaphoreType.DMA((NDEV - 1,)), pltpu.SemaphoreType.DMA((NDEV - 1,)), pltpu.SemaphoreType.DMA]


def all_gather8(x, name):
    vm = pl.BlockSpec(memory_space=pltpu.VMEM)
    return pl.pallas_call(_gather8_body(), name=name, in_specs=[vm], out_specs=vm,
                          out_shape=jax.ShapeDtypeStruct((NDEV,) + x.shape, x.dtype), scratch_shapes=_gather8_sems())(x)


def _gather8_body():
    def body(x_ref, o_ref, ssem, rsem, lsem):
        _gather8(x_ref, o_ref, ssem, rsem, lsem)
    return body


def prologue(small, c_ctx_rows, wm, bm, w_in_shard):
    ex = ChipExchange("gather", [w_in_shard])

    def body(s_ref, cc_ref, w_ref, b_ref, win_ref, s8_ref, m8_ref, wing_ref, mloc_ref, *sems):
        ex.start([win_ref], [wing_ref], sems[6:])
        _gather8(s_ref, s8_ref, *sems[0:3])
        cext = jnp.concatenate([s8_ref[:, 0, :], cc_ref[...]], axis=0)
        mloc_ref[...] = _dot(_silu(cext).astype(BF), w_ref[...].astype(BF)) + b_ref[...]
        _gather8(mloc_ref, m8_ref, *sems[3:6])
        ex.wait([win_ref], [wing_ref], sems[6:])

    vm = pl.BlockSpec(memory_space=pltpu.VMEM)
    wcols = wm.shape[1]
    return pl.pallas_call(
        body, name="prologue", in_specs=[vm, vm, vm, vm, ANY], out_specs=[vm, vm, ANY],
        out_shape=[jax.ShapeDtypeStruct((NDEV, 16, D), F32), jax.ShapeDtypeStruct((NDEV, 16, wcols), F32)] + ex.out_shape,
        scratch_shapes=[pltpu.VMEM((16, wcols), F32)] + _gather8_sems() + _gather8_sems() + ex.scratch,
        compiler_params=pltpu.CompilerParams(vmem_limit_bytes=VMEM_LIMIT),
    )(small, c_ctx_rows, wm, bm, w_in_shard)


def _chip_peers(mx, my):
    out = []
    for p in range(1, NCHIP):
        px, py = _flip(mx, p & 2), _flip(my, p & 1)
        out.append((px, py, 2 * px + py))
    return out


class ChipExchange:
    def __init__(self, kind, arrays):
        self.kind = kind
        self.n = len(arrays)
        if kind == "gather":
            self.out_shape = [jax.ShapeDtypeStruct((NCHIP,) + a.shape, a.dtype) for a in arrays]
        else:
            self.out_shape = [jax.ShapeDtypeStruct(a.shape, a.dtype) for a in arrays]
        self.scratch = [pltpu.SemaphoreType.DMA((3 * self.n,)), pltpu.SemaphoreType.DMA((3 * self.n,)),
                        pltpu.SemaphoreType.DMA((self.n,))]

    def _copies(self, ins, outs, sems):
        ssem, rsem, lsem = sems
        mx, my, mc = _me()
        jme = 2 * mx + my
        gather = self.kind == "gather"
        local, sends, waits = [], [], []
        for k in range(self.n):
            local.append(pltpu.make_async_copy(ins[k] if gather else ins[k].at[jme], outs[k].at[jme], lsem.at[k]))
            for p, (px, py, jp) in enumerate(_chip_peers(mx, my)):
                src = ins[k] if gather else ins[k].at[jp]
                sem = dict(send_sem=ssem.at[3 * k + p], recv_sem=rsem.at[3 * k + p], device_id=(px, py, mc),
                           device_id_type=MESH)
                sends.append(pltpu.make_async_remote_copy(src_ref=src, dst_ref=outs[k].at[jme], **sem))
                waits.append(pltpu.make_async_remote_copy(src_ref=src, dst_ref=outs[k].at[jp], **sem))
        return local, sends, waits

    def start(self, ins, outs, sems):
        local, sends, _ = self._copies(ins, outs, sems)
        for cp in local + sends:
            cp.start()

    def wait(self, ins, outs, sems):
        local, _, waits = self._copies(ins, outs, sems)
        for cp in waits:
            cp.wait_recv()
        for cp in waits:
            cp.wait_send()
        for cp in local:
            cp.wait()


def chip_exchange(kind, arrays, name):
    ex = ChipExchange(kind, arrays)
    n = ex.n

    def body(*refs):
        ins, outs, sems = refs[:n], refs[n:2 * n], refs[2 * n:]
        ex.start(ins, outs, sems)
        ex.wait(ins, outs, sems)

    return pl.pallas_call(body, name=name, in_specs=[ANY] * n, out_specs=[ANY] * n, out_shape=ex.out_shape,
                          scratch_shapes=ex.scratch)(*arrays)


def sibling_add(g, ngrp, hr, tr, name):
    c_ = g.shape[1]
    nt = hr // tr

    def body(cidx, keep_ref, give_ref, o_ref, land, ssem, rsem):
        mx, my, mc = _me()
        t = pl.program_id(0) * nt + pl.program_id(1)
        s = t % 2
        cp = pltpu.make_async_remote_copy(src_ref=give_ref, dst_ref=land.at[s], send_sem=ssem.at[s], recv_sem=rsem.at[s],
                                          device_id=(mx, my, 1 - mc), device_id_type=MESH)
        cp.start()
        cp.wait_recv()
        o_ref[...] = keep_ref[...] + land[s]
        cp.wait_send()

    grid_spec = pltpu.PrefetchScalarGridSpec(
        num_scalar_prefetch=1, grid=(ngrp, nt),
        in_specs=[pl.BlockSpec((tr, c_), lambda i, j, cr: ((2 * i + cr[0]) * nt + j, 0)),
                  pl.BlockSpec((tr, c_), lambda i, j, cr: ((2 * i + 1 - cr[0]) * nt + j, 0))],
        out_specs=pl.BlockSpec((tr, c_), lambda i, j, cr: (i * nt + j, 0)),
        scratch_shapes=[pltpu.VMEM((2, tr, c_), F32), pltpu.SemaphoreType.DMA((2,)), pltpu.SemaphoreType.DMA((2,))])
    cidx = lax.axis_index("c").astype(jnp.int32).reshape(1)
    return pl.pallas_call(body, grid_spec=grid_spec, name=name, out_shape=jax.ShapeDtypeStruct((ngrp * hr, c_), F32),
                          compiler_params=_cparams(("arbitrary", "arbitrary")))(cidx, g, g)


def finish_grad(b, tr, name):
    _, r2, c_ = b.shape

    def body(b_ref, g_ref, mine, land, ssem, rsem):
        mx, my, mc = _me()
        t = pl.program_id(0)
        s = t % 2
        mine[s] = (b_ref[0].astype(F32) + b_ref[1].astype(F32)) + (b_ref[2].astype(F32) + b_ref[3].astype(F32))
        cp = pltpu.make_async_remote_copy(src_ref=mine.at[s], dst_ref=land.at[s], send_sem=ssem.at[s], recv_sem=rsem.at[s],
                                          device_id=(mx, my, 1 - mc), device_id_type=MESH)
        cp.start()
        cp.wait_recv()
        g_ref[mc] = mine[s]
        g_ref[1 - mc] = land[s]
        cp.wait_send()

    return pl.pallas_call(
        body, grid=(r2 // tr,), name=name,
        in_specs=[pl.BlockSpec((NCHIP, tr, c_), lambda i: (0, i, 0))],
        out_specs=pl.BlockSpec((2, tr, c_), lambda i: (0, i, 0)), out_shape=jax.ShapeDtypeStruct((2, r2, c_), F32),
        scratch_shapes=[pltpu.VMEM((2, tr, c_), F32), pltpu.VMEM((2, tr, c_), F32), pltpu.SemaphoreType.DMA((2,)),
                        pltpu.SemaphoreType.DMA((2,))],
        compiler_params=_cparams(("arbitrary",)))(b)


TM_IN = 256
TM_GLA = 512
TM_MERGE = 512
TM_FFN = 256
TN_WGRAD = 2048

WEIGHTS = ['c_ctx', 'w_mod', 'b_mod', 'norm1_g', 'norm2_g', 'w_in', 'conv_w', 'conv_b', 'conv_ln_g', 'conv_ln_b', 'w_a2_f',
           'b_a_f', 'w_a2_b', 'b_a_b', 'gla_norm_g', 'w_out', 'w_gate', 'w_up', 'w_down', 'final_g']
BIG = ['w_in', 'w_out', 'w_gate', 'w_up', 'w_down']


def _rows(*vs):
    w = vs[0].size
    row = lax.broadcasted_iota(jnp.int32, (8, w), 0)
    out = jnp.zeros((8, w), F32)
    for i, v in enumerate(vs):
        out = jnp.where(row == i, v.reshape(1, w), out)
    return out


def _small_slab(p):
    cat = lambda *ks: jnp.concatenate([p[k].reshape(-1) for k in ks])
    vecs = _rows(p['c_ctx'], p['norm1_g'], p['norm2_g'], p['final_g'], cat('conv_b', 'conv_ln_g'),
                 cat('conv_ln_b', 'b_a_f', 'b_a_b'), jnp.pad(p['gla_norm_g'].reshape(-1), (0, D - HV)))
    bmod = jnp.pad(p['b_mod'].reshape(6, D), ((0, 2), (0, 0)))
    shards = jnp.pad(jnp.concatenate([jnp.pad(p['conv_w'].reshape(-1), (0, DC // NCHIP)), cat('w_a2_f', 'w_a2_b')]),
                     (0, 2 * D)).reshape(8, D)
    return jnp.concatenate([vecs, bmod, shards], axis=0)


def _unslab(s):
    return {
        'c_ctx': s[0], 'norm1_g': s[1:2], 'norm2_g': s[2:3], 'final_g': s[3],
        'conv_b': s[4:5, :DC], 'conv_ln_g': s[4:5, DC:], 'conv_ln_b': s[5:6, :DC],
        'b_a_f': s[5:6, DC:DC + DK], 'b_a_b': s[5:6, DC + DK:], 'gla_norm_g': s[6:7, :HV],
        'b_mod': s[8:14].reshape(1, 6 * D),
        'conv_w': s[16:20].reshape(32, DC // NCHIP)[:CW].reshape(1, CW, DC // NCHIP),
        'w_a2_f': s[20].reshape(1, RANK, DK // NCHIP), 'w_a2_b': s[21].reshape(1, RANK, DK // NCHIP),
    }


def kernel(x, c, ctx, c_ctx, w_mod, b_mod, norm1_g, norm2_g, w_in, conv_w, conv_b, conv_ln_g, conv_ln_b, w_a2_f, b_a_f, w_a2_b, b_a_b, gla_norm_g, w_out, w_gate, w_up, w_down, final_g, loss_target, m_c_ctx, m_w_mod, m_b_mod, m_norm1_g, m_norm2_g, m_w_in, m_conv_w, m_conv_b, m_conv_ln_g, m_conv_ln_b, m_w_a2_f, m_b_a_f, m_w_a2_b, m_b_a_b, m_gla_norm_g, m_w_out, m_w_gate, m_w_up, m_w_down, m_final_g, v_c_ctx, v_w_mod, v_b_mod, v_norm1_g, v_norm2_g, v_w_in, v_conv_w, v_conv_b, v_conv_ln_g, v_conv_ln_b, v_w_a2_f, v_b_a_f, v_w_a2_b, v_b_a_b, v_gla_norm_g, v_w_out, v_w_gate, v_w_up, v_w_down, v_final_g):
    w = dict(c_ctx=c_ctx, w_mod=w_mod, b_mod=b_mod, norm1_g=norm1_g, norm2_g=norm2_g, w_in=w_in, conv_w=conv_w, conv_b=conv_b,
             conv_ln_g=conv_ln_g, conv_ln_b=conv_ln_b, w_a2_f=w_a2_f, b_a_f=b_a_f, w_a2_b=w_a2_b, b_a_b=b_a_b,
             gla_norm_g=gla_norm_g, w_out=w_out, w_gate=w_gate, w_up=w_up, w_down=w_down, final_g=final_g)
    m = dict(c_ctx=m_c_ctx, w_mod=m_w_mod, b_mod=m_b_mod, norm1_g=m_norm1_g, norm2_g=m_norm2_g, w_in=m_w_in, conv_w=m_conv_w,
             conv_b=m_conv_b, conv_ln_g=m_conv_ln_g, conv_ln_b=m_conv_ln_b, w_a2_f=m_w_a2_f, b_a_f=m_b_a_f, w_a2_b=m_w_a2_b,
             b_a_b=m_b_a_b, gla_norm_g=m_gla_norm_g, w_out=m_w_out, w_gate=m_w_gate, w_up=m_w_up, w_down=m_w_down,
             final_g=m_final_g)
    v = dict(c_ctx=v_c_ctx, w_mod=v_w_mod, b_mod=v_b_mod, norm1_g=v_norm1_g, norm2_g=v_norm2_g, w_in=v_w_in, conv_w=v_conv_w,
             conv_b=v_conv_b, conv_ln_g=v_conv_ln_g, conv_ln_b=v_conv_ln_b, w_a2_f=v_w_a2_f, b_a_f=v_b_a_f, w_a2_b=v_w_a2_b,
             b_a_b=v_b_a_b, gla_norm_g=v_gla_norm_g, w_out=v_w_out, w_gate=v_w_gate, w_up=v_w_up, w_down=v_w_down,
             final_g=v_final_g)
    mx, my, mc = _me()
    jme = 2 * mx + my
    me = 4 * mx + 2 * my + mc
    wmc = D * 6 // NCHIP
    xx, tgt, cx = x[0], loss_target[0], ctx[0]
    n = xx.shape[0]

    bshard = [w[k][0].astype(BF) for k in BIG]
    sw = jnp.concatenate([jnp.pad(conv_w[0], ((0, 1), (0, 0))).reshape(-1), w_a2_f[0].reshape(-1), w_a2_b[0].reshape(-1)])
    small = jnp.concatenate([_rows(c[0]), jnp.pad(sw.reshape(6, D), ((0, 2), (0, 0)))], axis=0)
    cs8, mall, win_g = prologue(small, _rows(c_ctx), w_mod[0], lax.dynamic_slice_in_dim(b_mod, jme * wmc, wmc, axis=1),
                                bshard[0])
    cext = jnp.concatenate([cs8[:, 0, :], _rows(c_ctx)], axis=0)
    swc = jnp.stack([cs8[2 * j, 8:16] for j in range(NCHIP)]).reshape(NCHIP, 8 * D)
    convw = jnp.transpose(swc[:, :32 * 128].reshape(NCHIP, 32, 128), (1, 0, 2)).reshape(32, DC)
    a2 = lambda o: jnp.transpose(swc[:, o:o + RANK * 64].reshape(NCHIP, RANK, 64), (1, 0, 2)).reshape(RANK, DK)
    wa2 = jnp.zeros((128, 2 * DK), F32).at[0:RANK, 0:DK].set(a2(32 * 128)).at[RANK:2 * RANK, DK:].set(a2(32 * 128 + RANK * 64))
    wa2 = wa2.astype(BF)
    mall = jnp.concatenate([mall[2 * j] for j in range(NCHIP)], axis=1)
    sh1, sc1, g1, sh2, sc2, g2 = jnp.split(lax.dynamic_slice_in_dim(mall, me, 1, axis=0)[0], 6)
    csh1, csc1 = mall[8, :D], mall[8, D:2 * D]
    cols = lambda a: jnp.transpose(a, (1, 0, 2)).reshape(a.shape[1], -1)
    win = jnp.pad(cols(win_g), ((0, 0), (0, DINP - DIN)))
    ba = jnp.concatenate([b_a_f, b_a_b], axis=1)
    cvec = _rows(conv_b, conv_ln_g, conv_ln_b)
    vec1 = _rows(norm1_g, sh1, sc1)
    vecc = _rows(norm1_g, csh1, csc1)
    vecm = _rows(g1)
    vecf = _rows(norm2_g, sh2, sc2, g2, final_g)
    gn = jnp.tile(gla_norm_g, (1, NH))

    s0 = ctx_fwd(cx, vecc, win, wa2, ba)
    res = fwd_in(xx, vec1, win, convw, cvec, wa2, ba, TM_IN, ChipExchange("gather", bshard[1:]), bshard[1:])
    ag, yb, co, qk, vv, gg, la, r = res[:8]
    wout = res[8].reshape(D, D)
    wg, wu = cols(res[9]), cols(res[10])
    wd = res[11].reshape(DFF, D)
    o_f, o_b, se_f, se_b = gla_fwd(qk, vv, la, s0, TM_GLA)
    x1, y1, cat = merge_fwd(xx, o_f, o_b, gg, co, vecm, gn, wout, TM_MERGE)

    dx1, h2, act, dgt, dup, dy2, sf = ffn_fwd_bwd(x1, tgt, vecf, wg, wu, wd, TM_FFN)
    d_wg = wgrad(h2, dgt, None, D, DFF // 2, TN_WGRAD, "wgrad_gate")
    d_wu = wgrad(h2, dup, None, D, DFF // 2, TN_WGRAD, "wgrad_up")
    d_wd = wgrad(act, dy2, None, DFF // 2, D, TN_WGRAD, "wgrad_down")
    dy1, dco, do, dg, s1, s2 = merge_bwd(dx1, y1, o_f, o_b, gg, vecm, gn, wout, TM_MERGE)
    d_wout = wgrad(cat, dy1, None, D, D, TN_WGRAD, "wgrad_out")

    shard = lambda a, k: jnp.transpose(a.reshape(a.shape[0], NCHIP, k), (1, 0, 2))
    hd = D // 2
    parts = [sibling_add(d_wout, NCHIP, hd // NCHIP, hd // NCHIP, "xadd_w_out").reshape(NCHIP, hd // NCHIP, D),
             shard(sibling_add(d_wg, 1, hd, 128, "xadd_w_gate"), DFF // NCHIP),
             shard(sibling_add(d_wu, 1, hd, 128, "xadd_w_up"), DFF // NCHIP),
             sibling_add(d_wd, NCHIP, DFF // 8, DFF // 16, "xadd_w_down").reshape(NCHIP, DFF // 8, D)]
    parts = [p.astype(BF) for p in parts]
    res = gla_bwd(qk, vv, la, do, se_f, se_b, TM_GLA, ChipExchange("scatter", parts), parts)
    dqk_f, dv_f, dla_f, dqk_b, dv_b, dla_b, ds0 = res[:7]
    recv = list(res[7:])
    dwin_c, dwa2_c, sc = ctx_bwd(cx, vecc, win, wa2, ba, ds0)
    grad_x, h, dp, dwa2, dcw, sd = bwd_in(xx, dx1, ag, yb, dco, dqk_f, dqk_b, dv_f, dv_b, dg, dla_f, dla_b, la, r,
                                          vec1, win, convw, cvec, wa2, TM_IN)
    d_win = wgrad(h, dp, dwin_c, D, DINP // 3, TN_WGRAD, "wgrad_in")
    part_in = shard(sibling_add(d_win, 1, hd, 128, "xadd_w_in")[:, :DIN], DIN // NCHIP).astype(BF)
    recv = list(chip_exchange("scatter", [part_in], "scatter_w_in")) + recv

    rows16, dcw_t, dwa2_t = pack_small(sf, s1, s2, sd, sc, dcw, dwa2, dwa2_c)
    sp = jnp.concatenate([rows16, dcw_t.reshape(16, D), dwa2_t.reshape(16, D)], axis=0)
    g8 = all_gather8(sp, "gather_small_grads")
    tot, bm_g, loss8 = small_totals(g8)
    loss = loss8[0, 0]
    dmod8 = g8[:, 0:6, :].reshape(NDEV, 6 * D)
    dmodc = jnp.concatenate([tot[6], tot[7], jnp.zeros((4 * D,), F32)])
    dm = jnp.concatenate([dmod8, _rows(dmodc)], axis=0)
    dm = lax.dynamic_slice_in_dim(dm, jme * wmc, wmc, axis=1)
    g_wmod, dsil = mod_bwd(cext, dm, w_mod[0])
    p8 = all_gather8(dsil[8:16], "gather_dsilu")
    g_cctx = cctx_grad(p8, _rows(c_ctx))[0]

    grads, delta, new_m, new_v = {}, {}, {}, {}
    for i, k in enumerate(BIG):
        r2 = recv[i].shape[1]
        gk = finish_grad(recv[i], 176 if r2 % 128 else 128, "finish_" + k).reshape(w[k].shape[1:])
        outs = adamw(w[k][0], gk, m[k][0], v[k][0], 88 if gk.shape[0] % 128 else 128, "adamw_" + k, emit_grad=True)
        delta[k], new_m[k], new_v[k], grads[k] = (o[None] for o in outs)
    grads['w_mod'] = g_wmod[None]
    d_, m_, v_ = adamw(w_mod[0], g_wmod, m_w_mod[0], v_w_mod[0], 128, "adamw_w_mod")
    delta['w_mod'], new_m['w_mod'], new_v['w_mod'] = d_[None], m_[None], v_[None]
    small_g = {
        'c_ctx': g_cctx, 'b_mod': bm_g[0:6].reshape(1, 6 * D), 'norm1_g': tot[8:9], 'norm2_g': tot[9:10], 'final_g': tot[10],
        'conv_b': tot[11:12, :DC], 'conv_ln_g': tot[11:12, DC:], 'conv_ln_b': tot[12:13, :DC],
        'b_a_f': tot[12:13, DC:DC + DK], 'b_a_b': tot[12:13, DC + DK:], 'gla_norm_g': tot[13:14, :HV],
        'conv_w': lax.dynamic_slice_in_dim(tot[16:32].reshape(32, DC)[:CW], jme * (DC // NCHIP), DC // NCHIP, axis=1)[None],
        'w_a2_f': lax.dynamic_slice_in_dim(tot[32:48].reshape(32, 2 * DK)[0:RANK, 0:DK], jme * (DK // NCHIP), DK // NCHIP, axis=1)[None],
        'w_a2_b': lax.dynamic_slice_in_dim(tot[32:48].reshape(32, 2 * DK)[RANK:2 * RANK, DK:], jme * (DK // NCHIP), DK // NCHIP, axis=1)[None],
    }
    grads.update(small_g)
    sd_, sm_, sv_ = adamw(_small_slab(w), _small_slab(small_g), _small_slab(m), _small_slab(v), 24,
                          "adamw_small")
    for dst, slab in ((delta, sd_), (new_m, sm_), (new_v, sv_)):
        dst.update(_unslab(slab))
    out = [loss, grad_x[None]]
    for group in (grads, delta, new_m, new_v):
        out += [group[k].reshape(w[k].shape) for k in WEIGHTS]
    return tuple(out)
```

```python
import functools

import jax
import jax.numpy as jnp
from jax import lax
from jax.experimental import pallas as pl
from jax.experimental.pallas import tpu as pltpu

F32 = jnp.float32
BF = jnp.bfloat16

D = 1024
DC = 512
NH = 4
HK = 64
HV = 128
DK = NH * HK
DV = NH * HV
RANK = 16
CH = 64
GW = 64
CW = 31
CPAD = CW // 2
SEGP = GW + 32
DFF = 2816
DIN = 2592
DINP = 2688
EPS = 1e-6
TAU = 16.0
QSCALE = HK ** -0.5
NCHIP = 4
NDEV = 8

ADAM_LR = 0.001
ADAM_B1 = 0.9
ADAM_B2 = 0.999
ADAM_EPS = 1e-08
ADAM_WD = 0.01
ADAM_STEP = 10

VMEM_LIMIT = 56 * 1024 * 1024
MESH = pl.DeviceIdType.MESH


def _dot(a, b):
    return jnp.dot(a, b, preferred_element_type=F32)


def _dot_nt(a, b):
    return lax.dot_general(a, b, (((1,), (1,)), ((), ())), preferred_element_type=F32)


def _dot_tn(a, b):
    return lax.dot_general(a, b, (((0,), (0,)), ((), ())), preferred_element_type=F32)


def _split3(x):
    hi = x.astype(BF)
    r1 = x - hi.astype(F32)
    mid = r1.astype(BF)
    lo = (r1 - mid.astype(F32)).astype(BF)
    return hi, mid, lo


def _mask_dot(t, x):
    hi, mid, lo = _split3(x)
    return _dot(t, hi) + _dot(t, mid) + _dot(t, lo)


def _sigmoid(x):
    return 1.0 / (1.0 + jnp.exp(-x))


def _log_sigmoid(x):
    return jnp.minimum(x, 0.0) - jnp.log(1.0 + jnp.exp(-jnp.abs(x)))


def _colsum8(z):
    t, c = z.shape
    return jnp.sum(z.reshape(t // 8, 8, c), axis=0)


def _tri(n, kind):
    r = lax.broadcasted_iota(jnp.int32, (n, n), 0)
    c = lax.broadcasted_iota(jnp.int32, (n, n), 1)
    m = {"le": c <= r, "lt": c < r, "ge": c >= r, "gt": c > r}[kind]
    return m


def _full(shape):
    nd = len(shape)
    return pl.BlockSpec(shape, lambda *_: (0,) * nd)


def _cparams(sem, vmem=VMEM_LIMIT):
    return pltpu.CompilerParams(dimension_semantics=sem, vmem_limit_bytes=vmem)


def _call(body, grid, name, in_specs, out_specs, out_shape, scratch, operands, exchange=None, carried=()):
    n_in, n_out, n_scr = len(in_specs), len(out_specs), len(scratch)
    if exchange is None:
        fn = body
    else:
        n = exchange.n

        def fn(*refs):
            ins, cin = refs[:n_in], refs[n_in:n_in + n]
            outs, cout = refs[n_in + n:n_in + n + n_out], refs[n_in + n + n_out:n_in + 2 * n + n_out]
            rest = refs[n_in + 2 * n + n_out:]
            scr, sems = rest[:n_scr], rest[n_scr:]

            @pl.when(pl.program_id(0) == 0)
            def _():
                exchange.start(cin, cout, sems)

            body(*ins, *outs, *scr)

            @pl.when(pl.program_id(0) == pl.num_programs(0) - 1)
            def _():
                exchange.wait(cin, cout, sems)

        any_spec = pl.BlockSpec(memory_space=pl.ANY)
        in_specs = list(in_specs) + [any_spec] * n
        out_specs = list(out_specs) + [any_spec] * n
        out_shape = list(out_shape) + exchange.out_shape
        scratch = list(scratch) + exchange.scratch
    return pl.pallas_call(fn, grid=grid, name=name, in_specs=in_specs, out_specs=out_specs, out_shape=out_shape,
                          scratch_shapes=scratch, compiler_params=_cparams(("arbitrary",)))(*operands, *carried)


def _fill_padded(pad_ref, val, nseg):
    zeros = jnp.zeros((nseg, 16, val.shape[-1]), F32)
    pad_ref[:, 0:16, :] = zeros
    pad_ref[:, 16 + GW:SEGP, :] = zeros
    pad_ref[:, 16:16 + GW, :] = val.reshape(nseg, GW, val.shape[-1])


def _tap_slabs(pad_ref, s, cs):
    whole = pad_ref[s, :, cs]
    for r in range(8):
        slab = whole if r == 0 else pltpu.roll(whole, SEGP - r, axis=0)
        for a in range(4):
            j = r + 8 * a - 1
            if 0 <= j < CW:
                yield j, slab[8 * a:8 * a + GW]


def _conv_taps(pad_ref, s, w_ref, c0, cw, flip):
    acc = jnp.zeros((GW, cw), F32)
    for j, rows in _tap_slabs(pad_ref, s, pl.ds(c0, cw)):
        acc = acc + w_ref[pl.ds((CW - 1 - j) if flip else j, 1), pl.ds(c0, cw)] * rows
    return acc


def _ln_stats(yb):
    mu = jnp.mean(yb, axis=-1, keepdims=True)
    yc = yb - mu
    var = jnp.mean(yc * yc, axis=-1, keepdims=True)
    rs = lax.rsqrt(var + EPS)
    return yc * rs, rs


def fwd_in(x, vec1, win, convw, cvec, wa2, ba, tm, exchange=None, carried=()):
    n = x.shape[0]
    nseg = tm // GW
    cg = 128

    def body(x_ref, vec_ref, win_ref, cw_ref, cv_ref, wa2_ref, ba_ref,
             ag_ref, yb_ref, co_ref, qk_ref, v_ref, g_ref, la_ref, r_ref, pad_ref):
        xx = x_ref[...]
        rstd = lax.rsqrt(jnp.mean(xx * xx, axis=-1, keepdims=True) + EPS)
        h = ((xx * rstd * vec_ref[0:1, :]) * (1.0 + vec_ref[2:3, :]) + vec_ref[1:2, :]).astype(BF)
        pc = _dot(h, win_ref[:, :2 * DC])
        ag_ref[...] = pc.astype(BF)
        _fill_padded(pad_ref, pc[:, :DC] * _sigmoid(pc[:, DC:]), nseg)
        for s in range(nseg):
            for c0 in range(0, DC, cg):
                y = _conv_taps(pad_ref, s, cw_ref, c0, cg, False)
                yb_ref[pl.ds(s * GW, GW), pl.ds(c0, cg)] = y + cv_ref[0:1, c0:c0 + cg]
        p = _dot(h, win_ref[:, 2 * DC:])
        qk_ref[...] = p[:, :2 * DK].astype(BF)
        v_ref[...] = p[:, 2 * DK:2 * DK + DV].astype(BF)
        g_ref[...] = p[:, 2 * DK + DV:2 * DK + 2 * DV].astype(BF)
        r = p[:, 2 * DK + 2 * DV:].astype(BF)
        r_ref[...] = r
        la_ref[...] = _log_sigmoid(_dot(r, wa2_ref[...]) + ba_ref[...]) * (1.0 / TAU)
        yn, _ = _ln_stats(yb_ref[...])
        ln = yn * cv_ref[1:2, :] + cv_ref[2:3, :]
        co_ref[...] = (ln * _sigmoid(ln)).astype(BF)

    tok = lambda w: pl.BlockSpec((tm, w), lambda i: (i, 0))
    return _call(
        body, (n // tm,), "fwd_in",
        [tok(D), _full(vec1.shape), _full(win.shape), _full(convw.shape), _full(cvec.shape), _full(wa2.shape), _full(ba.shape)],
        [tok(2 * DC), tok(DC), tok(DC), tok(2 * DK), tok(DV), tok(DV), tok(2 * DK), tok(128)],
        [jax.ShapeDtypeStruct((n, 2 * DC), BF), jax.ShapeDtypeStruct((n, DC), F32),
         jax.ShapeDtypeStruct((n, DC), BF), jax.ShapeDtypeStruct((n, 2 * DK), BF),
         jax.ShapeDtypeStruct((n, DV), BF), jax.ShapeDtypeStruct((n, DV), BF),
         jax.ShapeDtypeStruct((n, 2 * DK), F32), jax.ShapeDtypeStruct((n, 128), BF)],
        [pltpu.VMEM((nseg, SEGP, DC), F32)],
        (x, vec1, win, convw, cvec, wa2, ba), exchange, carried)


def _gla_dir(d):
    return (_tri(CH, "le"), CH - 1) if d == 0 else (_tri(CH, "ge"), 0)


def _gla_chunk_terms(qk, la, d):
    seen, last = _gla_dir(d)
    b = _mask_dot(seen.astype(BF), la)
    bl = b[last:last + 1, :]
    eb = jnp.exp(b)
    enb = jnp.exp(-b)
    ekd = jnp.exp(bl - b)
    ebl = jnp.exp(bl)
    q = qk[:, :DK].astype(F32) * QSCALE
    k = qk[:, DK:].astype(F32)
    return eb, enb, ekd, ebl, q * eb, k * enb, k * ekd


NP = NH // 2
PW = 2 * HK


def _lo_lanes(shape):
    return lax.broadcasted_iota(jnp.int32, shape, len(shape) - 1) < HK


def _pair_sel(lo, hi):
    return jnp.where(_lo_lanes(lo.shape), lo, hi)


def _only(x, which):
    keep = _lo_lanes(x.shape) if which == 0 else jnp.logical_not(_lo_lanes(x.shape))
    return jnp.where(keep, x, jnp.zeros_like(x))


def gla_fwd(qk, v, la, s0, tm):
    n = qk.shape[0]
    nt = n // tm
    nc = tm // CH

    def body(qkf_ref, vf_ref, laf_ref, qkb_ref, vb_ref, lab_ref, s0_ref, of_ref, ob_ref, sef_ref, seb_ref, st_ref):
        @pl.when(pl.program_id(0) == 0)
        def _():
            st_ref[...] = s0_ref[...]

        def chunk(ci, carry):
            t = []
            for d, (qk_ref, v_ref, la_ref) in enumerate(((qkf_ref, vf_ref, laf_ref), (qkb_ref, vb_ref, lab_ref))):
                c = ci if d == 0 else nc - 1 - ci
                rows = pl.ds(pl.multiple_of(c * CH, CH), CH)
                eb, enb, ekd, ebl, qt, kt, kd = _gla_chunk_terms(qk_ref[rows, :], la_ref[rows, :], d)
                t.append(dict(c=c, rows=rows, ebl=ebl, qt=qt.astype(BF), kt=kt.astype(BF), kd=kd.astype(BF),
                              vv=v_ref[rows, :], st=[st_ref[d, p] for p in range(NP)], amask=_gla_dir(d)[0]))
            dh = [(d, h) for d in range(2) for h in range(NH)]
            ps = lambda h: slice((h // 2) * PW, (h // 2 + 1) * PW)
            vs = lambda h: slice(h * HV, (h + 1) * HV)
            qm = {(d, h): _only(t[d]['qt'][:, ps(h)], h % 2) for d, h in dh}
            a = {(d, h): jnp.where(t[d]['amask'], _dot_nt(qm[d, h], t[d]['kt'][:, ps(h)]), 0.0).astype(BF) for d, h in dh}
            o = {(d, h): _dot(a[d, h], t[d]['vv'][:, vs(h)]) + _dot_nt(qm[d, h], t[d]['st'][h // 2].astype(BF))
                 for d, h in dh}
            kv = {(d, h): _dot_tn(t[d]['vv'][:, vs(h)], t[d]['kd'][:, ps(h)]) for d, h in dh}
            for d, (o_ref, se_ref) in enumerate(((of_ref, sef_ref), (ob_ref, seb_ref))):
                for h in range(NH):
                    o_ref[t[d]['rows'], vs(h)] = o[d, h].astype(BF)
                for p in range(NP):
                    se_ref[t[d]['c'], p] = t[d]['st'][p]
                    st_ref[d, p] = (t[d]['ebl'][:, p * PW:(p + 1) * PW] * t[d]['st'][p]
                                    + _pair_sel(kv[d, 2 * p], kv[d, 2 * p + 1]))
            return carry

        lax.fori_loop(0, nc, chunk, 0, unroll=4)

    fw = lambda w, col=0: pl.BlockSpec((tm, w), lambda i: (i, col))
    bw = lambda w, col=0: pl.BlockSpec((tm, w), lambda i: (nt - 1 - i, col))
    se_f = pl.BlockSpec((nc, NP, HV, PW), lambda i: (i, 0, 0, 0))
    se_b = pl.BlockSpec((nc, NP, HV, PW), lambda i: (nt - 1 - i, 0, 0, 0))
    se_shape = jax.ShapeDtypeStruct((n // CH, NP, HV, PW), F32)
    return pl.pallas_call(
        body, grid=(nt,), name="gla_fwd",
        in_specs=[fw(2 * DK), fw(DV), fw(DK, 0), bw(2 * DK), bw(DV), bw(DK, 1), _full(s0.shape)],
        out_specs=[fw(DV), bw(DV), se_f, se_b],
        out_shape=[jax.ShapeDtypeStruct((n, DV), BF), jax.ShapeDtypeStruct((n, DV), BF), se_shape, se_shape],
        scratch_shapes=[pltpu.VMEM((2, NP, HV, PW), F32)],
        compiler_params=_cparams(("arbitrary",)),
    )(qk, v, la, qk, v, la, s0)


def gla_bwd(qk, v, la, do, se_f, se_b, tm, exchange=None, carried=()):
    n = qk.shape[0]
    nt = n // tm
    nc = tm // CH

    def body(qkf_ref, vf_ref, laf_ref, dof_ref, sef_ref, qkb_ref, vb_ref, lab_ref, dob_ref, seb_ref,
             dqkf_ref, dvf_ref, dlaf_ref, dqkb_ref, dvb_ref, dlab_ref, ds0_ref, ds_ref):
        @pl.when(pl.program_id(0) == 0)
        def _():
            ds_ref[...] = jnp.zeros_like(ds_ref)

        def chunk(ci, carry):
            t = []
            for d, (qk_ref, v_ref, la_ref, do_ref, se_ref) in enumerate(
                    ((qkf_ref, vf_ref, laf_ref, dof_ref, sef_ref), (qkb_ref, vb_ref, lab_ref, dob_ref, seb_ref))):
                c = nc - 1 - ci if d == 0 else ci
                rows = pl.ds(pl.multiple_of(c * CH, CH), CH)
                amask, last = _gla_dir(d)
                eb, enb, ekd, ebl, qt, kt, kd = _gla_chunk_terms(qk_ref[rows, :], la_ref[rows, :], d)
                t.append(dict(rows=rows, amask=amask, last=last, eb=eb, enb=enb, ekd=ekd, ebl=ebl, qt=qt, kt=kt, kd=kd,
                              qtb=qt.astype(BF), ktb=kt.astype(BF), kdb=kd.astype(BF), vv=v_ref[rows, :], dd=do_ref[rows, :],
                              st=[se_ref[c, p] for p in range(NP)], dsn=[ds_ref[d, p] for p in range(NP)]))
            dh = [(d, h) for d in range(2) for h in range(NH)]
            dp = [(d, p) for d in range(2) for p in range(NP)]
            ps = lambda h: slice((h // 2) * PW, (h // 2 + 1) * PW)
            vs = lambda h: slice(h * HV, (h + 1) * HV)
            stb = {(d, p): t[d]['st'][p].astype(BF) for d, p in dp}
            dsnb = {(d, p): t[d]['dsn'][p].astype(BF) for d, p in dp}
            qm = {(d, h): _only(t[d]['qtb'][:, ps(h)], h % 2) for d, h in dh}
            km = {(d, h): _only(t[d]['kdb'][:, ps(h)], h % 2) for d, h in dh}
            a = {(d, h): jnp.where(t[d]['amask'], _dot_nt(qm[d, h], t[d]['ktb'][:, ps(h)]), 0.0).astype(BF) for d, h in dh}
            da = {(d, h): jnp.where(t[d]['amask'], _dot_nt(t[d]['dd'][:, vs(h)], t[d]['vv'][:, vs(h)]), 0.0).astype(BF)
                  for d, h in dh}
            dv = {(d, h): _dot_tn(a[d, h], t[d]['dd'][:, vs(h)]) + _dot_nt(km[d, h], dsnb[d, h // 2]) for d, h in dh}
            dkd = {(d, h): _dot(t[d]['vv'][:, vs(h)], dsnb[d, h // 2]) for d, h in dh}
            dqt = {(d, h): _dot(da[d, h], t[d]['ktb'][:, ps(h)]) + _dot(t[d]['dd'][:, vs(h)], stb[d, h // 2]) for d, h in dh}
            dkt = {(d, h): _dot_tn(da[d, h], t[d]['qtb'][:, ps(h)]) for d, h in dh}
            dsq = {(d, h): _dot_tn(t[d]['dd'][:, vs(h)], t[d]['qtb'][:, ps(h)]) for d, h in dh}
            for d, (dqk_ref, dv_ref, dla_ref) in enumerate(((dqkf_ref, dvf_ref, dlaf_ref), (dqkb_ref, dvb_ref, dlab_ref))):
                td = t[d]
                rows = td['rows']
                for h in range(NH):
                    dv_ref[rows, vs(h)] = dv[d, h].astype(BF)
                pair = lambda x: jnp.concatenate([_pair_sel(x[d, 2 * p], x[d, 2 * p + 1]) for p in range(NP)], axis=1)
                dqt_, dkt_, dkd_ = pair(dqt), pair(dkt), pair(dkd)
                debl = jnp.concatenate([jnp.sum(td['st'][p] * td['dsn'][p], axis=0, keepdims=True) for p in range(NP)], axis=1)
                for p in range(NP):
                    ds_ref[d, p] = _pair_sel(dsq[d, 2 * p], dsq[d, 2 * p + 1]) + td['ebl'][:, p * PW:(p + 1) * PW] * td['dsn'][p]
                dkdkd = dkd_ * td['kd']
                dbl = jnp.sum(dkdkd, axis=0, keepdims=True) + debl * td['ebl']
                is_last = lax.broadcasted_iota(jnp.int32, (CH, DK), 0) == td['last']
                db = dqt_ * td['qt'] - dkt_ * td['kt'] - dkdkd + jnp.where(is_last, dbl, 0.0)
                dqk_ref[rows, :] = jnp.concatenate([dqt_ * td['eb'] * QSCALE, dkt_ * td['enb'] + dkd_ * td['ekd']], axis=1).astype(BF)
                dla_ref[rows, :] = _mask_dot(_gla_dir(1 - d)[0].astype(BF), db)
            return carry

        lax.fori_loop(0, nc, chunk, 0, unroll=4)

        @pl.when(pl.program_id(0) == nt - 1)
        def _():
            ds0_ref[...] = ds_ref[...]

    up = lambda w, col=0: pl.BlockSpec((tm, w), lambda i: (i, col))
    dn = lambda w, col=0: pl.BlockSpec((tm, w), lambda i: (nt - 1 - i, col))
    se_up = pl.BlockSpec((nc, NP, HV, PW), lambda i: (i, 0, 0, 0))
    se_dn = pl.BlockSpec((nc, NP, HV, PW), lambda i: (nt - 1 - i, 0, 0, 0))
    return _call(
        body, (nt,), "gla_bwd",
        [dn(2 * DK), dn(DV), dn(DK, 0), dn(DV), se_dn, up(2 * DK), up(DV), up(DK, 1), up(DV), se_up],
        [dn(2 * DK), dn(DV), dn(DK), up(2 * DK), up(DV), up(DK), _full((2, NP, HV, PW))],
        [jax.ShapeDtypeStruct((n, 2 * DK), BF), jax.ShapeDtypeStruct((n, DV), BF),
         jax.ShapeDtypeStruct((n, DK), F32), jax.ShapeDtypeStruct((n, 2 * DK), BF),
         jax.ShapeDtypeStruct((n, DV), BF), jax.ShapeDtypeStruct((n, DK), F32),
         jax.ShapeDtypeStruct((2, NP, HV, PW), F32)],
        [pltpu.VMEM((2, NP, HV, PW), F32)],
        (qk, v, la, do, se_f, qk, v, la, do, se_b), exchange, carried)


def _head_norm(o):
    ons, rss = [], []
    for h in range(NH):
        oh = o[:, h * HV:(h + 1) * HV]
        rs = lax.rsqrt(jnp.mean(oh * oh, axis=-1, keepdims=True) + EPS)
        ons.append(oh * rs)
        rss.append(rs)
    return ons, rss


def merge_fwd(x, o_f, o_b, g, co, vecm, gn, wout, tm):
    n = x.shape[0]

    def body(x_ref, of_ref, ob_ref, g_ref, co_ref, vec_ref, gn_ref, w_ref, x1_ref, y1_ref, cat_ref):
        o = of_ref[...].astype(F32) + ob_ref[...].astype(F32)
        ons, _ = _head_norm(o)
        gg = g_ref[...].astype(F32)
        sil = gg * _sigmoid(gg)
        cat_ref[:, :DC] = co_ref[...]
        for h in range(NH):
            vs = slice(h * HV, (h + 1) * HV)
            cat_ref[:, DC + h * HV:DC + (h + 1) * HV] = (ons[h] * gn_ref[:, vs] * sil[:, vs]).astype(BF)
        y1 = _dot(cat_ref[...], w_ref[...])
        y1_ref[...] = y1.astype(BF)
        x1_ref[...] = x_ref[...] + vec_ref[0:1, :] * y1

    tok = lambda w: pl.BlockSpec((tm, w), lambda i: (i, 0))
    return pl.pallas_call(
        body, grid=(n // tm,), name="merge_fwd",
        in_specs=[tok(D), tok(DV), tok(DV), tok(DV), tok(DC), _full(vecm.shape), _full(gn.shape), _full(wout.shape)],
        out_specs=[tok(D), tok(D), tok(D)],
        out_shape=[jax.ShapeDtypeStruct((n, D), F32), jax.ShapeDtypeStruct((n, D), BF), jax.ShapeDtypeStruct((n, D), BF)],
        compiler_params=_cparams(("arbitrary",)),
    )(x, o_f, o_b, g, co, vecm, gn, wout)


def merge_bwd(dx1, y1, o_f, o_b, g, vecm, gn, wout, tm):
    n = dx1.shape[0]

    def body(dx1_ref, y1_ref, of_ref, ob_ref, g_ref, vec_ref, gn_ref, w_ref,
             dy1_ref, dco_ref, do_ref, dg_ref, s1_ref, s2_ref):
        @pl.when(pl.program_id(0) == 0)
        def _():
            s1_ref[...] = jnp.zeros_like(s1_ref)
            s2_ref[...] = jnp.zeros_like(s2_ref)

        dx1 = dx1_ref[...]
        s1_ref[...] += _colsum8(dx1 * y1_ref[...].astype(F32))
        dy1 = (dx1 * vec_ref[0:1, :]).astype(BF)
        dy1_ref[...] = dy1
        dcat = _dot_nt(dy1, w_ref[...])
        dco_ref[...] = dcat[:, :DC].astype(BF)
        o = of_ref[...].astype(F32) + ob_ref[...].astype(F32)
        ons, rss = _head_norm(o)
        gg = g_ref[...].astype(F32)
        sg = _sigmoid(gg)
        sil = gg * sg
        dsil = sg * (1.0 + gg * (1.0 - sg))
        for h in range(NH):
            vs = slice(h * HV, (h + 1) * HV)
            do2 = dcat[:, DC + h * HV:DC + (h + 1) * HV]
            gnh = gn_ref[:, vs]
            t = do2 * sil[:, vs]
            s2_ref[:, vs] += _colsum8(t * ons[h])
            don = t * gnh
            do_ref[:, vs] = (rss[h] * (don - ons[h] * jnp.mean(don * ons[h], axis=-1, keepdims=True))).astype(BF)
            dg_ref[:, vs] = (do2 * ons[h] * gnh * dsil[:, vs]).astype(BF)

    tok = lambda w: pl.BlockSpec((tm, w), lambda i: (i, 0))
    return pl.pallas_call(
        body, grid=(n // tm,), name="merge_bwd",
        in_specs=[tok(D), tok(D), tok(DV), tok(DV), tok(DV), _full(vecm.shape), _full(gn.shape), _full(wout.shape)],
        out_specs=[tok(D), tok(DC), tok(DV), tok(DV), _full((8, D)), _full((8, DV))],
        out_shape=[jax.ShapeDtypeStruct((n, D), BF), jax.ShapeDtypeStruct((n, DC), BF), jax.ShapeDtypeStruct((n, DV), BF),
                   jax.ShapeDtypeStruct((n, DV), BF), jax.ShapeDtypeStruct((8, D), F32), jax.ShapeDtypeStruct((8, DV), F32)],
        compiler_params=_cparams(("arbitrary",)),
    )(dx1, y1, o_f, o_b, g, vecm, gn, wout)


def ffn_fwd_bwd(x1, tgt, vecf, wg, wu, wd, tm):
    n = x1.shape[0]

    def body(x1_ref, t_ref, vec_ref, wg_ref, wu_ref, wd_ref,
             dx1_ref, h2_ref, act_ref, dgt_ref, dup_ref, dy2_ref, s_ref):
        @pl.when(pl.program_id(0) == 0)
        def _():
            s_ref[...] = jnp.zeros_like(s_ref)

        n2g, sh2, sc2, g2, fg = (vec_ref[i:i + 1, :] for i in range(5))
        x1 = x1_ref[...]
        r2 = lax.rsqrt(jnp.mean(x1 * x1, axis=-1, keepdims=True) + EPS)
        xn2 = x1 * r2
        h2 = (xn2 * n2g * (1.0 + sc2) + sh2).astype(BF)
        h2_ref[...] = h2
        gt = _dot(h2, wg_ref[...])
        up = _dot(h2, wu_ref[...])
        sg = _sigmoid(gt)
        sil = gt * sg
        act = (sil * up).astype(BF)
        act_ref[...] = act
        y2 = _dot(act, wd_ref[...])
        x2 = x1 + g2 * y2
        r3 = lax.rsqrt(jnp.mean(x2 * x2, axis=-1, keepdims=True) + EPS)
        xn3 = x2 * r3
        e = xn3 * fg - t_ref[...]
        s_ref[40:48, :] += _colsum8(e * e) * (0.5 / D)
        dyo = e * (1.0 / D)
        s_ref[0:8, :] += _colsum8(dyo * xn3)
        dxn3 = dyo * fg
        dx2 = r3 * (dxn3 - xn3 * jnp.mean(dxn3 * xn3, axis=-1, keepdims=True))
        s_ref[8:16, :] += _colsum8(dx2 * y2)
        dy2 = (dx2 * g2).astype(BF)
        dy2_ref[...] = dy2
        dact = _dot_nt(dy2, wd_ref[...])
        dup = (dact * sil).astype(BF)
        dgt = (dact * up * (sg * (1.0 + gt * (1.0 - sg)))).astype(BF)
        dup_ref[...] = dup
        dgt_ref[...] = dgt
        dh2 = _dot_nt(dgt, wg_ref[...]) + _dot_nt(dup, wu_ref[...])
        s_ref[16:24, :] += _colsum8(dh2)
        t = dh2 * xn2
        s_ref[24:32, :] += _colsum8(t * n2g)
        s_ref[32:40, :] += _colsum8(t * (1.0 + sc2))
        dxn2 = dh2 * ((1.0 + sc2) * n2g)
        dx1_ref[...] = dx2 + r2 * (dxn2 - xn2 * jnp.mean(dxn2 * xn2, axis=-1, keepdims=True))

    tok = lambda w: pl.BlockSpec((tm, w), lambda i: (i, 0))
    wspec = lambda a: pl.BlockSpec(a.shape, lambda i: (0, 0), pipeline_mode=pl.Buffered(1))
    return pl.pallas_call(
        body, grid=(n // tm,), name="ffn_fwd_bwd",
        in_specs=[tok(D), tok(D), _full(vecf.shape), wspec(wg), wspec(wu), wspec(wd)],
        out_specs=[tok(D), tok(D), tok(DFF), tok(DFF), tok(DFF), tok(D), _full((48, D))],
        out_shape=[jax.ShapeDtypeStruct((n, D), F32), jax.ShapeDtypeStruct((n, D), BF), jax.ShapeDtypeStruct((n, DFF), BF),
                   jax.ShapeDtypeStruct((n, DFF), BF), jax.ShapeDtypeStruct((n, DFF), BF), jax.ShapeDtypeStruct((n, D), BF),
                   jax.ShapeDtypeStruct((48, D), F32)],
        compiler_params=_cparams(("arbitrary",)),
    )(x1, tgt, vecf, wg, wu, wd)


def wgrad(a, b, init, t1, t2, tn, name):
    n, k1 = a.shape
    k2 = b.shape[1]

    def body(a_ref, b_ref, *rest):
        o_ref = rest[-1]

        @pl.when(pl.program_id(2) == 0)
        def _():
            o_ref[...] = rest[0][...] if init is not None else jnp.zeros_like(o_ref)

        o_ref[...] += _dot_tn(a_ref[...], b_ref[...])

    ospec = pl.BlockSpec((t1, t2), lambda i, j, k: (i, j))
    extra = ([ospec], {2: 0}, (init,)) if init is not None else ([], {}, ())
    return pl.pallas_call(
        body, grid=(k1 // t1, k2 // t2, n // tn), name=name,
        in_specs=[pl.BlockSpec((tn, t1), lambda i, j, k: (k, i)), pl.BlockSpec((tn, t2), lambda i, j, k: (k, j))] + extra[0],
        out_specs=ospec, out_shape=jax.ShapeDtypeStruct((k1, k2), F32), input_output_aliases=extra[1],
        compiler_params=_cparams(("parallel", "parallel", "arbitrary")),
    )(a, b, *extra[2])


def bwd_in(x, dx1, ag, yb, dco, dqk_f, dqk_b, dv_f, dv_b, dg, dla_f, dla_b, la, r, vec1, win, convw, cvec, wa2, tm):
    n = x.shape[0]
    nseg = tm // GW
    cg = 128

    def body(x_ref, dx1_ref, ag_ref, yb_ref, dco_ref, dqkf_ref, dqkb_ref, dvf_ref, dvb_ref, dg_ref, dlaf_ref, dlab_ref,
             la_ref, r_ref, vec_ref, win_ref, cw_ref, cv_ref, wa2_ref,
             gx_ref, h_ref, dp_ref, dwa2_ref, dcw_ref, s_ref, vc_ref, pad2_ref, dvc_ref, dcw8_ref):
        first = pl.program_id(0) == 0

        @pl.when(first)
        def _():
            s_ref[...] = jnp.zeros_like(s_ref)
            dwa2_ref[...] = jnp.zeros_like(dwa2_ref)
            dcw8_ref[...] = jnp.zeros_like(dcw8_ref)

        yn, rs = _ln_stats(yb_ref[...])
        lng = cv_ref[1:2, :]
        ln = yn * lng + cv_ref[2:3, :]
        sgl = _sigmoid(ln)
        dln = dco_ref[...].astype(F32) * (sgl * (1.0 + ln * (1.0 - sgl)))
        dyn = dln * lng
        dyb = rs * (dyn - jnp.mean(dyn, axis=-1, keepdims=True) - yn * jnp.mean(dyn * yn, axis=-1, keepdims=True))
        s_ref[24:32, 0:DC] += _colsum8(dyb)
        s_ref[24:32, DC:D] += _colsum8(dln * yn)
        s_ref[32:40, 0:DC] += _colsum8(dln)

        agv = ag_ref[...].astype(F32)
        a = agv[:, :DC]
        sgg = _sigmoid(agv[:, DC:])
        vc_ref[...] = a * sgg
        _fill_padded(pad2_ref, dyb, nseg)

        dp_ref[:, 2 * DC:2 * DC + 2 * DK] = (dqkf_ref[...].astype(F32) + dqkb_ref[...].astype(F32)).astype(BF)
        dp_ref[:, 2 * DC + 2 * DK:2 * DC + 2 * DK + DV] = (dvf_ref[...].astype(F32) + dvb_ref[...].astype(F32)).astype(BF)
        dp_ref[:, 2 * DC + 2 * DK + DV:2 * DC + 2 * DK + 2 * DV] = dg_ref[...]

        la = la_ref[...]
        dla = jnp.concatenate([dlaf_ref[...], dlab_ref[...]], axis=1)
        dpre = dla * (1.0 - jnp.exp(TAU * la)) * (1.0 / TAU)
        s_ref[32:40, DC:D] += _colsum8(dpre)
        dpreb = dpre.astype(BF)
        dwa2_ref[...] += _dot_tn(r_ref[...], dpreb)
        dp_ref[:, DINP - 128:] = _dot_nt(dpreb, wa2_ref[...]).astype(BF)
        dh_rest = _dot_nt(dp_ref[:, 2 * DC:], win_ref[:, 2 * DC:])

        for s in range(nseg):
            rows = pl.ds(s * GW, GW)
            for c0 in range(0, DC, cg):
                cs = pl.ds(c0, cg)
                vcs = vc_ref[rows, cs]
                acc = jnp.zeros((GW, cg), F32)
                for j, rows_j in _tap_slabs(pad2_ref, s, cs):
                    acc = acc + cw_ref[pl.ds(CW - 1 - j, 1), cs] * rows_j
                    dcw8_ref[CW - 1 - j, :, cs] += _colsum8(vcs * rows_j)
                dvc_ref[rows, cs] = acc
        dvc = dvc_ref[...]
        dp_ref[:, 0:DC] = (dvc * sgg).astype(BF)
        dp_ref[:, DC:2 * DC] = (dvc * a * sgg * (1.0 - sgg)).astype(BF)

        dh = dh_rest + _dot_nt(dp_ref[:, :2 * DC], win_ref[:, :2 * DC])
        xx = x_ref[...]
        n1g, sh1, sc1 = vec_ref[0:1, :], vec_ref[1:2, :], vec_ref[2:3, :]
        rstd = lax.rsqrt(jnp.mean(xx * xx, axis=-1, keepdims=True) + EPS)
        xn = xx * rstd
        h_ref[...] = (xn * n1g * (1.0 + sc1) + sh1).astype(BF)
        s_ref[0:8, :] += _colsum8(dh)
        t = dh * xn
        s_ref[8:16, :] += _colsum8(t * n1g)
        s_ref[16:24, :] += _colsum8(t * (1.0 + sc1))
        dxn = dh * ((1.0 + sc1) * n1g)
        gx_ref[...] = dx1_ref[...] + rstd * (dxn - xn * jnp.mean(dxn * xn, axis=-1, keepdims=True))

        @pl.when(pl.program_id(0) == pl.num_programs(0) - 1)
        def _():
            dcw_ref[...] = jnp.sum(dcw8_ref[...], axis=1)

    tok = lambda w: pl.BlockSpec((tm, w), lambda i: (i, 0))
    return pl.pallas_call(
        body, grid=(n // tm,), name="bwd_in",
        in_specs=[tok(D), tok(D), tok(2 * DC), tok(DC), tok(DC), tok(2 * DK), tok(2 * DK), tok(DV), tok(DV), tok(DV),
                  tok(DK), tok(DK), tok(2 * DK), tok(128), _full(vec1.shape),
                  pl.BlockSpec(win.shape, lambda i: (0, 0), pipeline_mode=pl.Buffered(1)),
                  _full(convw.shape), _full(cvec.shape), _full(wa2.shape)],
        out_specs=[tok(D), tok(D), tok(DINP), _full((128, 2 * DK)), _full((32, DC)), _full((40, D))],
        out_shape=[jax.ShapeDtypeStruct((n, D), F32), jax.ShapeDtypeStruct((n, D), BF), jax.ShapeDtypeStruct((n, DINP), BF),
                   jax.ShapeDtypeStruct((128, 2 * DK), F32), jax.ShapeDtypeStruct((32, DC), F32),
                   jax.ShapeDtypeStruct((40, D), F32)],
        scratch_shapes=[pltpu.VMEM((tm, DC), F32), pltpu.VMEM((nseg, SEGP, DC), F32), pltpu.VMEM((tm, DC), F32),
                        pltpu.VMEM((32, 8, DC), F32)],
        compiler_params=_cparams(("arbitrary",)),
    )(x, dx1, ag, yb, dco, dqk_f, dqk_b, dv_f, dv_b, dg, dla_f, dla_b, la, r, vec1, win, convw, cvec, wa2)


def _ctx_common(ctx_ref, vec_ref, win_ref, wa2_ref, ba_ref):
    cx = ctx_ref[...]
    t = cx.shape[0]
    rstd = lax.rsqrt(jnp.mean(cx * cx, axis=-1, keepdims=True) + EPS)
    xn = cx * rstd
    hc = (xn * vec_ref[0:1, :] * (1.0 + vec_ref[2:3, :]) + vec_ref[1:2, :]).astype(BF)
    k0 = 2 * DC + DK
    kv = _dot(hc, win_ref[:, k0:k0 + DK + DV]).astype(BF).astype(F32)
    r = _dot(hc, win_ref[:, DINP - 128:]).astype(BF)
    la = _log_sigmoid(_dot(r, wa2_ref[...]) + ba_ref[...]) * (1.0 / TAU)
    incl = _tri(t, "le").astype(BF)
    strict = _tri(t, "lt").astype(BF)
    bf = _mask_dot(incl, la[:, :DK])
    wf = jnp.exp(bf[t - 1:t, :] - bf)
    wb = jnp.exp(_mask_dot(strict, la[:, DK:]))
    return xn, hc, kv[:, :DK], kv[:, DK:], r, la, wf, wb


def ctx_fwd(ctx, vecc, win, wa2, ba):
    def body(ctx_ref, vec_ref, win_ref, wa2_ref, ba_ref, s_ref):
        _, _, k, v, _, _, wf, wb = _ctx_common(ctx_ref, vec_ref, win_ref, wa2_ref, ba_ref)
        vb = v.astype(BF)
        for d, w in enumerate((wf, wb)):
            kd = (k * w).astype(BF)
            for h in range(NH):
                s_ref[d, h // 2, :, (h % 2) * HK:(h % 2 + 1) * HK] = _dot_tn(vb[:, h * HV:(h + 1) * HV], kd[:, h * HK:(h + 1) * HK])

    return pl.pallas_call(
        body, name="ctx_fwd", out_shape=jax.ShapeDtypeStruct((2, NP, HV, PW), F32),
        compiler_params=pltpu.CompilerParams(vmem_limit_bytes=VMEM_LIMIT),
    )(ctx, vecc, win, wa2, ba)


def ctx_bwd(ctx, vecc, win, wa2, ba, ds0):
    t = ctx.shape[0]

    def body(ctx_ref, vec_ref, win_ref, wa2_ref, ba_ref, ds_ref, dwin_ref, dwa2_ref, s_ref, dpc_ref):
        xn, hc, k, v, r, la, wf, wb = _ctx_common(ctx_ref, vec_ref, win_ref, wa2_ref, ba_ref)
        vb = v.astype(BF)
        strict = _tri(t, "lt").astype(BF)
        strict_t = _tri(t, "gt").astype(BF)
        dpc_ref[...] = jnp.zeros_like(dpc_ref)
        k0 = 2 * DC + DK
        dk = jnp.zeros((t, DK), F32)
        des = []
        for d, w in enumerate((wf, wb)):
            kd = (k * w).astype(BF)
            dkds = []
            for h in range(NH):
                dsb = ds_ref[d, h // 2, :, (h % 2) * HK:(h % 2 + 1) * HK].astype(BF)
                dkds.append(_dot(vb[:, h * HV:(h + 1) * HV], dsb))
                dvh = _dot_nt(kd[:, h * HK:(h + 1) * HK], dsb)
                vs = slice(k0 + DK + h * HV, k0 + DK + (h + 1) * HV)
                if d == 0:
                    dpc_ref[:, vs] = dvh.astype(BF)
                else:
                    dpc_ref[:, vs] = (dpc_ref[:, vs].astype(F32) + dvh).astype(BF)
            dkd = jnp.concatenate(dkds, axis=1)
            dk = dk + dkd * w
            des.append(dkd * k * w)
        dpc_ref[:, k0:k0 + DK] = dk.astype(BF)
        dla = jnp.concatenate([_mask_dot(strict, des[0]), _mask_dot(strict_t, des[1])], axis=1)
        dpre = dla * (1.0 - jnp.exp(TAU * la)) * (1.0 / TAU)
        dpreb = dpre.astype(BF)
        dwa2_ref[...] = _dot_tn(r, dpreb)
        dpc_ref[:, DINP - 128:] = _dot_nt(dpreb, wa2_ref[...]).astype(BF)
        dpc = dpc_ref[...]
        dwin_ref[...] = _dot_tn(hc, dpc)
        dhc = _dot_nt(dpc, win_ref[...])
        n1g, sc1 = vec_ref[0:1, :], vec_ref[2:3, :]
        tt = dhc * xn
        s_ref[...] = jnp.zeros_like(s_ref)
        s_ref[0:1, :] = jnp.sum(tt * (1.0 + sc1), axis=0, keepdims=True)
        s_ref[1:2, :] = jnp.sum(dhc, axis=0, keepdims=True)
        s_ref[2:3, :] = jnp.sum(tt * n1g, axis=0, keepdims=True)
        s_ref[3:4, DC:D] = jnp.sum(dpre, axis=0, keepdims=True)

    return pl.pallas_call(
        body, name="ctx_bwd",
        out_shape=[jax.ShapeDtypeStruct((D, DINP), F32), jax.ShapeDtypeStruct((128, 2 * DK), F32),
                   jax.ShapeDtypeStruct((8, D), F32)],
        scratch_shapes=[pltpu.VMEM((t, DINP), BF)],
        compiler_params=pltpu.CompilerParams(vmem_limit_bytes=VMEM_LIMIT),
    )(ctx, vecc, win, wa2, ba, ds0)


def _silu(x):
    return x * _sigmoid(x)


def mod_fwd(cext, wm, bm):
    def body(c_ref, w_ref, b_ref, o_ref):
        o_ref[...] = _dot(_silu(c_ref[...]).astype(BF), w_ref[...].astype(BF)) + b_ref[...]

    return pl.pallas_call(body, name="mod_fwd", out_shape=jax.ShapeDtypeStruct((cext.shape[0], wm.shape[1]), F32),
                          compiler_params=pltpu.CompilerParams(vmem_limit_bytes=VMEM_LIMIT))(cext, wm, bm)


def mod_bwd(cext, dm, wm):
    def body(c_ref, d_ref, w_ref, gw_ref, ds_ref):
        dmb = d_ref[...].astype(BF)
        gw_ref[...] = _dot_tn(_silu(c_ref[...]).astype(BF), dmb)
        ds_ref[...] = _dot_nt(dmb, w_ref[...].astype(BF))

    return pl.pallas_call(body, name="mod_bwd",
                          out_shape=[jax.ShapeDtypeStruct(wm.shape, F32), jax.ShapeDtypeStruct(cext.shape, F32)],
                          compiler_params=pltpu.CompilerParams(vmem_limit_bytes=VMEM_LIMIT))(cext, dm, wm)


def pack_small(sf, s1, s2, sd, sc, dcw, dwa2, dwa2_c):
    def body(sf_ref, s1_ref, s2_ref, sd_ref, sc_ref, dcw_ref, dwa2_ref, dwa2c_ref, o_ref, ocw_ref, owa_ref):
        rsum = lambda ref, i: jnp.sum(ref[8 * i:8 * i + 8, :], axis=0, keepdims=True)
        o_ref[...] = jnp.zeros_like(o_ref)
        o_ref[0:1, :] = rsum(sd_ref, 0)
        o_ref[1:2, :] = rsum(sd_ref, 1)
        o_ref[2:3, :] = rsum(s1_ref, 0)
        o_ref[3:4, :] = rsum(sf_ref, 2)
        o_ref[4:5, :] = rsum(sf_ref, 3)
        o_ref[5:6, :] = rsum(sf_ref, 1)
        o_ref[6:7, :] = sc_ref[1:2, :]
        o_ref[7:8, :] = sc_ref[2:3, :]
        o_ref[8:9, :] = rsum(sd_ref, 2) + sc_ref[0:1, :]
        o_ref[9:10, :] = rsum(sf_ref, 4)
        o_ref[10:11, :] = rsum(sf_ref, 0)
        o_ref[11:12, :] = rsum(sd_ref, 3)
        o_ref[12:13, :] = rsum(sd_ref, 4) + sc_ref[3:4, :]
        g = jnp.sum(s2_ref[...], axis=0, keepdims=True)
        o_ref[13:14, 0:HV] = g[:, 0:HV] + g[:, HV:2 * HV] + g[:, 2 * HV:3 * HV] + g[:, 3 * HV:4 * HV]
        o_ref[14:15, :] = rsum(sf_ref, 5)
        ocw_ref[...] = dcw_ref[...]
        owa_ref[...] = dwa2_ref[0:32, :] + dwa2c_ref[0:32, :]

    return pl.pallas_call(body, name="pack_small",
                          out_shape=[jax.ShapeDtypeStruct((16, D), F32), jax.ShapeDtypeStruct((32, DC), F32),
                                     jax.ShapeDtypeStruct((32, 2 * DK), F32)])(sf, s1, s2, sd, sc, dcw, dwa2, dwa2_c)


def small_totals(g8):
    r = g8.shape[1]

    def body(g_ref, t_ref, bm_ref, loss_ref):
        acc = g_ref[0]
        for i in range(1, NDEV):
            acc = acc + g_ref[i]
        t_ref[...] = acc
        bm_ref[...] = jnp.zeros_like(bm_ref)
        bm_ref[0:6, :] = acc[0:6, :]
        bm_ref[0:2, :] += acc[6:8, :]
        loss_ref[...] = jnp.broadcast_to(jnp.sum(acc[14:15, :], axis=1, keepdims=True), loss_ref.shape)

    return pl.pallas_call(body, name="small_totals",
                          out_shape=[jax.ShapeDtypeStruct((r, D), F32), jax.ShapeDtypeStruct((8, D), F32),
                                     jax.ShapeDtypeStruct((8, 128), F32)])(g8)


def cctx_grad(p8, c_ctx_row):
    def body(p_ref, c_ref, o_ref):
        acc = p_ref[0, 0:1, :]
        for j in range(1, NCHIP):
            acc = acc + p_ref[2 * j, 0:1, :]
        cc = c_ref[0:1, :]
        sg = _sigmoid(cc)
        o_ref[...] = jnp.zeros_like(o_ref)
        o_ref[0:1, :] = acc * (sg * (1.0 + cc * (1.0 - sg)))

    return pl.pallas_call(body, name="cctx_grad", out_shape=jax.ShapeDtypeStruct((8, D), F32))(p8, c_ctx_row)


def adamw(w, g, m, v, rows, name, emit_grad=False):
    r, c = w.shape

    def body(w_ref, g_ref, m_ref, v_ref, d_ref, nm_ref, nv_ref, *go_ref):
        gg = g_ref[...]
        nm = ADAM_B1 * m_ref[...] + (1.0 - ADAM_B1) * gg
        nv = ADAM_B2 * v_ref[...] + (1.0 - ADAM_B2) * (gg * gg)
        m_hat = nm / (1.0 - ADAM_B1 ** ADAM_STEP)
        v_hat = nv / (1.0 - ADAM_B2 ** ADAM_STEP)
        d_ref[...] = -ADAM_LR * (m_hat / (jnp.sqrt(v_hat) + ADAM_EPS) + ADAM_WD * w_ref[...])
        nm_ref[...] = nm
        nv_ref[...] = nv
        if emit_grad:
            go_ref[0][...] = gg

    spec = pl.BlockSpec((rows, c), lambda i: (i, 0))
    sds = jax.ShapeDtypeStruct((r, c), F32)
    nout = 4 if emit_grad else 3
    return pl.pallas_call(
        body, grid=(r // rows,), name=name, in_specs=[spec] * 4, out_specs=[spec] * nout, out_shape=[sds] * nout,
        compiler_params=_cparams(("parallel",)),
    )(w, g, m, v)


def _me():
    return lax.axis_index("x"), lax.axis_index("y"), lax.axis_index("c")


def _flip(v, bit):
    return 1 - v if bit else v


ANY = pl.BlockSpec(memory_space=pl.ANY)


def _gather8(x_ref, o_ref, ssem, rsem, lsem):
    mx, my, mc = _me()
    me = 4 * mx + 2 * my + mc
    local = pltpu.make_async_copy(x_ref, o_ref.at[me], lsem)
    local.start()
    peer = lambda k: (_flip(mx, k & 4), _flip(my, k & 2), _flip(mc, k & 1))
    sends = []
    for k in range(1, NDEV):
        cp = pltpu.make_async_remote_copy(src_ref=x_ref, dst_ref=o_ref.at[me], send_sem=ssem.at[k - 1],
                                          recv_sem=rsem.at[k - 1], device_id=peer(k), device_id_type=MESH)
        cp.start()
        sends.append(cp)
    for k in range(1, NDEV):
        px, py, pc = peer(k)
        pltpu.make_async_remote_copy(src_ref=x_ref, dst_ref=o_ref.at[4 * px + 2 * py + pc], send_sem=ssem.at[k - 1],
                                     recv_sem=rsem.at[k - 1], device_id=(px, py, pc), device_id_type=MESH).wait_recv()
    for cp in sends:
        cp.wait_send()
    local.wait()


def _gather8_sems():
    return [pltpu.SemaphoreType.DMA((NDEV - 1,)), pltpu.SemaphoreType.DMA((NDEV - 1,)), pltpu.SemaphoreType.DMA]


def all_gather8(x, name):
    vm = pl.BlockSpec(memory_space=pltpu.VMEM)
    return pl.pallas_call(_gather8_body(), name=name, in_specs=[vm], out_specs=vm,
                          out_shape=jax.ShapeDtypeStruct((NDEV,) + x.shape, x.dtype), scratch_shapes=_gather8_sems())(x)


def _gather8_body():
    def body(x_ref, o_ref, ssem, rsem, lsem):
        _gather8(x_ref, o_ref, ssem, rsem, lsem)
    return body


def prologue(small, c_ctx_rows, wm, bm, w_in_shard):
    ex = ChipExchange("gather", [w_in_shard])

    def body(s_ref, cc_ref, w_ref, b_ref, win_ref, s8_ref, m8_ref, wing_ref, mloc_ref, *sems):
        ex.start([win_ref], [wing_ref], sems[6:])
        _gather8(s_ref, s8_ref, *sems[0:3])
        cext = jnp.concatenate([s8_ref[:, 0, :], cc_ref[...]], axis=0)
        mloc_ref[...] = _dot(_silu(cext).astype(BF), w_ref[...].astype(BF)) + b_ref[...]
        _gather8(mloc_ref, m8_ref, *sems[3:6])
        ex.wait([win_ref], [wing_ref], sems[6:])

    vm = pl.BlockSpec(memory_space=pltpu.VMEM)
    wcols = wm.shape[1]
    return pl.pallas_call(
        body, name="prologue", in_specs=[vm, vm, vm, vm, ANY], out_specs=[vm, vm, ANY],
        out_shape=[jax.ShapeDtypeStruct((NDEV, 16, D), F32), jax.ShapeDtypeStruct((NDEV, 16, wcols), F32)] + ex.out_shape,
        scratch_shapes=[pltpu.VMEM((16, wcols), F32)] + _gather8_sems() + _gather8_sems() + ex.scratch,
        compiler_params=pltpu.CompilerParams(vmem_limit_bytes=VMEM_LIMIT),
    )(small, c_ctx_rows, wm, bm, w_in_shard)


def _chip_peers(mx, my):
    out = []
    for p in range(1, NCHIP):
        px, py = _flip(mx, p & 2), _flip(my, p & 1)
        out.append((px, py, 2 * px + py))
    return out


class ChipExchange:
    def __init__(self, kind, arrays):
        self.kind = kind
        self.n = len(arrays)
        if kind == "gather":
            self.out_shape = [jax.ShapeDtypeStruct((NCHIP,) + a.shape, a.dtype) for a in arrays]
        else:
            self.out_shape = [jax.ShapeDtypeStruct(a.shape, a.dtype) for a in arrays]
        self.scratch = [pltpu.SemaphoreType.DMA((3 * self.n,)), pltpu.SemaphoreType.DMA((3 * self.n,)),
                        pltpu.SemaphoreType.DMA((self.n,))]

    def _copies(self, ins, outs, sems):
        ssem, rsem, lsem = sems
        mx, my, mc = _me()
        jme = 2 * mx + my
        gather = self.kind == "gather"
        local, sends, waits = [], [], []
        for k in range(self.n):
            local.append(pltpu.make_async_copy(ins[k] if gather else ins[k].at[jme], outs[k].at[jme], lsem.at[k]))
            for p, (px, py, jp) in enumerate(_chip_peers(mx, my)):
                src = ins[k] if gather else ins[k].at[jp]
                sem = dict(send_sem=ssem.at[3 * k + p], recv_sem=rsem.at[3 * k + p], device_id=(px, py, mc),
                           device_id_type=MESH)
                sends.append(pltpu.make_async_remote_copy(src_ref=src, dst_ref=outs[k].at[jme], **sem))
                waits.append(pltpu.make_async_remote_copy(src_ref=src, dst_ref=outs[k].at[jp], **sem))
        return local, sends, waits

    def start(self, ins, outs, sems):
        local, sends, _ = self._copies(ins, outs, sems)
        for cp in local + sends:
            cp.start()

    def wait(self, ins, outs, sems):
        local, _, waits = self._copies(ins, outs, sems)
        for cp in waits:
            cp.wait_recv()
        for cp in waits:
            cp.wait_send()
        for cp in local:
            cp.wait()


def chip_exchange(kind, arrays, name):
    ex = ChipExchange(kind, arrays)
    n = ex.n

    def body(*refs):
        ins, outs, sems = refs[:n], refs[n:2 * n], refs[2 * n:]
        ex.start(ins, outs, sems)
        ex.wait(ins, outs, sems)

    return pl.pallas_call(body, name=name, in_specs=[ANY] * n, out_specs=[ANY] * n, out_shape=ex.out_shape,
                          scratch_shapes=ex.scratch)(*arrays)


def sibling_add(g, ngrp, hr, tr, name):
    c_ = g.shape[1]
    nt = hr // tr

    def body(cidx, keep_ref, give_ref, o_ref, land, ssem, rsem):
        mx, my, mc = _me()
        t = pl.program_id(0) * nt + pl.program_id(1)
        s = t % 2
        cp = pltpu.make_async_remote_copy(src_ref=give_ref, dst_ref=land.at[s], send_sem=ssem.at[s], recv_sem=rsem.at[s],
                                          device_id=(mx, my, 1 - mc), device_id_type=MESH)
        cp.start()
        cp.wait_recv()
        o_ref[...] = keep_ref[...] + land[s]
        cp.wait_send()

    grid_spec = pltpu.PrefetchScalarGridSpec(
        num_scalar_prefetch=1, grid=(ngrp, nt),
        in_specs=[pl.BlockSpec((tr, c_), lambda i, j, cr: ((2 * i + cr[0]) * nt + j, 0)),
                  pl.BlockSpec((tr, c_), lambda i, j, cr: ((2 * i + 1 - cr[0]) * nt + j, 0))],
        out_specs=pl.BlockSpec((tr, c_), lambda i, j, cr: (i * nt + j, 0)),
        scratch_shapes=[pltpu.VMEM((2, tr, c_), F32), pltpu.SemaphoreType.DMA((2,)), pltpu.SemaphoreType.DMA((2,))])
    cidx = lax.axis_index("c").astype(jnp.int32).reshape(1)
    return pl.pallas_call(body, grid_spec=grid_spec, name=name, out_shape=jax.ShapeDtypeStruct((ngrp * hr, c_), F32),
                          compiler_params=_cparams(("arbitrary", "arbitrary")))(cidx, g, g)


def finish_grad(b, tr, name):
    _, r2, c_ = b.shape

    def body(b_ref, g_ref, mine, land, ssem, rsem):
        mx, my, mc = _me()
        t = pl.program_id(0)
        s = t % 2
        mine[s] = (b_ref[0].astype(F32) + b_ref[1].astype(F32)) + (b_ref[2].astype(F32) + b_ref[3].astype(F32))
        cp = pltpu.make_async_remote_copy(src_ref=mine.at[s], dst_ref=land.at[s], send_sem=ssem.at[s], recv_sem=rsem.at[s],
                                          device_id=(mx, my, 1 - mc), device_id_type=MESH)
        cp.start()
        cp.wait_recv()
        g_ref[mc] = mine[s]
        g_ref[1 - mc] = land[s]
        cp.wait_send()

    return pl.pallas_call(
        body, grid=(r2 // tr,), name=name,
        in_specs=[pl.BlockSpec((NCHIP, tr, c_), lambda i: (0, i, 0))],
        out_specs=pl.BlockSpec((2, tr, c_), lambda i: (0, i, 0)), out_shape=jax.ShapeDtypeStruct((2, r2, c_), F32),
        scratch_shapes=[pltpu.VMEM((2, tr, c_), F32), pltpu.VMEM((2, tr, c_), F32), pltpu.SemaphoreType.DMA((2,)),
                        pltpu.SemaphoreType.DMA((2,))],
        compiler_params=_cparams(("arbitrary",)))(b)


TM_IN = 512
TM_GLA = 512
TM_MERGE = 512
TM_FFN = 256
TN_WGRAD = 2048

WEIGHTS = ['c_ctx', 'w_mod', 'b_mod', 'norm1_g', 'norm2_g', 'w_in', 'conv_w', 'conv_b', 'conv_ln_g', 'conv_ln_b', 'w_a2_f',
           'b_a_f', 'w_a2_b', 'b_a_b', 'gla_norm_g', 'w_out', 'w_gate', 'w_up', 'w_down', 'final_g']
BIG = ['w_in', 'w_out', 'w_gate', 'w_up', 'w_down']


def _rows(*vs):
    w = vs[0].size
    row = lax.broadcasted_iota(jnp.int32, (8, w), 0)
    out = jnp.zeros((8, w), F32)
    for i, v in enumerate(vs):
        out = jnp.where(row == i, v.reshape(1, w), out)
    return out


def _small_slab(p):
    cat = lambda *ks: jnp.concatenate([p[k].reshape(-1) for k in ks])
    vecs = _rows(p['c_ctx'], p['norm1_g'], p['norm2_g'], p['final_g'], cat('conv_b', 'conv_ln_g'),
                 cat('conv_ln_b', 'b_a_f', 'b_a_b'), jnp.pad(p['gla_norm_g'].reshape(-1), (0, D - HV)))
    bmod = jnp.pad(p['b_mod'].reshape(6, D), ((0, 2), (0, 0)))
    shards = jnp.pad(jnp.concatenate([jnp.pad(p['conv_w'].reshape(-1), (0, DC // NCHIP)), cat('w_a2_f', 'w_a2_b')]),
                     (0, 2 * D)).reshape(8, D)
    return jnp.concatenate([vecs, bmod, shards], axis=0)


def _unslab(s):
    return {
        'c_ctx': s[0], 'norm1_g': s[1:2], 'norm2_g': s[2:3], 'final_g': s[3],
        'conv_b': s[4:5, :DC], 'conv_ln_g': s[4:5, DC:], 'conv_ln_b': s[5:6, :DC],
        'b_a_f': s[5:6, DC:DC + DK], 'b_a_b': s[5:6, DC + DK:], 'gla_norm_g': s[6:7, :HV],
        'b_mod': s[8:14].reshape(1, 6 * D),
        'conv_w': s[16:20].reshape(32, DC // NCHIP)[:CW].reshape(1, CW, DC // NCHIP),
        'w_a2_f': s[20].reshape(1, RANK, DK // NCHIP), 'w_a2_b': s[21].reshape(1, RANK, DK // NCHIP),
    }


def kernel(x, c, ctx, c_ctx, w_mod, b_mod, norm1_g, norm2_g, w_in, conv_w, conv_b, conv_ln_g, conv_ln_b, w_a2_f, b_a_f, w_a2_b, b_a_b, gla_norm_g, w_out, w_gate, w_up, w_down, final_g, loss_target, m_c_ctx, m_w_mod, m_b_mod, m_norm1_g, m_norm2_g, m_w_in, m_conv_w, m_conv_b, m_conv_ln_g, m_conv_ln_b, m_w_a2_f, m_b_a_f, m_w_a2_b, m_b_a_b, m_gla_norm_g, m_w_out, m_w_gate, m_w_up, m_w_down, m_final_g, v_c_ctx, v_w_mod, v_b_mod, v_norm1_g, v_norm2_g, v_w_in, v_conv_w, v_conv_b, v_conv_ln_g, v_conv_ln_b, v_w_a2_f, v_b_a_f, v_w_a2_b, v_b_a_b, v_gla_norm_g, v_w_out, v_w_gate, v_w_up, v_w_down, v_final_g):
    w = dict(c_ctx=c_ctx, w_mod=w_mod, b_mod=b_mod, norm1_g=norm1_g, norm2_g=norm2_g, w_in=w_in, conv_w=conv_w, conv_b=conv_b,
             conv_ln_g=conv_ln_g, conv_ln_b=conv_ln_b, w_a2_f=w_a2_f, b_a_f=b_a_f, w_a2_b=w_a2_b, b_a_b=b_a_b,
             gla_norm_g=gla_norm_g, w_out=w_out, w_gate=w_gate, w_up=w_up, w_down=w_down, final_g=final_g)
    m = dict(c_ctx=m_c_ctx, w_mod=m_w_mod, b_mod=m_b_mod, norm1_g=m_norm1_g, norm2_g=m_norm2_g, w_in=m_w_in, conv_w=m_conv_w,
             conv_b=m_conv_b, conv_ln_g=m_conv_ln_g, conv_ln_b=m_conv_ln_b, w_a2_f=m_w_a2_f, b_a_f=m_b_a_f, w_a2_b=m_w_a2_b,
             b_a_b=m_b_a_b, gla_norm_g=m_gla_norm_g, w_out=m_w_out, w_gate=m_w_gate, w_up=m_w_up, w_down=m_w_down,
             final_g=m_final_g)
    v = dict(c_ctx=v_c_ctx, w_mod=v_w_mod, b_mod=v_b_mod, norm1_g=v_norm1_g, norm2_g=v_norm2_g, w_in=v_w_in, conv_w=v_conv_w,
             conv_b=v_conv_b, conv_ln_g=v_conv_ln_g, conv_ln_b=v_conv_ln_b, w_a2_f=v_w_a2_f, b_a_f=v_b_a_f, w_a2_b=v_w_a2_b,
             b_a_b=v_b_a_b, gla_norm_g=v_gla_norm_g, w_out=v_w_out, w_gate=v_w_gate, w_up=v_w_up, w_down=v_w_down,
             final_g=v_final_g)
    mx, my, mc = _me()
    jme = 2 * mx + my
    me = 4 * mx + 2 * my + mc
    wmc = D * 6 // NCHIP
    xx, tgt, cx = x[0], loss_target[0], ctx[0]
    n = xx.shape[0]

    bshard = [w[k][0].astype(BF) for k in BIG]
    sw = jnp.concatenate([jnp.pad(conv_w[0], ((0, 1), (0, 0))).reshape(-1), w_a2_f[0].reshape(-1), w_a2_b[0].reshape(-1)])
    small = jnp.concatenate([_rows(c[0]), jnp.pad(sw.reshape(6, D), ((0, 2), (0, 0)))], axis=0)
    cs8, mall, win_g = prologue(small, _rows(c_ctx), w_mod[0], lax.dynamic_slice_in_dim(b_mod, jme * wmc, wmc, axis=1),
                                bshard[0])
    cext = jnp.concatenate([cs8[:, 0, :], _rows(c_ctx)], axis=0)
    swc = jnp.stack([cs8[2 * j, 8:16] for j in range(NCHIP)]).reshape(NCHIP, 8 * D)
    convw = jnp.transpose(swc[:, :32 * 128].reshape(NCHIP, 32, 128), (1, 0, 2)).reshape(32, DC)
    a2 = lambda o: jnp.transpose(swc[:, o:o + RANK * 64].reshape(NCHIP, RANK, 64), (1, 0, 2)).reshape(RANK, DK)
    wa2 = jnp.zeros((128, 2 * DK), F32).at[0:RANK, 0:DK].set(a2(32 * 128)).at[RANK:2 * RANK, DK:].set(a2(32 * 128 + RANK * 64))
    wa2 = wa2.astype(BF)
    mall = jnp.concatenate([mall[2 * j] for j in range(NCHIP)], axis=1)
    sh1, sc1, g1, sh2, sc2, g2 = jnp.split(lax.dynamic_slice_in_dim(mall, me, 1, axis=0)[0], 6)
    csh1, csc1 = mall[8, :D], mall[8, D:2 * D]
    cols = lambda a: jnp.transpose(a, (1, 0, 2)).reshape(a.shape[1], -1)
    win = jnp.pad(cols(win_g), ((0, 0), (0, DINP - DIN)))
    ba = jnp.concatenate([b_a_f, b_a_b], axis=1)
    cvec = _rows(conv_b, conv_ln_g, conv_ln_b)
    vec1 = _rows(norm1_g, sh1, sc1)
    vecc = _rows(norm1_g, csh1, csc1)
    vecm = _rows(g1)
    vecf = _rows(norm2_g, sh2, sc2, g2, final_g)
    gn = jnp.tile(gla_norm_g, (1, NH))

    s0 = ctx_fwd(cx, vecc, win, wa2, ba)
    res = fwd_in(xx, vec1, win, convw, cvec, wa2, ba, TM_IN, ChipExchange("gather", bshard[1:]), bshard[1:])
    ag, yb, co, qk, vv, gg, la, r = res[:8]
    wout = res[8].reshape(D, D)
    wg, wu = cols(res[9]), cols(res[10])
    wd = res[11].reshape(DFF, D)
    o_f, o_b, se_f, se_b = gla_fwd(qk, vv, la, s0, TM_GLA)
    x1, y1, cat = merge_fwd(xx, o_f, o_b, gg, co, vecm, gn, wout, TM_MERGE)

    dx1, h2, act, dgt, dup, dy2, sf = ffn_fwd_bwd(x1, tgt, vecf, wg, wu, wd, TM_FFN)
    d_wg = wgrad(h2, dgt, None, D, DFF // 2, TN_WGRAD, "wgrad_gate")
    d_wu = wgrad(h2, dup, None, D, DFF // 2, TN_WGRAD, "wgrad_up")
    d_wd = wgrad(act, dy2, None, DFF // 2, D, TN_WGRAD, "wgrad_down")
    dy1, dco, do, dg, s1, s2 = merge_bwd(dx1, y1, o_f, o_b, gg, vecm, gn, wout, TM_MERGE)
    d_wout = wgrad(cat, dy1, None, D, D, TN_WGRAD, "wgrad_out")

    shard = lambda a, k: jnp.transpose(a.reshape(a.shape[0], NCHIP, k), (1, 0, 2))
    hd = D // 2
    parts = [sibling_add(d_wout, NCHIP, hd // NCHIP, hd // NCHIP, "xadd_w_out").reshape(NCHIP, hd // NCHIP, D),
             shard(sibling_add(d_wg, 1, hd, 128, "xadd_w_gate"), DFF // NCHIP),
             shard(sibling_add(d_wu, 1, hd, 128, "xadd_w_up"), DFF // NCHIP),
             sibling_add(d_wd, NCHIP, DFF // 8, DFF // 16, "xadd_w_down").reshape(NCHIP, DFF // 8, D)]
    parts = [p.astype(BF) for p in parts]
    res = gla_bwd(qk, vv, la, do, se_f, se_b, TM_GLA, ChipExchange("scatter", parts), parts)
    dqk_f, dv_f, dla_f, dqk_b, dv_b, dla_b, ds0 = res[:7]
    recv = list(res[7:])
    dwin_c, dwa2_c, sc = ctx_bwd(cx, vecc, win, wa2, ba, ds0)
    grad_x, h, dp, dwa2, dcw, sd = bwd_in(xx, dx1, ag, yb, dco, dqk_f, dqk_b, dv_f, dv_b, dg, dla_f, dla_b, la, r,
                                          vec1, win, convw, cvec, wa2, TM_IN)
    d_win = wgrad(h, dp, dwin_c, D, DINP // 3, TN_WGRAD, "wgrad_in")
    part_in = shard(sibling_add(d_win, 1, hd, 128, "xadd_w_in")[:, :DIN], DIN // NCHIP).astype(BF)
    recv = list(chip_exchange("scatter", [part_in], "scatter_w_in")) + recv

    rows16, dcw_t, dwa2_t = pack_small(sf, s1, s2, sd, sc, dcw, dwa2, dwa2_c)
    sp = jnp.concatenate([rows16, dcw_t.reshape(16, D), dwa2_t.reshape(16, D)], axis=0)
    g8 = all_gather8(sp, "gather_small_grads")
    tot, bm_g, loss8 = small_totals(g8)
    loss = loss8[0, 0]
    dmod8 = g8[:, 0:6, :].reshape(NDEV, 6 * D)
    dmodc = jnp.concatenate([tot[6], tot[7], jnp.zeros((4 * D,), F32)])
    dm = jnp.concatenate([dmod8, _rows(dmodc)], axis=0)
    dm = lax.dynamic_slice_in_dim(dm, jme * wmc, wmc, axis=1)
    g_wmod, dsil = mod_bwd(cext, dm, w_mod[0])
    p8 = all_gather8(dsil[8:16], "gather_dsilu")
    g_cctx = cctx_grad(p8, _rows(c_ctx))[0]

    grads, delta, new_m, new_v = {}, {}, {}, {}
    for i, k in enumerate(BIG):
        r2 = recv[i].shape[1]
        gk = finish_grad(recv[i], 176 if r2 % 128 else 128, "finish_" + k).reshape(w[k].shape[1:])
        outs = adamw(w[k][0], gk, m[k][0], v[k][0], 88 if gk.shape[0] % 128 else 128, "adamw_" + k, emit_grad=True)
        delta[k], new_m[k], new_v[k], grads[k] = (o[None] for o in outs)
    grads['w_mod'] = g_wmod[None]
    d_, m_, v_ = adamw(w_mod[0], g_wmod, m_w_mod[0], v_w_mod[0], 128, "adamw_w_mod")
    delta['w_mod'], new_m['w_mod'], new_v['w_mod'] = d_[None], m_[None], v_[None]
    small_g = {
        'c_ctx': g_cctx, 'b_mod': bm_g[0:6].reshape(1, 6 * D), 'norm1_g': tot[8:9], 'norm2_g': tot[9:10], 'final_g': tot[10],
        'conv_b': tot[11:12, :DC], 'conv_ln_g': tot[11:12, DC:], 'conv_ln_b': tot[12:13, :DC],
        'b_a_f': tot[12:13, DC:DC + DK], 'b_a_b': tot[12:13, DC + DK:], 'gla_norm_g': tot[13:14, :HV],
        'conv_w': lax.dynamic_slice_in_dim(tot[16:32].reshape(32, DC)[:CW], jme * (DC // NCHIP), DC // NCHIP, axis=1)[None],
        'w_a2_f': lax.dynamic_slice_in_dim(tot[32:48].reshape(32, 2 * DK)[0:RANK, 0:DK], jme * (DK // NCHIP), DK // NCHIP, axis=1)[None],
        'w_a2_b': lax.dynamic_slice_in_dim(tot[32:48].reshape(32, 2 * DK)[RANK:2 * RANK, DK:], jme * (DK // NCHIP), DK // NCHIP, axis=1)[None],
    }
    grads.update(small_g)
    sd_, sm_, sv_ = adamw(_small_slab(w), _small_slab(small_g), _small_slab(m), _small_slab(v), 24,
                          "adamw_small")
    for dst, slab in ((delta, sd_), (new_m, sm_), (new_v, sv_)):
        dst.update(_unslab(slab))
    out = [loss, grad_x[None]]
    for group in (grads, delta, new_m, new_v):
        out += [group[k].reshape(w[k].shape) for k in WEIGHTS]
    return tuple(out)
```

```python
import jax
import jax.numpy as jnp
from jax import lax
from jax.experimental import pallas as pl
from jax.experimental.pallas import tpu as pltpu

F32 = jnp.float32
BF = jnp.bfloat16

D = 1024
DC = 512
NH = 4
HK = 64
HV = 128
DK = NH * HK
DV = NH * HV
RANK = 16
CH = 64
GW = 64
CW = 31
SEGP = GW + 32
DFF = 2816
DIN = 2592
DINP = 2688
EPS = 1e-6
TAU = 16.0
QSCALE = HK ** -0.5
NCHIP = 4
NDEV = 8

ADAM_LR = 0.001
ADAM_B1 = 0.9
ADAM_B2 = 0.999
ADAM_EPS = 1e-08
ADAM_WD = 0.01
ADAM_STEP = 10

VMEM_LIMIT = 56 * 1024 * 1024
MESH = pl.DeviceIdType.MESH


def _dot(a, b):
    return jnp.dot(a, b, preferred_element_type=F32)


def _dot_nt(a, b):
    return lax.dot_general(a, b, (((1,), (1,)), ((), ())), preferred_element_type=F32)


def _dot_tn(a, b):
    return lax.dot_general(a, b, (((0,), (0,)), ((), ())), preferred_element_type=F32)


def _split3(x):
    hi = x.astype(BF)
    r1 = x - hi.astype(F32)
    mid = r1.astype(BF)
    lo = (r1 - mid.astype(F32)).astype(BF)
    return hi, mid, lo


def _mask_dot(t, x):
    hi, mid, lo = _split3(x)
    return _dot(t, hi) + _dot(t, mid) + _dot(t, lo)


def _sigmoid(x):
    return 1.0 / (1.0 + jnp.exp(-x))


def _log_sigmoid(x):
    return jnp.minimum(x, 0.0) - jnp.log(1.0 + jnp.exp(-jnp.abs(x)))


def _colsum8(z):
    t, c = z.shape
    return jnp.sum(z.reshape(t // 8, 8, c), axis=0)


def _tri(n, kind):
    r = lax.broadcasted_iota(jnp.int32, (n, n), 0)
    c = lax.broadcasted_iota(jnp.int32, (n, n), 1)
    m = {"le": c <= r, "lt": c < r, "ge": c >= r, "gt": c > r}[kind]
    return m


def _full(shape):
    nd = len(shape)
    return pl.BlockSpec(shape, lambda *_: (0,) * nd)


def _cparams(sem, vmem=VMEM_LIMIT):
    return pltpu.CompilerParams(dimension_semantics=sem, vmem_limit_bytes=vmem)


def _call(body, grid, name, in_specs, out_specs, out_shape, scratch, operands, exchange=None, carried=()):
    n_in, n_out, n_scr = len(in_specs), len(out_specs), len(scratch)
    if exchange is None:
        fn = body
    else:
        n = exchange.n

        def fn(*refs):
            ins, cin = refs[:n_in], refs[n_in:n_in + n]
            outs, cout = refs[n_in + n:n_in + n + n_out], refs[n_in + n + n_out:n_in + 2 * n + n_out]
            rest = refs[n_in + 2 * n + n_out:]
            scr, sems = rest[:n_scr], rest[n_scr:]

            @pl.when(pl.program_id(0) == 0)
            def _():
                exchange.start(cin, cout, sems)

            body(*ins, *outs, *scr)

            @pl.when(pl.program_id(0) == pl.num_programs(0) - 1)
            def _():
                exchange.wait(cin, cout, sems)

        any_spec = pl.BlockSpec(memory_space=pl.ANY)
        in_specs = list(in_specs) + [any_spec] * n
        out_specs = list(out_specs) + [any_spec] * n
        out_shape = list(out_shape) + exchange.out_shape
        scratch = list(scratch) + exchange.scratch
    return pl.pallas_call(fn, grid=grid, name=name, in_specs=in_specs, out_specs=out_specs, out_shape=out_shape,
                          scratch_shapes=scratch, compiler_params=_cparams(("arbitrary",)))(*operands, *carried)


def _fill_padded(pad_ref, val, nseg):
    zeros = jnp.zeros((nseg, 16, val.shape[-1]), F32)
    pad_ref[:, 0:16, :] = zeros
    pad_ref[:, 16 + GW:SEGP, :] = zeros
    pad_ref[:, 16:16 + GW, :] = val.reshape(nseg, GW, val.shape[-1])


def _tap_slabs(pad_ref, s, cs):
    whole = pad_ref[s, :, cs]
    for r in range(8):
        slab = whole if r == 0 else pltpu.roll(whole, SEGP - r, axis=0)
        for a in range(4):
            j = r + 8 * a - 1
            if 0 <= j < CW:
                yield j, slab[8 * a:8 * a + GW]


def _conv_taps(pad_ref, s, w_ref, c0, cw, flip):
    acc = jnp.zeros((GW, cw), F32)
    for j, rows in _tap_slabs(pad_ref, s, pl.ds(c0, cw)):
        acc = acc + w_ref[pl.ds((CW - 1 - j) if flip else j, 1), pl.ds(c0, cw)] * rows
    return acc


def _ln_stats(yb):
    mu = jnp.mean(yb, axis=-1, keepdims=True)
    yc = yb - mu
    var = jnp.mean(yc * yc, axis=-1, keepdims=True)
    rs = lax.rsqrt(var + EPS)
    return yc * rs, rs


def fwd_in(x, vec1, win, convw, cvec, wa2, ba, tm, exchange=None, carried=()):
    n = x.shape[0]
    nseg = tm // GW
    cg = 128

    def body(x_ref, vec_ref, win_ref, cw_ref, cv_ref, wa2_ref, ba_ref,
             ag_ref, yb_ref, co_ref, qk_ref, v_ref, g_ref, la_ref, r_ref, pad_ref):
        xx = x_ref[...]
        rstd = lax.rsqrt(jnp.mean(xx * xx, axis=-1, keepdims=True) + EPS)
        h = ((xx * rstd * vec_ref[0:1, :]) * (1.0 + vec_ref[2:3, :]) + vec_ref[1:2, :]).astype(BF)
        pc = _dot(h, win_ref[:, :2 * DC])
        ag_ref[...] = pc.astype(BF)
        _fill_padded(pad_ref, pc[:, :DC] * _sigmoid(pc[:, DC:]), nseg)
        for s in range(nseg):
            for c0 in range(0, DC, cg):
                y = _conv_taps(pad_ref, s, cw_ref, c0, cg, False)
                yb_ref[pl.ds(s * GW, GW), pl.ds(c0, cg)] = y + cv_ref[0:1, c0:c0 + cg]
        p = _dot(h, win_ref[:, 2 * DC:])
        qk_ref[...] = p[:, :2 * DK].astype(BF)
        v_ref[...] = p[:, 2 * DK:2 * DK + DV].astype(BF)
        g_ref[...] = p[:, 2 * DK + DV:2 * DK + 2 * DV].astype(BF)
        r = p[:, 2 * DK + 2 * DV:].astype(BF)
        r_ref[...] = r
        la_ref[...] = _log_sigmoid(_dot(r, wa2_ref[...]) + ba_ref[...]) * (1.0 / TAU)
        yn, _ = _ln_stats(yb_ref[...])
        ln = yn * cv_ref[1:2, :] + cv_ref[2:3, :]
        co_ref[...] = (ln * _sigmoid(ln)).astype(BF)

    tok = lambda w: pl.BlockSpec((tm, w), lambda i: (i, 0))
    return _call(
        body, (n // tm,), "fwd_in",
        [tok(D), _full(vec1.shape), _full(win.shape), _full(convw.shape), _full(cvec.shape), _full(wa2.shape), _full(ba.shape)],
        [tok(2 * DC), tok(DC), tok(DC), tok(2 * DK), tok(DV), tok(DV), tok(2 * DK), tok(128)],
        [jax.ShapeDtypeStruct((n, 2 * DC), BF), jax.ShapeDtypeStruct((n, DC), F32),
         jax.ShapeDtypeStruct((n, DC), BF), jax.ShapeDtypeStruct((n, 2 * DK), BF),
         jax.ShapeDtypeStruct((n, DV), BF), jax.ShapeDtypeStruct((n, DV), BF),
         jax.ShapeDtypeStruct((n, 2 * DK), F32), jax.ShapeDtypeStruct((n, 128), BF)],
        [pltpu.VMEM((nseg, SEGP, DC), F32)],
        (x, vec1, win, convw, cvec, wa2, ba), exchange, carried)


def _gla_dir(d):
    return (_tri(CH, "le"), CH - 1) if d == 0 else (_tri(CH, "ge"), 0)


def _gla_chunk_terms(qk, la, d):
    seen, last = _gla_dir(d)
    b = _mask_dot(seen.astype(BF), la)
    bl = b[last:last + 1, :]
    eb = jnp.exp(b)
    enb = jnp.exp(-b)
    ekd = jnp.exp(bl - b)
    ebl = jnp.exp(bl)
    q = qk[:, :DK].astype(F32) * QSCALE
    k = qk[:, DK:].astype(F32)
    return eb, enb, ekd, ebl, q * eb, k * enb, k * ekd


NP = NH // 2
PW = 2 * HK


def _lo_lanes(shape):
    return lax.broadcasted_iota(jnp.int32, shape, len(shape) - 1) < HK


def _pair_sel(lo, hi):
    return jnp.where(_lo_lanes(lo.shape), lo, hi)


def _only(x, which):
    keep = _lo_lanes(x.shape) if which == 0 else jnp.logical_not(_lo_lanes(x.shape))
    return jnp.where(keep, x, jnp.zeros_like(x))


def gla_fwd(qk, v, la, s0, tm):
    n = qk.shape[0]
    nt = n // tm
    nc = tm // CH

    def body(qkf_ref, vf_ref, laf_ref, qkb_ref, vb_ref, lab_ref, s0_ref, of_ref, ob_ref, sef_ref, seb_ref, st_ref):
        @pl.when(pl.program_id(0) == 0)
        def _():
            st_ref[...] = s0_ref[...]

        def chunk(ci, carry):
            t = []
            for d, (qk_ref, v_ref, la_ref) in enumerate(((qkf_ref, vf_ref, laf_ref), (qkb_ref, vb_ref, lab_ref))):
                c = ci if d == 0 else nc - 1 - ci
                rows = pl.ds(pl.multiple_of(c * CH, CH), CH)
                eb, enb, ekd, ebl, qt, kt, kd = _gla_chunk_terms(qk_ref[rows, :], la_ref[rows, :], d)
                t.append(dict(c=c, rows=rows, ebl=ebl, qt=qt.astype(BF), kt=kt.astype(BF), kd=kd.astype(BF),
                              vv=v_ref[rows, :], st=[st_ref[d, p] for p in range(NP)], amask=_gla_dir(d)[0]))
            dh = [(d, h) for d in range(2) for h in range(NH)]
            ps = lambda h: slice((h // 2) * PW, (h // 2 + 1) * PW)
            vs = lambda h: slice(h * HV, (h + 1) * HV)
            qm = {(d, h): _only(t[d]['qt'][:, ps(h)], h % 2) for d, h in dh}
            a = {(d, h): jnp.where(t[d]['amask'], _dot_nt(qm[d, h], t[d]['kt'][:, ps(h)]), 0.0).astype(BF) for d, h in dh}
            o = {(d, h): _dot(a[d, h], t[d]['vv'][:, vs(h)]) + _dot_nt(qm[d, h], t[d]['st'][h // 2].astype(BF))
                 for d, h in dh}
            kv = {(d, h): _dot_tn(t[d]['vv'][:, vs(h)], t[d]['kd'][:, ps(h)]) for d, h in dh}
            for d, (o_ref, se_ref) in enumerate(((of_ref, sef_ref), (ob_ref, seb_ref))):
                for h in range(NH):
                    o_ref[t[d]['rows'], vs(h)] = o[d, h].astype(BF)
                for p in range(NP):
                    se_ref[t[d]['c'], p] = t[d]['st'][p]
                    st_ref[d, p] = (t[d]['ebl'][:, p * PW:(p + 1) * PW] * t[d]['st'][p]
                                    + _pair_sel(kv[d, 2 * p], kv[d, 2 * p + 1]))
            return carry

        lax.fori_loop(0, nc, chunk, 0, unroll=4)

    fw = lambda w, col=0: pl.BlockSpec((tm, w), lambda i: (i, col))
    bw = lambda w, col=0: pl.BlockSpec((tm, w), lambda i: (nt - 1 - i, col))
    se_f = pl.BlockSpec((nc, NP, HV, PW), lambda i: (i, 0, 0, 0))
    se_b = pl.BlockSpec((nc, NP, HV, PW), lambda i: (nt - 1 - i, 0, 0, 0))
    se_shape = jax.ShapeDtypeStruct((n // CH, NP, HV, PW), F32)
    return pl.pallas_call(
        body, grid=(nt,), name="gla_fwd",
        in_specs=[fw(2 * DK), fw(DV), fw(DK, 0), bw(2 * DK), bw(DV), bw(DK, 1), _full(s0.shape)],
        out_specs=[fw(DV), bw(DV), se_f, se_b],
        out_shape=[jax.ShapeDtypeStruct((n, DV), BF), jax.ShapeDtypeStruct((n, DV), BF), se_shape, se_shape],
        scratch_shapes=[pltpu.VMEM((2, NP, HV, PW), F32)],
        compiler_params=_cparams(("arbitrary",)),
    )(qk, v, la, qk, v, la, s0)


def gla_bwd(qk, v, la, do, se_f, se_b, tm, exchange=None, carried=()):
    n = qk.shape[0]
    nt = n // tm
    nc = tm // CH

    def body(qkf_ref, vf_ref, laf_ref, dof_ref, sef_ref, qkb_ref, vb_ref, lab_ref, dob_ref, seb_ref,
             dqkf_ref, dvf_ref, dlaf_ref, dqkb_ref, dvb_ref, dlab_ref, ds0_ref, ds_ref):
        @pl.when(pl.program_id(0) == 0)
        def _():
            ds_ref[...] = jnp.zeros_like(ds_ref)

        def chunk(ci, carry):
            t = []
            for d, (qk_ref, v_ref, la_ref, do_ref, se_ref) in enumerate(
                    ((qkf_ref, vf_ref, laf_ref, dof_ref, sef_ref), (qkb_ref, vb_ref, lab_ref, dob_ref, seb_ref))):
                c = nc - 1 - ci if d == 0 else ci
                rows = pl.ds(pl.multiple_of(c * CH, CH), CH)
                amask, last = _gla_dir(d)
                eb, enb, ekd, ebl, qt, kt, kd = _gla_chunk_terms(qk_ref[rows, :], la_ref[rows, :], d)
                t.append(dict(rows=rows, amask=amask, last=last, eb=eb, enb=enb, ekd=ekd, ebl=ebl, qt=qt, kt=kt, kd=kd,
                              qtb=qt.astype(BF), ktb=kt.astype(BF), kdb=kd.astype(BF), vv=v_ref[rows, :], dd=do_ref[rows, :],
                              st=[se_ref[c, p] for p in range(NP)], dsn=[ds_ref[d, p] for p in range(NP)]))
            dh = [(d, h) for d in range(2) for h in range(NH)]
            dp = [(d, p) for d in range(2) for p in range(NP)]
            ps = lambda h: slice((h // 2) * PW, (h // 2 + 1) * PW)
            vs = lambda h: slice(h * HV, (h + 1) * HV)
            stb = {(d, p): t[d]['st'][p].astype(BF) for d, p in dp}
            dsnb = {(d, p): t[d]['dsn'][p].astype(BF) for d, p in dp}
            qm = {(d, h): _only(t[d]['qtb'][:, ps(h)], h % 2) for d, h in dh}
            km = {(d, h): _only(t[d]['kdb'][:, ps(h)], h % 2) for d, h in dh}
            a = {(d, h): jnp.where(t[d]['amask'], _dot_nt(qm[d, h], t[d]['ktb'][:, ps(h)]), 0.0).astype(BF) for d, h in dh}
            da = {(d, h): jnp.where(t[d]['amask'], _dot_nt(t[d]['dd'][:, vs(h)], t[d]['vv'][:, vs(h)]), 0.0).astype(BF)
                  for d, h in dh}
            dv = {(d, h): _dot_tn(a[d, h], t[d]['dd'][:, vs(h)]) + _dot_nt(km[d, h], dsnb[d, h // 2]) for d, h in dh}
            dkd = {(d, h): _dot(t[d]['vv'][:, vs(h)], dsnb[d, h // 2]) for d, h in dh}
            dqt = {(d, h): _dot(da[d, h], t[d]['ktb'][:, ps(h)]) + _dot(t[d]['dd'][:, vs(h)], stb[d, h // 2]) for d, h in dh}
            dkt = {(d, h): _dot_tn(da[d, h], t[d]['qtb'][:, ps(h)]) for d, h in dh}
            dsq = {(d, h): _dot_tn(t[d]['dd'][:, vs(h)], t[d]['qtb'][:, ps(h)]) for d, h in dh}
            for d, (dqk_ref, dv_ref, dla_ref) in enumerate(((dqkf_ref, dvf_ref, dlaf_ref), (dqkb_ref, dvb_ref, dlab_ref))):
                td = t[d]
                rows = td['rows']
                for h in range(NH):
                    dv_ref[rows, vs(h)] = dv[d, h].astype(BF)
                pair = lambda x: jnp.concatenate([_pair_sel(x[d, 2 * p], x[d, 2 * p + 1]) for p in range(NP)], axis=1)
                dqt_, dkt_, dkd_ = pair(dqt), pair(dkt), pair(dkd)
                debl = jnp.concatenate([jnp.sum(td['st'][p] * td['dsn'][p], axis=0, keepdims=True) for p in range(NP)], axis=1)
                for p in range(NP):
                    ds_ref[d, p] = _pair_sel(dsq[d, 2 * p], dsq[d, 2 * p + 1]) + td['ebl'][:, p * PW:(p + 1) * PW] * td['dsn'][p]
                dkdkd = dkd_ * td['kd']
                dbl = jnp.sum(dkdkd, axis=0, keepdims=True) + debl * td['ebl']
                is_last = lax.broadcasted_iota(jnp.int32, (CH, DK), 0) == td['last']
                db = dqt_ * td['qt'] - dkt_ * td['kt'] - dkdkd + jnp.where(is_last, dbl, 0.0)
                dqk_ref[rows, :] = jnp.concatenate([dqt_ * td['eb'] * QSCALE, dkt_ * td['enb'] + dkd_ * td['ekd']], axis=1).astype(BF)
                dla_ref[rows, :] = _mask_dot(_gla_dir(1 - d)[0].astype(BF), db)
            return carry

        lax.fori_loop(0, nc, chunk, 0, unroll=4)

        @pl.when(pl.program_id(0) == nt - 1)
        def _():
            ds0_ref[...] = ds_ref[...]

    up = lambda w, col=0: pl.BlockSpec((tm, w), lambda i: (i, col))
    dn = lambda w, col=0: pl.BlockSpec((tm, w), lambda i: (nt - 1 - i, col))
    se_up = pl.BlockSpec((nc, NP, HV, PW), lambda i: (i, 0, 0, 0))
    se_dn = pl.BlockSpec((nc, NP, HV, PW), lambda i: (nt - 1 - i, 0, 0, 0))
    return _call(
        body, (nt,), "gla_bwd",
        [dn(2 * DK), dn(DV), dn(DK, 0), dn(DV), se_dn, up(2 * DK), up(DV), up(DK, 1), up(DV), se_up],
        [dn(2 * DK), dn(DV), dn(DK), up(2 * DK), up(DV), up(DK), _full((2, NP, HV, PW))],
        [jax.ShapeDtypeStruct((n, 2 * DK), BF), jax.ShapeDtypeStruct((n, DV), BF),
         jax.ShapeDtypeStruct((n, DK), F32), jax.ShapeDtypeStruct((n, 2 * DK), BF),
         jax.ShapeDtypeStruct((n, DV), BF), jax.ShapeDtypeStruct((n, DK), F32),
         jax.ShapeDtypeStruct((2, NP, HV, PW), F32)],
        [pltpu.VMEM((2, NP, HV, PW), F32)],
        (qk, v, la, do, se_f, qk, v, la, do, se_b), exchange, carried)


def _head_norm(o):
    ons, rss = [], []
    for h in range(NH):
        oh = o[:, h * HV:(h + 1) * HV]
        rs = lax.rsqrt(jnp.mean(oh * oh, axis=-1, keepdims=True) + EPS)
        ons.append(oh * rs)
        rss.append(rs)
    return ons, rss


def merge_fwd(x, o_f, o_b, g, co, vecm, gn, wout, tm):
    n = x.shape[0]

    def body(x_ref, of_ref, ob_ref, g_ref, co_ref, vec_ref, gn_ref, w_ref, x1_ref, y1_ref, cat_ref):
        o = of_ref[...].astype(F32) + ob_ref[...].astype(F32)
        ons, _ = _head_norm(o)
        gg = g_ref[...].astype(F32)
        sil = gg * _sigmoid(gg)
        cat_ref[:, :DC] = co_ref[...]
        for h in range(NH):
            vs = slice(h * HV, (h + 1) * HV)
            cat_ref[:, DC + h * HV:DC + (h + 1) * HV] = (ons[h] * gn_ref[:, vs] * sil[:, vs]).astype(BF)
        y1 = _dot(cat_ref[...], w_ref[...])
        y1_ref[...] = y1.astype(BF)
        x1_ref[...] = x_ref[...] + vec_ref[0:1, :] * y1

    tok = lambda w: pl.BlockSpec((tm, w), lambda i: (i, 0))
    return pl.pallas_call(
        body, grid=(n // tm,), name="merge_fwd",
        in_specs=[tok(D), tok(DV), tok(DV), tok(DV), tok(DC), _full(vecm.shape), _full(gn.shape), _full(wout.shape)],
        out_specs=[tok(D), tok(D), tok(D)],
        out_shape=[jax.ShapeDtypeStruct((n, D), F32), jax.ShapeDtypeStruct((n, D), BF), jax.ShapeDtypeStruct((n, D), BF)],
        compiler_params=_cparams(("arbitrary",)),
    )(x, o_f, o_b, g, co, vecm, gn, wout)


def merge_bwd(dx1, y1, o_f, o_b, g, vecm, gn, wout, tm):
    n = dx1.shape[0]

    def body(dx1_ref, y1_ref, of_ref, ob_ref, g_ref, vec_ref, gn_ref, w_ref,
             dy1_ref, dco_ref, do_ref, dg_ref, s1_ref, s2_ref):
        @pl.when(pl.program_id(0) == 0)
        def _():
            s1_ref[...] = jnp.zeros_like(s1_ref)
            s2_ref[...] = jnp.zeros_like(s2_ref)

        dx1 = dx1_ref[...]
        s1_ref[...] += _colsum8(dx1 * y1_ref[...].astype(F32))
        dy1 = (dx1 * vec_ref[0:1, :]).astype(BF)
        dy1_ref[...] = dy1
        dcat = _dot_nt(dy1, w_ref[...])
        dco_ref[...] = dcat[:, :DC].astype(BF)
        o = of_ref[...].astype(F32) + ob_ref[...].astype(F32)
        ons, rss = _head_norm(o)
        gg = g_ref[...].astype(F32)
        sg = _sigmoid(gg)
        sil = gg * sg
        dsil = sg * (1.0 + gg * (1.0 - sg))
        for h in range(NH):
            vs = slice(h * HV, (h + 1) * HV)
            do2 = dcat[:, DC + h * HV:DC + (h + 1) * HV]
            gnh = gn_ref[:, vs]
            t = do2 * sil[:, vs]
            s2_ref[:, vs] += _colsum8(t * ons[h])
            don = t * gnh
            do_ref[:, vs] = (rss[h] * (don - ons[h] * jnp.mean(don * ons[h], axis=-1, keepdims=True))).astype(BF)
            dg_ref[:, vs] = (do2 * ons[h] * gnh * dsil[:, vs]).astype(BF)

    tok = lambda w: pl.BlockSpec((tm, w), lambda i: (i, 0))
    return pl.pallas_call(
        body, grid=(n // tm,), name="merge_bwd",
        in_specs=[tok(D), tok(D), tok(DV), tok(DV), tok(DV), _full(vecm.shape), _full(gn.shape), _full(wout.shape)],
        out_specs=[tok(D), tok(DC), tok(DV), tok(DV), _full((8, D)), _full((8, DV))],
        out_shape=[jax.ShapeDtypeStruct((n, D), BF), jax.ShapeDtypeStruct((n, DC), BF), jax.ShapeDtypeStruct((n, DV), BF),
                   jax.ShapeDtypeStruct((n, DV), BF), jax.ShapeDtypeStruct((8, D), F32), jax.ShapeDtypeStruct((8, DV), F32)],
        compiler_params=_cparams(("arbitrary",)),
    )(dx1, y1, o_f, o_b, g, vecm, gn, wout)


def ffn_fwd_bwd(x1, tgt, vecf, wg, wu, wd, tm):
    n = x1.shape[0]

    def body(x1_ref, t_ref, vec_ref, wg_ref, wu_ref, wd_ref,
             dx1_ref, h2_ref, act_ref, dgt_ref, dup_ref, dy2_ref, s_ref):
        @pl.when(pl.program_id(0) == 0)
        def _():
            s_ref[...] = jnp.zeros_like(s_ref)

        n2g, sh2, sc2, g2, fg = (vec_ref[i:i + 1, :] for i in range(5))
        x1 = x1_ref[...]
        r2 = lax.rsqrt(jnp.mean(x1 * x1, axis=-1, keepdims=True) + EPS)
        xn2 = x1 * r2
        h2 = (xn2 * n2g * (1.0 + sc2) + sh2).astype(BF)
        h2_ref[...] = h2
        gt = _dot(h2, wg_ref[...])
        up = _dot(h2, wu_ref[...])
        sg = _sigmoid(gt)
        sil = gt * sg
        act = (sil * up).astype(BF)
        act_ref[...] = act
        y2 = _dot(act, wd_ref[...])
        x2 = x1 + g2 * y2
        r3 = lax.rsqrt(jnp.mean(x2 * x2, axis=-1, keepdims=True) + EPS)
        xn3 = x2 * r3
        e = xn3 * fg - t_ref[...]
        s_ref[40:48, :] += _colsum8(e * e) * (0.5 / D)
        dyo = e * (1.0 / D)
        s_ref[0:8, :] += _colsum8(dyo * xn3)
        dxn3 = dyo * fg
        dx2 = r3 * (dxn3 - xn3 * jnp.mean(dxn3 * xn3, axis=-1, keepdims=True))
        s_ref[8:16, :] += _colsum8(dx2 * y2)
        dy2 = (dx2 * g2).astype(BF)
        dy2_ref[...] = dy2
        dact = _dot_nt(dy2, wd_ref[...])
        dup = (dact * sil).astype(BF)
        dgt = (dact * up * (sg * (1.0 + gt * (1.0 - sg)))).astype(BF)
        dup_ref[...] = dup
        dgt_ref[...] = dgt
        dh2 = _dot_nt(dgt, wg_ref[...]) + _dot_nt(dup, wu_ref[...])
        s_ref[16:24, :] += _colsum8(dh2)
        t = dh2 * xn2
        s_ref[24:32, :] += _colsum8(t * n2g)
        s_ref[32:40, :] += _colsum8(t * (1.0 + sc2))
        dxn2 = dh2 * ((1.0 + sc2) * n2g)
        dx1_ref[...] = dx2 + r2 * (dxn2 - xn2 * jnp.mean(dxn2 * xn2, axis=-1, keepdims=True))

    tok = lambda w: pl.BlockSpec((tm, w), lambda i: (i, 0))
    wspec = lambda a: pl.BlockSpec(a.shape, lambda i: (0, 0), pipeline_mode=pl.Buffered(1))
    return pl.pallas_call(
        body, grid=(n // tm,), name="ffn_fwd_bwd",
        in_specs=[tok(D), tok(D), _full(vecf.shape), wspec(wg), wspec(wu), wspec(wd)],
        out_specs=[tok(D), tok(D), tok(DFF), tok(DFF), tok(DFF), tok(D), _full((48, D))],
        out_shape=[jax.ShapeDtypeStruct((n, D), F32), jax.ShapeDtypeStruct((n, D), BF), jax.ShapeDtypeStruct((n, DFF), BF),
                   jax.ShapeDtypeStruct((n, DFF), BF), jax.ShapeDtypeStruct((n, DFF), BF), jax.ShapeDtypeStruct((n, D), BF),
                   jax.ShapeDtypeStruct((48, D), F32)],
        compiler_params=_cparams(("arbitrary",)),
    )(x1, tgt, vecf, wg, wu, wd)


def wgrad(a, b, init, t1, t2, tn, name):
    n, k1 = a.shape
    k2 = b.shape[1]

    def body(a_ref, b_ref, *rest):
        o_ref = rest[-1]

        @pl.when(pl.program_id(2) == 0)
        def _():
            o_ref[...] = rest[0][...] if init is not None else jnp.zeros_like(o_ref)

        o_ref[...] += _dot_tn(a_ref[...], b_ref[...])

    ospec = pl.BlockSpec((t1, t2), lambda i, j, k: (i, j))
    extra = ([ospec], {2: 0}, (init,)) if init is not None else ([], {}, ())
    return pl.pallas_call(
        body, grid=(k1 // t1, k2 // t2, n // tn), name=name,
        in_specs=[pl.BlockSpec((tn, t1), lambda i, j, k: (k, i)), pl.BlockSpec((tn, t2), lambda i, j, k: (k, j))] + extra[0],
        out_specs=ospec, out_shape=jax.ShapeDtypeStruct((k1, k2), F32), input_output_aliases=extra[1],
        compiler_params=_cparams(("parallel", "parallel", "arbitrary")),
    )(a, b, *extra[2])


def bwd_in(x, dx1, ag, yb, dco, dqk_f, dqk_b, dv_f, dv_b, dg, dla_f, dla_b, la, r, vec1, win, convw, cvec, wa2, tm):
    n = x.shape[0]
    nseg = tm // GW
    cg = 128

    def body(x_ref, dx1_ref, ag_ref, yb_ref, dco_ref, dqkf_ref, dqkb_ref, dvf_ref, dvb_ref, dg_ref, dlaf_ref, dlab_ref,
             la_ref, r_ref, vec_ref, win_ref, cw_ref, cv_ref, wa2_ref,
             gx_ref, h_ref, dp_ref, dwa2_ref, dcw_ref, s_ref, vc_ref, pad2_ref, dvc_ref, dcw8_ref):
        first = pl.program_id(0) == 0

        @pl.when(first)
        def _():
            s_ref[...] = jnp.zeros_like(s_ref)
            dwa2_ref[...] = jnp.zeros_like(dwa2_ref)
            dcw8_ref[...] = jnp.zeros_like(dcw8_ref)

        yn, rs = _ln_stats(yb_ref[...])
        lng = cv_ref[1:2, :]
        ln = yn * lng + cv_ref[2:3, :]
        sgl = _sigmoid(ln)
        dln = dco_ref[...].astype(F32) * (sgl * (1.0 + ln * (1.0 - sgl)))
        dyn = dln * lng
        dyb = rs * (dyn - jnp.mean(dyn, axis=-1, keepdims=True) - yn * jnp.mean(dyn * yn, axis=-1, keepdims=True))
        s_ref[24:32, 0:DC] += _colsum8(dyb)
        s_ref[24:32, DC:D] += _colsum8(dln * yn)
        s_ref[32:40, 0:DC] += _colsum8(dln)

        agv = ag_ref[...].astype(F32)
        a = agv[:, :DC]
        sgg = _sigmoid(agv[:, DC:])
        vc_ref[...] = a * sgg
        _fill_padded(pad2_ref, dyb, nseg)

        dp_ref[:, 2 * DC:2 * DC + 2 * DK] = (dqkf_ref[...].astype(F32) + dqkb_ref[...].astype(F32)).astype(BF)
        dp_ref[:, 2 * DC + 2 * DK:2 * DC + 2 * DK + DV] = (dvf_ref[...].astype(F32) + dvb_ref[...].astype(F32)).astype(BF)
        dp_ref[:, 2 * DC + 2 * DK + DV:2 * DC + 2 * DK + 2 * DV] = dg_ref[...]

        la = la_ref[...]
        dla = jnp.concatenate([dlaf_ref[...], dlab_ref[...]], axis=1)
        dpre = dla * (1.0 - jnp.exp(TAU * la)) * (1.0 / TAU)
        s_ref[32:40, DC:D] += _colsum8(dpre)
        dpreb = dpre.astype(BF)
        dwa2_ref[...] += _dot_tn(r_ref[...], dpreb)
        dp_ref[:, DINP - 128:] = _dot_nt(dpreb, wa2_ref[...]).astype(BF)
        dh_rest = _dot_nt(dp_ref[:, 2 * DC:], win_ref[:, 2 * DC:])

        for s in range(nseg):
            rows = pl.ds(s * GW, GW)
            for c0 in range(0, DC, cg):
                cs = pl.ds(c0, cg)
                vcs = vc_ref[rows, cs]
                acc = jnp.zeros((GW, cg), F32)
                for j, rows_j in _tap_slabs(pad2_ref, s, cs):
                    acc = acc + cw_ref[pl.ds(CW - 1 - j, 1), cs] * rows_j
                    dcw8_ref[CW - 1 - j, :, cs] += _colsum8(vcs * rows_j)
                dvc_ref[rows, cs] = acc
        dvc = dvc_ref[...]
        dp_ref[:, 0:DC] = (dvc * sgg).astype(BF)
        dp_ref[:, DC:2 * DC] = (dvc * a * sgg * (1.0 - sgg)).astype(BF)

        dh = dh_rest + _dot_nt(dp_ref[:, :2 * DC], win_ref[:, :2 * DC])
        xx = x_ref[...]
        n1g, sh1, sc1 = vec_ref[0:1, :], vec_ref[1:2, :], vec_ref[2:3, :]
        rstd = lax.rsqrt(jnp.mean(xx * xx, axis=-1, keepdims=True) + EPS)
        xn = xx * rstd
        h_ref[...] = (xn * n1g * (1.0 + sc1) + sh1).astype(BF)
        s_ref[0:8, :] += _colsum8(dh)
        t = dh * xn
        s_ref[8:16, :] += _colsum8(t * n1g)
        s_ref[16:24, :] += _colsum8(t * (1.0 + sc1))
        dxn = dh * ((1.0 + sc1) * n1g)
        gx_ref[...] = dx1_ref[...] + rstd * (dxn - xn * jnp.mean(dxn * xn, axis=-1, keepdims=True))

        @pl.when(pl.program_id(0) == pl.num_programs(0) - 1)
        def _():
            dcw_ref[...] = jnp.sum(dcw8_ref[...], axis=1)

    tok = lambda w: pl.BlockSpec((tm, w), lambda i: (i, 0))
    return pl.pallas_call(
        body, grid=(n // tm,), name="bwd_in",
        in_specs=[tok(D), tok(D), tok(2 * DC), tok(DC), tok(DC), tok(2 * DK), tok(2 * DK), tok(DV), tok(DV), tok(DV),
                  tok(DK), tok(DK), tok(2 * DK), tok(128), _full(vec1.shape),
                  pl.BlockSpec(win.shape, lambda i: (0, 0), pipeline_mode=pl.Buffered(1)),
                  _full(convw.shape), _full(cvec.shape), _full(wa2.shape)],
        out_specs=[tok(D), tok(D), tok(DINP), _full((128, 2 * DK)), _full((32, DC)), _full((40, D))],
        out_shape=[jax.ShapeDtypeStruct((n, D), F32), jax.ShapeDtypeStruct((n, D), BF), jax.ShapeDtypeStruct((n, DINP), BF),
                   jax.ShapeDtypeStruct((128, 2 * DK), F32), jax.ShapeDtypeStruct((32, DC), F32),
                   jax.ShapeDtypeStruct((40, D), F32)],
        scratch_shapes=[pltpu.VMEM((tm, DC), F32), pltpu.VMEM((nseg, SEGP, DC), F32), pltpu.VMEM((tm, DC), F32),
                        pltpu.VMEM((32, 8, DC), F32)],
        compiler_params=_cparams(("arbitrary",)),
    )(x, dx1, ag, yb, dco, dqk_f, dqk_b, dv_f, dv_b, dg, dla_f, dla_b, la, r, vec1, win, convw, cvec, wa2)


def _ctx_common(ctx_ref, vec_ref, win_ref, wa2_ref, ba_ref):
    cx = ctx_ref[...]
    t = cx.shape[0]
    rstd = lax.rsqrt(jnp.mean(cx * cx, axis=-1, keepdims=True) + EPS)
    xn = cx * rstd
    hc = (xn * vec_ref[0:1, :] * (1.0 + vec_ref[2:3, :]) + vec_ref[1:2, :]).astype(BF)
    k0 = 2 * DC + DK
    kv = _dot(hc, win_ref[:, k0:k0 + DK + DV]).astype(BF).astype(F32)
    r = _dot(hc, win_ref[:, DINP - 128:]).astype(BF)
    la = _log_sigmoid(_dot(r, wa2_ref[...]) + ba_ref[...]) * (1.0 / TAU)
    incl = _tri(t, "le").astype(BF)
    strict = _tri(t, "lt").astype(BF)
    bf = _mask_dot(incl, la[:, :DK])
    wf = jnp.exp(bf[t - 1:t, :] - bf)
    wb = jnp.exp(_mask_dot(strict, la[:, DK:]))
    return xn, hc, kv[:, :DK], kv[:, DK:], r, la, wf, wb


def ctx_fwd(ctx, vecc, win, wa2, ba):
    def body(ctx_ref, vec_ref, win_ref, wa2_ref, ba_ref, s_ref):
        _, _, k, v, _, _, wf, wb = _ctx_common(ctx_ref, vec_ref, win_ref, wa2_ref, ba_ref)
        vb = v.astype(BF)
        for d, w in enumerate((wf, wb)):
            kd = (k * w).astype(BF)
            for h in range(NH):
                s_ref[d, h // 2, :, (h % 2) * HK:(h % 2 + 1) * HK] = _dot_tn(vb[:, h * HV:(h + 1) * HV], kd[:, h * HK:(h + 1) * HK])

    return pl.pallas_call(
        body, name="ctx_fwd", out_shape=jax.ShapeDtypeStruct((2, NP, HV, PW), F32),
        compiler_params=pltpu.CompilerParams(vmem_limit_bytes=VMEM_LIMIT),
    )(ctx, vecc, win, wa2, ba)


def ctx_bwd(ctx, vecc, win, wa2, ba, ds0):
    t = ctx.shape[0]

    def body(ctx_ref, vec_ref, win_ref, wa2_ref, ba_ref, ds_ref, dwin_ref, dwa2_ref, s_ref, dpc_ref):
        xn, hc, k, v, r, la, wf, wb = _ctx_common(ctx_ref, vec_ref, win_ref, wa2_ref, ba_ref)
        vb = v.astype(BF)
        strict = _tri(t, "lt").astype(BF)
        strict_t = _tri(t, "gt").astype(BF)
        dpc_ref[...] = jnp.zeros_like(dpc_ref)
        k0 = 2 * DC + DK
        dk = jnp.zeros((t, DK), F32)
        des = []
        for d, w in enumerate((wf, wb)):
            kd = (k * w).astype(BF)
            dkds = []
            for h in range(NH):
                dsb = ds_ref[d, h // 2, :, (h % 2) * HK:(h % 2 + 1) * HK].astype(BF)
                dkds.append(_dot(vb[:, h * HV:(h + 1) * HV], dsb))
                dvh = _dot_nt(kd[:, h * HK:(h + 1) * HK], dsb)
                vs = slice(k0 + DK + h * HV, k0 + DK + (h + 1) * HV)
                if d == 0:
                    dpc_ref[:, vs] = dvh.astype(BF)
                else:
                    dpc_ref[:, vs] = (dpc_ref[:, vs].astype(F32) + dvh).astype(BF)
            dkd = jnp.concatenate(dkds, axis=1)
            dk = dk + dkd * w
            des.append(dkd * k * w)
        dpc_ref[:, k0:k0 + DK] = dk.astype(BF)
        dla = jnp.concatenate([_mask_dot(strict, des[0]), _mask_dot(strict_t, des[1])], axis=1)
        dpre = dla * (1.0 - jnp.exp(TAU * la)) * (1.0 / TAU)
        dpreb = dpre.astype(BF)
        dwa2_ref[...] = _dot_tn(r, dpreb)
        dpc_ref[:, DINP - 128:] = _dot_nt(dpreb, wa2_ref[...]).astype(BF)
        dpc = dpc_ref[...]
        dwin_ref[...] = _dot_tn(hc, dpc)
        dhc = _dot_nt(dpc, win_ref[...])
        n1g, sc1 = vec_ref[0:1, :], vec_ref[2:3, :]
        tt = dhc * xn
        s_ref[...] = jnp.zeros_like(s_ref)
        s_ref[0:1, :] = jnp.sum(tt * (1.0 + sc1), axis=0, keepdims=True)
        s_ref[1:2, :] = jnp.sum(dhc, axis=0, keepdims=True)
        s_ref[2:3, :] = jnp.sum(tt * n1g, axis=0, keepdims=True)
        s_ref[3:4, DC:D] = jnp.sum(dpre, axis=0, keepdims=True)

    return pl.pallas_call(
        body, name="ctx_bwd",
        out_shape=[jax.ShapeDtypeStruct((D, DINP), F32), jax.ShapeDtypeStruct((128, 2 * DK), F32),
                   jax.ShapeDtypeStruct((8, D), F32)],
        scratch_shapes=[pltpu.VMEM((t, DINP), BF)],
        compiler_params=pltpu.CompilerParams(vmem_limit_bytes=VMEM_LIMIT),
    )(ctx, vecc, win, wa2, ba, ds0)


def _silu(x):
    return x * _sigmoid(x)


def mod_bwd(cext, dm, wm):
    def body(c_ref, d_ref, w_ref, gw_ref, ds_ref):
        dmb = d_ref[...].astype(BF)
        gw_ref[...] = _dot_tn(_silu(c_ref[...]).astype(BF), dmb)
        ds_ref[...] = _dot_nt(dmb, w_ref[...].astype(BF))

    return pl.pallas_call(body, name="mod_bwd",
                          out_shape=[jax.ShapeDtypeStruct(wm.shape, F32), jax.ShapeDtypeStruct(cext.shape, F32)],
                          compiler_params=pltpu.CompilerParams(vmem_limit_bytes=VMEM_LIMIT))(cext, dm, wm)


def pack_small(sf, s1, s2, sd, sc, dcw, dwa2, dwa2_c):
    def body(sf_ref, s1_ref, s2_ref, sd_ref, sc_ref, dcw_ref, dwa2_ref, dwa2c_ref, o_ref, ocw_ref, owa_ref):
        rsum = lambda ref, i: jnp.sum(ref[8 * i:8 * i + 8, :], axis=0, keepdims=True)
        o_ref[...] = jnp.zeros_like(o_ref)
        o_ref[0:1, :] = rsum(sd_ref, 0)
        o_ref[1:2, :] = rsum(sd_ref, 1)
        o_ref[2:3, :] = rsum(s1_ref, 0)
        o_ref[3:4, :] = rsum(sf_ref, 2)
        o_ref[4:5, :] = rsum(sf_ref, 3)
        o_ref[5:6, :] = rsum(sf_ref, 1)
        o_ref[6:7, :] = sc_ref[1:2, :]
        o_ref[7:8, :] = sc_ref[2:3, :]
        o_ref[8:9, :] = rsum(sd_ref, 2) + sc_ref[0:1, :]
        o_ref[9:10, :] = rsum(sf_ref, 4)
        o_ref[10:11, :] = rsum(sf_ref, 0)
        o_ref[11:12, :] = rsum(sd_ref, 3)
        o_ref[12:13, :] = rsum(sd_ref, 4) + sc_ref[3:4, :]
        g = jnp.sum(s2_ref[...], axis=0, keepdims=True)
        o_ref[13:14, 0:HV] = g[:, 0:HV] + g[:, HV:2 * HV] + g[:, 2 * HV:3 * HV] + g[:, 3 * HV:4 * HV]
        o_ref[14:15, :] = rsum(sf_ref, 5)
        ocw_ref[...] = dcw_ref[...]
        owa_ref[...] = dwa2_ref[0:32, :] + dwa2c_ref[0:32, :]

    return pl.pallas_call(body, name="pack_small",
                          out_shape=[jax.ShapeDtypeStruct((16, D), F32), jax.ShapeDtypeStruct((32, DC), F32),
                                     jax.ShapeDtypeStruct((32, 2 * DK), F32)])(sf, s1, s2, sd, sc, dcw, dwa2, dwa2_c)


def small_totals(g8):
    r = g8.shape[1]

    def body(g_ref, t_ref, bm_ref, loss_ref):
        acc = g_ref[0]
        for i in range(1, NDEV):
            acc = acc + g_ref[i]
        t_ref[...] = acc
        bm_ref[...] = jnp.zeros_like(bm_ref)
        bm_ref[0:6, :] = acc[0:6, :]
        bm_ref[0:2, :] += acc[6:8, :]
        loss_ref[...] = jnp.broadcast_to(jnp.sum(acc[14:15, :], axis=1, keepdims=True), loss_ref.shape)

    return pl.pallas_call(body, name="small_totals",
                          out_shape=[jax.ShapeDtypeStruct((r, D), F32), jax.ShapeDtypeStruct((8, D), F32),
                                     jax.ShapeDtypeStruct((8, 128), F32)])(g8)


def cctx_grad(p8, c_ctx_row):
    def body(p_ref, c_ref, o_ref):
        acc = p_ref[0, 0:1, :]
        for j in range(1, NCHIP):
            acc = acc + p_ref[2 * j, 0:1, :]
        cc = c_ref[0:1, :]
        sg = _sigmoid(cc)
        o_ref[...] = jnp.zeros_like(o_ref)
        o_ref[0:1, :] = acc * (sg * (1.0 + cc * (1.0 - sg)))

    return pl.pallas_call(body, name="cctx_grad", out_shape=jax.ShapeDtypeStruct((8, D), F32))(p8, c_ctx_row)


def adamw(w, g, m, v, rows, name, emit_grad=False):
    r, c = w.shape

    def body(w_ref, g_ref, m_ref, v_ref, d_ref, nm_ref, nv_ref, *go_ref):
        gg = g_ref[...]
        nm = ADAM_B1 * m_ref[...] + (1.0 - ADAM_B1) * gg
        nv = ADAM_B2 * v_ref[...] + (1.0 - ADAM_B2) * (gg * gg)
        m_hat = nm / (1.0 - ADAM_B1 ** ADAM_STEP)
        v_hat = nv / (1.0 - ADAM_B2 ** ADAM_STEP)
        d_ref[...] = -ADAM_LR * (m_hat / (jnp.sqrt(v_hat) + ADAM_EPS) + ADAM_WD * w_ref[...])
        nm_ref[...] = nm
        nv_ref[...] = nv
        if emit_grad:
            go_ref[0][...] = gg

    spec = pl.BlockSpec((rows, c), lambda i: (i, 0))
    sds = jax.ShapeDtypeStruct((r, c), F32)
    nout = 4 if emit_grad else 3
    return pl.pallas_call(
        body, grid=(r // rows,), name=name, in_specs=[spec] * 4, out_specs=[spec] * nout, out_shape=[sds] * nout,
        compiler_params=_cparams(("parallel",)),
    )(w, g, m, v)


def _me():
    return lax.axis_index("x"), lax.axis_index("y"), lax.axis_index("c")


def _flip(v, bit):
    return 1 - v if bit else v


ANY = pl.BlockSpec(memory_space=pl.ANY)


def _gather8(x_ref, o_ref, ssem, rsem, lsem):
    mx, my, mc = _me()
    me = 4 * mx + 2 * my + mc
    local = pltpu.make_async_copy(x_ref, o_ref.at[me], lsem)
    local.start()
    peer = lambda k: (_flip(mx, k & 4), _flip(my, k & 2), _flip(mc, k & 1))
    sends = []
    for k in range(1, NDEV):
        cp = pltpu.make_async_remote_copy(src_ref=x_ref, dst_ref=o_ref.at[me], send_sem=ssem.at[k - 1],
                                          recv_sem=rsem.at[k - 1], device_id=peer(k), device_id_type=MESH)
        cp.start()
        sends.append(cp)
    for k in range(1, NDEV):
        px, py, pc = peer(k)
        pltpu.make_async_remote_copy(src_ref=x_ref, dst_ref=o_ref.at[4 * px + 2 * py + pc], send_sem=ssem.at[k - 1],
                                     recv_sem=rsem.at[k - 1], device_id=(px, py, pc), device_id_type=MESH).wait_recv()
    for cp in sends:
        cp.wait_send()
    local.wait()


def _gather8_sems():
    return [pltpu.SemaphoreType.DMA((NDEV - 1,)), pltpu.SemaphoreType.DMA((NDEV - 1,)), pltpu.SemaphoreType.DMA]


def all_gather8(x, name):
    vm = pl.BlockSpec(memory_space=pltpu.VMEM)
    return pl.pallas_call(_gather8_body(), name=name, in_specs=[vm], out_specs=vm,
                          out_shape=jax.ShapeDtypeStruct((NDEV,) + x.shape, x.dtype), scratch_shapes=_gather8_sems())(x)


def _gather8_body():
    def body(x_ref, o_ref, ssem, rsem, lsem):
        _gather8(x_ref, o_ref, ssem, rsem, lsem)
    return body


def prologue(small, c_ctx_rows, wm, bm, w_in_shard):
    ex = ChipExchange("gather", [w_in_shard])

    def body(s_ref, cc_ref, w_ref, b_ref, win_ref, s8_ref, m8_ref, wing_ref, mloc_ref, *sems):
        ex.start([win_ref], [wing_ref], sems[6:])
        _gather8(s_ref, s8_ref, *sems[0:3])
        cext = jnp.concatenate([s8_ref[:, 0, :], cc_ref[...]], axis=0)
        mloc_ref[...] = _dot(_silu(cext).astype(BF), w_ref[...].astype(BF)) + b_ref[...]
        _gather8(mloc_ref, m8_ref, *sems[3:6])
        ex.wait([win_ref], [wing_ref], sems[6:])

    vm = pl.BlockSpec(memory_space=pltpu.VMEM)
    wcols = wm.shape[1]
    return pl.pallas_call(
        body, name="prologue", in_specs=[vm, vm, vm, vm, ANY], out_specs=[vm, vm, ANY],
        out_shape=[jax.ShapeDtypeStruct((NDEV, 16, D), F32), jax.ShapeDtypeStruct((NDEV, 16, wcols), F32)] + ex.out_shape,
        scratch_shapes=[pltpu.VMEM((16, wcols), F32)] + _gather8_sems() + _gather8_sems() + ex.scratch,
        compiler_params=pltpu.CompilerParams(vmem_limit_bytes=VMEM_LIMIT),
    )(small, c_ctx_rows, wm, bm, w_in_shard)


def _chip_peers(mx, my):
    out = []
    for p in range(1, NCHIP):
        px, py = _flip(mx, p & 2), _flip(my, p & 1)
        out.append((px, py, 2 * px + py))
    return out


class ChipExchange:
    def __init__(self, kind, arrays):
        self.kind = kind
        self.n = len(arrays)
        if kind == "gather":
            self.out_shape = [jax.ShapeDtypeStruct((NCHIP,) + a.shape, a.dtype) for a in arrays]
        else:
            self.out_shape = [jax.ShapeDtypeStruct(a.shape, a.dtype) for a in arrays]
        self.scratch = [pltpu.SemaphoreType.DMA((3 * self.n,)), pltpu.SemaphoreType.DMA((3 * self.n,)),
                        pltpu.SemaphoreType.DMA((self.n,))]

    def _copies(self, ins, outs, sems):
        ssem, rsem, lsem = sems
        mx, my, mc = _me()
        jme = 2 * mx + my
        gather = self.kind == "gather"
        local, sends, waits = [], [], []
        for k in range(self.n):
            local.append(pltpu.make_async_copy(ins[k] if gather else ins[k].at[jme], outs[k].at[jme], lsem.at[k]))
            for p, (px, py, jp) in enumerate(_chip_peers(mx, my)):
                src = ins[k] if gather else ins[k].at[jp]
                sem = dict(send_sem=ssem.at[3 * k + p], recv_sem=rsem.at[3 * k + p], device_id=(px, py, mc),
                           device_id_type=MESH)
                sends.append(pltpu.make_async_remote_copy(src_ref=src, dst_ref=outs[k].at[jme], **sem))
                waits.append(pltpu.make_async_remote_copy(src_ref=src, dst_ref=outs[k].at[jp], **sem))
        return local, sends, waits

    def start(self, ins, outs, sems):
        local, sends, _ = self._copies(ins, outs, sems)
        for cp in local + sends:
            cp.start()

    def wait(self, ins, outs, sems):
        local, _, waits = self._copies(ins, outs, sems)
        for cp in waits:
            cp.wait_recv()
        for cp in waits:
            cp.wait_send()
        for cp in local:
            cp.wait()


def chip_exchange(kind, arrays, name):
    ex = ChipExchange(kind, arrays)
    n = ex.n

    def body(*refs):
        ins, outs, sems = refs[:n], refs[n:2 * n], refs[2 * n:]
        ex.start(ins, outs, sems)
        ex.wait(ins, outs, sems)

    return pl.pallas_call(body, name=name, in_specs=[ANY] * n, out_specs=[ANY] * n, out_shape=ex.out_shape,
                          scratch_shapes=ex.scratch)(*arrays)


def sibling_add(g, ngrp, hr, tr, name):
    c_ = g.shape[1]
    nt = hr // tr

    def body(cidx, keep_ref, give_ref, o_ref, land, ssem, rsem):
        mx, my, mc = _me()
        t = pl.program_id(0) * nt + pl.program_id(1)
        s = t % 2
        cp = pltpu.make_async_remote_copy(src_ref=give_ref, dst_ref=land.at[s], send_sem=ssem.at[s], recv_sem=rsem.at[s],
                                          device_id=(mx, my, 1 - mc), device_id_type=MESH)
        cp.start()
        cp.wait_recv()
        o_ref[...] = keep_ref[...] + land[s]
        cp.wait_send()

    grid_spec = pltpu.PrefetchScalarGridSpec(
        num_scalar_prefetch=1, grid=(ngrp, nt),
        in_specs=[pl.BlockSpec((tr, c_), lambda i, j, cr: ((2 * i + cr[0]) * nt + j, 0)),
                  pl.BlockSpec((tr, c_), lambda i, j, cr: ((2 * i + 1 - cr[0]) * nt + j, 0))],
        out_specs=pl.BlockSpec((tr, c_), lambda i, j, cr: (i * nt + j, 0)),
        scratch_shapes=[pltpu.VMEM((2, tr, c_), F32), pltpu.SemaphoreType.DMA((2,)), pltpu.SemaphoreType.DMA((2,))])
    cidx = lax.axis_index("c").astype(jnp.int32).reshape(1)
    return pl.pallas_call(body, grid_spec=grid_spec, name=name, out_shape=jax.ShapeDtypeStruct((ngrp * hr, c_), F32),
                          compiler_params=_cparams(("arbitrary", "arbitrary")))(cidx, g, g)


def finish_grad(b, tr, name):
    _, r2, c_ = b.shape

    def body(b_ref, g_ref, mine, land, ssem, rsem):
        mx, my, mc = _me()
        t = pl.program_id(0)
        s = t % 2
        mine[s] = (b_ref[0].astype(F32) + b_ref[1].astype(F32)) + (b_ref[2].astype(F32) + b_ref[3].astype(F32))
        cp = pltpu.make_async_remote_copy(src_ref=mine.at[s], dst_ref=land.at[s], send_sem=ssem.at[s], recv_sem=rsem.at[s],
                                          device_id=(mx, my, 1 - mc), device_id_type=MESH)
        cp.start()
        cp.wait_recv()
        g_ref[mc] = mine[s]
        g_ref[1 - mc] = land[s]
        cp.wait_send()

    return pl.pallas_call(
        body, grid=(r2 // tr,), name=name,
        in_specs=[pl.BlockSpec((NCHIP, tr, c_), lambda i: (0, i, 0))],
        out_specs=pl.BlockSpec((2, tr, c_), lambda i: (0, i, 0)), out_shape=jax.ShapeDtypeStruct((2, r2, c_), F32),
        scratch_shapes=[pltpu.VMEM((2, tr, c_), F32), pltpu.VMEM((2, tr, c_), F32), pltpu.SemaphoreType.DMA((2,)),
                        pltpu.SemaphoreType.DMA((2,))],
        compiler_params=_cparams(("arbitrary",)))(b)


TM_IN = 512
TM_GLA = 512
TM_MERGE = 512
TM_FFN = 256
TN_WGRAD = 2048

WEIGHTS = ['c_ctx', 'w_mod', 'b_mod', 'norm1_g', 'norm2_g', 'w_in', 'conv_w', 'conv_b', 'conv_ln_g', 'conv_ln_b', 'w_a2_f',
           'b_a_f', 'w_a2_b', 'b_a_b', 'gla_norm_g', 'w_out', 'w_gate', 'w_up', 'w_down', 'final_g']
BIG = ['w_in', 'w_out', 'w_gate', 'w_up', 'w_down']


def _rows(*vs):
    w = vs[0].size
    row = lax.broadcasted_iota(jnp.int32, (8, w), 0)
    out = jnp.zeros((8, w), F32)
    for i, v in enumerate(vs):
        out = jnp.where(row == i, v.reshape(1, w), out)
    return out


def _small_slab(p):
    cat = lambda *ks: jnp.concatenate([p[k].reshape(-1) for k in ks])
    vecs = _rows(p['c_ctx'], p['norm1_g'], p['norm2_g'], p['final_g'], cat('conv_b', 'conv_ln_g'),
                 cat('conv_ln_b', 'b_a_f', 'b_a_b'), jnp.pad(p['gla_norm_g'].reshape(-1), (0, D - HV)))
    bmod = jnp.pad(p['b_mod'].reshape(6, D), ((0, 2), (0, 0)))
    shards = jnp.pad(jnp.concatenate([jnp.pad(p['conv_w'].reshape(-1), (0, DC // NCHIP)), cat('w_a2_f', 'w_a2_b')]),
                     (0, 2 * D)).reshape(8, D)
    return jnp.concatenate([vecs, bmod, shards], axis=0)


def _unslab(s):
    return {
        'c_ctx': s[0], 'norm1_g': s[1:2], 'norm2_g': s[2:3], 'final_g': s[3],
        'conv_b': s[4:5, :DC], 'conv_ln_g': s[4:5, DC:], 'conv_ln_b': s[5:6, :DC],
        'b_a_f': s[5:6, DC:DC + DK], 'b_a_b': s[5:6, DC + DK:], 'gla_norm_g': s[6:7, :HV],
        'b_mod': s[8:14].reshape(1, 6 * D),
        'conv_w': s[16:20].reshape(32, DC // NCHIP)[:CW].reshape(1, CW, DC // NCHIP),
        'w_a2_f': s[20].reshape(1, RANK, DK // NCHIP), 'w_a2_b': s[21].reshape(1, RANK, DK // NCHIP),
    }


def kernel(x, c, ctx, c_ctx, w_mod, b_mod, norm1_g, norm2_g, w_in, conv_w, conv_b, conv_ln_g, conv_ln_b, w_a2_f, b_a_f, w_a2_b, b_a_b, gla_norm_g, w_out, w_gate, w_up, w_down, final_g, loss_target, m_c_ctx, m_w_mod, m_b_mod, m_norm1_g, m_norm2_g, m_w_in, m_conv_w, m_conv_b, m_conv_ln_g, m_conv_ln_b, m_w_a2_f, m_b_a_f, m_w_a2_b, m_b_a_b, m_gla_norm_g, m_w_out, m_w_gate, m_w_up, m_w_down, m_final_g, v_c_ctx, v_w_mod, v_b_mod, v_norm1_g, v_norm2_g, v_w_in, v_conv_w, v_conv_b, v_conv_ln_g, v_conv_ln_b, v_w_a2_f, v_b_a_f, v_w_a2_b, v_b_a_b, v_gla_norm_g, v_w_out, v_w_gate, v_w_up, v_w_down, v_final_g):
    w = dict(c_ctx=c_ctx, w_mod=w_mod, b_mod=b_mod, norm1_g=norm1_g, norm2_g=norm2_g, w_in=w_in, conv_w=conv_w, conv_b=conv_b,
             conv_ln_g=conv_ln_g, conv_ln_b=conv_ln_b, w_a2_f=w_a2_f, b_a_f=b_a_f, w_a2_b=w_a2_b, b_a_b=b_a_b,
             gla_norm_g=gla_norm_g, w_out=w_out, w_gate=w_gate, w_up=w_up, w_down=w_down, final_g=final_g)
    m = dict(c_ctx=m_c_ctx, w_mod=m_w_mod, b_mod=m_b_mod, norm1_g=m_norm1_g, norm2_g=m_norm2_g, w_in=m_w_in, conv_w=m_conv_w,
             conv_b=m_conv_b, conv_ln_g=m_conv_ln_g, conv_ln_b=m_conv_ln_b, w_a2_f=m_w_a2_f, b_a_f=m_b_a_f, w_a2_b=m_w_a2_b,
             b_a_b=m_b_a_b, gla_norm_g=m_gla_norm_g, w_out=m_w_out, w_gate=m_w_gate, w_up=m_w_up, w_down=m_w_down,
             final_g=m_final_g)
    v = dict(c_ctx=v_c_ctx, w_mod=v_w_mod, b_mod=v_b_mod, norm1_g=v_norm1_g, norm2_g=v_norm2_g, w_in=v_w_in, conv_w=v_conv_w,
             conv_b=v_conv_b, conv_ln_g=v_conv_ln_g, conv_ln_b=v_conv_ln_b, w_a2_f=v_w_a2_f, b_a_f=v_b_a_f, w_a2_b=v_w_a2_b,
             b_a_b=v_b_a_b, gla_norm_g=v_gla_norm_g, w_out=v_w_out, w_gate=v_w_gate, w_up=v_w_up, w_down=v_w_down,
             final_g=v_final_g)
    mx, my, mc = _me()
    jme = 2 * mx + my
    me = 4 * mx + 2 * my + mc
    wmc = D * 6 // NCHIP
    xx, tgt, cx = x[0], loss_target[0], ctx[0]

    bshard = [w[k][0].astype(BF) for k in BIG]
    sw = jnp.concatenate([jnp.pad(conv_w[0], ((0, 1), (0, 0))).reshape(-1), w_a2_f[0].reshape(-1), w_a2_b[0].reshape(-1)])
    small = jnp.concatenate([_rows(c[0]), jnp.pad(sw.reshape(6, D), ((0, 2), (0, 0)))], axis=0)
    cs8, mall, win_g = prologue(small, _rows(c_ctx), w_mod[0], lax.dynamic_slice_in_dim(b_mod, jme * wmc, wmc, axis=1),
                                bshard[0])
    cext = jnp.concatenate([cs8[:, 0, :], _rows(c_ctx)], axis=0)
    swc = jnp.stack([cs8[2 * j, 8:16] for j in range(NCHIP)]).reshape(NCHIP, 8 * D)
    convw = jnp.transpose(swc[:, :32 * 128].reshape(NCHIP, 32, 128), (1, 0, 2)).reshape(32, DC)
    a2 = lambda o: jnp.transpose(swc[:, o:o + RANK * 64].reshape(NCHIP, RANK, 64), (1, 0, 2)).reshape(RANK, DK)
    wa2 = jnp.zeros((128, 2 * DK), F32).at[0:RANK, 0:DK].set(a2(32 * 128)).at[RANK:2 * RANK, DK:].set(a2(32 * 128 + RANK * 64))
    wa2 = wa2.astype(BF)
    mall = jnp.concatenate([mall[2 * j] for j in range(NCHIP)], axis=1)
    sh1, sc1, g1, sh2, sc2, g2 = jnp.split(lax.dynamic_slice_in_dim(mall, me, 1, axis=0)[0], 6)
    csh1, csc1 = mall[8, :D], mall[8, D:2 * D]
    cols = lambda a: jnp.transpose(a, (1, 0, 2)).reshape(a.shape[1], -1)
    win = jnp.pad(cols(win_g), ((0, 0), (0, DINP - DIN)))
    ba = jnp.concatenate([b_a_f, b_a_b], axis=1)
    cvec = _rows(conv_b, conv_ln_g, conv_ln_b)
    vec1 = _rows(norm1_g, sh1, sc1)
    vecc = _rows(norm1_g, csh1, csc1)
    vecm = _rows(g1)
    vecf = _rows(norm2_g, sh2, sc2, g2, final_g)
    gn = jnp.tile(gla_norm_g, (1, NH))

    s0 = ctx_fwd(cx, vecc, win, wa2, ba)
    res = fwd_in(xx, vec1, win, convw, cvec, wa2, ba, TM_IN, ChipExchange("gather", bshard[1:]), bshard[1:])
    ag, yb, co, qk, vv, gg, la, r = res[:8]
    wout = res[8].reshape(D, D)
    wg, wu = cols(res[9]), cols(res[10])
    wd = res[11].reshape(DFF, D)
    o_f, o_b, se_f, se_b = gla_fwd(qk, vv, la, s0, TM_GLA)
    x1, y1, cat = merge_fwd(xx, o_f, o_b, gg, co, vecm, gn, wout, TM_MERGE)

    dx1, h2, act, dgt, dup, dy2, sf = ffn_fwd_bwd(x1, tgt, vecf, wg, wu, wd, TM_FFN)
    d_wg = wgrad(h2, dgt, None, D, DFF // 2, TN_WGRAD, "wgrad_gate")
    d_wu = wgrad(h2, dup, None, D, DFF // 2, TN_WGRAD, "wgrad_up")
    d_wd = wgrad(act, dy2, None, DFF // 2, D, TN_WGRAD, "wgrad_down")
    dy1, dco, do, dg, s1, s2 = merge_bwd(dx1, y1, o_f, o_b, gg, vecm, gn, wout, TM_MERGE)
    d_wout = wgrad(cat, dy1, None, D, D, TN_WGRAD, "wgrad_out")

    shard = lambda a, k: jnp.transpose(a.reshape(a.shape[0], NCHIP, k), (1, 0, 2))
    hd = D // 2
    parts = [sibling_add(d_wout, NCHIP, hd // NCHIP, hd // NCHIP, "xadd_w_out").reshape(NCHIP, hd // NCHIP, D),
             shard(sibling_add(d_wg, 1, hd, hd // 2, "xadd_w_gate"), DFF // NCHIP),
             shard(sibling_add(d_wu, 1, hd, hd // 2, "xadd_w_up"), DFF // NCHIP),
             sibling_add(d_wd, NCHIP, DFF // 8, DFF // 8, "xadd_w_down").reshape(NCHIP, DFF // 8, D)]
    parts = [p.astype(BF) for p in parts]
    res = gla_bwd(qk, vv, la, do, se_f, se_b, TM_GLA, ChipExchange("scatter", parts), parts)
    dqk_f, dv_f, dla_f, dqk_b, dv_b, dla_b, ds0 = res[:7]
    recv = list(res[7:])
    dwin_c, dwa2_c, sc = ctx_bwd(cx, vecc, win, wa2, ba, ds0)
    grad_x, h, dp, dwa2, dcw, sd = bwd_in(xx, dx1, ag, yb, dco, dqk_f, dqk_b, dv_f, dv_b, dg, dla_f, dla_b, la, r,
                                          vec1, win, convw, cvec, wa2, TM_IN)
    d_win = wgrad(h, dp, dwin_c, D, DINP // 3, TN_WGRAD, "wgrad_in")
    part_in = shard(sibling_add(d_win, 1, hd, hd // 2, "xadd_w_in")[:, :DIN], DIN // NCHIP).astype(BF)
    recv = list(chip_exchange("scatter", [part_in], "scatter_w_in")) + recv

    rows16, dcw_t, dwa2_t = pack_small(sf, s1, s2, sd, sc, dcw, dwa2, dwa2_c)
    sp = jnp.concatenate([rows16, dcw_t.reshape(16, D), dwa2_t.reshape(16, D)], axis=0)
    g8 = all_gather8(sp, "gather_small_grads")
    tot, bm_g, loss8 = small_totals(g8)
    loss = loss8[0, 0]
    dmod8 = g8[:, 0:6, :].reshape(NDEV, 6 * D)
    dmodc = jnp.concatenate([tot[6], tot[7], jnp.zeros((4 * D,), F32)])
    dm = jnp.concatenate([dmod8, _rows(dmodc)], axis=0)
    dm = lax.dynamic_slice_in_dim(dm, jme * wmc, wmc, axis=1)
    g_wmod, dsil = mod_bwd(cext, dm, w_mod[0])
    p8 = all_gather8(dsil[8:16], "gather_dsilu")
    g_cctx = cctx_grad(p8, _rows(c_ctx))[0]

    grads, delta, new_m, new_v = {}, {}, {}, {}
    for i, k in enumerate(BIG):
        r2 = recv[i].shape[1]
        gk = finish_grad(recv[i], r2 // 2 if r2 >= 512 else r2, "finish_" + k).reshape(w[k].shape[1:])
        rk = gk.shape[0]
        outs = adamw(w[k][0], gk, m[k][0], v[k][0], rk // 2 if rk >= 512 else rk, "adamw_" + k, emit_grad=True)
        delta[k], new_m[k], new_v[k], grads[k] = (o[None] for o in outs)
    grads['w_mod'] = g_wmod[None]
    d_, m_, v_ = adamw(w_mod[0], g_wmod, m_w_mod[0], v_w_mod[0], 256, "adamw_w_mod")
    delta['w_mod'], new_m['w_mod'], new_v['w_mod'] = d_[None], m_[None], v_[None]
    small_g = {
        'c_ctx': g_cctx, 'b_mod': bm_g[0:6].reshape(1, 6 * D), 'norm1_g': tot[8:9], 'norm2_g': tot[9:10], 'final_g': tot[10],
        'conv_b': tot[11:12, :DC], 'conv_ln_g': tot[11:12, DC:], 'conv_ln_b': tot[12:13, :DC],
        'b_a_f': tot[12:13, DC:DC + DK], 'b_a_b': tot[12:13, DC + DK:], 'gla_norm_g': tot[13:14, :HV],
        'conv_w': lax.dynamic_slice_in_dim(tot[16:32].reshape(32, DC)[:CW], jme * (DC // NCHIP), DC // NCHIP, axis=1)[None],
        'w_a2_f': lax.dynamic_slice_in_dim(tot[32:48].reshape(32, 2 * DK)[0:RANK, 0:DK], jme * (DK // NCHIP), DK // NCHIP, axis=1)[None],
        'w_a2_b': lax.dynamic_slice_in_dim(tot[32:48].reshape(32, 2 * DK)[RANK:2 * RANK, DK:], jme * (DK // NCHIP), DK // NCHIP, axis=1)[None],
    }
    grads.update(small_g)
    sd_, sm_, sv_ = adamw(_small_slab(w), _small_slab(small_g), _small_slab(m), _small_slab(v), 24,
                          "adamw_small")
    for dst, slab in ((delta, sd_), (new_m, sm_), (new_v, sv_)):
        dst.update(_unslab(slab))
    out = [loss, grad_x[None]]
    for group in (grads, delta, new_m, new_v):
        out += [group[k].reshape(w[k].shape) for k in WEIGHTS]
    return tuple(out)
```

```python
import jax
import jax.numpy as jnp
from jax import lax
from jax.experimental import pallas as pl
from jax.experimental.pallas import tpu as pltpu

F32 = jnp.float32
BF = jnp.bfloat16

D = 1024
DC = 512
NH = 4
HK = 64
HV = 128
DK = NH * HK
DV = NH * HV
RANK = 16
CH = 64
GW = 64
CW = 31
SEGP = GW + 32
DFF = 2816
DIN = 2592
DINP = 2688
EPS = 1e-6
TAU = 16.0
QSCALE = HK ** -0.5
NCHIP = 4
NDEV = 8

ADAM_LR = 0.001
ADAM_B1 = 0.9
ADAM_B2 = 0.999
ADAM_EPS = 1e-08
ADAM_WD = 0.01
ADAM_STEP = 10

VMEM_LIMIT = 56 * 1024 * 1024
MESH = pl.DeviceIdType.MESH


def _dot(a, b):
    return jnp.dot(a, b, preferred_element_type=F32)


def _dot_nt(a, b):
    return lax.dot_general(a, b, (((1,), (1,)), ((), ())), preferred_element_type=F32)


def _dot_tn(a, b):
    return lax.dot_general(a, b, (((0,), (0,)), ((), ())), preferred_element_type=F32)


def _split3(x):
    hi = x.astype(BF)
    r1 = x - hi.astype(F32)
    mid = r1.astype(BF)
    lo = (r1 - mid.astype(F32)).astype(BF)
    return hi, mid, lo


def _mask_dot(t, x):
    hi, mid, lo = _split3(x)
    return _dot(t, hi) + _dot(t, mid) + _dot(t, lo)


def _sigmoid(x):
    return 1.0 / (1.0 + jnp.exp(-x))


def _log_sigmoid(x):
    return jnp.minimum(x, 0.0) - jnp.log(1.0 + jnp.exp(-jnp.abs(x)))


def _colsum8(z):
    t, c = z.shape
    return jnp.sum(z.reshape(t // 8, 8, c), axis=0)


def _tri(n, kind):
    r = lax.broadcasted_iota(jnp.int32, (n, n), 0)
    c = lax.broadcasted_iota(jnp.int32, (n, n), 1)
    m = {"le": c <= r, "lt": c < r, "ge": c >= r, "gt": c > r}[kind]
    return m


def _full(shape):
    nd = len(shape)
    return pl.BlockSpec(shape, lambda *_: (0,) * nd)


def _cparams(sem, vmem=VMEM_LIMIT):
    return pltpu.CompilerParams(dimension_semantics=sem, vmem_limit_bytes=vmem)


def _call(body, grid, name, in_specs, out_specs, out_shape, scratch, operands, exchange=None, carried=()):
    n_in, n_out, n_scr = len(in_specs), len(out_specs), len(scratch)
    if exchange is None:
        fn = body
    else:
        n = exchange.n

        def fn(*refs):
            ins, cin = refs[:n_in], refs[n_in:n_in + n]
            outs, cout = refs[n_in + n:n_in + n + n_out], refs[n_in + n + n_out:n_in + 2 * n + n_out]
            rest = refs[n_in + 2 * n + n_out:]
            scr, sems = rest[:n_scr], rest[n_scr:]

            @pl.when(pl.program_id(0) == 0)
            def _():
                exchange.start(cin, cout, sems)

            body(*ins, *outs, *scr)

            @pl.when(pl.program_id(0) == pl.num_programs(0) - 1)
            def _():
                exchange.wait(cin, cout, sems)

        any_spec = pl.BlockSpec(memory_space=pl.ANY)
        in_specs = list(in_specs) + [any_spec] * n
        out_specs = list(out_specs) + [any_spec] * n
        out_shape = list(out_shape) + exchange.out_shape
        scratch = list(scratch) + exchange.scratch
    return pl.pallas_call(fn, grid=grid, name=name, in_specs=in_specs, out_specs=out_specs, out_shape=out_shape,
                          scratch_shapes=scratch, compiler_params=_cparams(("arbitrary",)))(*operands, *carried)


def _fill_padded(pad_ref, val, nseg):
    zeros = jnp.zeros((nseg, 16, val.shape[-1]), F32)
    pad_ref[:, 0:16, :] = zeros
    pad_ref[:, 16 + GW:SEGP, :] = zeros
    pad_ref[:, 16:16 + GW, :] = val.reshape(nseg, GW, val.shape[-1])


def _tap_slabs(pad_ref, s, cs):
    whole = pad_ref[s, :, cs]
    for r in range(8):
        slab = whole if r == 0 else pltpu.roll(whole, SEGP - r, axis=0)
        for a in range(4):
            j = r + 8 * a - 1
            if 0 <= j < CW:
                yield j, slab[8 * a:8 * a + GW]


def _conv_taps(pad_ref, s, w_ref, c0, cw, flip):
    acc = jnp.zeros((GW, cw), F32)
    for j, rows in _tap_slabs(pad_ref, s, pl.ds(c0, cw)):
        acc = acc + w_ref[pl.ds((CW - 1 - j) if flip else j, 1), pl.ds(c0, cw)] * rows
    return acc


def _ln_stats(yb):
    mu = jnp.mean(yb, axis=-1, keepdims=True)
    yc = yb - mu
    var = jnp.mean(yc * yc, axis=-1, keepdims=True)
    rs = lax.rsqrt(var + EPS)
    return yc * rs, rs


def fwd_in(x, vec1, win, convw, cvec, wa2, ba, tm, exchange=None, carried=()):
    n = x.shape[0]
    nseg = tm // GW
    cg = 128

    def body(x_ref, vec_ref, win_ref, cw_ref, cv_ref, wa2_ref, ba_ref,
             ag_ref, yb_ref, co_ref, qk_ref, v_ref, g_ref, la_ref, r_ref, pad_ref):
        xx = x_ref[...]
        rstd = lax.rsqrt(jnp.mean(xx * xx, axis=-1, keepdims=True) + EPS)
        h = ((xx * rstd * vec_ref[0:1, :]) * (1.0 + vec_ref[2:3, :]) + vec_ref[1:2, :]).astype(BF)
        pc = _dot(h, win_ref[:, :2 * DC])
        ag_ref[...] = pc.astype(BF)
        _fill_padded(pad_ref, pc[:, :DC] * _sigmoid(pc[:, DC:]), nseg)
        for s in range(nseg):
            for c0 in range(0, DC, cg):
                y = _conv_taps(pad_ref, s, cw_ref, c0, cg, False)
                yb_ref[pl.ds(s * GW, GW), pl.ds(c0, cg)] = y + cv_ref[0:1, c0:c0 + cg]
        p = _dot(h, win_ref[:, 2 * DC:])
        qk_ref[...] = p[:, :2 * DK].astype(BF)
        v_ref[...] = p[:, 2 * DK:2 * DK + DV].astype(BF)
        g_ref[...] = p[:, 2 * DK + DV:2 * DK + 2 * DV].astype(BF)
        r = p[:, 2 * DK + 2 * DV:].astype(BF)
        r_ref[...] = r
        la_ref[...] = _log_sigmoid(_dot(r, wa2_ref[...]) + ba_ref[...]) * (1.0 / TAU)
        yn, _ = _ln_stats(yb_ref[...])
        ln = yn * cv_ref[1:2, :] + cv_ref[2:3, :]
        co_ref[...] = (ln * _sigmoid(ln)).astype(BF)

    tok = lambda w: pl.BlockSpec((tm, w), lambda i: (i, 0))
    return _call(
        body, (n // tm,), "fwd_in",
        [tok(D), _full(vec1.shape), _full(win.shape), _full(convw.shape), _full(cvec.shape), _full(wa2.shape), _full(ba.shape)],
        [tok(2 * DC), tok(DC), tok(DC), tok(2 * DK), tok(DV), tok(DV), tok(2 * DK), tok(128)],
        [jax.ShapeDtypeStruct((n, 2 * DC), BF), jax.ShapeDtypeStruct((n, DC), F32),
         jax.ShapeDtypeStruct((n, DC), BF), jax.ShapeDtypeStruct((n, 2 * DK), BF),
         jax.ShapeDtypeStruct((n, DV), BF), jax.ShapeDtypeStruct((n, DV), BF),
         jax.ShapeDtypeStruct((n, 2 * DK), F32), jax.ShapeDtypeStruct((n, 128), BF)],
        [pltpu.VMEM((nseg, SEGP, DC), F32)],
        (x, vec1, win, convw, cvec, wa2, ba), exchange, carried)


def _gla_dir(d):
    return (_tri(CH, "le"), CH - 1) if d == 0 else (_tri(CH, "ge"), 0)


def _gla_chunk_terms(qk, la, d):
    seen, last = _gla_dir(d)
    b = _mask_dot(seen.astype(BF), la)
    bl = b[last:last + 1, :]
    eb = jnp.exp(b)
    enb = jnp.exp(-b)
    ekd = jnp.exp(bl - b)
    ebl = jnp.exp(bl)
    q = qk[:, :DK].astype(F32) * QSCALE
    k = qk[:, DK:].astype(F32)
    return eb, enb, ekd, ebl, q * eb, k * enb, k * ekd


NP = NH // 2
PW = 2 * HK


def _lo_lanes(shape):
    return lax.broadcasted_iota(jnp.int32, shape, len(shape) - 1) < HK


def _pair_sel(lo, hi):
    return jnp.where(_lo_lanes(lo.shape), lo, hi)


def _only(x, which):
    keep = _lo_lanes(x.shape) if which == 0 else jnp.logical_not(_lo_lanes(x.shape))
    return jnp.where(keep, x, jnp.zeros_like(x))


def gla_fwd(qk, v, la, s0, tm):
    n = qk.shape[0]
    nt = n // tm
    nc = tm // CH

    def body(qkf_ref, vf_ref, laf_ref, qkb_ref, vb_ref, lab_ref, s0_ref, of_ref, ob_ref, sef_ref, seb_ref, st_ref):
        @pl.when(pl.program_id(0) == 0)
        def _():
            st_ref[...] = s0_ref[...]

        def chunk(ci, carry):
            t = []
            for d, (qk_ref, v_ref, la_ref) in enumerate(((qkf_ref, vf_ref, laf_ref), (qkb_ref, vb_ref, lab_ref))):
                c = ci if d == 0 else nc - 1 - ci
                rows = pl.ds(pl.multiple_of(c * CH, CH), CH)
                eb, enb, ekd, ebl, qt, kt, kd = _gla_chunk_terms(qk_ref[rows, :], la_ref[rows, :], d)
                t.append(dict(c=c, rows=rows, ebl=ebl, qt=qt.astype(BF), kt=kt.astype(BF), kd=kd.astype(BF),
                              vv=v_ref[rows, :], st=[st_ref[d, p] for p in range(NP)], amask=_gla_dir(d)[0]))
            dh = [(d, h) for d in range(2) for h in range(NH)]
            dp = [(d, p) for d in range(2) for p in range(NP)]
            ps = lambda h: slice((h // 2) * PW, (h // 2 + 1) * PW)
            vs = lambda h: slice(h * HV, (h + 1) * HV)
            v2 = lambda p: slice(2 * p * HV, 2 * (p + 1) * HV)
            qm = {(d, h): _only(t[d]['qt'][:, ps(h)], h % 2) for d, h in dh}
            a = {(d, h): jnp.where(t[d]['amask'], _dot_nt(qm[d, h], t[d]['kt'][:, ps(h)]), 0.0).astype(BF) for d, h in dh}
            oi = {(d, p): _dot_nt(jnp.concatenate([qm[d, 2 * p], qm[d, 2 * p + 1]], axis=0), t[d]['st'][p].astype(BF))
                  for d, p in dp}
            o = {(d, h): _dot(a[d, h], t[d]['vv'][:, vs(h)]) + oi[d, h // 2][(h % 2) * CH:(h % 2 + 1) * CH] for d, h in dh}
            kv = {(d, p): _dot_tn(t[d]['vv'][:, v2(p)], t[d]['kd'][:, p * PW:(p + 1) * PW]) for d, p in dp}
            for d, (o_ref, se_ref) in enumerate(((of_ref, sef_ref), (ob_ref, seb_ref))):
                for h in range(NH):
                    o_ref[t[d]['rows'], vs(h)] = o[d, h].astype(BF)
                for p in range(NP):
                    se_ref[t[d]['c'], p] = t[d]['st'][p]
                    st_ref[d, p] = (t[d]['ebl'][:, p * PW:(p + 1) * PW] * t[d]['st'][p]
                                    + _pair_sel(kv[d, p][:HV], kv[d, p][HV:]))
            return carry

        lax.fori_loop(0, nc, chunk, 0, unroll=4)

    fw = lambda w, col=0: pl.BlockSpec((tm, w), lambda i: (i, col))
    bw = lambda w, col=0: pl.BlockSpec((tm, w), lambda i: (nt - 1 - i, col))
    se_f = pl.BlockSpec((nc, NP, HV, PW), lambda i: (i, 0, 0, 0))
    se_b = pl.BlockSpec((nc, NP, HV, PW), lambda i: (nt - 1 - i, 0, 0, 0))
    se_shape = jax.ShapeDtypeStruct((n // CH, NP, HV, PW), F32)
    return pl.pallas_call(
        body, grid=(nt,), name="gla_fwd",
        in_specs=[fw(2 * DK), fw(DV), fw(DK, 0), bw(2 * DK), bw(DV), bw(DK, 1), _full(s0.shape)],
        out_specs=[fw(DV), bw(DV), se_f, se_b],
        out_shape=[jax.ShapeDtypeStruct((n, DV), BF), jax.ShapeDtypeStruct((n, DV), BF), se_shape, se_shape],
        scratch_shapes=[pltpu.VMEM((2, NP, HV, PW), F32)],
        compiler_params=_cparams(("arbitrary",)),
    )(qk, v, la, qk, v, la, s0)


def gla_bwd(qk, v, la, do, se_f, se_b, tm, exchange=None, carried=()):
    n = qk.shape[0]
    nt = n // tm
    nc = tm // CH

    def body(qkf_ref, vf_ref, laf_ref, dof_ref, sef_ref, qkb_ref, vb_ref, lab_ref, dob_ref, seb_ref,
             dqkf_ref, dvf_ref, dlaf_ref, dqkb_ref, dvb_ref, dlab_ref, ds0_ref, ds_ref):
        @pl.when(pl.program_id(0) == 0)
        def _():
            ds_ref[...] = jnp.zeros_like(ds_ref)

        def chunk(ci, carry):
            t = []
            for d, (qk_ref, v_ref, la_ref, do_ref, se_ref) in enumerate(
                    ((qkf_ref, vf_ref, laf_ref, dof_ref, sef_ref), (qkb_ref, vb_ref, lab_ref, dob_ref, seb_ref))):
                c = nc - 1 - ci if d == 0 else ci
                rows = pl.ds(pl.multiple_of(c * CH, CH), CH)
                amask, last = _gla_dir(d)
                eb, enb, ekd, ebl, qt, kt, kd = _gla_chunk_terms(qk_ref[rows, :], la_ref[rows, :], d)
                t.append(dict(rows=rows, amask=amask, last=last, eb=eb, enb=enb, ekd=ekd, ebl=ebl, qt=qt, kt=kt, kd=kd,
                              qtb=qt.astype(BF), ktb=kt.astype(BF), kdb=kd.astype(BF), vv=v_ref[rows, :], dd=do_ref[rows, :],
                              st=[se_ref[c, p] for p in range(NP)], dsn=[ds_ref[d, p] for p in range(NP)]))
            dh = [(d, h) for d in range(2) for h in range(NH)]
            dp = [(d, p) for d in range(2) for p in range(NP)]
            ps = lambda h: slice((h // 2) * PW, (h // 2 + 1) * PW)
            vs = lambda h: slice(h * HV, (h + 1) * HV)
            stb = {(d, p): t[d]['st'][p].astype(BF) for d, p in dp}
            dsnb = {(d, p): t[d]['dsn'][p].astype(BF) for d, p in dp}
            qm = {(d, h): _only(t[d]['qtb'][:, ps(h)], h % 2) for d, h in dh}
            km = {(d, h): _only(t[d]['kdb'][:, ps(h)], h % 2) for d, h in dh}
            a = {(d, h): jnp.where(t[d]['amask'], _dot_nt(qm[d, h], t[d]['ktb'][:, ps(h)]), 0.0).astype(BF) for d, h in dh}
            da = {(d, h): jnp.where(t[d]['amask'], _dot_nt(t[d]['dd'][:, vs(h)], t[d]['vv'][:, vs(h)]), 0.0).astype(BF)
                  for d, h in dh}
            v2 = lambda p: slice(2 * p * HV, 2 * (p + 1) * HV)
            rows2 = lambda x, d, p: jnp.concatenate([x[d, 2 * p], x[d, 2 * p + 1]], axis=0)
            half = lambda x, h: x[(h % 2) * CH:(h % 2 + 1) * CH]
            dvs = {(d, p): _dot_nt(rows2(km, d, p), dsnb[d, p]) for d, p in dp}
            dv = {(d, h): _dot_tn(a[d, h], t[d]['dd'][:, vs(h)]) + half(dvs[d, h // 2], h) for d, h in dh}
            vrows = lambda d, p: jnp.concatenate([t[d]['vv'][:, vs(2 * p)], t[d]['vv'][:, vs(2 * p + 1)]], axis=0)
            drows = lambda d, p: jnp.concatenate([t[d]['dd'][:, vs(2 * p)], t[d]['dd'][:, vs(2 * p + 1)]], axis=0)
            dkd2 = {(d, p): _dot(vrows(d, p), dsnb[d, p]) for d, p in dp}
            dqt2 = {(d, p): _dot(rows2(da, d, p), t[d]['ktb'][:, p * PW:(p + 1) * PW]) + _dot(drows(d, p), stb[d, p])
                    for d, p in dp}
            dkt2 = {(d, p): _dot_tn(jnp.concatenate([da[d, 2 * p], da[d, 2 * p + 1]], axis=1), t[d]['qtb'][:, p * PW:(p + 1) * PW])
                    for d, p in dp}
            dsq2 = {(d, p): _dot_tn(t[d]['dd'][:, v2(p)], t[d]['qtb'][:, p * PW:(p + 1) * PW]) for d, p in dp}
            two = lambda x, d, p, n: _pair_sel(x[d, p][:n], x[d, p][n:])
            for d, (dqk_ref, dv_ref, dla_ref) in enumerate(((dqkf_ref, dvf_ref, dlaf_ref), (dqkb_ref, dvb_ref, dlab_ref))):
                td = t[d]
                rows = td['rows']
                for h in range(NH):
                    dv_ref[rows, vs(h)] = dv[d, h].astype(BF)
                pair = lambda x: jnp.concatenate([two(x, d, p, CH) for p in range(NP)], axis=1)
                dqt_, dkt_, dkd_ = pair(dqt2), pair(dkt2), pair(dkd2)
                debl = jnp.concatenate([jnp.sum(td['st'][p] * td['dsn'][p], axis=0, keepdims=True) for p in range(NP)], axis=1)
                for p in range(NP):
                    ds_ref[d, p] = two(dsq2, d, p, HV) + td['ebl'][:, p * PW:(p + 1) * PW] * td['dsn'][p]
                dkdkd = dkd_ * td['kd']
                dbl = jnp.sum(dkdkd, axis=0, keepdims=True) + debl * td['ebl']
                is_last = lax.broadcasted_iota(jnp.int32, (CH, DK), 0) == td['last']
                db = dqt_ * td['qt'] - dkt_ * td['kt'] - dkdkd + jnp.where(is_last, dbl, 0.0)
                dqk_ref[rows, :] = jnp.concatenate([dqt_ * td['eb'] * QSCALE, dkt_ * td['enb'] + dkd_ * td['ekd']], axis=1).astype(BF)
                dla_ref[rows, :] = _mask_dot(_gla_dir(1 - d)[0].astype(BF), db)
            return carry

        lax.fori_loop(0, nc, chunk, 0, unroll=4)

        @pl.when(pl.program_id(0) == nt - 1)
        def _():
            ds0_ref[...] = ds_ref[...]

    up = lambda w, col=0: pl.BlockSpec((tm, w), lambda i: (i, col))
    dn = lambda w, col=0: pl.BlockSpec((tm, w), lambda i: (nt - 1 - i, col))
    se_up = pl.BlockSpec((nc, NP, HV, PW), lambda i: (i, 0, 0, 0))
    se_dn = pl.BlockSpec((nc, NP, HV, PW), lambda i: (nt - 1 - i, 0, 0, 0))
    return _call(
        body, (nt,), "gla_bwd",
        [dn(2 * DK), dn(DV), dn(DK, 0), dn(DV), se_dn, up(2 * DK), up(DV), up(DK, 1), up(DV), se_up],
        [dn(2 * DK), dn(DV), dn(DK), up(2 * DK), up(DV), up(DK), _full((2, NP, HV, PW))],
        [jax.ShapeDtypeStruct((n, 2 * DK), BF), jax.ShapeDtypeStruct((n, DV), BF),
         jax.ShapeDtypeStruct((n, DK), F32), jax.ShapeDtypeStruct((n, 2 * DK), BF),
         jax.ShapeDtypeStruct((n, DV), BF), jax.ShapeDtypeStruct((n, DK), F32),
         jax.ShapeDtypeStruct((2, NP, HV, PW), F32)],
        [pltpu.VMEM((2, NP, HV, PW), F32)],
        (qk, v, la, do, se_f, qk, v, la, do, se_b), exchange, carried)


def _head_norm(o):
    ons, rss = [], []
    for h in range(NH):
        oh = o[:, h * HV:(h + 1) * HV]
        rs = lax.rsqrt(jnp.mean(oh * oh, axis=-1, keepdims=True) + EPS)
        ons.append(oh * rs)
        rss.append(rs)
    return ons, rss


def merge_fwd(x, o_f, o_b, g, co, vecm, gn, wout, tm):
    n = x.shape[0]

    def body(x_ref, of_ref, ob_ref, g_ref, co_ref, vec_ref, gn_ref, w_ref, x1_ref, y1_ref, cat_ref):
        o = of_ref[...].astype(F32) + ob_ref[...].astype(F32)
        ons, _ = _head_norm(o)
        gg = g_ref[...].astype(F32)
        sil = gg * _sigmoid(gg)
        cat_ref[:, :DC] = co_ref[...]
        for h in range(NH):
            vs = slice(h * HV, (h + 1) * HV)
            cat_ref[:, DC + h * HV:DC + (h + 1) * HV] = (ons[h] * gn_ref[:, vs] * sil[:, vs]).astype(BF)
        y1 = _dot(cat_ref[...], w_ref[...])
        y1_ref[...] = y1.astype(BF)
        x1_ref[...] = x_ref[...] + vec_ref[0:1, :] * y1

    tok = lambda w: pl.BlockSpec((tm, w), lambda i: (i, 0))
    return pl.pallas_call(
        body, grid=(n // tm,), name="merge_fwd",
        in_specs=[tok(D), tok(DV), tok(DV), tok(DV), tok(DC), _full(vecm.shape), _full(gn.shape), _full(wout.shape)],
        out_specs=[tok(D), tok(D), tok(D)],
        out_shape=[jax.ShapeDtypeStruct((n, D), F32), jax.ShapeDtypeStruct((n, D), BF), jax.ShapeDtypeStruct((n, D), BF)],
        compiler_params=_cparams(("arbitrary",)),
    )(x, o_f, o_b, g, co, vecm, gn, wout)


def merge_bwd(dx1, y1, o_f, o_b, g, vecm, gn, wout, tm):
    n = dx1.shape[0]

    def body(dx1_ref, y1_ref, of_ref, ob_ref, g_ref, vec_ref, gn_ref, w_ref,
             dy1_ref, dco_ref, do_ref, dg_ref, s1_ref, s2_ref):
        @pl.when(pl.program_id(0) == 0)
        def _():
            s1_ref[...] = jnp.zeros_like(s1_ref)
            s2_ref[...] = jnp.zeros_like(s2_ref)

        dx1 = dx1_ref[...]
        s1_ref[...] += _colsum8(dx1 * y1_ref[...].astype(F32))
        dy1 = (dx1 * vec_ref[0:1, :]).astype(BF)
        dy1_ref[...] = dy1
        dcat = _dot_nt(dy1, w_ref[...])
        dco_ref[...] = dcat[:, :DC].astype(BF)
        o = of_ref[...].astype(F32) + ob_ref[...].astype(F32)
        ons, rss = _head_norm(o)
        gg = g_ref[...].astype(F32)
        sg = _sigmoid(gg)
        sil = gg * sg
        dsil = sg * (1.0 + gg * (1.0 - sg))
        for h in range(NH):
            vs = slice(h * HV, (h + 1) * HV)
            do2 = dcat[:, DC + h * HV:DC + (h + 1) * HV]
            gnh = gn_ref[:, vs]
            t = do2 * sil[:, vs]
            s2_ref[:, vs] += _colsum8(t * ons[h])
            don = t * gnh
            do_ref[:, vs] = (rss[h] * (don - ons[h] * jnp.mean(don * ons[h], axis=-1, keepdims=True))).astype(BF)
            dg_ref[:, vs] = (do2 * ons[h] * gnh * dsil[:, vs]).astype(BF)

    tok = lambda w: pl.BlockSpec((tm, w), lambda i: (i, 0))
    return pl.pallas_call(
        body, grid=(n // tm,), name="merge_bwd",
        in_specs=[tok(D), tok(D), tok(DV), tok(DV), tok(DV), _full(vecm.shape), _full(gn.shape), _full(wout.shape)],
        out_specs=[tok(D), tok(DC), tok(DV), tok(DV), _full((8, D)), _full((8, DV))],
        out_shape=[jax.ShapeDtypeStruct((n, D), BF), jax.ShapeDtypeStruct((n, DC), BF), jax.ShapeDtypeStruct((n, DV), BF),
                   jax.ShapeDtypeStruct((n, DV), BF), jax.ShapeDtypeStruct((8, D), F32), jax.ShapeDtypeStruct((8, DV), F32)],
        compiler_params=_cparams(("arbitrary",)),
    )(dx1, y1, o_f, o_b, g, vecm, gn, wout)


def ffn_fwd_bwd(x1, tgt, vecf, wg, wu, wd, tm):
    n = x1.shape[0]

    def body(x1_ref, t_ref, vec_ref, wg_ref, wu_ref, wd_ref,
             dx1_ref, h2_ref, act_ref, dgt_ref, dup_ref, dy2_ref, s_ref):
        @pl.when(pl.program_id(0) == 0)
        def _():
            s_ref[...] = jnp.zeros_like(s_ref)

        n2g, sh2, sc2, g2, fg = (vec_ref[i:i + 1, :] for i in range(5))
        x1 = x1_ref[...]
        r2 = lax.rsqrt(jnp.mean(x1 * x1, axis=-1, keepdims=True) + EPS)
        xn2 = x1 * r2
        h2 = (xn2 * n2g * (1.0 + sc2) + sh2).astype(BF)
        h2_ref[...] = h2
        gt = _dot(h2, wg_ref[...])
        up = _dot(h2, wu_ref[...])
        sg = _sigmoid(gt)
        sil = gt * sg
        act = (sil * up).astype(BF)
        act_ref[...] = act
        y2 = _dot(act, wd_ref[...])
        x2 = x1 + g2 * y2
        r3 = lax.rsqrt(jnp.mean(x2 * x2, axis=-1, keepdims=True) + EPS)
        xn3 = x2 * r3
        e = xn3 * fg - t_ref[...]
        s_ref[40:48, :] += _colsum8(e * e) * (0.5 / D)
        dyo = e * (1.0 / D)
        s_ref[0:8, :] += _colsum8(dyo * xn3)
        dxn3 = dyo * fg
        dx2 = r3 * (dxn3 - xn3 * jnp.mean(dxn3 * xn3, axis=-1, keepdims=True))
        s_ref[8:16, :] += _colsum8(dx2 * y2)
        dy2 = (dx2 * g2).astype(BF)
        dy2_ref[...] = dy2
        dact = _dot_nt(dy2, wd_ref[...])
        dup = (dact * sil).astype(BF)
        dgt = (dact * up * (sg * (1.0 + gt * (1.0 - sg)))).astype(BF)
        dup_ref[...] = dup
        dgt_ref[...] = dgt
        dh2 = _dot_nt(dgt, wg_ref[...]) + _dot_nt(dup, wu_ref[...])
        s_ref[16:24, :] += _colsum8(dh2)
        t = dh2 * xn2
        s_ref[24:32, :] += _colsum8(t * n2g)
        s_ref[32:40, :] += _colsum8(t * (1.0 + sc2))
        dxn2 = dh2 * ((1.0 + sc2) * n2g)
        dx1_ref[...] = dx2 + r2 * (dxn2 - xn2 * jnp.mean(dxn2 * xn2, axis=-1, keepdims=True))

    tok = lambda w: pl.BlockSpec((tm, w), lambda i: (i, 0))
    wspec = lambda a: pl.BlockSpec(a.shape, lambda i: (0, 0), pipeline_mode=pl.Buffered(1))
    return pl.pallas_call(
        body, grid=(n // tm,), name="ffn_fwd_bwd",
        in_specs=[tok(D), tok(D), _full(vecf.shape), wspec(wg), wspec(wu), wspec(wd)],
        out_specs=[tok(D), tok(D), tok(DFF), tok(DFF), tok(DFF), tok(D), _full((48, D))],
        out_shape=[jax.ShapeDtypeStruct((n, D), F32), jax.ShapeDtypeStruct((n, D), BF), jax.ShapeDtypeStruct((n, DFF), BF),
                   jax.ShapeDtypeStruct((n, DFF), BF), jax.ShapeDtypeStruct((n, DFF), BF), jax.ShapeDtypeStruct((n, D), BF),
                   jax.ShapeDtypeStruct((48, D), F32)],
        compiler_params=_cparams(("arbitrary",)),
    )(x1, tgt, vecf, wg, wu, wd)


def wgrad(a, b, init, t1, t2, tn, name):
    n, k1 = a.shape
    k2 = b.shape[1]

    def body(a_ref, b_ref, *rest):
        o_ref = rest[-1]

        @pl.when(pl.program_id(2) == 0)
        def _():
            o_ref[...] = rest[0][...] if init is not None else jnp.zeros_like(o_ref)

        o_ref[...] += _dot_tn(a_ref[...], b_ref[...])

    ospec = pl.BlockSpec((t1, t2), lambda i, j, k: (i, j))
    extra = ([ospec], {2: 0}, (init,)) if init is not None else ([], {}, ())
    return pl.pallas_call(
        body, grid=(k1 // t1, k2 // t2, n // tn), name=name,
        in_specs=[pl.BlockSpec((tn, t1), lambda i, j, k: (k, i)), pl.BlockSpec((tn, t2), lambda i, j, k: (k, j))] + extra[0],
        out_specs=ospec, out_shape=jax.ShapeDtypeStruct((k1, k2), F32), input_output_aliases=extra[1],
        compiler_params=_cparams(("parallel", "parallel", "arbitrary")),
    )(a, b, *extra[2])


def bwd_in(x, dx1, ag, yb, dco, dqk_f, dqk_b, dv_f, dv_b, dg, dla_f, dla_b, la, r, vec1, win, convw, cvec, wa2, tm):
    n = x.shape[0]
    nseg = tm // GW
    cg = 128

    def body(x_ref, dx1_ref, ag_ref, yb_ref, dco_ref, dqkf_ref, dqkb_ref, dvf_ref, dvb_ref, dg_ref, dlaf_ref, dlab_ref,
             la_ref, r_ref, vec_ref, win_ref, cw_ref, cv_ref, wa2_ref,
             gx_ref, h_ref, dp_ref, dwa2_ref, dcw_ref, s_ref, vc_ref, pad2_ref, dvc_ref, dcw8_ref):
        first = pl.program_id(0) == 0

        @pl.when(first)
        def _():
            s_ref[...] = jnp.zeros_like(s_ref)
            dwa2_ref[...] = jnp.zeros_like(dwa2_ref)
            dcw8_ref[...] = jnp.zeros_like(dcw8_ref)

        yn, rs = _ln_stats(yb_ref[...])
        lng = cv_ref[1:2, :]
        ln = yn * lng + cv_ref[2:3, :]
        sgl = _sigmoid(ln)
        dln = dco_ref[...].astype(F32) * (sgl * (1.0 + ln * (1.0 - sgl)))
        dyn = dln * lng
        dyb = rs * (dyn - jnp.mean(dyn, axis=-1, keepdims=True) - yn * jnp.mean(dyn * yn, axis=-1, keepdims=True))
        s_ref[24:32, 0:DC] += _colsum8(dyb)
        s_ref[24:32, DC:D] += _colsum8(dln * yn)
        s_ref[32:40, 0:DC] += _colsum8(dln)

        agv = ag_ref[...].astype(F32)
        a = agv[:, :DC]
        sgg = _sigmoid(agv[:, DC:])
        vc_ref[...] = a * sgg
        _fill_padded(pad2_ref, dyb, nseg)

        dp_ref[:, 2 * DC:2 * DC + 2 * DK] = (dqkf_ref[...].astype(F32) + dqkb_ref[...].astype(F32)).astype(BF)
        dp_ref[:, 2 * DC + 2 * DK:2 * DC + 2 * DK + DV] = (dvf_ref[...].astype(F32) + dvb_ref[...].astype(F32)).astype(BF)
        dp_ref[:, 2 * DC + 2 * DK + DV:2 * DC + 2 * DK + 2 * DV] = dg_ref[...]

        la = la_ref[...]
        dla = jnp.concatenate([dlaf_ref[...], dlab_ref[...]], axis=1)
        dpre = dla * (1.0 - jnp.exp(TAU * la)) * (1.0 / TAU)
        s_ref[32:40, DC:D] += _colsum8(dpre)
        dpreb = dpre.astype(BF)
        dwa2_ref[...] += _dot_tn(r_ref[...], dpreb)
        dp_ref[:, DINP - 128:] = _dot_nt(dpreb, wa2_ref[...]).astype(BF)
        dh_rest = _dot_nt(dp_ref[:, 2 * DC:], win_ref[:, 2 * DC:])

        for s in range(nseg):
            rows = pl.ds(s * GW, GW)
            for c0 in range(0, DC, cg):
                cs = pl.ds(c0, cg)
                vcs = vc_ref[rows, cs]
                acc = jnp.zeros((GW, cg), F32)
                for j, rows_j in _tap_slabs(pad2_ref, s, cs):
                    acc = acc + cw_ref[pl.ds(CW - 1 - j, 1), cs] * rows_j
                    dcw8_ref[CW - 1 - j, :, cs] += _colsum8(vcs * rows_j)
                dvc_ref[rows, cs] = acc
        dvc = dvc_ref[...]
        dp_ref[:, 0:DC] = (dvc * sgg).astype(BF)
        dp_ref[:, DC:2 * DC] = (dvc * a * sgg * (1.0 - sgg)).astype(BF)

        dh = dh_rest + _dot_nt(dp_ref[:, :2 * DC], win_ref[:, :2 * DC])
        xx = x_ref[...]
        n1g, sh1, sc1 = vec_ref[0:1, :], vec_ref[1:2, :], vec_ref[2:3, :]
        rstd = lax.rsqrt(jnp.mean(xx * xx, axis=-1, keepdims=True) + EPS)
        xn = xx * rstd
        h_ref[...] = (xn * n1g * (1.0 + sc1) + sh1).astype(BF)
        s_ref[0:8, :] += _colsum8(dh)
        t = dh * xn
        s_ref[8:16, :] += _colsum8(t * n1g)
        s_ref[16:24, :] += _colsum8(t * (1.0 + sc1))
        dxn = dh * ((1.0 + sc1) * n1g)
        gx_ref[...] = dx1_ref[...] + rstd * (dxn - xn * jnp.mean(dxn * xn, axis=-1, keepdims=True))

        @pl.when(pl.program_id(0) == pl.num_programs(0) - 1)
        def _():
            dcw_ref[...] = jnp.sum(dcw8_ref[...], axis=1)

    tok = lambda w: pl.BlockSpec((tm, w), lambda i: (i, 0))
    return pl.pallas_call(
        body, grid=(n // tm,), name="bwd_in",
        in_specs=[tok(D), tok(D), tok(2 * DC), tok(DC), tok(DC), tok(2 * DK), tok(2 * DK), tok(DV), tok(DV), tok(DV),
                  tok(DK), tok(DK), tok(2 * DK), tok(128), _full(vec1.shape),
                  pl.BlockSpec(win.shape, lambda i: (0, 0), pipeline_mode=pl.Buffered(1)),
                  _full(convw.shape), _full(cvec.shape), _full(wa2.shape)],
        out_specs=[tok(D), tok(D), tok(DINP), _full((128, 2 * DK)), _full((32, DC)), _full((40, D))],
        out_shape=[jax.ShapeDtypeStruct((n, D), F32), jax.ShapeDtypeStruct((n, D), BF), jax.ShapeDtypeStruct((n, DINP), BF),
                   jax.ShapeDtypeStruct((128, 2 * DK), F32), jax.ShapeDtypeStruct((32, DC), F32),
                   jax.ShapeDtypeStruct((40, D), F32)],
        scratch_shapes=[pltpu.VMEM((tm, DC), F32), pltpu.VMEM((nseg, SEGP, DC), F32), pltpu.VMEM((tm, DC), F32),
                        pltpu.VMEM((32, 8, DC), F32)],
        compiler_params=_cparams(("arbitrary",)),
    )(x, dx1, ag, yb, dco, dqk_f, dqk_b, dv_f, dv_b, dg, dla_f, dla_b, la, r, vec1, win, convw, cvec, wa2)


def _ctx_common(ctx_ref, vec_ref, win_ref, wa2_ref, ba_ref):
    cx = ctx_ref[...]
    t = cx.shape[0]
    rstd = lax.rsqrt(jnp.mean(cx * cx, axis=-1, keepdims=True) + EPS)
    xn = cx * rstd
    hc = (xn * vec_ref[0:1, :] * (1.0 + vec_ref[2:3, :]) + vec_ref[1:2, :]).astype(BF)
    k0 = 2 * DC + DK
    kv = _dot(hc, win_ref[:, k0:k0 + DK + DV]).astype(BF).astype(F32)
    r = _dot(hc, win_ref[:, DINP - 128:]).astype(BF)
    la = _log_sigmoid(_dot(r, wa2_ref[...]) + ba_ref[...]) * (1.0 / TAU)
    incl = _tri(t, "le").astype(BF)
    strict = _tri(t, "lt").astype(BF)
    bf = _mask_dot(incl, la[:, :DK])
    wf = jnp.exp(bf[t - 1:t, :] - bf)
    wb = jnp.exp(_mask_dot(strict, la[:, DK:]))
    return xn, hc, kv[:, :DK], kv[:, DK:], r, la, wf, wb


def ctx_fwd(ctx, vecc, win, wa2, ba):
    def body(ctx_ref, vec_ref, win_ref, wa2_ref, ba_ref, s_ref):
        _, _, k, v, _, _, wf, wb = _ctx_common(ctx_ref, vec_ref, win_ref, wa2_ref, ba_ref)
        vb = v.astype(BF)
        for d, w in enumerate((wf, wb)):
            kd = (k * w).astype(BF)
            for h in range(NH):
                s_ref[d, h // 2, :, (h % 2) * HK:(h % 2 + 1) * HK] = _dot_tn(vb[:, h * HV:(h + 1) * HV], kd[:, h * HK:(h + 1) * HK])

    return pl.pallas_call(
        body, name="ctx_fwd", out_shape=jax.ShapeDtypeStruct((2, NP, HV, PW), F32),
        compiler_params=pltpu.CompilerParams(vmem_limit_bytes=VMEM_LIMIT),
    )(ctx, vecc, win, wa2, ba)


def ctx_bwd(ctx, vecc, win, wa2, ba, ds0):
    t = ctx.shape[0]

    def body(ctx_ref, vec_ref, win_ref, wa2_ref, ba_ref, ds_ref, dwin_ref, dwa2_ref, s_ref, dpc_ref):
        xn, hc, k, v, r, la, wf, wb = _ctx_common(ctx_ref, vec_ref, win_ref, wa2_ref, ba_ref)
        vb = v.astype(BF)
        strict = _tri(t, "lt").astype(BF)
        strict_t = _tri(t, "gt").astype(BF)
        dpc_ref[...] = jnp.zeros_like(dpc_ref)
        k0 = 2 * DC + DK
        dk = jnp.zeros((t, DK), F32)
        des = []
        for d, w in enumerate((wf, wb)):
            kd = (k * w).astype(BF)
            dkds = []
            for h in range(NH):
                dsb = ds_ref[d, h // 2, :, (h % 2) * HK:(h % 2 + 1) * HK].astype(BF)
                dkds.append(_dot(vb[:, h * HV:(h + 1) * HV], dsb))
                dvh = _dot_nt(kd[:, h * HK:(h + 1) * HK], dsb)
                vs = slice(k0 + DK + h * HV, k0 + DK + (h + 1) * HV)
                if d == 0:
                    dpc_ref[:, vs] = dvh.astype(BF)
                else:
                    dpc_ref[:, vs] = (dpc_ref[:, vs].astype(F32) + dvh).astype(BF)
            dkd = jnp.concatenate(dkds, axis=1)
            dk = dk + dkd * w
            des.append(dkd * k * w)
        dpc_ref[:, k0:k0 + DK] = dk.astype(BF)
        dla = jnp.concatenate([_mask_dot(strict, des[0]), _mask_dot(strict_t, des[1])], axis=1)
        dpre = dla * (1.0 - jnp.exp(TAU * la)) * (1.0 / TAU)
        dpreb = dpre.astype(BF)
        dwa2_ref[...] = _dot_tn(r, dpreb)
        dpc_ref[:, DINP - 128:] = _dot_nt(dpreb, wa2_ref[...]).astype(BF)
        dpc = dpc_ref[...]
        dwin_ref[...] = _dot_tn(hc, dpc)
        dhc = _dot_nt(dpc, win_ref[...])
        n1g, sc1 = vec_ref[0:1, :], vec_ref[2:3, :]
        tt = dhc * xn
        s_ref[...] = jnp.zeros_like(s_ref)
        s_ref[0:1, :] = jnp.sum(tt * (1.0 + sc1), axis=0, keepdims=True)
        s_ref[1:2, :] = jnp.sum(dhc, axis=0, keepdims=True)
        s_ref[2:3, :] = jnp.sum(tt * n1g, axis=0, keepdims=True)
        s_ref[3:4, DC:D] = jnp.sum(dpre, axis=0, keepdims=True)

    return pl.pallas_call(
        body, name="ctx_bwd",
        out_shape=[jax.ShapeDtypeStruct((D, DINP), F32), jax.ShapeDtypeStruct((128, 2 * DK), F32),
                   jax.ShapeDtypeStruct((8, D), F32)],
        scratch_shapes=[pltpu.VMEM((t, DINP), BF)],
        compiler_params=pltpu.CompilerParams(vmem_limit_bytes=VMEM_LIMIT),
    )(ctx, vecc, win, wa2, ba, ds0)


def _silu(x):
    return x * _sigmoid(x)


def mod_bwd(cext, dm, wm):
    def body(c_ref, d_ref, w_ref, gw_ref, ds_ref):
        dmb = d_ref[...].astype(BF)
        gw_ref[...] = _dot_tn(_silu(c_ref[...]).astype(BF), dmb)
        ds_ref[...] = _dot_nt(dmb, w_ref[...].astype(BF))

    return pl.pallas_call(body, name="mod_bwd",
                          out_shape=[jax.ShapeDtypeStruct(wm.shape, F32), jax.ShapeDtypeStruct(cext.shape, F32)],
                          compiler_params=pltpu.CompilerParams(vmem_limit_bytes=VMEM_LIMIT))(cext, dm, wm)


def pack_small(sf, s1, s2, sd, sc, dcw, dwa2, dwa2_c):
    def body(sf_ref, s1_ref, s2_ref, sd_ref, sc_ref, dcw_ref, dwa2_ref, dwa2c_ref, o_ref, ocw_ref, owa_ref):
        rsum = lambda ref, i: jnp.sum(ref[8 * i:8 * i + 8, :], axis=0, keepdims=True)
        o_ref[...] = jnp.zeros_like(o_ref)
        o_ref[0:1, :] = rsum(sd_ref, 0)
        o_ref[1:2, :] = rsum(sd_ref, 1)
        o_ref[2:3, :] = rsum(s1_ref, 0)
        o_ref[3:4, :] = rsum(sf_ref, 2)
        o_ref[4:5, :] = rsum(sf_ref, 3)
        o_ref[5:6, :] = rsum(sf_ref, 1)
        o_ref[6:7, :] = sc_ref[1:2, :]
        o_ref[7:8, :] = sc_ref[2:3, :]
        o_ref[8:9, :] = rsum(sd_ref, 2) + sc_ref[0:1, :]
        o_ref[9:10, :] = rsum(sf_ref, 4)
        o_ref[10:11, :] = rsum(sf_ref, 0)
        o_ref[11:12, :] = rsum(sd_ref, 3)
        o_ref[12:13, :] = rsum(sd_ref, 4) + sc_ref[3:4, :]
        g = jnp.sum(s2_ref[...], axis=0, keepdims=True)
        o_ref[13:14, 0:HV] = g[:, 0:HV] + g[:, HV:2 * HV] + g[:, 2 * HV:3 * HV] + g[:, 3 * HV:4 * HV]
        o_ref[14:15, :] = rsum(sf_ref, 5)
        ocw_ref[...] = dcw_ref[...]
        owa_ref[...] = dwa2_ref[0:32, :] + dwa2c_ref[0:32, :]

    return pl.pallas_call(body, name="pack_small",
                          out_shape=[jax.ShapeDtypeStruct((16, D), F32), jax.ShapeDtypeStruct((32, DC), F32),
                                     jax.ShapeDtypeStruct((32, 2 * DK), F32)])(sf, s1, s2, sd, sc, dcw, dwa2, dwa2_c)


def small_totals(g8):
    r = g8.shape[1]

    def body(g_ref, t_ref, bm_ref, loss_ref):
        acc = g_ref[0]
        for i in range(1, NDEV):
            acc = acc + g_ref[i]
        t_ref[...] = acc
        bm_ref[...] = jnp.zeros_like(bm_ref)
        bm_ref[0:6, :] = acc[0:6, :]
        bm_ref[0:2, :] += acc[6:8, :]
        loss_ref[...] = jnp.broadcast_to(jnp.sum(acc[14:15, :], axis=1, keepdims=True), loss_ref.shape)

    return pl.pallas_call(body, name="small_totals",
                          out_shape=[jax.ShapeDtypeStruct((r, D), F32), jax.ShapeDtypeStruct((8, D), F32),
                                     jax.ShapeDtypeStruct((8, 128), F32)])(g8)


def cctx_grad(p8, c_ctx_row):
    def body(p_ref, c_ref, o_ref):
        acc = p_ref[0, 0:1, :]
        for j in range(1, NCHIP):
            acc = acc + p_ref[2 * j, 0:1, :]
        cc = c_ref[0:1, :]
        sg = _sigmoid(cc)
        o_ref[...] = jnp.zeros_like(o_ref)
        o_ref[0:1, :] = acc * (sg * (1.0 + cc * (1.0 - sg)))

    return pl.pallas_call(body, name="cctx_grad", out_shape=jax.ShapeDtypeStruct((8, D), F32))(p8, c_ctx_row)


def adamw(w, g, m, v, rows, name, emit_grad=False):
    r, c = w.shape

    def body(w_ref, g_ref, m_ref, v_ref, d_ref, nm_ref, nv_ref, *go_ref):
        gg = g_ref[...]
        nm = ADAM_B1 * m_ref[...] + (1.0 - ADAM_B1) * gg
        nv = ADAM_B2 * v_ref[...] + (1.0 - ADAM_B2) * (gg * gg)
        m_hat = nm / (1.0 - ADAM_B1 ** ADAM_STEP)
        v_hat = nv / (1.0 - ADAM_B2 ** ADAM_STEP)
        d_ref[...] = -ADAM_LR * (m_hat / (jnp.sqrt(v_hat) + ADAM_EPS) + ADAM_WD * w_ref[...])
        nm_ref[...] = nm
        nv_ref[...] = nv
        if emit_grad:
            go_ref[0][...] = gg

    spec = pl.BlockSpec((rows, c), lambda i: (i, 0))
    sds = jax.ShapeDtypeStruct((r, c), F32)
    nout = 4 if emit_grad else 3
    return pl.pallas_call(
        body, grid=(r // rows,), name=name, in_specs=[spec] * 4, out_specs=[spec] * nout, out_shape=[sds] * nout,
        compiler_params=_cparams(("parallel",)),
    )(w, g, m, v)


def _me():
    return lax.axis_index("x"), lax.axis_index("y"), lax.axis_index("c")


def _flip(v, bit):
    return 1 - v if bit else v


ANY = pl.BlockSpec(memory_space=pl.ANY)


def _gather8(x_ref, o_ref, ssem, rsem, lsem):
    mx, my, mc = _me()
    me = 4 * mx + 2 * my + mc
    local = pltpu.make_async_copy(x_ref, o_ref.at[me], lsem)
    local.start()
    peer = lambda k: (_flip(mx, k & 4), _flip(my, k & 2), _flip(mc, k & 1))
    sends = []
    for k in range(1, NDEV):
        cp = pltpu.make_async_remote_copy(src_ref=x_ref, dst_ref=o_ref.at[me], send_sem=ssem.at[k - 1],
                                          recv_sem=rsem.at[k - 1], device_id=peer(k), device_id_type=MESH)
        cp.start()
        sends.append(cp)
    for k in range(1, NDEV):
        px, py, pc = peer(k)
        pltpu.make_async_remote_copy(src_ref=x_ref, dst_ref=o_ref.at[4 * px + 2 * py + pc], send_sem=ssem.at[k - 1],
                                     recv_sem=rsem.at[k - 1], device_id=(px, py, pc), device_id_type=MESH).wait_recv()
    for cp in sends:
        cp.wait_send()
    local.wait()


def _gather8_sems():
    return [pltpu.SemaphoreType.DMA((NDEV - 1,)), pltpu.SemaphoreType.DMA((NDEV - 1,)), pltpu.SemaphoreType.DMA]


def all_gather8(x, name):
    vm = pl.BlockSpec(memory_space=pltpu.VMEM)
    return pl.pallas_call(_gather8_body(), name=name, in_specs=[vm], out_specs=vm,
                          out_shape=jax.ShapeDtypeStruct((NDEV,) + x.shape, x.dtype), scratch_shapes=_gather8_sems())(x)


def _gather8_body():
    def body(x_ref, o_ref, ssem, rsem, lsem):
        _gather8(x_ref, o_ref, ssem, rsem, lsem)
    return body


def prologue(small, c_ctx_rows, wm, bm, w_in_shard):
    ex = ChipExchange("gather", [w_in_shard])

    def body(s_ref, cc_ref, w_ref, b_ref, win_ref, s8_ref, m8_ref, wing_ref, mloc_ref, *sems):
        ex.start([win_ref], [wing_ref], sems[6:])
        _gather8(s_ref, s8_ref, *sems[0:3])
        cext = jnp.concatenate([s8_ref[:, 0, :], cc_ref[...]], axis=0)
        mloc_ref[...] = _dot(_silu(cext).astype(BF), w_ref[...].astype(BF)) + b_ref[...]
        _gather8(mloc_ref, m8_ref, *sems[3:6])
        ex.wait([win_ref], [wing_ref], sems[6:])

    vm = pl.BlockSpec(memory_space=pltpu.VMEM)
    wcols = wm.shape[1]
    return pl.pallas_call(
        body, name="prologue", in_specs=[vm, vm, vm, vm, ANY], out_specs=[vm, vm, ANY],
        out_shape=[jax.ShapeDtypeStruct((NDEV, 16, D), F32), jax.ShapeDtypeStruct((NDEV, 16, wcols), F32)] + ex.out_shape,
        scratch_shapes=[pltpu.VMEM((16, wcols), F32)] + _gather8_sems() + _gather8_sems() + ex.scratch,
        compiler_params=pltpu.CompilerParams(vmem_limit_bytes=VMEM_LIMIT),
    )(small, c_ctx_rows, wm, bm, w_in_shard)


def _chip_peers(mx, my):
    out = []
    for p in range(1, NCHIP):
        px, py = _flip(mx, p & 2), _flip(my, p & 1)
        out.append((px, py, 2 * px + py))
    return out


class ChipExchange:
    def __init__(self, kind, arrays):
        self.kind = kind
        self.n = len(arrays)
        if kind == "gather":
            self.out_shape = [jax.ShapeDtypeStruct((NCHIP,) + a.shape, a.dtype) for a in arrays]
        else:
            self.out_shape = [jax.ShapeDtypeStruct(a.shape, a.dtype) for a in arrays]
        self.scratch = [pltpu.SemaphoreType.DMA((3 * self.n,)), pltpu.SemaphoreType.DMA((3 * self.n,)),
                        pltpu.SemaphoreType.DMA((self.n,))]

    def _copies(self, ins, outs, sems):
        ssem, rsem, lsem = sems
        mx, my, mc = _me()
        jme = 2 * mx + my
        gather = self.kind == "gather"
        local, sends, waits = [], [], []
        for k in range(self.n):
            local.append(pltpu.make_async_copy(ins[k] if gather else ins[k].at[jme], outs[k].at[jme], lsem.at[k]))
            for p, (px, py, jp) in enumerate(_chip_peers(mx, my)):
                src = ins[k] if gather else ins[k].at[jp]
                sem = dict(send_sem=ssem.at[3 * k + p], recv_sem=rsem.at[3 * k + p], device_id=(px, py, mc),
                           device_id_type=MESH)
                sends.append(pltpu.make_async_remote_copy(src_ref=src, dst_ref=outs[k].at[jme], **sem))
                waits.append(pltpu.make_async_remote_copy(src_ref=src, dst_ref=outs[k].at[jp], **sem))
        return local, sends, waits

    def start(self, ins, outs, sems):
        local, sends, _ = self._copies(ins, outs, sems)
        for cp in local + sends:
            cp.start()

    def wait(self, ins, outs, sems):
        local, _, waits = self._copies(ins, outs, sems)
        for cp in waits:
            cp.wait_recv()
        for cp in waits:
            cp.wait_send()
        for cp in local:
            cp.wait()


def chip_exchange(kind, arrays, name):
    ex = ChipExchange(kind, arrays)
    n = ex.n

    def body(*refs):
        ins, outs, sems = refs[:n], refs[n:2 * n], refs[2 * n:]
        ex.start(ins, outs, sems)
        ex.wait(ins, outs, sems)

    return pl.pallas_call(body, name=name, in_specs=[ANY] * n, out_specs=[ANY] * n, out_shape=ex.out_shape,
                          scratch_shapes=ex.scratch)(*arrays)


def sibling_add(g, ngrp, hr, tr, name):
    c_ = g.shape[1]
    nt = hr // tr

    def body(cidx, keep_ref, give_ref, o_ref, land, ssem, rsem):
        mx, my, mc = _me()
        t = pl.program_id(0) * nt + pl.program_id(1)
        s = t % 2
        cp = pltpu.make_async_remote_copy(src_ref=give_ref, dst_ref=land.at[s], send_sem=ssem.at[s], recv_sem=rsem.at[s],
                                          device_id=(mx, my, 1 - mc), device_id_type=MESH)
        cp.start()
        cp.wait_recv()
        o_ref[...] = keep_ref[...] + land[s]
        cp.wait_send()

    grid_spec = pltpu.PrefetchScalarGridSpec(
        num_scalar_prefetch=1, grid=(ngrp, nt),
        in_specs=[pl.BlockSpec((tr, c_), lambda i, j, cr: ((2 * i + cr[0]) * nt + j, 0)),
                  pl.BlockSpec((tr, c_), lambda i, j, cr: ((2 * i + 1 - cr[0]) * nt + j, 0))],
        out_specs=pl.BlockSpec((tr, c_), lambda i, j, cr: (i * nt + j, 0)),
        scratch_shapes=[pltpu.VMEM((2, tr, c_), F32), pltpu.SemaphoreType.DMA((2,)), pltpu.SemaphoreType.DMA((2,))])
    cidx = lax.axis_index("c").astype(jnp.int32).reshape(1)
    return pl.pallas_call(body, grid_spec=grid_spec, name=name, out_shape=jax.ShapeDtypeStruct((ngrp * hr, c_), F32),
                          compiler_params=_cparams(("arbitrary", "arbitrary")))(cidx, g, g)


def finish_grad(b, tr, name):
    _, r2, c_ = b.shape

    def body(b_ref, g_ref, mine, land, ssem, rsem):
        mx, my, mc = _me()
        t = pl.program_id(0)
        s = t % 2
        mine[s] = (b_ref[0].astype(F32) + b_ref[1].astype(F32)) + (b_ref[2].astype(F32) + b_ref[3].astype(F32))
        cp = pltpu.make_async_remote_copy(src_ref=mine.at[s], dst_ref=land.at[s], send_sem=ssem.at[s], recv_sem=rsem.at[s],
                                          device_id=(mx, my, 1 - mc), device_id_type=MESH)
        cp.start()
        cp.wait_recv()
        g_ref[mc] = mine[s]
        g_ref[1 - mc] = land[s]
        cp.wait_send()

    return pl.pallas_call(
        body, grid=(r2 // tr,), name=name,
        in_specs=[pl.BlockSpec((NCHIP, tr, c_), lambda i: (0, i, 0))],
        out_specs=pl.BlockSpec((2, tr, c_), lambda i: (0, i, 0)), out_shape=jax.ShapeDtypeStruct((2, r2, c_), F32),
        scratch_shapes=[pltpu.VMEM((2, tr, c_), F32), pltpu.VMEM((2, tr, c_), F32), pltpu.SemaphoreType.DMA((2,)),
                        pltpu.SemaphoreType.DMA((2,))],
        compiler_params=_cparams(("arbitrary",)))(b)


TM_IN = 512
TM_GLA = 512
TM_MERGE = 512
TM_FFN = 256
TN_WGRAD = 2048

WEIGHTS = ['c_ctx', 'w_mod', 'b_mod', 'norm1_g', 'norm2_g', 'w_in', 'conv_w', 'conv_b', 'conv_ln_g', 'conv_ln_b', 'w_a2_f',
           'b_a_f', 'w_a2_b', 'b_a_b', 'gla_norm_g', 'w_out', 'w_gate', 'w_up', 'w_down', 'final_g']
BIG = ['w_in', 'w_out', 'w_gate', 'w_up', 'w_down']


def _rows(*vs):
    w = vs[0].size
    row = lax.broadcasted_iota(jnp.int32, (8, w), 0)
    out = jnp.zeros((8, w), F32)
    for i, v in enumerate(vs):
        out = jnp.where(row == i, v.reshape(1, w), out)
    return out


def _small_slab(p):
    cat = lambda *ks: jnp.concatenate([p[k].reshape(-1) for k in ks])
    vecs = _rows(p['c_ctx'], p['norm1_g'], p['norm2_g'], p['final_g'], cat('conv_b', 'conv_ln_g'),
                 cat('conv_ln_b', 'b_a_f', 'b_a_b'), jnp.pad(p['gla_norm_g'].reshape(-1), (0, D - HV)))
    bmod = jnp.pad(p['b_mod'].reshape(6, D), ((0, 2), (0, 0)))
    shards = jnp.pad(jnp.concatenate([jnp.pad(p['conv_w'].reshape(-1), (0, DC // NCHIP)), cat('w_a2_f', 'w_a2_b')]),
                     (0, 2 * D)).reshape(8, D)
    return jnp.concatenate([vecs, bmod, shards], axis=0)


def _unslab(s):
    return {
        'c_ctx': s[0], 'norm1_g': s[1:2], 'norm2_g': s[2:3], 'final_g': s[3],
        'conv_b': s[4:5, :DC], 'conv_ln_g': s[4:5, DC:], 'conv_ln_b': s[5:6, :DC],
        'b_a_f': s[5:6, DC:DC + DK], 'b_a_b': s[5:6, DC + DK:], 'gla_norm_g': s[6:7, :HV],
        'b_mod': s[8:14].reshape(1, 6 * D),
        'conv_w': s[16:20].reshape(32, DC // NCHIP)[:CW].reshape(1, CW, DC // NCHIP),
        'w_a2_f': s[20].reshape(1, RANK, DK // NCHIP), 'w_a2_b': s[21].reshape(1, RANK, DK // NCHIP),
    }


def kernel(x, c, ctx, c_ctx, w_mod, b_mod, norm1_g, norm2_g, w_in, conv_w, conv_b, conv_ln_g, conv_ln_b, w_a2_f, b_a_f, w_a2_b, b_a_b, gla_norm_g, w_out, w_gate, w_up, w_down, final_g, loss_target, m_c_ctx, m_w_mod, m_b_mod, m_norm1_g, m_norm2_g, m_w_in, m_conv_w, m_conv_b, m_conv_ln_g, m_conv_ln_b, m_w_a2_f, m_b_a_f, m_w_a2_b, m_b_a_b, m_gla_norm_g, m_w_out, m_w_gate, m_w_up, m_w_down, m_final_g, v_c_ctx, v_w_mod, v_b_mod, v_norm1_g, v_norm2_g, v_w_in, v_conv_w, v_conv_b, v_conv_ln_g, v_conv_ln_b, v_w_a2_f, v_b_a_f, v_w_a2_b, v_b_a_b, v_gla_norm_g, v_w_out, v_w_gate, v_w_up, v_w_down, v_final_g):
    w = dict(c_ctx=c_ctx, w_mod=w_mod, b_mod=b_mod, norm1_g=norm1_g, norm2_g=norm2_g, w_in=w_in, conv_w=conv_w, conv_b=conv_b,
             conv_ln_g=conv_ln_g, conv_ln_b=conv_ln_b, w_a2_f=w_a2_f, b_a_f=b_a_f, w_a2_b=w_a2_b, b_a_b=b_a_b,
             gla_norm_g=gla_norm_g, w_out=w_out, w_gate=w_gate, w_up=w_up, w_down=w_down, final_g=final_g)
    m = dict(c_ctx=m_c_ctx, w_mod=m_w_mod, b_mod=m_b_mod, norm1_g=m_norm1_g, norm2_g=m_norm2_g, w_in=m_w_in, conv_w=m_conv_w,
             conv_b=m_conv_b, conv_ln_g=m_conv_ln_g, conv_ln_b=m_conv_ln_b, w_a2_f=m_w_a2_f, b_a_f=m_b_a_f, w_a2_b=m_w_a2_b,
             b_a_b=m_b_a_b, gla_norm_g=m_gla_norm_g, w_out=m_w_out, w_gate=m_w_gate, w_up=m_w_up, w_down=m_w_down,
             final_g=m_final_g)
    v = dict(c_ctx=v_c_ctx, w_mod=v_w_mod, b_mod=v_b_mod, norm1_g=v_norm1_g, norm2_g=v_norm2_g, w_in=v_w_in, conv_w=v_conv_w,
             conv_b=v_conv_b, conv_ln_g=v_conv_ln_g, conv_ln_b=v_conv_ln_b, w_a2_f=v_w_a2_f, b_a_f=v_b_a_f, w_a2_b=v_w_a2_b,
             b_a_b=v_b_a_b, gla_norm_g=v_gla_norm_g, w_out=v_w_out, w_gate=v_w_gate, w_up=v_w_up, w_down=v_w_down,
             final_g=v_final_g)
    mx, my, mc = _me()
    jme = 2 * mx + my
    me = 4 * mx + 2 * my + mc
    wmc = D * 6 // NCHIP
    xx, tgt, cx = x[0], loss_target[0], ctx[0]

    bshard = [w[k][0].astype(BF) for k in BIG]
    sw = jnp.concatenate([jnp.pad(conv_w[0], ((0, 1), (0, 0))).reshape(-1), w_a2_f[0].reshape(-1), w_a2_b[0].reshape(-1)])
    small = jnp.concatenate([_rows(c[0]), jnp.pad(sw.reshape(6, D), ((0, 2), (0, 0)))], axis=0)
    cs8, mall, win_g = prologue(small, _rows(c_ctx), w_mod[0], lax.dynamic_slice_in_dim(b_mod, jme * wmc, wmc, axis=1),
                                bshard[0])
    cext = jnp.concatenate([cs8[:, 0, :], _rows(c_ctx)], axis=0)
    swc = jnp.stack([cs8[2 * j, 8:16] for j in range(NCHIP)]).reshape(NCHIP, 8 * D)
    convw = jnp.transpose(swc[:, :32 * 128].reshape(NCHIP, 32, 128), (1, 0, 2)).reshape(32, DC)
    a2 = lambda o: jnp.transpose(swc[:, o:o + RANK * 64].reshape(NCHIP, RANK, 64), (1, 0, 2)).reshape(RANK, DK)
    wa2 = jnp.zeros((128, 2 * DK), F32).at[0:RANK, 0:DK].set(a2(32 * 128)).at[RANK:2 * RANK, DK:].set(a2(32 * 128 + RANK * 64))
    wa2 = wa2.astype(BF)
    mall = jnp.concatenate([mall[2 * j] for j in range(NCHIP)], axis=1)
    sh1, sc1, g1, sh2, sc2, g2 = jnp.split(lax.dynamic_slice_in_dim(mall, me, 1, axis=0)[0], 6)
    csh1, csc1 = mall[8, :D], mall[8, D:2 * D]
    cols = lambda a: jnp.transpose(a, (1, 0, 2)).reshape(a.shape[1], -1)
    win = jnp.pad(cols(win_g), ((0, 0), (0, DINP - DIN)))
    ba = jnp.concatenate([b_a_f, b_a_b], axis=1)
    cvec = _rows(conv_b, conv_ln_g, conv_ln_b)
    vec1 = _rows(norm1_g, sh1, sc1)
    vecc = _rows(norm1_g, csh1, csc1)
    vecm = _rows(g1)
    vecf = _rows(norm2_g, sh2, sc2, g2, final_g)
    gn = jnp.tile(gla_norm_g, (1, NH))

    s0 = ctx_fwd(cx, vecc, win, wa2, ba)
    res = fwd_in(xx, vec1, win, convw, cvec, wa2, ba, TM_IN, ChipExchange("gather", bshard[1:]), bshard[1:])
    ag, yb, co, qk, vv, gg, la, r = res[:8]
    wout = res[8].reshape(D, D)
    wg, wu = cols(res[9]), cols(res[10])
    wd = res[11].reshape(DFF, D)
    o_f, o_b, se_f, se_b = gla_fwd(qk, vv, la, s0, TM_GLA)
    x1, y1, cat = merge_fwd(xx, o_f, o_b, gg, co, vecm, gn, wout, TM_MERGE)

    dx1, h2, act, dgt, dup, dy2, sf = ffn_fwd_bwd(x1, tgt, vecf, wg, wu, wd, TM_FFN)
    d_wg = wgrad(h2, dgt, None, D, DFF // 2, TN_WGRAD, "wgrad_gate")
    d_wu = wgrad(h2, dup, None, D, DFF // 2, TN_WGRAD, "wgrad_up")
    d_wd = wgrad(act, dy2, None, DFF // 2, D, TN_WGRAD, "wgrad_down")
    dy1, dco, do, dg, s1, s2 = merge_bwd(dx1, y1, o_f, o_b, gg, vecm, gn, wout, TM_MERGE)
    d_wout = wgrad(cat, dy1, None, D, D, TN_WGRAD, "wgrad_out")

    shard = lambda a, k: jnp.transpose(a.reshape(a.shape[0], NCHIP, k), (1, 0, 2))
    hd = D // 2
    parts = [sibling_add(d_wout, NCHIP, hd // NCHIP, hd // NCHIP, "xadd_w_out").reshape(NCHIP, hd // NCHIP, D),
             shard(sibling_add(d_wg, 1, hd, hd // 2, "xadd_w_gate"), DFF // NCHIP),
             shard(sibling_add(d_wu, 1, hd, hd // 2, "xadd_w_up"), DFF // NCHIP),
             sibling_add(d_wd, NCHIP, DFF // 8, DFF // 8, "xadd_w_down").reshape(NCHIP, DFF // 8, D)]
    parts = [p.astype(BF) for p in parts]
    res = gla_bwd(qk, vv, la, do, se_f, se_b, TM_GLA, ChipExchange("scatter", parts), parts)
    dqk_f, dv_f, dla_f, dqk_b, dv_b, dla_b, ds0 = res[:7]
    recv = list(res[7:])
    dwin_c, dwa2_c, sc = ctx_bwd(cx, vecc, win, wa2, ba, ds0)
    grad_x, h, dp, dwa2, dcw, sd = bwd_in(xx, dx1, ag, yb, dco, dqk_f, dqk_b, dv_f, dv_b, dg, dla_f, dla_b, la, r,
                                          vec1, win, convw, cvec, wa2, TM_IN)
    d_win = wgrad(h, dp, dwin_c, D, DINP // 3, TN_WGRAD, "wgrad_in")
    part_in = shard(sibling_add(d_win, 1, hd, hd // 2, "xadd_w_in")[:, :DIN], DIN // NCHIP).astype(BF)
    recv = list(chip_exchange("scatter", [part_in], "scatter_w_in")) + recv

    rows16, dcw_t, dwa2_t = pack_small(sf, s1, s2, sd, sc, dcw, dwa2, dwa2_c)
    sp = jnp.concatenate([rows16, dcw_t.reshape(16, D), dwa2_t.reshape(16, D)], axis=0)
    g8 = all_gather8(sp, "gather_small_grads")
    tot, bm_g, loss8 = small_totals(g8)
    loss = loss8[0, 0]
    dmod8 = g8[:, 0:6, :].reshape(NDEV, 6 * D)
    dmodc = jnp.concatenate([tot[6], tot[7], jnp.zeros((4 * D,), F32)])
    dm = jnp.concatenate([dmod8, _rows(dmodc)], axis=0)
    dm = lax.dynamic_slice_in_dim(dm, jme * wmc, wmc, axis=1)
    g_wmod, dsil = mod_bwd(cext, dm, w_mod[0])
    p8 = all_gather8(dsil[8:16], "gather_dsilu")
    g_cctx = cctx_grad(p8, _rows(c_ctx))[0]

    grads, delta, new_m, new_v = {}, {}, {}, {}
    for i, k in enumerate(BIG):
        r2 = recv[i].shape[1]
        gk = finish_grad(recv[i], r2 // 2 if r2 >= 512 else r2, "finish_" + k).reshape(w[k].shape[1:])
        rk = gk.shape[0]
        outs = adamw(w[k][0], gk, m[k][0], v[k][0], rk // 2 if rk >= 512 else rk, "adamw_" + k, emit_grad=True)
        delta[k], new_m[k], new_v[k], grads[k] = (o[None] for o in outs)
    grads['w_mod'] = g_wmod[None]
    d_, m_, v_ = adamw(w_mod[0], g_wmod, m_w_mod[0], v_w_mod[0], 256, "adamw_w_mod")
    delta['w_mod'], new_m['w_mod'], new_v['w_mod'] = d_[None], m_[None], v_[None]
    small_g = {
        'c_ctx': g_cctx, 'b_mod': bm_g[0:6].reshape(1, 6 * D), 'norm1_g': tot[8:9], 'norm2_g': tot[9:10], 'final_g': tot[10],
        'conv_b': tot[11:12, :DC], 'conv_ln_g': tot[11:12, DC:], 'conv_ln_b': tot[12:13, :DC],
        'b_a_f': tot[12:13, DC:DC + DK], 'b_a_b': tot[12:13, DC + DK:], 'gla_norm_g': tot[13:14, :HV],
        'conv_w': lax.dynamic_slice_in_dim(tot[16:32].reshape(32, DC)[:CW], jme * (DC // NCHIP), DC // NCHIP, axis=1)[None],
        'w_a2_f': lax.dynamic_slice_in_dim(tot[32:48].reshape(32, 2 * DK)[0:RANK, 0:DK], jme * (DK // NCHIP), DK // NCHIP, axis=1)[None],
        'w_a2_b': lax.dynamic_slice_in_dim(tot[32:48].reshape(32, 2 * DK)[RANK:2 * RANK, DK:], jme * (DK // NCHIP), DK // NCHIP, axis=1)[None],
    }
    grads.update(small_g)
    sd_, sm_, sv_ = adamw(_small_slab(w), _small_slab(small_g), _small_slab(m), _small_slab(v), 24,
                          "adamw_small")
    for dst, slab in ((delta, sd_), (new_m, sm_), (new_v, sv_)):
        dst.update(_unslab(slab))
    out = [loss, grad_x[None]]
    for group in (grads, delta, new_m, new_v):
        out += [group[k].reshape(w[k].shape) for k in WEIGHTS]
    return tuple(out)
```

```python
import jax
import jax.numpy as jnp
from jax import lax
from jax.experimental import pallas as pl
from jax.experimental.pallas import tpu as pltpu

F32 = jnp.float32
BF = jnp.bfloat16

D = 1024
DC = 512
NH = 4
HK = 64
HV = 128
DK = NH * HK
DV = NH * HV
RANK = 16
CH = 64
GW = 64
CW = 31
SEGP = GW + 32
DFF = 2816
DIN = 2592
DINP = 2688
EPS = 1e-6
TAU = 16.0
QSCALE = HK ** -0.5
NCHIP = 4
NDEV = 8

ADAM_LR = 0.001
ADAM_B1 = 0.9
ADAM_B2 = 0.999
ADAM_EPS = 1e-08
ADAM_WD = 0.01
ADAM_STEP = 10

VMEM_LIMIT = 56 * 1024 * 1024
MESH = pl.DeviceIdType.MESH


def _dot(a, b):
    return jnp.dot(a, b, preferred_element_type=F32)


def _dot_nt(a, b):
    return lax.dot_general(a, b, (((1,), (1,)), ((), ())), preferred_element_type=F32)


def _dot_tn(a, b):
    return lax.dot_general(a, b, (((0,), (0,)), ((), ())), preferred_element_type=F32)


def _split3(x):
    hi = x.astype(BF)
    r1 = x - hi.astype(F32)
    mid = r1.astype(BF)
    lo = (r1 - mid.astype(F32)).astype(BF)
    return hi, mid, lo


def _mask_dot(t, x):
    hi, mid, lo = _split3(x)
    return _dot(t, hi) + _dot(t, mid) + _dot(t, lo)


def _sigmoid(x):
    return 1.0 / (1.0 + jnp.exp(-x))


def _log_sigmoid(x):
    return jnp.minimum(x, 0.0) - jnp.log(1.0 + jnp.exp(-jnp.abs(x)))


def _colsum8(z):
    t, c = z.shape
    return jnp.sum(z.reshape(t // 8, 8, c), axis=0)


def _tri(n, kind):
    r = lax.broadcasted_iota(jnp.int32, (n, n), 0)
    c = lax.broadcasted_iota(jnp.int32, (n, n), 1)
    m = {"le": c <= r, "lt": c < r, "ge": c >= r, "gt": c > r}[kind]
    return m


def _full(shape):
    nd = len(shape)
    return pl.BlockSpec(shape, lambda *_: (0,) * nd)


def _cparams(sem, vmem=VMEM_LIMIT):
    return pltpu.CompilerParams(dimension_semantics=sem, vmem_limit_bytes=vmem)


def _call(body, grid, name, in_specs, out_specs, out_shape, scratch, operands, exchange=None, carried=()):
    n_in, n_out, n_scr = len(in_specs), len(out_specs), len(scratch)
    if exchange is None:
        fn = body
    else:
        n = exchange.n

        def fn(*refs):
            ins, cin = refs[:n_in], refs[n_in:n_in + n]
            outs, cout = refs[n_in + n:n_in + n + n_out], refs[n_in + n + n_out:n_in + 2 * n + n_out]
            rest = refs[n_in + 2 * n + n_out:]
            scr, sems = rest[:n_scr], rest[n_scr:]

            @pl.when(pl.program_id(0) == 0)
            def _():
                exchange.start(cin, cout, sems)

            body(*ins, *outs, *scr)

            @pl.when(pl.program_id(0) == pl.num_programs(0) - 1)
            def _():
                exchange.wait(cin, cout, sems)

        any_spec = pl.BlockSpec(memory_space=pl.ANY)
        in_specs = list(in_specs) + [any_spec] * n
        out_specs = list(out_specs) + [any_spec] * n
        out_shape = list(out_shape) + exchange.out_shape
        scratch = list(scratch) + exchange.scratch
    return pl.pallas_call(fn, grid=grid, name=name, in_specs=in_specs, out_specs=out_specs, out_shape=out_shape,
                          scratch_shapes=scratch, compiler_params=_cparams(("arbitrary",)))(*operands, *carried)


def _fill_padded(pad_ref, val, nseg):
    zeros = jnp.zeros((nseg, 16, val.shape[-1]), F32)
    pad_ref[:, 0:16, :] = zeros
    pad_ref[:, 16 + GW:SEGP, :] = zeros
    pad_ref[:, 16:16 + GW, :] = val.reshape(nseg, GW, val.shape[-1])


def _tap_slabs(pad_ref, s, cs):
    whole = pad_ref[s, :, cs]
    for r in range(8):
        slab = whole if r == 0 else pltpu.roll(whole, SEGP - r, axis=0)
        for a in range(4):
            j = r + 8 * a - 1
            if 0 <= j < CW:
                yield j, slab[8 * a:8 * a + GW]


def _conv_taps(pad_ref, s, w_ref, c0, cw, flip):
    acc = jnp.zeros((GW, cw), F32)
    for j, rows in _tap_slabs(pad_ref, s, pl.ds(c0, cw)):
        acc = acc + w_ref[pl.ds((CW - 1 - j) if flip else j, 1), pl.ds(c0, cw)] * rows
    return acc


def _ln_stats(yb):
    mu = jnp.mean(yb, axis=-1, keepdims=True)
    yc = yb - mu
    var = jnp.mean(yc * yc, axis=-1, keepdims=True)
    rs = lax.rsqrt(var + EPS)
    return yc * rs, rs


def fwd_in(x, vec1, win, convw, cvec, wa2, ba, tm, exchange=None, carried=()):
    n = x.shape[0]
    nseg = tm // GW
    cg = 128

    def body(x_ref, vec_ref, win_ref, cw_ref, cv_ref, wa2_ref, ba_ref,
             ag_ref, yb_ref, co_ref, qk_ref, v_ref, g_ref, la_ref, r_ref, pad_ref):
        xx = x_ref[...]
        rstd = lax.rsqrt(jnp.mean(xx * xx, axis=-1, keepdims=True) + EPS)
        h = ((xx * rstd * vec_ref[0:1, :]) * (1.0 + vec_ref[2:3, :]) + vec_ref[1:2, :]).astype(BF)
        pc = _dot(h, win_ref[:, :2 * DC])
        ag_ref[...] = pc.astype(BF)
        _fill_padded(pad_ref, pc[:, :DC] * _sigmoid(pc[:, DC:]), nseg)
        for s in range(nseg):
            for c0 in range(0, DC, cg):
                y = _conv_taps(pad_ref, s, cw_ref, c0, cg, False)
                yb_ref[pl.ds(s * GW, GW), pl.ds(c0, cg)] = y + cv_ref[0:1, c0:c0 + cg]
        p = _dot(h, win_ref[:, 2 * DC:])
        qk_ref[...] = p[:, :2 * DK].astype(BF)
        v_ref[...] = p[:, 2 * DK:2 * DK + DV].astype(BF)
        g_ref[...] = p[:, 2 * DK + DV:2 * DK + 2 * DV].astype(BF)
        r = p[:, 2 * DK + 2 * DV:].astype(BF)
        r_ref[...] = r
        la_ref[...] = _log_sigmoid(_dot(r, wa2_ref[...]) + ba_ref[...]) * (1.0 / TAU)
        yn, _ = _ln_stats(yb_ref[...])
        ln = yn * cv_ref[1:2, :] + cv_ref[2:3, :]
        co_ref[...] = (ln * _sigmoid(ln)).astype(BF)

    tok = lambda w: pl.BlockSpec((tm, w), lambda i: (i, 0))
    return _call(
        body, (n // tm,), "fwd_in",
        [tok(D), _full(vec1.shape), _full(win.shape), _full(convw.shape), _full(cvec.shape), _full(wa2.shape), _full(ba.shape)],
        [tok(2 * DC), tok(DC), tok(DC), tok(2 * DK), tok(DV), tok(DV), tok(2 * DK), tok(128)],
        [jax.ShapeDtypeStruct((n, 2 * DC), BF), jax.ShapeDtypeStruct((n, DC), F32),
         jax.ShapeDtypeStruct((n, DC), BF), jax.ShapeDtypeStruct((n, 2 * DK), BF),
         jax.ShapeDtypeStruct((n, DV), BF), jax.ShapeDtypeStruct((n, DV), BF),
         jax.ShapeDtypeStruct((n, 2 * DK), F32), jax.ShapeDtypeStruct((n, 128), BF)],
        [pltpu.VMEM((nseg, SEGP, DC), F32)],
        (x, vec1, win, convw, cvec, wa2, ba), exchange, carried)


def _gla_dir(d):
    return (_tri(CH, "le"), CH - 1) if d == 0 else (_tri(CH, "ge"), 0)


def _gla_chunk_terms(qk, la, d):
    seen, last = _gla_dir(d)
    b = _mask_dot(seen.astype(BF), la)
    bl = b[last:last + 1, :]
    eb = jnp.exp(b)
    enb = jnp.exp(-b)
    ekd = jnp.exp(bl - b)
    ebl = jnp.exp(bl)
    q = qk[:, :DK].astype(F32) * QSCALE
    k = qk[:, DK:].astype(F32)
    return eb, enb, ekd, ebl, q * eb, k * enb, k * ekd


NP = NH // 2
PW = 2 * HK


def _lo_lanes(shape):
    return lax.broadcasted_iota(jnp.int32, shape, len(shape) - 1) < HK


def _pair_sel(lo, hi):
    return jnp.where(_lo_lanes(lo.shape), lo, hi)


def _only(x, which):
    keep = _lo_lanes(x.shape) if which == 0 else jnp.logical_not(_lo_lanes(x.shape))
    return jnp.where(keep, x, jnp.zeros_like(x))


def gla_fwd(qk, v, la, s0, tm):
    n = qk.shape[0]
    nt = n // tm
    nc = tm // CH

    def body(qkf_ref, vf_ref, laf_ref, qkb_ref, vb_ref, lab_ref, s0_ref, of_ref, ob_ref, sef_ref, seb_ref, st_ref):
        @pl.when(pl.program_id(0) == 0)
        def _():
            st_ref[...] = s0_ref[...]

        def chunk(ci, carry):
            t = []
            for d, (qk_ref, v_ref, la_ref) in enumerate(((qkf_ref, vf_ref, laf_ref), (qkb_ref, vb_ref, lab_ref))):
                c = ci if d == 0 else nc - 1 - ci
                rows = pl.ds(pl.multiple_of(c * CH, CH), CH)
                eb, enb, ekd, ebl, qt, kt, kd = _gla_chunk_terms(qk_ref[rows, :], la_ref[rows, :], d)
                t.append(dict(c=c, rows=rows, ebl=ebl, qt=qt.astype(BF), kt=kt.astype(BF), kd=kd.astype(BF),
                              vv=v_ref[rows, :], st=[st_ref[d, p] for p in range(NP)], amask=_gla_dir(d)[0]))
            dh = [(d, h) for d in range(2) for h in range(NH)]
            dp = [(d, p) for d in range(2) for p in range(NP)]
            ps = lambda h: slice((h // 2) * PW, (h // 2 + 1) * PW)
            vs = lambda h: slice(h * HV, (h + 1) * HV)
            v2 = lambda p: slice(2 * p * HV, 2 * (p + 1) * HV)
            qm = {(d, h): _only(t[d]['qt'][:, ps(h)], h % 2) for d, h in dh}
            q2 = {(d, p): jnp.concatenate([qm[d, 2 * p], qm[d, 2 * p + 1]], axis=0) for d, p in dp}
            a2 = {(d, p): _dot_nt(q2[d, p], t[d]['kt'][:, p * PW:(p + 1) * PW]) for d, p in dp}
            a = {(d, h): jnp.where(t[d]['amask'], a2[d, h // 2][(h % 2) * CH:(h % 2 + 1) * CH], 0.0).astype(BF) for d, h in dh}
            oi = {(d, p): _dot_nt(q2[d, p], t[d]['st'][p].astype(BF)) for d, p in dp}
            o = {(d, h): _dot(a[d, h], t[d]['vv'][:, vs(h)]) + oi[d, h // 2][(h % 2) * CH:(h % 2 + 1) * CH] for d, h in dh}
            kv = {(d, p): _dot_tn(t[d]['vv'][:, v2(p)], t[d]['kd'][:, p * PW:(p + 1) * PW]) for d, p in dp}
            for d, (o_ref, se_ref) in enumerate(((of_ref, sef_ref), (ob_ref, seb_ref))):
                for h in range(NH):
                    o_ref[t[d]['rows'], vs(h)] = o[d, h].astype(BF)
                for p in range(NP):
                    se_ref[t[d]['c'], p] = t[d]['st'][p]
                    st_ref[d, p] = (t[d]['ebl'][:, p * PW:(p + 1) * PW] * t[d]['st'][p]
                                    + _pair_sel(kv[d, p][:HV], kv[d, p][HV:]))
            return carry

        lax.fori_loop(0, nc, chunk, 0, unroll=4)

    fw = lambda w, col=0: pl.BlockSpec((tm, w), lambda i: (i, col))
    bw = lambda w, col=0: pl.BlockSpec((tm, w), lambda i: (nt - 1 - i, col))
    se_f = pl.BlockSpec((nc, NP, HV, PW), lambda i: (i, 0, 0, 0))
    se_b = pl.BlockSpec((nc, NP, HV, PW), lambda i: (nt - 1 - i, 0, 0, 0))
    se_shape = jax.ShapeDtypeStruct((n // CH, NP, HV, PW), F32)
    return pl.pallas_call(
        body, grid=(nt,), name="gla_fwd",
        in_specs=[fw(2 * DK), fw(DV), fw(DK, 0), bw(2 * DK), bw(DV), bw(DK, 1), _full(s0.shape)],
        out_specs=[fw(DV), bw(DV), se_f, se_b],
        out_shape=[jax.ShapeDtypeStruct((n, DV), BF), jax.ShapeDtypeStruct((n, DV), BF), se_shape, se_shape],
        scratch_shapes=[pltpu.VMEM((2, NP, HV, PW), F32)],
        compiler_params=_cparams(("arbitrary",)),
    )(qk, v, la, qk, v, la, s0)


def gla_bwd(qk, v, la, do, se_f, se_b, tm, exchange=None, carried=()):
    n = qk.shape[0]
    nt = n // tm
    nc = tm // CH

    def body(qkf_ref, vf_ref, laf_ref, dof_ref, sef_ref, qkb_ref, vb_ref, lab_ref, dob_ref, seb_ref,
             dqkf_ref, dvf_ref, dlaf_ref, dqkb_ref, dvb_ref, dlab_ref, ds0_ref, ds_ref):
        @pl.when(pl.program_id(0) == 0)
        def _():
            ds_ref[...] = jnp.zeros_like(ds_ref)

        def chunk(ci, carry):
            t = []
            for d, (qk_ref, v_ref, la_ref, do_ref, se_ref) in enumerate(
                    ((qkf_ref, vf_ref, laf_ref, dof_ref, sef_ref), (qkb_ref, vb_ref, lab_ref, dob_ref, seb_ref))):
                c = nc - 1 - ci if d == 0 else ci
                rows = pl.ds(pl.multiple_of(c * CH, CH), CH)
                amask, last = _gla_dir(d)
                eb, enb, ekd, ebl, qt, kt, kd = _gla_chunk_terms(qk_ref[rows, :], la_ref[rows, :], d)
                t.append(dict(rows=rows, amask=amask, last=last, eb=eb, enb=enb, ekd=ekd, ebl=ebl, qt=qt, kt=kt, kd=kd,
                              qtb=qt.astype(BF), ktb=kt.astype(BF), kdb=kd.astype(BF), vv=v_ref[rows, :], dd=do_ref[rows, :],
                              st=[se_ref[c, p] for p in range(NP)], dsn=[ds_ref[d, p] for p in range(NP)]))
            dh = [(d, h) for d in range(2) for h in range(NH)]
            dp = [(d, p) for d in range(2) for p in range(NP)]
            ps = lambda h: slice((h // 2) * PW, (h // 2 + 1) * PW)
            vs = lambda h: slice(h * HV, (h + 1) * HV)
            stb = {(d, p): t[d]['st'][p].astype(BF) for d, p in dp}
            dsnb = {(d, p): t[d]['dsn'][p].astype(BF) for d, p in dp}
            qm = {(d, h): _only(t[d]['qtb'][:, ps(h)], h % 2) for d, h in dh}
            km = {(d, h): _only(t[d]['kdb'][:, ps(h)], h % 2) for d, h in dh}
            a2 = {(d, p): _dot_nt(jnp.concatenate([qm[d, 2 * p], qm[d, 2 * p + 1]], axis=0), t[d]['ktb'][:, p * PW:(p + 1) * PW])
                  for d, p in dp}
            a = {(d, h): jnp.where(t[d]['amask'], a2[d, h // 2][(h % 2) * CH:(h % 2 + 1) * CH], 0.0).astype(BF) for d, h in dh}
            da = {(d, h): jnp.where(t[d]['amask'], _dot_nt(t[d]['dd'][:, vs(h)], t[d]['vv'][:, vs(h)]), 0.0).astype(BF)
                  for d, h in dh}
            v2 = lambda p: slice(2 * p * HV, 2 * (p + 1) * HV)
            rows2 = lambda x, d, p: jnp.concatenate([x[d, 2 * p], x[d, 2 * p + 1]], axis=0)
            half = lambda x, h: x[(h % 2) * CH:(h % 2 + 1) * CH]
            dvs = {(d, p): _dot_nt(rows2(km, d, p), dsnb[d, p]) for d, p in dp}
            dv = {(d, h): _dot_tn(a[d, h], t[d]['dd'][:, vs(h)]) + half(dvs[d, h // 2], h) for d, h in dh}
            vrows = lambda d, p: jnp.concatenate([t[d]['vv'][:, vs(2 * p)], t[d]['vv'][:, vs(2 * p + 1)]], axis=0)
            drows = lambda d, p: jnp.concatenate([t[d]['dd'][:, vs(2 * p)], t[d]['dd'][:, vs(2 * p + 1)]], axis=0)
            dkd2 = {(d, p): _dot(vrows(d, p), dsnb[d, p]) for d, p in dp}
            dqt2 = {(d, p): _dot(rows2(da, d, p), t[d]['ktb'][:, p * PW:(p + 1) * PW]) + _dot(drows(d, p), stb[d, p])
                    for d, p in dp}
            dkt2 = {(d, p): _dot_tn(jnp.concatenate([da[d, 2 * p], da[d, 2 * p + 1]], axis=1), t[d]['qtb'][:, p * PW:(p + 1) * PW])
                    for d, p in dp}
            dsq2 = {(d, p): _dot_tn(t[d]['dd'][:, v2(p)], t[d]['qtb'][:, p * PW:(p + 1) * PW]) for d, p in dp}
            two = lambda x, d, p, n: _pair_sel(x[d, p][:n], x[d, p][n:])
            for d, (dqk_ref, dv_ref, dla_ref) in enumerate(((dqkf_ref, dvf_ref, dlaf_ref), (dqkb_ref, dvb_ref, dlab_ref))):
                td = t[d]
                rows = td['rows']
                for h in range(NH):
                    dv_ref[rows, vs(h)] = dv[d, h].astype(BF)
                pair = lambda x: jnp.concatenate([two(x, d, p, CH) for p in range(NP)], axis=1)
                dqt_, dkt_, dkd_ = pair(dqt2), pair(dkt2), pair(dkd2)
                debl = jnp.concatenate([jnp.sum(td['st'][p] * td['dsn'][p], axis=0, keepdims=True) for p in range(NP)], axis=1)
                for p in range(NP):
                    ds_ref[d, p] = two(dsq2, d, p, HV) + td['ebl'][:, p * PW:(p + 1) * PW] * td['dsn'][p]
                dkdkd = dkd_ * td['kd']
                dbl = jnp.sum(dkdkd, axis=0, keepdims=True) + debl * td['ebl']
                is_last = lax.broadcasted_iota(jnp.int32, (CH, DK), 0) == td['last']
                db = dqt_ * td['qt'] - dkt_ * td['kt'] - dkdkd + jnp.where(is_last, dbl, 0.0)
                dqk_ref[rows, :] = jnp.concatenate([dqt_ * td['eb'] * QSCALE, dkt_ * td['enb'] + dkd_ * td['ekd']], axis=1).astype(BF)
                dla_ref[rows, :] = _mask_dot(_gla_dir(1 - d)[0].astype(BF), db)
            return carry

        lax.fori_loop(0, nc, chunk, 0, unroll=4)

        @pl.when(pl.program_id(0) == nt - 1)
        def _():
            ds0_ref[...] = ds_ref[...]

    up = lambda w, col=0: pl.BlockSpec((tm, w), lambda i: (i, col))
    dn = lambda w, col=0: pl.BlockSpec((tm, w), lambda i: (nt - 1 - i, col))
    se_up = pl.BlockSpec((nc, NP, HV, PW), lambda i: (i, 0, 0, 0))
    se_dn = pl.BlockSpec((nc, NP, HV, PW), lambda i: (nt - 1 - i, 0, 0, 0))
    return _call(
        body, (nt,), "gla_bwd",
        [dn(2 * DK), dn(DV), dn(DK, 0), dn(DV), se_dn, up(2 * DK), up(DV), up(DK, 1), up(DV), se_up],
        [dn(2 * DK), dn(DV), dn(DK), up(2 * DK), up(DV), up(DK), _full((2, NP, HV, PW))],
        [jax.ShapeDtypeStruct((n, 2 * DK), BF), jax.ShapeDtypeStruct((n, DV), BF),
         jax.ShapeDtypeStruct((n, DK), F32), jax.ShapeDtypeStruct((n, 2 * DK), BF),
         jax.ShapeDtypeStruct((n, DV), BF), jax.ShapeDtypeStruct((n, DK), F32),
         jax.ShapeDtypeStruct((2, NP, HV, PW), F32)],
        [pltpu.VMEM((2, NP, HV, PW), F32)],
        (qk, v, la, do, se_f, qk, v, la, do, se_b), exchange, carried)


def _head_norm(o):
    ons, rss = [], []
    for h in range(NH):
        oh = o[:, h * HV:(h + 1) * HV]
        rs = lax.rsqrt(jnp.mean(oh * oh, axis=-1, keepdims=True) + EPS)
        ons.append(oh * rs)
        rss.append(rs)
    return ons, rss


def merge_fwd(x, o_f, o_b, g, co, vecm, gn, wout, tm):
    n = x.shape[0]

    def body(x_ref, of_ref, ob_ref, g_ref, co_ref, vec_ref, gn_ref, w_ref, x1_ref, y1_ref, cat_ref):
        o = of_ref[...].astype(F32) + ob_ref[...].astype(F32)
        ons, _ = _head_norm(o)
        gg = g_ref[...].astype(F32)
        sil = gg * _sigmoid(gg)
        cat_ref[:, :DC] = co_ref[...]
        for h in range(NH):
            vs = slice(h * HV, (h + 1) * HV)
            cat_ref[:, DC + h * HV:DC + (h + 1) * HV] = (ons[h] * gn_ref[:, vs] * sil[:, vs]).astype(BF)
        y1 = _dot(cat_ref[...], w_ref[...])
        y1_ref[...] = y1.astype(BF)
        x1_ref[...] = x_ref[...] + vec_ref[0:1, :] * y1

    tok = lambda w: pl.BlockSpec((tm, w), lambda i: (i, 0))
    return pl.pallas_call(
        body, grid=(n // tm,), name="merge_fwd",
        in_specs=[tok(D), tok(DV), tok(DV), tok(DV), tok(DC), _full(vecm.shape), _full(gn.shape), _full(wout.shape)],
        out_specs=[tok(D), tok(D), tok(D)],
        out_shape=[jax.ShapeDtypeStruct((n, D), F32), jax.ShapeDtypeStruct((n, D), BF), jax.ShapeDtypeStruct((n, D), BF)],
        compiler_params=_cparams(("arbitrary",)),
    )(x, o_f, o_b, g, co, vecm, gn, wout)


def merge_bwd(dx1, y1, o_f, o_b, g, vecm, gn, wout, tm):
    n = dx1.shape[0]

    def body(dx1_ref, y1_ref, of_ref, ob_ref, g_ref, vec_ref, gn_ref, w_ref,
             dy1_ref, dco_ref, do_ref, dg_ref, s1_ref, s2_ref):
        @pl.when(pl.program_id(0) == 0)
        def _():
            s1_ref[...] = jnp.zeros_like(s1_ref)
            s2_ref[...] = jnp.zeros_like(s2_ref)

        dx1 = dx1_ref[...]
        s1_ref[...] += _colsum8(dx1 * y1_ref[...].astype(F32))
        dy1 = (dx1 * vec_ref[0:1, :]).astype(BF)
        dy1_ref[...] = dy1
        dcat = _dot_nt(dy1, w_ref[...])
        dco_ref[...] = dcat[:, :DC].astype(BF)
        o = of_ref[...].astype(F32) + ob_ref[...].astype(F32)
        ons, rss = _head_norm(o)
        gg = g_ref[...].astype(F32)
        sg = _sigmoid(gg)
        sil = gg * sg
        dsil = sg * (1.0 + gg * (1.0 - sg))
        for h in range(NH):
            vs = slice(h * HV, (h + 1) * HV)
            do2 = dcat[:, DC + h * HV:DC + (h + 1) * HV]
            gnh = gn_ref[:, vs]
            t = do2 * sil[:, vs]
            s2_ref[:, vs] += _colsum8(t * ons[h])
            don = t * gnh
            do_ref[:, vs] = (rss[h] * (don - ons[h] * jnp.mean(don * ons[h], axis=-1, keepdims=True))).astype(BF)
            dg_ref[:, vs] = (do2 * ons[h] * gnh * dsil[:, vs]).astype(BF)

    tok = lambda w: pl.BlockSpec((tm, w), lambda i: (i, 0))
    return pl.pallas_call(
        body, grid=(n // tm,), name="merge_bwd",
        in_specs=[tok(D), tok(D), tok(DV), tok(DV), tok(DV), _full(vecm.shape), _full(gn.shape), _full(wout.shape)],
        out_specs=[tok(D), tok(DC), tok(DV), tok(DV), _full((8, D)), _full((8, DV))],
        out_shape=[jax.ShapeDtypeStruct((n, D), BF), jax.ShapeDtypeStruct((n, DC), BF), jax.ShapeDtypeStruct((n, DV), BF),
                   jax.ShapeDtypeStruct((n, DV), BF), jax.ShapeDtypeStruct((8, D), F32), jax.ShapeDtypeStruct((8, DV), F32)],
        compiler_params=_cparams(("arbitrary",)),
    )(dx1, y1, o_f, o_b, g, vecm, gn, wout)


def ffn_fwd_bwd(x1, tgt, vecf, wg, wu, wd, tm):
    n = x1.shape[0]

    def body(x1_ref, t_ref, vec_ref, wg_ref, wu_ref, wd_ref,
             dx1_ref, h2_ref, act_ref, dgt_ref, dup_ref, dy2_ref, s_ref):
        @pl.when(pl.program_id(0) == 0)
        def _():
            s_ref[...] = jnp.zeros_like(s_ref)

        n2g, sh2, sc2, g2, fg = (vec_ref[i:i + 1, :] for i in range(5))
        x1 = x1_ref[...]
        r2 = lax.rsqrt(jnp.mean(x1 * x1, axis=-1, keepdims=True) + EPS)
        xn2 = x1 * r2
        h2 = (xn2 * n2g * (1.0 + sc2) + sh2).astype(BF)
        h2_ref[...] = h2
        gt = _dot(h2, wg_ref[...])
        up = _dot(h2, wu_ref[...])
        sg = _sigmoid(gt)
        sil = gt * sg
        act = (sil * up).astype(BF)
        act_ref[...] = act
        y2 = _dot(act, wd_ref[...])
        x2 = x1 + g2 * y2
        r3 = lax.rsqrt(jnp.mean(x2 * x2, axis=-1, keepdims=True) + EPS)
        xn3 = x2 * r3
        e = xn3 * fg - t_ref[...]
        s_ref[40:48, :] += _colsum8(e * e) * (0.5 / D)
        dyo = e * (1.0 / D)
        s_ref[0:8, :] += _colsum8(dyo * xn3)
        dxn3 = dyo * fg
        dx2 = r3 * (dxn3 - xn3 * jnp.mean(dxn3 * xn3, axis=-1, keepdims=True))
        s_ref[8:16, :] += _colsum8(dx2 * y2)
        dy2 = (dx2 * g2).astype(BF)
        dy2_ref[...] = dy2
        dact = _dot_nt(dy2, wd_ref[...])
        dup = (dact * sil).astype(BF)
        dgt = (dact * up * (sg * (1.0 + gt * (1.0 - sg)))).astype(BF)
        dup_ref[...] = dup
        dgt_ref[...] = dgt
        dh2 = _dot_nt(dgt, wg_ref[...]) + _dot_nt(dup, wu_ref[...])
        s_ref[16:24, :] += _colsum8(dh2)
        t = dh2 * xn2
        s_ref[24:32, :] += _colsum8(t * n2g)
        s_ref[32:40, :] += _colsum8(t * (1.0 + sc2))
        dxn2 = dh2 * ((1.0 + sc2) * n2g)
        dx1_ref[...] = dx2 + r2 * (dxn2 - xn2 * jnp.mean(dxn2 * xn2, axis=-1, keepdims=True))

    tok = lambda w: pl.BlockSpec((tm, w), lambda i: (i, 0))
    wspec = lambda a: pl.BlockSpec(a.shape, lambda i: (0, 0), pipeline_mode=pl.Buffered(1))
    return pl.pallas_call(
        body, grid=(n // tm,), name="ffn_fwd_bwd",
        in_specs=[tok(D), tok(D), _full(vecf.shape), wspec(wg), wspec(wu), wspec(wd)],
        out_specs=[tok(D), tok(D), tok(DFF), tok(DFF), tok(DFF), tok(D), _full((48, D))],
        out_shape=[jax.ShapeDtypeStruct((n, D), F32), jax.ShapeDtypeStruct((n, D), BF), jax.ShapeDtypeStruct((n, DFF), BF),
                   jax.ShapeDtypeStruct((n, DFF), BF), jax.ShapeDtypeStruct((n, DFF), BF), jax.ShapeDtypeStruct((n, D), BF),
                   jax.ShapeDtypeStruct((48, D), F32)],
        compiler_params=_cparams(("arbitrary",)),
    )(x1, tgt, vecf, wg, wu, wd)


def wgrad(a, b, init, t1, t2, tn, name):
    n, k1 = a.shape
    k2 = b.shape[1]

    def body(a_ref, b_ref, *rest):
        o_ref = rest[-1]

        @pl.when(pl.program_id(2) == 0)
        def _():
            o_ref[...] = rest[0][...] if init is not None else jnp.zeros_like(o_ref)

        o_ref[...] += _dot_tn(a_ref[...], b_ref[...])

    ospec = pl.BlockSpec((t1, t2), lambda i, j, k: (i, j))
    extra = ([ospec], {2: 0}, (init,)) if init is not None else ([], {}, ())
    return pl.pallas_call(
        body, grid=(k1 // t1, k2 // t2, n // tn), name=name,
        in_specs=[pl.BlockSpec((tn, t1), lambda i, j, k: (k, i)), pl.BlockSpec((tn, t2), lambda i, j, k: (k, j))] + extra[0],
        out_specs=ospec, out_shape=jax.ShapeDtypeStruct((k1, k2), F32), input_output_aliases=extra[1],
        compiler_params=_cparams(("parallel", "parallel", "arbitrary")),
    )(a, b, *extra[2])


def bwd_in(x, dx1, ag, yb, dco, dqk_f, dqk_b, dv_f, dv_b, dg, dla_f, dla_b, la, r, vec1, win, convw, cvec, wa2, tm):
    n = x.shape[0]
    nseg = tm // GW
    cg = 128

    def body(x_ref, dx1_ref, ag_ref, yb_ref, dco_ref, dqkf_ref, dqkb_ref, dvf_ref, dvb_ref, dg_ref, dlaf_ref, dlab_ref,
             la_ref, r_ref, vec_ref, win_ref, cw_ref, cv_ref, wa2_ref,
             gx_ref, h_ref, dp_ref, dwa2_ref, dcw_ref, s_ref, vc_ref, pad2_ref, dvc_ref, dcw8_ref):
        first = pl.program_id(0) == 0

        @pl.when(first)
        def _():
            s_ref[...] = jnp.zeros_like(s_ref)
            dwa2_ref[...] = jnp.zeros_like(dwa2_ref)
            dcw8_ref[...] = jnp.zeros_like(dcw8_ref)

        yn, rs = _ln_stats(yb_ref[...])
        lng = cv_ref[1:2, :]
        ln = yn * lng + cv_ref[2:3, :]
        sgl = _sigmoid(ln)
        dln = dco_ref[...].astype(F32) * (sgl * (1.0 + ln * (1.0 - sgl)))
        dyn = dln * lng
        dyb = rs * (dyn - jnp.mean(dyn, axis=-1, keepdims=True) - yn * jnp.mean(dyn * yn, axis=-1, keepdims=True))
        s_ref[24:32, 0:DC] += _colsum8(dyb)
        s_ref[24:32, DC:D] += _colsum8(dln * yn)
        s_ref[32:40, 0:DC] += _colsum8(dln)

        agv = ag_ref[...].astype(F32)
        a = agv[:, :DC]
        sgg = _sigmoid(agv[:, DC:])
        vc_ref[...] = a * sgg
        _fill_padded(pad2_ref, dyb, nseg)

        dp_ref[:, 2 * DC:2 * DC + 2 * DK] = (dqkf_ref[...].astype(F32) + dqkb_ref[...].astype(F32)).astype(BF)
        dp_ref[:, 2 * DC + 2 * DK:2 * DC + 2 * DK + DV] = (dvf_ref[...].astype(F32) + dvb_ref[...].astype(F32)).astype(BF)
        dp_ref[:, 2 * DC + 2 * DK + DV:2 * DC + 2 * DK + 2 * DV] = dg_ref[...]

        la = la_ref[...]
        dla = jnp.concatenate([dlaf_ref[...], dlab_ref[...]], axis=1)
        dpre = dla * (1.0 - jnp.exp(TAU * la)) * (1.0 / TAU)
        s_ref[32:40, DC:D] += _colsum8(dpre)
        dpreb = dpre.astype(BF)
        dwa2_ref[...] += _dot_tn(r_ref[...], dpreb)
        dp_ref[:, DINP - 128:] = _dot_nt(dpreb, wa2_ref[...]).astype(BF)
        dh_rest = _dot_nt(dp_ref[:, 2 * DC:], win_ref[:, 2 * DC:])

        for s in range(nseg):
            rows = pl.ds(s * GW, GW)
            for c0 in range(0, DC, cg):
                cs = pl.ds(c0, cg)
                vcs = vc_ref[rows, cs]
                acc = jnp.zeros((GW, cg), F32)
                for j, rows_j in _tap_slabs(pad2_ref, s, cs):
                    acc = acc + cw_ref[pl.ds(CW - 1 - j, 1), cs] * rows_j
                    dcw8_ref[CW - 1 - j, :, cs] += _colsum8(vcs * rows_j)
                dvc_ref[rows, cs] = acc
        dvc = dvc_ref[...]
        dp_ref[:, 0:DC] = (dvc * sgg).astype(BF)
        dp_ref[:, DC:2 * DC] = (dvc * a * sgg * (1.0 - sgg)).astype(BF)

        dh = dh_rest + _dot_nt(dp_ref[:, :2 * DC], win_ref[:, :2 * DC])
        xx = x_ref[...]
        n1g, sh1, sc1 = vec_ref[0:1, :], vec_ref[1:2, :], vec_ref[2:3, :]
        rstd = lax.rsqrt(jnp.mean(xx * xx, axis=-1, keepdims=True) + EPS)
        xn = xx * rstd
        h_ref[...] = (xn * n1g * (1.0 + sc1) + sh1).astype(BF)
        s_ref[0:8, :] += _colsum8(dh)
        t = dh * xn
        s_ref[8:16, :] += _colsum8(t * n1g)
        s_ref[16:24, :] += _colsum8(t * (1.0 + sc1))
        dxn = dh * ((1.0 + sc1) * n1g)
        gx_ref[...] = dx1_ref[...] + rstd * (dxn - xn * jnp.mean(dxn * xn, axis=-1, keepdims=True))

        @pl.when(pl.program_id(0) == pl.num_programs(0) - 1)
        def _():
            dcw_ref[...] = jnp.sum(dcw8_ref[...], axis=1)

    tok = lambda w: pl.BlockSpec((tm, w), lambda i: (i, 0))
    return pl.pallas_call(
        body, grid=(n // tm,), name="bwd_in",
        in_specs=[tok(D), tok(D), tok(2 * DC), tok(DC), tok(DC), tok(2 * DK), tok(2 * DK), tok(DV), tok(DV), tok(DV),
                  tok(DK), tok(DK), tok(2 * DK), tok(128), _full(vec1.shape),
                  pl.BlockSpec(win.shape, lambda i: (0, 0), pipeline_mode=pl.Buffered(1)),
                  _full(convw.shape), _full(cvec.shape), _full(wa2.shape)],
        out_specs=[tok(D), tok(D), tok(DINP), _full((128, 2 * DK)), _full((32, DC)), _full((40, D))],
        out_shape=[jax.ShapeDtypeStruct((n, D), F32), jax.ShapeDtypeStruct((n, D), BF), jax.ShapeDtypeStruct((n, DINP), BF),
                   jax.ShapeDtypeStruct((128, 2 * DK), F32), jax.ShapeDtypeStruct((32, DC), F32),
                   jax.ShapeDtypeStruct((40, D), F32)],
        scratch_shapes=[pltpu.VMEM((tm, DC), F32), pltpu.VMEM((nseg, SEGP, DC), F32), pltpu.VMEM((tm, DC), F32),
                        pltpu.VMEM((32, 8, DC), F32)],
        compiler_params=_cparams(("arbitrary",)),
    )(x, dx1, ag, yb, dco, dqk_f, dqk_b, dv_f, dv_b, dg, dla_f, dla_b, la, r, vec1, win, convw, cvec, wa2)


def _ctx_common(ctx_ref, vec_ref, win_ref, wa2_ref, ba_ref):
    cx = ctx_ref[...]
    t = cx.shape[0]
    rstd = lax.rsqrt(jnp.mean(cx * cx, axis=-1, keepdims=True) + EPS)
    xn = cx * rstd
    hc = (xn * vec_ref[0:1, :] * (1.0 + vec_ref[2:3, :]) + vec_ref[1:2, :]).astype(BF)
    k0 = 2 * DC + DK
    kv = _dot(hc, win_ref[:, k0:k0 + DK + DV]).astype(BF).astype(F32)
    r = _dot(hc, win_ref[:, DINP - 128:]).astype(BF)
    la = _log_sigmoid(_dot(r, wa2_ref[...]) + ba_ref[...]) * (1.0 / TAU)
    incl = _tri(t, "le").astype(BF)
    strict = _tri(t, "lt").astype(BF)
    bf = _mask_dot(incl, la[:, :DK])
    wf = jnp.exp(bf[t - 1:t, :] - bf)
    wb = jnp.exp(_mask_dot(strict, la[:, DK:]))
    return xn, hc, kv[:, :DK], kv[:, DK:], r, la, wf, wb


def ctx_fwd(ctx, vecc, win, wa2, ba):
    def body(ctx_ref, vec_ref, win_ref, wa2_ref, ba_ref, s_ref):
        _, _, k, v, _, _, wf, wb = _ctx_common(ctx_ref, vec_ref, win_ref, wa2_ref, ba_ref)
        vb = v.astype(BF)
        for d, w in enumerate((wf, wb)):
            kd = (k * w).astype(BF)
            for h in range(NH):
                s_ref[d, h // 2, :, (h % 2) * HK:(h % 2 + 1) * HK] = _dot_tn(vb[:, h * HV:(h + 1) * HV], kd[:, h * HK:(h + 1) * HK])

    return pl.pallas_call(
        body, name="ctx_fwd", out_shape=jax.ShapeDtypeStruct((2, NP, HV, PW), F32),
        compiler_params=pltpu.CompilerParams(vmem_limit_bytes=VMEM_LIMIT),
    )(ctx, vecc, win, wa2, ba)


def ctx_bwd(ctx, vecc, win, wa2, ba, ds0):
    t = ctx.shape[0]

    def body(ctx_ref, vec_ref, win_ref, wa2_ref, ba_ref, ds_ref, dwin_ref, dwa2_ref, s_ref, dpc_ref):
        xn, hc, k, v, r, la, wf, wb = _ctx_common(ctx_ref, vec_ref, win_ref, wa2_ref, ba_ref)
        vb = v.astype(BF)
        strict = _tri(t, "lt").astype(BF)
        strict_t = _tri(t, "gt").astype(BF)
        dpc_ref[...] = jnp.zeros_like(dpc_ref)
        k0 = 2 * DC + DK
        dk = jnp.zeros((t, DK), F32)
        des = []
        for d, w in enumerate((wf, wb)):
            kd = (k * w).astype(BF)
            dkds = []
            for h in range(NH):
                dsb = ds_ref[d, h // 2, :, (h % 2) * HK:(h % 2 + 1) * HK].astype(BF)
                dkds.append(_dot(vb[:, h * HV:(h + 1) * HV], dsb))
                dvh = _dot_nt(kd[:, h * HK:(h + 1) * HK], dsb)
                vs = slice(k0 + DK + h * HV, k0 + DK + (h + 1) * HV)
                if d == 0:
                    dpc_ref[:, vs] = dvh.astype(BF)
                else:
                    dpc_ref[:, vs] = (dpc_ref[:, vs].astype(F32) + dvh).astype(BF)
            dkd = jnp.concatenate(dkds, axis=1)
            dk = dk + dkd * w
            des.append(dkd * k * w)
        dpc_ref[:, k0:k0 + DK] = dk.astype(BF)
        dla = jnp.concatenate([_mask_dot(strict, des[0]), _mask_dot(strict_t, des[1])], axis=1)
        dpre = dla * (1.0 - jnp.exp(TAU * la)) * (1.0 / TAU)
        dpreb = dpre.astype(BF)
        dwa2_ref[...] = _dot_tn(r, dpreb)
        dpc_ref[:, DINP - 128:] = _dot_nt(dpreb, wa2_ref[...]).astype(BF)
        dpc = dpc_ref[...]
        dwin_ref[...] = _dot_tn(hc, dpc)
        dhc = _dot_nt(dpc, win_ref[...])
        n1g, sc1 = vec_ref[0:1, :], vec_ref[2:3, :]
        tt = dhc * xn
        s_ref[...] = jnp.zeros_like(s_ref)
        s_ref[0:1, :] = jnp.sum(tt * (1.0 + sc1), axis=0, keepdims=True)
        s_ref[1:2, :] = jnp.sum(dhc, axis=0, keepdims=True)
        s_ref[2:3, :] = jnp.sum(tt * n1g, axis=0, keepdims=True)
        s_ref[3:4, DC:D] = jnp.sum(dpre, axis=0, keepdims=True)

    return pl.pallas_call(
        body, name="ctx_bwd",
        out_shape=[jax.ShapeDtypeStruct((D, DINP), F32), jax.ShapeDtypeStruct((128, 2 * DK), F32),
                   jax.ShapeDtypeStruct((8, D), F32)],
        scratch_shapes=[pltpu.VMEM((t, DINP), BF)],
        compiler_params=pltpu.CompilerParams(vmem_limit_bytes=VMEM_LIMIT),
    )(ctx, vecc, win, wa2, ba, ds0)


def _silu(x):
    return x * _sigmoid(x)


def mod_bwd(cext, dm, wm):
    def body(c_ref, d_ref, w_ref, gw_ref, ds_ref):
        dmb = d_ref[...].astype(BF)
        gw_ref[...] = _dot_tn(_silu(c_ref[...]).astype(BF), dmb)
        ds_ref[...] = _dot_nt(dmb, w_ref[...].astype(BF))

    return pl.pallas_call(body, name="mod_bwd",
                          out_shape=[jax.ShapeDtypeStruct(wm.shape, F32), jax.ShapeDtypeStruct(cext.shape, F32)],
                          compiler_params=pltpu.CompilerParams(vmem_limit_bytes=VMEM_LIMIT))(cext, dm, wm)


def pack_small(sf, s1, s2, sd, sc, dcw, dwa2, dwa2_c):
    def body(sf_ref, s1_ref, s2_ref, sd_ref, sc_ref, dcw_ref, dwa2_ref, dwa2c_ref, o_ref, ocw_ref, owa_ref):
        rsum = lambda ref, i: jnp.sum(ref[8 * i:8 * i + 8, :], axis=0, keepdims=True)
        o_ref[...] = jnp.zeros_like(o_ref)
        o_ref[0:1, :] = rsum(sd_ref, 0)
        o_ref[1:2, :] = rsum(sd_ref, 1)
        o_ref[2:3, :] = rsum(s1_ref, 0)
        o_ref[3:4, :] = rsum(sf_ref, 2)
        o_ref[4:5, :] = rsum(sf_ref, 3)
        o_ref[5:6, :] = rsum(sf_ref, 1)
        o_ref[6:7, :] = sc_ref[1:2, :]
        o_ref[7:8, :] = sc_ref[2:3, :]
        o_ref[8:9, :] = rsum(sd_ref, 2) + sc_ref[0:1, :]
        o_ref[9:10, :] = rsum(sf_ref, 4)
        o_ref[10:11, :] = rsum(sf_ref, 0)
        o_ref[11:12, :] = rsum(sd_ref, 3)
        o_ref[12:13, :] = rsum(sd_ref, 4) + sc_ref[3:4, :]
        g = jnp.sum(s2_ref[...], axis=0, keepdims=True)
        o_ref[13:14, 0:HV] = g[:, 0:HV] + g[:, HV:2 * HV] + g[:, 2 * HV:3 * HV] + g[:, 3 * HV:4 * HV]
        o_ref[14:15, :] = rsum(sf_ref, 5)
        ocw_ref[...] = dcw_ref[...]
        owa_ref[...] = dwa2_ref[0:32, :] + dwa2c_ref[0:32, :]

    return pl.pallas_call(body, name="pack_small",
                          out_shape=[jax.ShapeDtypeStruct((16, D), F32), jax.ShapeDtypeStruct((32, DC), F32),
                                     jax.ShapeDtypeStruct((32, 2 * DK), F32)])(sf, s1, s2, sd, sc, dcw, dwa2, dwa2_c)


def small_totals(g8):
    r = g8.shape[1]

    def body(g_ref, t_ref, bm_ref, loss_ref):
        acc = g_ref[0]
        for i in range(1, NDEV):
            acc = acc + g_ref[i]
        t_ref[...] = acc
        bm_ref[...] = jnp.zeros_like(bm_ref)
        bm_ref[0:6, :] = acc[0:6, :]
        bm_ref[0:2, :] += acc[6:8, :]
        loss_ref[...] = jnp.broadcast_to(jnp.sum(acc[14:15, :], axis=1, keepdims=True), loss_ref.shape)

    return pl.pallas_call(body, name="small_totals",
                          out_shape=[jax.ShapeDtypeStruct((r, D), F32), jax.ShapeDtypeStruct((8, D), F32),
                                     jax.ShapeDtypeStruct((8, 128), F32)])(g8)


def cctx_grad(p8, c_ctx_row):
    def body(p_ref, c_ref, o_ref):
        acc = p_ref[0, 0:1, :]
        for j in range(1, NCHIP):
            acc = acc + p_ref[2 * j, 0:1, :]
        cc = c_ref[0:1, :]
        sg = _sigmoid(cc)
        o_ref[...] = jnp.zeros_like(o_ref)
        o_ref[0:1, :] = acc * (sg * (1.0 + cc * (1.0 - sg)))

    return pl.pallas_call(body, name="cctx_grad", out_shape=jax.ShapeDtypeStruct((8, D), F32))(p8, c_ctx_row)


def adamw(w, g, m, v, rows, name, emit_grad=False):
    r, c = w.shape

    def body(w_ref, g_ref, m_ref, v_ref, d_ref, nm_ref, nv_ref, *go_ref):
        gg = g_ref[...]
        nm = ADAM_B1 * m_ref[...] + (1.0 - ADAM_B1) * gg
        nv = ADAM_B2 * v_ref[...] + (1.0 - ADAM_B2) * (gg * gg)
        m_hat = nm / (1.0 - ADAM_B1 ** ADAM_STEP)
        v_hat = nv / (1.0 - ADAM_B2 ** ADAM_STEP)
        d_ref[...] = -ADAM_LR * (m_hat / (jnp.sqrt(v_hat) + ADAM_EPS) + ADAM_WD * w_ref[...])
        nm_ref[...] = nm
        nv_ref[...] = nv
        if emit_grad:
            go_ref[0][...] = gg

    spec = pl.BlockSpec((rows, c), lambda i: (i, 0))
    sds = jax.ShapeDtypeStruct((r, c), F32)
    nout = 4 if emit_grad else 3
    return pl.pallas_call(
        body, grid=(r // rows,), name=name, in_specs=[spec] * 4, out_specs=[spec] * nout, out_shape=[sds] * nout,
        compiler_params=_cparams(("parallel",)),
    )(w, g, m, v)


def _me():
    return lax.axis_index("x"), lax.axis_index("y"), lax.axis_index("c")


def _flip(v, bit):
    return 1 - v if bit else v


ANY = pl.BlockSpec(memory_space=pl.ANY)


def _gather8(x_ref, o_ref, ssem, rsem, lsem):
    mx, my, mc = _me()
    me = 4 * mx + 2 * my + mc
    local = pltpu.make_async_copy(x_ref, o_ref.at[me], lsem)
    local.start()
    peer = lambda k: (_flip(mx, k & 4), _flip(my, k & 2), _flip(mc, k & 1))
    sends = []
    for k in range(1, NDEV):
        cp = pltpu.make_async_remote_copy(src_ref=x_ref, dst_ref=o_ref.at[me], send_sem=ssem.at[k - 1],
                                          recv_sem=rsem.at[k - 1], device_id=peer(k), device_id_type=MESH)
        cp.start()
        sends.append(cp)
    for k in range(1, NDEV):
        px, py, pc = peer(k)
        pltpu.make_async_remote_copy(src_ref=x_ref, dst_ref=o_ref.at[4 * px + 2 * py + pc], send_sem=ssem.at[k - 1],
                                     recv_sem=rsem.at[k - 1], device_id=(px, py, pc), device_id_type=MESH).wait_recv()
    for cp in sends:
        cp.wait_send()
    local.wait()


def _gather8_sems():
    return [pltpu.SemaphoreType.DMA((NDEV - 1,)), pltpu.SemaphoreType.DMA((NDEV - 1,)), pltpu.SemaphoreType.DMA]


def all_gather8(x, name):
    vm = pl.BlockSpec(memory_space=pltpu.VMEM)
    return pl.pallas_call(_gather8_body(), name=name, in_specs=[vm], out_specs=vm,
                          out_shape=jax.ShapeDtypeStruct((NDEV,) + x.shape, x.dtype), scratch_shapes=_gather8_sems())(x)


def _gather8_body():
    def body(x_ref, o_ref, ssem, rsem, lsem):
        _gather8(x_ref, o_ref, ssem, rsem, lsem)
    return body


def prologue(small, c_ctx_rows, wm, bm, w_in_shard):
    ex = ChipExchange("gather", [w_in_shard])

    def body(s_ref, cc_ref, w_ref, b_ref, win_ref, s8_ref, m8_ref, wing_ref, mloc_ref, *sems):
        ex.start([win_ref], [wing_ref], sems[6:])
        _gather8(s_ref, s8_ref, *sems[0:3])
        cext = jnp.concatenate([s8_ref[:, 0, :], cc_ref[...]], axis=0)
        mloc_ref[...] = _dot(_silu(cext).astype(BF), w_ref[...].astype(BF)) + b_ref[...]
        _gather8(mloc_ref, m8_ref, *sems[3:6])
        ex.wait([win_ref], [wing_ref], sems[6:])

    vm = pl.BlockSpec(memory_space=pltpu.VMEM)
    wcols = wm.shape[1]
    return pl.pallas_call(
        body, name="prologue", in_specs=[vm, vm, vm, vm, ANY], out_specs=[vm, vm, ANY],
        out_shape=[jax.ShapeDtypeStruct((NDEV, 16, D), F32), jax.ShapeDtypeStruct((NDEV, 16, wcols), F32)] + ex.out_shape,
        scratch_shapes=[pltpu.VMEM((16, wcols), F32)] + _gather8_sems() + _gather8_sems() + ex.scratch,
        compiler_params=pltpu.CompilerParams(vmem_limit_bytes=VMEM_LIMIT),
    )(small, c_ctx_rows, wm, bm, w_in_shard)


def _chip_peers(mx, my):
    out = []
    for p in range(1, NCHIP):
        px, py = _flip(mx, p & 2), _flip(my, p & 1)
        out.append((px, py, 2 * px + py))
    return out


class ChipExchange:
    def __init__(self, kind, arrays):
        self.kind = kind
        self.n = len(arrays)
        if kind == "gather":
            self.out_shape = [jax.ShapeDtypeStruct((NCHIP,) + a.shape, a.dtype) for a in arrays]
        else:
            self.out_shape = [jax.ShapeDtypeStruct(a.shape, a.dtype) for a in arrays]
        self.scratch = [pltpu.SemaphoreType.DMA((3 * self.n,)), pltpu.SemaphoreType.DMA((3 * self.n,)),
                        pltpu.SemaphoreType.DMA((self.n,))]

    def _copies(self, ins, outs, sems):
        ssem, rsem, lsem = sems
        mx, my, mc = _me()
        jme = 2 * mx + my
        gather = self.kind == "gather"
        local, sends, waits = [], [], []
        for k in range(self.n):
            local.append(pltpu.make_async_copy(ins[k] if gather else ins[k].at[jme], outs[k].at[jme], lsem.at[k]))
            for p, (px, py, jp) in enumerate(_chip_peers(mx, my)):
                src = ins[k] if gather else ins[k].at[jp]
                sem = dict(send_sem=ssem.at[3 * k + p], recv_sem=rsem.at[3 * k + p], device_id=(px, py, mc),
                           device_id_type=MESH)
                sends.append(pltpu.make_async_remote_copy(src_ref=src, dst_ref=outs[k].at[jme], **sem))
                waits.append(pltpu.make_async_remote_copy(src_ref=src, dst_ref=outs[k].at[jp], **sem))
        return local, sends, waits

    def start(self, ins, outs, sems):
        local, sends, _ = self._copies(ins, outs, sems)
        for cp in local + sends:
            cp.start()

    def wait(self, ins, outs, sems):
        local, _, waits = self._copies(ins, outs, sems)
        for cp in waits:
            cp.wait_recv()
        for cp in waits:
            cp.wait_send()
        for cp in local:
            cp.wait()


def chip_exchange(kind, arrays, name):
    ex = ChipExchange(kind, arrays)
    n = ex.n

    def body(*refs):
        ins, outs, sems = refs[:n], refs[n:2 * n], refs[2 * n:]
        ex.start(ins, outs, sems)
        ex.wait(ins, outs, sems)

    return pl.pallas_call(body, name=name, in_specs=[ANY] * n, out_specs=[ANY] * n, out_shape=ex.out_shape,
                          scratch_shapes=ex.scratch)(*arrays)


def sibling_add(g, ngrp, hr, tr, name):
    c_ = g.shape[1]
    nt = hr // tr

    def body(cidx, keep_ref, give_ref, o_ref, land, ssem, rsem):
        mx, my, mc = _me()
        t = pl.program_id(0) * nt + pl.program_id(1)
        s = t % 2
        cp = pltpu.make_async_remote_copy(src_ref=give_ref, dst_ref=land.at[s], send_sem=ssem.at[s], recv_sem=rsem.at[s],
                                          device_id=(mx, my, 1 - mc), device_id_type=MESH)
        cp.start()
        cp.wait_recv()
        o_ref[...] = keep_ref[...] + land[s]
        cp.wait_send()

    grid_spec = pltpu.PrefetchScalarGridSpec(
        num_scalar_prefetch=1, grid=(ngrp, nt),
        in_specs=[pl.BlockSpec((tr, c_), lambda i, j, cr: ((2 * i + cr[0]) * nt + j, 0)),
                  pl.BlockSpec((tr, c_), lambda i, j, cr: ((2 * i + 1 - cr[0]) * nt + j, 0))],
        out_specs=pl.BlockSpec((tr, c_), lambda i, j, cr: (i * nt + j, 0)),
        scratch_shapes=[pltpu.VMEM((2, tr, c_), F32), pltpu.SemaphoreType.DMA((2,)), pltpu.SemaphoreType.DMA((2,))])
    cidx = lax.axis_index("c").astype(jnp.int32).reshape(1)
    return pl.pallas_call(body, grid_spec=grid_spec, name=name, out_shape=jax.ShapeDtypeStruct((ngrp * hr, c_), F32),
                          compiler_params=_cparams(("arbitrary", "arbitrary")))(cidx, g, g)


def finish_grad(b, tr, name):
    _, r2, c_ = b.shape

    def body(b_ref, g_ref, mine, land, ssem, rsem):
        mx, my, mc = _me()
        t = pl.program_id(0)
        s = t % 2
        mine[s] = (b_ref[0].astype(F32) + b_ref[1].astype(F32)) + (b_ref[2].astype(F32) + b_ref[3].astype(F32))
        cp = pltpu.make_async_remote_copy(src_ref=mine.at[s], dst_ref=land.at[s], send_sem=ssem.at[s], recv_sem=rsem.at[s],
                                          device_id=(mx, my, 1 - mc), device_id_type=MESH)
        cp.start()
        cp.wait_recv()
        g_ref[mc] = mine[s]
        g_ref[1 - mc] = land[s]
        cp.wait_send()

    return pl.pallas_call(
        body, grid=(r2 // tr,), name=name,
        in_specs=[pl.BlockSpec((NCHIP, tr, c_), lambda i: (0, i, 0))],
        out_specs=pl.BlockSpec((2, tr, c_), lambda i: (0, i, 0)), out_shape=jax.ShapeDtypeStruct((2, r2, c_), F32),
        scratch_shapes=[pltpu.VMEM((2, tr, c_), F32), pltpu.VMEM((2, tr, c_), F32), pltpu.SemaphoreType.DMA((2,)),
                        pltpu.SemaphoreType.DMA((2,))],
        compiler_params=_cparams(("arbitrary",)))(b)


TM_IN = 512
TM_GLA = 512
TM_MERGE = 512
TM_FFN = 256
TN_WGRAD = 2048

WEIGHTS = ['c_ctx', 'w_mod', 'b_mod', 'norm1_g', 'norm2_g', 'w_in', 'conv_w', 'conv_b', 'conv_ln_g', 'conv_ln_b', 'w_a2_f',
           'b_a_f', 'w_a2_b', 'b_a_b', 'gla_norm_g', 'w_out', 'w_gate', 'w_up', 'w_down', 'final_g']
BIG = ['w_in', 'w_out', 'w_gate', 'w_up', 'w_down']


def _rows(*vs):
    w = vs[0].size
    row = lax.broadcasted_iota(jnp.int32, (8, w), 0)
    out = jnp.zeros((8, w), F32)
    for i, v in enumerate(vs):
        out = jnp.where(row == i, v.reshape(1, w), out)
    return out


def _small_slab(p):
    cat = lambda *ks: jnp.concatenate([p[k].reshape(-1) for k in ks])
    vecs = _rows(p['c_ctx'], p['norm1_g'], p['norm2_g'], p['final_g'], cat('conv_b', 'conv_ln_g'),
                 cat('conv_ln_b', 'b_a_f', 'b_a_b'), jnp.pad(p['gla_norm_g'].reshape(-1), (0, D - HV)))
    bmod = jnp.pad(p['b_mod'].reshape(6, D), ((0, 2), (0, 0)))
    shards = jnp.pad(jnp.concatenate([jnp.pad(p['conv_w'].reshape(-1), (0, DC // NCHIP)), cat('w_a2_f', 'w_a2_b')]),
                     (0, 2 * D)).reshape(8, D)
    return jnp.concatenate([vecs, bmod, shards], axis=0)


def _unslab(s):
    return {
        'c_ctx': s[0], 'norm1_g': s[1:2], 'norm2_g': s[2:3], 'final_g': s[3],
        'conv_b': s[4:5, :DC], 'conv_ln_g': s[4:5, DC:], 'conv_ln_b': s[5:6, :DC],
        'b_a_f': s[5:6, DC:DC + DK], 'b_a_b': s[5:6, DC + DK:], 'gla_norm_g': s[6:7, :HV],
        'b_mod': s[8:14].reshape(1, 6 * D),
        'conv_w': s[16:20].reshape(32, DC // NCHIP)[:CW].reshape(1, CW, DC // NCHIP),
        'w_a2_f': s[20].reshape(1, RANK, DK // NCHIP), 'w_a2_b': s[21].reshape(1, RANK, DK // NCHIP),
    }


def kernel(x, c, ctx, c_ctx, w_mod, b_mod, norm1_g, norm2_g, w_in, conv_w, conv_b, conv_ln_g, conv_ln_b, w_a2_f, b_a_f, w_a2_b, b_a_b, gla_norm_g, w_out, w_gate, w_up, w_down, final_g, loss_target, m_c_ctx, m_w_mod, m_b_mod, m_norm1_g, m_norm2_g, m_w_in, m_conv_w, m_conv_b, m_conv_ln_g, m_conv_ln_b, m_w_a2_f, m_b_a_f, m_w_a2_b, m_b_a_b, m_gla_norm_g, m_w_out, m_w_gate, m_w_up, m_w_down, m_final_g, v_c_ctx, v_w_mod, v_b_mod, v_norm1_g, v_norm2_g, v_w_in, v_conv_w, v_conv_b, v_conv_ln_g, v_conv_ln_b, v_w_a2_f, v_b_a_f, v_w_a2_b, v_b_a_b, v_gla_norm_g, v_w_out, v_w_gate, v_w_up, v_w_down, v_final_g):
    w = dict(c_ctx=c_ctx, w_mod=w_mod, b_mod=b_mod, norm1_g=norm1_g, norm2_g=norm2_g, w_in=w_in, conv_w=conv_w, conv_b=conv_b,
             conv_ln_g=conv_ln_g, conv_ln_b=conv_ln_b, w_a2_f=w_a2_f, b_a_f=b_a_f, w_a2_b=w_a2_b, b_a_b=b_a_b,
             gla_norm_g=gla_norm_g, w_out=w_out, w_gate=w_gate, w_up=w_up, w_down=w_down, final_g=final_g)
    m = dict(c_ctx=m_c_ctx, w_mod=m_w_mod, b_mod=m_b_mod, norm1_g=m_norm1_g, norm2_g=m_norm2_g, w_in=m_w_in, conv_w=m_conv_w,
             conv_b=m_conv_b, conv_ln_g=m_conv_ln_g, conv_ln_b=m_conv_ln_b, w_a2_f=m_w_a2_f, b_a_f=m_b_a_f, w_a2_b=m_w_a2_b,
             b_a_b=m_b_a_b, gla_norm_g=m_gla_norm_g, w_out=m_w_out, w_gate=m_w_gate, w_up=m_w_up, w_down=m_w_down,
             final_g=m_final_g)
    v = dict(c_ctx=v_c_ctx, w_mod=v_w_mod, b_mod=v_b_mod, norm1_g=v_norm1_g, norm2_g=v_norm2_g, w_in=v_w_in, conv_w=v_conv_w,
             conv_b=v_conv_b, conv_ln_g=v_conv_ln_g, conv_ln_b=v_conv_ln_b, w_a2_f=v_w_a2_f, b_a_f=v_b_a_f, w_a2_b=v_w_a2_b,
             b_a_b=v_b_a_b, gla_norm_g=v_gla_norm_g, w_out=v_w_out, w_gate=v_w_gate, w_up=v_w_up, w_down=v_w_down,
             final_g=v_final_g)
    mx, my, mc = _me()
    jme = 2 * mx + my
    me = 4 * mx + 2 * my + mc
    wmc = D * 6 // NCHIP
    xx, tgt, cx = x[0], loss_target[0], ctx[0]

    bshard = [w[k][0].astype(BF) for k in BIG]
    sw = jnp.concatenate([jnp.pad(conv_w[0], ((0, 1), (0, 0))).reshape(-1), w_a2_f[0].reshape(-1), w_a2_b[0].reshape(-1)])
    small = jnp.concatenate([_rows(c[0]), jnp.pad(sw.reshape(6, D), ((0, 2), (0, 0)))], axis=0)
    cs8, mall, win_g = prologue(small, _rows(c_ctx), w_mod[0], lax.dynamic_slice_in_dim(b_mod, jme * wmc, wmc, axis=1),
                                bshard[0])
    cext = jnp.concatenate([cs8[:, 0, :], _rows(c_ctx)], axis=0)
    swc = jnp.stack([cs8[2 * j, 8:16] for j in range(NCHIP)]).reshape(NCHIP, 8 * D)
    convw = jnp.transpose(swc[:, :32 * 128].reshape(NCHIP, 32, 128), (1, 0, 2)).reshape(32, DC)
    a2 = lambda o: jnp.transpose(swc[:, o:o + RANK * 64].reshape(NCHIP, RANK, 64), (1, 0, 2)).reshape(RANK, DK)
    wa2 = jnp.zeros((128, 2 * DK), F32).at[0:RANK, 0:DK].set(a2(32 * 128)).at[RANK:2 * RANK, DK:].set(a2(32 * 128 + RANK * 64))
    wa2 = wa2.astype(BF)
    mall = jnp.concatenate([mall[2 * j] for j in range(NCHIP)], axis=1)
    sh1, sc1, g1, sh2, sc2, g2 = jnp.split(lax.dynamic_slice_in_dim(mall, me, 1, axis=0)[0], 6)
    csh1, csc1 = mall[8, :D], mall[8, D:2 * D]
    cols = lambda a: jnp.transpose(a, (1, 0, 2)).reshape(a.shape[1], -1)
    win = jnp.pad(cols(win_g), ((0, 0), (0, DINP - DIN)))
    ba = jnp.concatenate([b_a_f, b_a_b], axis=1)
    cvec = _rows(conv_b, conv_ln_g, conv_ln_b)
    vec1 = _rows(norm1_g, sh1, sc1)
    vecc = _rows(norm1_g, csh1, csc1)
    vecm = _rows(g1)
    vecf = _rows(norm2_g, sh2, sc2, g2, final_g)
    gn = jnp.tile(gla_norm_g, (1, NH))

    s0 = ctx_fwd(cx, vecc, win, wa2, ba)
    res = fwd_in(xx, vec1, win, convw, cvec, wa2, ba, TM_IN, ChipExchange("gather", bshard[1:]), bshard[1:])
    ag, yb, co, qk, vv, gg, la, r = res[:8]
    wout = res[8].reshape(D, D)
    wg, wu = cols(res[9]), cols(res[10])
    wd = res[11].reshape(DFF, D)
    o_f, o_b, se_f, se_b = gla_fwd(qk, vv, la, s0, TM_GLA)
    x1, y1, cat = merge_fwd(xx, o_f, o_b, gg, co, vecm, gn, wout, TM_MERGE)

    dx1, h2, act, dgt, dup, dy2, sf = ffn_fwd_bwd(x1, tgt, vecf, wg, wu, wd, TM_FFN)
    d_wg = wgrad(h2, dgt, None, D, DFF // 2, TN_WGRAD, "wgrad_gate")
    d_wu = wgrad(h2, dup, None, D, DFF // 2, TN_WGRAD, "wgrad_up")
    d_wd = wgrad(act, dy2, None, DFF // 2, D, TN_WGRAD, "wgrad_down")
    dy1, dco, do, dg, s1, s2 = merge_bwd(dx1, y1, o_f, o_b, gg, vecm, gn, wout, TM_MERGE)
    d_wout = wgrad(cat, dy1, None, D, D, TN_WGRAD, "wgrad_out")

    shard = lambda a, k: jnp.transpose(a.reshape(a.shape[0], NCHIP, k), (1, 0, 2))
    hd = D // 2
    parts = [sibling_add(d_wout, NCHIP, hd // NCHIP, hd // NCHIP, "xadd_w_out").reshape(NCHIP, hd // NCHIP, D),
             shard(sibling_add(d_wg, 1, hd, hd // 2, "xadd_w_gate"), DFF // NCHIP),
             shard(sibling_add(d_wu, 1, hd, hd // 2, "xadd_w_up"), DFF // NCHIP),
             sibling_add(d_wd, NCHIP, DFF // 8, DFF // 8, "xadd_w_down").reshape(NCHIP, DFF // 8, D)]
    parts = [p.astype(BF) for p in parts]
    res = gla_bwd(qk, vv, la, do, se_f, se_b, TM_GLA, ChipExchange("scatter", parts), parts)
    dqk_f, dv_f, dla_f, dqk_b, dv_b, dla_b, ds0 = res[:7]
    recv = list(res[7:])
    dwin_c, dwa2_c, sc = ctx_bwd(cx, vecc, win, wa2, ba, ds0)
    grad_x, h, dp, dwa2, dcw, sd = bwd_in(xx, dx1, ag, yb, dco, dqk_f, dqk_b, dv_f, dv_b, dg, dla_f, dla_b, la, r,
                                          vec1, win, convw, cvec, wa2, TM_IN)
    d_win = wgrad(h, dp, dwin_c, D, DINP // 3, TN_WGRAD, "wgrad_in")
    part_in = shard(sibling_add(d_win, 1, hd, hd // 2, "xadd_w_in")[:, :DIN], DIN // NCHIP).astype(BF)
    recv = list(chip_exchange("scatter", [part_in], "scatter_w_in")) + recv

    rows16, dcw_t, dwa2_t = pack_small(sf, s1, s2, sd, sc, dcw, dwa2, dwa2_c)
    sp = jnp.concatenate([rows16, dcw_t.reshape(16, D), dwa2_t.reshape(16, D)], axis=0)
    g8 = all_gather8(sp, "gather_small_grads")
    tot, bm_g, loss8 = small_totals(g8)
    loss = loss8[0, 0]
    dmod8 = g8[:, 0:6, :].reshape(NDEV, 6 * D)
    dmodc = jnp.concatenate([tot[6], tot[7], jnp.zeros((4 * D,), F32)])
    dm = jnp.concatenate([dmod8, _rows(dmodc)], axis=0)
    dm = lax.dynamic_slice_in_dim(dm, jme * wmc, wmc, axis=1)
    g_wmod, dsil = mod_bwd(cext, dm, w_mod[0])
    p8 = all_gather8(dsil[8:16], "gather_dsilu")
    g_cctx = cctx_grad(p8, _rows(c_ctx))[0]

    grads, delta, new_m, new_v = {}, {}, {}, {}
    for i, k in enumerate(BIG):
        r2 = recv[i].shape[1]
        gk = finish_grad(recv[i], r2 // 2 if r2 >= 512 else r2, "finish_" + k).reshape(w[k].shape[1:])
        rk = gk.shape[0]
        outs = adamw(w[k][0], gk, m[k][0], v[k][0], rk // 2 if rk >= 512 else rk, "adamw_" + k, emit_grad=True)
        delta[k], new_m[k], new_v[k], grads[k] = (o[None] for o in outs)
    grads['w_mod'] = g_wmod[None]
    d_, m_, v_ = adamw(w_mod[0], g_wmod, m_w_mod[0], v_w_mod[0], 256, "adamw_w_mod")
    delta['w_mod'], new_m['w_mod'], new_v['w_mod'] = d_[None], m_[None], v_[None]
    small_g = {
        'c_ctx': g_cctx, 'b_mod': bm_g[0:6].reshape(1, 6 * D), 'norm1_g': tot[8:9], 'norm2_g': tot[9:10], 'final_g': tot[10],
        'conv_b': tot[11:12, :DC], 'conv_ln_g': tot[11:12, DC:], 'conv_ln_b': tot[12:13, :DC],
        'b_a_f': tot[12:13, DC:DC + DK], 'b_a_b': tot[12:13, DC + DK:], 'gla_norm_g': tot[13:14, :HV],
        'conv_w': lax.dynamic_slice_in_dim(tot[16:32].reshape(32, DC)[:CW], jme * (DC // NCHIP), DC // NCHIP, axis=1)[None],
        'w_a2_f': lax.dynamic_slice_in_dim(tot[32:48].reshape(32, 2 * DK)[0:RANK, 0:DK], jme * (DK // NCHIP), DK // NCHIP, axis=1)[None],
        'w_a2_b': lax.dynamic_slice_in_dim(tot[32:48].reshape(32, 2 * DK)[RANK:2 * RANK, DK:], jme * (DK // NCHIP), DK // NCHIP, axis=1)[None],
    }
    grads.update(small_g)
    sd_, sm_, sv_ = adamw(_small_slab(w), _small_slab(small_g), _small_slab(m), _small_slab(v), 24,
                          "adamw_small")
    for dst, slab in ((delta, sd_), (new_m, sm_), (new_v, sv_)):
        dst.update(_unslab(slab))
    out = [loss, grad_x[None]]
    for group in (grads, delta, new_m, new_v):
        out += [group[k].reshape(w[k].shape) for k in WEIGHTS]
    return tuple(out)
```

```python
import jax
import jax.numpy as jnp
from jax import lax
from jax.experimental import pallas as pl
from jax.experimental.pallas import tpu as pltpu

F32 = jnp.float32
BF = jnp.bfloat16

D = 1024
DC = 512
NH = 4
HK = 64
HV = 128
DK = NH * HK
DV = NH * HV
RANK = 16
CH = 64
GW = 64
CW = 31
SEGP = GW + 32
DFF = 2816
DIN = 2592
DINP = 2688
EPS = 1e-6
TAU = 16.0
QSCALE = HK ** -0.5
NCHIP = 4
NDEV = 8

ADAM_LR = 0.001
ADAM_B1 = 0.9
ADAM_B2 = 0.999
ADAM_EPS = 1e-08
ADAM_WD = 0.01
ADAM_STEP = 10

VMEM_LIMIT = 56 * 1024 * 1024
MESH = pl.DeviceIdType.MESH


def _dot(a, b):
    return jnp.dot(a, b, preferred_element_type=F32)


def _dot_nt(a, b):
    return lax.dot_general(a, b, (((1,), (1,)), ((), ())), preferred_element_type=F32)


def _dot_tn(a, b):
    return lax.dot_general(a, b, (((0,), (0,)), ((), ())), preferred_element_type=F32)


def _mask_dot(t, x):
    hi = x.astype(BF)
    lo = (x - hi.astype(F32)).astype(BF)
    return _dot(t, hi) + _dot(t, lo)


def _sigmoid(x):
    return 1.0 / (1.0 + jnp.exp(-x))


def _log_sigmoid(x):
    return jnp.minimum(x, 0.0) - jnp.log(1.0 + jnp.exp(-jnp.abs(x)))


def _colsum8(z):
    t, c = z.shape
    return jnp.sum(z.reshape(t // 8, 8, c), axis=0)


def _tri(n, kind):
    r = lax.broadcasted_iota(jnp.int32, (n, n), 0)
    c = lax.broadcasted_iota(jnp.int32, (n, n), 1)
    m = {"le": c <= r, "lt": c < r, "ge": c >= r, "gt": c > r}[kind]
    return m


def _full(shape):
    nd = len(shape)
    return pl.BlockSpec(shape, lambda *_: (0,) * nd)


def _cparams(sem, vmem=VMEM_LIMIT):
    return pltpu.CompilerParams(dimension_semantics=sem, vmem_limit_bytes=vmem)


def _call(body, grid, name, in_specs, out_specs, out_shape, scratch, operands, exchange=None, carried=()):
    n_in, n_out, n_scr = len(in_specs), len(out_specs), len(scratch)
    if exchange is None:
        fn = body
    else:
        n = exchange.n

        def fn(*refs):
            ins, cin = refs[:n_in], refs[n_in:n_in + n]
            outs, cout = refs[n_in + n:n_in + n + n_out], refs[n_in + n + n_out:n_in + 2 * n + n_out]
            rest = refs[n_in + 2 * n + n_out:]
            scr, sems = rest[:n_scr], rest[n_scr:]

            @pl.when(pl.program_id(0) == 0)
            def _():
                exchange.start(cin, cout, sems)

            body(*ins, *outs, *scr)

            @pl.when(pl.program_id(0) == pl.num_programs(0) - 1)
            def _():
                exchange.wait(cin, cout, sems)

        any_spec = pl.BlockSpec(memory_space=pl.ANY)
        in_specs = list(in_specs) + [any_spec] * n
        out_specs = list(out_specs) + [any_spec] * n
        out_shape = list(out_shape) + exchange.out_shape
        scratch = list(scratch) + exchange.scratch
    return pl.pallas_call(fn, grid=grid, name=name, in_specs=in_specs, out_specs=out_specs, out_shape=out_shape,
                          scratch_shapes=scratch, compiler_params=_cparams(("arbitrary",)))(*operands, *carried)


def _fill_padded(pad_ref, val, nseg):
    zeros = jnp.zeros((nseg, 16, val.shape[-1]), F32)
    pad_ref[:, 0:16, :] = zeros
    pad_ref[:, 16 + GW:SEGP, :] = zeros
    pad_ref[:, 16:16 + GW, :] = val.reshape(nseg, GW, val.shape[-1])


def _tap_slabs(pad_ref, s, cs):
    whole = pad_ref[s, :, cs]
    for r in range(8):
        slab = whole if r == 0 else pltpu.roll(whole, SEGP - r, axis=0)
        for a in range(4):
            j = r + 8 * a - 1
            if 0 <= j < CW:
                yield j, slab[8 * a:8 * a + GW]


def _conv_taps(pad_ref, s, w_ref, c0, cw, flip):
    acc = jnp.zeros((GW, cw), F32)
    for j, rows in _tap_slabs(pad_ref, s, pl.ds(c0, cw)):
        acc = acc + w_ref[pl.ds((CW - 1 - j) if flip else j, 1), pl.ds(c0, cw)] * rows
    return acc


def _ln_stats(yb):
    mu = jnp.mean(yb, axis=-1, keepdims=True)
    yc = yb - mu
    var = jnp.mean(yc * yc, axis=-1, keepdims=True)
    rs = lax.rsqrt(var + EPS)
    return yc * rs, rs


def fwd_in(x, vec1, win, convw, cvec, wa2, ba, tm, exchange=None, carried=()):
    n = x.shape[0]
    nseg = tm // GW
    cg = 128

    def body(x_ref, vec_ref, win_ref, cw_ref, cv_ref, wa2_ref, ba_ref,
             ag_ref, yb_ref, co_ref, qk_ref, v_ref, g_ref, la_ref, r_ref, pad_ref):
        xx = x_ref[...]
        rstd = lax.rsqrt(jnp.mean(xx * xx, axis=-1, keepdims=True) + EPS)
        h = ((xx * rstd * vec_ref[0:1, :]) * (1.0 + vec_ref[2:3, :]) + vec_ref[1:2, :]).astype(BF)
        pc = _dot(h, win_ref[:, :2 * DC])
        ag_ref[...] = pc.astype(BF)
        _fill_padded(pad_ref, pc[:, :DC] * _sigmoid(pc[:, DC:]), nseg)
        for s in range(nseg):
            for c0 in range(0, DC, cg):
                y = _conv_taps(pad_ref, s, cw_ref, c0, cg, False)
                yb_ref[pl.ds(s * GW, GW), pl.ds(c0, cg)] = y + cv_ref[0:1, c0:c0 + cg]
        p = _dot(h, win_ref[:, 2 * DC:])
        qk_ref[...] = p[:, :2 * DK].astype(BF)
        v_ref[...] = p[:, 2 * DK:2 * DK + DV].astype(BF)
        g_ref[...] = p[:, 2 * DK + DV:2 * DK + 2 * DV].astype(BF)
        r = p[:, 2 * DK + 2 * DV:].astype(BF)
        r_ref[...] = r
        la_ref[...] = _log_sigmoid(_dot(r, wa2_ref[...]) + ba_ref[...]) * (1.0 / TAU)
        yn, _ = _ln_stats(yb_ref[...])
        ln = yn * cv_ref[1:2, :] + cv_ref[2:3, :]
        co_ref[...] = (ln * _sigmoid(ln)).astype(BF)

    tok = lambda w: pl.BlockSpec((tm, w), lambda i: (i, 0))
    return _call(
        body, (n // tm,), "fwd_in",
        [tok(D), _full(vec1.shape), _full(win.shape), _full(convw.shape), _full(cvec.shape), _full(wa2.shape), _full(ba.shape)],
        [tok(2 * DC), tok(DC), tok(DC), tok(2 * DK), tok(DV), tok(DV), tok(2 * DK), tok(128)],
        [jax.ShapeDtypeStruct((n, 2 * DC), BF), jax.ShapeDtypeStruct((n, DC), F32),
         jax.ShapeDtypeStruct((n, DC), BF), jax.ShapeDtypeStruct((n, 2 * DK), BF),
         jax.ShapeDtypeStruct((n, DV), BF), jax.ShapeDtypeStruct((n, DV), BF),
         jax.ShapeDtypeStruct((n, 2 * DK), F32), jax.ShapeDtypeStruct((n, 128), BF)],
        [pltpu.VMEM((nseg, SEGP, DC), F32)],
        (x, vec1, win, convw, cvec, wa2, ba), exchange, carried)


def _gla_dir(d):
    return (_tri(CH, "le"), CH - 1) if d == 0 else (_tri(CH, "ge"), 0)


def _gla_chunk_terms(qk, la, d):
    seen, last = _gla_dir(d)
    b = _mask_dot(seen.astype(BF), la)
    bl = b[last:last + 1, :]
    eb = jnp.exp(b)
    enb = jnp.exp(-b)
    ekd = jnp.exp(bl - b)
    ebl = jnp.exp(bl)
    q = qk[:, :DK].astype(F32) * QSCALE
    k = qk[:, DK:].astype(F32)
    return eb, enb, ekd, ebl, q * eb, k * enb, k * ekd


NP = NH // 2
PW = 2 * HK


def _lo_lanes(shape):
    return lax.broadcasted_iota(jnp.int32, shape, len(shape) - 1) < HK


def _pair_sel(lo, hi):
    return jnp.where(_lo_lanes(lo.shape), lo, hi)


def _only(x, which):
    keep = _lo_lanes(x.shape) if which == 0 else jnp.logical_not(_lo_lanes(x.shape))
    return jnp.where(keep, x, jnp.zeros_like(x))


def gla_fwd(qk, v, la, s0, tm):
    n = qk.shape[0]
    nt = n // tm
    nc = tm // CH

    def body(qkf_ref, vf_ref, laf_ref, qkb_ref, vb_ref, lab_ref, s0_ref, of_ref, ob_ref, sef_ref, seb_ref, st_ref):
        @pl.when(pl.program_id(0) == 0)
        def _():
            st_ref[...] = s0_ref[...]

        def chunk(ci, carry):
            t = []
            for d, (qk_ref, v_ref, la_ref) in enumerate(((qkf_ref, vf_ref, laf_ref), (qkb_ref, vb_ref, lab_ref))):
                c = ci if d == 0 else nc - 1 - ci
                rows = pl.ds(pl.multiple_of(c * CH, CH), CH)
                eb, enb, ekd, ebl, qt, kt, kd = _gla_chunk_terms(qk_ref[rows, :], la_ref[rows, :], d)
                t.append(dict(c=c, rows=rows, ebl=ebl, qt=qt.astype(BF), kt=kt.astype(BF), kd=kd.astype(BF),
                              vv=v_ref[rows, :], st=[st_ref[d, p] for p in range(NP)], amask=_gla_dir(d)[0]))
            dh = [(d, h) for d in range(2) for h in range(NH)]
            dp = [(d, p) for d in range(2) for p in range(NP)]
            ps = lambda h: slice((h // 2) * PW, (h // 2 + 1) * PW)
            vs = lambda h: slice(h * HV, (h + 1) * HV)
            v2 = lambda p: slice(2 * p * HV, 2 * (p + 1) * HV)
            qm = {(d, h): _only(t[d]['qt'][:, ps(h)], h % 2) for d, h in dh}
            q2 = {(d, p): jnp.concatenate([qm[d, 2 * p], qm[d, 2 * p + 1]], axis=0) for d, p in dp}
            a2 = {(d, p): _dot_nt(q2[d, p], t[d]['kt'][:, p * PW:(p + 1) * PW]) for d, p in dp}
            a = {(d, h): jnp.where(t[d]['amask'], a2[d, h // 2][(h % 2) * CH:(h % 2 + 1) * CH], 0.0).astype(BF) for d, h in dh}
            oi = {(d, p): _dot_nt(q2[d, p], t[d]['st'][p].astype(BF)) for d, p in dp}
            o = {(d, h): _dot(a[d, h], t[d]['vv'][:, vs(h)]) + oi[d, h // 2][(h % 2) * CH:(h % 2 + 1) * CH] for d, h in dh}
            kv = {(d, p): _dot_tn(t[d]['vv'][:, v2(p)], t[d]['kd'][:, p * PW:(p + 1) * PW]) for d, p in dp}
            for d, (o_ref, se_ref) in enumerate(((of_ref, sef_ref), (ob_ref, seb_ref))):
                for h in range(NH):
                    o_ref[t[d]['rows'], vs(h)] = o[d, h].astype(BF)
                for p in range(NP):
                    se_ref[t[d]['c'], p] = t[d]['st'][p]
                    st_ref[d, p] = (t[d]['ebl'][:, p * PW:(p + 1) * PW] * t[d]['st'][p]
                                    + _pair_sel(kv[d, p][:HV], kv[d, p][HV:]))
            return carry

        lax.fori_loop(0, nc, chunk, 0, unroll=4)

    fw = lambda w, col=0: pl.BlockSpec((tm, w), lambda i: (i, col))
    bw = lambda w, col=0: pl.BlockSpec((tm, w), lambda i: (nt - 1 - i, col))
    se_f = pl.BlockSpec((nc, NP, HV, PW), lambda i: (i, 0, 0, 0))
    se_b = pl.BlockSpec((nc, NP, HV, PW), lambda i: (nt - 1 - i, 0, 0, 0))
    se_shape = jax.ShapeDtypeStruct((n // CH, NP, HV, PW), F32)
    return pl.pallas_call(
        body, grid=(nt,), name="gla_fwd",
        in_specs=[fw(2 * DK), fw(DV), fw(DK, 0), bw(2 * DK), bw(DV), bw(DK, 1), _full(s0.shape)],
        out_specs=[fw(DV), bw(DV), se_f, se_b],
        out_shape=[jax.ShapeDtypeStruct((n, DV), BF), jax.ShapeDtypeStruct((n, DV), BF), se_shape, se_shape],
        scratch_shapes=[pltpu.VMEM((2, NP, HV, PW), F32)],
        compiler_params=_cparams(("arbitrary",)),
    )(qk, v, la, qk, v, la, s0)


def gla_bwd(qk, v, la, do, se_f, se_b, tm, exchange=None, carried=()):
    n = qk.shape[0]
    nt = n // tm
    nc = tm // CH

    def body(qkf_ref, vf_ref, laf_ref, dof_ref, sef_ref, qkb_ref, vb_ref, lab_ref, dob_ref, seb_ref,
             dqkf_ref, dvf_ref, dlaf_ref, dqkb_ref, dvb_ref, dlab_ref, ds0_ref, ds_ref):
        @pl.when(pl.program_id(0) == 0)
        def _():
            ds_ref[...] = jnp.zeros_like(ds_ref)

        def chunk(ci, carry):
            t = []
            for d, (qk_ref, v_ref, la_ref, do_ref, se_ref) in enumerate(
                    ((qkf_ref, vf_ref, laf_ref, dof_ref, sef_ref), (qkb_ref, vb_ref, lab_ref, dob_ref, seb_ref))):
                c = nc - 1 - ci if d == 0 else ci
                rows = pl.ds(pl.multiple_of(c * CH, CH), CH)
                amask, last = _gla_dir(d)
                eb, enb, ekd, ebl, qt, kt, kd = _gla_chunk_terms(qk_ref[rows, :], la_ref[rows, :], d)
                t.append(dict(rows=rows, amask=amask, last=last, eb=eb, enb=enb, ekd=ekd, ebl=ebl, qt=qt, kt=kt, kd=kd,
                              qtb=qt.astype(BF), ktb=kt.astype(BF), kdb=kd.astype(BF), vv=v_ref[rows, :], dd=do_ref[rows, :],
                              st=[se_ref[c, p] for p in range(NP)], dsn=[ds_ref[d, p] for p in range(NP)]))
            dh = [(d, h) for d in range(2) for h in range(NH)]
            dp = [(d, p) for d in range(2) for p in range(NP)]
            ps = lambda h: slice((h // 2) * PW, (h // 2 + 1) * PW)
            vs = lambda h: slice(h * HV, (h + 1) * HV)
            stb = {(d, p): t[d]['st'][p].astype(BF) for d, p in dp}
            dsnb = {(d, p): t[d]['dsn'][p].astype(BF) for d, p in dp}
            qm = {(d, h): _only(t[d]['qtb'][:, ps(h)], h % 2) for d, h in dh}
            km = {(d, h): _only(t[d]['kdb'][:, ps(h)], h % 2) for d, h in dh}
            a2 = {(d, p): _dot_nt(jnp.concatenate([qm[d, 2 * p], qm[d, 2 * p + 1]], axis=0), t[d]['ktb'][:, p * PW:(p + 1) * PW])
                  for d, p in dp}
            a = {(d, h): jnp.where(t[d]['amask'], a2[d, h // 2][(h % 2) * CH:(h % 2 + 1) * CH], 0.0).astype(BF) for d, h in dh}
            da = {(d, h): jnp.where(t[d]['amask'], _dot_nt(t[d]['dd'][:, vs(h)], t[d]['vv'][:, vs(h)]), 0.0).astype(BF)
                  for d, h in dh}
            v2 = lambda p: slice(2 * p * HV, 2 * (p + 1) * HV)
            rows2 = lambda x, d, p: jnp.concatenate([x[d, 2 * p], x[d, 2 * p + 1]], axis=0)
            half = lambda x, h: x[(h % 2) * CH:(h % 2 + 1) * CH]
            dvs = {(d, p): _dot_nt(rows2(km, d, p), dsnb[d, p]) for d, p in dp}
            dv = {(d, h): _dot_tn(a[d, h], t[d]['dd'][:, vs(h)]) + half(dvs[d, h // 2], h) for d, h in dh}
            vrows = lambda d, p: jnp.concatenate([t[d]['vv'][:, vs(2 * p)], t[d]['vv'][:, vs(2 * p + 1)]], axis=0)
            drows = lambda d, p: jnp.concatenate([t[d]['dd'][:, vs(2 * p)], t[d]['dd'][:, vs(2 * p + 1)]], axis=0)
            dkd2 = {(d, p): _dot(vrows(d, p), dsnb[d, p]) for d, p in dp}
            dqt2 = {(d, p): _dot(rows2(da, d, p), t[d]['ktb'][:, p * PW:(p + 1) * PW]) + _dot(drows(d, p), stb[d, p])
                    for d, p in dp}
            dkt2 = {(d, p): _dot_tn(jnp.concatenate([da[d, 2 * p], da[d, 2 * p + 1]], axis=1), t[d]['qtb'][:, p * PW:(p + 1) * PW])
                    for d, p in dp}
            dsq2 = {(d, p): _dot_tn(t[d]['dd'][:, v2(p)], t[d]['qtb'][:, p * PW:(p + 1) * PW]) for d, p in dp}
            two = lambda x, d, p, n: _pair_sel(x[d, p][:n], x[d, p][n:])
            for d, (dqk_ref, dv_ref, dla_ref) in enumerate(((dqkf_ref, dvf_ref, dlaf_ref), (dqkb_ref, dvb_ref, dlab_ref))):
                td = t[d]
                rows = td['rows']
                for h in range(NH):
                    dv_ref[rows, vs(h)] = dv[d, h].astype(BF)
                pair = lambda x: jnp.concatenate([two(x, d, p, CH) for p in range(NP)], axis=1)
                dqt_, dkt_, dkd_ = pair(dqt2), pair(dkt2), pair(dkd2)
                debl = jnp.concatenate([jnp.sum(td['st'][p] * td['dsn'][p], axis=0, keepdims=True) for p in range(NP)], axis=1)
                for p in range(NP):
                    ds_ref[d, p] = two(dsq2, d, p, HV) + td['ebl'][:, p * PW:(p + 1) * PW] * td['dsn'][p]
                dkdkd = dkd_ * td['kd']
                dbl = jnp.sum(dkdkd, axis=0, keepdims=True) + debl * td['ebl']
                is_last = lax.broadcasted_iota(jnp.int32, (CH, DK), 0) == td['last']
                db = dqt_ * td['qt'] - dkt_ * td['kt'] - dkdkd + jnp.where(is_last, dbl, 0.0)
                dqk_ref[rows, :] = jnp.concatenate([dqt_ * td['eb'] * QSCALE, dkt_ * td['enb'] + dkd_ * td['ekd']], axis=1).astype(BF)
                dla_ref[rows, :] = _mask_dot(_gla_dir(1 - d)[0].astype(BF), db)
            return carry

        lax.fori_loop(0, nc, chunk, 0, unroll=4)

        @pl.when(pl.program_id(0) == nt - 1)
        def _():
            ds0_ref[...] = ds_ref[...]

    up = lambda w, col=0: pl.BlockSpec((tm, w), lambda i: (i, col))
    dn = lambda w, col=0: pl.BlockSpec((tm, w), lambda i: (nt - 1 - i, col))
    se_up = pl.BlockSpec((nc, NP, HV, PW), lambda i: (i, 0, 0, 0))
    se_dn = pl.BlockSpec((nc, NP, HV, PW), lambda i: (nt - 1 - i, 0, 0, 0))
    return _call(
        body, (nt,), "gla_bwd",
        [dn(2 * DK), dn(DV), dn(DK, 0), dn(DV), se_dn, up(2 * DK), up(DV), up(DK, 1), up(DV), se_up],
        [dn(2 * DK), dn(DV), dn(DK), up(2 * DK), up(DV), up(DK), _full((2, NP, HV, PW))],
        [jax.ShapeDtypeStruct((n, 2 * DK), BF), jax.ShapeDtypeStruct((n, DV), BF),
         jax.ShapeDtypeStruct((n, DK), F32), jax.ShapeDtypeStruct((n, 2 * DK), BF),
         jax.ShapeDtypeStruct((n, DV), BF), jax.ShapeDtypeStruct((n, DK), F32),
         jax.ShapeDtypeStruct((2, NP, HV, PW), F32)],
        [pltpu.VMEM((2, NP, HV, PW), F32)],
        (qk, v, la, do, se_f, qk, v, la, do, se_b), exchange, carried)


def _head_norm(o):
    ons, rss = [], []
    for h in range(NH):
        oh = o[:, h * HV:(h + 1) * HV]
        rs = lax.rsqrt(jnp.mean(oh * oh, axis=-1, keepdims=True) + EPS)
        ons.append(oh * rs)
        rss.append(rs)
    return ons, rss


def merge_fwd(x, o_f, o_b, g, co, vecm, gn, wout, tm):
    n = x.shape[0]

    def body(x_ref, of_ref, ob_ref, g_ref, co_ref, vec_ref, gn_ref, w_ref, x1_ref, cat_ref):
        o = of_ref[...].astype(F32) + ob_ref[...].astype(F32)
        ons, _ = _head_norm(o)
        gg = g_ref[...].astype(F32)
        sil = gg * _sigmoid(gg)
        cat_ref[:, :DC] = co_ref[...]
        for h in range(NH):
            vs = slice(h * HV, (h + 1) * HV)
            cat_ref[:, DC + h * HV:DC + (h + 1) * HV] = (ons[h] * gn_ref[:, vs] * sil[:, vs]).astype(BF)
        x1_ref[...] = x_ref[...] + vec_ref[0:1, :] * _dot(cat_ref[...], w_ref[...])

    tok = lambda w: pl.BlockSpec((tm, w), lambda i: (i, 0))
    return pl.pallas_call(
        body, grid=(n // tm,), name="merge_fwd",
        in_specs=[tok(D), tok(DV), tok(DV), tok(DV), tok(DC), _full(vecm.shape), _full(gn.shape), _full(wout.shape)],
        out_specs=[tok(D), tok(D)],
        out_shape=[jax.ShapeDtypeStruct((n, D), F32), jax.ShapeDtypeStruct((n, D), BF)],
        compiler_params=_cparams(("arbitrary",)),
    )(x, o_f, o_b, g, co, vecm, gn, wout)


def merge_bwd(dx1, cat, o_f, o_b, g, vecm, gn, wout, tm):
    n = dx1.shape[0]

    def body(dx1_ref, cat_ref, of_ref, ob_ref, g_ref, vec_ref, gn_ref, w_ref,
             dy1_ref, dco_ref, do_ref, dg_ref, s1_ref, s2_ref):
        @pl.when(pl.program_id(0) == 0)
        def _():
            s1_ref[...] = jnp.zeros_like(s1_ref)
            s2_ref[...] = jnp.zeros_like(s2_ref)

        dx1 = dx1_ref[...]
        s1_ref[...] += _colsum8(dx1 * _dot(cat_ref[...], w_ref[...]))
        dy1 = (dx1 * vec_ref[0:1, :]).astype(BF)
        dy1_ref[...] = dy1
        dcat = _dot_nt(dy1, w_ref[...])
        dco_ref[...] = dcat[:, :DC].astype(BF)
        o = of_ref[...].astype(F32) + ob_ref[...].astype(F32)
        ons, rss = _head_norm(o)
        gg = g_ref[...].astype(F32)
        sg = _sigmoid(gg)
        sil = gg * sg
        dsil = sg * (1.0 + gg * (1.0 - sg))
        for h in range(NH):
            vs = slice(h * HV, (h + 1) * HV)
            do2 = dcat[:, DC + h * HV:DC + (h + 1) * HV]
            gnh = gn_ref[:, vs]
            t = do2 * sil[:, vs]
            s2_ref[:, vs] += _colsum8(t * ons[h])
            don = t * gnh
            do_ref[:, vs] = (rss[h] * (don - ons[h] * jnp.mean(don * ons[h], axis=-1, keepdims=True))).astype(BF)
            dg_ref[:, vs] = (do2 * ons[h] * gnh * dsil[:, vs]).astype(BF)

    tok = lambda w: pl.BlockSpec((tm, w), lambda i: (i, 0))
    return pl.pallas_call(
        body, grid=(n // tm,), name="merge_bwd",
        in_specs=[tok(D), tok(D), tok(DV), tok(DV), tok(DV), _full(vecm.shape), _full(gn.shape), _full(wout.shape)],
        out_specs=[tok(D), tok(DC), tok(DV), tok(DV), _full((8, D)), _full((8, DV))],
        out_shape=[jax.ShapeDtypeStruct((n, D), BF), jax.ShapeDtypeStruct((n, DC), BF), jax.ShapeDtypeStruct((n, DV), BF),
                   jax.ShapeDtypeStruct((n, DV), BF), jax.ShapeDtypeStruct((8, D), F32), jax.ShapeDtypeStruct((8, DV), F32)],
        compiler_params=_cparams(("arbitrary",)),
    )(dx1, cat, o_f, o_b, g, vecm, gn, wout)


def ffn_fwd_bwd(x1, tgt, vecf, wg, wu, wd, tm):
    n = x1.shape[0]

    def body(x1_ref, t_ref, vec_ref, wg_ref, wu_ref, wd_ref,
             dx1_ref, h2_ref, act_ref, dgt_ref, dup_ref, dy2_ref, s_ref):
        @pl.when(pl.program_id(0) == 0)
        def _():
            s_ref[...] = jnp.zeros_like(s_ref)

        n2g, sh2, sc2, g2, fg = (vec_ref[i:i + 1, :] for i in range(5))
        x1 = x1_ref[...]
        r2 = lax.rsqrt(jnp.mean(x1 * x1, axis=-1, keepdims=True) + EPS)
        xn2 = x1 * r2
        h2 = (xn2 * n2g * (1.0 + sc2) + sh2).astype(BF)
        h2_ref[...] = h2
        gt = _dot(h2, wg_ref[...])
        up = _dot(h2, wu_ref[...])
        sg = _sigmoid(gt)
        sil = gt * sg
        act = (sil * up).astype(BF)
        act_ref[...] = act
        y2 = _dot(act, wd_ref[...])
        x2 = x1 + g2 * y2
        r3 = lax.rsqrt(jnp.mean(x2 * x2, axis=-1, keepdims=True) + EPS)
        xn3 = x2 * r3
        e = xn3 * fg - t_ref[...]
        s_ref[40:48, :] += _colsum8(e * e) * (0.5 / D)
        dyo = e * (1.0 / D)
        s_ref[0:8, :] += _colsum8(dyo * xn3)
        dxn3 = dyo * fg
        dx2 = r3 * (dxn3 - xn3 * jnp.mean(dxn3 * xn3, axis=-1, keepdims=True))
        s_ref[8:16, :] += _colsum8(dx2 * y2)
        dy2 = (dx2 * g2).astype(BF)
        dy2_ref[...] = dy2
        dact = _dot_nt(dy2, wd_ref[...])
        dup = (dact * sil).astype(BF)
        dgt = (dact * up * (sg * (1.0 + gt * (1.0 - sg)))).astype(BF)
        dup_ref[...] = dup
        dgt_ref[...] = dgt
        dh2 = _dot_nt(dgt, wg_ref[...]) + _dot_nt(dup, wu_ref[...])
        s_ref[16:24, :] += _colsum8(dh2)
        t = dh2 * xn2
        s_ref[24:32, :] += _colsum8(t * n2g)
        s_ref[32:40, :] += _colsum8(t * (1.0 + sc2))
        dxn2 = dh2 * ((1.0 + sc2) * n2g)
        dx1_ref[...] = dx2 + r2 * (dxn2 - xn2 * jnp.mean(dxn2 * xn2, axis=-1, keepdims=True))

    tok = lambda w: pl.BlockSpec((tm, w), lambda i: (i, 0))
    wspec = lambda a: pl.BlockSpec(a.shape, lambda i: (0, 0), pipeline_mode=pl.Buffered(1))
    return pl.pallas_call(
        body, grid=(n // tm,), name="ffn_fwd_bwd",
        in_specs=[tok(D), tok(D), _full(vecf.shape), wspec(wg), wspec(wu), wspec(wd)],
        out_specs=[tok(D), tok(D), tok(DFF), tok(DFF), tok(DFF), tok(D), _full((48, D))],
        out_shape=[jax.ShapeDtypeStruct((n, D), F32), jax.ShapeDtypeStruct((n, D), BF), jax.ShapeDtypeStruct((n, DFF), BF),
                   jax.ShapeDtypeStruct((n, DFF), BF), jax.ShapeDtypeStruct((n, DFF), BF), jax.ShapeDtypeStruct((n, D), BF),
                   jax.ShapeDtypeStruct((48, D), F32)],
        compiler_params=_cparams(("arbitrary",)),
    )(x1, tgt, vecf, wg, wu, wd)


def wgrad(a, b, init, t1, t2, tn, name):
    n, k1 = a.shape
    k2 = b.shape[1]

    def body(a_ref, b_ref, *rest):
        o_ref = rest[-1]

        @pl.when(pl.program_id(2) == 0)
        def _():
            o_ref[...] = rest[0][...] if init is not None else jnp.zeros_like(o_ref)

        o_ref[...] += _dot_tn(a_ref[...], b_ref[...])

    ospec = pl.BlockSpec((t1, t2), lambda i, j, k: (i, j))
    extra = ([ospec], {2: 0}, (init,)) if init is not None else ([], {}, ())
    return pl.pallas_call(
        body, grid=(k1 // t1, k2 // t2, n // tn), name=name,
        in_specs=[pl.BlockSpec((tn, t1), lambda i, j, k: (k, i)), pl.BlockSpec((tn, t2), lambda i, j, k: (k, j))] + extra[0],
        out_specs=ospec, out_shape=jax.ShapeDtypeStruct((k1, k2), F32), input_output_aliases=extra[1],
        compiler_params=_cparams(("parallel", "parallel", "arbitrary")),
    )(a, b, *extra[2])


def bwd_in(x, dx1, ag, yb, dco, dqk_f, dqk_b, dv_f, dv_b, dg, dla_f, dla_b, la, r, vec1, win, convw, cvec, wa2, tm):
    n = x.shape[0]
    nseg = tm // GW
    cg = 128

    def body(x_ref, dx1_ref, ag_ref, yb_ref, dco_ref, dqkf_ref, dqkb_ref, dvf_ref, dvb_ref, dg_ref, dlaf_ref, dlab_ref,
             la_ref, r_ref, vec_ref, win_ref, cw_ref, cv_ref, wa2_ref,
             gx_ref, h_ref, dp_ref, dwa2_ref, dcw_ref, s_ref, vc_ref, pad2_ref, dvc_ref, dcw8_ref):
        first = pl.program_id(0) == 0

        @pl.when(first)
        def _():
            s_ref[...] = jnp.zeros_like(s_ref)
            dwa2_ref[...] = jnp.zeros_like(dwa2_ref)
            dcw8_ref[...] = jnp.zeros_like(dcw8_ref)

        yn, rs = _ln_stats(yb_ref[...])
        lng = cv_ref[1:2, :]
        ln = yn * lng + cv_ref[2:3, :]
        sgl = _sigmoid(ln)
        dln = dco_ref[...].astype(F32) * (sgl * (1.0 + ln * (1.0 - sgl)))
        dyn = dln * lng
        dyb = rs * (dyn - jnp.mean(dyn, axis=-1, keepdims=True) - yn * jnp.mean(dyn * yn, axis=-1, keepdims=True))
        s_ref[24:32, 0:DC] += _colsum8(dyb)
        s_ref[24:32, DC:D] += _colsum8(dln * yn)
        s_ref[32:40, 0:DC] += _colsum8(dln)

        agv = ag_ref[...].astype(F32)
        a = agv[:, :DC]
        sgg = _sigmoid(agv[:, DC:])
        vc_ref[...] = a * sgg
        _fill_padded(pad2_ref, dyb, nseg)

        dp_ref[:, 2 * DC:2 * DC + 2 * DK] = (dqkf_ref[...].astype(F32) + dqkb_ref[...].astype(F32)).astype(BF)
        dp_ref[:, 2 * DC + 2 * DK:2 * DC + 2 * DK + DV] = (dvf_ref[...].astype(F32) + dvb_ref[...].astype(F32)).astype(BF)
        dp_ref[:, 2 * DC + 2 * DK + DV:2 * DC + 2 * DK + 2 * DV] = dg_ref[...]

        la = la_ref[...]
        dla = jnp.concatenate([dlaf_ref[...], dlab_ref[...]], axis=1)
        dpre = dla * (1.0 - jnp.exp(TAU * la)) * (1.0 / TAU)
        s_ref[32:40, DC:D] += _colsum8(dpre)
        dpreb = dpre.astype(BF)
        dwa2_ref[...] += _dot_tn(r_ref[...], dpreb)
        dp_ref[:, DINP - 128:] = _dot_nt(dpreb, wa2_ref[...]).astype(BF)
        dh_rest = _dot_nt(dp_ref[:, 2 * DC:], win_ref[:, 2 * DC:])

        for s in range(nseg):
            rows = pl.ds(s * GW, GW)
            for c0 in range(0, DC, cg):
                cs = pl.ds(c0, cg)
                vcs = vc_ref[rows, cs]
                acc = jnp.zeros((GW, cg), F32)
                for j, rows_j in _tap_slabs(pad2_ref, s, cs):
                    acc = acc + cw_ref[pl.ds(CW - 1 - j, 1), cs] * rows_j
                    dcw8_ref[CW - 1 - j, :, cs] += _colsum8(vcs * rows_j)
                dvc_ref[rows, cs] = acc
        dvc = dvc_ref[...]
        dp_ref[:, 0:DC] = (dvc * sgg).astype(BF)
        dp_ref[:, DC:2 * DC] = (dvc * a * sgg * (1.0 - sgg)).astype(BF)

        dh = dh_rest + _dot_nt(dp_ref[:, :2 * DC], win_ref[:, :2 * DC])
        xx = x_ref[...]
        n1g, sh1, sc1 = vec_ref[0:1, :], vec_ref[1:2, :], vec_ref[2:3, :]
        rstd = lax.rsqrt(jnp.mean(xx * xx, axis=-1, keepdims=True) + EPS)
        xn = xx * rstd
        h_ref[...] = (xn * n1g * (1.0 + sc1) + sh1).astype(BF)
        s_ref[0:8, :] += _colsum8(dh)
        t = dh * xn
        s_ref[8:16, :] += _colsum8(t * n1g)
        s_ref[16:24, :] += _colsum8(t * (1.0 + sc1))
        dxn = dh * ((1.0 + sc1) * n1g)
        gx_ref[...] = dx1_ref[...] + rstd * (dxn - xn * jnp.mean(dxn * xn, axis=-1, keepdims=True))

        @pl.when(pl.program_id(0) == pl.num_programs(0) - 1)
        def _():
            dcw_ref[...] = jnp.sum(dcw8_ref[...], axis=1)

    tok = lambda w: pl.BlockSpec((tm, w), lambda i: (i, 0))
    return pl.pallas_call(
        body, grid=(n // tm,), name="bwd_in",
        in_specs=[tok(D), tok(D), tok(2 * DC), tok(DC), tok(DC), tok(2 * DK), tok(2 * DK), tok(DV), tok(DV), tok(DV),
                  tok(DK), tok(DK), tok(2 * DK), tok(128), _full(vec1.shape),
                  pl.BlockSpec(win.shape, lambda i: (0, 0), pipeline_mode=pl.Buffered(1)),
                  _full(convw.shape), _full(cvec.shape), _full(wa2.shape)],
        out_specs=[tok(D), tok(D), tok(DINP), _full((128, 2 * DK)), _full((32, DC)), _full((40, D))],
        out_shape=[jax.ShapeDtypeStruct((n, D), F32), jax.ShapeDtypeStruct((n, D), BF), jax.ShapeDtypeStruct((n, DINP), BF),
                   jax.ShapeDtypeStruct((128, 2 * DK), F32), jax.ShapeDtypeStruct((32, DC), F32),
                   jax.ShapeDtypeStruct((40, D), F32)],
        scratch_shapes=[pltpu.VMEM((tm, DC), F32), pltpu.VMEM((nseg, SEGP, DC), F32), pltpu.VMEM((tm, DC), F32),
                        pltpu.VMEM((32, 8, DC), F32)],
        compiler_params=_cparams(("arbitrary",)),
    )(x, dx1, ag, yb, dco, dqk_f, dqk_b, dv_f, dv_b, dg, dla_f, dla_b, la, r, vec1, win, convw, cvec, wa2)


def _ctx_common(ctx_ref, vec_ref, win_ref, wa2_ref, ba_ref):
    cx = ctx_ref[...]
    t = cx.shape[0]
    rstd = lax.rsqrt(jnp.mean(cx * cx, axis=-1, keepdims=True) + EPS)
    xn = cx * rstd
    hc = (xn * vec_ref[0:1, :] * (1.0 + vec_ref[2:3, :]) + vec_ref[1:2, :]).astype(BF)
    k0 = 2 * DC + DK
    kv = _dot(hc, win_ref[:, k0:k0 + DK + DV]).astype(BF).astype(F32)
    r = _dot(hc, win_ref[:, DINP - 128:]).astype(BF)
    la = _log_sigmoid(_dot(r, wa2_ref[...]) + ba_ref[...]) * (1.0 / TAU)
    incl = _tri(t, "le").astype(BF)
    strict = _tri(t, "lt").astype(BF)
    bf = _mask_dot(incl, la[:, :DK])
    wf = jnp.exp(bf[t - 1:t, :] - bf)
    wb = jnp.exp(_mask_dot(strict, la[:, DK:]))
    return xn, hc, kv[:, :DK], kv[:, DK:], r, la, wf, wb


def ctx_fwd(ctx, vecc, win, wa2, ba):
    def body(ctx_ref, vec_ref, win_ref, wa2_ref, ba_ref, s_ref):
        _, _, k, v, _, _, wf, wb = _ctx_common(ctx_ref, vec_ref, win_ref, wa2_ref, ba_ref)
        vb = v.astype(BF)
        for d, w in enumerate((wf, wb)):
            kd = (k * w).astype(BF)
            for h in range(NH):
                s_ref[d, h // 2, :, (h % 2) * HK:(h % 2 + 1) * HK] = _dot_tn(vb[:, h * HV:(h + 1) * HV], kd[:, h * HK:(h + 1) * HK])

    return pl.pallas_call(
        body, name="ctx_fwd", out_shape=jax.ShapeDtypeStruct((2, NP, HV, PW), F32),
        compiler_params=pltpu.CompilerParams(vmem_limit_bytes=VMEM_LIMIT),
    )(ctx, vecc, win, wa2, ba)


def ctx_bwd(ctx, vecc, win, wa2, ba, ds0):
    t = ctx.shape[0]

    def body(ctx_ref, vec_ref, win_ref, wa2_ref, ba_ref, ds_ref, dwin_ref, dwa2_ref, s_ref, dpc_ref):
        xn, hc, k, v, r, la, wf, wb = _ctx_common(ctx_ref, vec_ref, win_ref, wa2_ref, ba_ref)
        vb = v.astype(BF)
        strict = _tri(t, "lt").astype(BF)
        strict_t = _tri(t, "gt").astype(BF)
        dpc_ref[...] = jnp.zeros_like(dpc_ref)
        k0 = 2 * DC + DK
        dk = jnp.zeros((t, DK), F32)
        des = []
        for d, w in enumerate((wf, wb)):
            kd = (k * w).astype(BF)
            dkds = []
            for h in range(NH):
                dsb = ds_ref[d, h // 2, :, (h % 2) * HK:(h % 2 + 1) * HK].astype(BF)
                dkds.append(_dot(vb[:, h * HV:(h + 1) * HV], dsb))
                dvh = _dot_nt(kd[:, h * HK:(h + 1) * HK], dsb)
                vs = slice(k0 + DK + h * HV, k0 + DK + (h + 1) * HV)
                if d == 0:
                    dpc_ref[:, vs] = dvh.astype(BF)
                else:
                    dpc_ref[:, vs] = (dpc_ref[:, vs].astype(F32) + dvh).astype(BF)
            dkd = jnp.concatenate(dkds, axis=1)
            dk = dk + dkd * w
            des.append(dkd * k * w)
        dpc_ref[:, k0:k0 + DK] = dk.astype(BF)
        dla = jnp.concatenate([_mask_dot(strict, des[0]), _mask_dot(strict_t, des[1])], axis=1)
        dpre = dla * (1.0 - jnp.exp(TAU * la)) * (1.0 / TAU)
        dpreb = dpre.astype(BF)
        dwa2_ref[...] = _dot_tn(r, dpreb)
        dpc_ref[:, DINP - 128:] = _dot_nt(dpreb, wa2_ref[...]).astype(BF)
        dpc = dpc_ref[...]
        dwin_ref[...] = _dot_tn(hc, dpc)
        dhc = _dot_nt(dpc, win_ref[...])
        n1g, sc1 = vec_ref[0:1, :], vec_ref[2:3, :]
        tt = dhc * xn
        s_ref[...] = jnp.zeros_like(s_ref)
        s_ref[0:1, :] = jnp.sum(tt * (1.0 + sc1), axis=0, keepdims=True)
        s_ref[1:2, :] = jnp.sum(dhc, axis=0, keepdims=True)
        s_ref[2:3, :] = jnp.sum(tt * n1g, axis=0, keepdims=True)
        s_ref[3:4, DC:D] = jnp.sum(dpre, axis=0, keepdims=True)

    return pl.pallas_call(
        body, name="ctx_bwd",
        out_shape=[jax.ShapeDtypeStruct((D, DINP), F32), jax.ShapeDtypeStruct((128, 2 * DK), F32),
                   jax.ShapeDtypeStruct((8, D), F32)],
        scratch_shapes=[pltpu.VMEM((t, DINP), BF)],
        compiler_params=pltpu.CompilerParams(vmem_limit_bytes=VMEM_LIMIT),
    )(ctx, vecc, win, wa2, ba, ds0)


def _silu(x):
    return x * _sigmoid(x)


def mod_bwd(cext, dm, wm):
    def body(c_ref, d_ref, w_ref, gw_ref, ds_ref):
        dmb = d_ref[...].astype(BF)
        gw_ref[...] = _dot_tn(_silu(c_ref[...]).astype(BF), dmb)
        ds_ref[...] = _dot_nt(dmb, w_ref[...].astype(BF))

    return pl.pallas_call(body, name="mod_bwd",
                          out_shape=[jax.ShapeDtypeStruct(wm.shape, F32), jax.ShapeDtypeStruct(cext.shape, F32)],
                          compiler_params=pltpu.CompilerParams(vmem_limit_bytes=VMEM_LIMIT))(cext, dm, wm)


def pack_small(sf, s1, s2, sd, sc, dcw, dwa2, dwa2_c):
    def body(sf_ref, s1_ref, s2_ref, sd_ref, sc_ref, dcw_ref, dwa2_ref, dwa2c_ref, o_ref, ocw_ref, owa_ref):
        rsum = lambda ref, i: jnp.sum(ref[8 * i:8 * i + 8, :], axis=0, keepdims=True)
        o_ref[...] = jnp.zeros_like(o_ref)
        o_ref[0:1, :] = rsum(sd_ref, 0)
        o_ref[1:2, :] = rsum(sd_ref, 1)
        o_ref[2:3, :] = rsum(s1_ref, 0)
        o_ref[3:4, :] = rsum(sf_ref, 2)
        o_ref[4:5, :] = rsum(sf_ref, 3)
        o_ref[5:6, :] = rsum(sf_ref, 1)
        o_ref[6:7, :] = sc_ref[1:2, :]
        o_ref[7:8, :] = sc_ref[2:3, :]
        o_ref[8:9, :] = rsum(sd_ref, 2) + sc_ref[0:1, :]
        o_ref[9:10, :] = rsum(sf_ref, 4)
        o_ref[10:11, :] = rsum(sf_ref, 0)
        o_ref[11:12, :] = rsum(sd_ref, 3)
        o_ref[12:13, :] = rsum(sd_ref, 4) + sc_ref[3:4, :]
        g = jnp.sum(s2_ref[...], axis=0, keepdims=True)
        o_ref[13:14, 0:HV] = g[:, 0:HV] + g[:, HV:2 * HV] + g[:, 2 * HV:3 * HV] + g[:, 3 * HV:4 * HV]
        o_ref[14:15, :] = rsum(sf_ref, 5)
        ocw_ref[...] = dcw_ref[...]
        owa_ref[...] = dwa2_ref[0:32, :] + dwa2c_ref[0:32, :]

    return pl.pallas_call(body, name="pack_small",
                          out_shape=[jax.ShapeDtypeStruct((16, D), F32), jax.ShapeDtypeStruct((32, DC), F32),
                                     jax.ShapeDtypeStruct((32, 2 * DK), F32)])(sf, s1, s2, sd, sc, dcw, dwa2, dwa2_c)


def small_totals(g8):
    r = g8.shape[1]

    def body(g_ref, t_ref, bm_ref, loss_ref):
        acc = g_ref[0]
        for i in range(1, NDEV):
            acc = acc + g_ref[i]
        t_ref[...] = acc
        bm_ref[...] = jnp.zeros_like(bm_ref)
        bm_ref[0:6, :] = acc[0:6, :]
        bm_ref[0:2, :] += acc[6:8, :]
        loss_ref[...] = jnp.broadcast_to(jnp.sum(acc[14:15, :], axis=1, keepdims=True), loss_ref.shape)

    return pl.pallas_call(body, name="small_totals",
                          out_shape=[jax.ShapeDtypeStruct((r, D), F32), jax.ShapeDtypeStruct((8, D), F32),
                                     jax.ShapeDtypeStruct((8, 128), F32)])(g8)


def cctx_grad(p8, c_ctx_row):
    def body(p_ref, c_ref, o_ref):
        acc = p_ref[0, 0:1, :]
        for j in range(1, NCHIP):
            acc = acc + p_ref[2 * j, 0:1, :]
        cc = c_ref[0:1, :]
        sg = _sigmoid(cc)
        o_ref[...] = jnp.zeros_like(o_ref)
        o_ref[0:1, :] = acc * (sg * (1.0 + cc * (1.0 - sg)))

    return pl.pallas_call(body, name="cctx_grad", out_shape=jax.ShapeDtypeStruct((8, D), F32))(p8, c_ctx_row)


def adamw(w, g, m, v, rows, name, emit_grad=False):
    r, c = w.shape

    def body(w_ref, g_ref, m_ref, v_ref, d_ref, nm_ref, nv_ref, *go_ref):
        gg = g_ref[...]
        nm = ADAM_B1 * m_ref[...] + (1.0 - ADAM_B1) * gg
        nv = ADAM_B2 * v_ref[...] + (1.0 - ADAM_B2) * (gg * gg)
        m_hat = nm / (1.0 - ADAM_B1 ** ADAM_STEP)
        v_hat = nv / (1.0 - ADAM_B2 ** ADAM_STEP)
        d_ref[...] = -ADAM_LR * (m_hat / (jnp.sqrt(v_hat) + ADAM_EPS) + ADAM_WD * w_ref[...])
        nm_ref[...] = nm
        nv_ref[...] = nv
        if emit_grad:
            go_ref[0][...] = gg

    spec = pl.BlockSpec((rows, c), lambda i: (i, 0))
    sds = jax.ShapeDtypeStruct((r, c), F32)
    nout = 4 if emit_grad else 3
    return pl.pallas_call(
        body, grid=(r // rows,), name=name, in_specs=[spec] * 4, out_specs=[spec] * nout, out_shape=[sds] * nout,
        compiler_params=_cparams(("parallel",)),
    )(w, g, m, v)


def _me():
    return lax.axis_index("x"), lax.axis_index("y"), lax.axis_index("c")


def _flip(v, bit):
    return 1 - v if bit else v


ANY = pl.BlockSpec(memory_space=pl.ANY)


def _gather8(x_ref, o_ref, ssem, rsem, lsem):
    mx, my, mc = _me()
    me = 4 * mx + 2 * my + mc
    local = pltpu.make_async_copy(x_ref, o_ref.at[me], lsem)
    local.start()
    peer = lambda k: (_flip(mx, k & 4), _flip(my, k & 2), _flip(mc, k & 1))
    sends = []
    for k in range(1, NDEV):
        cp = pltpu.make_async_remote_copy(src_ref=x_ref, dst_ref=o_ref.at[me], send_sem=ssem.at[k - 1],
                                          recv_sem=rsem.at[k - 1], device_id=peer(k), device_id_type=MESH)
        cp.start()
        sends.append(cp)
    for k in range(1, NDEV):
        px, py, pc = peer(k)
        pltpu.make_async_remote_copy(src_ref=x_ref, dst_ref=o_ref.at[4 * px + 2 * py + pc], send_sem=ssem.at[k - 1],
                                     recv_sem=rsem.at[k - 1], device_id=(px, py, pc), device_id_type=MESH).wait_recv()
    for cp in sends:
        cp.wait_send()
    local.wait()


def _gather8_sems():
    return [pltpu.SemaphoreType.DMA((NDEV - 1,)), pltpu.SemaphoreType.DMA((NDEV - 1,)), pltpu.SemaphoreType.DMA]


def all_gather8(x, name):
    vm = pl.BlockSpec(memory_space=pltpu.VMEM)
    return pl.pallas_call(_gather8_body(), name=name, in_specs=[vm], out_specs=vm,
                          out_shape=jax.ShapeDtypeStruct((NDEV,) + x.shape, x.dtype), scratch_shapes=_gather8_sems())(x)


def _gather8_body():
    def body(x_ref, o_ref, ssem, rsem, lsem):
        _gather8(x_ref, o_ref, ssem, rsem, lsem)
    return body


def prologue(small, c_ctx_rows, wm, bm, w_in_shard):
    ex = ChipExchange("gather", [w_in_shard])

    def body(s_ref, cc_ref, w_ref, b_ref, win_ref, s8_ref, m8_ref, wing_ref, mloc_ref, *sems):
        ex.start([win_ref], [wing_ref], sems[6:])
        _gather8(s_ref, s8_ref, *sems[0:3])
        cext = jnp.concatenate([s8_ref[:, 0, :], cc_ref[...]], axis=0)
        mloc_ref[...] = _dot(_silu(cext).astype(BF), w_ref[...].astype(BF)) + b_ref[...]
        _gather8(mloc_ref, m8_ref, *sems[3:6])
        ex.wait([win_ref], [wing_ref], sems[6:])

    vm = pl.BlockSpec(memory_space=pltpu.VMEM)
    wcols = wm.shape[1]
    return pl.pallas_call(
        body, name="prologue", in_specs=[vm, vm, vm, vm, ANY], out_specs=[vm, vm, ANY],
        out_shape=[jax.ShapeDtypeStruct((NDEV, 16, D), F32), jax.ShapeDtypeStruct((NDEV, 16, wcols), F32)] + ex.out_shape,
        scratch_shapes=[pltpu.VMEM((16, wcols), F32)] + _gather8_sems() + _gather8_sems() + ex.scratch,
        compiler_params=pltpu.CompilerParams(vmem_limit_bytes=VMEM_LIMIT),
    )(small, c_ctx_rows, wm, bm, w_in_shard)


def _chip_peers(mx, my):
    out = []
    for p in range(1, NCHIP):
        px, py = _flip(mx, p & 2), _flip(my, p & 1)
        out.append((px, py, 2 * px + py))
    return out


class ChipExchange:
    def __init__(self, kind, arrays):
        self.kind = kind
        self.n = len(arrays)
        if kind == "gather":
            self.out_shape = [jax.ShapeDtypeStruct((NCHIP,) + a.shape, a.dtype) for a in arrays]
        else:
            self.out_shape = [jax.ShapeDtypeStruct(a.shape, a.dtype) for a in arrays]
        self.scratch = [pltpu.SemaphoreType.DMA((3 * self.n,)), pltpu.SemaphoreType.DMA((3 * self.n,)),
                        pltpu.SemaphoreType.DMA((self.n,))]

    def _copies(self, ins, outs, sems):
        ssem, rsem, lsem = sems
        mx, my, mc = _me()
        jme = 2 * mx + my
        gather = self.kind == "gather"
        local, sends, waits = [], [], []
        for k in range(self.n):
            local.append(pltpu.make_async_copy(ins[k] if gather else ins[k].at[jme], outs[k].at[jme], lsem.at[k]))
            for p, (px, py, jp) in enumerate(_chip_peers(mx, my)):
                src = ins[k] if gather else ins[k].at[jp]
                sem = dict(send_sem=ssem.at[3 * k + p], recv_sem=rsem.at[3 * k + p], device_id=(px, py, mc),
                           device_id_type=MESH)
                sends.append(pltpu.make_async_remote_copy(src_ref=src, dst_ref=outs[k].at[jme], **sem))
                waits.append(pltpu.make_async_remote_copy(src_ref=src, dst_ref=outs[k].at[jp], **sem))
        return local, sends, waits

    def start(self, ins, outs, sems):
        local, sends, _ = self._copies(ins, outs, sems)
        for cp in local + sends:
            cp.start()

    def wait(self, ins, outs, sems):
        local, _, waits = self._copies(ins, outs, sems)
        for cp in waits:
            cp.wait_recv()
        for cp in waits:
            cp.wait_send()
        for cp in local:
            cp.wait()


def chip_exchange(kind, arrays, name):
    ex = ChipExchange(kind, arrays)
    n = ex.n

    def body(*refs):
        ins, outs, sems = refs[:n], refs[n:2 * n], refs[2 * n:]
        ex.start(ins, outs, sems)
        ex.wait(ins, outs, sems)

    return pl.pallas_call(body, name=name, in_specs=[ANY] * n, out_specs=[ANY] * n, out_shape=ex.out_shape,
                          scratch_shapes=ex.scratch)(*arrays)


def sibling_add(g, ngrp, hr, tr, name):
    c_ = g.shape[1]
    nt = hr // tr

    def body(cidx, keep_ref, give_ref, o_ref, land, ssem, rsem):
        mx, my, mc = _me()
        t = pl.program_id(0) * nt + pl.program_id(1)
        s = t % 2
        cp = pltpu.make_async_remote_copy(src_ref=give_ref, dst_ref=land.at[s], send_sem=ssem.at[s], recv_sem=rsem.at[s],
                                          device_id=(mx, my, 1 - mc), device_id_type=MESH)
        cp.start()
        cp.wait_recv()
        o_ref[...] = keep_ref[...] + land[s]
        cp.wait_send()

    grid_spec = pltpu.PrefetchScalarGridSpec(
        num_scalar_prefetch=1, grid=(ngrp, nt),
        in_specs=[pl.BlockSpec((tr, c_), lambda i, j, cr: ((2 * i + cr[0]) * nt + j, 0)),
                  pl.BlockSpec((tr, c_), lambda i, j, cr: ((2 * i + 1 - cr[0]) * nt + j, 0))],
        out_specs=pl.BlockSpec((tr, c_), lambda i, j, cr: (i * nt + j, 0)),
        scratch_shapes=[pltpu.VMEM((2, tr, c_), F32), pltpu.SemaphoreType.DMA((2,)), pltpu.SemaphoreType.DMA((2,))])
    cidx = lax.axis_index("c").astype(jnp.int32).reshape(1)
    return pl.pallas_call(body, grid_spec=grid_spec, name=name, out_shape=jax.ShapeDtypeStruct((ngrp * hr, c_), F32),
                          compiler_params=_cparams(("arbitrary", "arbitrary")))(cidx, g, g)


def finish_grad(b, tr, name):
    _, r2, c_ = b.shape

    def body(b_ref, g_ref, mine, land, ssem, rsem):
        mx, my, mc = _me()
        t = pl.program_id(0)
        s = t % 2
        mine[s] = (b_ref[0].astype(F32) + b_ref[1].astype(F32)) + (b_ref[2].astype(F32) + b_ref[3].astype(F32))
        cp = pltpu.make_async_remote_copy(src_ref=mine.at[s], dst_ref=land.at[s], send_sem=ssem.at[s], recv_sem=rsem.at[s],
                                          device_id=(mx, my, 1 - mc), device_id_type=MESH)
        cp.start()
        cp.wait_recv()
        g_ref[mc] = mine[s]
        g_ref[1 - mc] = land[s]
        cp.wait_send()

    return pl.pallas_call(
        body, grid=(r2 // tr,), name=name,
        in_specs=[pl.BlockSpec((NCHIP, tr, c_), lambda i: (0, i, 0))],
        out_specs=pl.BlockSpec((2, tr, c_), lambda i: (0, i, 0)), out_shape=jax.ShapeDtypeStruct((2, r2, c_), F32),
        scratch_shapes=[pltpu.VMEM((2, tr, c_), F32), pltpu.VMEM((2, tr, c_), F32), pltpu.SemaphoreType.DMA((2,)),
                        pltpu.SemaphoreType.DMA((2,))],
        compiler_params=_cparams(("arbitrary",)))(b)


TM_IN = 512
TM_GLA = 512
TM_MERGE = 512
TM_FFN = 256
TN_WGRAD = 2048

WEIGHTS = ['c_ctx', 'w_mod', 'b_mod', 'norm1_g', 'norm2_g', 'w_in', 'conv_w', 'conv_b', 'conv_ln_g', 'conv_ln_b', 'w_a2_f',
           'b_a_f', 'w_a2_b', 'b_a_b', 'gla_norm_g', 'w_out', 'w_gate', 'w_up', 'w_down', 'final_g']
BIG = ['w_in', 'w_out', 'w_gate', 'w_up', 'w_down']


def _rows(*vs):
    w = vs[0].size
    row = lax.broadcasted_iota(jnp.int32, (8, w), 0)
    out = jnp.zeros((8, w), F32)
    for i, v in enumerate(vs):
        out = jnp.where(row == i, v.reshape(1, w), out)
    return out


def _small_slab(p):
    cat = lambda *ks: jnp.concatenate([p[k].reshape(-1) for k in ks])
    vecs = _rows(p['c_ctx'], p['norm1_g'], p['norm2_g'], p['final_g'], cat('conv_b', 'conv_ln_g'),
                 cat('conv_ln_b', 'b_a_f', 'b_a_b'), jnp.pad(p['gla_norm_g'].reshape(-1), (0, D - HV)))
    bmod = jnp.pad(p['b_mod'].reshape(6, D), ((0, 2), (0, 0)))
    shards = jnp.pad(jnp.concatenate([jnp.pad(p['conv_w'].reshape(-1), (0, DC // NCHIP)), cat('w_a2_f', 'w_a2_b')]),
                     (0, 2 * D)).reshape(8, D)
    return jnp.concatenate([vecs, bmod, shards], axis=0)


def _unslab(s):
    return {
        'c_ctx': s[0], 'norm1_g': s[1:2], 'norm2_g': s[2:3], 'final_g': s[3],
        'conv_b': s[4:5, :DC], 'conv_ln_g': s[4:5, DC:], 'conv_ln_b': s[5:6, :DC],
        'b_a_f': s[5:6, DC:DC + DK], 'b_a_b': s[5:6, DC + DK:], 'gla_norm_g': s[6:7, :HV],
        'b_mod': s[8:14].reshape(1, 6 * D),
        'conv_w': s[16:20].reshape(32, DC // NCHIP)[:CW].reshape(1, CW, DC // NCHIP),
        'w_a2_f': s[20].reshape(1, RANK, DK // NCHIP), 'w_a2_b': s[21].reshape(1, RANK, DK // NCHIP),
    }


def kernel(x, c, ctx, c_ctx, w_mod, b_mod, norm1_g, norm2_g, w_in, conv_w, conv_b, conv_ln_g, conv_ln_b, w_a2_f, b_a_f, w_a2_b, b_a_b, gla_norm_g, w_out, w_gate, w_up, w_down, final_g, loss_target, m_c_ctx, m_w_mod, m_b_mod, m_norm1_g, m_norm2_g, m_w_in, m_conv_w, m_conv_b, m_conv_ln_g, m_conv_ln_b, m_w_a2_f, m_b_a_f, m_w_a2_b, m_b_a_b, m_gla_norm_g, m_w_out, m_w_gate, m_w_up, m_w_down, m_final_g, v_c_ctx, v_w_mod, v_b_mod, v_norm1_g, v_norm2_g, v_w_in, v_conv_w, v_conv_b, v_conv_ln_g, v_conv_ln_b, v_w_a2_f, v_b_a_f, v_w_a2_b, v_b_a_b, v_gla_norm_g, v_w_out, v_w_gate, v_w_up, v_w_down, v_final_g):
    w = dict(c_ctx=c_ctx, w_mod=w_mod, b_mod=b_mod, norm1_g=norm1_g, norm2_g=norm2_g, w_in=w_in, conv_w=conv_w, conv_b=conv_b,
             conv_ln_g=conv_ln_g, conv_ln_b=conv_ln_b, w_a2_f=w_a2_f, b_a_f=b_a_f, w_a2_b=w_a2_b, b_a_b=b_a_b,
             gla_norm_g=gla_norm_g, w_out=w_out, w_gate=w_gate, w_up=w_up, w_down=w_down, final_g=final_g)
    m = dict(c_ctx=m_c_ctx, w_mod=m_w_mod, b_mod=m_b_mod, norm1_g=m_norm1_g, norm2_g=m_norm2_g, w_in=m_w_in, conv_w=m_conv_w,
             conv_b=m_conv_b, conv_ln_g=m_conv_ln_g, conv_ln_b=m_conv_ln_b, w_a2_f=m_w_a2_f, b_a_f=m_b_a_f, w_a2_b=m_w_a2_b,
             b_a_b=m_b_a_b, gla_norm_g=m_gla_norm_g, w_out=m_w_out, w_gate=m_w_gate, w_up=m_w_up, w_down=m_w_down,
             final_g=m_final_g)
    v = dict(c_ctx=v_c_ctx, w_mod=v_w_mod, b_mod=v_b_mod, norm1_g=v_norm1_g, norm2_g=v_norm2_g, w_in=v_w_in, conv_w=v_conv_w,
             conv_b=v_conv_b, conv_ln_g=v_conv_ln_g, conv_ln_b=v_conv_ln_b, w_a2_f=v_w_a2_f, b_a_f=v_b_a_f, w_a2_b=v_w_a2_b,
             b_a_b=v_b_a_b, gla_norm_g=v_gla_norm_g, w_out=v_w_out, w_gate=v_w_gate, w_up=v_w_up, w_down=v_w_down,
             final_g=v_final_g)
    mx, my, mc = _me()
    jme = 2 * mx + my
    me = 4 * mx + 2 * my + mc
    wmc = D * 6 // NCHIP
    xx, tgt, cx = x[0], loss_target[0], ctx[0]

    bshard = [w[k][0].astype(BF) for k in BIG]
    sw = jnp.concatenate([jnp.pad(conv_w[0], ((0, 1), (0, 0))).reshape(-1), w_a2_f[0].reshape(-1), w_a2_b[0].reshape(-1)])
    small = jnp.concatenate([_rows(c[0]), jnp.pad(sw.reshape(6, D), ((0, 2), (0, 0)))], axis=0)
    cs8, mall, win_g = prologue(small, _rows(c_ctx), w_mod[0], lax.dynamic_slice_in_dim(b_mod, jme * wmc, wmc, axis=1),
                                bshard[0])
    cext = jnp.concatenate([cs8[:, 0, :], _rows(c_ctx)], axis=0)
    swc = jnp.stack([cs8[2 * j, 8:16] for j in range(NCHIP)]).reshape(NCHIP, 8 * D)
    convw = jnp.transpose(swc[:, :32 * 128].reshape(NCHIP, 32, 128), (1, 0, 2)).reshape(32, DC)
    a2 = lambda o: jnp.transpose(swc[:, o:o + RANK * 64].reshape(NCHIP, RANK, 64), (1, 0, 2)).reshape(RANK, DK)
    wa2 = jnp.zeros((128, 2 * DK), F32).at[0:RANK, 0:DK].set(a2(32 * 128)).at[RANK:2 * RANK, DK:].set(a2(32 * 128 + RANK * 64))
    wa2 = wa2.astype(BF)
    mall = jnp.concatenate([mall[2 * j] for j in range(NCHIP)], axis=1)
    sh1, sc1, g1, sh2, sc2, g2 = jnp.split(lax.dynamic_slice_in_dim(mall, me, 1, axis=0)[0], 6)
    csh1, csc1 = mall[8, :D], mall[8, D:2 * D]
    cols = lambda a: jnp.transpose(a, (1, 0, 2)).reshape(a.shape[1], -1)
    win = jnp.pad(cols(win_g), ((0, 0), (0, DINP - DIN)))
    ba = jnp.concatenate([b_a_f, b_a_b], axis=1)
    cvec = _rows(conv_b, conv_ln_g, conv_ln_b)
    vec1 = _rows(norm1_g, sh1, sc1)
    vecc = _rows(norm1_g, csh1, csc1)
    vecm = _rows(g1)
    vecf = _rows(norm2_g, sh2, sc2, g2, final_g)
    gn = jnp.tile(gla_norm_g, (1, NH))

    s0 = ctx_fwd(cx, vecc, win, wa2, ba)
    res = fwd_in(xx, vec1, win, convw, cvec, wa2, ba, TM_IN, ChipExchange("gather", bshard[1:]), bshard[1:])
    ag, yb, co, qk, vv, gg, la, r = res[:8]
    wout = res[8].reshape(D, D)
    wg, wu = cols(res[9]), cols(res[10])
    wd = res[11].reshape(DFF, D)
    o_f, o_b, se_f, se_b = gla_fwd(qk, vv, la, s0, TM_GLA)
    x1, cat = merge_fwd(xx, o_f, o_b, gg, co, vecm, gn, wout, TM_MERGE)

    dx1, h2, act, dgt, dup, dy2, sf = ffn_fwd_bwd(x1, tgt, vecf, wg, wu, wd, TM_FFN)
    d_wg = wgrad(h2, dgt, None, D, DFF // 2, TN_WGRAD, "wgrad_gate")
    d_wu = wgrad(h2, dup, None, D, DFF // 2, TN_WGRAD, "wgrad_up")
    d_wd = wgrad(act, dy2, None, DFF // 2, D, TN_WGRAD, "wgrad_down")
    dy1, dco, do, dg, s1, s2 = merge_bwd(dx1, cat, o_f, o_b, gg, vecm, gn, wout, TM_MERGE)
    d_wout = wgrad(cat, dy1, None, D, D, TN_WGRAD, "wgrad_out")

    shard = lambda a, k: jnp.transpose(a.reshape(a.shape[0], NCHIP, k), (1, 0, 2))
    hd = D // 2
    parts = [sibling_add(d_wout, NCHIP, hd // NCHIP, hd // NCHIP, "xadd_w_out").reshape(NCHIP, hd // NCHIP, D),
             shard(sibling_add(d_wg, 1, hd, hd // 2, "xadd_w_gate"), DFF // NCHIP),
             shard(sibling_add(d_wu, 1, hd, hd // 2, "xadd_w_up"), DFF // NCHIP),
             sibling_add(d_wd, NCHIP, DFF // 8, DFF // 8, "xadd_w_down").reshape(NCHIP, DFF // 8, D)]
    parts = [p.astype(BF) for p in parts]
    res = gla_bwd(qk, vv, la, do, se_f, se_b, TM_GLA, ChipExchange("scatter", parts), parts)
    dqk_f, dv_f, dla_f, dqk_b, dv_b, dla_b, ds0 = res[:7]
    recv = list(res[7:])
    dwin_c, dwa2_c, sc = ctx_bwd(cx, vecc, win, wa2, ba, ds0)
    grad_x, h, dp, dwa2, dcw, sd = bwd_in(xx, dx1, ag, yb, dco, dqk_f, dqk_b, dv_f, dv_b, dg, dla_f, dla_b, la, r,
                                          vec1, win, convw, cvec, wa2, TM_IN)
    d_win = wgrad(h, dp, dwin_c, D, DINP // 3, TN_WGRAD, "wgrad_in")
    part_in = shard(sibling_add(d_win, 1, hd, hd // 2, "xadd_w_in")[:, :DIN], DIN // NCHIP).astype(BF)
    recv = list(chip_exchange("scatter", [part_in], "scatter_w_in")) + recv

    rows16, dcw_t, dwa2_t = pack_small(sf, s1, s2, sd, sc, dcw, dwa2, dwa2_c)
    sp = jnp.concatenate([rows16, dcw_t.reshape(16, D), dwa2_t.reshape(16, D)], axis=0)
    g8 = all_gather8(sp, "gather_small_grads")
    tot, bm_g, loss8 = small_totals(g8)
    loss = loss8[0, 0]
    dmod8 = g8[:, 0:6, :].reshape(NDEV, 6 * D)
    dmodc = jnp.concatenate([tot[6], tot[7], jnp.zeros((4 * D,), F32)])
    dm = jnp.concatenate([dmod8, _rows(dmodc)], axis=0)
    dm = lax.dynamic_slice_in_dim(dm, jme * wmc, wmc, axis=1)
    g_wmod, dsil = mod_bwd(cext, dm, w_mod[0])
    p8 = all_gather8(dsil[8:16], "gather_dsilu")
    g_cctx = cctx_grad(p8, _rows(c_ctx))[0]

    grads, delta, new_m, new_v = {}, {}, {}, {}
    for i, k in enumerate(BIG):
        r2 = recv[i].shape[1]
        gk = finish_grad(recv[i], r2 // 2 if r2 >= 512 else r2, "finish_" + k).reshape(w[k].shape[1:])
        rk = gk.shape[0]
        outs = adamw(w[k][0], gk, m[k][0], v[k][0], rk // 2 if rk >= 512 else rk, "adamw_" + k, emit_grad=True)
        delta[k], new_m[k], new_v[k], grads[k] = (o[None] for o in outs)
    grads['w_mod'] = g_wmod[None]
    d_, m_, v_ = adamw(w_mod[0], g_wmod, m_w_mod[0], v_w_mod[0], 256, "adamw_w_mod")
    delta['w_mod'], new_m['w_mod'], new_v['w_mod'] = d_[None], m_[None], v_[None]
    small_g = {
        'c_ctx': g_cctx, 'b_mod': bm_g[0:6].reshape(1, 6 * D), 'norm1_g': tot[8:9], 'norm2_g': tot[9:10], 'final_g': tot[10],
        'conv_b': tot[11:12, :DC], 'conv_ln_g': tot[11:12, DC:], 'conv_ln_b': tot[12:13, :DC],
        'b_a_f': tot[12:13, DC:DC + DK], 'b_a_b': tot[12:13, DC + DK:], 'gla_norm_g': tot[13:14, :HV],
        'conv_w': lax.dynamic_slice_in_dim(tot[16:32].reshape(32, DC)[:CW], jme * (DC // NCHIP), DC // NCHIP, axis=1)[None],
        'w_a2_f': lax.dynamic_slice_in_dim(tot[32:48].reshape(32, 2 * DK)[0:RANK, 0:DK], jme * (DK // NCHIP), DK // NCHIP, axis=1)[None],
        'w_a2_b': lax.dynamic_slice_in_dim(tot[32:48].reshape(32, 2 * DK)[RANK:2 * RANK, DK:], jme * (DK // NCHIP), DK // NCHIP, axis=1)[None],
    }
    grads.update(small_g)
    sd_, sm_, sv_ = adamw(_small_slab(w), _small_slab(small_g), _small_slab(m), _small_slab(v), 24,
                          "adamw_small")
    for dst, slab in ((delta, sd_), (new_m, sm_), (new_v, sv_)):
        dst.update(_unslab(slab))
    out = [loss, grad_x[None]]
    for group in (grads, delta, new_m, new_v):
        out += [group[k].reshape(w[k].shape) for k in WEIGHTS]
    return tuple(out)
```

```python
import jax
import jax.numpy as jnp
from jax import lax
from jax.experimental import pallas as pl
from jax.experimental.pallas import tpu as pltpu

F32 = jnp.float32
BF = jnp.bfloat16

D = 1024
DC = 512
NH = 4
HK = 64
HV = 128
DK = NH * HK
DV = NH * HV
RANK = 16
CH = 64
GW = 64
CW = 31
SEGP = GW + 32
DFF = 2816
DIN = 2592
DINP = 2688
EPS = 1e-6
TAU = 16.0
QSCALE = HK ** -0.5
NCHIP = 4
NDEV = 8

ADAM_LR = 0.001
ADAM_B1 = 0.9
ADAM_B2 = 0.999
ADAM_EPS = 1e-08
ADAM_WD = 0.01
ADAM_STEP = 10

VMEM_LIMIT = 56 * 1024 * 1024
MESH = pl.DeviceIdType.MESH


def _dot(a, b):
    return jnp.dot(a, b, preferred_element_type=F32)


def _dot_nt(a, b):
    return lax.dot_general(a, b, (((1,), (1,)), ((), ())), preferred_element_type=F32)


def _dot_tn(a, b):
    return lax.dot_general(a, b, (((0,), (0,)), ((), ())), preferred_element_type=F32)


def _mask_dot(t, x):
    hi = x.astype(BF)
    lo = (x - hi.astype(F32)).astype(BF)
    return _dot(t, hi) + _dot(t, lo)


def _sigmoid(x):
    return 1.0 / (1.0 + jnp.exp(-x))


def _log_sigmoid(x):
    return jnp.minimum(x, 0.0) - jnp.log(1.0 + jnp.exp(-jnp.abs(x)))


def _colsum8(z):
    t, c = z.shape
    return jnp.sum(z.reshape(t // 8, 8, c), axis=0)


def _tri(n, kind):
    r = lax.broadcasted_iota(jnp.int32, (n, n), 0)
    c = lax.broadcasted_iota(jnp.int32, (n, n), 1)
    m = {"le": c <= r, "lt": c < r, "ge": c >= r, "gt": c > r}[kind]
    return m


def _full(shape):
    nd = len(shape)
    return pl.BlockSpec(shape, lambda *_: (0,) * nd)


def _cparams(sem, vmem=VMEM_LIMIT):
    return pltpu.CompilerParams(dimension_semantics=sem, vmem_limit_bytes=vmem)


def _call(body, grid, name, in_specs, out_specs, out_shape, scratch, operands, exchange=None, carried=()):
    n_in, n_out, n_scr = len(in_specs), len(out_specs), len(scratch)
    if exchange is None:
        fn = body
    else:
        n = exchange.n

        def fn(*refs):
            ins, cin = refs[:n_in], refs[n_in:n_in + n]
            outs, cout = refs[n_in + n:n_in + n + n_out], refs[n_in + n + n_out:n_in + 2 * n + n_out]
            rest = refs[n_in + 2 * n + n_out:]
            scr, sems = rest[:n_scr], rest[n_scr:]

            @pl.when(pl.program_id(0) == 0)
            def _():
                exchange.start(cin, cout, sems)

            body(*ins, *outs, *scr)

            @pl.when(pl.program_id(0) == pl.num_programs(0) - 1)
            def _():
                exchange.wait(cin, cout, sems)

        any_spec = pl.BlockSpec(memory_space=pl.ANY)
        in_specs = list(in_specs) + [any_spec] * n
        out_specs = list(out_specs) + [any_spec] * n
        out_shape = list(out_shape) + exchange.out_shape
        scratch = list(scratch) + exchange.scratch
    return pl.pallas_call(fn, grid=grid, name=name, in_specs=in_specs, out_specs=out_specs, out_shape=out_shape,
                          scratch_shapes=scratch, compiler_params=_cparams(("arbitrary",)))(*operands, *carried)


def _fill_padded(pad_ref, val, nseg):
    zeros = jnp.zeros((nseg, 16, val.shape[-1]), F32)
    pad_ref[:, 0:16, :] = zeros
    pad_ref[:, 16 + GW:SEGP, :] = zeros
    pad_ref[:, 16:16 + GW, :] = val.reshape(nseg, GW, val.shape[-1])


def _tap_slabs(pad_ref, s, cs):
    whole = pad_ref[s, :, cs]
    for r in range(8):
        slab = whole if r == 0 else pltpu.roll(whole, SEGP - r, axis=0)
        for a in range(4):
            j = r + 8 * a - 1
            if 0 <= j < CW:
                yield j, slab[8 * a:8 * a + GW]


def _conv_taps(pad_ref, s, w_ref, c0, cw, flip):
    acc = jnp.zeros((GW, cw), F32)
    for j, rows in _tap_slabs(pad_ref, s, pl.ds(c0, cw)):
        acc = acc + w_ref[pl.ds((CW - 1 - j) if flip else j, 1), pl.ds(c0, cw)] * rows
    return acc


def _ln_stats(yb):
    mu = jnp.mean(yb, axis=-1, keepdims=True)
    yc = yb - mu
    var = jnp.mean(yc * yc, axis=-1, keepdims=True)
    rs = lax.rsqrt(var + EPS)
    return yc * rs, rs


def fwd_in(x, vec1, win, convw, cvec, wa2, ba, tm, exchange=None, carried=()):
    n = x.shape[0]
    nseg = tm // GW
    cg = 128

    def body(x_ref, vec_ref, win_ref, cw_ref, cv_ref, wa2_ref, ba_ref,
             ag_ref, yb_ref, co_ref, qk_ref, v_ref, g_ref, la_ref, r_ref, pad_ref):
        xx = x_ref[...]
        rstd = lax.rsqrt(jnp.mean(xx * xx, axis=-1, keepdims=True) + EPS)
        h = ((xx * rstd * vec_ref[0:1, :]) * (1.0 + vec_ref[2:3, :]) + vec_ref[1:2, :]).astype(BF)
        pc = _dot(h, win_ref[:, :2 * DC])
        ag_ref[...] = pc.astype(BF)
        _fill_padded(pad_ref, pc[:, :DC] * _sigmoid(pc[:, DC:]), nseg)
        for s in range(nseg):
            for c0 in range(0, DC, cg):
                y = _conv_taps(pad_ref, s, cw_ref, c0, cg, False)
                yb_ref[pl.ds(s * GW, GW), pl.ds(c0, cg)] = y + cv_ref[0:1, c0:c0 + cg]
        p = _dot(h, win_ref[:, 2 * DC:])
        qk_ref[...] = p[:, :2 * DK].astype(BF)
        v_ref[...] = p[:, 2 * DK:2 * DK + DV].astype(BF)
        g_ref[...] = p[:, 2 * DK + DV:2 * DK + 2 * DV].astype(BF)
        r = p[:, 2 * DK + 2 * DV:].astype(BF)
        r_ref[...] = r
        la_ref[...] = _log_sigmoid(_dot(r, wa2_ref[...]) + ba_ref[...]) * (1.0 / TAU)
        yn, _ = _ln_stats(yb_ref[...])
        ln = yn * cv_ref[1:2, :] + cv_ref[2:3, :]
        co_ref[...] = (ln * _sigmoid(ln)).astype(BF)

    tok = lambda w: pl.BlockSpec((tm, w), lambda i: (i, 0))
    return _call(
        body, (n // tm,), "fwd_in",
        [tok(D), _full(vec1.shape), _full(win.shape), _full(convw.shape), _full(cvec.shape), _full(wa2.shape), _full(ba.shape)],
        [tok(2 * DC), tok(DC), tok(DC), tok(2 * DK), tok(DV), tok(DV), tok(2 * DK), tok(128)],
        [jax.ShapeDtypeStruct((n, 2 * DC), BF), jax.ShapeDtypeStruct((n, DC), F32),
         jax.ShapeDtypeStruct((n, DC), BF), jax.ShapeDtypeStruct((n, 2 * DK), BF),
         jax.ShapeDtypeStruct((n, DV), BF), jax.ShapeDtypeStruct((n, DV), BF),
         jax.ShapeDtypeStruct((n, 2 * DK), F32), jax.ShapeDtypeStruct((n, 128), BF)],
        [pltpu.VMEM((nseg, SEGP, DC), F32)],
        (x, vec1, win, convw, cvec, wa2, ba), exchange, carried)


def _gla_dir(d):
    return (_tri(CH, "le"), CH - 1) if d == 0 else (_tri(CH, "ge"), 0)


def _gla_chunk_terms(qk, la, d):
    seen, last = _gla_dir(d)
    b = _mask_dot(seen.astype(BF), la)
    bl = b[last:last + 1, :]
    eb = jnp.exp(b)
    enb = jnp.exp(-b)
    ekd = jnp.exp(bl - b)
    ebl = jnp.exp(bl)
    q = qk[:, :DK].astype(F32) * QSCALE
    k = qk[:, DK:].astype(F32)
    return eb, enb, ekd, ebl, q * eb, k * enb, k * ekd


NP = NH // 2
PW = 2 * HK


def _lo_lanes(shape):
    return lax.broadcasted_iota(jnp.int32, shape, len(shape) - 1) < HK


def _pair_sel(lo, hi):
    return jnp.where(_lo_lanes(lo.shape), lo, hi)


def _only(x, which):
    keep = _lo_lanes(x.shape) if which == 0 else jnp.logical_not(_lo_lanes(x.shape))
    return jnp.where(keep, x, jnp.zeros_like(x))


def gla_fwd(qk, v, la, s0, tm):
    n = qk.shape[0]
    nt = n // tm
    nc = tm // CH

    def body(qkf_ref, vf_ref, laf_ref, qkb_ref, vb_ref, lab_ref, s0_ref, of_ref, ob_ref, sef_ref, seb_ref, st_ref):
        @pl.when(pl.program_id(0) == 0)
        def _():
            st_ref[...] = s0_ref[...]

        def chunk(ci, carry):
            t = []
            for d, (qk_ref, v_ref, la_ref) in enumerate(((qkf_ref, vf_ref, laf_ref), (qkb_ref, vb_ref, lab_ref))):
                c = ci if d == 0 else nc - 1 - ci
                rows = pl.ds(pl.multiple_of(c * CH, CH), CH)
                eb, enb, ekd, ebl, qt, kt, kd = _gla_chunk_terms(qk_ref[rows, :], la_ref[rows, :], d)
                t.append(dict(c=c, rows=rows, ebl=ebl, qt=qt.astype(BF), kt=kt.astype(BF), kd=kd.astype(BF),
                              vv=v_ref[rows, :], st=[st_ref[d, p] for p in range(NP)], amask=_gla_dir(d)[0]))
            dh = [(d, h) for d in range(2) for h in range(NH)]
            dp = [(d, p) for d in range(2) for p in range(NP)]
            ps = lambda h: slice((h // 2) * PW, (h // 2 + 1) * PW)
            vs = lambda h: slice(h * HV, (h + 1) * HV)
            v2 = lambda p: slice(2 * p * HV, 2 * (p + 1) * HV)
            qm = {(d, h): _only(t[d]['qt'][:, ps(h)], h % 2) for d, h in dh}
            q2 = {(d, p): jnp.concatenate([qm[d, 2 * p], qm[d, 2 * p + 1]], axis=0) for d, p in dp}
            a2 = {(d, p): _dot_nt(q2[d, p], t[d]['kt'][:, p * PW:(p + 1) * PW]) for d, p in dp}
            a = {(d, h): jnp.where(t[d]['amask'], a2[d, h // 2][(h % 2) * CH:(h % 2 + 1) * CH], 0.0).astype(BF) for d, h in dh}
            oi = {(d, p): _dot_nt(q2[d, p], t[d]['st'][p].astype(BF)) for d, p in dp}
            o = {(d, h): _dot(a[d, h], t[d]['vv'][:, vs(h)]) + oi[d, h // 2][(h % 2) * CH:(h % 2 + 1) * CH] for d, h in dh}
            kv = {(d, p): _dot_tn(t[d]['vv'][:, v2(p)], t[d]['kd'][:, p * PW:(p + 1) * PW]) for d, p in dp}
            for d, (o_ref, se_ref) in enumerate(((of_ref, sef_ref), (ob_ref, seb_ref))):
                for h in range(NH):
                    o_ref[t[d]['rows'], vs(h)] = o[d, h].astype(BF)
                for p in range(NP):
                    se_ref[t[d]['c'], p] = t[d]['st'][p]
                    st_ref[d, p] = (t[d]['ebl'][:, p * PW:(p + 1) * PW] * t[d]['st'][p]
                                    + _pair_sel(kv[d, p][:HV], kv[d, p][HV:]))
            return carry

        lax.fori_loop(0, nc, chunk, 0, unroll=4)

    fw = lambda w, col=0: pl.BlockSpec((tm, w), lambda i: (i, col))
    bw = lambda w, col=0: pl.BlockSpec((tm, w), lambda i: (nt - 1 - i, col))
    se_f = pl.BlockSpec((nc, NP, HV, PW), lambda i: (i, 0, 0, 0))
    se_b = pl.BlockSpec((nc, NP, HV, PW), lambda i: (nt - 1 - i, 0, 0, 0))
    se_shape = jax.ShapeDtypeStruct((n // CH, NP, HV, PW), F32)
    return pl.pallas_call(
        body, grid=(nt,), name="gla_fwd",
        in_specs=[fw(2 * DK), fw(DV), fw(DK, 0), bw(2 * DK), bw(DV), bw(DK, 1), _full(s0.shape)],
        out_specs=[fw(DV), bw(DV), se_f, se_b],
        out_shape=[jax.ShapeDtypeStruct((n, DV), BF), jax.ShapeDtypeStruct((n, DV), BF), se_shape, se_shape],
        scratch_shapes=[pltpu.VMEM((2, NP, HV, PW), F32)],
        compiler_params=_cparams(("arbitrary",)),
    )(qk, v, la, qk, v, la, s0)


def gla_bwd(qk, v, la, do, se_f, se_b, tm, exchange=None, carried=()):
    n = qk.shape[0]
    nt = n // tm
    nc = tm // CH

    def body(qkf_ref, vf_ref, laf_ref, dof_ref, sef_ref, qkb_ref, vb_ref, lab_ref, dob_ref, seb_ref,
             dqkf_ref, dvf_ref, dlaf_ref, dqkb_ref, dvb_ref, dlab_ref, ds0_ref, ds_ref):
        @pl.when(pl.program_id(0) == 0)
        def _():
            ds_ref[...] = jnp.zeros_like(ds_ref)

        def chunk(ci, carry):
            t = []
            for d, (qk_ref, v_ref, la_ref, do_ref, se_ref) in enumerate(
                    ((qkf_ref, vf_ref, laf_ref, dof_ref, sef_ref), (qkb_ref, vb_ref, lab_ref, dob_ref, seb_ref))):
                c = nc - 1 - ci if d == 0 else ci
                rows = pl.ds(pl.multiple_of(c * CH, CH), CH)
                amask, last = _gla_dir(d)
                eb, enb, ekd, ebl, qt, kt, kd = _gla_chunk_terms(qk_ref[rows, :], la_ref[rows, :], d)
                t.append(dict(rows=rows, amask=amask, last=last, eb=eb, enb=enb, ekd=ekd, ebl=ebl, qt=qt, kt=kt, kd=kd,
                              qtb=qt.astype(BF), ktb=kt.astype(BF), kdb=kd.astype(BF), vv=v_ref[rows, :], dd=do_ref[rows, :],
                              st=[se_ref[c, p] for p in range(NP)], dsn=[ds_ref[d, p] for p in range(NP)]))
            dh = [(d, h) for d in range(2) for h in range(NH)]
            dp = [(d, p) for d in range(2) for p in range(NP)]
            ps = lambda h: slice((h // 2) * PW, (h // 2 + 1) * PW)
            vs = lambda h: slice(h * HV, (h + 1) * HV)
            stb = {(d, p): t[d]['st'][p].astype(BF) for d, p in dp}
            dsnb = {(d, p): t[d]['dsn'][p].astype(BF) for d, p in dp}
            qm = {(d, h): _only(t[d]['qtb'][:, ps(h)], h % 2) for d, h in dh}
            km = {(d, h): _only(t[d]['kdb'][:, ps(h)], h % 2) for d, h in dh}
            a2 = {(d, p): _dot_nt(jnp.concatenate([qm[d, 2 * p], qm[d, 2 * p + 1]], axis=0), t[d]['ktb'][:, p * PW:(p + 1) * PW])
                  for d, p in dp}
            a = {(d, h): jnp.where(t[d]['amask'], a2[d, h // 2][(h % 2) * CH:(h % 2 + 1) * CH], 0.0).astype(BF) for d, h in dh}
            da = {(d, h): jnp.where(t[d]['amask'], _dot_nt(t[d]['dd'][:, vs(h)], t[d]['vv'][:, vs(h)]), 0.0).astype(BF)
                  for d, h in dh}
            v2 = lambda p: slice(2 * p * HV, 2 * (p + 1) * HV)
            rows2 = lambda x, d, p: jnp.concatenate([x[d, 2 * p], x[d, 2 * p + 1]], axis=0)
            half = lambda x, h: x[(h % 2) * CH:(h % 2 + 1) * CH]
            dvs = {(d, p): _dot_nt(rows2(km, d, p), dsnb[d, p]) for d, p in dp}
            dv = {(d, h): _dot_tn(a[d, h], t[d]['dd'][:, vs(h)]) + half(dvs[d, h // 2], h) for d, h in dh}
            vrows = lambda d, p: jnp.concatenate([t[d]['vv'][:, vs(2 * p)], t[d]['vv'][:, vs(2 * p + 1)]], axis=0)
            drows = lambda d, p: jnp.concatenate([t[d]['dd'][:, vs(2 * p)], t[d]['dd'][:, vs(2 * p + 1)]], axis=0)
            dkd2 = {(d, p): _dot(vrows(d, p), dsnb[d, p]) for d, p in dp}
            dqt2 = {(d, p): _dot(rows2(da, d, p), t[d]['ktb'][:, p * PW:(p + 1) * PW]) + _dot(drows(d, p), stb[d, p])
                    for d, p in dp}
            dkt2 = {(d, p): _dot_tn(jnp.concatenate([da[d, 2 * p], da[d, 2 * p + 1]], axis=1), t[d]['qtb'][:, p * PW:(p + 1) * PW])
                    for d, p in dp}
            dsq2 = {(d, p): _dot_tn(t[d]['dd'][:, v2(p)], t[d]['qtb'][:, p * PW:(p + 1) * PW]) for d, p in dp}
            two = lambda x, d, p, n: _pair_sel(x[d, p][:n], x[d, p][n:])
            for d, (dqk_ref, dv_ref, dla_ref) in enumerate(((dqkf_ref, dvf_ref, dlaf_ref), (dqkb_ref, dvb_ref, dlab_ref))):
                td = t[d]
                rows = td['rows']
                for h in range(NH):
                    dv_ref[rows, vs(h)] = dv[d, h].astype(BF)
                pair = lambda x: jnp.concatenate([two(x, d, p, CH) for p in range(NP)], axis=1)
                dqt_, dkt_, dkd_ = pair(dqt2), pair(dkt2), pair(dkd2)
                debl = jnp.concatenate([jnp.sum(td['st'][p] * td['dsn'][p], axis=0, keepdims=True) for p in range(NP)], axis=1)
                for p in range(NP):
                    ds_ref[d, p] = two(dsq2, d, p, HV) + td['ebl'][:, p * PW:(p + 1) * PW] * td['dsn'][p]
                dkdkd = dkd_ * td['kd']
                dbl = jnp.sum(dkdkd, axis=0, keepdims=True) + debl * td['ebl']
                is_last = lax.broadcasted_iota(jnp.int32, (CH, DK), 0) == td['last']
                db = dqt_ * td['qt'] - dkt_ * td['kt'] - dkdkd + jnp.where(is_last, dbl, 0.0)
                dqk_ref[rows, :] = jnp.concatenate([dqt_ * td['eb'] * QSCALE, dkt_ * td['enb'] + dkd_ * td['ekd']], axis=1).astype(BF)
                dla_ref[rows, :] = _mask_dot(_gla_dir(1 - d)[0].astype(BF), db)
            return carry

        lax.fori_loop(0, nc, chunk, 0, unroll=4)

        @pl.when(pl.program_id(0) == nt - 1)
        def _():
            ds0_ref[...] = ds_ref[...]

    up = lambda w, col=0: pl.BlockSpec((tm, w), lambda i: (i, col))
    dn = lambda w, col=0: pl.BlockSpec((tm, w), lambda i: (nt - 1 - i, col))
    se_up = pl.BlockSpec((nc, NP, HV, PW), lambda i: (i, 0, 0, 0))
    se_dn = pl.BlockSpec((nc, NP, HV, PW), lambda i: (nt - 1 - i, 0, 0, 0))
    return _call(
        body, (nt,), "gla_bwd",
        [dn(2 * DK), dn(DV), dn(DK, 0), dn(DV), se_dn, up(2 * DK), up(DV), up(DK, 1), up(DV), se_up],
        [dn(2 * DK), dn(DV), dn(DK), up(2 * DK), up(DV), up(DK), _full((2, NP, HV, PW))],
        [jax.ShapeDtypeStruct((n, 2 * DK), BF), jax.ShapeDtypeStruct((n, DV), BF),
         jax.ShapeDtypeStruct((n, DK), F32), jax.ShapeDtypeStruct((n, 2 * DK), BF),
         jax.ShapeDtypeStruct((n, DV), BF), jax.ShapeDtypeStruct((n, DK), F32),
         jax.ShapeDtypeStruct((2, NP, HV, PW), F32)],
        [pltpu.VMEM((2, NP, HV, PW), F32)],
        (qk, v, la, do, se_f, qk, v, la, do, se_b), exchange, carried)


def _head_norm(o):
    ons, rss = [], []
    for h in range(NH):
        oh = o[:, h * HV:(h + 1) * HV]
        rs = lax.rsqrt(jnp.mean(oh * oh, axis=-1, keepdims=True) + EPS)
        ons.append(oh * rs)
        rss.append(rs)
    return ons, rss


def merge_fwd(x, o_f, o_b, g, co, vecm, gn, wout, tm):
    n = x.shape[0]

    def body(x_ref, of_ref, ob_ref, g_ref, co_ref, vec_ref, gn_ref, w_ref, x1_ref, y1_ref, cat_ref):
        o = of_ref[...].astype(F32) + ob_ref[...].astype(F32)
        ons, _ = _head_norm(o)
        gg = g_ref[...].astype(F32)
        sil = gg * _sigmoid(gg)
        cat_ref[:, :DC] = co_ref[...]
        for h in range(NH):
            vs = slice(h * HV, (h + 1) * HV)
            cat_ref[:, DC + h * HV:DC + (h + 1) * HV] = (ons[h] * gn_ref[:, vs] * sil[:, vs]).astype(BF)
        y1 = _dot(cat_ref[...], w_ref[...])
        y1_ref[...] = y1.astype(BF)
        x1_ref[...] = x_ref[...] + vec_ref[0:1, :] * y1

    tok = lambda w: pl.BlockSpec((tm, w), lambda i: (i, 0))
    return pl.pallas_call(
        body, grid=(n // tm,), name="merge_fwd",
        in_specs=[tok(D), tok(DV), tok(DV), tok(DV), tok(DC), _full(vecm.shape), _full(gn.shape), _full(wout.shape)],
        out_specs=[tok(D), tok(D), tok(D)],
        out_shape=[jax.ShapeDtypeStruct((n, D), F32), jax.ShapeDtypeStruct((n, D), BF), jax.ShapeDtypeStruct((n, D), BF)],
        compiler_params=_cparams(("arbitrary",)),
    )(x, o_f, o_b, g, co, vecm, gn, wout)


def merge_bwd(dx1, y1, o_f, o_b, g, vecm, gn, wout, tm):
    n = dx1.shape[0]

    def body(dx1_ref, y1_ref, of_ref, ob_ref, g_ref, vec_ref, gn_ref, w_ref,
             dy1_ref, dco_ref, do_ref, dg_ref, s1_ref, s2_ref):
        @pl.when(pl.program_id(0) == 0)
        def _():
            s1_ref[...] = jnp.zeros_like(s1_ref)
            s2_ref[...] = jnp.zeros_like(s2_ref)

        dx1 = dx1_ref[...]
        s1_ref[...] += _colsum8(dx1 * y1_ref[...].astype(F32))
        dy1 = (dx1 * vec_ref[0:1, :]).astype(BF)
        dy1_ref[...] = dy1
        dcat = _dot_nt(dy1, w_ref[...])
        dco_ref[...] = dcat[:, :DC].astype(BF)
        o = of_ref[...].astype(F32) + ob_ref[...].astype(F32)
        ons, rss = _head_norm(o)
        gg = g_ref[...].astype(F32)
        sg = _sigmoid(gg)
        sil = gg * sg
        dsil = sg * (1.0 + gg * (1.0 - sg))
        for h in range(NH):
            vs = slice(h * HV, (h + 1) * HV)
            do2 = dcat[:, DC + h * HV:DC + (h + 1) * HV]
            gnh = gn_ref[:, vs]
            t = do2 * sil[:, vs]
            s2_ref[:, vs] += _colsum8(t * ons[h])
            don = t * gnh
            do_ref[:, vs] = (rss[h] * (don - ons[h] * jnp.mean(don * ons[h], axis=-1, keepdims=True))).astype(BF)
            dg_ref[:, vs] = (do2 * ons[h] * gnh * dsil[:, vs]).astype(BF)

    tok = lambda w: pl.BlockSpec((tm, w), lambda i: (i, 0))
    return pl.pallas_call(
        body, grid=(n // tm,), name="merge_bwd",
        in_specs=[tok(D), tok(D), tok(DV), tok(DV), tok(DV), _full(vecm.shape), _full(gn.shape), _full(wout.shape)],
        out_specs=[tok(D), tok(DC), tok(DV), tok(DV), _full((8, D)), _full((8, DV))],
        out_shape=[jax.ShapeDtypeStruct((n, D), BF), jax.ShapeDtypeStruct((n, DC), BF), jax.ShapeDtypeStruct((n, DV), BF),
                   jax.ShapeDtypeStruct((n, DV), BF), jax.ShapeDtypeStruct((8, D), F32), jax.ShapeDtypeStruct((8, DV), F32)],
        compiler_params=_cparams(("arbitrary",)),
    )(dx1, y1, o_f, o_b, g, vecm, gn, wout)


def ffn_fwd_bwd(x1, tgt, vecf, wg, wu, wd, tm):
    n = x1.shape[0]

    def body(x1_ref, t_ref, vec_ref, wg_ref, wu_ref, wd_ref,
             dx1_ref, h2_ref, act_ref, dgt_ref, dup_ref, dy2_ref, s_ref):
        @pl.when(pl.program_id(0) == 0)
        def _():
            s_ref[...] = jnp.zeros_like(s_ref)

        n2g, sh2, sc2, g2, fg = (vec_ref[i:i + 1, :] for i in range(5))
        x1 = x1_ref[...]
        r2 = lax.rsqrt(jnp.mean(x1 * x1, axis=-1, keepdims=True) + EPS)
        xn2 = x1 * r2
        h2 = (xn2 * n2g * (1.0 + sc2) + sh2).astype(BF)
        h2_ref[...] = h2
        gt = _dot(h2, wg_ref[...])
        up = _dot(h2, wu_ref[...])
        sg = _sigmoid(gt)
        sil = gt * sg
        act = (sil * up).astype(BF)
        act_ref[...] = act
        y2 = _dot(act, wd_ref[...])
        x2 = x1 + g2 * y2
        r3 = lax.rsqrt(jnp.mean(x2 * x2, axis=-1, keepdims=True) + EPS)
        xn3 = x2 * r3
        e = xn3 * fg - t_ref[...]
        s_ref[40:48, :] += _colsum8(e * e) * (0.5 / D)
        dyo = e * (1.0 / D)
        s_ref[0:8, :] += _colsum8(dyo * xn3)
        dxn3 = dyo * fg
        dx2 = r3 * (dxn3 - xn3 * jnp.mean(dxn3 * xn3, axis=-1, keepdims=True))
        s_ref[8:16, :] += _colsum8(dx2 * y2)
        dy2 = (dx2 * g2).astype(BF)
        dy2_ref[...] = dy2
        dact = _dot_nt(dy2, wd_ref[...])
        dup = (dact * sil).astype(BF)
        dgt = (dact * up * (sg * (1.0 + gt * (1.0 - sg)))).astype(BF)
        dup_ref[...] = dup
        dgt_ref[...] = dgt
        dh2 = _dot_nt(dgt, wg_ref[...]) + _dot_nt(dup, wu_ref[...])
        s_ref[16:24, :] += _colsum8(dh2)
        t = dh2 * xn2
        s_ref[24:32, :] += _colsum8(t * n2g)
        s_ref[32:40, :] += _colsum8(t * (1.0 + sc2))
        dxn2 = dh2 * ((1.0 + sc2) * n2g)
        dx1_ref[...] = dx2 + r2 * (dxn2 - xn2 * jnp.mean(dxn2 * xn2, axis=-1, keepdims=True))

    tok = lambda w: pl.BlockSpec((tm, w), lambda i: (i, 0))
    wspec = lambda a: pl.BlockSpec(a.shape, lambda i: (0, 0), pipeline_mode=pl.Buffered(1))
    return pl.pallas_call(
        body, grid=(n // tm,), name="ffn_fwd_bwd",
        in_specs=[tok(D), tok(D), _full(vecf.shape), wspec(wg), wspec(wu), wspec(wd)],
        out_specs=[tok(D), tok(D), tok(DFF), tok(DFF), tok(DFF), tok(D), _full((48, D))],
        out_shape=[jax.ShapeDtypeStruct((n, D), F32), jax.ShapeDtypeStruct((n, D), BF), jax.ShapeDtypeStruct((n, DFF), BF),
                   jax.ShapeDtypeStruct((n, DFF), BF), jax.ShapeDtypeStruct((n, DFF), BF), jax.ShapeDtypeStruct((n, D), BF),
                   jax.ShapeDtypeStruct((48, D), F32)],
        compiler_params=_cparams(("arbitrary",)),
    )(x1, tgt, vecf, wg, wu, wd)


def wgrad(a, b, init, t1, t2, tn, name):
    n, k1 = a.shape
    k2 = b.shape[1]

    def body(a_ref, b_ref, *rest):
        o_ref = rest[-1]

        @pl.when(pl.program_id(2) == 0)
        def _():
            o_ref[...] = rest[0][...] if init is not None else jnp.zeros_like(o_ref)

        o_ref[...] += _dot_tn(a_ref[...], b_ref[...])

    ospec = pl.BlockSpec((t1, t2), lambda i, j, k: (i, j))
    extra = ([ospec], {2: 0}, (init,)) if init is not None else ([], {}, ())
    return pl.pallas_call(
        body, grid=(k1 // t1, k2 // t2, n // tn), name=name,
        in_specs=[pl.BlockSpec((tn, t1), lambda i, j, k: (k, i)), pl.BlockSpec((tn, t2), lambda i, j, k: (k, j))] + extra[0],
        out_specs=ospec, out_shape=jax.ShapeDtypeStruct((k1, k2), F32), input_output_aliases=extra[1],
        compiler_params=_cparams(("parallel", "parallel", "arbitrary")),
    )(a, b, *extra[2])


def bwd_in(x, dx1, ag, yb, dco, dqk_f, dqk_b, dv_f, dv_b, dg, dla_f, dla_b, la, r, vec1, win, convw, cvec, wa2, tm):
    n = x.shape[0]
    nseg = tm // GW
    cg = 128

    def body(x_ref, dx1_ref, ag_ref, yb_ref, dco_ref, dqkf_ref, dqkb_ref, dvf_ref, dvb_ref, dg_ref, dlaf_ref, dlab_ref,
             la_ref, r_ref, vec_ref, win_ref, cw_ref, cv_ref, wa2_ref,
             gx_ref, h_ref, dp_ref, dwa2_ref, dcw_ref, s_ref, vc_ref, pad2_ref, dvc_ref, dcw8_ref):
        first = pl.program_id(0) == 0

        @pl.when(first)
        def _():
            s_ref[...] = jnp.zeros_like(s_ref)
            dwa2_ref[...] = jnp.zeros_like(dwa2_ref)
            dcw8_ref[...] = jnp.zeros_like(dcw8_ref)

        yn, rs = _ln_stats(yb_ref[...])
        lng = cv_ref[1:2, :]
        ln = yn * lng + cv_ref[2:3, :]
        sgl = _sigmoid(ln)
        dln = dco_ref[...].astype(F32) * (sgl * (1.0 + ln * (1.0 - sgl)))
        dyn = dln * lng
        dyb = rs * (dyn - jnp.mean(dyn, axis=-1, keepdims=True) - yn * jnp.mean(dyn * yn, axis=-1, keepdims=True))
        s_ref[24:32, 0:DC] += _colsum8(dyb)
        s_ref[24:32, DC:D] += _colsum8(dln * yn)
        s_ref[32:40, 0:DC] += _colsum8(dln)

        agv = ag_ref[...].astype(F32)
        a = agv[:, :DC]
        sgg = _sigmoid(agv[:, DC:])
        vc_ref[...] = a * sgg
        _fill_padded(pad2_ref, dyb, nseg)

        dp_ref[:, 2 * DC:2 * DC + 2 * DK] = (dqkf_ref[...].astype(F32) + dqkb_ref[...].astype(F32)).astype(BF)
        dp_ref[:, 2 * DC + 2 * DK:2 * DC + 2 * DK + DV] = (dvf_ref[...].astype(F32) + dvb_ref[...].astype(F32)).astype(BF)
        dp_ref[:, 2 * DC + 2 * DK + DV:2 * DC + 2 * DK + 2 * DV] = dg_ref[...]

        la = la_ref[...]
        dla = jnp.concatenate([dlaf_ref[...], dlab_ref[...]], axis=1)
        dpre = dla * (1.0 - jnp.exp(TAU * la)) * (1.0 / TAU)
        s_ref[32:40, DC:D] += _colsum8(dpre)
        dpreb = dpre.astype(BF)
        dwa2_ref[...] += _dot_tn(r_ref[...], dpreb)
        dp_ref[:, DINP - 128:] = _dot_nt(dpreb, wa2_ref[...]).astype(BF)
        dh_rest = _dot_nt(dp_ref[:, 2 * DC:], win_ref[:, 2 * DC:])

        for s in range(nseg):
            rows = pl.ds(s * GW, GW)
            for c0 in range(0, DC, cg):
                cs = pl.ds(c0, cg)
                vcs = vc_ref[rows, cs]
                acc = jnp.zeros((GW, cg), F32)
                for j, rows_j in _tap_slabs(pad2_ref, s, cs):
                    acc = acc + cw_ref[pl.ds(CW - 1 - j, 1), cs] * rows_j
                    dcw8_ref[CW - 1 - j, :, cs] += _colsum8(vcs * rows_j)
                dvc_ref[rows, cs] = acc
        dvc = dvc_ref[...]
        dp_ref[:, 0:DC] = (dvc * sgg).astype(BF)
        dp_ref[:, DC:2 * DC] = (dvc * a * sgg * (1.0 - sgg)).astype(BF)

        dh = dh_rest + _dot_nt(dp_ref[:, :2 * DC], win_ref[:, :2 * DC])
        xx = x_ref[...]
        n1g, sh1, sc1 = vec_ref[0:1, :], vec_ref[1:2, :], vec_ref[2:3, :]
        rstd = lax.rsqrt(jnp.mean(xx * xx, axis=-1, keepdims=True) + EPS)
        xn = xx * rstd
        h_ref[...] = (xn * n1g * (1.0 + sc1) + sh1).astype(BF)
        s_ref[0:8, :] += _colsum8(dh)
        t = dh * xn
        s_ref[8:16, :] += _colsum8(t * n1g)
        s_ref[16:24, :] += _colsum8(t * (1.0 + sc1))
        dxn = dh * ((1.0 + sc1) * n1g)
        gx_ref[...] = dx1_ref[...] + rstd * (dxn - xn * jnp.mean(dxn * xn, axis=-1, keepdims=True))

        @pl.when(pl.program_id(0) == pl.num_programs(0) - 1)
        def _():
            dcw_ref[...] = jnp.sum(dcw8_ref[...], axis=1)

    tok = lambda w: pl.BlockSpec((tm, w), lambda i: (i, 0))
    return pl.pallas_call(
        body, grid=(n // tm,), name="bwd_in",
        in_specs=[tok(D), tok(D), tok(2 * DC), tok(DC), tok(DC), tok(2 * DK), tok(2 * DK), tok(DV), tok(DV), tok(DV),
                  tok(DK), tok(DK), tok(2 * DK), tok(128), _full(vec1.shape),
                  pl.BlockSpec(win.shape, lambda i: (0, 0), pipeline_mode=pl.Buffered(1)),
                  _full(convw.shape), _full(cvec.shape), _full(wa2.shape)],
        out_specs=[tok(D), tok(D), tok(DINP), _full((128, 2 * DK)), _full((32, DC)), _full((40, D))],
        out_shape=[jax.ShapeDtypeStruct((n, D), F32), jax.ShapeDtypeStruct((n, D), BF), jax.ShapeDtypeStruct((n, DINP), BF),
                   jax.ShapeDtypeStruct((128, 2 * DK), F32), jax.ShapeDtypeStruct((32, DC), F32),
                   jax.ShapeDtypeStruct((40, D), F32)],
        scratch_shapes=[pltpu.VMEM((tm, DC), F32), pltpu.VMEM((nseg, SEGP, DC), F32), pltpu.VMEM((tm, DC), F32),
                        pltpu.VMEM((32, 8, DC), F32)],
        compiler_params=_cparams(("arbitrary",)),
    )(x, dx1, ag, yb, dco, dqk_f, dqk_b, dv_f, dv_b, dg, dla_f, dla_b, la, r, vec1, win, convw, cvec, wa2)


def _ctx_common(ctx_ref, vec_ref, win_ref, wa2_ref, ba_ref):
    cx = ctx_ref[...]
    t = cx.shape[0]
    rstd = lax.rsqrt(jnp.mean(cx * cx, axis=-1, keepdims=True) + EPS)
    xn = cx * rstd
    hc = (xn * vec_ref[0:1, :] * (1.0 + vec_ref[2:3, :]) + vec_ref[1:2, :]).astype(BF)
    k0 = 2 * DC + DK
    kv = _dot(hc, win_ref[:, k0:k0 + DK + DV]).astype(BF).astype(F32)
    r = _dot(hc, win_ref[:, DINP - 128:]).astype(BF)
    la = _log_sigmoid(_dot(r, wa2_ref[...]) + ba_ref[...]) * (1.0 / TAU)
    incl = _tri(t, "le").astype(BF)
    strict = _tri(t, "lt").astype(BF)
    bf = _mask_dot(incl, la[:, :DK])
    wf = jnp.exp(bf[t - 1:t, :] - bf)
    wb = jnp.exp(_mask_dot(strict, la[:, DK:]))
    return xn, hc, kv[:, :DK], kv[:, DK:], r, la, wf, wb


def ctx_fwd(ctx, vecc, win, wa2, ba):
    def body(ctx_ref, vec_ref, win_ref, wa2_ref, ba_ref, s_ref):
        _, _, k, v, _, _, wf, wb = _ctx_common(ctx_ref, vec_ref, win_ref, wa2_ref, ba_ref)
        vb = v.astype(BF)
        for d, w in enumerate((wf, wb)):
            kd = (k * w).astype(BF)
            for h in range(NH):
                s_ref[d, h // 2, :, (h % 2) * HK:(h % 2 + 1) * HK] = _dot_tn(vb[:, h * HV:(h + 1) * HV], kd[:, h * HK:(h + 1) * HK])

    return pl.pallas_call(
        body, name="ctx_fwd", out_shape=jax.ShapeDtypeStruct((2, NP, HV, PW), F32),
        compiler_params=pltpu.CompilerParams(vmem_limit_bytes=VMEM_LIMIT),
    )(ctx, vecc, win, wa2, ba)


def ctx_bwd(ctx, vecc, win, wa2, ba, ds0):
    t = ctx.shape[0]

    def body(ctx_ref, vec_ref, win_ref, wa2_ref, ba_ref, ds_ref, dwin_ref, dwa2_ref, s_ref, dpc_ref):
        xn, hc, k, v, r, la, wf, wb = _ctx_common(ctx_ref, vec_ref, win_ref, wa2_ref, ba_ref)
        vb = v.astype(BF)
        strict = _tri(t, "lt").astype(BF)
        strict_t = _tri(t, "gt").astype(BF)
        dpc_ref[...] = jnp.zeros_like(dpc_ref)
        k0 = 2 * DC + DK
        dk = jnp.zeros((t, DK), F32)
        des = []
        for d, w in enumerate((wf, wb)):
            kd = (k * w).astype(BF)
            dkds = []
            for h in range(NH):
                dsb = ds_ref[d, h // 2, :, (h % 2) * HK:(h % 2 + 1) * HK].astype(BF)
                dkds.append(_dot(vb[:, h * HV:(h + 1) * HV], dsb))
                dvh = _dot_nt(kd[:, h * HK:(h + 1) * HK], dsb)
                vs = slice(k0 + DK + h * HV, k0 + DK + (h + 1) * HV)
                if d == 0:
                    dpc_ref[:, vs] = dvh.astype(BF)
                else:
                    dpc_ref[:, vs] = (dpc_ref[:, vs].astype(F32) + dvh).astype(BF)
            dkd = jnp.concatenate(dkds, axis=1)
            dk = dk + dkd * w
            des.append(dkd * k * w)
        dpc_ref[:, k0:k0 + DK] = dk.astype(BF)
        dla = jnp.concatenate([_mask_dot(strict, des[0]), _mask_dot(strict_t, des[1])], axis=1)
        dpre = dla * (1.0 - jnp.exp(TAU * la)) * (1.0 / TAU)
        dpreb = dpre.astype(BF)
        dwa2_ref[...] = _dot_tn(r, dpreb)
        dpc_ref[:, DINP - 128:] = _dot_nt(dpreb, wa2_ref[...]).astype(BF)
        dpc = dpc_ref[...]
        dwin_ref[...] = _dot_tn(hc, dpc)
        dhc = _dot_nt(dpc, win_ref[...])
        n1g, sc1 = vec_ref[0:1, :], vec_ref[2:3, :]
        tt = dhc * xn
        s_ref[...] = jnp.zeros_like(s_ref)
        s_ref[0:1, :] = jnp.sum(tt * (1.0 + sc1), axis=0, keepdims=True)
        s_ref[1:2, :] = jnp.sum(dhc, axis=0, keepdims=True)
        s_ref[2:3, :] = jnp.sum(tt * n1g, axis=0, keepdims=True)
        s_ref[3:4, DC:D] = jnp.sum(dpre, axis=0, keepdims=True)

    return pl.pallas_call(
        body, name="ctx_bwd",
        out_shape=[jax.ShapeDtypeStruct((D, DINP), F32), jax.ShapeDtypeStruct((128, 2 * DK), F32),
                   jax.ShapeDtypeStruct((8, D), F32)],
        scratch_shapes=[pltpu.VMEM((t, DINP), BF)],
        compiler_params=pltpu.CompilerParams(vmem_limit_bytes=VMEM_LIMIT),
    )(ctx, vecc, win, wa2, ba, ds0)


def _silu(x):
    return x * _sigmoid(x)


def mod_bwd(cext, dm, wm):
    def body(c_ref, d_ref, w_ref, gw_ref, ds_ref):
        dmb = d_ref[...].astype(BF)
        gw_ref[...] = _dot_tn(_silu(c_ref[...]).astype(BF), dmb)
        ds_ref[...] = _dot_nt(dmb, w_ref[...].astype(BF))

    return pl.pallas_call(body, name="mod_bwd",
                          out_shape=[jax.ShapeDtypeStruct(wm.shape, F32), jax.ShapeDtypeStruct(cext.shape, F32)],
                          compiler_params=pltpu.CompilerParams(vmem_limit_bytes=VMEM_LIMIT))(cext, dm, wm)


def pack_small(sf, s1, s2, sd, sc, dcw, dwa2, dwa2_c):
    def body(sf_ref, s1_ref, s2_ref, sd_ref, sc_ref, dcw_ref, dwa2_ref, dwa2c_ref, o_ref, ocw_ref, owa_ref):
        rsum = lambda ref, i: jnp.sum(ref[8 * i:8 * i + 8, :], axis=0, keepdims=True)
        o_ref[...] = jnp.zeros_like(o_ref)
        o_ref[0:1, :] = rsum(sd_ref, 0)
        o_ref[1:2, :] = rsum(sd_ref, 1)
        o_ref[2:3, :] = rsum(s1_ref, 0)
        o_ref[3:4, :] = rsum(sf_ref, 2)
        o_ref[4:5, :] = rsum(sf_ref, 3)
        o_ref[5:6, :] = rsum(sf_ref, 1)
        o_ref[6:7, :] = sc_ref[1:2, :]
        o_ref[7:8, :] = sc_ref[2:3, :]
        o_ref[8:9, :] = rsum(sd_ref, 2) + sc_ref[0:1, :]
        o_ref[9:10, :] = rsum(sf_ref, 4)
        o_ref[10:11, :] = rsum(sf_ref, 0)
        o_ref[11:12, :] = rsum(sd_ref, 3)
        o_ref[12:13, :] = rsum(sd_ref, 4) + sc_ref[3:4, :]
        g = jnp.sum(s2_ref[...], axis=0, keepdims=True)
        o_ref[13:14, 0:HV] = g[:, 0:HV] + g[:, HV:2 * HV] + g[:, 2 * HV:3 * HV] + g[:, 3 * HV:4 * HV]
        o_ref[14:15, :] = rsum(sf_ref, 5)
        ocw_ref[...] = dcw_ref[...]
        owa_ref[...] = dwa2_ref[0:32, :] + dwa2c_ref[0:32, :]

    return pl.pallas_call(body, name="pack_small",
                          out_shape=[jax.ShapeDtypeStruct((16, D), F32), jax.ShapeDtypeStruct((32, DC), F32),
                                     jax.ShapeDtypeStruct((32, 2 * DK), F32)])(sf, s1, s2, sd, sc, dcw, dwa2, dwa2_c)


def small_totals(g8):
    r = g8.shape[1]

    def body(g_ref, t_ref, bm_ref, loss_ref):
        acc = g_ref[0]
        for i in range(1, NDEV):
            acc = acc + g_ref[i]
        t_ref[...] = acc
        bm_ref[...] = jnp.zeros_like(bm_ref)
        bm_ref[0:6, :] = acc[0:6, :]
        bm_ref[0:2, :] += acc[6:8, :]
        loss_ref[...] = jnp.broadcast_to(jnp.sum(acc[14:15, :], axis=1, keepdims=True), loss_ref.shape)

    return pl.pallas_call(body, name="small_totals",
                          out_shape=[jax.ShapeDtypeStruct((r, D), F32), jax.ShapeDtypeStruct((8, D), F32),
                                     jax.ShapeDtypeStruct((8, 128), F32)])(g8)


def cctx_grad(p8, c_ctx_row):
    def body(p_ref, c_ref, o_ref):
        acc = p_ref[0, 0:1, :]
        for j in range(1, NCHIP):
            acc = acc + p_ref[2 * j, 0:1, :]
        cc = c_ref[0:1, :]
        sg = _sigmoid(cc)
        o_ref[...] = jnp.zeros_like(o_ref)
        o_ref[0:1, :] = acc * (sg * (1.0 + cc * (1.0 - sg)))

    return pl.pallas_call(body, name="cctx_grad", out_shape=jax.ShapeDtypeStruct((8, D), F32))(p8, c_ctx_row)


def adamw(w, g, m, v, rows, name, emit_grad=False):
    r, c = w.shape

    def body(w_ref, g_ref, m_ref, v_ref, d_ref, nm_ref, nv_ref, *go_ref):
        gg = g_ref[...]
        nm = ADAM_B1 * m_ref[...] + (1.0 - ADAM_B1) * gg
        nv = ADAM_B2 * v_ref[...] + (1.0 - ADAM_B2) * (gg * gg)
        m_hat = nm / (1.0 - ADAM_B1 ** ADAM_STEP)
        v_hat = nv / (1.0 - ADAM_B2 ** ADAM_STEP)
        d_ref[...] = -ADAM_LR * (m_hat / (jnp.sqrt(v_hat) + ADAM_EPS) + ADAM_WD * w_ref[...])
        nm_ref[...] = nm
        nv_ref[...] = nv
        if emit_grad:
            go_ref[0][...] = gg

    spec = pl.BlockSpec((rows, c), lambda i: (i, 0))
    sds = jax.ShapeDtypeStruct((r, c), F32)
    nout = 4 if emit_grad else 3
    return pl.pallas_call(
        body, grid=(r // rows,), name=name, in_specs=[spec] * 4, out_specs=[spec] * nout, out_shape=[sds] * nout,
        compiler_params=_cparams(("parallel",)),
    )(w, g, m, v)


def _me():
    return lax.axis_index("x"), lax.axis_index("y"), lax.axis_index("c")


def _flip(v, bit):
    return 1 - v if bit else v


ANY = pl.BlockSpec(memory_space=pl.ANY)


def _gather8(x_ref, o_ref, ssem, rsem, lsem):
    mx, my, mc = _me()
    me = 4 * mx + 2 * my + mc
    local = pltpu.make_async_copy(x_ref, o_ref.at[me], lsem)
    local.start()
    peer = lambda k: (_flip(mx, k & 4), _flip(my, k & 2), _flip(mc, k & 1))
    sends = []
    for k in range(1, NDEV):
        cp = pltpu.make_async_remote_copy(src_ref=x_ref, dst_ref=o_ref.at[me], send_sem=ssem.at[k - 1],
                                          recv_sem=rsem.at[k - 1], device_id=peer(k), device_id_type=MESH)
        cp.start()
        sends.append(cp)
    for k in range(1, NDEV):
        px, py, pc = peer(k)
        pltpu.make_async_remote_copy(src_ref=x_ref, dst_ref=o_ref.at[4 * px + 2 * py + pc], send_sem=ssem.at[k - 1],
                                     recv_sem=rsem.at[k - 1], device_id=(px, py, pc), device_id_type=MESH).wait_recv()
    for cp in sends:
        cp.wait_send()
    local.wait()


def _gather8_sems():
    return [pltpu.SemaphoreType.DMA((NDEV - 1,)), pltpu.SemaphoreType.DMA((NDEV - 1,)), pltpu.SemaphoreType.DMA]


def all_gather8(x, name):
    vm = pl.BlockSpec(memory_space=pltpu.VMEM)
    return pl.pallas_call(_gather8_body(), name=name, in_specs=[vm], out_specs=vm,
                          out_shape=jax.ShapeDtypeStruct((NDEV,) + x.shape, x.dtype), scratch_shapes=_gather8_sems())(x)


def _gather8_body():
    def body(x_ref, o_ref, ssem, rsem, lsem):
        _gather8(x_ref, o_ref, ssem, rsem, lsem)
    return body


def prologue(small, c_ctx_rows, wm, bm, w_in_shard):
    ex = ChipExchange("gather", [w_in_shard])

    def body(s_ref, cc_ref, w_ref, b_ref, win_ref, s8_ref, m8_ref, wing_ref, mloc_ref, *sems):
        ex.start([win_ref], [wing_ref], sems[6:])
        _gather8(s_ref, s8_ref, *sems[0:3])
        cext = jnp.concatenate([s8_ref[:, 0, :], cc_ref[...]], axis=0)
        mloc_ref[...] = _dot(_silu(cext).astype(BF), w_ref[...].astype(BF)) + b_ref[...]
        _gather8(mloc_ref, m8_ref, *sems[3:6])
        ex.wait([win_ref], [wing_ref], sems[6:])

    vm = pl.BlockSpec(memory_space=pltpu.VMEM)
    wcols = wm.shape[1]
    return pl.pallas_call(
        body, name="prologue", in_specs=[vm, vm, vm, vm, ANY], out_specs=[vm, vm, ANY],
        out_shape=[jax.ShapeDtypeStruct((NDEV, 16, D), F32), jax.ShapeDtypeStruct((NDEV, 16, wcols), F32)] + ex.out_shape,
        scratch_shapes=[pltpu.VMEM((16, wcols), F32)] + _gather8_sems() + _gather8_sems() + ex.scratch,
        compiler_params=pltpu.CompilerParams(vmem_limit_bytes=VMEM_LIMIT),
    )(small, c_ctx_rows, wm, bm, w_in_shard)


def _chip_peers(mx, my):
    out = []
    for p in range(1, NCHIP):
        px, py = _flip(mx, p & 2), _flip(my, p & 1)
        out.append((px, py, 2 * px + py))
    return out


class ChipExchange:
    def __init__(self, kind, arrays):
        self.kind = kind
        self.n = len(arrays)
        if kind == "gather":
            self.out_shape = [jax.ShapeDtypeStruct((NCHIP,) + a.shape, a.dtype) for a in arrays]
        else:
            self.out_shape = [jax.ShapeDtypeStruct(a.shape, a.dtype) for a in arrays]
        self.scratch = [pltpu.SemaphoreType.DMA((3 * self.n,)), pltpu.SemaphoreType.DMA((3 * self.n,)),
                        pltpu.SemaphoreType.DMA((self.n,))]

    def _copies(self, ins, outs, sems):
        ssem, rsem, lsem = sems
        mx, my, mc = _me()
        jme = 2 * mx + my
        gather = self.kind == "gather"
        local, sends, waits = [], [], []
        for k in range(self.n):
            local.append(pltpu.make_async_copy(ins[k] if gather else ins[k].at[jme], outs[k].at[jme], lsem.at[k]))
            for p, (px, py, jp) in enumerate(_chip_peers(mx, my)):
                src = ins[k] if gather else ins[k].at[jp]
                sem = dict(send_sem=ssem.at[3 * k + p], recv_sem=rsem.at[3 * k + p], device_id=(px, py, mc),
                           device_id_type=MESH)
                sends.append(pltpu.make_async_remote_copy(src_ref=src, dst_ref=outs[k].at[jme], **sem))
                waits.append(pltpu.make_async_remote_copy(src_ref=src, dst_ref=outs[k].at[jp], **sem))
        return local, sends, waits

    def start(self, ins, outs, sems):
        local, sends, _ = self._copies(ins, outs, sems)
        for cp in local + sends:
            cp.start()

    def wait(self, ins, outs, sems):
        local, _, waits = self._copies(ins, outs, sems)
        for cp in waits:
            cp.wait_recv()
        for cp in waits:
            cp.wait_send()
        for cp in local:
            cp.wait()


def chip_exchange(kind, arrays, name):
    ex = ChipExchange(kind, arrays)
    n = ex.n

    def body(*refs):
        ins, outs, sems = refs[:n], refs[n:2 * n], refs[2 * n:]
        ex.start(ins, outs, sems)
        ex.wait(ins, outs, sems)

    return pl.pallas_call(body, name=name, in_specs=[ANY] * n, out_specs=[ANY] * n, out_shape=ex.out_shape,
                          scratch_shapes=ex.scratch)(*arrays)


def sibling_add(g, ngrp, hr, tr, name):
    c_ = g.shape[1]
    nt = hr // tr

    def body(cidx, keep_ref, give_ref, o_ref, land, ssem, rsem):
        mx, my, mc = _me()
        t = pl.program_id(0) * nt + pl.program_id(1)
        s = t % 2
        cp = pltpu.make_async_remote_copy(src_ref=give_ref, dst_ref=land.at[s], send_sem=ssem.at[s], recv_sem=rsem.at[s],
                                          device_id=(mx, my, 1 - mc), device_id_type=MESH)
        cp.start()
        cp.wait_recv()
        o_ref[...] = keep_ref[...] + land[s]
        cp.wait_send()

    grid_spec = pltpu.PrefetchScalarGridSpec(
        num_scalar_prefetch=1, grid=(ngrp, nt),
        in_specs=[pl.BlockSpec((tr, c_), lambda i, j, cr: ((2 * i + cr[0]) * nt + j, 0)),
                  pl.BlockSpec((tr, c_), lambda i, j, cr: ((2 * i + 1 - cr[0]) * nt + j, 0))],
        out_specs=pl.BlockSpec((tr, c_), lambda i, j, cr: (i * nt + j, 0)),
        scratch_shapes=[pltpu.VMEM((2, tr, c_), F32), pltpu.SemaphoreType.DMA((2,)), pltpu.SemaphoreType.DMA((2,))])
    cidx = lax.axis_index("c").astype(jnp.int32).reshape(1)
    return pl.pallas_call(body, grid_spec=grid_spec, name=name, out_shape=jax.ShapeDtypeStruct((ngrp * hr, c_), F32),
                          compiler_params=_cparams(("arbitrary", "arbitrary")))(cidx, g, g)


def finish_grad(b, tr, name):
    _, r2, c_ = b.shape

    def body(b_ref, g_ref, mine, land, ssem, rsem):
        mx, my, mc = _me()
        t = pl.program_id(0)
        s = t % 2
        mine[s] = (b_ref[0].astype(F32) + b_ref[1].astype(F32)) + (b_ref[2].astype(F32) + b_ref[3].astype(F32))
        cp = pltpu.make_async_remote_copy(src_ref=mine.at[s], dst_ref=land.at[s], send_sem=ssem.at[s], recv_sem=rsem.at[s],
                                          device_id=(mx, my, 1 - mc), device_id_type=MESH)
        cp.start()
        cp.wait_recv()
        g_ref[mc] = mine[s]
        g_ref[1 - mc] = land[s]
        cp.wait_send()

    return pl.pallas_call(
        body, grid=(r2 // tr,), name=name,
        in_specs=[pl.BlockSpec((NCHIP, tr, c_), lambda i: (0, i, 0))],
        out_specs=pl.BlockSpec((2, tr, c_), lambda i: (0, i, 0)), out_shape=jax.ShapeDtypeStruct((2, r2, c_), F32),
        scratch_shapes=[pltpu.VMEM((2, tr, c_), F32), pltpu.VMEM((2, tr, c_), F32), pltpu.SemaphoreType.DMA((2,)),
                        pltpu.SemaphoreType.DMA((2,))],
        compiler_params=_cparams(("arbitrary",)))(b)


TM_IN = 512
TM_GLA = 512
TM_MERGE = 512
TM_FFN = 256
TN_WGRAD = 2048

WEIGHTS = ['c_ctx', 'w_mod', 'b_mod', 'norm1_g', 'norm2_g', 'w_in', 'conv_w', 'conv_b', 'conv_ln_g', 'conv_ln_b', 'w_a2_f',
           'b_a_f', 'w_a2_b', 'b_a_b', 'gla_norm_g', 'w_out', 'w_gate', 'w_up', 'w_down', 'final_g']
BIG = ['w_in', 'w_out', 'w_gate', 'w_up', 'w_down']


def _rows(*vs):
    w = vs[0].size
    row = lax.broadcasted_iota(jnp.int32, (8, w), 0)
    out = jnp.zeros((8, w), F32)
    for i, v in enumerate(vs):
        out = jnp.where(row == i, v.reshape(1, w), out)
    return out


def _small_slab(p):
    cat = lambda *ks: jnp.concatenate([p[k].reshape(-1) for k in ks])
    vecs = _rows(p['c_ctx'], p['norm1_g'], p['norm2_g'], p['final_g'], cat('conv_b', 'conv_ln_g'),
                 cat('conv_ln_b', 'b_a_f', 'b_a_b'), jnp.pad(p['gla_norm_g'].reshape(-1), (0, D - HV)))
    bmod = jnp.pad(p['b_mod'].reshape(6, D), ((0, 2), (0, 0)))
    shards = jnp.pad(jnp.concatenate([jnp.pad(p['conv_w'].reshape(-1), (0, DC // NCHIP)), cat('w_a2_f', 'w_a2_b')]),
                     (0, 2 * D)).reshape(8, D)
    return jnp.concatenate([vecs, bmod, shards], axis=0)


def _unslab(s):
    return {
        'c_ctx': s[0], 'norm1_g': s[1:2], 'norm2_g': s[2:3], 'final_g': s[3],
        'conv_b': s[4:5, :DC], 'conv_ln_g': s[4:5, DC:], 'conv_ln_b': s[5:6, :DC],
        'b_a_f': s[5:6, DC:DC + DK], 'b_a_b': s[5:6, DC + DK:], 'gla_norm_g': s[6:7, :HV],
        'b_mod': s[8:14].reshape(1, 6 * D),
        'conv_w': s[16:20].reshape(32, DC // NCHIP)[:CW].reshape(1, CW, DC // NCHIP),
        'w_a2_f': s[20].reshape(1, RANK, DK // NCHIP), 'w_a2_b': s[21].reshape(1, RANK, DK // NCHIP),
    }


def kernel(x, c, ctx, c_ctx, w_mod, b_mod, norm1_g, norm2_g, w_in, conv_w, conv_b, conv_ln_g, conv_ln_b, w_a2_f, b_a_f, w_a2_b, b_a_b, gla_norm_g, w_out, w_gate, w_up, w_down, final_g, loss_target, m_c_ctx, m_w_mod, m_b_mod, m_norm1_g, m_norm2_g, m_w_in, m_conv_w, m_conv_b, m_conv_ln_g, m_conv_ln_b, m_w_a2_f, m_b_a_f, m_w_a2_b, m_b_a_b, m_gla_norm_g, m_w_out, m_w_gate, m_w_up, m_w_down, m_final_g, v_c_ctx, v_w_mod, v_b_mod, v_norm1_g, v_norm2_g, v_w_in, v_conv_w, v_conv_b, v_conv_ln_g, v_conv_ln_b, v_w_a2_f, v_b_a_f, v_w_a2_b, v_b_a_b, v_gla_norm_g, v_w_out, v_w_gate, v_w_up, v_w_down, v_final_g):
    w = dict(c_ctx=c_ctx, w_mod=w_mod, b_mod=b_mod, norm1_g=norm1_g, norm2_g=norm2_g, w_in=w_in, conv_w=conv_w, conv_b=conv_b,
             conv_ln_g=conv_ln_g, conv_ln_b=conv_ln_b, w_a2_f=w_a2_f, b_a_f=b_a_f, w_a2_b=w_a2_b, b_a_b=b_a_b,
             gla_norm_g=gla_norm_g, w_out=w_out, w_gate=w_gate, w_up=w_up, w_down=w_down, final_g=final_g)
    m = dict(c_ctx=m_c_ctx, w_mod=m_w_mod, b_mod=m_b_mod, norm1_g=m_norm1_g, norm2_g=m_norm2_g, w_in=m_w_in, conv_w=m_conv_w,
             conv_b=m_conv_b, conv_ln_g=m_conv_ln_g, conv_ln_b=m_conv_ln_b, w_a2_f=m_w_a2_f, b_a_f=m_b_a_f, w_a2_b=m_w_a2_b,
             b_a_b=m_b_a_b, gla_norm_g=m_gla_norm_g, w_out=m_w_out, w_gate=m_w_gate, w_up=m_w_up, w_down=m_w_down,
             final_g=m_final_g)
    v = dict(c_ctx=v_c_ctx, w_mod=v_w_mod, b_mod=v_b_mod, norm1_g=v_norm1_g, norm2_g=v_norm2_g, w_in=v_w_in, conv_w=v_conv_w,
             conv_b=v_conv_b, conv_ln_g=v_conv_ln_g, conv_ln_b=v_conv_ln_b, w_a2_f=v_w_a2_f, b_a_f=v_b_a_f, w_a2_b=v_w_a2_b,
             b_a_b=v_b_a_b, gla_norm_g=v_gla_norm_g, w_out=v_w_out, w_gate=v_w_gate, w_up=v_w_up, w_down=v_w_down,
             final_g=v_final_g)
    mx, my, mc = _me()
    jme = 2 * mx + my
    me = 4 * mx + 2 * my + mc
    wmc = D * 6 // NCHIP
    xx, tgt, cx = x[0], loss_target[0], ctx[0]

    bshard = [w[k][0].astype(BF) for k in BIG]
    sw = jnp.concatenate([jnp.pad(conv_w[0], ((0, 1), (0, 0))).reshape(-1), w_a2_f[0].reshape(-1), w_a2_b[0].reshape(-1)])
    small = jnp.concatenate([_rows(c[0]), jnp.pad(sw.reshape(6, D), ((0, 2), (0, 0)))], axis=0)
    cs8, mall, win_g = prologue(small, _rows(c_ctx), w_mod[0], lax.dynamic_slice_in_dim(b_mod, jme * wmc, wmc, axis=1),
                                bshard[0])
    cext = jnp.concatenate([cs8[:, 0, :], _rows(c_ctx)], axis=0)
    swc = jnp.stack([cs8[2 * j, 8:16] for j in range(NCHIP)]).reshape(NCHIP, 8 * D)
    convw = jnp.transpose(swc[:, :32 * 128].reshape(NCHIP, 32, 128), (1, 0, 2)).reshape(32, DC)
    a2 = lambda o: jnp.transpose(swc[:, o:o + RANK * 64].reshape(NCHIP, RANK, 64), (1, 0, 2)).reshape(RANK, DK)
    wa2 = jnp.zeros((128, 2 * DK), F32).at[0:RANK, 0:DK].set(a2(32 * 128)).at[RANK:2 * RANK, DK:].set(a2(32 * 128 + RANK * 64))
    wa2 = wa2.astype(BF)
    mall = jnp.concatenate([mall[2 * j] for j in range(NCHIP)], axis=1)
    sh1, sc1, g1, sh2, sc2, g2 = jnp.split(lax.dynamic_slice_in_dim(mall, me, 1, axis=0)[0], 6)
    csh1, csc1 = mall[8, :D], mall[8, D:2 * D]
    cols = lambda a: jnp.transpose(a, (1, 0, 2)).reshape(a.shape[1], -1)
    win = jnp.pad(cols(win_g), ((0, 0), (0, DINP - DIN)))
    ba = jnp.concatenate([b_a_f, b_a_b], axis=1)
    cvec = _rows(conv_b, conv_ln_g, conv_ln_b)
    vec1 = _rows(norm1_g, sh1, sc1)
    vecc = _rows(norm1_g, csh1, csc1)
    vecm = _rows(g1)
    vecf = _rows(norm2_g, sh2, sc2, g2, final_g)
    gn = jnp.tile(gla_norm_g, (1, NH))

    s0 = ctx_fwd(cx, vecc, win, wa2, ba)
    res = fwd_in(xx, vec1, win, convw, cvec, wa2, ba, TM_IN, ChipExchange("gather", bshard[1:]), bshard[1:])
    ag, yb, co, qk, vv, gg, la, r = res[:8]
    wout = res[8].reshape(D, D)
    wg, wu = cols(res[9]), cols(res[10])
    wd = res[11].reshape(DFF, D)
    o_f, o_b, se_f, se_b = gla_fwd(qk, vv, la, s0, TM_GLA)
    x1, y1, cat = merge_fwd(xx, o_f, o_b, gg, co, vecm, gn, wout, TM_MERGE)

    dx1, h2, act, dgt, dup, dy2, sf = ffn_fwd_bwd(x1, tgt, vecf, wg, wu, wd, TM_FFN)
    d_wg = wgrad(h2, dgt, None, D // 2, DFF, TN_WGRAD, "wgrad_gate")
    d_wu = wgrad(h2, dup, None, D // 2, DFF, TN_WGRAD, "wgrad_up")
    d_wd = wgrad(act, dy2, None, DFF // 2, D, TN_WGRAD, "wgrad_down")
    dy1, dco, do, dg, s1, s2 = merge_bwd(dx1, y1, o_f, o_b, gg, vecm, gn, wout, TM_MERGE)
    d_wout = wgrad(cat, dy1, None, D, D, TN_WGRAD, "wgrad_out")

    shard = lambda a, k: jnp.transpose(a.reshape(a.shape[0], NCHIP, k), (1, 0, 2))
    hd = D // 2
    parts = [sibling_add(d_wout, NCHIP, hd // NCHIP, hd // NCHIP, "xadd_w_out").reshape(NCHIP, hd // NCHIP, D),
             shard(sibling_add(d_wg, 1, hd, hd // 2, "xadd_w_gate"), DFF // NCHIP),
             shard(sibling_add(d_wu, 1, hd, hd // 2, "xadd_w_up"), DFF // NCHIP),
             sibling_add(d_wd, NCHIP, DFF // 8, DFF // 8, "xadd_w_down").reshape(NCHIP, DFF // 8, D)]
    parts = [p.astype(BF) for p in parts]
    res = gla_bwd(qk, vv, la, do, se_f, se_b, TM_GLA, ChipExchange("scatter", parts), parts)
    dqk_f, dv_f, dla_f, dqk_b, dv_b, dla_b, ds0 = res[:7]
    recv = list(res[7:])
    dwin_c, dwa2_c, sc = ctx_bwd(cx, vecc, win, wa2, ba, ds0)
    grad_x, h, dp, dwa2, dcw, sd = bwd_in(xx, dx1, ag, yb, dco, dqk_f, dqk_b, dv_f, dv_b, dg, dla_f, dla_b, la, r,
                                          vec1, win, convw, cvec, wa2, TM_IN)
    d_win = wgrad(h, dp, dwin_c, D // 2, DINP, TN_WGRAD, "wgrad_in")
    part_in = shard(sibling_add(d_win, 1, hd, hd // 2, "xadd_w_in")[:, :DIN], DIN // NCHIP).astype(BF)
    recv = list(chip_exchange("scatter", [part_in], "scatter_w_in")) + recv

    rows16, dcw_t, dwa2_t = pack_small(sf, s1, s2, sd, sc, dcw, dwa2, dwa2_c)
    sp = jnp.concatenate([rows16, dcw_t.reshape(16, D), dwa2_t.reshape(16, D)], axis=0)
    g8 = all_gather8(sp, "gather_small_grads")
    tot, bm_g, loss8 = small_totals(g8)
    loss = loss8[0, 0]
    dmod8 = g8[:, 0:6, :].reshape(NDEV, 6 * D)
    dmodc = jnp.concatenate([tot[6], tot[7], jnp.zeros((4 * D,), F32)])
    dm = jnp.concatenate([dmod8, _rows(dmodc)], axis=0)
    dm = lax.dynamic_slice_in_dim(dm, jme * wmc, wmc, axis=1)
    g_wmod, dsil = mod_bwd(cext, dm, w_mod[0])
    p8 = all_gather8(dsil[8:16], "gather_dsilu")
    g_cctx = cctx_grad(p8, _rows(c_ctx))[0]

    grads, delta, new_m, new_v = {}, {}, {}, {}
    for i, k in enumerate(BIG):
        r2 = recv[i].shape[1]
        gk = finish_grad(recv[i], r2 // 2 if r2 >= 512 else r2, "finish_" + k).reshape(w[k].shape[1:])
        rk = gk.shape[0]
        outs = adamw(w[k][0], gk, m[k][0], v[k][0], rk // 2 if rk >= 512 else rk, "adamw_" + k, emit_grad=True)
        delta[k], new_m[k], new_v[k], grads[k] = (o[None] for o in outs)
    grads['w_mod'] = g_wmod[None]
    d_, m_, v_ = adamw(w_mod[0], g_wmod, m_w_mod[0], v_w_mod[0], 256, "adamw_w_mod")
    delta['w_mod'], new_m['w_mod'], new_v['w_mod'] = d_[None], m_[None], v_[None]
    small_g = {
        'c_ctx': g_cctx, 'b_mod': bm_g[0:6].reshape(1, 6 * D), 'norm1_g': tot[8:9], 'norm2_g': tot[9:10], 'final_g': tot[10],
        'conv_b': tot[11:12, :DC], 'conv_ln_g': tot[11:12, DC:], 'conv_ln_b': tot[12:13, :DC],
        'b_a_f': tot[12:13, DC:DC + DK], 'b_a_b': tot[12:13, DC + DK:], 'gla_norm_g': tot[13:14, :HV],
        'conv_w': lax.dynamic_slice_in_dim(tot[16:32].reshape(32, DC)[:CW], jme * (DC // NCHIP), DC // NCHIP, axis=1)[None],
        'w_a2_f': lax.dynamic_slice_in_dim(tot[32:48].reshape(32, 2 * DK)[0:RANK, 0:DK], jme * (DK // NCHIP), DK // NCHIP, axis=1)[None],
        'w_a2_b': lax.dynamic_slice_in_dim(tot[32:48].reshape(32, 2 * DK)[RANK:2 * RANK, DK:], jme * (DK // NCHIP), DK // NCHIP, axis=1)[None],
    }
    grads.update(small_g)
    sd_, sm_, sv_ = adamw(_small_slab(w), _small_slab(small_g), _small_slab(m), _small_slab(v), 24,
                          "adamw_small")
    for dst, slab in ((delta, sd_), (new_m, sm_), (new_v, sv_)):
        dst.update(_unslab(slab))
    out = [loss, grad_x[None]]
    for group in (grads, delta, new_m, new_v):
        out += [group[k].reshape(w[k].shape) for k in WEIGHTS]
    return tuple(out)
```

```python
import jax
import jax.numpy as jnp
from jax import lax
from jax.experimental import pallas as pl
from jax.experimental.pallas import tpu as pltpu

F32 = jnp.float32
BF = jnp.bfloat16

D = 1024
DC = 512
NH = 4
HK = 64
HV = 128
DK = NH * HK
DV = NH * HV
RANK = 16
CH = 64
GW = 64
CW = 31
SEGP = GW + 32
DFF = 2816
DIN = 2592
DINP = 2688
EPS = 1e-6
TAU = 16.0
QSCALE = HK ** -0.5
NCHIP = 4
NDEV = 8

ADAM_LR = 0.001
ADAM_B1 = 0.9
ADAM_B2 = 0.999
ADAM_EPS = 1e-08
ADAM_WD = 0.01
ADAM_STEP = 10

VMEM_LIMIT = 56 * 1024 * 1024
MESH = pl.DeviceIdType.MESH


def _dot(a, b):
    return jnp.dot(a, b, preferred_element_type=F32)


def _dot_nt(a, b):
    return lax.dot_general(a, b, (((1,), (1,)), ((), ())), preferred_element_type=F32)


def _dot_tn(a, b):
    return lax.dot_general(a, b, (((0,), (0,)), ((), ())), preferred_element_type=F32)


def _mask_dot(t, x):
    hi = x.astype(BF)
    lo = (x - hi.astype(F32)).astype(BF)
    return _dot(t, hi) + _dot(t, lo)


def _sigmoid(x):
    return 1.0 / (1.0 + jnp.exp(-x))


def _log_sigmoid(x):
    return jnp.minimum(x, 0.0) - jnp.log(1.0 + jnp.exp(-jnp.abs(x)))


def _colsum8(z):
    t, c = z.shape
    return jnp.sum(z.reshape(t // 8, 8, c), axis=0)


def _tri(n, kind):
    r = lax.broadcasted_iota(jnp.int32, (n, n), 0)
    c = lax.broadcasted_iota(jnp.int32, (n, n), 1)
    m = {"le": c <= r, "lt": c < r, "ge": c >= r, "gt": c > r}[kind]
    return m


def _full(shape):
    nd = len(shape)
    return pl.BlockSpec(shape, lambda *_: (0,) * nd)


def _cparams(sem, vmem=VMEM_LIMIT):
    return pltpu.CompilerParams(dimension_semantics=sem, vmem_limit_bytes=vmem)


def _call(body, grid, name, in_specs, out_specs, out_shape, scratch, operands, exchange=None, carried=()):
    n_in, n_out, n_scr = len(in_specs), len(out_specs), len(scratch)
    if exchange is None:
        fn = body
    else:
        n = exchange.n

        def fn(*refs):
            ins, cin = refs[:n_in], refs[n_in:n_in + n]
            outs, cout = refs[n_in + n:n_in + n + n_out], refs[n_in + n + n_out:n_in + 2 * n + n_out]
            rest = refs[n_in + 2 * n + n_out:]
            scr, sems = rest[:n_scr], rest[n_scr:]

            @pl.when(pl.program_id(0) == 0)
            def _():
                exchange.start(cin, cout, sems)

            body(*ins, *outs, *scr)

            @pl.when(pl.program_id(0) == pl.num_programs(0) - 1)
            def _():
                exchange.wait(cin, cout, sems)

        any_spec = pl.BlockSpec(memory_space=pl.ANY)
        in_specs = list(in_specs) + [any_spec] * n
        out_specs = list(out_specs) + [any_spec] * n
        out_shape = list(out_shape) + exchange.out_shape
        scratch = list(scratch) + exchange.scratch
    return pl.pallas_call(fn, grid=grid, name=name, in_specs=in_specs, out_specs=out_specs, out_shape=out_shape,
                          scratch_shapes=scratch, compiler_params=_cparams(("arbitrary",)))(*operands, *carried)


def _fill_padded(pad_ref, val, nseg):
    zeros = jnp.zeros((nseg, 16, val.shape[-1]), F32)
    pad_ref[:, 0:16, :] = zeros
    pad_ref[:, 16 + GW:SEGP, :] = zeros
    pad_ref[:, 16:16 + GW, :] = val.reshape(nseg, GW, val.shape[-1])


def _tap_slabs(pad_ref, s, cs):
    whole = pad_ref[s, :, cs]
    for r in range(8):
        slab = whole if r == 0 else pltpu.roll(whole, SEGP - r, axis=0)
        for a in range(4):
            j = r + 8 * a - 1
            if 0 <= j < CW:
                yield j, slab[8 * a:8 * a + GW]


def _conv_taps(pad_ref, s, w_ref, c0, cw, flip):
    acc = jnp.zeros((GW, cw), F32)
    for j, rows in _tap_slabs(pad_ref, s, pl.ds(c0, cw)):
        acc = acc + w_ref[pl.ds((CW - 1 - j) if flip else j, 1), pl.ds(c0, cw)] * rows
    return acc


def _ln_stats(yb):
    mu = jnp.mean(yb, axis=-1, keepdims=True)
    yc = yb - mu
    var = jnp.mean(yc * yc, axis=-1, keepdims=True)
    rs = lax.rsqrt(var + EPS)
    return yc * rs, rs


def fwd_in(x, vec1, win, convw, cvec, wa2, ba, tm, exchange=None, carried=()):
    n = x.shape[0]
    nseg = tm // GW
    cg = 128

    def body(x_ref, vec_ref, win_ref, cw_ref, cv_ref, wa2_ref, ba_ref,
             ag_ref, yb_ref, co_ref, qk_ref, v_ref, g_ref, la_ref, r_ref, pad_ref):
        xx = x_ref[...]
        rstd = lax.rsqrt(jnp.mean(xx * xx, axis=-1, keepdims=True) + EPS)
        h = ((xx * rstd * vec_ref[0:1, :]) * (1.0 + vec_ref[2:3, :]) + vec_ref[1:2, :]).astype(BF)
        pc = _dot(h, win_ref[:, :2 * DC])
        ag_ref[...] = pc.astype(BF)
        _fill_padded(pad_ref, pc[:, :DC] * _sigmoid(pc[:, DC:]), nseg)
        for s in range(nseg):
            for c0 in range(0, DC, cg):
                y = _conv_taps(pad_ref, s, cw_ref, c0, cg, False)
                yb_ref[pl.ds(s * GW, GW), pl.ds(c0, cg)] = y + cv_ref[0:1, c0:c0 + cg]
        p = _dot(h, win_ref[:, 2 * DC:])
        qk_ref[...] = p[:, :2 * DK].astype(BF)
        v_ref[...] = p[:, 2 * DK:2 * DK + DV].astype(BF)
        g_ref[...] = p[:, 2 * DK + DV:2 * DK + 2 * DV].astype(BF)
        r = p[:, 2 * DK + 2 * DV:].astype(BF)
        r_ref[...] = r
        la_ref[...] = _log_sigmoid(_dot(r, wa2_ref[...]) + ba_ref[...]) * (1.0 / TAU)
        yn, _ = _ln_stats(yb_ref[...])
        ln = yn * cv_ref[1:2, :] + cv_ref[2:3, :]
        co_ref[...] = (ln * _sigmoid(ln)).astype(BF)

    tok = lambda w: pl.BlockSpec((tm, w), lambda i: (i, 0))
    return _call(
        body, (n // tm,), "fwd_in",
        [tok(D), _full(vec1.shape), _full(win.shape), _full(convw.shape), _full(cvec.shape), _full(wa2.shape), _full(ba.shape)],
        [tok(2 * DC), tok(DC), tok(DC), tok(2 * DK), tok(DV), tok(DV), tok(2 * DK), tok(128)],
        [jax.ShapeDtypeStruct((n, 2 * DC), BF), jax.ShapeDtypeStruct((n, DC), F32),
         jax.ShapeDtypeStruct((n, DC), BF), jax.ShapeDtypeStruct((n, 2 * DK), BF),
         jax.ShapeDtypeStruct((n, DV), BF), jax.ShapeDtypeStruct((n, DV), BF),
         jax.ShapeDtypeStruct((n, 2 * DK), F32), jax.ShapeDtypeStruct((n, 128), BF)],
        [pltpu.VMEM((nseg, SEGP, DC), F32)],
        (x, vec1, win, convw, cvec, wa2, ba), exchange, carried)


def _gla_dir(d):
    return (_tri(CH, "le"), CH - 1) if d == 0 else (_tri(CH, "ge"), 0)


def _gla_chunk_terms(qk, la, d):
    seen, last = _gla_dir(d)
    b = _mask_dot(seen.astype(BF), la)
    bl = b[last:last + 1, :]
    eb = jnp.exp(b)
    enb = jnp.exp(-b)
    ekd = jnp.exp(bl - b)
    ebl = jnp.exp(bl)
    q = qk[:, :DK].astype(F32) * QSCALE
    k = qk[:, DK:].astype(F32)
    return eb, enb, ekd, ebl, q * eb, k * enb, k * ekd


NP = NH // 2
PW = 2 * HK


def _lo_lanes(shape):
    return lax.broadcasted_iota(jnp.int32, shape, len(shape) - 1) < HK


def _pair_sel(lo, hi):
    return jnp.where(_lo_lanes(lo.shape), lo, hi)


def _only(x, which):
    keep = _lo_lanes(x.shape) if which == 0 else jnp.logical_not(_lo_lanes(x.shape))
    return jnp.where(keep, x, jnp.zeros_like(x))


def gla_fwd(qk, v, la, s0, tm):
    n = qk.shape[0]
    nt = n // tm
    nc = tm // CH

    def body(qkf_ref, vf_ref, laf_ref, qkb_ref, vb_ref, lab_ref, s0_ref, of_ref, ob_ref, sef_ref, seb_ref, st_ref):
        @pl.when(pl.program_id(0) == 0)
        def _():
            st_ref[...] = s0_ref[...]

        def chunk(ci, carry):
            t = []
            for d, (qk_ref, v_ref, la_ref) in enumerate(((qkf_ref, vf_ref, laf_ref), (qkb_ref, vb_ref, lab_ref))):
                c = ci if d == 0 else nc - 1 - ci
                rows = pl.ds(pl.multiple_of(c * CH, CH), CH)
                eb, enb, ekd, ebl, qt, kt, kd = _gla_chunk_terms(qk_ref[rows, :], la_ref[rows, :], d)
                t.append(dict(c=c, rows=rows, ebl=ebl, qt=qt.astype(BF), kt=kt.astype(BF), kd=kd.astype(BF),
                              vv=v_ref[rows, :], st=[st_ref[d, p] for p in range(NP)], amask=_gla_dir(d)[0]))
            dh = [(d, h) for d in range(2) for h in range(NH)]
            dp = [(d, p) for d in range(2) for p in range(NP)]
            ps = lambda h: slice((h // 2) * PW, (h // 2 + 1) * PW)
            vs = lambda h: slice(h * HV, (h + 1) * HV)
            v2 = lambda p: slice(2 * p * HV, 2 * (p + 1) * HV)
            qm = {(d, h): _only(t[d]['qt'][:, ps(h)], h % 2) for d, h in dh}
            q2 = {(d, p): jnp.concatenate([qm[d, 2 * p], qm[d, 2 * p + 1]], axis=0) for d, p in dp}
            a2 = {(d, p): _dot_nt(q2[d, p], t[d]['kt'][:, p * PW:(p + 1) * PW]) for d, p in dp}
            a = {(d, h): jnp.where(t[d]['amask'], a2[d, h // 2][(h % 2) * CH:(h % 2 + 1) * CH], 0.0).astype(BF) for d, h in dh}
            oi = {(d, p): _dot_nt(q2[d, p], t[d]['st'][p].astype(BF)) for d, p in dp}
            o = {(d, h): _dot(a[d, h], t[d]['vv'][:, vs(h)]) + oi[d, h // 2][(h % 2) * CH:(h % 2 + 1) * CH] for d, h in dh}
            kv = {(d, p): _dot_tn(t[d]['vv'][:, v2(p)], t[d]['kd'][:, p * PW:(p + 1) * PW]) for d, p in dp}
            for d, (o_ref, se_ref) in enumerate(((of_ref, sef_ref), (ob_ref, seb_ref))):
                for h in range(NH):
                    o_ref[t[d]['rows'], vs(h)] = o[d, h].astype(BF)
                for p in range(NP):
                    se_ref[t[d]['c'], p] = t[d]['st'][p]
                    st_ref[d, p] = (t[d]['ebl'][:, p * PW:(p + 1) * PW] * t[d]['st'][p]
                                    + _pair_sel(kv[d, p][:HV], kv[d, p][HV:]))
            return carry

        lax.fori_loop(0, nc, chunk, 0, unroll=4)

    fw = lambda w, col=0: pl.BlockSpec((tm, w), lambda i: (i, col))
    bw = lambda w, col=0: pl.BlockSpec((tm, w), lambda i: (nt - 1 - i, col))
    se_f = pl.BlockSpec((nc, NP, HV, PW), lambda i: (i, 0, 0, 0))
    se_b = pl.BlockSpec((nc, NP, HV, PW), lambda i: (nt - 1 - i, 0, 0, 0))
    se_shape = jax.ShapeDtypeStruct((n // CH, NP, HV, PW), F32)
    return pl.pallas_call(
        body, grid=(nt,), name="gla_fwd",
        in_specs=[fw(2 * DK), fw(DV), fw(DK, 0), bw(2 * DK), bw(DV), bw(DK, 1), _full(s0.shape)],
        out_specs=[fw(DV), bw(DV), se_f, se_b],
        out_shape=[jax.ShapeDtypeStruct((n, DV), BF), jax.ShapeDtypeStruct((n, DV), BF), se_shape, se_shape],
        scratch_shapes=[pltpu.VMEM((2, NP, HV, PW), F32)],
        compiler_params=_cparams(("arbitrary",)),
    )(qk, v, la, qk, v, la, s0)


def gla_bwd(qk, v, la, do, se_f, se_b, tm, exchange=None, carried=()):
    n = qk.shape[0]
    nt = n // tm
    nc = tm // CH

    def body(qkf_ref, vf_ref, laf_ref, dof_ref, sef_ref, qkb_ref, vb_ref, lab_ref, dob_ref, seb_ref,
             dqkf_ref, dvf_ref, dlaf_ref, dqkb_ref, dvb_ref, dlab_ref, ds0_ref, ds_ref):
        @pl.when(pl.program_id(0) == 0)
        def _():
            ds_ref[...] = jnp.zeros_like(ds_ref)

        def chunk(ci, carry):
            t = []
            for d, (qk_ref, v_ref, la_ref, do_ref, se_ref) in enumerate(
                    ((qkf_ref, vf_ref, laf_ref, dof_ref, sef_ref), (qkb_ref, vb_ref, lab_ref, dob_ref, seb_ref))):
                c = nc - 1 - ci if d == 0 else ci
                rows = pl.ds(pl.multiple_of(c * CH, CH), CH)
                amask, last = _gla_dir(d)
                eb, enb, ekd, ebl, qt, kt, kd = _gla_chunk_terms(qk_ref[rows, :], la_ref[rows, :], d)
                t.append(dict(rows=rows, amask=amask, last=last, eb=eb, enb=enb, ekd=ekd, ebl=ebl, qt=qt, kt=kt, kd=kd,
                              qtb=qt.astype(BF), ktb=kt.astype(BF), kdb=kd.astype(BF), vv=v_ref[rows, :], dd=do_ref[rows, :],
                              st=[se_ref[c, p] for p in range(NP)], dsn=[ds_ref[d, p] for p in range(NP)]))
            dh = [(d, h) for d in range(2) for h in range(NH)]
            dp = [(d, p) for d in range(2) for p in range(NP)]
            ps = lambda h: slice((h // 2) * PW, (h // 2 + 1) * PW)
            vs = lambda h: slice(h * HV, (h + 1) * HV)
            stb = {(d, p): t[d]['st'][p].astype(BF) for d, p in dp}
            dsnb = {(d, p): t[d]['dsn'][p].astype(BF) for d, p in dp}
            qm = {(d, h): _only(t[d]['qtb'][:, ps(h)], h % 2) for d, h in dh}
            km = {(d, h): _only(t[d]['kdb'][:, ps(h)], h % 2) for d, h in dh}
            a2 = {(d, p): _dot_nt(jnp.concatenate([qm[d, 2 * p], qm[d, 2 * p + 1]], axis=0), t[d]['ktb'][:, p * PW:(p + 1) * PW])
                  for d, p in dp}
            a = {(d, h): jnp.where(t[d]['amask'], a2[d, h // 2][(h % 2) * CH:(h % 2 + 1) * CH], 0.0).astype(BF) for d, h in dh}
            da = {(d, h): jnp.where(t[d]['amask'], _dot_nt(t[d]['dd'][:, vs(h)], t[d]['vv'][:, vs(h)]), 0.0).astype(BF)
                  for d, h in dh}
            v2 = lambda p: slice(2 * p * HV, 2 * (p + 1) * HV)
            rows2 = lambda x, d, p: jnp.concatenate([x[d, 2 * p], x[d, 2 * p + 1]], axis=0)
            half = lambda x, h: x[(h % 2) * CH:(h % 2 + 1) * CH]
            dvs = {(d, p): _dot_nt(rows2(km, d, p), dsnb[d, p]) for d, p in dp}
            dv = {(d, h): _dot_tn(a[d, h], t[d]['dd'][:, vs(h)]) + half(dvs[d, h // 2], h) for d, h in dh}
            vrows = lambda d, p: jnp.concatenate([t[d]['vv'][:, vs(2 * p)], t[d]['vv'][:, vs(2 * p + 1)]], axis=0)
            drows = lambda d, p: jnp.concatenate([t[d]['dd'][:, vs(2 * p)], t[d]['dd'][:, vs(2 * p + 1)]], axis=0)
            dkd2 = {(d, p): _dot(vrows(d, p), dsnb[d, p]) for d, p in dp}
            dqt2 = {(d, p): _dot(rows2(da, d, p), t[d]['ktb'][:, p * PW:(p + 1) * PW]) + _dot(drows(d, p), stb[d, p])
                    for d, p in dp}
            dkt2 = {(d, p): _dot_tn(jnp.concatenate([da[d, 2 * p], da[d, 2 * p + 1]], axis=1), t[d]['qtb'][:, p * PW:(p + 1) * PW])
                    for d, p in dp}
            dsq2 = {(d, p): _dot_tn(t[d]['dd'][:, v2(p)], t[d]['qtb'][:, p * PW:(p + 1) * PW]) for d, p in dp}
            two = lambda x, d, p, n: _pair_sel(x[d, p][:n], x[d, p][n:])
            for d, (dqk_ref, dv_ref, dla_ref) in enumerate(((dqkf_ref, dvf_ref, dlaf_ref), (dqkb_ref, dvb_ref, dlab_ref))):
                td = t[d]
                rows = td['rows']
                for h in range(NH):
                    dv_ref[rows, vs(h)] = dv[d, h].astype(BF)
                pair = lambda x: jnp.concatenate([two(x, d, p, CH) for p in range(NP)], axis=1)
                dqt_, dkt_, dkd_ = pair(dqt2), pair(dkt2), pair(dkd2)
                debl = jnp.concatenate([jnp.sum(td['st'][p] * td['dsn'][p], axis=0, keepdims=True) for p in range(NP)], axis=1)
                for p in range(NP):
                    ds_ref[d, p] = two(dsq2, d, p, HV) + td['ebl'][:, p * PW:(p + 1) * PW] * td['dsn'][p]
                dkdkd = dkd_ * td['kd']
                dbl = jnp.sum(dkdkd, axis=0, keepdims=True) + debl * td['ebl']
                is_last = lax.broadcasted_iota(jnp.int32, (CH, DK), 0) == td['last']
                db = dqt_ * td['qt'] - dkt_ * td['kt'] - dkdkd + jnp.where(is_last, dbl, 0.0)
                dqk_ref[rows, :] = jnp.concatenate([dqt_ * td['eb'] * QSCALE, dkt_ * td['enb'] + dkd_ * td['ekd']], axis=1).astype(BF)
                dla_ref[rows, :] = _mask_dot(_gla_dir(1 - d)[0].astype(BF), db)
            return carry

        lax.fori_loop(0, nc, chunk, 0, unroll=4)

        @pl.when(pl.program_id(0) == nt - 1)
        def _():
            ds0_ref[...] = ds_ref[...]

    up = lambda w, col=0: pl.BlockSpec((tm, w), lambda i: (i, col))
    dn = lambda w, col=0: pl.BlockSpec((tm, w), lambda i: (nt - 1 - i, col))
    se_up = pl.BlockSpec((nc, NP, HV, PW), lambda i: (i, 0, 0, 0))
    se_dn = pl.BlockSpec((nc, NP, HV, PW), lambda i: (nt - 1 - i, 0, 0, 0))
    return _call(
        body, (nt,), "gla_bwd",
        [dn(2 * DK), dn(DV), dn(DK, 0), dn(DV), se_dn, up(2 * DK), up(DV), up(DK, 1), up(DV), se_up],
        [dn(2 * DK), dn(DV), dn(DK), up(2 * DK), up(DV), up(DK), _full((2, NP, HV, PW))],
        [jax.ShapeDtypeStruct((n, 2 * DK), BF), jax.ShapeDtypeStruct((n, DV), BF),
         jax.ShapeDtypeStruct((n, DK), F32), jax.ShapeDtypeStruct((n, 2 * DK), BF),
         jax.ShapeDtypeStruct((n, DV), BF), jax.ShapeDtypeStruct((n, DK), F32),
         jax.ShapeDtypeStruct((2, NP, HV, PW), F32)],
        [pltpu.VMEM((2, NP, HV, PW), F32)],
        (qk, v, la, do, se_f, qk, v, la, do, se_b), exchange, carried)


def _head_norm(o):
    ons, rss = [], []
    for h in range(NH):
        oh = o[:, h * HV:(h + 1) * HV]
        rs = lax.rsqrt(jnp.mean(oh * oh, axis=-1, keepdims=True) + EPS)
        ons.append(oh * rs)
        rss.append(rs)
    return ons, rss


def merge_fwd(x, o_f, o_b, g, co, vecm, gn, wout, tm):
    n = x.shape[0]

    def body(x_ref, of_ref, ob_ref, g_ref, co_ref, vec_ref, gn_ref, w_ref, x1_ref, y1_ref, cat_ref):
        o = of_ref[...].astype(F32) + ob_ref[...].astype(F32)
        ons, _ = _head_norm(o)
        gg = g_ref[...].astype(F32)
        sil = gg * _sigmoid(gg)
        cat_ref[:, :DC] = co_ref[...]
        for h in range(NH):
            vs = slice(h * HV, (h + 1) * HV)
            cat_ref[:, DC + h * HV:DC + (h + 1) * HV] = (ons[h] * gn_ref[:, vs] * sil[:, vs]).astype(BF)
        y1 = _dot(cat_ref[...], w_ref[...])
        y1_ref[...] = y1.astype(BF)
        x1_ref[...] = x_ref[...] + vec_ref[0:1, :] * y1

    tok = lambda w: pl.BlockSpec((tm, w), lambda i: (i, 0))
    return pl.pallas_call(
        body, grid=(n // tm,), name="merge_fwd",
        in_specs=[tok(D), tok(DV), tok(DV), tok(DV), tok(DC), _full(vecm.shape), _full(gn.shape), _full(wout.shape)],
        out_specs=[tok(D), tok(D), tok(D)],
        out_shape=[jax.ShapeDtypeStruct((n, D), F32), jax.ShapeDtypeStruct((n, D), BF), jax.ShapeDtypeStruct((n, D), BF)],
        compiler_params=_cparams(("arbitrary",)),
    )(x, o_f, o_b, g, co, vecm, gn, wout)


def merge_bwd(dx1, y1, o_f, o_b, g, vecm, gn, wout, tm):
    n = dx1.shape[0]

    def body(dx1_ref, y1_ref, of_ref, ob_ref, g_ref, vec_ref, gn_ref, w_ref,
             dy1_ref, dco_ref, do_ref, dg_ref, s1_ref, s2_ref):
        @pl.when(pl.program_id(0) == 0)
        def _():
            s1_ref[...] = jnp.zeros_like(s1_ref)
            s2_ref[...] = jnp.zeros_like(s2_ref)

        dx1 = dx1_ref[...]
        s1_ref[...] += _colsum8(dx1 * y1_ref[...].astype(F32))
        dy1 = (dx1 * vec_ref[0:1, :]).astype(BF)
        dy1_ref[...] = dy1
        dcat = _dot_nt(dy1, w_ref[...])
        dco_ref[...] = dcat[:, :DC].astype(BF)
        o = of_ref[...].astype(F32) + ob_ref[...].astype(F32)
        ons, rss = _head_norm(o)
        gg = g_ref[...].astype(F32)
        sg = _sigmoid(gg)
        sil = gg * sg
        dsil = sg * (1.0 + gg * (1.0 - sg))
        for h in range(NH):
            vs = slice(h * HV, (h + 1) * HV)
            do2 = dcat[:, DC + h * HV:DC + (h + 1) * HV]
            gnh = gn_ref[:, vs]
            t = do2 * sil[:, vs]
            s2_ref[:, vs] += _colsum8(t * ons[h])
            don = t * gnh
            do_ref[:, vs] = (rss[h] * (don - ons[h] * jnp.mean(don * ons[h], axis=-1, keepdims=True))).astype(BF)
            dg_ref[:, vs] = (do2 * ons[h] * gnh * dsil[:, vs]).astype(BF)

    tok = lambda w: pl.BlockSpec((tm, w), lambda i: (i, 0))
    return pl.pallas_call(
        body, grid=(n // tm,), name="merge_bwd",
        in_specs=[tok(D), tok(D), tok(DV), tok(DV), tok(DV), _full(vecm.shape), _full(gn.shape), _full(wout.shape)],
        out_specs=[tok(D), tok(DC), tok(DV), tok(DV), _full((8, D)), _full((8, DV))],
        out_shape=[jax.ShapeDtypeStruct((n, D), BF), jax.ShapeDtypeStruct((n, DC), BF), jax.ShapeDtypeStruct((n, DV), BF),
                   jax.ShapeDtypeStruct((n, DV), BF), jax.ShapeDtypeStruct((8, D), F32), jax.ShapeDtypeStruct((8, DV), F32)],
        compiler_params=_cparams(("arbitrary",)),
    )(dx1, y1, o_f, o_b, g, vecm, gn, wout)


def ffn_fwd_bwd(x1, tgt, vecf, wg, wu, wd, tm):
    n = x1.shape[0]

    def body(x1_ref, t_ref, vec_ref, wg_ref, wu_ref, wd_ref,
             dx1_ref, h2_ref, act_ref, dgt_ref, dup_ref, dy2_ref, s_ref):
        @pl.when(pl.program_id(0) == 0)
        def _():
            s_ref[...] = jnp.zeros_like(s_ref)

        n2g, sh2, sc2, g2, fg = (vec_ref[i:i + 1, :] for i in range(5))
        x1 = x1_ref[...]
        r2 = lax.rsqrt(jnp.mean(x1 * x1, axis=-1, keepdims=True) + EPS)
        xn2 = x1 * r2
        h2 = (xn2 * n2g * (1.0 + sc2) + sh2).astype(BF)
        h2_ref[...] = h2
        gt = _dot(h2, wg_ref[...])
        up = _dot(h2, wu_ref[...])
        sg = _sigmoid(gt)
        sil = gt * sg
        act = (sil * up).astype(BF)
        act_ref[...] = act
        y2 = _dot(act, wd_ref[...])
        x2 = x1 + g2 * y2
        r3 = lax.rsqrt(jnp.mean(x2 * x2, axis=-1, keepdims=True) + EPS)
        xn3 = x2 * r3
        e = xn3 * fg - t_ref[...]
        s_ref[40:48, :] += _colsum8(e * e) * (0.5 / D)
        dyo = e * (1.0 / D)
        s_ref[0:8, :] += _colsum8(dyo * xn3)
        dxn3 = dyo * fg
        dx2 = r3 * (dxn3 - xn3 * jnp.mean(dxn3 * xn3, axis=-1, keepdims=True))
        s_ref[8:16, :] += _colsum8(dx2 * y2)
        dy2 = (dx2 * g2).astype(BF)
        dy2_ref[...] = dy2
        dact = _dot_nt(dy2, wd_ref[...])
        dup = (dact * sil).astype(BF)
        dgt = (dact * up * (sg * (1.0 + gt * (1.0 - sg)))).astype(BF)
        dup_ref[...] = dup
        dgt_ref[...] = dgt
        dh2 = _dot_nt(dgt, wg_ref[...]) + _dot_nt(dup, wu_ref[...])
        s_ref[16:24, :] += _colsum8(dh2)
        t = dh2 * xn2
        s_ref[24:32, :] += _colsum8(t * n2g)
        s_ref[32:40, :] += _colsum8(t * (1.0 + sc2))
        dxn2 = dh2 * ((1.0 + sc2) * n2g)
        dx1_ref[...] = dx2 + r2 * (dxn2 - xn2 * jnp.mean(dxn2 * xn2, axis=-1, keepdims=True))

    tok = lambda w: pl.BlockSpec((tm, w), lambda i: (i, 0))
    wspec = lambda a: pl.BlockSpec(a.shape, lambda i: (0, 0), pipeline_mode=pl.Buffered(1))
    return pl.pallas_call(
        body, grid=(n // tm,), name="ffn_fwd_bwd",
        in_specs=[tok(D), tok(D), _full(vecf.shape), wspec(wg), wspec(wu), wspec(wd)],
        out_specs=[tok(D), tok(D), tok(DFF), tok(DFF), tok(DFF), tok(D), _full((48, D))],
        out_shape=[jax.ShapeDtypeStruct((n, D), F32), jax.ShapeDtypeStruct((n, D), BF), jax.ShapeDtypeStruct((n, DFF), BF),
                   jax.ShapeDtypeStruct((n, DFF), BF), jax.ShapeDtypeStruct((n, DFF), BF), jax.ShapeDtypeStruct((n, D), BF),
                   jax.ShapeDtypeStruct((48, D), F32)],
        compiler_params=_cparams(("arbitrary",)),
    )(x1, tgt, vecf, wg, wu, wd)


def wgrad(a, b, init, t1, t2, tn, name):
    n, k1 = a.shape
    k2 = b.shape[1]

    def body(a_ref, b_ref, *rest):
        o_ref = rest[-1]

        @pl.when(pl.program_id(2) == 0)
        def _():
            o_ref[...] = rest[0][...] if init is not None else jnp.zeros_like(o_ref)

        o_ref[...] += _dot_tn(a_ref[...], b_ref[...])

    ospec = pl.BlockSpec((t1, t2), lambda i, j, k: (i, j))
    extra = ([ospec], {2: 0}, (init,)) if init is not None else ([], {}, ())
    return pl.pallas_call(
        body, grid=(k1 // t1, k2 // t2, n // tn), name=name,
        in_specs=[pl.BlockSpec((tn, t1), lambda i, j, k: (k, i)), pl.BlockSpec((tn, t2), lambda i, j, k: (k, j))] + extra[0],
        out_specs=ospec, out_shape=jax.ShapeDtypeStruct((k1, k2), F32), input_output_aliases=extra[1],
        compiler_params=_cparams(("parallel", "parallel", "arbitrary")),
    )(a, b, *extra[2])


def bwd_in(x, dx1, ag, yb, dco, dqk_f, dqk_b, dv_f, dv_b, dg, dla_f, dla_b, la, r, vec1, win, convw, cvec, wa2, tm):
    n = x.shape[0]
    nseg = tm // GW
    cg = 128

    def body(x_ref, dx1_ref, ag_ref, yb_ref, dco_ref, dqkf_ref, dqkb_ref, dvf_ref, dvb_ref, dg_ref, dlaf_ref, dlab_ref,
             la_ref, r_ref, vec_ref, win_ref, cw_ref, cv_ref, wa2_ref,
             gx_ref, h_ref, dp_ref, dwa2_ref, dcw_ref, s_ref, vc_ref, pad2_ref, dvc_ref, dcw8_ref):
        first = pl.program_id(0) == 0

        @pl.when(first)
        def _():
            s_ref[...] = jnp.zeros_like(s_ref)
            dwa2_ref[...] = jnp.zeros_like(dwa2_ref)
            dcw8_ref[...] = jnp.zeros_like(dcw8_ref)

        yn, rs = _ln_stats(yb_ref[...])
        lng = cv_ref[1:2, :]
        ln = yn * lng + cv_ref[2:3, :]
        sgl = _sigmoid(ln)
        dln = dco_ref[...].astype(F32) * (sgl * (1.0 + ln * (1.0 - sgl)))
        dyn = dln * lng
        dyb = rs * (dyn - jnp.mean(dyn, axis=-1, keepdims=True) - yn * jnp.mean(dyn * yn, axis=-1, keepdims=True))
        s_ref[24:32, 0:DC] += _colsum8(dyb)
        s_ref[24:32, DC:D] += _colsum8(dln * yn)
        s_ref[32:40, 0:DC] += _colsum8(dln)

        agv = ag_ref[...].astype(F32)
        a = agv[:, :DC]
        sgg = _sigmoid(agv[:, DC:])
        vc_ref[...] = a * sgg
        _fill_padded(pad2_ref, dyb, nseg)

        dp_ref[:, 2 * DC:2 * DC + 2 * DK] = (dqkf_ref[...].astype(F32) + dqkb_ref[...].astype(F32)).astype(BF)
        dp_ref[:, 2 * DC + 2 * DK:2 * DC + 2 * DK + DV] = (dvf_ref[...].astype(F32) + dvb_ref[...].astype(F32)).astype(BF)
        dp_ref[:, 2 * DC + 2 * DK + DV:2 * DC + 2 * DK + 2 * DV] = dg_ref[...]

        la = la_ref[...]
        dla = jnp.concatenate([dlaf_ref[...], dlab_ref[...]], axis=1)
        dpre = dla * (1.0 - jnp.exp(TAU * la)) * (1.0 / TAU)
        s_ref[32:40, DC:D] += _colsum8(dpre)
        dpreb = dpre.astype(BF)
        dwa2_ref[...] += _dot_tn(r_ref[...], dpreb)
        dp_ref[:, DINP - 128:] = _dot_nt(dpreb, wa2_ref[...]).astype(BF)
        dh_rest = _dot_nt(dp_ref[:, 2 * DC:], win_ref[:, 2 * DC:])

        for s in range(nseg):
            rows = pl.ds(s * GW, GW)
            for c0 in range(0, DC, cg):
                cs = pl.ds(c0, cg)
                vcs = vc_ref[rows, cs]
                acc = jnp.zeros((GW, cg), F32)
                for j, rows_j in _tap_slabs(pad2_ref, s, cs):
                    acc = acc + cw_ref[pl.ds(CW - 1 - j, 1), cs] * rows_j
                    dcw8_ref[CW - 1 - j, :, cs] += _colsum8(vcs * rows_j)
                dvc_ref[rows, cs] = acc
        dvc = dvc_ref[...]
        dp_ref[:, 0:DC] = (dvc * sgg).astype(BF)
        dp_ref[:, DC:2 * DC] = (dvc * a * sgg * (1.0 - sgg)).astype(BF)

        dh = dh_rest + _dot_nt(dp_ref[:, :2 * DC], win_ref[:, :2 * DC])
        xx = x_ref[...]
        n1g, sh1, sc1 = vec_ref[0:1, :], vec_ref[1:2, :], vec_ref[2:3, :]
        rstd = lax.rsqrt(jnp.mean(xx * xx, axis=-1, keepdims=True) + EPS)
        xn = xx * rstd
        h_ref[...] = (xn * n1g * (1.0 + sc1) + sh1).astype(BF)
        s_ref[0:8, :] += _colsum8(dh)
        t = dh * xn
        s_ref[8:16, :] += _colsum8(t * n1g)
        s_ref[16:24, :] += _colsum8(t * (1.0 + sc1))
        dxn = dh * ((1.0 + sc1) * n1g)
        gx_ref[...] = dx1_ref[...] + rstd * (dxn - xn * jnp.mean(dxn * xn, axis=-1, keepdims=True))

        @pl.when(pl.program_id(0) == pl.num_programs(0) - 1)
        def _():
            dcw_ref[...] = jnp.sum(dcw8_ref[...], axis=1)

    tok = lambda w: pl.BlockSpec((tm, w), lambda i: (i, 0))
    return pl.pallas_call(
        body, grid=(n // tm,), name="bwd_in",
        in_specs=[tok(D), tok(D), tok(2 * DC), tok(DC), tok(DC), tok(2 * DK), tok(2 * DK), tok(DV), tok(DV), tok(DV),
                  tok(DK), tok(DK), tok(2 * DK), tok(128), _full(vec1.shape),
                  pl.BlockSpec(win.shape, lambda i: (0, 0), pipeline_mode=pl.Buffered(1)),
                  _full(convw.shape), _full(cvec.shape), _full(wa2.shape)],
        out_specs=[tok(D), tok(D), tok(DINP), _full((128, 2 * DK)), _full((32, DC)), _full((40, D))],
        out_shape=[jax.ShapeDtypeStruct((n, D), F32), jax.ShapeDtypeStruct((n, D), BF), jax.ShapeDtypeStruct((n, DINP), BF),
                   jax.ShapeDtypeStruct((128, 2 * DK), F32), jax.ShapeDtypeStruct((32, DC), F32),
                   jax.ShapeDtypeStruct((40, D), F32)],
        scratch_shapes=[pltpu.VMEM((tm, DC), F32), pltpu.VMEM((nseg, SEGP, DC), F32), pltpu.VMEM((tm, DC), F32),
                        pltpu.VMEM((32, 8, DC), F32)],
        compiler_params=_cparams(("arbitrary",)),
    )(x, dx1, ag, yb, dco, dqk_f, dqk_b, dv_f, dv_b, dg, dla_f, dla_b, la, r, vec1, win, convw, cvec, wa2)


def _ctx_common(ctx_ref, vec_ref, win_ref, wa2_ref, ba_ref):
    cx = ctx_ref[...]
    t = cx.shape[0]
    rstd = lax.rsqrt(jnp.mean(cx * cx, axis=-1, keepdims=True) + EPS)
    xn = cx * rstd
    hc = (xn * vec_ref[0:1, :] * (1.0 + vec_ref[2:3, :]) + vec_ref[1:2, :]).astype(BF)
    k0 = 2 * DC + DK
    kv = _dot(hc, win_ref[:, k0:k0 + DK + DV]).astype(BF).astype(F32)
    r = _dot(hc, win_ref[:, DINP - 128:]).astype(BF)
    la = _log_sigmoid(_dot(r, wa2_ref[...]) + ba_ref[...]) * (1.0 / TAU)
    incl = _tri(t, "le").astype(BF)
    strict = _tri(t, "lt").astype(BF)
    bf = _mask_dot(incl, la[:, :DK])
    wf = jnp.exp(bf[t - 1:t, :] - bf)
    wb = jnp.exp(_mask_dot(strict, la[:, DK:]))
    return xn, hc, kv[:, :DK], kv[:, DK:], r, la, wf, wb


def ctx_fwd(ctx, vecc, win, wa2, ba):
    def body(ctx_ref, vec_ref, win_ref, wa2_ref, ba_ref, s_ref):
        _, _, k, v, _, _, wf, wb = _ctx_common(ctx_ref, vec_ref, win_ref, wa2_ref, ba_ref)
        vb = v.astype(BF)
        for d, w in enumerate((wf, wb)):
            kd = (k * w).astype(BF)
            for h in range(NH):
                s_ref[d, h // 2, :, (h % 2) * HK:(h % 2 + 1) * HK] = _dot_tn(vb[:, h * HV:(h + 1) * HV], kd[:, h * HK:(h + 1) * HK])

    return pl.pallas_call(
        body, name="ctx_fwd", out_shape=jax.ShapeDtypeStruct((2, NP, HV, PW), F32),
        compiler_params=pltpu.CompilerParams(vmem_limit_bytes=VMEM_LIMIT),
    )(ctx, vecc, win, wa2, ba)


def ctx_bwd(ctx, vecc, win, wa2, ba, ds0):
    t = ctx.shape[0]

    def body(ctx_ref, vec_ref, win_ref, wa2_ref, ba_ref, ds_ref, dwin_ref, dwa2_ref, s_ref, dpc_ref):
        xn, hc, k, v, r, la, wf, wb = _ctx_common(ctx_ref, vec_ref, win_ref, wa2_ref, ba_ref)
        vb = v.astype(BF)
        strict = _tri(t, "lt").astype(BF)
        strict_t = _tri(t, "gt").astype(BF)
        dpc_ref[...] = jnp.zeros_like(dpc_ref)
        k0 = 2 * DC + DK
        dk = jnp.zeros((t, DK), F32)
        des = []
        for d, w in enumerate((wf, wb)):
            kd = (k * w).astype(BF)
            dkds = []
            for h in range(NH):
                dsb = ds_ref[d, h // 2, :, (h % 2) * HK:(h % 2 + 1) * HK].astype(BF)
                dkds.append(_dot(vb[:, h * HV:(h + 1) * HV], dsb))
                dvh = _dot_nt(kd[:, h * HK:(h + 1) * HK], dsb)
                vs = slice(k0 + DK + h * HV, k0 + DK + (h + 1) * HV)
                if d == 0:
                    dpc_ref[:, vs] = dvh.astype(BF)
                else:
                    dpc_ref[:, vs] = (dpc_ref[:, vs].astype(F32) + dvh).astype(BF)
            dkd = jnp.concatenate(dkds, axis=1)
            dk = dk + dkd * w
            des.append(dkd * k * w)
        dpc_ref[:, k0:k0 + DK] = dk.astype(BF)
        dla = jnp.concatenate([_mask_dot(strict, des[0]), _mask_dot(strict_t, des[1])], axis=1)
        dpre = dla * (1.0 - jnp.exp(TAU * la)) * (1.0 / TAU)
        dpreb = dpre.astype(BF)
        dwa2_ref[...] = _dot_tn(r, dpreb)
        dpc_ref[:, DINP - 128:] = _dot_nt(dpreb, wa2_ref[...]).astype(BF)
        dpc = dpc_ref[...]
        dwin_ref[...] = _dot_tn(hc, dpc)
        dhc = _dot_nt(dpc, win_ref[...])
        n1g, sc1 = vec_ref[0:1, :], vec_ref[2:3, :]
        tt = dhc * xn
        s_ref[...] = jnp.zeros_like(s_ref)
        s_ref[0:1, :] = jnp.sum(tt * (1.0 + sc1), axis=0, keepdims=True)
        s_ref[1:2, :] = jnp.sum(dhc, axis=0, keepdims=True)
        s_ref[2:3, :] = jnp.sum(tt * n1g, axis=0, keepdims=True)
        s_ref[3:4, DC:D] = jnp.sum(dpre, axis=0, keepdims=True)

    return pl.pallas_call(
        body, name="ctx_bwd",
        out_shape=[jax.ShapeDtypeStruct((D, DINP), F32), jax.ShapeDtypeStruct((128, 2 * DK), F32),
                   jax.ShapeDtypeStruct((8, D), F32)],
        scratch_shapes=[pltpu.VMEM((t, DINP), BF)],
        compiler_params=pltpu.CompilerParams(vmem_limit_bytes=VMEM_LIMIT),
    )(ctx, vecc, win, wa2, ba, ds0)


def _silu(x):
    return x * _sigmoid(x)


def mod_bwd(cext, dm, wm):
    def body(c_ref, d_ref, w_ref, gw_ref, ds_ref):
        dmb = d_ref[...].astype(BF)
        gw_ref[...] = _dot_tn(_silu(c_ref[...]).astype(BF), dmb)
        ds_ref[...] = _dot_nt(dmb, w_ref[...].astype(BF))

    return pl.pallas_call(body, name="mod_bwd",
                          out_shape=[jax.ShapeDtypeStruct(wm.shape, F32), jax.ShapeDtypeStruct(cext.shape, F32)],
                          compiler_params=pltpu.CompilerParams(vmem_limit_bytes=VMEM_LIMIT))(cext, dm, wm)


def pack_small(sf, s1, s2, sd, sc, dcw, dwa2, dwa2_c):
    def body(sf_ref, s1_ref, s2_ref, sd_ref, sc_ref, dcw_ref, dwa2_ref, dwa2c_ref, o_ref, ocw_ref, owa_ref):
        rsum = lambda ref, i: jnp.sum(ref[8 * i:8 * i + 8, :], axis=0, keepdims=True)
        o_ref[...] = jnp.zeros_like(o_ref)
        o_ref[0:1, :] = rsum(sd_ref, 0)
        o_ref[1:2, :] = rsum(sd_ref, 1)
        o_ref[2:3, :] = rsum(s1_ref, 0)
        o_ref[3:4, :] = rsum(sf_ref, 2)
        o_ref[4:5, :] = rsum(sf_ref, 3)
        o_ref[5:6, :] = rsum(sf_ref, 1)
        o_ref[6:7, :] = sc_ref[1:2, :]
        o_ref[7:8, :] = sc_ref[2:3, :]
        o_ref[8:9, :] = rsum(sd_ref, 2) + sc_ref[0:1, :]
        o_ref[9:10, :] = rsum(sf_ref, 4)
        o_ref[10:11, :] = rsum(sf_ref, 0)
        o_ref[11:12, :] = rsum(sd_ref, 3)
        o_ref[12:13, :] = rsum(sd_ref, 4) + sc_ref[3:4, :]
        g = jnp.sum(s2_ref[...], axis=0, keepdims=True)
        o_ref[13:14, 0:HV] = g[:, 0:HV] + g[:, HV:2 * HV] + g[:, 2 * HV:3 * HV] + g[:, 3 * HV:4 * HV]
        o_ref[14:15, :] = rsum(sf_ref, 5)
        ocw_ref[...] = dcw_ref[...]
        owa_ref[...] = dwa2_ref[0:32, :] + dwa2c_ref[0:32, :]

    return pl.pallas_call(body, name="pack_small",
                          out_shape=[jax.ShapeDtypeStruct((16, D), F32), jax.ShapeDtypeStruct((32, DC), F32),
                                     jax.ShapeDtypeStruct((32, 2 * DK), F32)])(sf, s1, s2, sd, sc, dcw, dwa2, dwa2_c)


def small_totals(g8):
    r = g8.shape[1]

    def body(g_ref, t_ref, bm_ref, loss_ref):
        acc = g_ref[0]
        for i in range(1, NDEV):
            acc = acc + g_ref[i]
        t_ref[...] = acc
        bm_ref[...] = jnp.zeros_like(bm_ref)
        bm_ref[0:6, :] = acc[0:6, :]
        bm_ref[0:2, :] += acc[6:8, :]
        loss_ref[...] = jnp.broadcast_to(jnp.sum(acc[14:15, :], axis=1, keepdims=True), loss_ref.shape)

    return pl.pallas_call(body, name="small_totals",
                          out_shape=[jax.ShapeDtypeStruct((r, D), F32), jax.ShapeDtypeStruct((8, D), F32),
                                     jax.ShapeDtypeStruct((8, 128), F32)])(g8)


def cctx_grad(p8, c_ctx_row):
    def body(p_ref, c_ref, o_ref):
        acc = p_ref[0, 0:1, :]
        for j in range(1, NCHIP):
            acc = acc + p_ref[2 * j, 0:1, :]
        cc = c_ref[0:1, :]
        sg = _sigmoid(cc)
        o_ref[...] = jnp.zeros_like(o_ref)
        o_ref[0:1, :] = acc * (sg * (1.0 + cc * (1.0 - sg)))

    return pl.pallas_call(body, name="cctx_grad", out_shape=jax.ShapeDtypeStruct((8, D), F32))(p8, c_ctx_row)


def adamw(w, g, m, v, rows, name, emit_grad=False):
    r, c = w.shape

    def body(w_ref, g_ref, m_ref, v_ref, d_ref, nm_ref, nv_ref, *go_ref):
        gg = g_ref[...]
        nm = ADAM_B1 * m_ref[...] + (1.0 - ADAM_B1) * gg
        nv = ADAM_B2 * v_ref[...] + (1.0 - ADAM_B2) * (gg * gg)
        m_hat = nm / (1.0 - ADAM_B1 ** ADAM_STEP)
        v_hat = nv / (1.0 - ADAM_B2 ** ADAM_STEP)
        d_ref[...] = -ADAM_LR * (m_hat / (jnp.sqrt(v_hat) + ADAM_EPS) + ADAM_WD * w_ref[...])
        nm_ref[...] = nm
        nv_ref[...] = nv
        if emit_grad:
            go_ref[0][...] = gg

    spec = pl.BlockSpec((rows, c), lambda i: (i, 0))
    sds = jax.ShapeDtypeStruct((r, c), F32)
    nout = 4 if emit_grad else 3
    return pl.pallas_call(
        body, grid=(r // rows,), name=name, in_specs=[spec] * 4, out_specs=[spec] * nout, out_shape=[sds] * nout,
        compiler_params=_cparams(("parallel",)),
    )(w, g, m, v)


def _me():
    return lax.axis_index("x"), lax.axis_index("y"), lax.axis_index("c")


def _flip(v, bit):
    return 1 - v if bit else v


ANY = pl.BlockSpec(memory_space=pl.ANY)


def _gather8(x_ref, o_ref, ssem, rsem, lsem):
    mx, my, mc = _me()
    me = 4 * mx + 2 * my + mc
    local = pltpu.make_async_copy(x_ref, o_ref.at[me], lsem)
    local.start()
    peer = lambda k: (_flip(mx, k & 4), _flip(my, k & 2), _flip(mc, k & 1))
    sends = []
    for k in range(1, NDEV):
        cp = pltpu.make_async_remote_copy(src_ref=x_ref, dst_ref=o_ref.at[me], send_sem=ssem.at[k - 1],
                                          recv_sem=rsem.at[k - 1], device_id=peer(k), device_id_type=MESH)
        cp.start()
        sends.append(cp)
    for k in range(1, NDEV):
        px, py, pc = peer(k)
        pltpu.make_async_remote_copy(src_ref=x_ref, dst_ref=o_ref.at[4 * px + 2 * py + pc], send_sem=ssem.at[k - 1],
                                     recv_sem=rsem.at[k - 1], device_id=(px, py, pc), device_id_type=MESH).wait_recv()
    for cp in sends:
        cp.wait_send()
    local.wait()


def _gather8_sems():
    return [pltpu.SemaphoreType.DMA((NDEV - 1,)), pltpu.SemaphoreType.DMA((NDEV - 1,)), pltpu.SemaphoreType.DMA]


def all_gather8(x, name):
    vm = pl.BlockSpec(memory_space=pltpu.VMEM)
    return pl.pallas_call(_gather8_body(), name=name, in_specs=[vm], out_specs=vm,
                          out_shape=jax.ShapeDtypeStruct((NDEV,) + x.shape, x.dtype), scratch_shapes=_gather8_sems())(x)


def _gather8_body():
    def body(x_ref, o_ref, ssem, rsem, lsem):
        _gather8(x_ref, o_ref, ssem, rsem, lsem)
    return body


def prologue(small, c_ctx_rows, wm, bm, w_in_shard):
    ex = ChipExchange("gather", [w_in_shard])

    def body(s_ref, cc_ref, w_ref, b_ref, win_ref, s8_ref, m8_ref, wing_ref, mloc_ref, *sems):
        ex.start([win_ref], [wing_ref], sems[6:])
        _gather8(s_ref, s8_ref, *sems[0:3])
        cext = jnp.concatenate([s8_ref[:, 0, :], cc_ref[...]], axis=0)
        mloc_ref[...] = _dot(_silu(cext).astype(BF), w_ref[...].astype(BF)) + b_ref[...]
        _gather8(mloc_ref, m8_ref, *sems[3:6])
        ex.wait([win_ref], [wing_ref], sems[6:])

    vm = pl.BlockSpec(memory_space=pltpu.VMEM)
    wcols = wm.shape[1]
    return pl.pallas_call(
        body, name="prologue", in_specs=[vm, vm, vm, vm, ANY], out_specs=[vm, vm, ANY],
        out_shape=[jax.ShapeDtypeStruct((NDEV, 16, D), F32), jax.ShapeDtypeStruct((NDEV, 16, wcols), F32)] + ex.out_shape,
        scratch_shapes=[pltpu.VMEM((16, wcols), F32)] + _gather8_sems() + _gather8_sems() + ex.scratch,
        compiler_params=pltpu.CompilerParams(vmem_limit_bytes=VMEM_LIMIT),
    )(small, c_ctx_rows, wm, bm, w_in_shard)


def _chip_peers(mx, my):
    out = []
    for p in range(1, NCHIP):
        px, py = _flip(mx, p & 2), _flip(my, p & 1)
        out.append((px, py, 2 * px + py))
    return out


class ChipExchange:
    def __init__(self, kind, arrays):
        self.kind = kind
        self.n = len(arrays)
        if kind == "gather":
            self.out_shape = [jax.ShapeDtypeStruct((NCHIP,) + a.shape, a.dtype) for a in arrays]
        else:
            self.out_shape = [jax.ShapeDtypeStruct(a.shape, a.dtype) for a in arrays]
        self.scratch = [pltpu.SemaphoreType.DMA((3 * self.n,)), pltpu.SemaphoreType.DMA((3 * self.n,)),
                        pltpu.SemaphoreType.DMA((self.n,))]

    def _copies(self, ins, outs, sems):
        ssem, rsem, lsem = sems
        mx, my, mc = _me()
        jme = 2 * mx + my
        gather = self.kind == "gather"
        local, sends, waits = [], [], []
        for k in range(self.n):
            local.append(pltpu.make_async_copy(ins[k] if gather else ins[k].at[jme], outs[k].at[jme], lsem.at[k]))
            for p, (px, py, jp) in enumerate(_chip_peers(mx, my)):
                src = ins[k] if gather else ins[k].at[jp]
                sem = dict(send_sem=ssem.at[3 * k + p], recv_sem=rsem.at[3 * k + p], device_id=(px, py, mc),
                           device_id_type=MESH)
                sends.append(pltpu.make_async_remote_copy(src_ref=src, dst_ref=outs[k].at[jme], **sem))
                waits.append(pltpu.make_async_remote_copy(src_ref=src, dst_ref=outs[k].at[jp], **sem))
        return local, sends, waits

    def start(self, ins, outs, sems):
        local, sends, _ = self._copies(ins, outs, sems)
        for cp in local + sends:
            cp.start()

    def wait(self, ins, outs, sems):
        local, _, waits = self._copies(ins, outs, sems)
        for cp in waits:
            cp.wait_recv()
        for cp in waits:
            cp.wait_send()
        for cp in local:
            cp.wait()


def chip_exchange(kind, arrays, name):
    ex = ChipExchange(kind, arrays)
    n = ex.n

    def body(*refs):
        ins, outs, sems = refs[:n], refs[n:2 * n], refs[2 * n:]
        ex.start(ins, outs, sems)
        ex.wait(ins, outs, sems)

    return pl.pallas_call(body, name=name, in_specs=[ANY] * n, out_specs=[ANY] * n, out_shape=ex.out_shape,
                          scratch_shapes=ex.scratch)(*arrays)


def sibling_add(g, ngrp, hr, tr, name):
    c_ = g.shape[1]
    nt = hr // tr

    def body(cidx, keep_ref, give_ref, o_ref, land, ssem, rsem):
        mx, my, mc = _me()
        t = pl.program_id(0) * nt + pl.program_id(1)
        s = t % 2
        cp = pltpu.make_async_remote_copy(src_ref=give_ref, dst_ref=land.at[s], send_sem=ssem.at[s], recv_sem=rsem.at[s],
                                          device_id=(mx, my, 1 - mc), device_id_type=MESH)
        cp.start()
        cp.wait_recv()
        o_ref[...] = keep_ref[...] + land[s]
        cp.wait_send()

    grid_spec = pltpu.PrefetchScalarGridSpec(
        num_scalar_prefetch=1, grid=(ngrp, nt),
        in_specs=[pl.BlockSpec((tr, c_), lambda i, j, cr: ((2 * i + cr[0]) * nt + j, 0)),
                  pl.BlockSpec((tr, c_), lambda i, j, cr: ((2 * i + 1 - cr[0]) * nt + j, 0))],
        out_specs=pl.BlockSpec((tr, c_), lambda i, j, cr: (i * nt + j, 0)),
        scratch_shapes=[pltpu.VMEM((2, tr, c_), F32), pltpu.SemaphoreType.DMA((2,)), pltpu.SemaphoreType.DMA((2,))])
    cidx = lax.axis_index("c").astype(jnp.int32).reshape(1)
    return pl.pallas_call(body, grid_spec=grid_spec, name=name, out_shape=jax.ShapeDtypeStruct((ngrp * hr, c_), F32),
                          compiler_params=_cparams(("arbitrary", "arbitrary")))(cidx, g, g)


def finish_grad(b, tr, name):
    _, r2, c_ = b.shape

    def body(b_ref, g_ref, mine, land, ssem, rsem):
        mx, my, mc = _me()
        t = pl.program_id(0)
        s = t % 2
        mine[s] = (b_ref[0].astype(F32) + b_ref[1].astype(F32)) + (b_ref[2].astype(F32) + b_ref[3].astype(F32))
        cp = pltpu.make_async_remote_copy(src_ref=mine.at[s], dst_ref=land.at[s], send_sem=ssem.at[s], recv_sem=rsem.at[s],
                                          device_id=(mx, my, 1 - mc), device_id_type=MESH)
        cp.start()
        cp.wait_recv()
        g_ref[mc] = mine[s]
        g_ref[1 - mc] = land[s]
        cp.wait_send()

    return pl.pallas_call(
        body, grid=(r2 // tr,), name=name,
        in_specs=[pl.BlockSpec((NCHIP, tr, c_), lambda i: (0, i, 0))],
        out_specs=pl.BlockSpec((2, tr, c_), lambda i: (0, i, 0)), out_shape=jax.ShapeDtypeStruct((2, r2, c_), F32),
        scratch_shapes=[pltpu.VMEM((2, tr, c_), F32), pltpu.VMEM((2, tr, c_), F32), pltpu.SemaphoreType.DMA((2,)),
                        pltpu.SemaphoreType.DMA((2,))],
        compiler_params=_cparams(("arbitrary",)))(b)


TM_IN = 512
TM_GLA = 512
TM_MERGE = 1024
TM_FFN = 256
TN_WGRAD = 2048

WEIGHTS = ['c_ctx', 'w_mod', 'b_mod', 'norm1_g', 'norm2_g', 'w_in', 'conv_w', 'conv_b', 'conv_ln_g', 'conv_ln_b', 'w_a2_f',
           'b_a_f', 'w_a2_b', 'b_a_b', 'gla_norm_g', 'w_out', 'w_gate', 'w_up', 'w_down', 'final_g']
BIG = ['w_in', 'w_out', 'w_gate', 'w_up', 'w_down']


def _rows(*vs):
    w = vs[0].size
    row = lax.broadcasted_iota(jnp.int32, (8, w), 0)
    out = jnp.zeros((8, w), F32)
    for i, v in enumerate(vs):
        out = jnp.where(row == i, v.reshape(1, w), out)
    return out


def _small_slab(p):
    cat = lambda *ks: jnp.concatenate([p[k].reshape(-1) for k in ks])
    vecs = _rows(p['c_ctx'], p['norm1_g'], p['norm2_g'], p['final_g'], cat('conv_b', 'conv_ln_g'),
                 cat('conv_ln_b', 'b_a_f', 'b_a_b'), jnp.pad(p['gla_norm_g'].reshape(-1), (0, D - HV)))
    bmod = jnp.pad(p['b_mod'].reshape(6, D), ((0, 2), (0, 0)))
    shards = jnp.pad(jnp.concatenate([jnp.pad(p['conv_w'].reshape(-1), (0, DC // NCHIP)), cat('w_a2_f', 'w_a2_b')]),
                     (0, 2 * D)).reshape(8, D)
    return jnp.concatenate([vecs, bmod, shards], axis=0)


def _unslab(s):
    return {
        'c_ctx': s[0], 'norm1_g': s[1:2], 'norm2_g': s[2:3], 'final_g': s[3],
        'conv_b': s[4:5, :DC], 'conv_ln_g': s[4:5, DC:], 'conv_ln_b': s[5:6, :DC],
        'b_a_f': s[5:6, DC:DC + DK], 'b_a_b': s[5:6, DC + DK:], 'gla_norm_g': s[6:7, :HV],
        'b_mod': s[8:14].reshape(1, 6 * D),
        'conv_w': s[16:20].reshape(32, DC // NCHIP)[:CW].reshape(1, CW, DC // NCHIP),
        'w_a2_f': s[20].reshape(1, RANK, DK // NCHIP), 'w_a2_b': s[21].reshape(1, RANK, DK // NCHIP),
    }


def kernel(x, c, ctx, c_ctx, w_mod, b_mod, norm1_g, norm2_g, w_in, conv_w, conv_b, conv_ln_g, conv_ln_b, w_a2_f, b_a_f, w_a2_b, b_a_b, gla_norm_g, w_out, w_gate, w_up, w_down, final_g, loss_target, m_c_ctx, m_w_mod, m_b_mod, m_norm1_g, m_norm2_g, m_w_in, m_conv_w, m_conv_b, m_conv_ln_g, m_conv_ln_b, m_w_a2_f, m_b_a_f, m_w_a2_b, m_b_a_b, m_gla_norm_g, m_w_out, m_w_gate, m_w_up, m_w_down, m_final_g, v_c_ctx, v_w_mod, v_b_mod, v_norm1_g, v_norm2_g, v_w_in, v_conv_w, v_conv_b, v_conv_ln_g, v_conv_ln_b, v_w_a2_f, v_b_a_f, v_w_a2_b, v_b_a_b, v_gla_norm_g, v_w_out, v_w_gate, v_w_up, v_w_down, v_final_g):
    w = dict(c_ctx=c_ctx, w_mod=w_mod, b_mod=b_mod, norm1_g=norm1_g, norm2_g=norm2_g, w_in=w_in, conv_w=conv_w, conv_b=conv_b,
             conv_ln_g=conv_ln_g, conv_ln_b=conv_ln_b, w_a2_f=w_a2_f, b_a_f=b_a_f, w_a2_b=w_a2_b, b_a_b=b_a_b,
             gla_norm_g=gla_norm_g, w_out=w_out, w_gate=w_gate, w_up=w_up, w_down=w_down, final_g=final_g)
    m = dict(c_ctx=m_c_ctx, w_mod=m_w_mod, b_mod=m_b_mod, norm1_g=m_norm1_g, norm2_g=m_norm2_g, w_in=m_w_in, conv_w=m_conv_w,
             conv_b=m_conv_b, conv_ln_g=m_conv_ln_g, conv_ln_b=m_conv_ln_b, w_a2_f=m_w_a2_f, b_a_f=m_b_a_f, w_a2_b=m_w_a2_b,
             b_a_b=m_b_a_b, gla_norm_g=m_gla_norm_g, w_out=m_w_out, w_gate=m_w_gate, w_up=m_w_up, w_down=m_w_down,
             final_g=m_final_g)
    v = dict(c_ctx=v_c_ctx, w_mod=v_w_mod, b_mod=v_b_mod, norm1_g=v_norm1_g, norm2_g=v_norm2_g, w_in=v_w_in, conv_w=v_conv_w,
             conv_b=v_conv_b, conv_ln_g=v_conv_ln_g, conv_ln_b=v_conv_ln_b, w_a2_f=v_w_a2_f, b_a_f=v_b_a_f, w_a2_b=v_w_a2_b,
             b_a_b=v_b_a_b, gla_norm_g=v_gla_norm_g, w_out=v_w_out, w_gate=v_w_gate, w_up=v_w_up, w_down=v_w_down,
             final_g=v_final_g)
    mx, my, mc = _me()
    jme = 2 * mx + my
    me = 4 * mx + 2 * my + mc
    wmc = D * 6 // NCHIP
    xx, tgt, cx = x[0], loss_target[0], ctx[0]

    bshard = [w[k][0].astype(BF) for k in BIG]
    sw = jnp.concatenate([jnp.pad(conv_w[0], ((0, 1), (0, 0))).reshape(-1), w_a2_f[0].reshape(-1), w_a2_b[0].reshape(-1)])
    small = jnp.concatenate([_rows(c[0]), jnp.pad(sw.reshape(6, D), ((0, 2), (0, 0)))], axis=0)
    cs8, mall, win_g = prologue(small, _rows(c_ctx), w_mod[0], lax.dynamic_slice_in_dim(b_mod, jme * wmc, wmc, axis=1),
                                bshard[0])
    cext = jnp.concatenate([cs8[:, 0, :], _rows(c_ctx)], axis=0)
    swc = jnp.stack([cs8[2 * j, 8:16] for j in range(NCHIP)]).reshape(NCHIP, 8 * D)
    convw = jnp.transpose(swc[:, :32 * 128].reshape(NCHIP, 32, 128), (1, 0, 2)).reshape(32, DC)
    a2 = lambda o: jnp.transpose(swc[:, o:o + RANK * 64].reshape(NCHIP, RANK, 64), (1, 0, 2)).reshape(RANK, DK)
    wa2 = jnp.zeros((128, 2 * DK), F32).at[0:RANK, 0:DK].set(a2(32 * 128)).at[RANK:2 * RANK, DK:].set(a2(32 * 128 + RANK * 64))
    wa2 = wa2.astype(BF)
    mall = jnp.concatenate([mall[2 * j] for j in range(NCHIP)], axis=1)
    sh1, sc1, g1, sh2, sc2, g2 = jnp.split(lax.dynamic_slice_in_dim(mall, me, 1, axis=0)[0], 6)
    csh1, csc1 = mall[8, :D], mall[8, D:2 * D]
    cols = lambda a: jnp.transpose(a, (1, 0, 2)).reshape(a.shape[1], -1)
    win = jnp.pad(cols(win_g), ((0, 0), (0, DINP - DIN)))
    ba = jnp.concatenate([b_a_f, b_a_b], axis=1)
    cvec = _rows(conv_b, conv_ln_g, conv_ln_b)
    vec1 = _rows(norm1_g, sh1, sc1)
    vecc = _rows(norm1_g, csh1, csc1)
    vecm = _rows(g1)
    vecf = _rows(norm2_g, sh2, sc2, g2, final_g)
    gn = jnp.tile(gla_norm_g, (1, NH))

    s0 = ctx_fwd(cx, vecc, win, wa2, ba)
    res = fwd_in(xx, vec1, win, convw, cvec, wa2, ba, TM_IN, ChipExchange("gather", bshard[1:]), bshard[1:])
    ag, yb, co, qk, vv, gg, la, r = res[:8]
    wout = res[8].reshape(D, D)
    wg, wu = cols(res[9]), cols(res[10])
    wd = res[11].reshape(DFF, D)
    o_f, o_b, se_f, se_b = gla_fwd(qk, vv, la, s0, TM_GLA)
    x1, y1, cat = merge_fwd(xx, o_f, o_b, gg, co, vecm, gn, wout, TM_MERGE)

    dx1, h2, act, dgt, dup, dy2, sf = ffn_fwd_bwd(x1, tgt, vecf, wg, wu, wd, TM_FFN)
    d_wg = wgrad(h2, dgt, None, D // 2, DFF, TN_WGRAD, "wgrad_gate")
    d_wu = wgrad(h2, dup, None, D // 2, DFF, TN_WGRAD, "wgrad_up")
    d_wd = wgrad(act, dy2, None, DFF // 2, D, TN_WGRAD, "wgrad_down")
    dy1, dco, do, dg, s1, s2 = merge_bwd(dx1, y1, o_f, o_b, gg, vecm, gn, wout, TM_MERGE)
    d_wout = wgrad(cat, dy1, None, D, D, TN_WGRAD, "wgrad_out")

    shard = lambda a, k: jnp.transpose(a.reshape(a.shape[0], NCHIP, k), (1, 0, 2))
    hd = D // 2
    parts = [sibling_add(d_wout, NCHIP, hd // NCHIP, hd // NCHIP, "xadd_w_out").reshape(NCHIP, hd // NCHIP, D),
             shard(sibling_add(d_wg, 1, hd, hd // 2, "xadd_w_gate"), DFF // NCHIP),
             shard(sibling_add(d_wu, 1, hd, hd // 2, "xadd_w_up"), DFF // NCHIP),
             sibling_add(d_wd, NCHIP, DFF // 8, DFF // 8, "xadd_w_down").reshape(NCHIP, DFF // 8, D)]
    parts = [p.astype(BF) for p in parts]
    res = gla_bwd(qk, vv, la, do, se_f, se_b, TM_GLA, ChipExchange("scatter", parts), parts)
    dqk_f, dv_f, dla_f, dqk_b, dv_b, dla_b, ds0 = res[:7]
    recv = list(res[7:])
    dwin_c, dwa2_c, sc = ctx_bwd(cx, vecc, win, wa2, ba, ds0)
    grad_x, h, dp, dwa2, dcw, sd = bwd_in(xx, dx1, ag, yb, dco, dqk_f, dqk_b, dv_f, dv_b, dg, dla_f, dla_b, la, r,
                                          vec1, win, convw, cvec, wa2, TM_IN)
    d_win = wgrad(h, dp, dwin_c, D // 2, DINP, TN_WGRAD, "wgrad_in")
    part_in = shard(sibling_add(d_win, 1, hd, hd // 2, "xadd_w_in")[:, :DIN], DIN // NCHIP).astype(BF)
    recv = list(chip_exchange("scatter", [part_in], "scatter_w_in")) + recv

    rows16, dcw_t, dwa2_t = pack_small(sf, s1, s2, sd, sc, dcw, dwa2, dwa2_c)
    sp = jnp.concatenate([rows16, dcw_t.reshape(16, D), dwa2_t.reshape(16, D)], axis=0)
    g8 = all_gather8(sp, "gather_small_grads")
    tot, bm_g, loss8 = small_totals(g8)
    loss = loss8[0, 0]
    dmod8 = g8[:, 0:6, :].reshape(NDEV, 6 * D)
    dmodc = jnp.concatenate([tot[6], tot[7], jnp.zeros((4 * D,), F32)])
    dm = jnp.concatenate([dmod8, _rows(dmodc)], axis=0)
    dm = lax.dynamic_slice_in_dim(dm, jme * wmc, wmc, axis=1)
    g_wmod, dsil = mod_bwd(cext, dm, w_mod[0])
    p8 = all_gather8(dsil[8:16], "gather_dsilu")
    g_cctx = cctx_grad(p8, _rows(c_ctx))[0]

    grads, delta, new_m, new_v = {}, {}, {}, {}
    for i, k in enumerate(BIG):
        r2 = recv[i].shape[1]
        gk = finish_grad(recv[i], r2 // 2 if r2 >= 512 else r2, "finish_" + k).reshape(w[k].shape[1:])
        rk = gk.shape[0]
        outs = adamw(w[k][0], gk, m[k][0], v[k][0], rk // 2 if rk >= 512 else rk, "adamw_" + k, emit_grad=True)
        delta[k], new_m[k], new_v[k], grads[k] = (o[None] for o in outs)
    grads['w_mod'] = g_wmod[None]
    d_, m_, v_ = adamw(w_mod[0], g_wmod, m_w_mod[0], v_w_mod[0], 256, "adamw_w_mod")
    delta['w_mod'], new_m['w_mod'], new_v['w_mod'] = d_[None], m_[None], v_[None]
    small_g = {
        'c_ctx': g_cctx, 'b_mod': bm_g[0:6].reshape(1, 6 * D), 'norm1_g': tot[8:9], 'norm2_g': tot[9:10], 'final_g': tot[10],
        'conv_b': tot[11:12, :DC], 'conv_ln_g': tot[11:12, DC:], 'conv_ln_b': tot[12:13, :DC],
        'b_a_f': tot[12:13, DC:DC + DK], 'b_a_b': tot[12:13, DC + DK:], 'gla_norm_g': tot[13:14, :HV],
        'conv_w': lax.dynamic_slice_in_dim(tot[16:32].reshape(32, DC)[:CW], jme * (DC // NCHIP), DC // NCHIP, axis=1)[None],
        'w_a2_f': lax.dynamic_slice_in_dim(tot[32:48].reshape(32, 2 * DK)[0:RANK, 0:DK], jme * (DK // NCHIP), DK // NCHIP, axis=1)[None],
        'w_a2_b': lax.dynamic_slice_in_dim(tot[32:48].reshape(32, 2 * DK)[RANK:2 * RANK, DK:], jme * (DK // NCHIP), DK // NCHIP, axis=1)[None],
    }
    grads.update(small_g)
    sd_, sm_, sv_ = adamw(_small_slab(w), _small_slab(small_g), _small_slab(m), _small_slab(v), 24,
                          "adamw_small")
    for dst, slab in ((delta, sd_), (new_m, sm_), (new_v, sv_)):
        dst.update(_unslab(slab))
    out = [loss, grad_x[None]]
    for group in (grads, delta, new_m, new_v):
        out += [group[k].reshape(w[k].shape) for k in WEIGHTS]
    return tuple(out)
```
